```python
import jax, jax.numpy as jnp
from jax import lax
import numpy as np

D_MODEL = 4096
BATCH = 8
SEQ = 4096
DEPTH = 1

N_BRANCHES = 2
POOL_WIDTH = D_MODEL // 2
CONV_WIDTH = D_MODEL // 2
POOL_WINDOWS = (2, 4, 8, 16)
N_POOL_GROUPS = len(POOL_WINDOWS)
POOL_GROUP_DIM = POOL_WIDTH // N_POOL_GROUPS
CONV_K = 3
NORM_EPS = 1e-6
SPLIT_SIZES = (POOL_WIDTH, POOL_WIDTH, CONV_WIDTH, CONV_WIDTH, CONV_WIDTH, CONV_WIDTH,
               N_BRANCHES * D_MODEL)
PROJ_WIDTH = sum(SPLIT_SIZES)
SPLIT_POINTS = tuple(int(v) for v in np.cumsum(SPLIT_SIZES)[:-1])

kernel_name = "hybrid_pool_shortconv_gated_merge"


def rmsnorm(x, w):
    xf = x.astype(jnp.float32)
    y = xf * lax.rsqrt(jnp.mean(xf * xf, axis=-1, keepdims=True) + NORM_EPS)
    return (y * w.astype(jnp.float32)).astype(x.dtype)


def causal_multiscale_pool(u, pool_w, pool_scale):
    b, s, _ = u.shape
    ug = u.reshape(b, s, N_POOL_GROUPS, POOL_GROUP_DIM).astype(jnp.float32)
    cs = lax.cumsum(ug, axis=1)
    cs_pad = jnp.concatenate([jnp.zeros((b, 1, N_POOL_GROUPS, POOL_GROUP_DIM), jnp.float32), cs], axis=1)
    t1 = jnp.arange(1, s + 1, dtype=jnp.float32)
    pooled = []
    for g, w in enumerate(POOL_WINDOWS):
        upper = cs_pad[:, 1:, g]
        lower = jnp.concatenate([jnp.zeros((b, w - 1, POOL_GROUP_DIM), jnp.float32),
                                 cs_pad[:, : s + 1 - w, g]], axis=1)
        count = jnp.minimum(t1, jnp.float32(w))[None, :, None]
        pooled.append((upper - lower) / count - ug[:, :, g])
    pooled = jnp.stack(pooled, axis=2).astype(u.dtype)
    mixed = jnp.einsum('bsgc,gcd->bsgd', pooled, pool_w)
    return mixed.reshape(b, s, POOL_WIDTH) * pool_scale


def causal_gated_shortconv(u, b_gate, c_gate, conv_w, conv_b):
    s = u.shape[1]
    v = c_gate * u
    vpad = jnp.pad(v, ((0, 0), (CONV_K - 1, 0), (0, 0)))
    y = conv_b + sum(conv_w[k] * vpad[:, k:k + s] for k in range(CONV_K))
    return b_gate * y


def _fwd_setup_inputs(seed: int = 0) -> dict:
    key = jax.random.key(seed)
    ks = jax.random.split(key, 11)
    d = D_MODEL
    x = jax.random.normal(ks[0], (BATCH, SEQ, d), jnp.float32)
    norm_w = 1.0 + 0.02 * jax.random.normal(ks[1], (DEPTH, d), jnp.float32)
    w_in = jax.random.normal(ks[2], (DEPTH, d, PROJ_WIDTH), jnp.float32) * d ** -0.5
    pool_w = jax.random.normal(ks[3], (DEPTH, N_POOL_GROUPS, POOL_GROUP_DIM, POOL_GROUP_DIM), jnp.float32) * POOL_GROUP_DIM ** -0.5
    pool_scale = 1.0 + 0.1 * jax.random.normal(ks[4], (DEPTH, POOL_WIDTH), jnp.float32)
    conv_w = jax.random.normal(ks[5], (DEPTH, CONV_K, CONV_WIDTH), jnp.float32) * CONV_K ** -0.5
    conv_b = 0.02 * jax.random.normal(ks[6], (DEPTH, CONV_WIDTH), jnp.float32)
    gate_b = 0.02 * jax.random.normal(ks[7], (DEPTH, N_BRANCHES, d), jnp.float32)
    w_branch = jax.random.normal(ks[8], (DEPTH, N_BRANCHES, POOL_WIDTH, d), jnp.float32) * POOL_WIDTH ** -0.5
    w_out = jax.random.normal(ks[9], (DEPTH, d, d), jnp.float32) * d ** -0.5
    final_norm_w = 1.0 + 0.02 * jax.random.normal(ks[10], (d,), jnp.float32)
    return {"x": x, "norm_w": norm_w, "w_in": w_in, "pool_w": pool_w,
            "pool_scale": pool_scale, "conv_w": conv_w, "conv_b": conv_b,
            "gate_b": gate_b, "w_branch": w_branch, "w_out": w_out,
            "final_norm_w": final_norm_w}


def _fwd_reference(x, norm_w, w_in, pool_w, pool_scale, conv_w, conv_b, gate_b, w_branch, w_out, final_norm_w):
    b, s, d = x.shape
    for l in range(DEPTH):
        h = rmsnorm(x, norm_w[l])
        proj = jnp.einsum('bsd,de->bse', h, w_in[l])
        u_p, z_p, u_c, b_c, c_c, z_c, g_logit = jnp.split(proj, SPLIT_POINTS, axis=-1)
        y_pool = causal_multiscale_pool(u_p, pool_w[l], pool_scale[l]) * jax.nn.silu(z_p)
        y_conv = causal_gated_shortconv(u_c, b_c, c_c, conv_w[l], conv_b[l]) * jax.nn.silu(z_c)
        ys = jnp.stack([y_pool, y_conv], axis=2)
        br = jnp.einsum('bsnc,ncd->bsnd', ys, w_branch[l])
        gates = jax.nn.sigmoid(g_logit.reshape(b, s, N_BRANCHES, d) + gate_b[l])
        merged = jnp.sum(gates * br, axis=2)
        x = x + jnp.einsum('bsd,de->bse', merged, w_out[l])
    return rmsnorm(x, final_norm_w)


import jax as _jax
import jax.numpy as _jnp

TWIN_FORMAT = 'train_step'
FWD_PARAMS = ['x', 'norm_w', 'w_in', 'pool_w', 'pool_scale', 'conv_w', 'conv_b', 'gate_b', 'w_branch', 'w_out', 'final_norm_w']
TWIN_WEIGHTS = ['norm_w', 'w_in', 'pool_w', 'pool_scale', 'conv_w', 'conv_b', 'gate_b', 'w_branch', 'w_out', 'final_norm_w']
TWIN_DIFF_INPUT = 'x'
TWIN_INPUTS = ['x', 'norm_w', 'w_in', 'pool_w', 'pool_scale', 'conv_w', 'conv_b', 'gate_b', 'w_branch', 'w_out', 'final_norm_w', 'loss_target', 'm_norm_w', 'm_w_in', 'm_pool_w', 'm_pool_scale', 'm_conv_w', 'm_conv_b', 'm_gate_b', 'm_w_branch', 'm_w_out', 'm_final_norm_w', 'v_norm_w', 'v_w_in', 'v_pool_w', 'v_pool_scale', 'v_conv_w', 'v_conv_b', 'v_gate_b', 'v_w_branch', 'v_w_out', 'v_final_norm_w']
TWIN_OUTPUTS = ['loss', 'grad_x', 'grad_norm_w', 'grad_w_in', 'grad_pool_w', 'grad_pool_scale', 'grad_conv_w', 'grad_conv_b', 'grad_gate_b', 'grad_w_branch', 'grad_w_out', 'grad_final_norm_w', 'delta_norm_w', 'delta_w_in', 'delta_pool_w', 'delta_pool_scale', 'delta_conv_w', 'delta_conv_b', 'delta_gate_b', 'delta_w_branch', 'delta_w_out', 'delta_final_norm_w', 'new_m_norm_w', 'new_m_w_in', 'new_m_pool_w', 'new_m_pool_scale', 'new_m_conv_w', 'new_m_conv_b', 'new_m_gate_b', 'new_m_w_branch', 'new_m_w_out', 'new_m_final_norm_w', 'new_v_norm_w', 'new_v_w_in', 'new_v_pool_w', 'new_v_pool_scale', 'new_v_conv_w', 'new_v_conv_b', 'new_v_gate_b', 'new_v_w_branch', 'new_v_w_out', 'new_v_final_norm_w']
TWIN_LEAF_KINDS = {'loss': 'loss', 'grad_x': 'grad_x', 'grad_norm_w': 'grad_w', 'grad_w_in': 'grad_w', 'grad_pool_w': 'grad_w', 'grad_pool_scale': 'grad_w', 'grad_conv_w': 'grad_w', 'grad_conv_b': 'grad_w', 'grad_gate_b': 'grad_w', 'grad_w_branch': 'grad_w', 'grad_w_out': 'grad_w', 'grad_final_norm_w': 'grad_w', 'delta_norm_w': 'delta_w', 'delta_w_in': 'delta_w', 'delta_pool_w': 'delta_w', 'delta_pool_scale': 'delta_w', 'delta_conv_w': 'delta_w', 'delta_conv_b': 'delta_w', 'delta_gate_b': 'delta_w', 'delta_w_branch': 'delta_w', 'delta_w_out': 'delta_w', 'delta_final_norm_w': 'delta_w', 'new_m_norm_w': 'new_m', 'new_m_w_in': 'new_m', 'new_m_pool_w': 'new_m', 'new_m_pool_scale': 'new_m', 'new_m_conv_w': 'new_m', 'new_m_conv_b': 'new_m', 'new_m_gate_b': 'new_m', 'new_m_w_branch': 'new_m', 'new_m_w_out': 'new_m', 'new_m_final_norm_w': 'new_m', 'new_v_norm_w': 'new_v', 'new_v_w_in': 'new_v', 'new_v_pool_w': 'new_v', 'new_v_pool_scale': 'new_v', 'new_v_conv_w': 'new_v', 'new_v_conv_b': 'new_v', 'new_v_gate_b': 'new_v', 'new_v_w_branch': 'new_v', 'new_v_w_out': 'new_v', 'new_v_final_norm_w': 'new_v'}


def _forward(args):
    return _fwd_reference(*[args[k] for k in FWD_PARAMS])


def _output_shape():
    out = _jax.eval_shape(lambda: _forward(_fwd_setup_inputs(0)))
    return out.shape, out.dtype

N_MICROBATCH = 1
ADAM_LR = 0.001
ADAM_B1 = 0.9
ADAM_B2 = 0.999
ADAM_EPS = 1e-08
ADAM_WD = 0.01
ADAM_STEP = 10
PER_EXAMPLE_BATCH_AXIS = {'x': 0, 'loss_target': 0}
SHARED_INPUTS = []
_WEIGHT_DTYPES = {'norm_w': _jnp.float32, 'w_in': _jnp.float32, 'pool_w': _jnp.float32, 'pool_scale': _jnp.float32, 'conv_w': _jnp.float32, 'conv_b': _jnp.float32, 'gate_b': _jnp.float32, 'w_branch': _jnp.float32, 'w_out': _jnp.float32, 'final_norm_w': _jnp.float32}
MOMENT_SCALE = {'norm_w': 3.208001e-02, 'w_in': 1.431550e-02, 'pool_w': 1.643314e-02, 'pool_scale': 1.606568e-02, 'conv_w': 1.895202e-02, 'conv_b': 1.881276e-02, 'gate_b': 4.834687e-03, 'w_branch': 1.234055e-02, 'w_out': 1.743408e-02, 'final_norm_w': 7.978970e+00}


def _to_microbatches(a, axis):
    t = _jnp.moveaxis(a, axis, 0)
    t = t.reshape((N_MICROBATCH, t.shape[0] // N_MICROBATCH) + t.shape[1:])
    return _jnp.moveaxis(t, 1, axis + 1)


def setup_inputs(seed: int = 0) -> dict:
    inp = _fwd_setup_inputs(seed)
    key = _jax.random.fold_in(_jax.random.key(seed), 7919)
    shape, _ = _output_shape()
    out = dict(inp)
    out["loss_target"] = _jax.random.normal(_jax.random.fold_in(key, 0), shape, _jnp.float32)
    for i, name in enumerate(TWIN_WEIGHTS):
        w = inp[name].astype(_jnp.float32)
        if MOMENT_SCALE is None:
            s = _jnp.sqrt(_jnp.mean(_jnp.square(w)) + 1e-30)
        else:
            s = MOMENT_SCALE[name]
        km, kv = _jax.random.split(_jax.random.fold_in(key, i + 1))
        out[name] = w
        out["m_" + name] = s * _jax.random.normal(km, w.shape, _jnp.float32)
        out["v_" + name] = (s * s) * _jax.random.uniform(kv, w.shape, _jnp.float32, 0.5, 1.5)
    if N_MICROBATCH > 1:
        for name, axis in PER_EXAMPLE_BATCH_AXIS.items():
            out[name] = _to_microbatches(out[name], axis)
    return {'x': out['x'], 'norm_w': out['norm_w'], 'w_in': out['w_in'], 'pool_w': out['pool_w'], 'pool_scale': out['pool_scale'], 'conv_w': out['conv_w'], 'conv_b': out['conv_b'], 'gate_b': out['gate_b'], 'w_branch': out['w_branch'], 'w_out': out['w_out'], 'final_norm_w': out['final_norm_w'], 'loss_target': out['loss_target'], 'm_norm_w': out['m_norm_w'], 'm_w_in': out['m_w_in'], 'm_pool_w': out['m_pool_w'], 'm_pool_scale': out['m_pool_scale'], 'm_conv_w': out['m_conv_w'], 'm_conv_b': out['m_conv_b'], 'm_gate_b': out['m_gate_b'], 'm_w_branch': out['m_w_branch'], 'm_w_out': out['m_w_out'], 'm_final_norm_w': out['m_final_norm_w'], 'v_norm_w': out['v_norm_w'], 'v_w_in': out['v_w_in'], 'v_pool_w': out['v_pool_w'], 'v_pool_scale': out['v_pool_scale'], 'v_conv_w': out['v_conv_w'], 'v_conv_b': out['v_conv_b'], 'v_gate_b': out['v_gate_b'], 'v_w_branch': out['v_w_branch'], 'v_w_out': out['v_w_out'], 'v_final_norm_w': out['v_final_norm_w']}


def _loss(weights, diff, rest, loss_target):
    with _jax.named_scope("forward"):
        args = {**rest, TWIN_DIFF_INPUT: diff, **{k: w.astype(_WEIGHT_DTYPES[k]) for k, w in weights.items()}}
        y = _forward(args)
    with _jax.named_scope("loss_head"):
        err = _jnp.square(y.astype(_jnp.float32) - loss_target)
        return 0.5 * _jnp.sum(_jnp.mean(err, axis=-1)) if err.ndim else 0.5 * err


def _adamw(w, g, m, v):
    m = ADAM_B1 * m + (1.0 - ADAM_B1) * g
    v = ADAM_B2 * v + (1.0 - ADAM_B2) * _jnp.square(g)
    m_hat = m / (1.0 - ADAM_B1 ** ADAM_STEP)
    v_hat = v / (1.0 - ADAM_B2 ** ADAM_STEP)
    delta = -ADAM_LR * (m_hat / (_jnp.sqrt(v_hat) + ADAM_EPS) + ADAM_WD * w)
    return delta, m, v


def reference(x, norm_w, w_in, pool_w, pool_scale, conv_w, conv_b, gate_b, w_branch, w_out, final_norm_w, loss_target, m_norm_w, m_w_in, m_pool_w, m_pool_scale, m_conv_w, m_conv_b, m_gate_b, m_w_branch, m_w_out, m_final_norm_w, v_norm_w, v_w_in, v_pool_w, v_pool_scale, v_conv_w, v_conv_b, v_gate_b, v_w_branch, v_w_out, v_final_norm_w):
    given = dict(x=x, norm_w=norm_w, w_in=w_in, pool_w=pool_w, pool_scale=pool_scale, conv_w=conv_w, conv_b=conv_b, gate_b=gate_b, w_branch=w_branch, w_out=w_out, final_norm_w=final_norm_w, loss_target=loss_target, m_norm_w=m_norm_w, m_w_in=m_w_in, m_pool_w=m_pool_w, m_pool_scale=m_pool_scale, m_conv_w=m_conv_w, m_conv_b=m_conv_b, m_gate_b=m_gate_b, m_w_branch=m_w_branch, m_w_out=m_w_out, m_final_norm_w=m_final_norm_w, v_norm_w=v_norm_w, v_w_in=v_w_in, v_pool_w=v_pool_w, v_pool_scale=v_pool_scale, v_conv_w=v_conv_w, v_conv_b=v_conv_b, v_gate_b=v_gate_b, v_w_branch=v_w_branch, v_w_out=v_w_out, v_final_norm_w=v_final_norm_w)
    weights = {n: given[n] for n in TWIN_WEIGHTS}
    shared = {n: given[n] for n in SHARED_INPUTS}
    per_example = {n: given[n] for n in ['x']}
    grad_fn = _jax.value_and_grad(_loss, argnums=(0, 1))

    def one_microbatch(ex, loss_target):
        ex = dict(ex)
        diff = ex.pop(TWIN_DIFF_INPUT)
        return grad_fn(weights, diff, {**shared, **ex}, loss_target)

    if N_MICROBATCH == 1:
        loss, (grad_w, grad_x) = one_microbatch(per_example, given["loss_target"])
    else:
        def body(carry, xs):
            loss_sum, grad_sum = carry
            l_k, (gw_k, gx_k) = one_microbatch(xs[0], xs[1])
            with _jax.named_scope("update"):
                return (loss_sum + l_k, _jax.tree.map(_jnp.add, grad_sum, gw_k)), gx_k

        init = (_jnp.zeros((), _jnp.float32), _jax.tree.map(_jnp.zeros_like, weights))
        (loss, grad_w), grad_x = _jax.lax.scan(body, init, (per_example, given["loss_target"]))
    with _jax.named_scope("update"):
        delta_w, new_m, new_v = {}, {}, {}
        for n in TWIN_WEIGHTS:
            delta_w[n], new_m[n], new_v[n] = _adamw(weights[n], grad_w[n], given["m_" + n], given["v_" + n])
    return (loss, grad_x, *[grad_w[n] for n in TWIN_WEIGHTS], *[delta_w[n] for n in TWIN_WEIGHTS],
            *[new_m[n] for n in TWIN_WEIGHTS], *[new_v[n] for n in TWIN_WEIGHTS])
```

```python
import math

import jax
import jax.numpy as jnp
from jax import lax
from jax.experimental import pallas as pl
from jax.experimental.pallas import tpu as pltpu

F32 = jnp.float32
BF16 = jnp.bfloat16

NORM_EPS = 1e-6
POOL_WINDOWS = (2, 4, 8, 16)
N_POOL_GROUPS = len(POOL_WINDOWS)
CONV_K = 3
ADAM_LR = 0.001
ADAM_B1 = 0.9
ADAM_B2 = 0.999
ADAM_EPS = 1e-08
ADAM_WD = 0.01
ADAM_STEP = 10

N_CHIPS = 4
N_DEV = 8
HALO = 16
LANES = 128
V7X_VMEM_BYTES = 64 * 1024 * 1024
VMEM_CAP = V7X_VMEM_BYTES - 8 * 1024 * 1024

MESH = pl.DeviceIdType.MESH
ANY = pl.BlockSpec(memory_space=pl.ANY)

NN = (((1,), (0,)), ((), ()))
NT = (((1,), (1,)), ((), ()))
TN = (((0,), (0,)), ((), ()))


def _pick(dim, pref, align):
    if dim <= pref:
        return dim
    t = (pref // align) * align
    while t >= align:
        if dim % t == 0:
            return t
        t -= align
    raise ValueError(f"no tile for {dim} (pref {pref}, align {align})")


def _nbytes(shape, dtype):
    n = 1
    for s in shape:
        if s is not None:
            n *= s
    return n * jnp.dtype(dtype).itemsize


def _params(semantics, block_bytes, extra_bytes=0):
    need = 2 * block_bytes + extra_bytes + (2 << 20)
    return pltpu.CompilerParams(dimension_semantics=semantics,
                                vmem_limit_bytes=int(min(max(need, 16 << 20), VMEM_CAP)))


def _sigmoid(z):
    return jax.nn.sigmoid(z)


def _mm(name, *, grid, operands, in_specs, out_shape, out_specs, pairs, dims, acc_shapes, epilogue,
        semantics=None, temp_bytes=0):
    n_in, n_out = len(operands), len(out_shape)
    kax = len(grid) - 1
    nk = grid[kax]

    def body(*refs):
        ins = refs[:n_in]
        outs = refs[n_in:n_in + n_out]
        accs = refs[n_in + n_out:]
        ids = [pl.program_id(a) for a in range(len(grid))]
        k = ids[kax]

        @pl.when(k == 0)
        def _():
            for a in accs:
                a[...] = jnp.zeros(a.shape, a.dtype)

        for (ai, bi, ci, cond) in pairs:
            def step(ai=ai, bi=bi, ci=ci):
                accs[ci][...] += lax.dot_general(ins[ai][...], ins[bi][...], dims,
                                                 preferred_element_type=F32)
            if cond is None:
                step()
            else:
                pl.when(cond(ids))(step)

        @pl.when(k == nk - 1)
        def _():
            epilogue([a[...] for a in accs], ins, outs, ids)

    if semantics is None:
        semantics = ("parallel",) * kax + ("arbitrary",)
    blk = 0
    for spec, op in zip(in_specs, operands):
        blk += _nbytes(spec.block_shape, op.dtype)
    for spec, o in zip(out_specs, out_shape):
        blk += _nbytes(spec.block_shape, o.dtype)
    acc_bytes = sum(_nbytes(s, F32) for s in acc_shapes)
    return pl.pallas_call(
        body, name=name, grid=grid, in_specs=in_specs, out_specs=out_specs, out_shape=out_shape,
        scratch_shapes=[pltpu.VMEM(s, F32) for s in acc_shapes],
        compiler_params=_params(semantics, blk, 3 * acc_bytes + temp_bytes),
    )(*operands)


def _cast_bf16(name, w2d):
    r, c = w2d.shape
    tr = _pick(r, max(16, (4 << 20) // (4 * c)), 16)

    def body(w_ref, o_ref):
        o_ref[...] = w_ref[...].astype(BF16)

    return pl.pallas_call(
        body, name=name, grid=(r // tr,),
        in_specs=[pl.BlockSpec((tr, c), lambda i: (i, 0))],
        out_specs=pl.BlockSpec((tr, c), lambda i: (i, 0)),
        out_shape=jax.ShapeDtypeStruct((r, c), BF16),
        compiler_params=_params(("parallel",), tr * c * 6),
    )(w2d)


def _rms_fwd(x, norm_w):
    s, d = x.shape
    ts = _pick(s, 256, 16)

    def body(x_ref, w_ref, h_ref):
        xv = x_ref[...]
        rstd = lax.rsqrt(jnp.mean(xv * xv, axis=-1, keepdims=True) + NORM_EPS)
        h_ref[...] = (xv * rstd * w_ref[...]).astype(BF16)

    return pl.pallas_call(
        body, name="rms_fwd", grid=(s // ts,),
        in_specs=[pl.BlockSpec((ts, d), lambda i: (i, 0)), pl.BlockSpec((1, d), lambda i: (0, 0))],
        out_specs=pl.BlockSpec((ts, d), lambda i: (i, 0)),
        out_shape=jax.ShapeDtypeStruct((s, d), BF16),
        compiler_params=_params(("parallel",), ts * d * 6, 3 * ts * d * 4),
    )(x, norm_w)


def _head(x, o, target, fnw):
    s, d = x.shape
    ts = _pick(s, 128, 16)

    def body(x_ref, o_ref, t_ref, w_ref, dx_ref, dxb_ref, loss_ref, gw_ref):
        i = pl.program_id(0)

        @pl.when(i == 0)
        def _():
            loss_ref[...] = jnp.zeros(loss_ref.shape, F32)
            gw_ref[...] = jnp.zeros(gw_ref.shape, F32)

        w = w_ref[...]
        x2 = x_ref[...] + o_ref[...]
        rstd = lax.rsqrt(jnp.mean(x2 * x2, axis=-1, keepdims=True) + NORM_EPS)
        n = x2 * rstd
        e = n * w - t_ref[...]
        loss_ref[...] += 0.5 * jnp.sum(e * e) / d
        dy = e / d
        gw_ref[...] += jnp.sum(dy * n, axis=0, keepdims=True)
        gy = dy * w
        dx = rstd * (gy - n * jnp.mean(gy * n, axis=-1, keepdims=True))
        dx_ref[...] = dx
        dxb_ref[...] = dx.astype(BF16)

    row = pl.BlockSpec((ts, d), lambda i: (i, 0))
    return pl.pallas_call(
        body, name="head", grid=(s // ts,),
        in_specs=[row, row, row, pl.BlockSpec((1, d), lambda i: (0, 0))],
        out_specs=[row, row, pl.BlockSpec((8, LANES), lambda i: (0, 0)),
                   pl.BlockSpec((1, d), lambda i: (0, 0))],
        out_shape=[jax.ShapeDtypeStruct((s, d), F32), jax.ShapeDtypeStruct((s, d), BF16),
                   jax.ShapeDtypeStruct((8, LANES), F32), jax.ShapeDtypeStruct((1, d), F32)],
        compiler_params=_params(("arbitrary",), ts * d * 22, 6 * ts * d * 4),
    )(x, o, target, fnw)


def _rms_bwd(x, dh, dx2, norm_w):
    s, d = x.shape
    ts = _pick(s, 128, 16)

    def body(x_ref, dh_ref, dx2_ref, w_ref, gx_ref, gw_ref):
        i = pl.program_id(0)

        @pl.when(i == 0)
        def _():
            gw_ref[...] = jnp.zeros(gw_ref.shape, F32)

        xv = x_ref[...]
        rstd = lax.rsqrt(jnp.mean(xv * xv, axis=-1, keepdims=True) + NORM_EPS)
        n = xv * rstd
        dhv = dh_ref[...]
        gw_ref[...] += jnp.sum(dhv * n, axis=0, keepdims=True)
        gh = dhv * w_ref[...]
        gx_ref[...] = dx2_ref[...] + rstd * (gh - n * jnp.mean(gh * n, axis=-1, keepdims=True))

    row = pl.BlockSpec((ts, d), lambda i: (i, 0))
    vec = pl.BlockSpec((1, d), lambda i: (0, 0))
    return pl.pallas_call(
        body, name="rms_bwd", grid=(s // ts,),
        in_specs=[row, row, row, vec], out_specs=[row, vec],
        out_shape=[jax.ShapeDtypeStruct((s, d), F32), jax.ShapeDtypeStruct((1, d), F32)],
        compiler_params=_params(("arbitrary",), ts * d * 16, 5 * ts * d * 4),
    )(x, dh, dx2, norm_w)


def _silu(z):
    return z * _sigmoid(z)


def _dsilu(z):
    sg = _sigmoid(z)
    return sg * (1.0 + z * (1.0 - sg))


def _window_sum(ext, window, back):
    n = ext.shape[0]
    acc = ext
    step = 1
    while step < window:
        acc = acc + pltpu.roll(acc, step if back else n - step, 0)
        step *= 2
    return acc


def _shift_rows(ext, k, back):
    n = ext.shape[0]
    return pltpu.roll(ext, k if back else n - k, 0)


def _mix_fwd(proj, pool_wg, pool_scale, conv_wg, conv_b, width):
    s = proj.shape[0]
    w = width
    cg = w // N_POOL_GROUPS
    ts = _pick(s, 128, HALO)
    hb = ts // HALO
    cols = 6 * w

    def body(p_ref, ph_ref, pw_ref, ps_ref, cw_ref, cb_ref, ys_ref):
        i = pl.program_id(0)
        first = i == 0
        t1 = (i * ts + lax.broadcasted_iota(jnp.int32, (ts, 1), 0) + 1).astype(F32)

        def tile(part, g):
            lo = part * w + g * cg
            return p_ref[:, lo:lo + cg].astype(F32)

        def prev(part, g):
            lo = part * w + g * cg
            return jnp.where(first, 0.0, ph_ref[:, lo:lo + cg].astype(F32))

        for g, win in enumerate(POOL_WINDOWS):
            gs = slice(g * cg, (g + 1) * cg)
            u = tile(0, g)
            ext = jnp.concatenate([prev(0, g), u], axis=0)
            wsum = _window_sum(ext, win, True)[HALO:]
            pooled = wsum / jnp.minimum(t1, float(win)) - u
            pw = pw_ref[:, g].reshape(cg, cg)
            mixed = jnp.dot(pooled.astype(BF16), pw, preferred_element_type=F32)
            ys_ref[0, :, gs] = (mixed * ps_ref[:, gs] * _silu(tile(1, g))).astype(BF16)
            v = tile(4, g) * tile(2, g)
            vext = jnp.concatenate([prev(4, g) * prev(2, g), v], axis=0)
            v1 = _shift_rows(vext, 1, True)[HALO:]
            v2 = _shift_rows(vext, 2, True)[HALO:]
            cw = [cw_ref[g, tap:tap + 1, :] for tap in range(CONV_K)]
            y = cb_ref[:, gs] + cw[0] * v2 + cw[1] * v1 + cw[2] * v
            ys_ref[1, :, gs] = (tile(3, g) * y * _silu(tile(5, g))).astype(BF16)

    return pl.pallas_call(
        body, name="mix_fwd", grid=(s // ts,),
        in_specs=[pl.BlockSpec((ts, cols), lambda i: (i, 0)),
                  pl.BlockSpec((HALO, cols), lambda i: (jnp.maximum(i * hb - 1, 0), 0)),
                  pl.BlockSpec(pool_wg.shape, lambda i: (0, 0, 0, 0)),
                  pl.BlockSpec((1, w), lambda i: (0, 0)),
                  pl.BlockSpec(conv_wg.shape, lambda i: (0, 0, 0)),
                  pl.BlockSpec((1, w), lambda i: (0, 0))],
        out_specs=pl.BlockSpec((2, ts, w), lambda i: (0, i, 0)),
        out_shape=jax.ShapeDtypeStruct((2, s, w), BF16),
        compiler_params=_params(("parallel",), (ts + HALO) * cols * 2 + 2 * ts * w * 2
                                + _nbytes(pool_wg.shape, BF16), 24 * (ts + HALO) * cg * 4),
    )(proj, proj, pool_wg, pool_scale, conv_wg, conv_b)


def _mix_bwd(proj, dys, pool_wg, pool_scale, conv_wg, conv_b, width):
    s = proj.shape[0]
    w = width
    cg = w // N_POOL_GROUPS
    ts = _pick(s, 128, HALO)
    hb = ts // HALO
    n_tiles = s // ts
    last_hb = s // HALO - 1
    cols = 6 * w

    def body(p_ref, ph_ref, pn_ref, dy_ref, dyn_ref, pw_ref, ps_ref, cw_ref, cb_ref,
             dp_ref, dpw_ref, dps_ref, dcw_ref, dcb_ref):
        i = pl.program_id(0)
        first = i == 0
        last = i == n_tiles - 1

        @pl.when(first)
        def _():
            dpw_ref[...] = jnp.zeros(dpw_ref.shape, F32)
            dps_ref[...] = jnp.zeros(dps_ref.shape, F32)
            dcw_ref[...] = jnp.zeros(dcw_ref.shape, F32)
            dcb_ref[...] = jnp.zeros(dcb_ref.shape, F32)

        row = i * ts + lax.broadcasted_iota(jnp.int32, (ts + HALO, 1), 0)
        t1_ext = (row + 1).astype(F32)
        t1 = t1_ext[:ts]

        def tile(part, g):
            lo = part * w + g * cg
            return p_ref[:, lo:lo + cg].astype(F32)

        def prev(part, g):
            lo = part * w + g * cg
            return jnp.where(first, 0.0, ph_ref[:, lo:lo + cg].astype(F32))

        def ahead(part, g):
            lo = part * w + g * cg
            return jnp.concatenate([tile(part, g), pn_ref[:, lo:lo + cg].astype(F32)], axis=0)

        def dy_ahead(n, g):
            gs = slice(g * cg, (g + 1) * cg)
            nxt = jnp.where(last, 0.0, dyn_ref[n, :, gs])
            return jnp.concatenate([dy_ref[n, :, gs], nxt], axis=0)

        for g, win in enumerate(POOL_WINDOWS):
            gs = slice(g * cg, (g + 1) * cg)
            u = tile(0, g)
            ext = jnp.concatenate([prev(0, g), u], axis=0)
            pooled = _window_sum(ext, win, True)[HALO:] / jnp.minimum(t1, float(win)) - u
            pooled_b = pooled.astype(BF16)
            pw = pw_ref[:, g].reshape(cg, cg)
            mixed = jnp.dot(pooled_b, pw, preferred_element_type=F32)
            zp_ext = ahead(1, g)
            dy0_ext = dy_ahead(0, g)
            scale = ps_ref[:, gs]
            dms_ext = dy0_ext * _silu(zp_ext)
            dmix_b = (dms_ext * scale).astype(BF16)
            dpooled_ext = lax.dot_general(dmix_b, pw, NT, preferred_element_type=F32)
            dy0 = dy0_ext[:ts]
            zp = zp_ext[:ts]
            dp_ref[:, w + g * cg:w + (g + 1) * cg] = (dy0 * mixed * scale * _dsilu(zp)).astype(BF16)
            dps_ref[:, gs] += jnp.sum(dms_ext[:ts] * mixed, axis=0, keepdims=True)
            dpw = lax.dot_general(pooled_b, dmix_b[:ts], TN, preferred_element_type=F32)
            dpw_ref[g // 2, :, g % 2] += dpw.reshape(N_CHIPS, cg // N_CHIPS, cg)
            q_ext = dpooled_ext / jnp.minimum(t1_ext, float(win))
            du = _window_sum(q_ext, win, False)[:ts] - dpooled_ext[:ts]
            dp_ref[:, gs] = du.astype(BF16)
            uc = tile(2, g)
            cc = tile(4, g)
            v = cc * uc
            vext = jnp.concatenate([prev(4, g) * prev(2, g), v], axis=0)
            v1 = _shift_rows(vext, 1, True)[HALO:]
            v2 = _shift_rows(vext, 2, True)[HALO:]
            cw = [cw_ref[g, tap:tap + 1, :] for tap in range(CONV_K)]
            y = cb_ref[:, gs] + cw[0] * v2 + cw[1] * v1 + cw[2] * v
            bc_ext = ahead(3, g)
            zc_ext = ahead(5, g)
            dy1_ext = dy_ahead(1, g)
            dyy_ext = dy1_ext * bc_ext * _silu(zc_ext)
            dy1 = dy1_ext[:ts]
            bc = bc_ext[:ts]
            zc = zc_ext[:ts]
            dp_ref[:, 3 * w + g * cg:3 * w + (g + 1) * cg] = (dy1 * y * _silu(zc)).astype(BF16)
            dp_ref[:, 5 * w + g * cg:5 * w + (g + 1) * cg] = (dy1 * bc * y * _dsilu(zc)).astype(BF16)
            dyy = dyy_ext[:ts]
            dcb_ref[:, gs] += jnp.sum(dyy, axis=0, keepdims=True)
            for tap, vt in enumerate((v2, v1, v)):
                dcw_ref[g, tap:tap + 1, :] += jnp.sum(dyy * vt, axis=0, keepdims=True)
            dv = (cw[2] * dyy + cw[1] * _shift_rows(dyy_ext, 1, False)[:ts]
                  + cw[0] * _shift_rows(dyy_ext, 2, False)[:ts])
            dp_ref[:, 4 * w + g * cg:4 * w + (g + 1) * cg] = (dv * uc).astype(BF16)
            dp_ref[:, 2 * w + g * cg:2 * w + (g + 1) * cg] = (dv * cc).astype(BF16)

    dpw_shape = (2, N_CHIPS, 2, cg // N_CHIPS, cg)
    return pl.pallas_call(
        body, name="mix_bwd", grid=(n_tiles,),
        in_specs=[pl.BlockSpec((ts, cols), lambda i: (i, 0)),
                  pl.BlockSpec((HALO, cols), lambda i: (jnp.maximum(i * hb - 1, 0), 0)),
                  pl.BlockSpec((HALO, cols), lambda i: (jnp.minimum((i + 1) * hb, last_hb), 0)),
                  pl.BlockSpec((2, ts, w), lambda i: (0, i, 0)),
                  pl.BlockSpec((2, HALO, w), lambda i: (0, jnp.minimum((i + 1) * hb, last_hb), 0)),
                  pl.BlockSpec(pool_wg.shape, lambda i: (0, 0, 0, 0)),
                  pl.BlockSpec((1, w), lambda i: (0, 0)),
                  pl.BlockSpec(conv_wg.shape, lambda i: (0, 0, 0)),
                  pl.BlockSpec((1, w), lambda i: (0, 0))],
        out_specs=[pl.BlockSpec((ts, cols), lambda i: (i, 0)),
                   pl.BlockSpec(dpw_shape, lambda i: (0, 0, 0, 0, 0)),
                   pl.BlockSpec((1, w), lambda i: (0, 0)),
                   pl.BlockSpec(conv_wg.shape, lambda i: (0, 0, 0)),
                   pl.BlockSpec((1, w), lambda i: (0, 0))],
        out_shape=[jax.ShapeDtypeStruct((s, cols), BF16), jax.ShapeDtypeStruct(dpw_shape, F32),
                   jax.ShapeDtypeStruct((1, w), F32), jax.ShapeDtypeStruct(conv_wg.shape, F32),
                   jax.ShapeDtypeStruct((1, w), F32)],
        compiler_params=_params(("arbitrary",), (2 * ts + 2 * HALO) * cols * 2 + (ts + HALO) * w * 8
                                + _nbytes(pool_wg.shape, BF16) + _nbytes(dpw_shape, F32),
                                40 * (ts + HALO) * cg * 4),
    )(proj, proj, proj, dys, dys, pool_wg, pool_scale, conv_wg, conv_b)


def _adamw(name, g, w, m, v):
    r, c = w.shape
    tr = _pick(r, max(8, (1 << 20) // (4 * c)), 8)

    def body(g_ref, w_ref, m_ref, v_ref, go_ref, d_ref, mo_ref, vo_ref):
        gv = g_ref[...]
        mn = ADAM_B1 * m_ref[...] + (1.0 - ADAM_B1) * gv
        vn = ADAM_B2 * v_ref[...] + (1.0 - ADAM_B2) * (gv * gv)
        m_hat = mn / (1.0 - ADAM_B1 ** ADAM_STEP)
        v_hat = vn / (1.0 - ADAM_B2 ** ADAM_STEP)
        go_ref[...] = gv
        d_ref[...] = -ADAM_LR * (m_hat / (jnp.sqrt(v_hat) + ADAM_EPS) + ADAM_WD * w_ref[...])
        mo_ref[...] = mn
        vo_ref[...] = vn

    blk = pl.BlockSpec((tr, c), lambda i: (i, 0))
    sh = jax.ShapeDtypeStruct((r, c), F32)
    return pl.pallas_call(
        body, name=name, grid=(r // tr,), in_specs=[blk] * 4, out_specs=[blk] * 4,
        out_shape=[sh] * 4, compiler_params=_params(("parallel",), tr * c * 32, 4 * tr * c * 4),
    )(g, w, m, v)


def _pair_add(name, g, r1, c_idx):
    _, r, c = g.shape
    tr = _pick(r, max(16, (2 << 20) // (2 * c)), 16)

    def body(c_ref, g_ref, r_ref, o_ref):
        o_ref[...] = (g_ref[...].astype(F32) + r_ref[...].astype(F32)).astype(BF16)

    return pl.pallas_call(
        body, name=name,
        grid_spec=pltpu.PrefetchScalarGridSpec(
            num_scalar_prefetch=1, grid=(r // tr,),
            in_specs=[pl.BlockSpec((None, tr, c), lambda i, cr: (cr[0], i, 0)),
                      pl.BlockSpec((tr, c), lambda i, cr: (i, 0))],
            out_specs=pl.BlockSpec((tr, c), lambda i, cr: (i, 0))),
        out_shape=jax.ShapeDtypeStruct((r, c), BF16),
        compiler_params=_params(("parallel",), tr * c * 6, 3 * tr * c * 4),
    )(c_idx, g, r1)


def _sum_chips(name, r2):
    n, r, c = r2.shape
    tr = _pick(r, max(16, (2 << 20) // (2 * c)), 16)

    def body(r_ref, o_ref):
        acc = r_ref[0].astype(F32)
        for k in range(1, n):
            acc = acc + r_ref[k].astype(F32)
        o_ref[...] = acc

    return pl.pallas_call(
        body, name=name, grid=(r // tr,),
        in_specs=[pl.BlockSpec((n, tr, c), lambda i: (0, i, 0))],
        out_specs=pl.BlockSpec((tr, c), lambda i: (i, 0)),
        out_shape=jax.ShapeDtypeStruct((r, c), F32),
        compiler_params=_params(("parallel",), tr * c * (2 * n + 4), 2 * tr * c * 4),
    )(r2)


def _sum_devices(packs):
    n, r, c = packs.shape

    def body(p_ref, o_ref):
        acc = p_ref[0]
        for k in range(1, n):
            acc = acc + p_ref[k]
        o_ref[...] = acc

    return pl.pallas_call(
        body, name="sum_devices", out_shape=jax.ShapeDtypeStruct((r, c), F32),
        in_specs=[pl.BlockSpec(memory_space=pltpu.VMEM)],
        out_specs=pl.BlockSpec(memory_space=pltpu.VMEM),
    )(packs)


def _place():
    x, y, c = lax.axis_index("x"), lax.axis_index("y"), lax.axis_index("c")
    return x, y, c


def _chip_peers(x, y):
    out = []
    for k, (fx, fy) in enumerate(((0, 1), (1, 0), (1, 1))):
        px = 1 - x if fx else x
        py = 1 - y if fy else y
        out.append((k, px, py, 2 * px + py))
    return out


def _gather_weights(big, small):
    nb, ns = len(big), len(small)

    def body(*refs):
        b_in = refs[:nb]
        s_in = refs[nb:nb + ns]
        b_out = refs[nb + ns:2 * nb + ns]
        s_out = refs[2 * nb + ns:2 * (nb + ns)]
        ici_s, ici_r, d2d_s, d2d_r, sm_s, sm_r, loc = refs[2 * (nb + ns):]
        x, y, c = _place()
        chip = 2 * x + y
        sibling = (x, y, 1 - c)
        peers = _chip_peers(x, y)

        local = [pltpu.make_async_copy(b_in[t], b_out[t].at[chip], loc.at[t]) for t in range(nb)]
        local += [pltpu.make_async_copy(s_in[t], s_out[t].at[chip], loc.at[nb + t]) for t in range(ns)]
        for cp in local:
            cp.start()

        def ici(t, k, px, py):
            return pltpu.make_async_remote_copy(
                src_ref=b_in[t].at[c], dst_ref=b_out[t].at[chip, c],
                send_sem=ici_s.at[t, k], recv_sem=ici_r.at[t, k],
                device_id=(px, py, c), device_id_type=MESH)

        def ici_small(t, k, px, py):
            return pltpu.make_async_remote_copy(
                src_ref=s_in[t], dst_ref=s_out[t].at[chip],
                send_sem=sm_s.at[t, k], recv_sem=sm_r.at[t, k],
                device_id=(px, py, c), device_id_type=MESH)

        def landed(t, k, pchip):
            return pltpu.make_async_remote_copy(
                src_ref=b_out[t].at[pchip, c], dst_ref=b_out[t].at[pchip, c],
                send_sem=ici_s.at[t, k], recv_sem=ici_r.at[t, k],
                device_id=sibling, device_id_type=MESH)

        def passed(t, k, pchip, half):
            return pltpu.make_async_remote_copy(
                src_ref=b_out[t].at[pchip, half], dst_ref=b_out[t].at[pchip, half],
                send_sem=d2d_s.at[t, k], recv_sem=d2d_r.at[t, k],
                device_id=sibling, device_id_type=MESH)

        sends = []
        for t in range(nb):
            for (k, px, py, pchip) in peers:
                sends.append(ici(t, k, px, py))
        for t in range(ns):
            for (k, px, py, pchip) in peers:
                sends.append(ici_small(t, k, px, py))
        for cp in sends:
            cp.start()
        for t in range(nb):
            for (k, px, py, pchip) in peers:
                landed(t, k, pchip).wait_recv()
                fwd = passed(t, k, pchip, c)
                fwd.start()
                sends.append(fwd)
        for t in range(ns):
            for (k, px, py, pchip) in peers:
                pltpu.make_async_remote_copy(
                    src_ref=s_in[t], dst_ref=s_out[t].at[pchip],
                    send_sem=sm_s.at[t, k], recv_sem=sm_r.at[t, k],
                    device_id=sibling, device_id_type=MESH).wait_recv()
        for t in range(nb):
            for (k, px, py, pchip) in peers:
                passed(t, k, pchip, 1 - c).wait_recv()
        for cp in sends:
            cp.wait_send()
        for cp in local:
            cp.wait()

    out_shape = [jax.ShapeDtypeStruct((N_CHIPS,) + b.shape, b.dtype) for b in big]
    out_shape += [jax.ShapeDtypeStruct((N_CHIPS,) + s.shape, s.dtype) for s in small]
    dma = pltpu.SemaphoreType.DMA
    outs = pl.pallas_call(
        body, name="gather_weights", out_shape=out_shape,
        in_specs=[ANY] * (nb + ns), out_specs=[ANY] * (nb + ns),
        scratch_shapes=[dma((nb, 3)), dma((nb, 3)), dma((nb, 3)), dma((nb, 3)),
                        dma((ns, 3)), dma((ns, 3)), dma((nb + ns,))],
    )(*big, *small)
    return outs[:nb], outs[nb:]


def _pair_exchange(grads):
    n = len(grads)

    def body(*refs):
        g_in = refs[:n]
        r_out = refs[n:2 * n]
        send, recv = refs[2 * n:]
        x, y, c = _place()
        cps = [pltpu.make_async_remote_copy(
            src_ref=g_in[t].at[1 - c], dst_ref=r_out[t], send_sem=send.at[t], recv_sem=recv.at[t],
            device_id=(x, y, 1 - c), device_id_type=MESH) for t in range(n)]
        for cp in cps:
            cp.start()
        for cp in cps:
            cp.wait()

    dma = pltpu.SemaphoreType.DMA
    return pl.pallas_call(
        body, name="pair_exchange",
        out_shape=[jax.ShapeDtypeStruct(g.shape[1:], g.dtype) for g in grads],
        in_specs=[ANY] * n, out_specs=[ANY] * n, scratch_shapes=[dma((n,)), dma((n,))],
    )(*grads)


def _scatter_partials(parts):
    n = len(parts)

    def body(*refs):
        p_in = refs[:n]
        r_out = refs[n:2 * n]
        send, recv, loc = refs[2 * n:]
        x, y, c = _place()
        chip = 2 * x + y
        peers = _chip_peers(x, y)
        local = [pltpu.make_async_copy(p_in[t].at[chip], r_out[t].at[chip], loc.at[t]) for t in range(n)]
        for cp in local:
            cp.start()
        cps = []
        for t in range(n):
            for (k, px, py, pchip) in peers:
                cps.append(pltpu.make_async_remote_copy(
                    src_ref=p_in[t].at[pchip], dst_ref=r_out[t].at[chip],
                    send_sem=send.at[t, k], recv_sem=recv.at[t, k],
                    device_id=(px, py, c), device_id_type=MESH))
        for cp in cps:
            cp.start()
        for t in range(n):
            for (k, px, py, pchip) in peers:
                pltpu.make_async_remote_copy(
                    src_ref=p_in[t].at[pchip], dst_ref=r_out[t].at[pchip],
                    send_sem=send.at[t, k], recv_sem=recv.at[t, k],
                    device_id=(px, py, c), device_id_type=MESH).wait_recv()
        for cp in cps:
            cp.wait_send()
        for cp in local:
            cp.wait()

    dma = pltpu.SemaphoreType.DMA
    return pl.pallas_call(
        body, name="scatter_partials",
        out_shape=[jax.ShapeDtypeStruct(p.shape, p.dtype) for p in parts],
        in_specs=[ANY] * n, out_specs=[ANY] * n,
        scratch_shapes=[dma((n, 3)), dma((n, 3)), dma((n,))],
    )(*parts)


def _pair_share(halves):
    n = len(halves)

    def body(*refs):
        h_in = refs[:n]
        f_out = refs[n:2 * n]
        send, recv, loc = refs[2 * n:]
        x, y, c = _place()
        local = [pltpu.make_async_copy(h_in[t], f_out[t].at[c], loc.at[t]) for t in range(n)]
        for cp in local:
            cp.start()
        cps = [pltpu.make_async_remote_copy(
            src_ref=h_in[t], dst_ref=f_out[t].at[c], send_sem=send.at[t], recv_sem=recv.at[t],
            device_id=(x, y, 1 - c), device_id_type=MESH) for t in range(n)]
        for cp in cps:
            cp.start()
        for t in range(n):
            pltpu.make_async_remote_copy(
                src_ref=h_in[t], dst_ref=f_out[t].at[1 - c], send_sem=send.at[t], recv_sem=recv.at[t],
                device_id=(x, y, 1 - c), device_id_type=MESH).wait_recv()
        for cp in cps:
            cp.wait_send()
        for cp in local:
            cp.wait()

    dma = pltpu.SemaphoreType.DMA
    return pl.pallas_call(
        body, name="pair_share",
        out_shape=[jax.ShapeDtypeStruct((2,) + h.shape, h.dtype) for h in halves],
        in_specs=[ANY] * n, out_specs=[ANY] * n,
        scratch_shapes=[dma((n,)), dma((n,)), dma((n,))],
    )(*halves)


def _gather_packs(pack):
    def body(p_ref, o_ref, send, recv, loc):
        x, y, c = _place()
        me = 4 * x + 2 * y + c
        mine = pltpu.make_async_copy(p_ref, o_ref.at[me], loc)
        mine.start()
        flips = [(fx, fy, fc) for fx in (0, 1) for fy in (0, 1) for fc in (0, 1)][1:]
        cps = []
        for k, (fx, fy, fc) in enumerate(flips):
            peer = (1 - x if fx else x, 1 - y if fy else y, 1 - c if fc else c)
            cps.append(pltpu.make_async_remote_copy(
                src_ref=p_ref, dst_ref=o_ref.at[me], send_sem=send.at[k], recv_sem=recv.at[k],
                device_id=peer, device_id_type=MESH))
        for cp in cps:
            cp.start()
        for k, (fx, fy, fc) in enumerate(flips):
            px, py, pc = (1 - x if fx else x, 1 - y if fy else y, 1 - c if fc else c)
            pltpu.make_async_remote_copy(
                src_ref=p_ref, dst_ref=o_ref.at[4 * px + 2 * py + pc],
                send_sem=send.at[k], recv_sem=recv.at[k],
                device_id=(px, py, pc), device_id_type=MESH).wait_recv()
        for cp in cps:
            cp.wait_send()
        mine.wait()

    dma = pltpu.SemaphoreType.DMA
    return pl.pallas_call(
        body, name="gather_packs", out_shape=jax.ShapeDtypeStruct((N_DEV,) + pack.shape, pack.dtype),
        in_specs=[ANY], out_specs=ANY, scratch_shapes=[dma((7,)), dma((7,)), dma],
    )(pack)


def _flat_pack(pieces):
    flat = jnp.concatenate([p.reshape(-1) for p in pieces])
    pad = (-flat.shape[0]) % (8 * LANES)
    flat = jnp.concatenate([flat, jnp.zeros((pad,), F32)])
    return flat.reshape(-1, LANES)


def _unpack(pack, shapes):
    flat = pack.reshape(-1)
    out, off = [], 0
    for sh in shapes:
        n = 1
        for s in sh:
            n *= s
        out.append(flat[off:off + n].reshape(sh))
        off += n
    return out


def kernel(x, norm_w, w_in, pool_w, pool_scale, conv_w, conv_b, gate_b, w_branch, w_out, final_norm_w, loss_target, m_norm_w, m_w_in, m_pool_w, m_pool_scale, m_conv_w, m_conv_b, m_gate_b, m_w_branch, m_w_out, m_final_norm_w, v_norm_w, v_w_in, v_pool_w, v_pool_scale, v_conv_w, v_conv_b, v_gate_b, v_w_branch, v_w_out, v_final_norm_w):
    _, s, d = x.shape
    w = d // 2
    cg = w // N_POOL_GROUPS
    p = 6 * w + 2 * d
    pq = p // N_CHIPS
    dq = d // N_CHIPS
    assert w_in.shape == (1, d, pq) and w_branch.shape == (1, 2, w, dq) and w_out.shape == (1, dq, d)
    assert pool_w.shape == (1, N_POOL_GROUPS, cg // N_CHIPS, cg) and conv_w.shape == (1, CONV_K, cg)

    x2d = x.reshape(s, d)
    tgt = loss_target.reshape(s, d)
    c_idx = lax.axis_index("c").astype(jnp.int32).reshape(1)
    chip = 2 * lax.axis_index("x") + lax.axis_index("y")

    big_w = [w_in.reshape(d, pq), w_out.reshape(dq, d), w_branch.reshape(2 * w, dq),
             pool_w.reshape(cg, cg)]
    names = ["in", "out", "branch", "pool"]
    big_b = [_cast_bf16("cast_" + nm, a) for nm, a in zip(names, big_w)]
    big_b = [b.reshape(2, b.shape[0] // 2, b.shape[1]) for b in big_b]
    (wg_in, wg_out, wg_br, wg_pool), (wg_cw, wg_gb) = _gather_weights(
        big_b, [conv_w.reshape(CONV_K, cg), gate_b.reshape(2, dq)])
    wg_in = wg_in.reshape(N_CHIPS, d, pq)
    wg_out = wg_out.reshape(d, d)
    wg_pool = wg_pool.reshape(N_CHIPS, N_POOL_GROUPS, cg // N_CHIPS, cg)

    h = _rms_fwd(x2d, norm_w)

    tm = _pick(s, 1024, 16)
    tn_p = _pick(math.gcd(pq, 6 * w, d), 1024, LANES)
    qp = pq // tn_p
    tk_d = _pick(d, 1024, LANES)
    proj = _mm(
        "proj", grid=(s // tm, p // tn_p, d // tk_d), operands=[h, wg_in],
        in_specs=[pl.BlockSpec((tm, tk_d), lambda i, j, k: (i, k)),
                  pl.BlockSpec((None, tk_d, tn_p), lambda i, j, k: (j // qp, k, j % qp))],
        out_shape=[jax.ShapeDtypeStruct((s, p), BF16)],
        out_specs=[pl.BlockSpec((tm, tn_p), lambda i, j, k: (i, j))],
        pairs=[(0, 1, 0, None)], dims=NN, acc_shapes=[(tm, tn_p)],
        epilogue=lambda accs, ins, outs, ids: outs[0].__setitem__(Ellipsis, accs[0].astype(BF16)),
    )[0]

    ys = _mix_fwd(proj, wg_pool, pool_scale, wg_cw, conv_b, w)

    tn_d = _pick(dq, 1024, LANES)
    qd = dq // tn_d
    tk_w = _pick(w, 1024, LANES)
    gl0 = (6 * w) // tn_d
    gl1 = (6 * w + d) // tn_d

    def gate_specs(im):
        return [pl.BlockSpec((tm, tn_d), lambda *a: (im(*a)[0], gl0 + im(*a)[1])),
                pl.BlockSpec((tm, tn_d), lambda *a: (im(*a)[0], gl1 + im(*a)[1])),
                pl.BlockSpec((None, 2, tn_d), lambda *a: (im(*a)[1] // qd, 0, im(*a)[1] % qd))]

    def merge_epilogue(accs, ins, outs, ids):
        gb = ins[6][...]
        g0 = _sigmoid(ins[4][...].astype(F32) + gb[0:1])
        g1 = _sigmoid(ins[5][...].astype(F32) + gb[1:2])
        outs[0][...] = (g0 * accs[0] + g1 * accs[1]).astype(BF16)
        outs[1][0] = accs[0].astype(BF16)
        outs[1][1] = accs[1].astype(BF16)

    merged, br = _mm(
        "branch_merge", grid=(s // tm, d // tn_d, w // tk_w),
        operands=[ys, wg_br, ys, wg_br, proj, proj, wg_gb],
        in_specs=[pl.BlockSpec((None, tm, tk_w), lambda i, j, k: (0, i, k)),
                  pl.BlockSpec((None, None, tk_w, tn_d), lambda i, j, k: (j // qd, 0, k, j % qd)),
                  pl.BlockSpec((None, tm, tk_w), lambda i, j, k: (1, i, k)),
                  pl.BlockSpec((None, None, tk_w, tn_d), lambda i, j, k: (j // qd, 1, k, j % qd)),
                  *gate_specs(lambda i, j, k: (i, j))],
        out_shape=[jax.ShapeDtypeStruct((s, d), BF16), jax.ShapeDtypeStruct((2, s, d), BF16)],
        out_specs=[pl.BlockSpec((tm, tn_d), lambda i, j, k: (i, j)),
                   pl.BlockSpec((2, tm, tn_d), lambda i, j, k: (0, i, j))],
        pairs=[(0, 1, 0, None), (2, 3, 1, None)], dims=NN, acc_shapes=[(tm, tn_d)] * 2,
        epilogue=merge_epilogue, temp_bytes=6 * tm * tn_d * 4,
    )

    tn_f = _pick(d, 1024, LANES)
    o = _mm(
        "out_proj", grid=(s // tm, d // tn_f, d // tk_d), operands=[merged, wg_out],
        in_specs=[pl.BlockSpec((tm, tk_d), lambda i, j, k: (i, k)),
                  pl.BlockSpec((tk_d, tn_f), lambda i, j, k: (k, j))],
        out_shape=[jax.ShapeDtypeStruct((s, d), F32)],
        out_specs=[pl.BlockSpec((tm, tn_f), lambda i, j, k: (i, j))],
        pairs=[(0, 1, 0, None)], dims=NN, acc_shapes=[(tm, tn_f)],
        epilogue=lambda accs, ins, outs, ids: outs[0].__setitem__(Ellipsis, accs[0]),
    )[0]

    dx2, dx2b, loss_part, g_fnw = _head(x2d, o, tgt, final_norm_w.reshape(1, d))

    def gate_bwd_epilogue(accs, ins, outs, ids):
        dm = accs[0]
        gb = ins[5][...]
        i = ids[1]

        @pl.when(i == 0)
        def _():
            outs[2][...] = jnp.zeros(outs[2].shape, F32)

        for n in range(2):
            gate = _sigmoid(ins[3 + n][...].astype(F32) + gb[n:n + 1])
            outs[0][n] = (dm * gate).astype(BF16)
            dgl = dm * ins[2][n].astype(F32) * gate * (1.0 - gate)
            outs[1][n] = dgl.astype(BF16)
            outs[2][n:n + 1, :] += jnp.sum(dgl, axis=0, keepdims=True)

    d_br, dgl, g_gb = _mm(
        "out_proj_bwd_gate", grid=(d // tn_d, s // tm, d // tk_d),
        operands=[dx2b, wg_out, br, proj, proj, wg_gb],
        in_specs=[pl.BlockSpec((tm, tk_d), lambda j, i, k: (i, k)),
                  pl.BlockSpec((tn_d, tk_d), lambda j, i, k: (j, k)),
                  pl.BlockSpec((2, tm, tn_d), lambda j, i, k: (0, i, j)),
                  *gate_specs(lambda j, i, k: (i, j))],
        out_shape=[jax.ShapeDtypeStruct((2, s, d), BF16), jax.ShapeDtypeStruct((2, s, d), BF16),
                   jax.ShapeDtypeStruct((2, d), F32)],
        out_specs=[pl.BlockSpec((2, tm, tn_d), lambda j, i, k: (0, i, j)),
                   pl.BlockSpec((2, tm, tn_d), lambda j, i, k: (0, i, j)),
                   pl.BlockSpec((2, tn_d), lambda j, i, k: (0, j))],
        pairs=[(0, 1, 0, None)], dims=NT, acc_shapes=[(tm, tn_d)],
        epilogue=gate_bwd_epilogue, semantics=("parallel", "arbitrary", "arbitrary"),
        temp_bytes=8 * tm * tn_d * 4,
    )

    hh_out = d // 8
    tm_o = _pick(hh_out, 512, LANES)
    nb_o = hh_out // tm_o
    tk_s = _pick(s, 1024, LANES)
    g_out = _mm(
        "w_out_grad", grid=(d // tm_o, d // tn_f, s // tk_s), operands=[merged, dx2b],
        in_specs=[pl.BlockSpec((tk_s, tm_o), lambda i, j, k: (k, i)),
                  pl.BlockSpec((tk_s, tn_f), lambda i, j, k: (k, j))],
        out_shape=[jax.ShapeDtypeStruct((2, N_CHIPS, hh_out, d), BF16)],
        out_specs=[pl.BlockSpec((None, None, tm_o, tn_f),
                                lambda i, j, k: ((i // nb_o) % 2, i // (2 * nb_o), i % nb_o, j))],
        pairs=[(0, 1, 0, None)], dims=TN, acc_shapes=[(tm_o, tn_f)],
        epilogue=lambda accs, ins, outs, ids: outs[0].__setitem__(Ellipsis, accs[0].astype(BF16)),
    )[0]

    tn_w = _pick(w, 1024, LANES)
    tk_q = _pick(dq, 1024, LANES)
    qk = dq // tk_q
    dys = _mm(
        "branch_bwd", grid=(2, s // tm, w // tn_w, d // tk_q), operands=[d_br, wg_br],
        in_specs=[pl.BlockSpec((None, tm, tk_q), lambda n, i, j, k: (n, i, k)),
                  pl.BlockSpec((None, None, tn_w, tk_q), lambda n, i, j, k: (k // qk, n, j, k % qk))],
        out_shape=[jax.ShapeDtypeStruct((2, s, w), F32)],
        out_specs=[pl.BlockSpec((None, tm, tn_w), lambda n, i, j, k: (n, i, j))],
        pairs=[(0, 1, 0, None)], dims=NT, acc_shapes=[(tm, tn_w)],
        epilogue=lambda accs, ins, outs, ids: outs[0].__setitem__(Ellipsis, accs[0]),
    )[0]

    tm_w = _pick(w, 1024, LANES)
    g_br = _mm(
        "w_branch_grad", grid=(2, w // tm_w, d // tn_d, s // tk_s), operands=[ys, d_br],
        in_specs=[pl.BlockSpec((None, tk_s, tm_w), lambda n, i, j, k: (n, k, i)),
                  pl.BlockSpec((None, tk_s, tn_d), lambda n, i, j, k: (n, k, j))],
        out_shape=[jax.ShapeDtypeStruct((2, N_CHIPS, w, dq), BF16)],
        out_specs=[pl.BlockSpec((None, None, tm_w, tn_d), lambda n, i, j, k: (n, j // qd, i, j % qd))],
        pairs=[(0, 1, 0, None)], dims=TN, acc_shapes=[(tm_w, tn_d)],
        epilogue=lambda accs, ins, outs, ids: outs[0].__setitem__(Ellipsis, accs[0].astype(BF16)),
    )[0]

    dpa, g_pool, g_ps, g_cw, g_cb = _mix_bwd(proj, dys, wg_pool, pool_scale, wg_cw, conv_b, w)

    na = (6 * w) // tn_p
    qg = d // tn_p
    hh_in = d // 2
    tm_i = _pick(hh_in, 1024, LANES)
    nb_i = hh_in // tm_i
    g_in = _mm(
        "w_in_grad", grid=(d // tm_i, p // tn_p, s // tk_s), operands=[h, dpa, dgl],
        in_specs=[pl.BlockSpec((tk_s, tm_i), lambda i, j, k: (k, i)),
                  pl.BlockSpec((tk_s, tn_p), lambda i, j, k: (k, jnp.minimum(j, na - 1))),
                  pl.BlockSpec((None, tk_s, tn_p),
                               lambda i, j, k: (jnp.maximum(j - na, 0) // qg, k, jnp.maximum(j - na, 0) % qg))],
        out_shape=[jax.ShapeDtypeStruct((2, N_CHIPS, hh_in, pq), BF16)],
        out_specs=[pl.BlockSpec((None, None, tm_i, tn_p),
                                lambda i, j, k: (i // nb_i, j // qp, i % nb_i, j % qp))],
        pairs=[(0, 1, 0, lambda ids: ids[1] < na), (0, 2, 0, lambda ids: ids[1] >= na)],
        dims=TN, acc_shapes=[(tm_i, tn_p)],
        epilogue=lambda accs, ins, outs, ids: outs[0].__setitem__(Ellipsis, accs[0].astype(BF16)),
    )[0]

    dh = _mm(
        "proj_bwd", grid=(s // tm, d // tn_f, p // tn_p), operands=[dpa, dgl, wg_in],
        in_specs=[pl.BlockSpec((tm, tn_p), lambda i, j, k: (i, jnp.minimum(k, na - 1))),
                  pl.BlockSpec((None, tm, tn_p),
                               lambda i, j, k: (jnp.maximum(k - na, 0) // qg, i, jnp.maximum(k - na, 0) % qg)),
                  pl.BlockSpec((None, tn_f, tn_p), lambda i, j, k: (k // qp, j, k % qp))],
        out_shape=[jax.ShapeDtypeStruct((s, d), F32)],
        out_specs=[pl.BlockSpec((tm, tn_f), lambda i, j, k: (i, j))],
        pairs=[(0, 2, 0, lambda ids: ids[2] < na), (1, 2, 0, lambda ids: ids[2] >= na)],
        dims=NT, acc_shapes=[(tm, tn_f)],
        epilogue=lambda accs, ins, outs, ids: outs[0].__setitem__(Ellipsis, accs[0]),
    )[0]

    grad_x, g_nw = _rms_bwd(x2d, dh, dx2, norm_w)

    g_pool = g_pool.astype(BF16).reshape(2, N_CHIPS, 2 * (cg // N_CHIPS), cg)
    grads = [g_in, g_out, g_br, g_pool]
    recv1 = _pair_exchange(grads)
    parts = []
    for nm, g, r1 in zip(names, grads, recv1):
        _, nc, hh, wd = g.shape
        pt = _pair_add("pair_add_" + nm, g.reshape(2, nc * hh, wd), r1.reshape(nc * hh, wd), c_idx)
        parts.append(pt.reshape(nc, hh, wd))
    recv2 = _scatter_partials(parts)
    halves = [_sum_chips("sum_chips_" + nm, r2) for nm, r2 in zip(names, recv2)]
    full = _pair_share(halves)

    g_cw_full = jnp.transpose(g_cw, (1, 0, 2)).reshape(CONV_K, w)
    small_shapes = [(LANES,), (1, d), (1, d), (1, w), (1, w), (CONV_K, w), (2, d)]
    pack = _flat_pack([loss_part[0], g_nw, g_fnw, g_ps, g_cb, g_cw_full, g_gb])
    total = _sum_devices(_gather_packs(pack))
    t_loss, t_nw, t_fnw, t_ps, t_cb, t_cw, t_gb = _unpack(total, small_shapes)
    loss = t_loss[0]
    t_cw = lax.dynamic_slice_in_dim(t_cw, chip * cg, cg, axis=1)
    t_gb = lax.dynamic_slice_in_dim(t_gb, chip * dq, dq, axis=1)

    out = {}
    big_names = ["w_in", "w_out", "w_branch", "pool_w"]
    big_m = [m_w_in, m_w_out, m_w_branch, m_pool_w]
    big_v = [v_w_in, v_w_out, v_w_branch, v_pool_w]
    big_orig = [w_in, w_out, w_branch, pool_w]
    for nm, g, w2, mm_, vv_, orig in zip(big_names, full, big_w, big_m, big_v, big_orig):
        res = _adamw("adamw_" + nm, g.reshape(w2.shape), w2, mm_.reshape(w2.shape), vv_.reshape(w2.shape))
        out[nm] = [r.reshape(orig.shape) for r in res]

    sm_names = ["norm_w", "final_norm_w", "pool_scale", "conv_b", "conv_w", "gate_b"]
    sm_g = [t_nw, t_fnw, t_ps, t_cb, t_cw, t_gb]
    sm_w = [norm_w, final_norm_w, pool_scale, conv_b, conv_w, gate_b]
    sm_m = [m_norm_w, m_final_norm_w, m_pool_scale, m_conv_b, m_conv_w, m_gate_b]
    sm_v = [v_norm_w, v_final_norm_w, v_pool_scale, v_conv_b, v_conv_w, v_gate_b]
    sm_shapes = [a.shape for a in sm_w]
    res = _adamw("adamw_small", _flat_pack(sm_g), _flat_pack(sm_w), _flat_pack(sm_m), _flat_pack(sm_v))
    res = [_unpack(r, sm_shapes) for r in res]
    for idx, nm in enumerate(sm_names):
        out[nm] = [r[idx] for r in res]

    order = ["norm_w", "w_in", "pool_w", "pool_scale", "conv_w", "conv_b", "gate_b", "w_branch", "w_out",
             "final_norm_w"]
    outs = [loss, grad_x.reshape(x.shape)]
    for kind in range(4):
        outs += [out[nm][kind] for nm in order]
    return tuple(outs)
```

```python
import math

import jax
import jax.numpy as jnp
from jax import lax
from jax.experimental import pallas as pl
from jax.experimental.pallas import tpu as pltpu

F32 = jnp.float32
BF16 = jnp.bfloat16

NORM_EPS = 1e-6
POOL_WINDOWS = (2, 4, 8, 16)
N_POOL_GROUPS = len(POOL_WINDOWS)
CONV_K = 3
ADAM_LR = 0.001
ADAM_B1 = 0.9
ADAM_B2 = 0.999
ADAM_EPS = 1e-08
ADAM_WD = 0.01
ADAM_STEP = 10

N_CHIPS = 4
N_DEV = 8
HALO = 16
LANES = 128
V7X_VMEM_BYTES = 64 * 1024 * 1024
VMEM_CAP = V7X_VMEM_BYTES - 8 * 1024 * 1024

MESH = pl.DeviceIdType.MESH
ANY = pl.BlockSpec(memory_space=pl.ANY)

NN = (((1,), (0,)), ((), ()))
NT = (((1,), (1,)), ((), ()))
TN = (((0,), (0,)), ((), ()))


def _pick(dim, pref, align):
    if dim <= pref:
        return dim
    t = (pref // align) * align
    while t >= align:
        if dim % t == 0:
            return t
        t -= align
    raise ValueError(f"no tile for {dim} (pref {pref}, align {align})")


def _nbytes(shape, dtype):
    n = 1
    for s in shape:
        if s is not None:
            n *= s
    return n * jnp.dtype(dtype).itemsize


def _params(semantics, block_bytes, extra_bytes=0):
    need = 2 * block_bytes + extra_bytes + (2 << 20)
    return pltpu.CompilerParams(dimension_semantics=semantics,
                                vmem_limit_bytes=int(min(max(need, 16 << 20), VMEM_CAP)))


def _sigmoid(z):
    return jax.nn.sigmoid(z)


def _mm(name, *, grid, operands, in_specs, out_shape, out_specs, pairs, dims, acc_shapes, epilogue,
        semantics=None, temp_bytes=0, comm=None):
    n_in, n_out = len(operands), len(out_shape)
    kax = len(grid) - 1
    nk = grid[kax]
    c_ops = list(comm.operands) if comm else []
    c_out = list(comm.out_shape) if comm else []
    c_sems = list(comm.scratch) if comm else []
    n_ci, n_co, n_cs = len(c_ops), len(c_out), len(c_sems)

    def body(*refs):
        ins = refs[:n_in]
        c_in_refs = refs[n_in:n_in + n_ci]
        outs = refs[n_in + n_ci:n_in + n_ci + n_out]
        c_out_refs = refs[n_in + n_ci + n_out:n_in + n_ci + n_out + n_co]
        scratch = refs[n_in + n_ci + n_out + n_co:]
        accs = scratch[:len(acc_shapes)]
        sems = scratch[len(acc_shapes):]
        ids = [pl.program_id(a) for a in range(len(grid))]
        k = ids[kax]

        if comm:
            is_first = ids[0] == 0
            is_last = ids[0] == grid[0] - 1
            for a in range(1, len(grid)):
                is_first = jnp.logical_and(is_first, ids[a] == 0)
                is_last = jnp.logical_and(is_last, ids[a] == grid[a] - 1)

            @pl.when(is_first)
            def _():
                comm.start(c_in_refs, c_out_refs, sems)

        @pl.when(k == 0)
        def _():
            for a in accs:
                a[...] = jnp.zeros(a.shape, a.dtype)

        for (ai, bi, ci, cond) in pairs:
            def step(ai=ai, bi=bi, ci=ci):
                accs[ci][...] += lax.dot_general(ins[ai][...], ins[bi][...], dims,
                                                 preferred_element_type=F32)
            if cond is None:
                step()
            else:
                pl.when(cond(ids))(step)

        @pl.when(k == nk - 1)
        def _():
            epilogue([a[...] for a in accs], ins, outs, ids)

        if comm:
            @pl.when(is_last)
            def _():
                comm.finish(c_in_refs, c_out_refs, sems)

    if semantics is None:
        semantics = ("parallel",) * kax + ("arbitrary",)
    if comm:
        semantics = ("arbitrary",) * len(grid)
    blk = 0
    for spec, op in zip(in_specs, operands):
        blk += _nbytes(spec.block_shape, op.dtype)
    for spec, o in zip(out_specs, out_shape):
        blk += _nbytes(spec.block_shape, o.dtype)
    acc_bytes = sum(_nbytes(s, F32) for s in acc_shapes)
    res = pl.pallas_call(
        body, name=name, grid=grid, in_specs=list(in_specs) + [ANY] * n_ci,
        out_specs=list(out_specs) + [ANY] * n_co, out_shape=list(out_shape) + c_out,
        scratch_shapes=[pltpu.VMEM(s, F32) for s in acc_shapes] + c_sems,
        compiler_params=_params(semantics, blk, 3 * acc_bytes + temp_bytes),
    )(*operands, *c_ops)
    return res[:n_out], res[n_out:]


def _cast_bf16(name, w2d):
    r, c = w2d.shape
    tr = _pick(r, max(16, (4 << 20) // (4 * c)), 16)

    def body(w_ref, o_ref):
        o_ref[...] = w_ref[...].astype(BF16)

    return pl.pallas_call(
        body, name=name, grid=(r // tr,),
        in_specs=[pl.BlockSpec((tr, c), lambda i: (i, 0))],
        out_specs=pl.BlockSpec((tr, c), lambda i: (i, 0)),
        out_shape=jax.ShapeDtypeStruct((r, c), BF16),
        compiler_params=_params(("parallel",), tr * c * 6),
    )(w2d)


def _rms_fwd(x, norm_w):
    s, d = x.shape
    ts = _pick(s, 256, 16)

    def body(x_ref, w_ref, h_ref):
        xv = x_ref[...]
        rstd = lax.rsqrt(jnp.mean(xv * xv, axis=-1, keepdims=True) + NORM_EPS)
        h_ref[...] = (xv * rstd * w_ref[...]).astype(BF16)

    return pl.pallas_call(
        body, name="rms_fwd", grid=(s // ts,),
        in_specs=[pl.BlockSpec((ts, d), lambda i: (i, 0)), pl.BlockSpec((1, d), lambda i: (0, 0))],
        out_specs=pl.BlockSpec((ts, d), lambda i: (i, 0)),
        out_shape=jax.ShapeDtypeStruct((s, d), BF16),
        compiler_params=_params(("parallel",), ts * d * 6, 3 * ts * d * 4),
    )(x, norm_w)


def _head(x, o, target, fnw):
    s, d = x.shape
    ts = _pick(s, 128, 16)

    def body(x_ref, o_ref, t_ref, w_ref, dx_ref, dxb_ref, loss_ref, gw_ref):
        i = pl.program_id(0)

        @pl.when(i == 0)
        def _():
            loss_ref[...] = jnp.zeros(loss_ref.shape, F32)
            gw_ref[...] = jnp.zeros(gw_ref.shape, F32)

        w = w_ref[...]
        x2 = x_ref[...] + o_ref[...]
        rstd = lax.rsqrt(jnp.mean(x2 * x2, axis=-1, keepdims=True) + NORM_EPS)
        n = x2 * rstd
        e = n * w - t_ref[...]
        loss_ref[...] += 0.5 * jnp.sum(e * e) / d
        dy = e / d
        gw_ref[...] += jnp.sum(dy * n, axis=0, keepdims=True)
        gy = dy * w
        dx = rstd * (gy - n * jnp.mean(gy * n, axis=-1, keepdims=True))
        dx_ref[...] = dx
        dxb_ref[...] = dx.astype(BF16)

    row = pl.BlockSpec((ts, d), lambda i: (i, 0))
    return pl.pallas_call(
        body, name="head", grid=(s // ts,),
        in_specs=[row, row, row, pl.BlockSpec((1, d), lambda i: (0, 0))],
        out_specs=[row, row, pl.BlockSpec((8, LANES), lambda i: (0, 0)),
                   pl.BlockSpec((1, d), lambda i: (0, 0))],
        out_shape=[jax.ShapeDtypeStruct((s, d), F32), jax.ShapeDtypeStruct((s, d), BF16),
                   jax.ShapeDtypeStruct((8, LANES), F32), jax.ShapeDtypeStruct((1, d), F32)],
        compiler_params=_params(("arbitrary",), ts * d * 22, 6 * ts * d * 4),
    )(x, o, target, fnw)


def _rms_bwd(x, dh, dx2, norm_w):
    s, d = x.shape
    ts = _pick(s, 128, 16)

    def body(x_ref, dh_ref, dx2_ref, w_ref, gx_ref, gw_ref):
        i = pl.program_id(0)

        @pl.when(i == 0)
        def _():
            gw_ref[...] = jnp.zeros(gw_ref.shape, F32)

        xv = x_ref[...]
        rstd = lax.rsqrt(jnp.mean(xv * xv, axis=-1, keepdims=True) + NORM_EPS)
        n = xv * rstd
        dhv = dh_ref[...]
        gw_ref[...] += jnp.sum(dhv * n, axis=0, keepdims=True)
        gh = dhv * w_ref[...]
        gx_ref[...] = dx2_ref[...] + rstd * (gh - n * jnp.mean(gh * n, axis=-1, keepdims=True))

    row = pl.BlockSpec((ts, d), lambda i: (i, 0))
    vec = pl.BlockSpec((1, d), lambda i: (0, 0))
    return pl.pallas_call(
        body, name="rms_bwd", grid=(s // ts,),
        in_specs=[row, row, row, vec], out_specs=[row, vec],
        out_shape=[jax.ShapeDtypeStruct((s, d), F32), jax.ShapeDtypeStruct((1, d), F32)],
        compiler_params=_params(("arbitrary",), ts * d * 16, 5 * ts * d * 4),
    )(x, dh, dx2, norm_w)


def _silu(z):
    return z * _sigmoid(z)


def _dsilu(z):
    sg = _sigmoid(z)
    return sg * (1.0 + z * (1.0 - sg))


def _window_sum(ext, window, back):
    n = ext.shape[0]
    acc = ext
    step = 1
    while step < window:
        acc = acc + pltpu.roll(acc, step if back else n - step, 0)
        step *= 2
    return acc


def _shift_rows(ext, k, back):
    n = ext.shape[0]
    return pltpu.roll(ext, k if back else n - k, 0)


def _mix_fwd(proj, pool_wg, pool_scale, conv_wg, conv_b, width):
    s = proj.shape[0]
    w = width
    cg = w // N_POOL_GROUPS
    ts = _pick(s, 128, HALO)
    hb = ts // HALO
    cols = 6 * w

    def body(p_ref, ph_ref, pw_ref, ps_ref, cw_ref, cb_ref, ys_ref):
        i = pl.program_id(0)
        first = i == 0
        t1 = (i * ts + lax.broadcasted_iota(jnp.int32, (ts, 1), 0) + 1).astype(F32)

        def tile(part, g):
            lo = part * w + g * cg
            return p_ref[:, lo:lo + cg].astype(F32)

        def prev(part, g):
            lo = part * w + g * cg
            return jnp.where(first, 0.0, ph_ref[:, lo:lo + cg].astype(F32))

        for g, win in enumerate(POOL_WINDOWS):
            gs = slice(g * cg, (g + 1) * cg)
            u = tile(0, g)
            ext = jnp.concatenate([prev(0, g), u], axis=0)
            wsum = _window_sum(ext, win, True)[HALO:]
            pooled = wsum / jnp.minimum(t1, float(win)) - u
            pw = pw_ref[:, g].reshape(cg, cg)
            mixed = jnp.dot(pooled.astype(BF16), pw, preferred_element_type=F32)
            ys_ref[0, :, gs] = (mixed * ps_ref[:, gs] * _silu(tile(1, g))).astype(BF16)
            v = tile(4, g) * tile(2, g)
            vext = jnp.concatenate([prev(4, g) * prev(2, g), v], axis=0)
            v1 = _shift_rows(vext, 1, True)[HALO:]
            v2 = _shift_rows(vext, 2, True)[HALO:]
            cw = [cw_ref[g, tap:tap + 1, :] for tap in range(CONV_K)]
            y = cb_ref[:, gs] + cw[0] * v2 + cw[1] * v1 + cw[2] * v
            ys_ref[1, :, gs] = (tile(3, g) * y * _silu(tile(5, g))).astype(BF16)

    return pl.pallas_call(
        body, name="mix_fwd", grid=(s // ts,),
        in_specs=[pl.BlockSpec((ts, cols), lambda i: (i, 0)),
                  pl.BlockSpec((HALO, cols), lambda i: (jnp.maximum(i * hb - 1, 0), 0)),
                  pl.BlockSpec(pool_wg.shape, lambda i: (0, 0, 0, 0)),
                  pl.BlockSpec((1, w), lambda i: (0, 0)),
                  pl.BlockSpec(conv_wg.shape, lambda i: (0, 0, 0)),
                  pl.BlockSpec((1, w), lambda i: (0, 0))],
        out_specs=pl.BlockSpec((2, ts, w), lambda i: (0, i, 0)),
        out_shape=jax.ShapeDtypeStruct((2, s, w), BF16),
        compiler_params=_params(("parallel",), (ts + HALO) * cols * 2 + 2 * ts * w * 2
                                + _nbytes(pool_wg.shape, BF16), 24 * (ts + HALO) * cg * 4),
    )(proj, proj, pool_wg, pool_scale, conv_wg, conv_b)


def _mix_bwd(proj, dys, pool_wg, pool_scale, conv_wg, conv_b, width):
    s = proj.shape[0]
    w = width
    cg = w // N_POOL_GROUPS
    ts = _pick(s, 128, HALO)
    hb = ts // HALO
    n_tiles = s // ts
    last_hb = s // HALO - 1
    cols = 6 * w

    def body(p_ref, ph_ref, pn_ref, dy_ref, dyn_ref, pw_ref, ps_ref, cw_ref, cb_ref,
             dp_ref, dpw_ref, dps_ref, dcw_ref, dcb_ref):
        i = pl.program_id(0)
        first = i == 0
        last = i == n_tiles - 1

        @pl.when(first)
        def _():
            dpw_ref[...] = jnp.zeros(dpw_ref.shape, F32)
            dps_ref[...] = jnp.zeros(dps_ref.shape, F32)
            dcw_ref[...] = jnp.zeros(dcw_ref.shape, F32)
            dcb_ref[...] = jnp.zeros(dcb_ref.shape, F32)

        row = i * ts + lax.broadcasted_iota(jnp.int32, (ts + HALO, 1), 0)
        t1_ext = (row + 1).astype(F32)
        t1 = t1_ext[:ts]

        def tile(part, g):
            lo = part * w + g * cg
            return p_ref[:, lo:lo + cg].astype(F32)

        def prev(part, g):
            lo = part * w + g * cg
            return jnp.where(first, 0.0, ph_ref[:, lo:lo + cg].astype(F32))

        def ahead(part, g):
            lo = part * w + g * cg
            return jnp.concatenate([tile(part, g), pn_ref[:, lo:lo + cg].astype(F32)], axis=0)

        def dy_ahead(n, g):
            gs = slice(g * cg, (g + 1) * cg)
            nxt = jnp.where(last, 0.0, dyn_ref[n, :, gs])
            return jnp.concatenate([dy_ref[n, :, gs], nxt], axis=0)

        for g, win in enumerate(POOL_WINDOWS):
            gs = slice(g * cg, (g + 1) * cg)
            u = tile(0, g)
            ext = jnp.concatenate([prev(0, g), u], axis=0)
            pooled = _window_sum(ext, win, True)[HALO:] / jnp.minimum(t1, float(win)) - u
            pooled_b = pooled.astype(BF16)
            pw = pw_ref[:, g].reshape(cg, cg)
            mixed = jnp.dot(pooled_b, pw, preferred_element_type=F32)
            zp_ext = ahead(1, g)
            dy0_ext = dy_ahead(0, g)
            scale = ps_ref[:, gs]
            dms_ext = dy0_ext * _silu(zp_ext)
            dmix_b = (dms_ext * scale).astype(BF16)
            dpooled_ext = lax.dot_general(dmix_b, pw, NT, preferred_element_type=F32)
            dy0 = dy0_ext[:ts]
            zp = zp_ext[:ts]
            dp_ref[:, w + g * cg:w + (g + 1) * cg] = (dy0 * mixed * scale * _dsilu(zp)).astype(BF16)
            dps_ref[:, gs] += jnp.sum(dms_ext[:ts] * mixed, axis=0, keepdims=True)
            dpw = lax.dot_general(pooled_b, dmix_b[:ts], TN, preferred_element_type=F32)
            dpw_ref[g // 2, :, g % 2] += dpw.reshape(N_CHIPS, cg // N_CHIPS, cg)
            q_ext = dpooled_ext / jnp.minimum(t1_ext, float(win))
            du = _window_sum(q_ext, win, False)[:ts] - dpooled_ext[:ts]
            dp_ref[:, gs] = du.astype(BF16)
            uc = tile(2, g)
            cc = tile(4, g)
            v = cc * uc
            vext = jnp.concatenate([prev(4, g) * prev(2, g), v], axis=0)
            v1 = _shift_rows(vext, 1, True)[HALO:]
            v2 = _shift_rows(vext, 2, True)[HALO:]
            cw = [cw_ref[g, tap:tap + 1, :] for tap in range(CONV_K)]
            y = cb_ref[:, gs] + cw[0] * v2 + cw[1] * v1 + cw[2] * v
            bc_ext = ahead(3, g)
            zc_ext = ahead(5, g)
            dy1_ext = dy_ahead(1, g)
            dyy_ext = dy1_ext * bc_ext * _silu(zc_ext)
            dy1 = dy1_ext[:ts]
            bc = bc_ext[:ts]
            zc = zc_ext[:ts]
            dp_ref[:, 3 * w + g * cg:3 * w + (g + 1) * cg] = (dy1 * y * _silu(zc)).astype(BF16)
            dp_ref[:, 5 * w + g * cg:5 * w + (g + 1) * cg] = (dy1 * bc * y * _dsilu(zc)).astype(BF16)
            dyy = dyy_ext[:ts]
            dcb_ref[:, gs] += jnp.sum(dyy, axis=0, keepdims=True)
            for tap, vt in enumerate((v2, v1, v)):
                dcw_ref[g, tap:tap + 1, :] += jnp.sum(dyy * vt, axis=0, keepdims=True)
            dv = (cw[2] * dyy + cw[1] * _shift_rows(dyy_ext, 1, False)[:ts]
                  + cw[0] * _shift_rows(dyy_ext, 2, False)[:ts])
            dp_ref[:, 4 * w + g * cg:4 * w + (g + 1) * cg] = (dv * uc).astype(BF16)
            dp_ref[:, 2 * w + g * cg:2 * w + (g + 1) * cg] = (dv * cc).astype(BF16)

    dpw_shape = (2, N_CHIPS, 2, cg // N_CHIPS, cg)
    return pl.pallas_call(
        body, name="mix_bwd", grid=(n_tiles,),
        in_specs=[pl.BlockSpec((ts, cols), lambda i: (i, 0)),
                  pl.BlockSpec((HALO, cols), lambda i: (jnp.maximum(i * hb - 1, 0), 0)),
                  pl.BlockSpec((HALO, cols), lambda i: (jnp.minimum((i + 1) * hb, last_hb), 0)),
                  pl.BlockSpec((2, ts, w), lambda i: (0, i, 0)),
                  pl.BlockSpec((2, HALO, w), lambda i: (0, jnp.minimum((i + 1) * hb, last_hb), 0)),
                  pl.BlockSpec(pool_wg.shape, lambda i: (0, 0, 0, 0)),
                  pl.BlockSpec((1, w), lambda i: (0, 0)),
                  pl.BlockSpec(conv_wg.shape, lambda i: (0, 0, 0)),
                  pl.BlockSpec((1, w), lambda i: (0, 0))],
        out_specs=[pl.BlockSpec((ts, cols), lambda i: (i, 0)),
                   pl.BlockSpec(dpw_shape, lambda i: (0, 0, 0, 0, 0)),
                   pl.BlockSpec((1, w), lambda i: (0, 0)),
                   pl.BlockSpec(conv_wg.shape, lambda i: (0, 0, 0)),
                   pl.BlockSpec((1, w), lambda i: (0, 0))],
        out_shape=[jax.ShapeDtypeStruct((s, cols), BF16), jax.ShapeDtypeStruct(dpw_shape, F32),
                   jax.ShapeDtypeStruct((1, w), F32), jax.ShapeDtypeStruct(conv_wg.shape, F32),
                   jax.ShapeDtypeStruct((1, w), F32)],
        compiler_params=_params(("arbitrary",), (2 * ts + 2 * HALO) * cols * 2 + (ts + HALO) * w * 8
                                + _nbytes(pool_wg.shape, BF16) + _nbytes(dpw_shape, F32),
                                40 * (ts + HALO) * cg * 4),
    )(proj, proj, proj, dys, dys, pool_wg, pool_scale, conv_wg, conv_b)


def _adamw(name, g, w, m, v):
    r, c = w.shape
    tr = _pick(r, max(8, (1 << 20) // (4 * c)), 8)

    def body(g_ref, w_ref, m_ref, v_ref, go_ref, d_ref, mo_ref, vo_ref):
        gv = g_ref[...]
        mn = ADAM_B1 * m_ref[...] + (1.0 - ADAM_B1) * gv
        vn = ADAM_B2 * v_ref[...] + (1.0 - ADAM_B2) * (gv * gv)
        m_hat = mn / (1.0 - ADAM_B1 ** ADAM_STEP)
        v_hat = vn / (1.0 - ADAM_B2 ** ADAM_STEP)
        go_ref[...] = gv
        d_ref[...] = -ADAM_LR * (m_hat / (jnp.sqrt(v_hat) + ADAM_EPS) + ADAM_WD * w_ref[...])
        mo_ref[...] = mn
        vo_ref[...] = vn

    blk = pl.BlockSpec((tr, c), lambda i: (i, 0))
    sh = jax.ShapeDtypeStruct((r, c), F32)
    return pl.pallas_call(
        body, name=name, grid=(r // tr,), in_specs=[blk] * 4, out_specs=[blk] * 4,
        out_shape=[sh] * 4, compiler_params=_params(("parallel",), tr * c * 32, 4 * tr * c * 4),
    )(g, w, m, v)


def _pair_add(name, g, r1, c_idx):
    _, r, c = g.shape
    tr = _pick(r, max(16, (2 << 20) // (2 * c)), 16)

    def body(c_ref, g_ref, r_ref, o_ref):
        o_ref[...] = (g_ref[...].astype(F32) + r_ref[...].astype(F32)).astype(BF16)

    return pl.pallas_call(
        body, name=name,
        grid_spec=pltpu.PrefetchScalarGridSpec(
            num_scalar_prefetch=1, grid=(r // tr,),
            in_specs=[pl.BlockSpec((None, tr, c), lambda i, cr: (cr[0], i, 0)),
                      pl.BlockSpec((tr, c), lambda i, cr: (i, 0))],
            out_specs=pl.BlockSpec((tr, c), lambda i, cr: (i, 0))),
        out_shape=jax.ShapeDtypeStruct((r, c), BF16),
        compiler_params=_params(("parallel",), tr * c * 6, 3 * tr * c * 4),
    )(c_idx, g, r1)


def _adamw_halves(name, g_own, g_other, c_idx, w, m, v):
    _, r, c = w.shape
    tr = _pick(r, max(8, (1 << 20) // (4 * c)), 8)

    def body(c_ref, go_ref, gt_ref, w_ref, m_ref, v_ref, g_out, d_ref, mo_ref, vo_ref):
        gv = jnp.where(pl.program_id(0) == c_ref[0], go_ref[...], gt_ref[...])
        mn = ADAM_B1 * m_ref[...] + (1.0 - ADAM_B1) * gv
        vn = ADAM_B2 * v_ref[...] + (1.0 - ADAM_B2) * (gv * gv)
        m_hat = mn / (1.0 - ADAM_B1 ** ADAM_STEP)
        v_hat = vn / (1.0 - ADAM_B2 ** ADAM_STEP)
        g_out[...] = gv
        d_ref[...] = -ADAM_LR * (m_hat / (jnp.sqrt(v_hat) + ADAM_EPS) + ADAM_WD * w_ref[...])
        mo_ref[...] = mn
        vo_ref[...] = vn

    blk = pl.BlockSpec((None, tr, c), lambda h, i, cr: (h, i, 0))
    sh = jax.ShapeDtypeStruct(w.shape, F32)
    return pl.pallas_call(
        body, name=name,
        grid_spec=pltpu.PrefetchScalarGridSpec(
            num_scalar_prefetch=1, grid=(2, r // tr),
            in_specs=[pl.BlockSpec((tr, c), lambda h, i, cr: (jnp.where(h == cr[0], i, 0), 0)),
                      pl.BlockSpec((tr, c), lambda h, i, cr: (jnp.where(h == cr[0], 0, i), 0)),
                      blk, blk, blk],
            out_specs=[blk] * 4),
        out_shape=[sh] * 4,
        compiler_params=_params(("arbitrary", "arbitrary"), tr * c * 36, 4 * tr * c * 4),
    )(c_idx, g_own, g_other, w, m, v)


def _sum_chips(name, part, r2, chip_idx):
    n, r, c = r2.shape
    tr = _pick(r, max(16, (1 << 20) // (2 * c)), 16)

    def body(ch_ref, own_ref, *rest):
        slots, o_ref = rest[:n], rest[n]
        acc = None
        for s in range(n):
            term = jnp.where(ch_ref[0] == s, own_ref[...], slots[s][...]).astype(F32)
            acc = term if acc is None else acc + term
        o_ref[...] = acc

    def slot_spec(s):
        return pl.BlockSpec((None, tr, c), lambda i, ch: (jnp.where(ch[0] == s, (s + 1) % n, s), i, 0))

    return pl.pallas_call(
        body, name=name,
        grid_spec=pltpu.PrefetchScalarGridSpec(
            num_scalar_prefetch=1, grid=(r // tr,),
            in_specs=[pl.BlockSpec((None, tr, c), lambda i, ch: (ch[0], i, 0))]
            + [slot_spec(s) for s in range(n)],
            out_specs=pl.BlockSpec((tr, c), lambda i, ch: (i, 0))),
        out_shape=jax.ShapeDtypeStruct((r, c), F32),
        compiler_params=_params(("parallel",), tr * c * (2 * n + 6), 3 * tr * c * 4),
    )(chip_idx, part, *([r2] * n))


def _sum_devices(packs):
    n, r, c = packs.shape

    def body(p_ref, o_ref):
        acc = p_ref[0]
        for k in range(1, n):
            acc = acc + p_ref[k]
        o_ref[...] = acc

    return pl.pallas_call(
        body, name="sum_devices", out_shape=jax.ShapeDtypeStruct((r, c), F32),
        in_specs=[pl.BlockSpec(memory_space=pltpu.VMEM)],
        out_specs=pl.BlockSpec(memory_space=pltpu.VMEM),
    )(packs)


def _place():
    x, y, c = lax.axis_index("x"), lax.axis_index("y"), lax.axis_index("c")
    return x, y, c


def _chip_peers(x, y):
    out = []
    for k, (fx, fy) in enumerate(((0, 1), (1, 0), (1, 1))):
        px = 1 - x if fx else x
        py = 1 - y if fy else y
        out.append((k, px, py, 2 * px + py))
    return out


def _gather_weights(big, small):
    nb, ns = len(big), len(small)

    def body(*refs):
        b_in = refs[:nb]
        s_in = refs[nb:nb + ns]
        b_out = refs[nb + ns:2 * nb + ns]
        s_out = refs[2 * nb + ns:2 * (nb + ns)]
        ici_s, ici_r, d2d_s, d2d_r, sm_s, sm_r, own_s, own_r, loc = refs[2 * (nb + ns):]
        x, y, c = _place()
        chip = 2 * x + y
        sibling = (x, y, 1 - c)
        peers = _chip_peers(x, y)

        local = [pltpu.make_async_copy(s_in[t], s_out[t].at[chip], loc.at[t]) for t in range(ns)]
        for cp in local:
            cp.start()
        own = [pltpu.make_async_remote_copy(
            src_ref=b_in[t], dst_ref=b_out[t].at[chip], send_sem=own_s.at[t], recv_sem=own_r.at[t],
            device_id=sibling, device_id_type=MESH) for t in range(nb)]
        for cp in own:
            cp.start()

        def ici(t, k, px, py):
            return pltpu.make_async_remote_copy(
                src_ref=b_in[t].at[c], dst_ref=b_out[t].at[chip, c],
                send_sem=ici_s.at[t, k], recv_sem=ici_r.at[t, k],
                device_id=(px, py, c), device_id_type=MESH)

        def ici_small(t, k, px, py):
            return pltpu.make_async_remote_copy(
                src_ref=s_in[t], dst_ref=s_out[t].at[chip],
                send_sem=sm_s.at[t, k], recv_sem=sm_r.at[t, k],
                device_id=(px, py, c), device_id_type=MESH)

        def landed(t, k, pchip):
            return pltpu.make_async_remote_copy(
                src_ref=b_out[t].at[pchip, c], dst_ref=b_out[t].at[pchip, c],
                send_sem=ici_s.at[t, k], recv_sem=ici_r.at[t, k],
                device_id=sibling, device_id_type=MESH)

        def passed(t, k, pchip, half):
            return pltpu.make_async_remote_copy(
                src_ref=b_out[t].at[pchip, half], dst_ref=b_out[t].at[pchip, half],
                send_sem=d2d_s.at[t, k], recv_sem=d2d_r.at[t, k],
                device_id=sibling, device_id_type=MESH)

        sends = []
        for t in range(nb):
            for (k, px, py, pchip) in peers:
                sends.append(ici(t, k, px, py))
        for t in range(ns):
            for (k, px, py, pchip) in peers:
                sends.append(ici_small(t, k, px, py))
        for cp in sends:
            cp.start()
        for t in range(nb):
            for (k, px, py, pchip) in peers:
                landed(t, k, pchip).wait_recv()
                fwd = passed(t, k, pchip, c)
                fwd.start()
                sends.append(fwd)
        for t in range(ns):
            for (k, px, py, pchip) in peers:
                pltpu.make_async_remote_copy(
                    src_ref=s_in[t], dst_ref=s_out[t].at[pchip],
                    send_sem=sm_s.at[t, k], recv_sem=sm_r.at[t, k],
                    device_id=sibling, device_id_type=MESH).wait_recv()
        for t in range(nb):
            for (k, px, py, pchip) in peers:
                passed(t, k, pchip, 1 - c).wait_recv()
        for cp in sends:
            cp.wait_send()
        for cp in own:
            cp.wait()
        for cp in local:
            cp.wait()

    out_shape = [jax.ShapeDtypeStruct((N_CHIPS,) + b.shape, b.dtype) for b in big]
    out_shape += [jax.ShapeDtypeStruct((N_CHIPS,) + s.shape, s.dtype) for s in small]
    dma = pltpu.SemaphoreType.DMA
    outs = pl.pallas_call(
        body, name="gather_weights", out_shape=out_shape,
        in_specs=[ANY] * (nb + ns), out_specs=[ANY] * (nb + ns),
        scratch_shapes=[dma((nb, 3)), dma((nb, 3)), dma((nb, 3)), dma((nb, 3)),
                        dma((ns, 3)), dma((ns, 3)), dma((nb,)), dma((nb,)), dma((ns,))],
    )(*big, *small)
    return outs[:nb], outs[nb:]


def _pair_exchange(name, grads):
    n = len(grads)

    def body(*refs):
        g_in = refs[:n]
        r_out = refs[n:2 * n]
        send, recv = refs[2 * n:]
        x, y, c = _place()
        cps = [pltpu.make_async_remote_copy(
            src_ref=g_in[t].at[1 - c], dst_ref=r_out[t], send_sem=send.at[t], recv_sem=recv.at[t],
            device_id=(x, y, 1 - c), device_id_type=MESH) for t in range(n)]
        for cp in cps:
            cp.start()
        for cp in cps:
            cp.wait()

    dma = pltpu.SemaphoreType.DMA
    return pl.pallas_call(
        body, name=name,
        out_shape=[jax.ShapeDtypeStruct(g.shape[1:], g.dtype) for g in grads],
        in_specs=[ANY] * n, out_specs=[ANY] * n, scratch_shapes=[dma((n,)), dma((n,))],
    )(*grads)


class _Exchange:
    def __init__(self, operands, out_shape, scratch, start, finish):
        self.operands, self.out_shape, self.scratch = operands, out_shape, scratch
        self.start, self.finish = start, finish


def _scatter_partials(parts):
    n = len(parts)

    def sends(p_in, r_out, sems):
        send, recv = sems
        x, y, c = _place()
        chip = 2 * x + y
        return [pltpu.make_async_remote_copy(
            src_ref=p_in[t].at[pchip], dst_ref=r_out[t].at[chip],
            send_sem=send.at[t, k], recv_sem=recv.at[t, k],
            device_id=(px, py, c), device_id_type=MESH)
            for t in range(n) for (k, px, py, pchip) in _chip_peers(x, y)]

    def start(p_in, r_out, sems):
        for cp in sends(p_in, r_out, sems):
            cp.start()

    def finish(p_in, r_out, sems):
        send, recv = sems
        x, y, c = _place()
        for t in range(n):
            for (k, px, py, pchip) in _chip_peers(x, y):
                pltpu.make_async_remote_copy(
                    src_ref=p_in[t].at[pchip], dst_ref=r_out[t].at[pchip],
                    send_sem=send.at[t, k], recv_sem=recv.at[t, k],
                    device_id=(px, py, c), device_id_type=MESH).wait_recv()
        for cp in sends(p_in, r_out, sems):
            cp.wait_send()

    dma = pltpu.SemaphoreType.DMA
    return _Exchange(list(parts), [jax.ShapeDtypeStruct(p.shape, p.dtype) for p in parts],
                     [dma((n, 3)), dma((n, 3))], start, finish)


def _pair_share(halves):
    n = len(halves)

    def body(*refs):
        h_in = refs[:n]
        o_out = refs[n:2 * n]
        send, recv = refs[2 * n:]
        x, y, c = _place()
        cps = [pltpu.make_async_remote_copy(
            src_ref=h_in[t], dst_ref=o_out[t], send_sem=send.at[t], recv_sem=recv.at[t],
            device_id=(x, y, 1 - c), device_id_type=MESH) for t in range(n)]
        for cp in cps:
            cp.start()
        for cp in cps:
            cp.wait()

    dma = pltpu.SemaphoreType.DMA
    return pl.pallas_call(
        body, name="pair_share",
        out_shape=[jax.ShapeDtypeStruct(h.shape, h.dtype) for h in halves],
        in_specs=[ANY] * n, out_specs=[ANY] * n, scratch_shapes=[dma((n,)), dma((n,))],
    )(*halves)


def _gather_packs(pack):
    def body(p_ref, o_ref, send, recv, loc):
        x, y, c = _place()
        me = 4 * x + 2 * y + c
        mine = pltpu.make_async_copy(p_ref, o_ref.at[me], loc)
        mine.start()
        flips = [(fx, fy, fc) for fx in (0, 1) for fy in (0, 1) for fc in (0, 1)][1:]
        cps = []
        for k, (fx, fy, fc) in enumerate(flips):
            peer = (1 - x if fx else x, 1 - y if fy else y, 1 - c if fc else c)
            cps.append(pltpu.make_async_remote_copy(
                src_ref=p_ref, dst_ref=o_ref.at[me], send_sem=send.at[k], recv_sem=recv.at[k],
                device_id=peer, device_id_type=MESH))
        for cp in cps:
            cp.start()
        for k, (fx, fy, fc) in enumerate(flips):
            px, py, pc = (1 - x if fx else x, 1 - y if fy else y, 1 - c if fc else c)
            pltpu.make_async_remote_copy(
                src_ref=p_ref, dst_ref=o_ref.at[4 * px + 2 * py + pc],
                send_sem=send.at[k], recv_sem=recv.at[k],
                device_id=(px, py, pc), device_id_type=MESH).wait_recv()
        for cp in cps:
            cp.wait_send()
        mine.wait()

    dma = pltpu.SemaphoreType.DMA
    return pl.pallas_call(
        body, name="gather_packs", out_shape=jax.ShapeDtypeStruct((N_DEV,) + pack.shape, pack.dtype),
        in_specs=[ANY], out_specs=ANY, scratch_shapes=[dma((7,)), dma((7,)), dma],
    )(pack)


def _flat_pack(pieces):
    flat = jnp.concatenate([p.reshape(-1) for p in pieces])
    pad = (-flat.shape[0]) % (8 * LANES)
    flat = jnp.concatenate([flat, jnp.zeros((pad,), F32)])
    return flat.reshape(-1, LANES)


def _unpack(pack, shapes):
    flat = pack.reshape(-1)
    out, off = [], 0
    for sh in shapes:
        n = 1
        for s in sh:
            n *= s
        out.append(flat[off:off + n].reshape(sh))
        off += n
    return out


def kernel(x, norm_w, w_in, pool_w, pool_scale, conv_w, conv_b, gate_b, w_branch, w_out, final_norm_w, loss_target, m_norm_w, m_w_in, m_pool_w, m_pool_scale, m_conv_w, m_conv_b, m_gate_b, m_w_branch, m_w_out, m_final_norm_w, v_norm_w, v_w_in, v_pool_w, v_pool_scale, v_conv_w, v_conv_b, v_gate_b, v_w_branch, v_w_out, v_final_norm_w):
    _, s, d = x.shape
    w = d // 2
    cg = w // N_POOL_GROUPS
    p = 6 * w + 2 * d
    pq = p // N_CHIPS
    dq = d // N_CHIPS
    assert w_in.shape == (1, d, pq) and w_branch.shape == (1, 2, w, dq) and w_out.shape == (1, dq, d)
    assert pool_w.shape == (1, N_POOL_GROUPS, cg // N_CHIPS, cg) and conv_w.shape == (1, CONV_K, cg)

    x2d = x.reshape(s, d)
    tgt = loss_target.reshape(s, d)
    c_idx = lax.axis_index("c").astype(jnp.int32).reshape(1)
    chip = 2 * lax.axis_index("x") + lax.axis_index("y")

    big_w = [w_in.reshape(d, pq), w_out.reshape(dq, d), w_branch.reshape(2 * w, dq),
             pool_w.reshape(cg, cg)]
    names = ["in", "out", "branch", "pool"]
    big_b = [_cast_bf16("cast_" + nm, a) for nm, a in zip(names, big_w)]
    big_b = [b.reshape(2, b.shape[0] // 2, b.shape[1]) for b in big_b]
    (wg_in, wg_out, wg_br, wg_pool), (wg_cw, wg_gb) = _gather_weights(
        big_b, [conv_w.reshape(CONV_K, cg), gate_b.reshape(2, dq)])
    wg_in = wg_in.reshape(N_CHIPS, d, pq)
    wg_out = wg_out.reshape(d, d)
    wg_pool = wg_pool.reshape(N_CHIPS, N_POOL_GROUPS, cg // N_CHIPS, cg)

    h = _rms_fwd(x2d, norm_w)

    tm = _pick(s, 1024, 16)
    tn_p = _pick(math.gcd(pq, 6 * w, d), 1024, LANES)
    qp = pq // tn_p
    tk_d = _pick(d, 1024, LANES)
    proj = _mm(
        "proj", grid=(s // tm, p // tn_p, d // tk_d), operands=[h, wg_in],
        in_specs=[pl.BlockSpec((tm, tk_d), lambda i, j, k: (i, k)),
                  pl.BlockSpec((None, tk_d, tn_p), lambda i, j, k: (j // qp, k, j % qp))],
        out_shape=[jax.ShapeDtypeStruct((s, p), BF16)],
        out_specs=[pl.BlockSpec((tm, tn_p), lambda i, j, k: (i, j))],
        pairs=[(0, 1, 0, None)], dims=NN, acc_shapes=[(tm, tn_p)],
        epilogue=lambda accs, ins, outs, ids: outs[0].__setitem__(Ellipsis, accs[0].astype(BF16)),
    )[0][0]

    ys = _mix_fwd(proj, wg_pool, pool_scale, wg_cw, conv_b, w)

    tn_d = _pick(dq, 1024, LANES)
    qd = dq // tn_d
    tk_w = _pick(w, 1024, LANES)
    gl0 = (6 * w) // tn_d
    gl1 = (6 * w + d) // tn_d

    def gate_specs(im):
        return [pl.BlockSpec((tm, tn_d), lambda *a: (im(*a)[0], gl0 + im(*a)[1])),
                pl.BlockSpec((tm, tn_d), lambda *a: (im(*a)[0], gl1 + im(*a)[1])),
                pl.BlockSpec((None, 2, tn_d), lambda *a: (im(*a)[1] // qd, 0, im(*a)[1] % qd))]

    def merge_epilogue(accs, ins, outs, ids):
        gb = ins[6][...]
        g0 = _sigmoid(ins[4][...].astype(F32) + gb[0:1])
        g1 = _sigmoid(ins[5][...].astype(F32) + gb[1:2])
        outs[0][...] = (g0 * accs[0] + g1 * accs[1]).astype(BF16)
        outs[1][0] = accs[0].astype(BF16)
        outs[1][1] = accs[1].astype(BF16)

    (merged, br), _ = _mm(
        "branch_merge", grid=(s // tm, d // tn_d, w // tk_w),
        operands=[ys, wg_br, ys, wg_br, proj, proj, wg_gb],
        in_specs=[pl.BlockSpec((None, tm, tk_w), lambda i, j, k: (0, i, k)),
                  pl.BlockSpec((None, None, tk_w, tn_d), lambda i, j, k: (j // qd, 0, k, j % qd)),
                  pl.BlockSpec((None, tm, tk_w), lambda i, j, k: (1, i, k)),
                  pl.BlockSpec((None, None, tk_w, tn_d), lambda i, j, k: (j // qd, 1, k, j % qd)),
                  *gate_specs(lambda i, j, k: (i, j))],
        out_shape=[jax.ShapeDtypeStruct((s, d), BF16), jax.ShapeDtypeStruct((2, s, d), BF16)],
        out_specs=[pl.BlockSpec((tm, tn_d), lambda i, j, k: (i, j)),
                   pl.BlockSpec((2, tm, tn_d), lambda i, j, k: (0, i, j))],
        pairs=[(0, 1, 0, None), (2, 3, 1, None)], dims=NN, acc_shapes=[(tm, tn_d)] * 2,
        epilogue=merge_epilogue, temp_bytes=6 * tm * tn_d * 4,
    )

    tn_f = _pick(d, 1024, LANES)
    o = _mm(
        "out_proj", grid=(s // tm, d // tn_f, d // tk_d), operands=[merged, wg_out],
        in_specs=[pl.BlockSpec((tm, tk_d), lambda i, j, k: (i, k)),
                  pl.BlockSpec((tk_d, tn_f), lambda i, j, k: (k, j))],
        out_shape=[jax.ShapeDtypeStruct((s, d), F32)],
        out_specs=[pl.BlockSpec((tm, tn_f), lambda i, j, k: (i, j))],
        pairs=[(0, 1, 0, None)], dims=NN, acc_shapes=[(tm, tn_f)],
        epilogue=lambda accs, ins, outs, ids: outs[0].__setitem__(Ellipsis, accs[0]),
    )[0][0]

    dx2, dx2b, loss_part, g_fnw =_head(x2d, o, tgt, final_norm_w.reshape(1, d))

    def gate_bwd_epilogue(accs, ins, outs, ids):
        dm = accs[0]
        gb = ins[5][...]
        i = ids[1]

        @pl.when(i == 0)
        def _():
            outs[2][...] = jnp.zeros(outs[2].shape, F32)

        for n in range(2):
            gate = _sigmoid(ins[3 + n][...].astype(F32) + gb[n:n + 1])
            outs[0][n] = (dm * gate).astype(BF16)
            dgl = dm * ins[2][n].astype(F32) * gate * (1.0 - gate)
            outs[1][n] = dgl.astype(BF16)
            outs[2][n:n + 1, :] += jnp.sum(dgl, axis=0, keepdims=True)

    (d_br, dgl, g_gb), _ = _mm(
        "out_proj_bwd_gate", grid=(d // tn_d, s // tm, d // tk_d),
        operands=[dx2b, wg_out, br, proj, proj, wg_gb],
        in_specs=[pl.BlockSpec((tm, tk_d), lambda j, i, k: (i, k)),
                  pl.BlockSpec((tn_d, tk_d), lambda j, i, k: (j, k)),
                  pl.BlockSpec((2, tm, tn_d), lambda j, i, k: (0, i, j)),
                  *gate_specs(lambda j, i, k: (i, j))],
        out_shape=[jax.ShapeDtypeStruct((2, s, d), BF16), jax.ShapeDtypeStruct((2, s, d), BF16),
                   jax.ShapeDtypeStruct((2, d), F32)],
        out_specs=[pl.BlockSpec((2, tm, tn_d), lambda j, i, k: (0, i, j)),
                   pl.BlockSpec((2, tm, tn_d), lambda j, i, k: (0, i, j)),
                   pl.BlockSpec((2, tn_d), lambda j, i, k: (0, j))],
        pairs=[(0, 1, 0, None)], dims=NT, acc_shapes=[(tm, tn_d)],
        epilogue=gate_bwd_epilogue, semantics=("parallel", "arbitrary", "arbitrary"),
        temp_bytes=8 * tm * tn_d * 4,
    )

    hh_out = d // 8
    tm_o = _pick(hh_out, 512, LANES)
    nb_o = hh_out // tm_o
    tk_s = _pick(s, 1024, LANES)
    g_out = _mm(
        "w_out_grad", grid=(d // tm_o, d // tn_f, s // tk_s), operands=[merged, dx2b],
        in_specs=[pl.BlockSpec((tk_s, tm_o), lambda i, j, k: (k, i)),
                  pl.BlockSpec((tk_s, tn_f), lambda i, j, k: (k, j))],
        out_shape=[jax.ShapeDtypeStruct((2, N_CHIPS, hh_out, d), BF16)],
        out_specs=[pl.BlockSpec((None, None, tm_o, tn_f),
                                lambda i, j, k: ((i // nb_o) % 2, i // (2 * nb_o), i % nb_o, j))],
        pairs=[(0, 1, 0, None)], dims=TN, acc_shapes=[(tm_o, tn_f)],
        epilogue=lambda accs, ins, outs, ids: outs[0].__setitem__(Ellipsis, accs[0].astype(BF16)),
    )[0][0]

    tn_w = _pick(w, 1024, LANES)
    tk_q = _pick(dq, 1024, LANES)
    qk = dq // tk_q
    dys = _mm(
        "branch_bwd", grid=(2, s // tm, w // tn_w, d // tk_q), operands=[d_br, wg_br],
        in_specs=[pl.BlockSpec((None, tm, tk_q), lambda n, i, j, k: (n, i, k)),
                  pl.BlockSpec((None, None, tn_w, tk_q), lambda n, i, j, k: (k // qk, n, j, k % qk))],
        out_shape=[jax.ShapeDtypeStruct((2, s, w), F32)],
        out_specs=[pl.BlockSpec((None, tm, tn_w), lambda n, i, j, k: (n, i, j))],
        pairs=[(0, 1, 0, None)], dims=NT, acc_shapes=[(tm, tn_w)],
        epilogue=lambda accs, ins, outs, ids: outs[0].__setitem__(Ellipsis, accs[0]),
    )[0][0]

    tm_w = _pick(w, 1024, LANES)
    g_br = _mm(
        "w_branch_grad", grid=(2, w // tm_w, d // tn_d, s // tk_s), operands=[ys, d_br],
        in_specs=[pl.BlockSpec((None, tk_s, tm_w), lambda n, i, j, k: (n, k, i)),
                  pl.BlockSpec((None, tk_s, tn_d), lambda n, i, j, k: (n, k, j))],
        out_shape=[jax.ShapeDtypeStruct((2, N_CHIPS, w, dq), BF16)],
        out_specs=[pl.BlockSpec((None, None, tm_w, tn_d), lambda n, i, j, k: (n, j // qd, i, j % qd))],
        pairs=[(0, 1, 0, None)], dims=TN, acc_shapes=[(tm_w, tn_d)],
        epilogue=lambda accs, ins, outs, ids: outs[0].__setitem__(Ellipsis, accs[0].astype(BF16)),
    )[0][0]

    dpa, g_pool, g_ps, g_cw, g_cb = _mix_bwd(proj, dys, wg_pool, pool_scale, wg_cw, conv_b, w)
    g_pool = g_pool.astype(BF16).reshape(2, N_CHIPS, 2 * (cg // N_CHIPS), cg)

    def chip_partials(nms, grads):
        recv = _pair_exchange("pair_exchange_" + nms[0], grads)
        parts = []
        for nm, g, r1 in zip(nms, grads, recv):
            _, nc, hh, wd = g.shape
            pt = _pair_add("pair_add_" + nm, g.reshape(2, nc * hh, wd), r1.reshape(nc * hh, wd), c_idx)
            parts.append(pt.reshape(nc, hh, wd))
        return parts

    parts_early = chip_partials(names[1:], [g_out, g_br, g_pool])

    na = (6 * w) // tn_p
    qg = d // tn_p
    hh_in = d // 2
    tm_i = _pick(hh_in, 1024, LANES)
    nb_i = hh_in // tm_i
    (g_in,), recv_early = _mm(
        "w_in_grad", grid=(d // tm_i, p // tn_p, s // tk_s), operands=[h, dpa, dgl],
        comm=_scatter_partials(parts_early),
        in_specs=[pl.BlockSpec((tk_s, tm_i), lambda i, j, k: (k, i)),
                  pl.BlockSpec((tk_s, tn_p), lambda i, j, k: (k, jnp.minimum(j, na - 1))),
                  pl.BlockSpec((None, tk_s, tn_p),
                               lambda i, j, k: (jnp.maximum(j - na, 0) // qg, k, jnp.maximum(j - na, 0) % qg))],
        out_shape=[jax.ShapeDtypeStruct((2, N_CHIPS, hh_in, pq), BF16)],
        out_specs=[pl.BlockSpec((None, None, tm_i, tn_p),
                                lambda i, j, k: (i // nb_i, j // qp, i % nb_i, j % qp))],
        pairs=[(0, 1, 0, lambda ids: ids[1] < na), (0, 2, 0, lambda ids: ids[1] >= na)],
        dims=TN, acc_shapes=[(tm_i, tn_p)],
        epilogue=lambda accs, ins, outs, ids: outs[0].__setitem__(Ellipsis, accs[0].astype(BF16)),
    )

    parts_in = chip_partials(names[:1], [g_in])
    (dh,), recv_in = _mm(
        "proj_bwd", grid=(s // tm, d // tn_f, p // tn_p), operands=[dpa, dgl, wg_in],
        comm=_scatter_partials(parts_in),
        in_specs=[pl.BlockSpec((tm, tn_p), lambda i, j, k: (i, jnp.minimum(k, na - 1))),
                  pl.BlockSpec((None, tm, tn_p),
                               lambda i, j, k: (jnp.maximum(k - na, 0) // qg, i, jnp.maximum(k - na, 0) % qg)),
                  pl.BlockSpec((None, tn_f, tn_p), lambda i, j, k: (k // qp, j, k % qp))],
        out_shape=[jax.ShapeDtypeStruct((s, d), F32)],
        out_specs=[pl.BlockSpec((tm, tn_f), lambda i, j, k: (i, j))],
        pairs=[(0, 2, 0, lambda ids: ids[2] < na), (1, 2, 0, lambda ids: ids[2] >= na)],
        dims=NT, acc_shapes=[(tm, tn_f)],
        epilogue=lambda accs, ins, outs, ids: outs[0].__setitem__(Ellipsis, accs[0]),
    )

    grad_x, g_nw = _rms_bwd(x2d, dh, dx2, norm_w)

    chip_idx = chip.astype(jnp.int32).reshape(1)
    halves = [_sum_chips("sum_chips_" + nm, pt, r2, chip_idx)
              for nm, pt, r2 in zip(names, parts_in + parts_early, recv_in + recv_early)]
    others = _pair_share(halves)

    g_cw_full = jnp.transpose(g_cw, (1, 0, 2)).reshape(CONV_K, w)
    small_shapes = [(LANES,), (1, d), (1, d), (1, w), (1, w), (CONV_K, w), (2, d)]
    pack = _flat_pack([loss_part[0], g_nw, g_fnw, g_ps, g_cb, g_cw_full, g_gb])
    total = _sum_devices(_gather_packs(pack))
    t_loss, t_nw, t_fnw, t_ps, t_cb, t_cw, t_gb = _unpack(total, small_shapes)
    loss = t_loss[0]
    t_cw = lax.dynamic_slice_in_dim(t_cw, chip * cg, cg, axis=1)
    t_gb = lax.dynamic_slice_in_dim(t_gb, chip * dq, dq, axis=1)

    out = {}
    big_names = ["w_in", "w_out", "w_branch", "pool_w"]
    big_m = [m_w_in, m_w_out, m_w_branch, m_pool_w]
    big_v = [v_w_in, v_w_out, v_w_branch, v_pool_w]
    big_orig = [w_in, w_out, w_branch, pool_w]
    for nm, g_own, g_other, w2, mm_, vv_, orig in zip(big_names, halves, others, big_w, big_m, big_v, big_orig):
        sh = (2,) + g_own.shape
        res = _adamw_halves("adamw_" + nm, g_own, g_other, c_idx,
                            w2.reshape(sh), mm_.reshape(sh), vv_.reshape(sh))
        out[nm] = [r.reshape(orig.shape) for r in res]

    sm_names = ["norm_w", "final_norm_w", "pool_scale", "conv_b", "conv_w", "gate_b"]
    sm_g = [t_nw, t_fnw, t_ps, t_cb, t_cw, t_gb]
    sm_w = [norm_w, final_norm_w, pool_scale, conv_b, conv_w, gate_b]
    sm_m = [m_norm_w, m_final_norm_w, m_pool_scale, m_conv_b, m_conv_w, m_gate_b]
    sm_v = [v_norm_w, v_final_norm_w, v_pool_scale, v_conv_b, v_conv_w, v_gate_b]
    sm_shapes = [a.shape for a in sm_w]
    res = _adamw("adamw_small", _flat_pack(sm_g), _flat_pack(sm_w), _flat_pack(sm_m), _flat_pack(sm_v))
    res = [_unpack(r, sm_shapes) for r in res]
    for idx, nm in enumerate(sm_names):
        out[nm] = [r[idx] for r in res]

    order = ["norm_w", "w_in", "pool_w", "pool_scale", "conv_w", "conv_b", "gate_b", "w_branch", "w_out",
             "final_norm_w"]
    outs = [loss, grad_x.reshape(x.shape)]
    for kind in range(4):
        outs += [out[nm][kind] for nm in order]
    return tuple(outs)
```

```python
import math

import jax
import jax.numpy as jnp
from jax import lax
from jax.experimental import pallas as pl
from jax.experimental.pallas import tpu as pltpu

F32 = jnp.float32
BF16 = jnp.bfloat16

NORM_EPS = 1e-6
POOL_WINDOWS = (2, 4, 8, 16)
N_POOL_GROUPS = len(POOL_WINDOWS)
CONV_K = 3
ADAM_LR = 0.001
ADAM_B1 = 0.9
ADAM_B2 = 0.999
ADAM_EPS = 1e-08
ADAM_WD = 0.01
ADAM_STEP = 10

N_CHIPS = 4
N_DEV = 8
HALO = 16
LANES = 128
V7X_VMEM_BYTES = 64 * 1024 * 1024
VMEM_CAP = V7X_VMEM_BYTES - 8 * 1024 * 1024

MESH = pl.DeviceIdType.MESH
ANY = pl.BlockSpec(memory_space=pl.ANY)

NN = (((1,), (0,)), ((), ()))
NT = (((1,), (1,)), ((), ()))
TN = (((0,), (0,)), ((), ()))


def _pick(dim, pref, align):
    if dim <= pref:
        return dim
    t = (pref // align) * align
    while t >= align:
        if dim % t == 0:
            return t
        t -= align
    raise ValueError(f"no tile for {dim} (pref {pref}, align {align})")


def _nbytes(shape, dtype):
    n = 1
    for s in shape:
        if s is not None:
            n *= s
    return n * jnp.dtype(dtype).itemsize


def _params(semantics, block_bytes, extra_bytes=0):
    need = 2 * block_bytes + extra_bytes + (2 << 20)
    return pltpu.CompilerParams(dimension_semantics=semantics,
                                vmem_limit_bytes=int(min(max(need, 16 << 20), VMEM_CAP)))


def _sigmoid(z):
    return jax.nn.sigmoid(z)


def _mm(name, *, grid, operands, in_specs, out_shape, out_specs, pairs, dims, acc_shapes, epilogue,
        semantics=None, temp_bytes=0, comm=None, prefetch=None, aliases=None):
    aliases = dict(aliases or {})
    n_in, n_out = len(operands), len(out_shape)
    kax = len(grid) - 1
    nk = grid[kax]
    single = nk == 1
    conditional = any(p[3] is not None for p in pairs)
    if single and conditional:
        assert len(acc_shapes) == 1 and all(p[3] is not None for p in pairs)
    n_acc = 0 if single else len(acc_shapes)
    n_pf = 0 if prefetch is None else 1
    c_ops = list(comm.operands) if comm else []
    c_out = list(comm.out_shape) if comm else []
    c_sems = list(comm.scratch) if comm else []
    c_alias = dict(comm.aliases) if comm else {}
    n_ci, n_co = len(c_ops), len(c_out)

    def product(ins, ai, bi):
        return lax.dot_general(ins[ai][...], ins[bi][...], dims, preferred_element_type=F32)

    def body(*refs):
        refs = refs[n_pf:]
        ins = refs[:n_in]
        c_in_refs = refs[n_in:n_in + n_ci]
        outs = refs[n_in + n_ci:n_in + n_ci + n_out]
        c_out_refs = refs[n_in + n_ci + n_out:n_in + n_ci + n_out + n_co]
        scratch = refs[n_in + n_ci + n_out + n_co:]
        accs = scratch[:n_acc]
        sems = scratch[n_acc:]
        ids = [pl.program_id(a) for a in range(len(grid))]
        k = ids[kax]

        if comm:
            is_first = ids[0] == 0
            is_last = ids[0] == grid[0] - 1
            for a in range(1, len(grid)):
                is_first = jnp.logical_and(is_first, ids[a] == 0)
                is_last = jnp.logical_and(is_last, ids[a] == grid[a] - 1)

            @pl.when(is_first)
            def _():
                comm.start(c_in_refs, c_out_refs, sems)

        if single and conditional:
            for (ai, bi, ci, cond) in pairs:
                def only(ai=ai, bi=bi):
                    epilogue([product(ins, ai, bi)], ins, outs, ids)
                pl.when(cond(ids))(only)
        elif single:
            vals = [None] * len(acc_shapes)
            for (ai, bi, ci, cond) in pairs:
                r = product(ins, ai, bi)
                vals[ci] = r if vals[ci] is None else vals[ci] + r
            epilogue(vals, ins, outs, ids)
        else:
            @pl.when(k == 0)
            def _():
                for a in accs:
                    a[...] = jnp.zeros(a.shape, a.dtype)

            for (ai, bi, ci, cond) in pairs:
                def step(ai=ai, bi=bi, ci=ci):
                    accs[ci][...] += product(ins, ai, bi)
                if cond is None:
                    step()
                else:
                    pl.when(cond(ids))(step)

            @pl.when(k == nk - 1)
            def _():
                epilogue([a[...] for a in accs], ins, outs, ids)

        if comm:
            @pl.when(is_last)
            def _():
                comm.finish(c_in_refs, c_out_refs, sems)

    if semantics is None:
        semantics = ("parallel",) * kax + ("arbitrary",)
    if comm:
        semantics = ("arbitrary",) * len(grid)
    in_specs = [ANY if idx in aliases else spec for idx, spec in enumerate(in_specs)]
    blk = 0
    for idx, (spec, op) in enumerate(zip(in_specs, operands)):
        if idx not in aliases:
            blk += _nbytes(spec.block_shape, op.dtype)
    for spec, o in zip(out_specs, out_shape):
        blk += _nbytes(spec.block_shape, o.dtype)
    acc_bytes = sum(_nbytes(s, F32) for s in acc_shapes)
    io_alias = {n_pf + i: o for i, o in aliases.items()}
    io_alias.update({n_pf + n_in + i: n_out + o for i, o in c_alias.items()})
    all_in = list(in_specs) + [ANY] * n_ci
    all_out = list(out_specs) + [ANY] * n_co
    scratch_shapes = [pltpu.VMEM(s, F32) for s in acc_shapes[:n_acc]] + c_sems
    params = _params(semantics, blk, 3 * acc_bytes + temp_bytes)
    shapes = list(out_shape) + c_out
    if prefetch is None:
        call = pl.pallas_call(
            body, name=name, grid=grid, in_specs=all_in, out_specs=all_out, out_shape=shapes,
            scratch_shapes=scratch_shapes, input_output_aliases=io_alias, compiler_params=params)
        res = call(*operands, *c_ops)
    else:
        call = pl.pallas_call(
            body, name=name, out_shape=shapes, input_output_aliases=io_alias, compiler_params=params,
            grid_spec=pltpu.PrefetchScalarGridSpec(
                num_scalar_prefetch=1, grid=grid, in_specs=all_in, out_specs=all_out,
                scratch_shapes=scratch_shapes))
        res = call(prefetch, *operands, *c_ops)
    return res[:n_out], res[n_out:]


def _cast_bf16(name, w2d):
    r, c = w2d.shape
    tr = _pick(r, max(16, (4 << 20) // (4 * c)), 16)

    def body(w_ref, o_ref):
        o_ref[...] = w_ref[...].astype(BF16)

    return pl.pallas_call(
        body, name=name, grid=(r // tr,),
        in_specs=[pl.BlockSpec((tr, c), lambda i: (i, 0))],
        out_specs=pl.BlockSpec((tr, c), lambda i: (i, 0)),
        out_shape=jax.ShapeDtypeStruct((r, c), BF16),
        compiler_params=_params(("parallel",), tr * c * 6),
    )(w2d)


def _rms_fwd(x, norm_w):
    s, d = x.shape
    ts = _pick(s, 256, 16)

    def body(x_ref, w_ref, h_ref):
        xv = x_ref[...]
        rstd = lax.rsqrt(jnp.mean(xv * xv, axis=-1, keepdims=True) + NORM_EPS)
        h_ref[...] = (xv * rstd * w_ref[...]).astype(BF16)

    return pl.pallas_call(
        body, name="rms_fwd", grid=(s // ts,),
        in_specs=[pl.BlockSpec((ts, d), lambda i: (i, 0)), pl.BlockSpec((1, d), lambda i: (0, 0))],
        out_specs=pl.BlockSpec((ts, d), lambda i: (i, 0)),
        out_shape=jax.ShapeDtypeStruct((s, d), BF16),
        compiler_params=_params(("parallel",), ts * d * 6, 3 * ts * d * 4),
    )(x, norm_w)


def _head(x, o, target, fnw):
    s, d = x.shape
    ts = _pick(s, 128, 16)

    def body(x_ref, o_ref, t_ref, w_ref, dx_ref, dxb_ref, loss_ref, gw_ref):
        i = pl.program_id(0)

        @pl.when(i == 0)
        def _():
            loss_ref[...] = jnp.zeros(loss_ref.shape, F32)
            gw_ref[...] = jnp.zeros(gw_ref.shape, F32)

        w = w_ref[...]
        x2 = x_ref[...] + o_ref[...]
        rstd = lax.rsqrt(jnp.mean(x2 * x2, axis=-1, keepdims=True) + NORM_EPS)
        n = x2 * rstd
        e = n * w - t_ref[...]
        loss_ref[...] += 0.5 * jnp.sum(e * e) / d
        dy = e / d
        gw_ref[...] += jnp.sum(dy * n, axis=0, keepdims=True)
        gy = dy * w
        dx = rstd * (gy - n * jnp.mean(gy * n, axis=-1, keepdims=True))
        dx_ref[...] = dx
        dxb_ref[...] = dx.astype(BF16)

    row = pl.BlockSpec((ts, d), lambda i: (i, 0))
    return pl.pallas_call(
        body, name="head", grid=(s // ts,),
        in_specs=[row, row, row, pl.BlockSpec((1, d), lambda i: (0, 0))],
        out_specs=[row, row, pl.BlockSpec((8, LANES), lambda i: (0, 0)),
                   pl.BlockSpec((1, d), lambda i: (0, 0))],
        out_shape=[jax.ShapeDtypeStruct((s, d), F32), jax.ShapeDtypeStruct((s, d), BF16),
                   jax.ShapeDtypeStruct((8, LANES), F32), jax.ShapeDtypeStruct((1, d), F32)],
        compiler_params=_params(("arbitrary",), ts * d * 22, 6 * ts * d * 4),
    )(x, o, target, fnw)


def _rms_bwd(x, dh, dx2, norm_w):
    s, d = x.shape
    ts = _pick(s, 128, 16)

    def body(x_ref, dh_ref, dx2_ref, w_ref, gx_ref, gw_ref):
        i = pl.program_id(0)

        @pl.when(i == 0)
        def _():
            gw_ref[...] = jnp.zeros(gw_ref.shape, F32)

        xv = x_ref[...]
        rstd = lax.rsqrt(jnp.mean(xv * xv, axis=-1, keepdims=True) + NORM_EPS)
        n = xv * rstd
        dhv = dh_ref[...]
        gw_ref[...] += jnp.sum(dhv * n, axis=0, keepdims=True)
        gh = dhv * w_ref[...]
        gx_ref[...] = dx2_ref[...] + rstd * (gh - n * jnp.mean(gh * n, axis=-1, keepdims=True))

    row = pl.BlockSpec((ts, d), lambda i: (i, 0))
    vec = pl.BlockSpec((1, d), lambda i: (0, 0))
    return pl.pallas_call(
        body, name="rms_bwd", grid=(s // ts,),
        in_specs=[row, row, row, vec], out_specs=[row, vec],
        out_shape=[jax.ShapeDtypeStruct((s, d), F32), jax.ShapeDtypeStruct((1, d), F32)],
        compiler_params=_params(("arbitrary",), ts * d * 16, 5 * ts * d * 4),
    )(x, dh, dx2, norm_w)


def _silu(z):
    return z * _sigmoid(z)


def _dsilu(z):
    sg = _sigmoid(z)
    return sg * (1.0 + z * (1.0 - sg))


def _window_sum(ext, window, back):
    n = ext.shape[0]
    acc = ext
    step = 1
    while step < window:
        acc = acc + pltpu.roll(acc, step if back else n - step, 0)
        step *= 2
    return acc


def _shift_rows(ext, k, back):
    n = ext.shape[0]
    return pltpu.roll(ext, k if back else n - k, 0)


def _mix_fwd(proj, pool_wg, pool_scale, conv_wg, conv_b, width):
    s = proj.shape[0]
    w = width
    cg = w // N_POOL_GROUPS
    ts = _pick(s, 128, HALO)
    hb = ts // HALO
    cols = 6 * w

    def body(p_ref, ph_ref, pw_ref, ps_ref, cw_ref, cb_ref, ys_ref):
        i = pl.program_id(0)
        first = i == 0
        t1 = (i * ts + lax.broadcasted_iota(jnp.int32, (ts, 1), 0) + 1).astype(F32)

        def tile(part, g):
            lo = part * w + g * cg
            return p_ref[:, lo:lo + cg].astype(F32)

        def prev(part, g):
            lo = part * w + g * cg
            return jnp.where(first, 0.0, ph_ref[:, lo:lo + cg].astype(F32))

        for g, win in enumerate(POOL_WINDOWS):
            gs = slice(g * cg, (g + 1) * cg)
            u = tile(0, g)
            ext = jnp.concatenate([prev(0, g), u], axis=0)
            wsum = _window_sum(ext, win, True)[HALO:]
            pooled = wsum / jnp.minimum(t1, float(win)) - u
            pw = pw_ref[:, g].reshape(cg, cg)
            mixed = jnp.dot(pooled.astype(BF16), pw, preferred_element_type=F32)
            ys_ref[0, :, gs] = (mixed * ps_ref[:, gs] * _silu(tile(1, g))).astype(BF16)
            v = tile(4, g) * tile(2, g)
            vext = jnp.concatenate([prev(4, g) * prev(2, g), v], axis=0)
            v1 = _shift_rows(vext, 1, True)[HALO:]
            v2 = _shift_rows(vext, 2, True)[HALO:]
            cw = [cw_ref[g, tap:tap + 1, :] for tap in range(CONV_K)]
            y = cb_ref[:, gs] + cw[0] * v2 + cw[1] * v1 + cw[2] * v
            ys_ref[1, :, gs] = (tile(3, g) * y * _silu(tile(5, g))).astype(BF16)

    return pl.pallas_call(
        body, name="mix_fwd", grid=(s // ts,),
        in_specs=[pl.BlockSpec((ts, cols), lambda i: (i, 0)),
                  pl.BlockSpec((HALO, cols), lambda i: (jnp.maximum(i * hb - 1, 0), 0)),
                  pl.BlockSpec(pool_wg.shape, lambda i: (0, 0, 0, 0)),
                  pl.BlockSpec((1, w), lambda i: (0, 0)),
                  pl.BlockSpec(conv_wg.shape, lambda i: (0, 0, 0)),
                  pl.BlockSpec((1, w), lambda i: (0, 0))],
        out_specs=pl.BlockSpec((2, ts, w), lambda i: (0, i, 0)),
        out_shape=jax.ShapeDtypeStruct((2, s, w), BF16),
        compiler_params=_params(("parallel",), (ts + HALO) * cols * 2 + 2 * ts * w * 2
                                + _nbytes(pool_wg.shape, BF16), 24 * (ts + HALO) * cg * 4),
    )(proj, proj, pool_wg, pool_scale, conv_wg, conv_b)


def _mix_bwd(proj, dys, pool_wg, pool_scale, conv_wg, conv_b, width):
    s = proj.shape[0]
    w = width
    cg = w // N_POOL_GROUPS
    ts = _pick(s, 128, HALO)
    hb = ts // HALO
    n_tiles = s // ts
    last_hb = s // HALO - 1
    cols = 6 * w

    def body(p_ref, ph_ref, pn_ref, dy_ref, dyn_ref, pw_ref, ps_ref, cw_ref, cb_ref,
             dp_ref, dpw_ref, dps_ref, dcw_ref, dcb_ref):
        i = pl.program_id(0)
        first = i == 0
        last = i == n_tiles - 1

        @pl.when(first)
        def _():
            dpw_ref[...] = jnp.zeros(dpw_ref.shape, F32)
            dps_ref[...] = jnp.zeros(dps_ref.shape, F32)
            dcw_ref[...] = jnp.zeros(dcw_ref.shape, F32)
            dcb_ref[...] = jnp.zeros(dcb_ref.shape, F32)

        row = i * ts + lax.broadcasted_iota(jnp.int32, (ts + HALO, 1), 0)
        t1_ext = (row + 1).astype(F32)
        t1 = t1_ext[:ts]

        def tile(part, g):
            lo = part * w + g * cg
            return p_ref[:, lo:lo + cg].astype(F32)

        def prev(part, g):
            lo = part * w + g * cg
            return jnp.where(first, 0.0, ph_ref[:, lo:lo + cg].astype(F32))

        def ahead(part, g):
            lo = part * w + g * cg
            return jnp.concatenate([tile(part, g), pn_ref[:, lo:lo + cg].astype(F32)], axis=0)

        def dy_ahead(n, g):
            gs = slice(g * cg, (g + 1) * cg)
            nxt = jnp.where(last, 0.0, dyn_ref[n, :, gs])
            return jnp.concatenate([dy_ref[n, :, gs], nxt], axis=0)

        for g, win in enumerate(POOL_WINDOWS):
            gs = slice(g * cg, (g + 1) * cg)
            u = tile(0, g)
            ext = jnp.concatenate([prev(0, g), u], axis=0)
            pooled = _window_sum(ext, win, True)[HALO:] / jnp.minimum(t1, float(win)) - u
            pooled_b = pooled.astype(BF16)
            pw = pw_ref[:, g].reshape(cg, cg)
            mixed = jnp.dot(pooled_b, pw, preferred_element_type=F32)
            zp_ext = ahead(1, g)
            dy0_ext = dy_ahead(0, g)
            scale = ps_ref[:, gs]
            dms_ext = dy0_ext * _silu(zp_ext)
            dmix_b = (dms_ext * scale).astype(BF16)
            dpooled_ext = lax.dot_general(dmix_b, pw, NT, preferred_element_type=F32)
            dy0 = dy0_ext[:ts]
            zp = zp_ext[:ts]
            dp_ref[:, w + g * cg:w + (g + 1) * cg] = (dy0 * mixed * scale * _dsilu(zp)).astype(BF16)
            dps_ref[:, gs] += jnp.sum(dms_ext[:ts] * mixed, axis=0, keepdims=True)
            dpw = lax.dot_general(pooled_b, dmix_b[:ts], TN, preferred_element_type=F32)
            dpw_ref[g // 2, :, g % 2] += dpw.reshape(N_CHIPS, cg // N_CHIPS, cg)
            q_ext = dpooled_ext / jnp.minimum(t1_ext, float(win))
            du = _window_sum(q_ext, win, False)[:ts] - dpooled_ext[:ts]
            dp_ref[:, gs] = du.astype(BF16)
            uc = tile(2, g)
            cc = tile(4, g)
            v = cc * uc
            vext = jnp.concatenate([prev(4, g) * prev(2, g), v], axis=0)
            v1 = _shift_rows(vext, 1, True)[HALO:]
            v2 = _shift_rows(vext, 2, True)[HALO:]
            cw = [cw_ref[g, tap:tap + 1, :] for tap in range(CONV_K)]
            y = cb_ref[:, gs] + cw[0] * v2 + cw[1] * v1 + cw[2] * v
            bc_ext = ahead(3, g)
            zc_ext = ahead(5, g)
            dy1_ext = dy_ahead(1, g)
            dyy_ext = dy1_ext * bc_ext * _silu(zc_ext)
            dy1 = dy1_ext[:ts]
            bc = bc_ext[:ts]
            zc = zc_ext[:ts]
            dp_ref[:, 3 * w + g * cg:3 * w + (g + 1) * cg] = (dy1 * y * _silu(zc)).astype(BF16)
            dp_ref[:, 5 * w + g * cg:5 * w + (g + 1) * cg] = (dy1 * bc * y * _dsilu(zc)).astype(BF16)
            dyy = dyy_ext[:ts]
            dcb_ref[:, gs] += jnp.sum(dyy, axis=0, keepdims=True)
            for tap, vt in enumerate((v2, v1, v)):
                dcw_ref[g, tap:tap + 1, :] += jnp.sum(dyy * vt, axis=0, keepdims=True)
            dv = (cw[2] * dyy + cw[1] * _shift_rows(dyy_ext, 1, False)[:ts]
                  + cw[0] * _shift_rows(dyy_ext, 2, False)[:ts])
            dp_ref[:, 4 * w + g * cg:4 * w + (g + 1) * cg] = (dv * uc).astype(BF16)
            dp_ref[:, 2 * w + g * cg:2 * w + (g + 1) * cg] = (dv * cc).astype(BF16)

    dpw_shape = (2, N_CHIPS, 2, cg // N_CHIPS, cg)
    return pl.pallas_call(
        body, name="mix_bwd", grid=(n_tiles,),
        in_specs=[pl.BlockSpec((ts, cols), lambda i: (i, 0)),
                  pl.BlockSpec((HALO, cols), lambda i: (jnp.maximum(i * hb - 1, 0), 0)),
                  pl.BlockSpec((HALO, cols), lambda i: (jnp.minimum((i + 1) * hb, last_hb), 0)),
                  pl.BlockSpec((2, ts, w), lambda i: (0, i, 0)),
                  pl.BlockSpec((2, HALO, w), lambda i: (0, jnp.minimum((i + 1) * hb, last_hb), 0)),
                  pl.BlockSpec(pool_wg.shape, lambda i: (0, 0, 0, 0)),
                  pl.BlockSpec((1, w), lambda i: (0, 0)),
                  pl.BlockSpec(conv_wg.shape, lambda i: (0, 0, 0)),
                  pl.BlockSpec((1, w), lambda i: (0, 0))],
        out_specs=[pl.BlockSpec((ts, cols), lambda i: (i, 0)),
                   pl.BlockSpec(dpw_shape, lambda i: (0, 0, 0, 0, 0)),
                   pl.BlockSpec((1, w), lambda i: (0, 0)),
                   pl.BlockSpec(conv_wg.shape, lambda i: (0, 0, 0)),
                   pl.BlockSpec((1, w), lambda i: (0, 0))],
        out_shape=[jax.ShapeDtypeStruct((s, cols), BF16), jax.ShapeDtypeStruct(dpw_shape, F32),
                   jax.ShapeDtypeStruct((1, w), F32), jax.ShapeDtypeStruct(conv_wg.shape, F32),
                   jax.ShapeDtypeStruct((1, w), F32)],
        compiler_params=_params(("arbitrary",), (2 * ts + 2 * HALO) * cols * 2 + (ts + HALO) * w * 8
                                + _nbytes(pool_wg.shape, BF16) + _nbytes(dpw_shape, F32),
                                40 * (ts + HALO) * cg * 4),
    )(proj, proj, proj, dys, dys, pool_wg, pool_scale, conv_wg, conv_b)


def _adamw(name, g, w, m, v):
    r, c = w.shape
    tr = _pick(r, max(8, (1 << 20) // (4 * c)), 8)

    def body(g_ref, w_ref, m_ref, v_ref, go_ref, d_ref, mo_ref, vo_ref):
        gv = g_ref[...]
        mn = ADAM_B1 * m_ref[...] + (1.0 - ADAM_B1) * gv
        vn = ADAM_B2 * v_ref[...] + (1.0 - ADAM_B2) * (gv * gv)
        m_hat = mn / (1.0 - ADAM_B1 ** ADAM_STEP)
        v_hat = vn / (1.0 - ADAM_B2 ** ADAM_STEP)
        go_ref[...] = gv
        d_ref[...] = -ADAM_LR * (m_hat / (jnp.sqrt(v_hat) + ADAM_EPS) + ADAM_WD * w_ref[...])
        mo_ref[...] = mn
        vo_ref[...] = vn

    blk = pl.BlockSpec((tr, c), lambda i: (i, 0))
    sh = jax.ShapeDtypeStruct((r, c), F32)
    return pl.pallas_call(
        body, name=name, grid=(r // tr,), in_specs=[blk] * 4, out_specs=[blk] * 4,
        out_shape=[sh] * 4, compiler_params=_params(("parallel",), tr * c * 32, 4 * tr * c * 4),
    )(g, w, m, v)


def _pair_add(name, g, r1, c_idx):
    _, r, c = g.shape
    tr = _pick(r, max(16, (2 << 20) // (2 * c)), 16)

    def body(c_ref, g_ref, r_ref, o_ref):
        o_ref[...] = (g_ref[...].astype(F32) + r_ref[...].astype(F32)).astype(BF16)

    return pl.pallas_call(
        body, name=name,
        grid_spec=pltpu.PrefetchScalarGridSpec(
            num_scalar_prefetch=1, grid=(r // tr,),
            in_specs=[pl.BlockSpec((None, tr, c), lambda i, cr: (cr[0], i, 0)),
                      pl.BlockSpec((tr, c), lambda i, cr: (i, 0))],
            out_specs=pl.BlockSpec((tr, c), lambda i, cr: (i, 0))),
        out_shape=jax.ShapeDtypeStruct((r, c), BF16),
        compiler_params=_params(("parallel",), tr * c * 6, 3 * tr * c * 4),
    )(c_idx, g, r1)


def _adamw_halves(name, g_own, g_other, c_idx, w, m, v):
    _, r, c = w.shape
    tr = _pick(r, max(8, (1 << 20) // (4 * c)), 8)

    def body(c_ref, go_ref, gt_ref, w_ref, m_ref, v_ref, g_out, d_ref, mo_ref, vo_ref):
        gv = jnp.where(pl.program_id(0) == c_ref[0], go_ref[...], gt_ref[...])
        mn = ADAM_B1 * m_ref[...] + (1.0 - ADAM_B1) * gv
        vn = ADAM_B2 * v_ref[...] + (1.0 - ADAM_B2) * (gv * gv)
        m_hat = mn / (1.0 - ADAM_B1 ** ADAM_STEP)
        v_hat = vn / (1.0 - ADAM_B2 ** ADAM_STEP)
        g_out[...] = gv
        d_ref[...] = -ADAM_LR * (m_hat / (jnp.sqrt(v_hat) + ADAM_EPS) + ADAM_WD * w_ref[...])
        mo_ref[...] = mn
        vo_ref[...] = vn

    blk = pl.BlockSpec((None, tr, c), lambda h, i, cr: (h, i, 0))
    sh = jax.ShapeDtypeStruct(w.shape, F32)
    return pl.pallas_call(
        body, name=name,
        grid_spec=pltpu.PrefetchScalarGridSpec(
            num_scalar_prefetch=1, grid=(2, r // tr),
            in_specs=[pl.BlockSpec((tr, c), lambda h, i, cr: (jnp.where(h == cr[0], i, 0), 0)),
                      pl.BlockSpec((tr, c), lambda h, i, cr: (jnp.where(h == cr[0], 0, i), 0)),
                      blk, blk, blk],
            out_specs=[blk] * 4),
        out_shape=[sh] * 4,
        compiler_params=_params(("arbitrary", "arbitrary"), tr * c * 36, 4 * tr * c * 4),
    )(c_idx, g_own, g_other, w, m, v)


def _sum_chips(name, part, r2, chip_idx):
    n, r, c = r2.shape
    tr = _pick(r, max(16, (1 << 20) // (2 * c)), 16)

    def body(ch_ref, own_ref, *rest):
        slots, o_ref = rest[:n], rest[n]
        acc = None
        for s in range(n):
            term = jnp.where(ch_ref[0] == s, own_ref[...], slots[s][...]).astype(F32)
            acc = term if acc is None else acc + term
        o_ref[...] = acc

    def slot_spec(s):
        return pl.BlockSpec((None, tr, c), lambda i, ch: (jnp.where(ch[0] == s, (s + 1) % n, s), i, 0))

    return pl.pallas_call(
        body, name=name,
        grid_spec=pltpu.PrefetchScalarGridSpec(
            num_scalar_prefetch=1, grid=(r // tr,),
            in_specs=[pl.BlockSpec((None, tr, c), lambda i, ch: (ch[0], i, 0))]
            + [slot_spec(s) for s in range(n)],
            out_specs=pl.BlockSpec((tr, c), lambda i, ch: (i, 0))),
        out_shape=jax.ShapeDtypeStruct((r, c), F32),
        compiler_params=_params(("parallel",), tr * c * (2 * n + 6), 3 * tr * c * 4),
    )(chip_idx, part, *([r2] * n))


def _sum_devices(packs):
    n, r, c = packs.shape

    def body(p_ref, o_ref):
        acc = p_ref[0]
        for k in range(1, n):
            acc = acc + p_ref[k]
        o_ref[...] = acc

    return pl.pallas_call(
        body, name="sum_devices", out_shape=jax.ShapeDtypeStruct((r, c), F32),
        in_specs=[pl.BlockSpec(memory_space=pltpu.VMEM)],
        out_specs=pl.BlockSpec(memory_space=pltpu.VMEM),
    )(packs)


def _place():
    x, y, c = lax.axis_index("x"), lax.axis_index("y"), lax.axis_index("c")
    return x, y, c


def _chip_peers(x, y):
    out = []
    for k, (fx, fy) in enumerate(((0, 1), (1, 0), (1, 1))):
        px = 1 - x if fx else x
        py = 1 - y if fy else y
        out.append((k, px, py, 2 * px + py))
    return out


def _gather_weights(big, small):
    nb, ns = len(big), len(small)

    class Copies:
        def __init__(self, c_in, c_out, sems):
            b_in, s_in = c_in[:nb], c_in[nb:]
            b_out, s_out = c_out[:nb], c_out[nb:]
            ici_s, ici_r, d2d_s, d2d_r, own_s, own_r = sems[:6]
            x, y, c = _place()
            chip = 2 * x + y
            sibling = (x, y, 1 - c)
            peers = _chip_peers(x, y)
            self.own = [pltpu.make_async_remote_copy(
                src_ref=b_in[t], dst_ref=b_out[t].at[chip], send_sem=own_s.at[t], recv_sem=own_r.at[t],
                device_id=sibling, device_id_type=MESH) for t in range(nb)]
            self.ici = [pltpu.make_async_remote_copy(
                src_ref=b_in[t].at[c], dst_ref=b_out[t].at[chip, c],
                send_sem=ici_s.at[t, k], recv_sem=ici_r.at[t, k],
                device_id=(px, py, c), device_id_type=MESH)
                for t in range(nb) for (k, px, py, pchip) in peers]
            self.landed = [pltpu.make_async_remote_copy(
                src_ref=b_out[t].at[pchip, c], dst_ref=b_out[t].at[pchip, c],
                send_sem=ici_s.at[t, k], recv_sem=ici_r.at[t, k],
                device_id=sibling, device_id_type=MESH)
                for t in range(nb) for (k, px, py, pchip) in peers]
            self.passed = [pltpu.make_async_remote_copy(
                src_ref=b_out[t].at[pchip, c], dst_ref=b_out[t].at[pchip, c],
                send_sem=d2d_s.at[t, k], recv_sem=d2d_r.at[t, k],
                device_id=sibling, device_id_type=MESH)
                for t in range(nb) for (k, px, py, pchip) in peers]
            self.from_sibling = [pltpu.make_async_remote_copy(
                src_ref=b_out[t].at[pchip, 1 - c], dst_ref=b_out[t].at[pchip, 1 - c],
                send_sem=d2d_s.at[t, k], recv_sem=d2d_r.at[t, k],
                device_id=sibling, device_id_type=MESH)
                for t in range(nb) for (k, px, py, pchip) in peers]
            self.small, self.small_landed, self.local = [], [], []
            if ns:
                sm_s, sm_r, loc = sems[6:]
                self.local = [pltpu.make_async_copy(s_in[t], s_out[t].at[chip], loc.at[t]) for t in range(ns)]
                self.small = [pltpu.make_async_remote_copy(
                    src_ref=s_in[t], dst_ref=s_out[t].at[chip],
                    send_sem=sm_s.at[t, k], recv_sem=sm_r.at[t, k],
                    device_id=(px, py, c), device_id_type=MESH)
                    for t in range(ns) for (k, px, py, pchip) in peers]
                self.small_landed = [pltpu.make_async_remote_copy(
                    src_ref=s_in[t], dst_ref=s_out[t].at[pchip],
                    send_sem=sm_s.at[t, k], recv_sem=sm_r.at[t, k],
                    device_id=sibling, device_id_type=MESH)
                    for t in range(ns) for (k, px, py, pchip) in peers]

    def start(c_in, c_out, sems):
        cps = Copies(c_in, c_out, sems)
        for cp in cps.local + cps.own + cps.ici + cps.small:
            cp.start()

    def finish(c_in, c_out, sems):
        cps = Copies(c_in, c_out, sems)
        for arrived, onward in zip(cps.landed, cps.passed):
            arrived.wait_recv()
            onward.start()
        for cp in cps.small_landed + cps.from_sibling:
            cp.wait_recv()
        for cp in cps.ici + cps.small + cps.passed:
            cp.wait_send()
        for cp in cps.own + cps.local:
            cp.wait()

    out_shape = [jax.ShapeDtypeStruct((N_CHIPS,) + b.shape, b.dtype) for b in big]
    out_shape += [jax.ShapeDtypeStruct((N_CHIPS,) + s.shape, s.dtype) for s in small]
    dma = pltpu.SemaphoreType.DMA
    scratch = [dma((nb, 3)), dma((nb, 3)), dma((nb, 3)), dma((nb, 3)), dma((nb,)), dma((nb,))]
    if ns:
        scratch += [dma((ns, 3)), dma((ns, 3)), dma((ns,))]
    return _Exchange(list(big) + list(small), out_shape, scratch, start, finish)


def _pair_exchange(name, grads):
    n = len(grads)

    def body(*refs):
        g_in = refs[:n]
        r_out = refs[n:2 * n]
        send, recv = refs[2 * n:]
        x, y, c = _place()
        cps = [pltpu.make_async_remote_copy(
            src_ref=g_in[t].at[1 - c], dst_ref=r_out[t], send_sem=send.at[t], recv_sem=recv.at[t],
            device_id=(x, y, 1 - c), device_id_type=MESH) for t in range(n)]
        for cp in cps:
            cp.start()
        for cp in cps:
            cp.wait()

    dma = pltpu.SemaphoreType.DMA
    return pl.pallas_call(
        body, name=name,
        out_shape=[jax.ShapeDtypeStruct(g.shape[1:], g.dtype) for g in grads],
        in_specs=[ANY] * n, out_specs=[ANY] * n, scratch_shapes=[dma((n,)), dma((n,))],
    )(*grads)


class _Exchange:
    def __init__(self, operands, out_shape, scratch, start, finish, aliases=None):
        self.operands, self.out_shape, self.scratch = operands, out_shape, scratch
        self.start, self.finish, self.aliases = start, finish, dict(aliases or {})


def _scatter_partials(parts):
    n = len(parts)

    def sends(p_in, r_out, sems):
        send, recv = sems
        x, y, c = _place()
        chip = 2 * x + y
        return [pltpu.make_async_remote_copy(
            src_ref=p_in[t].at[pchip], dst_ref=r_out[t].at[chip],
            send_sem=send.at[t, k], recv_sem=recv.at[t, k],
            device_id=(px, py, c), device_id_type=MESH)
            for t in range(n) for (k, px, py, pchip) in _chip_peers(x, y)]

    def start(p_in, r_out, sems):
        for cp in sends(p_in, r_out, sems):
            cp.start()

    def finish(p_in, r_out, sems):
        send, recv = sems
        x, y, c = _place()
        for t in range(n):
            for (k, px, py, pchip) in _chip_peers(x, y):
                pltpu.make_async_remote_copy(
                    src_ref=p_in[t].at[pchip], dst_ref=r_out[t].at[pchip],
                    send_sem=send.at[t, k], recv_sem=recv.at[t, k],
                    device_id=(px, py, c), device_id_type=MESH).wait_recv()
        for cp in sends(p_in, r_out, sems):
            cp.wait_send()

    dma = pltpu.SemaphoreType.DMA
    return _Exchange(list(parts), [jax.ShapeDtypeStruct(p.shape, p.dtype) for p in parts],
                     [dma((n, 3)), dma((n, 3))], start, finish)


def _pair_share(halves):
    n = len(halves)

    def body(*refs):
        h_in = refs[:n]
        o_out = refs[n:2 * n]
        send, recv = refs[2 * n:]
        x, y, c = _place()
        cps = [pltpu.make_async_remote_copy(
            src_ref=h_in[t], dst_ref=o_out[t], send_sem=send.at[t], recv_sem=recv.at[t],
            device_id=(x, y, 1 - c), device_id_type=MESH) for t in range(n)]
        for cp in cps:
            cp.start()
        for cp in cps:
            cp.wait()

    dma = pltpu.SemaphoreType.DMA
    return pl.pallas_call(
        body, name="pair_share",
        out_shape=[jax.ShapeDtypeStruct(h.shape, h.dtype) for h in halves],
        in_specs=[ANY] * n, out_specs=[ANY] * n, scratch_shapes=[dma((n,)), dma((n,))],
    )(*halves)


def _gather_packs(pack):
    def body(p_ref, o_ref, send, recv, loc):
        x, y, c = _place()
        me = 4 * x + 2 * y + c
        mine = pltpu.make_async_copy(p_ref, o_ref.at[me], loc)
        mine.start()
        flips = [(fx, fy, fc) for fx in (0, 1) for fy in (0, 1) for fc in (0, 1)][1:]
        cps = []
        for k, (fx, fy, fc) in enumerate(flips):
            peer = (1 - x if fx else x, 1 - y if fy else y, 1 - c if fc else c)
            cps.append(pltpu.make_async_remote_copy(
                src_ref=p_ref, dst_ref=o_ref.at[me], send_sem=send.at[k], recv_sem=recv.at[k],
                device_id=peer, device_id_type=MESH))
        for cp in cps:
            cp.start()
        for k, (fx, fy, fc) in enumerate(flips):
            px, py, pc = (1 - x if fx else x, 1 - y if fy else y, 1 - c if fc else c)
            pltpu.make_async_remote_copy(
                src_ref=p_ref, dst_ref=o_ref.at[4 * px + 2 * py + pc],
                send_sem=send.at[k], recv_sem=recv.at[k],
                device_id=(px, py, pc), device_id_type=MESH).wait_recv()
        for cp in cps:
            cp.wait_send()
        mine.wait()

    dma = pltpu.SemaphoreType.DMA
    return pl.pallas_call(
        body, name="gather_packs", out_shape=jax.ShapeDtypeStruct((N_DEV,) + pack.shape, pack.dtype),
        in_specs=[ANY], out_specs=ANY, scratch_shapes=[dma((7,)), dma((7,)), dma],
    )(pack)


def _flat_pack(pieces):
    flat = jnp.concatenate([p.reshape(-1) for p in pieces])
    pad = (-flat.shape[0]) % (8 * LANES)
    flat = jnp.concatenate([flat, jnp.zeros((pad,), F32)])
    return flat.reshape(-1, LANES)


def _unpack(pack, shapes):
    flat = pack.reshape(-1)
    out, off = [], 0
    for sh in shapes:
        n = 1
        for s in sh:
            n *= s
        out.append(flat[off:off + n].reshape(sh))
        off += n
    return out


def kernel(x, norm_w, w_in, pool_w, pool_scale, conv_w, conv_b, gate_b, w_branch, w_out, final_norm_w, loss_target, m_norm_w, m_w_in, m_pool_w, m_pool_scale, m_conv_w, m_conv_b, m_gate_b, m_w_branch, m_w_out, m_final_norm_w, v_norm_w, v_w_in, v_pool_w, v_pool_scale, v_conv_w, v_conv_b, v_gate_b, v_w_branch, v_w_out, v_final_norm_w):
    _, s, d = x.shape
    w = d // 2
    cg = w // N_POOL_GROUPS
    p = 6 * w + 2 * d
    pq = p // N_CHIPS
    dq = d // N_CHIPS
    assert w_in.shape == (1, d, pq) and w_branch.shape == (1, 2, w, dq) and w_out.shape == (1, dq, d)
    assert pool_w.shape == (1, N_POOL_GROUPS, cg // N_CHIPS, cg) and conv_w.shape == (1, CONV_K, cg)

    x2d = x.reshape(s, d)
    tgt = loss_target.reshape(s, d)
    c_idx = lax.axis_index("c").astype(jnp.int32).reshape(1)
    chip = 2 * lax.axis_index("x") + lax.axis_index("y")

    big_w = [w_in.reshape(d, pq), w_out.reshape(dq, d), w_branch.reshape(2 * w, dq),
             pool_w.reshape(cg, cg)]
    names = ["in", "out", "branch", "pool"]
    big_b = [_cast_bf16("cast_" + nm, a) for nm, a in zip(names, big_w)]
    big_b = [b.reshape(2, b.shape[0] // 2, b.shape[1]) for b in big_b]
    chip_idx = chip.astype(jnp.int32).reshape(1)

    h = _rms_fwd(x2d, norm_w)

    tm = _pick(s, 1024, 16)
    tn_p = _pick(math.gcd(pq, 6 * w, d), 1024, LANES)
    qp = pq // tn_p
    tk_d = _pick(d, 4096, LANES)
    cast_out = lambda accs, ins, outs, ids: outs[0].__setitem__(Ellipsis, accs[0].astype(BF16))
    (proj,), (wg_in,) = _mm(
        "proj_own", grid=(s // tm, qp, d // tk_d), operands=[h, big_b[0].reshape(d, pq)],
        in_specs=[pl.BlockSpec((tm, tk_d), lambda i, j, k, ch: (i, k)),
                  pl.BlockSpec((tk_d, tn_p), lambda i, j, k, ch: (k, j))],
        out_shape=[jax.ShapeDtypeStruct((s, p), BF16)],
        out_specs=[pl.BlockSpec((tm, tn_p), lambda i, j, k, ch: (i, ch[0] * qp + j))],
        pairs=[(0, 1, 0, None)], dims=NN, acc_shapes=[(tm, tn_p)], epilogue=cast_out,
        comm=_gather_weights(big_b[:1], []), prefetch=chip_idx,
    )
    wg_in = wg_in.reshape(N_CHIPS, d, pq)
    other = lambda j, ch: jnp.bitwise_xor(ch[0], 1 + j // qp)
    (proj,), (wg_out, wg_br, wg_pool, wg_cw, wg_gb) = _mm(
        "proj_others", grid=(s // tm, 3 * qp, d // tk_d), operands=[h, wg_in, proj],
        in_specs=[pl.BlockSpec((tm, tk_d), lambda i, j, k, ch: (i, k)),
                  pl.BlockSpec((None, tk_d, tn_p), lambda i, j, k, ch: (other(j, ch), k, j % qp)),
                  None],
        out_shape=[jax.ShapeDtypeStruct((s, p), BF16)],
        out_specs=[pl.BlockSpec((tm, tn_p), lambda i, j, k, ch: (i, other(j, ch) * qp + j % qp))],
        pairs=[(0, 1, 0, None)], dims=NN, acc_shapes=[(tm, tn_p)], epilogue=cast_out,
        comm=_gather_weights(big_b[1:], [conv_w.reshape(CONV_K, cg), gate_b.reshape(2, dq)]),
        prefetch=chip_idx, aliases={2: 0},
    )
    wg_out = wg_out.reshape(d, d)
    wg_pool = wg_pool.reshape(N_CHIPS, N_POOL_GROUPS, cg // N_CHIPS, cg)

    ys = _mix_fwd(proj, wg_pool, pool_scale, wg_cw, conv_b, w)

    tn_d = _pick(dq, 1024, LANES)
    qd = dq // tn_d
    tk_w = _pick(w, 2048, LANES)
    tm_g = _pick(s, 512, 16)
    gl0 = (6 * w) // tn_d
    gl1 = (6 * w + d) // tn_d

    def gate_specs(im):
        return [pl.BlockSpec((tm_g, tn_d), lambda *a: (im(*a)[0], gl0 + im(*a)[1])),
                pl.BlockSpec((tm_g, tn_d), lambda *a: (im(*a)[0], gl1 + im(*a)[1])),
                pl.BlockSpec((None, 2, tn_d), lambda *a: (im(*a)[1] // qd, 0, im(*a)[1] % qd))]

    def merge_epilogue(accs, ins, outs, ids):
        gb = ins[6][...]
        g0 = _sigmoid(ins[4][...].astype(F32) + gb[0:1])
        g1 = _sigmoid(ins[5][...].astype(F32) + gb[1:2])
        outs[0][...] = (g0 * accs[0] + g1 * accs[1]).astype(BF16)
        outs[1][0] = accs[0].astype(BF16)
        outs[1][1] = accs[1].astype(BF16)

    (merged, br), _ = _mm(
        "branch_merge", grid=(s // tm_g, d // tn_d, w // tk_w),
        operands=[ys, wg_br, ys, wg_br, proj, proj, wg_gb],
        in_specs=[pl.BlockSpec((None, tm_g, tk_w), lambda i, j, k: (0, i, k)),
                  pl.BlockSpec((None, None, tk_w, tn_d), lambda i, j, k: (j // qd, 0, k, j % qd)),
                  pl.BlockSpec((None, tm_g, tk_w), lambda i, j, k: (1, i, k)),
                  pl.BlockSpec((None, None, tk_w, tn_d), lambda i, j, k: (j // qd, 1, k, j % qd)),
                  *gate_specs(lambda i, j, k: (i, j))],
        out_shape=[jax.ShapeDtypeStruct((s, d), BF16), jax.ShapeDtypeStruct((2, s, d), BF16)],
        out_specs=[pl.BlockSpec((tm_g, tn_d), lambda i, j, k: (i, j)),
                   pl.BlockSpec((2, tm_g, tn_d), lambda i, j, k: (0, i, j))],
        pairs=[(0, 1, 0, None), (2, 3, 1, None)], dims=NN, acc_shapes=[(tm_g, tn_d)] * 2,
        epilogue=merge_epilogue, temp_bytes=6 * tm_g * tn_d * 4,
    )

    tn_f = _pick(d, 1024, LANES)
    o = _mm(
        "out_proj", grid=(s // tm, d // tn_f, d // tk_d), operands=[merged, wg_out],
        in_specs=[pl.BlockSpec((tm, tk_d), lambda i, j, k: (i, k)),
                  pl.BlockSpec((tk_d, tn_f), lambda i, j, k: (k, j))],
        out_shape=[jax.ShapeDtypeStruct((s, d), F32)],
        out_specs=[pl.BlockSpec((tm, tn_f), lambda i, j, k: (i, j))],
        pairs=[(0, 1, 0, None)], dims=NN, acc_shapes=[(tm, tn_f)],
        epilogue=lambda accs, ins, outs, ids: outs[0].__setitem__(Ellipsis, accs[0]),
    )[0][0]

    dx2, dx2b, loss_part, g_fnw =_head(x2d, o, tgt, final_norm_w.reshape(1, d))

    def gate_bwd_epilogue(accs, ins, outs, ids):
        dm = accs[0]
        gb = ins[5][...]
        i = ids[1]

        @pl.when(i == 0)
        def _():
            outs[2][...] = jnp.zeros(outs[2].shape, F32)

        for n in range(2):
            gate = _sigmoid(ins[3 + n][...].astype(F32) + gb[n:n + 1])
            outs[0][n] = (dm * gate).astype(BF16)
            dgl = dm * ins[2][n].astype(F32) * gate * (1.0 - gate)
            outs[1][n] = dgl.astype(BF16)
            outs[2][n:n + 1, :] += jnp.sum(dgl, axis=0, keepdims=True)

    tk_g = _pick(d, 2048, LANES)
    (d_br, dgl, g_gb), _ = _mm(
        "out_proj_bwd_gate", grid=(d // tn_d, s // tm_g, d // tk_g),
        operands=[dx2b, wg_out, br, proj, proj, wg_gb],
        in_specs=[pl.BlockSpec((tm_g, tk_g), lambda j, i, k: (i, k)),
                  pl.BlockSpec((tn_d, tk_g), lambda j, i, k: (j, k)),
                  pl.BlockSpec((2, tm_g, tn_d), lambda j, i, k: (0, i, j)),
                  *gate_specs(lambda j, i, k: (i, j))],
        out_shape=[jax.ShapeDtypeStruct((2, s, d), BF16), jax.ShapeDtypeStruct((2, s, d), BF16),
                   jax.ShapeDtypeStruct((2, d), F32)],
        out_specs=[pl.BlockSpec((2, tm_g, tn_d), lambda j, i, k: (0, i, j)),
                   pl.BlockSpec((2, tm_g, tn_d), lambda j, i, k: (0, i, j)),
                   pl.BlockSpec((2, tn_d), lambda j, i, k: (0, j))],
        pairs=[(0, 1, 0, None)], dims=NT, acc_shapes=[(tm_g, tn_d)],
        epilogue=gate_bwd_epilogue, semantics=("parallel", "arbitrary", "arbitrary"),
        temp_bytes=8 * tm_g * tn_d * 4,
    )

    hh_out = d // 8
    tm_o = _pick(hh_out, 512, LANES)
    nb_o = hh_out // tm_o
    tk_s = _pick(s, 4096, LANES)
    g_out = _mm(
        "w_out_grad", grid=(d // tm_o, d // tn_f, s // tk_s), operands=[merged, dx2b],
        in_specs=[pl.BlockSpec((tk_s, tm_o), lambda i, j, k: (k, i)),
                  pl.BlockSpec((tk_s, tn_f), lambda i, j, k: (k, j))],
        out_shape=[jax.ShapeDtypeStruct((2, N_CHIPS, hh_out, d), BF16)],
        out_specs=[pl.BlockSpec((None, None, tm_o, tn_f),
                                lambda i, j, k: ((i // nb_o) % 2, i // (2 * nb_o), i % nb_o, j))],
        pairs=[(0, 1, 0, None)], dims=TN, acc_shapes=[(tm_o, tn_f)],
        epilogue=lambda accs, ins, outs, ids: outs[0].__setitem__(Ellipsis, accs[0].astype(BF16)),
    )[0][0]

    tn_w = _pick(w, 2048, LANES)
    tk_q = _pick(dq, 1024, LANES)
    qk = dq // tk_q
    dys = _mm(
        "branch_bwd", grid=(2, s // tm, w // tn_w, d // tk_q), operands=[d_br, wg_br],
        in_specs=[pl.BlockSpec((None, tm, tk_q), lambda n, i, j, k: (n, i, k)),
                  pl.BlockSpec((None, None, tn_w, tk_q), lambda n, i, j, k: (k // qk, n, j, k % qk))],
        out_shape=[jax.ShapeDtypeStruct((2, s, w), F32)],
        out_specs=[pl.BlockSpec((None, tm, tn_w), lambda n, i, j, k: (n, i, j))],
        pairs=[(0, 1, 0, None)], dims=NT, acc_shapes=[(tm, tn_w)],
        epilogue=lambda accs, ins, outs, ids: outs[0].__setitem__(Ellipsis, accs[0]),
    )[0][0]

    tm_w = _pick(w, 1024, LANES)
    g_br = _mm(
        "w_branch_grad", grid=(2, w // tm_w, d // tn_d, s // tk_s), operands=[ys, d_br],
        in_specs=[pl.BlockSpec((None, tk_s, tm_w), lambda n, i, j, k: (n, k, i)),
                  pl.BlockSpec((None, tk_s, tn_d), lambda n, i, j, k: (n, k, j))],
        out_shape=[jax.ShapeDtypeStruct((2, N_CHIPS, w, dq), BF16)],
        out_specs=[pl.BlockSpec((None, None, tm_w, tn_d), lambda n, i, j, k: (n, j // qd, i, j % qd))],
        pairs=[(0, 1, 0, None)], dims=TN, acc_shapes=[(tm_w, tn_d)],
        epilogue=lambda accs, ins, outs, ids: outs[0].__setitem__(Ellipsis, accs[0].astype(BF16)),
    )[0][0]

    dpa, g_pool, g_ps, g_cw, g_cb = _mix_bwd(proj, dys, wg_pool, pool_scale, wg_cw, conv_b, w)
    g_pool = g_pool.astype(BF16).reshape(2, N_CHIPS, 2 * (cg // N_CHIPS), cg)

    def chip_partials(nms, grads):
        recv = _pair_exchange("pair_exchange_" + nms[0], grads)
        parts = []
        for nm, g, r1 in zip(nms, grads, recv):
            _, nc, hh, wd = g.shape
            pt = _pair_add("pair_add_" + nm, g.reshape(2, nc * hh, wd), r1.reshape(nc * hh, wd), c_idx)
            parts.append(pt.reshape(nc, hh, wd))
        return parts

    parts_early = chip_partials(names[1:], [g_out, g_br, g_pool])

    na = (6 * w) // tn_p
    qg = d // tn_p
    hh_in = d // 2
    tm_i = _pick(hh_in, 1024, LANES)
    nb_i = hh_in // tm_i
    tn_i = _pick(tn_p, 512, LANES)
    na_i, qg_i, qp_i = (6 * w) // tn_i, d // tn_i, pq // tn_i
    (g_in,), recv_early = _mm(
        "w_in_grad", grid=(d // tm_i, p // tn_i, s // tk_s), operands=[h, dpa, dgl],
        comm=_scatter_partials(parts_early),
        in_specs=[pl.BlockSpec((tk_s, tm_i), lambda i, j, k: (k, i)),
                  pl.BlockSpec((tk_s, tn_i), lambda i, j, k: (k, jnp.minimum(j, na_i - 1))),
                  pl.BlockSpec((None, tk_s, tn_i),
                               lambda i, j, k: (jnp.maximum(j - na_i, 0) // qg_i, k,
                                                jnp.maximum(j - na_i, 0) % qg_i))],
        out_shape=[jax.ShapeDtypeStruct((2, N_CHIPS, hh_in, pq), BF16)],
        out_specs=[pl.BlockSpec((None, None, tm_i, tn_i),
                                lambda i, j, k: (i // nb_i, j // qp_i, i % nb_i, j % qp_i))],
        pairs=[(0, 1, 0, lambda ids: ids[1] < na_i), (0, 2, 0, lambda ids: ids[1] >= na_i)],
        dims=TN, acc_shapes=[(tm_i, tn_i)],
        epilogue=lambda accs, ins, outs, ids: outs[0].__setitem__(Ellipsis, accs[0].astype(BF16)),
    )

    parts_in = chip_partials(names[:1], [g_in])
    tn_h = _pick(d, 2048, LANES)
    (dh,), recv_in = _mm(
        "proj_bwd", grid=(s // tm, d // tn_h, p // tn_p), operands=[dpa, dgl, wg_in],
        comm=_scatter_partials(parts_in),
        in_specs=[pl.BlockSpec((tm, tn_p), lambda i, j, k: (i, jnp.minimum(k, na - 1))),
                  pl.BlockSpec((None, tm, tn_p),
                               lambda i, j, k: (jnp.maximum(k - na, 0) // qg, i, jnp.maximum(k - na, 0) % qg)),
                  pl.BlockSpec((None, tn_h, tn_p), lambda i, j, k: (k // qp, j, k % qp))],
        out_shape=[jax.ShapeDtypeStruct((s, d), F32)],
        out_specs=[pl.BlockSpec((tm, tn_h), lambda i, j, k: (i, j))],
        pairs=[(0, 2, 0, lambda ids: ids[2] < na), (1, 2, 0, lambda ids: ids[2] >= na)],
        dims=NT, acc_shapes=[(tm, tn_h)],
        epilogue=lambda accs, ins, outs, ids: outs[0].__setitem__(Ellipsis, accs[0]),
    )

    grad_x, g_nw = _rms_bwd(x2d, dh, dx2, norm_w)

    halves = [_sum_chips("sum_chips_" + nm, pt, r2, chip_idx)
              for nm, pt, r2 in zip(names, parts_in + parts_early, recv_in + recv_early)]
    others = _pair_share(halves)

    g_cw_full = jnp.transpose(g_cw, (1, 0, 2)).reshape(CONV_K, w)
    small_shapes = [(LANES,), (1, d), (1, d), (1, w), (1, w), (CONV_K, w), (2, d)]
    pack = _flat_pack([loss_part[0], g_nw, g_fnw, g_ps, g_cb, g_cw_full, g_gb])
    total = _sum_devices(_gather_packs(pack))
    t_loss, t_nw, t_fnw, t_ps, t_cb, t_cw, t_gb = _unpack(total, small_shapes)
    loss = t_loss[0]
    t_cw = lax.dynamic_slice_in_dim(t_cw, chip * cg, cg, axis=1)
    t_gb = lax.dynamic_slice_in_dim(t_gb, chip * dq, dq, axis=1)

    out = {}
    big_names = ["w_in", "w_out", "w_branch", "pool_w"]
    big_m = [m_w_in, m_w_out, m_w_branch, m_pool_w]
    big_v = [v_w_in, v_w_out, v_w_branch, v_pool_w]
    big_orig = [w_in, w_out, w_branch, pool_w]
    for nm, g_own, g_other, w2, mm_, vv_, orig in zip(big_names, halves, others, big_w, big_m, big_v, big_orig):
        sh = (2,) + g_own.shape
        res = _adamw_halves("adamw_" + nm, g_own, g_other, c_idx,
                            w2.reshape(sh), mm_.reshape(sh), vv_.reshape(sh))
        out[nm] = [r.reshape(orig.shape) for r in res]

    sm_names = ["norm_w", "final_norm_w", "pool_scale", "conv_b", "conv_w", "gate_b"]
    sm_g = [t_nw, t_fnw, t_ps, t_cb, t_cw, t_gb]
    sm_w = [norm_w, final_norm_w, pool_scale, conv_b, conv_w, gate_b]
    sm_m = [m_norm_w, m_final_norm_w, m_pool_scale, m_conv_b, m_conv_w, m_gate_b]
    sm_v = [v_norm_w, v_final_norm_w, v_pool_scale, v_conv_b, v_conv_w, v_gate_b]
    sm_shapes = [a.shape for a in sm_w]
    res = _adamw("adamw_small", _flat_pack(sm_g), _flat_pack(sm_w), _flat_pack(sm_m), _flat_pack(sm_v))
    res = [_unpack(r, sm_shapes) for r in res]
    for idx, nm in enumerate(sm_names):
        out[nm] = [r[idx] for r in res]

    order = ["norm_w", "w_in", "pool_w", "pool_scale", "conv_w", "conv_b", "gate_b", "w_branch", "w_out",
             "final_norm_w"]
    outs = [loss, grad_x.reshape(x.shape)]
    for kind in range(4):
        outs += [out[nm][kind] for nm in order]
    return tuple(outs)
```

```python
import math

import jax
import jax.numpy as jnp
from jax import lax
from jax.experimental import pallas as pl
from jax.experimental.pallas import tpu as pltpu

F32 = jnp.float32
BF16 = jnp.bfloat16

NORM_EPS = 1e-6
POOL_WINDOWS = (2, 4, 8, 16)
N_POOL_GROUPS = len(POOL_WINDOWS)
CONV_K = 3
ADAM_LR = 0.001
ADAM_B1 = 0.9
ADAM_B2 = 0.999
ADAM_EPS = 1e-08
ADAM_WD = 0.01
ADAM_STEP = 10

N_CHIPS = 4
N_DEV = 8
HALO = 16
LANES = 128
V7X_VMEM_BYTES = 64 * 1024 * 1024
VMEM_CAP = V7X_VMEM_BYTES - 8 * 1024 * 1024

MESH = pl.DeviceIdType.MESH
ANY = pl.BlockSpec(memory_space=pl.ANY)

NN = (((1,), (0,)), ((), ()))
NT = (((1,), (1,)), ((), ()))
TN = (((0,), (0,)), ((), ()))


def _pick(dim, pref, align):
    if dim <= pref:
        return dim
    t = (pref // align) * align
    while t >= align:
        if dim % t == 0:
            return t
        t -= align
    raise ValueError(f"no tile for {dim} (pref {pref}, align {align})")


def _nbytes(shape, dtype):
    n = 1
    for s in shape:
        if s is not None:
            n *= s
    return n * jnp.dtype(dtype).itemsize


def _params(semantics, block_bytes, extra_bytes=0):
    need = 2 * block_bytes + extra_bytes + (2 << 20)
    return pltpu.CompilerParams(dimension_semantics=semantics,
                                vmem_limit_bytes=int(min(max(need, 16 << 20), VMEM_CAP)))


def _sigmoid(z):
    return jax.nn.sigmoid(z)


def _mm(name, *, grid, operands, in_specs, out_shape, out_specs, pairs, dims, acc_shapes, epilogue,
        semantics=None, temp_bytes=0, comm=None, prefetch=None, aliases=None):
    aliases = dict(aliases or {})
    n_in, n_out = len(operands), len(out_shape)
    kax = len(grid) - 1
    nk = grid[kax]
    single = nk == 1
    conditional = any(p[3] is not None for p in pairs)
    if single and conditional:
        assert len(acc_shapes) == 1 and all(p[3] is not None for p in pairs)
    n_acc = 0 if single else len(acc_shapes)
    n_pf = 0 if prefetch is None else 1
    c_ops = list(comm.operands) if comm else []
    c_out = list(comm.out_shape) if comm else []
    c_sems = list(comm.scratch) if comm else []
    c_alias = dict(comm.aliases) if comm else {}
    n_ci, n_co = len(c_ops), len(c_out)

    def product(ins, ai, bi):
        return lax.dot_general(ins[ai][...], ins[bi][...], dims, preferred_element_type=F32)

    def body(*refs):
        refs = refs[n_pf:]
        ins = refs[:n_in]
        c_in_refs = refs[n_in:n_in + n_ci]
        outs = refs[n_in + n_ci:n_in + n_ci + n_out]
        c_out_refs = refs[n_in + n_ci + n_out:n_in + n_ci + n_out + n_co]
        scratch = refs[n_in + n_ci + n_out + n_co:]
        accs = scratch[:n_acc]
        sems = scratch[n_acc:]
        ids = [pl.program_id(a) for a in range(len(grid))]
        k = ids[kax]

        if comm:
            is_first = ids[0] == 0
            is_last = ids[0] == grid[0] - 1
            for a in range(1, len(grid)):
                is_first = jnp.logical_and(is_first, ids[a] == 0)
                is_last = jnp.logical_and(is_last, ids[a] == grid[a] - 1)

            @pl.when(is_first)
            def _():
                comm.start(c_in_refs, c_out_refs, sems)

        if single and conditional:
            for (ai, bi, ci, cond) in pairs:
                def only(ai=ai, bi=bi):
                    epilogue([product(ins, ai, bi)], ins, outs, ids)
                pl.when(cond(ids))(only)
        elif single:
            vals = [None] * len(acc_shapes)
            for (ai, bi, ci, cond) in pairs:
                r = product(ins, ai, bi)
                vals[ci] = r if vals[ci] is None else vals[ci] + r
            epilogue(vals, ins, outs, ids)
        else:
            @pl.when(k == 0)
            def _():
                for a in accs:
                    a[...] = jnp.zeros(a.shape, a.dtype)

            for (ai, bi, ci, cond) in pairs:
                def step(ai=ai, bi=bi, ci=ci):
                    accs[ci][...] += product(ins, ai, bi)
                if cond is None:
                    step()
                else:
                    pl.when(cond(ids))(step)

            @pl.when(k == nk - 1)
            def _():
                epilogue([a[...] for a in accs], ins, outs, ids)

        if comm:
            @pl.when(is_last)
            def _():
                comm.finish(c_in_refs, c_out_refs, sems)

    if semantics is None:
        semantics = ("parallel",) * kax + ("arbitrary",)
    if comm:
        semantics = ("arbitrary",) * len(grid)
    in_specs = [ANY if idx in aliases else spec for idx, spec in enumerate(in_specs)]
    blk = 0
    for idx, (spec, op) in enumerate(zip(in_specs, operands)):
        if idx not in aliases:
            blk += _nbytes(spec.block_shape, op.dtype)
    for spec, o in zip(out_specs, out_shape):
        blk += _nbytes(spec.block_shape, o.dtype)
    acc_bytes = sum(_nbytes(s, F32) for s in acc_shapes)
    io_alias = {n_pf + i: o for i, o in aliases.items()}
    io_alias.update({n_pf + n_in + i: n_out + o for i, o in c_alias.items()})
    all_in = list(in_specs) + [ANY] * n_ci
    all_out = list(out_specs) + [ANY] * n_co
    scratch_shapes = [pltpu.VMEM(s, F32) for s in acc_shapes[:n_acc]] + c_sems
    params = _params(semantics, blk, 3 * acc_bytes + temp_bytes)
    shapes = list(out_shape) + c_out
    if prefetch is None:
        call = pl.pallas_call(
            body, name=name, grid=grid, in_specs=all_in, out_specs=all_out, out_shape=shapes,
            scratch_shapes=scratch_shapes, input_output_aliases=io_alias, compiler_params=params)
        res = call(*operands, *c_ops)
    else:
        call = pl.pallas_call(
            body, name=name, out_shape=shapes, input_output_aliases=io_alias, compiler_params=params,
            grid_spec=pltpu.PrefetchScalarGridSpec(
                num_scalar_prefetch=1, grid=grid, in_specs=all_in, out_specs=all_out,
                scratch_shapes=scratch_shapes))
        res = call(prefetch, *operands, *c_ops)
    return res[:n_out], res[n_out:]


def _cast_bf16(name, w2d):
    r, c = w2d.shape
    tr = _pick(r, max(16, (4 << 20) // (4 * c)), 16)

    def body(w_ref, o_ref):
        o_ref[...] = w_ref[...].astype(BF16)

    return pl.pallas_call(
        body, name=name, grid=(r // tr,),
        in_specs=[pl.BlockSpec((tr, c), lambda i: (i, 0))],
        out_specs=pl.BlockSpec((tr, c), lambda i: (i, 0)),
        out_shape=jax.ShapeDtypeStruct((r, c), BF16),
        compiler_params=_params(("parallel",), tr * c * 6),
    )(w2d)


def _rms_fwd(x, norm_w):
    s, d = x.shape
    ts = _pick(s, 256, 16)

    def body(x_ref, w_ref, h_ref):
        xv = x_ref[...]
        rstd = lax.rsqrt(jnp.mean(xv * xv, axis=-1, keepdims=True) + NORM_EPS)
        h_ref[...] = (xv * rstd * w_ref[...]).astype(BF16)

    return pl.pallas_call(
        body, name="rms_fwd", grid=(s // ts,),
        in_specs=[pl.BlockSpec((ts, d), lambda i: (i, 0)), pl.BlockSpec((1, d), lambda i: (0, 0))],
        out_specs=pl.BlockSpec((ts, d), lambda i: (i, 0)),
        out_shape=jax.ShapeDtypeStruct((s, d), BF16),
        compiler_params=_params(("parallel",), ts * d * 6, 3 * ts * d * 4),
    )(x, norm_w)


def _head(x, o, target, fnw):
    s, d = x.shape
    ts = _pick(s, 128, 16)

    def body(x_ref, o_ref, t_ref, w_ref, dx_ref, dxb_ref, loss_ref, gw_ref):
        i = pl.program_id(0)

        @pl.when(i == 0)
        def _():
            loss_ref[...] = jnp.zeros(loss_ref.shape, F32)
            gw_ref[...] = jnp.zeros(gw_ref.shape, F32)

        w = w_ref[...]
        x2 = x_ref[...] + o_ref[...]
        rstd = lax.rsqrt(jnp.mean(x2 * x2, axis=-1, keepdims=True) + NORM_EPS)
        n = x2 * rstd
        e = n * w - t_ref[...]
        loss_ref[...] += 0.5 * jnp.sum(e * e) / d
        dy = e / d
        gw_ref[...] += jnp.sum(dy * n, axis=0, keepdims=True)
        gy = dy * w
        dx = rstd * (gy - n * jnp.mean(gy * n, axis=-1, keepdims=True))
        dx_ref[...] = dx
        dxb_ref[...] = dx.astype(BF16)

    row = pl.BlockSpec((ts, d), lambda i: (i, 0))
    return pl.pallas_call(
        body, name="head", grid=(s // ts,),
        in_specs=[row, row, row, pl.BlockSpec((1, d), lambda i: (0, 0))],
        out_specs=[row, row, pl.BlockSpec((8, LANES), lambda i: (0, 0)),
                   pl.BlockSpec((1, d), lambda i: (0, 0))],
        out_shape=[jax.ShapeDtypeStruct((s, d), F32), jax.ShapeDtypeStruct((s, d), BF16),
                   jax.ShapeDtypeStruct((8, LANES), F32), jax.ShapeDtypeStruct((1, d), F32)],
        compiler_params=_params(("arbitrary",), ts * d * 22, 6 * ts * d * 4),
    )(x, o, target, fnw)


def _rms_bwd(x, dh, dx2, norm_w):
    s, d = x.shape
    ts = _pick(s, 128, 16)

    def body(x_ref, dh_ref, dx2_ref, w_ref, gx_ref, gw_ref):
        i = pl.program_id(0)

        @pl.when(i == 0)
        def _():
            gw_ref[...] = jnp.zeros(gw_ref.shape, F32)

        xv = x_ref[...]
        rstd = lax.rsqrt(jnp.mean(xv * xv, axis=-1, keepdims=True) + NORM_EPS)
        n = xv * rstd
        dhv = dh_ref[...]
        gw_ref[...] += jnp.sum(dhv * n, axis=0, keepdims=True)
        gh = dhv * w_ref[...]
        gx_ref[...] = dx2_ref[...] + rstd * (gh - n * jnp.mean(gh * n, axis=-1, keepdims=True))

    row = pl.BlockSpec((ts, d), lambda i: (i, 0))
    vec = pl.BlockSpec((1, d), lambda i: (0, 0))
    return pl.pallas_call(
        body, name="rms_bwd", grid=(s // ts,),
        in_specs=[row, row, row, vec], out_specs=[row, vec],
        out_shape=[jax.ShapeDtypeStruct((s, d), F32), jax.ShapeDtypeStruct((1, d), F32)],
        compiler_params=_params(("arbitrary",), ts * d * 16, 5 * ts * d * 4),
    )(x, dh, dx2, norm_w)


def _silu(z):
    return z * _sigmoid(z)


def _dsilu(z):
    sg = _sigmoid(z)
    return sg * (1.0 + z * (1.0 - sg))


def _window_sum(ext, window, back):
    n = ext.shape[0]
    acc = ext
    step = 1
    while step < window:
        acc = acc + pltpu.roll(acc, step if back else n - step, 0)
        step *= 2
    return acc


def _shift_rows(ext, k, back):
    n = ext.shape[0]
    return pltpu.roll(ext, k if back else n - k, 0)


def _mix_fwd(proj, pool_wg, pool_scale, conv_wg, conv_b, width):
    s = proj.shape[0]
    w = width
    cg = w // N_POOL_GROUPS
    ts = _pick(s, 128, HALO)
    hb = ts // HALO
    cols = 6 * w

    def body(p_ref, ph_ref, pw_ref, ps_ref, cw_ref, cb_ref, ys_ref):
        i = pl.program_id(0)
        first = i == 0
        t1 = (i * ts + lax.broadcasted_iota(jnp.int32, (ts, 1), 0) + 1).astype(F32)

        def tile(part, g):
            lo = part * w + g * cg
            return p_ref[:, lo:lo + cg].astype(F32)

        def prev(part, g):
            lo = part * w + g * cg
            return jnp.where(first, 0.0, ph_ref[:, lo:lo + cg].astype(F32))

        for g, win in enumerate(POOL_WINDOWS):
            gs = slice(g * cg, (g + 1) * cg)
            u = tile(0, g)
            ext = jnp.concatenate([prev(0, g), u], axis=0)
            wsum = _window_sum(ext, win, True)[HALO:]
            pooled = wsum / jnp.minimum(t1, float(win)) - u
            pw = pw_ref[:, g].reshape(cg, cg)
            mixed = jnp.dot(pooled.astype(BF16), pw, preferred_element_type=F32)
            ys_ref[0, :, gs] = (mixed * ps_ref[:, gs] * _silu(tile(1, g))).astype(BF16)
            v = tile(4, g) * tile(2, g)
            vext = jnp.concatenate([prev(4, g) * prev(2, g), v], axis=0)
            v1 = _shift_rows(vext, 1, True)[HALO:]
            v2 = _shift_rows(vext, 2, True)[HALO:]
            cw = [cw_ref[g, tap:tap + 1, :] for tap in range(CONV_K)]
            y = cb_ref[:, gs] + cw[0] * v2 + cw[1] * v1 + cw[2] * v
            ys_ref[1, :, gs] = (tile(3, g) * y * _silu(tile(5, g))).astype(BF16)

    return pl.pallas_call(
        body, name="mix_fwd", grid=(s // ts,),
        in_specs=[pl.BlockSpec((ts, cols), lambda i: (i, 0)),
                  pl.BlockSpec((HALO, cols), lambda i: (jnp.maximum(i * hb - 1, 0), 0)),
                  pl.BlockSpec(pool_wg.shape, lambda i: (0, 0, 0, 0)),
                  pl.BlockSpec((1, w), lambda i: (0, 0)),
                  pl.BlockSpec(conv_wg.shape, lambda i: (0, 0, 0)),
                  pl.BlockSpec((1, w), lambda i: (0, 0))],
        out_specs=pl.BlockSpec((2, ts, w), lambda i: (0, i, 0)),
        out_shape=jax.ShapeDtypeStruct((2, s, w), BF16),
        compiler_params=_params(("parallel",), (ts + HALO) * cols * 2 + 2 * ts * w * 2
                                + _nbytes(pool_wg.shape, BF16), 24 * (ts + HALO) * cg * 4),
    )(proj, proj, pool_wg, pool_scale, conv_wg, conv_b)


def _mix_bwd(proj, dys, pool_wg, pool_scale, conv_wg, conv_b, width):
    s = proj.shape[0]
    w = width
    cg = w // N_POOL_GROUPS
    ts = _pick(s, 128, HALO)
    hb = ts // HALO
    n_tiles = s // ts
    last_hb = s // HALO - 1
    cols = 6 * w

    def body(p_ref, ph_ref, pn_ref, dy_ref, dyn_ref, pw_ref, ps_ref, cw_ref, cb_ref,
             dp_ref, dpw_ref, dps_ref, dcw_ref, dcb_ref):
        i = pl.program_id(0)
        first = i == 0
        last = i == n_tiles - 1

        @pl.when(first)
        def _():
            dpw_ref[...] = jnp.zeros(dpw_ref.shape, F32)
            dps_ref[...] = jnp.zeros(dps_ref.shape, F32)
            dcw_ref[...] = jnp.zeros(dcw_ref.shape, F32)
            dcb_ref[...] = jnp.zeros(dcb_ref.shape, F32)

        row = i * ts + lax.broadcasted_iota(jnp.int32, (ts + HALO, 1), 0)
        t1_ext = (row + 1).astype(F32)
        t1 = t1_ext[:ts]

        def tile(part, g):
            lo = part * w + g * cg
            return p_ref[:, lo:lo + cg].astype(F32)

        def prev(part, g):
            lo = part * w + g * cg
            return jnp.where(first, 0.0, ph_ref[:, lo:lo + cg].astype(F32))

        def ahead(part, g):
            lo = part * w + g * cg
            return jnp.concatenate([tile(part, g), pn_ref[:, lo:lo + cg].astype(F32)], axis=0)

        def dy_ahead(n, g):
            gs = slice(g * cg, (g + 1) * cg)
            nxt = jnp.where(last, 0.0, dyn_ref[n, :, gs])
            return jnp.concatenate([dy_ref[n, :, gs], nxt], axis=0)

        for g, win in enumerate(POOL_WINDOWS):
            gs = slice(g * cg, (g + 1) * cg)
            u = tile(0, g)
            ext = jnp.concatenate([prev(0, g), u], axis=0)
            pooled = _window_sum(ext, win, True)[HALO:] / jnp.minimum(t1, float(win)) - u
            pooled_b = pooled.astype(BF16)
            pw = pw_ref[:, g].reshape(cg, cg)
            mixed = jnp.dot(pooled_b, pw, preferred_element_type=F32)
            zp_ext = ahead(1, g)
            dy0_ext = dy_ahead(0, g)
            scale = ps_ref[:, gs]
            dms_ext = dy0_ext * _silu(zp_ext)
            dmix_b = (dms_ext * scale).astype(BF16)
            dpooled_ext = lax.dot_general(dmix_b, pw, NT, preferred_element_type=F32)
            dy0 = dy0_ext[:ts]
            zp = zp_ext[:ts]
            dp_ref[:, w + g * cg:w + (g + 1) * cg] = (dy0 * mixed * scale * _dsilu(zp)).astype(BF16)
            dps_ref[:, gs] += jnp.sum(dms_ext[:ts] * mixed, axis=0, keepdims=True)
            dpw = lax.dot_general(pooled_b, dmix_b[:ts], TN, preferred_element_type=F32)
            dpw_ref[g // 2, :, g % 2] += dpw.reshape(N_CHIPS, cg // N_CHIPS, cg)
            q_ext = dpooled_ext / jnp.minimum(t1_ext, float(win))
            du = _window_sum(q_ext, win, False)[:ts] - dpooled_ext[:ts]
            dp_ref[:, gs] = du.astype(BF16)
            uc = tile(2, g)
            cc = tile(4, g)
            v = cc * uc
            vext = jnp.concatenate([prev(4, g) * prev(2, g), v], axis=0)
            v1 = _shift_rows(vext, 1, True)[HALO:]
            v2 = _shift_rows(vext, 2, True)[HALO:]
            cw = [cw_ref[g, tap:tap + 1, :] for tap in range(CONV_K)]
            y = cb_ref[:, gs] + cw[0] * v2 + cw[1] * v1 + cw[2] * v
            bc_ext = ahead(3, g)
            zc_ext = ahead(5, g)
            dy1_ext = dy_ahead(1, g)
            dyy_ext = dy1_ext * bc_ext * _silu(zc_ext)
            dy1 = dy1_ext[:ts]
            bc = bc_ext[:ts]
            zc = zc_ext[:ts]
            dp_ref[:, 3 * w + g * cg:3 * w + (g + 1) * cg] = (dy1 * y * _silu(zc)).astype(BF16)
            dp_ref[:, 5 * w + g * cg:5 * w + (g + 1) * cg] = (dy1 * bc * y * _dsilu(zc)).astype(BF16)
            dyy = dyy_ext[:ts]
            dcb_ref[:, gs] += jnp.sum(dyy, axis=0, keepdims=True)
            for tap, vt in enumerate((v2, v1, v)):
                dcw_ref[g, tap:tap + 1, :] += jnp.sum(dyy * vt, axis=0, keepdims=True)
            dv = (cw[2] * dyy + cw[1] * _shift_rows(dyy_ext, 1, False)[:ts]
                  + cw[0] * _shift_rows(dyy_ext, 2, False)[:ts])
            dp_ref[:, 4 * w + g * cg:4 * w + (g + 1) * cg] = (dv * uc).astype(BF16)
            dp_ref[:, 2 * w + g * cg:2 * w + (g + 1) * cg] = (dv * cc).astype(BF16)

    dpw_shape = (2, N_CHIPS, 2, cg // N_CHIPS, cg)
    return pl.pallas_call(
        body, name="mix_bwd", grid=(n_tiles,),
        in_specs=[pl.BlockSpec((ts, cols), lambda i: (i, 0)),
                  pl.BlockSpec((HALO, cols), lambda i: (jnp.maximum(i * hb - 1, 0), 0)),
                  pl.BlockSpec((HALO, cols), lambda i: (jnp.minimum((i + 1) * hb, last_hb), 0)),
                  pl.BlockSpec((2, ts, w), lambda i: (0, i, 0)),
                  pl.BlockSpec((2, HALO, w), lambda i: (0, jnp.minimum((i + 1) * hb, last_hb), 0)),
                  pl.BlockSpec(pool_wg.shape, lambda i: (0, 0, 0, 0)),
                  pl.BlockSpec((1, w), lambda i: (0, 0)),
                  pl.BlockSpec(conv_wg.shape, lambda i: (0, 0, 0)),
                  pl.BlockSpec((1, w), lambda i: (0, 0))],
        out_specs=[pl.BlockSpec((ts, cols), lambda i: (i, 0)),
                   pl.BlockSpec(dpw_shape, lambda i: (0, 0, 0, 0, 0)),
                   pl.BlockSpec((1, w), lambda i: (0, 0)),
                   pl.BlockSpec(conv_wg.shape, lambda i: (0, 0, 0)),
                   pl.BlockSpec((1, w), lambda i: (0, 0))],
        out_shape=[jax.ShapeDtypeStruct((s, cols), BF16), jax.ShapeDtypeStruct(dpw_shape, F32),
                   jax.ShapeDtypeStruct((1, w), F32), jax.ShapeDtypeStruct(conv_wg.shape, F32),
                   jax.ShapeDtypeStruct((1, w), F32)],
        compiler_params=_params(("arbitrary",), (2 * ts + 2 * HALO) * cols * 2 + (ts + HALO) * w * 8
                                + _nbytes(pool_wg.shape, BF16) + _nbytes(dpw_shape, F32),
                                40 * (ts + HALO) * cg * 4),
    )(proj, proj, proj, dys, dys, pool_wg, pool_scale, conv_wg, conv_b)


def _adamw(name, g, w, m, v):
    r, c = w.shape
    tr = _pick(r, max(8, (1 << 20) // (4 * c)), 8)

    def body(g_ref, w_ref, m_ref, v_ref, go_ref, d_ref, mo_ref, vo_ref):
        gv = g_ref[...]
        mn = ADAM_B1 * m_ref[...] + (1.0 - ADAM_B1) * gv
        vn = ADAM_B2 * v_ref[...] + (1.0 - ADAM_B2) * (gv * gv)
        m_hat = mn / (1.0 - ADAM_B1 ** ADAM_STEP)
        v_hat = vn / (1.0 - ADAM_B2 ** ADAM_STEP)
        go_ref[...] = gv
        d_ref[...] = -ADAM_LR * (m_hat / (jnp.sqrt(v_hat) + ADAM_EPS) + ADAM_WD * w_ref[...])
        mo_ref[...] = mn
        vo_ref[...] = vn

    blk = pl.BlockSpec((tr, c), lambda i: (i, 0))
    sh = jax.ShapeDtypeStruct((r, c), F32)
    return pl.pallas_call(
        body, name=name, grid=(r // tr,), in_specs=[blk] * 4, out_specs=[blk] * 4,
        out_shape=[sh] * 4, compiler_params=_params(("parallel",), tr * c * 32, 4 * tr * c * 4),
    )(g, w, m, v)


def _pair_add(name, g, r1, c_idx):
    _, r, c = g.shape
    tr = _pick(r, max(16, (2 << 20) // (2 * c)), 16)

    def body(c_ref, g_ref, r_ref, o_ref):
        o_ref[...] = (g_ref[...].astype(F32) + r_ref[...].astype(F32)).astype(BF16)

    return pl.pallas_call(
        body, name=name,
        grid_spec=pltpu.PrefetchScalarGridSpec(
            num_scalar_prefetch=1, grid=(r // tr,),
            in_specs=[pl.BlockSpec((None, tr, c), lambda i, cr: (cr[0], i, 0)),
                      pl.BlockSpec((tr, c), lambda i, cr: (i, 0))],
            out_specs=pl.BlockSpec((tr, c), lambda i, cr: (i, 0))),
        out_shape=jax.ShapeDtypeStruct((r, c), BF16),
        compiler_params=_params(("parallel",), tr * c * 6, 3 * tr * c * 4),
    )(c_idx, g, r1)


def _adamw_halves(name, g_own, g_other, c_idx, w, m, v):
    _, r, c = w.shape
    tr = _pick(r, max(8, (1 << 20) // (4 * c)), 8)

    def body(c_ref, go_ref, gt_ref, w_ref, m_ref, v_ref, g_out, d_ref, mo_ref, vo_ref):
        gv = jnp.where(pl.program_id(0) == c_ref[0], go_ref[...], gt_ref[...])
        mn = ADAM_B1 * m_ref[...] + (1.0 - ADAM_B1) * gv
        vn = ADAM_B2 * v_ref[...] + (1.0 - ADAM_B2) * (gv * gv)
        m_hat = mn / (1.0 - ADAM_B1 ** ADAM_STEP)
        v_hat = vn / (1.0 - ADAM_B2 ** ADAM_STEP)
        g_out[...] = gv
        d_ref[...] = -ADAM_LR * (m_hat / (jnp.sqrt(v_hat) + ADAM_EPS) + ADAM_WD * w_ref[...])
        mo_ref[...] = mn
        vo_ref[...] = vn

    blk = pl.BlockSpec((None, tr, c), lambda h, i, cr: (h, i, 0))
    sh = jax.ShapeDtypeStruct(w.shape, F32)
    return pl.pallas_call(
        body, name=name,
        grid_spec=pltpu.PrefetchScalarGridSpec(
            num_scalar_prefetch=1, grid=(2, r // tr),
            in_specs=[pl.BlockSpec((tr, c), lambda h, i, cr: (jnp.where(h == cr[0], i, 0), 0)),
                      pl.BlockSpec((tr, c), lambda h, i, cr: (jnp.where(h == cr[0], 0, i), 0)),
                      blk, blk, blk],
            out_specs=[blk] * 4),
        out_shape=[sh] * 4,
        compiler_params=_params(("arbitrary", "arbitrary"), tr * c * 36, 4 * tr * c * 4),
    )(c_idx, g_own, g_other, w, m, v)


def _sum_chips(name, part, r2, chip_idx, row0=0, prev=None):
    n, r, c = r2.shape
    rc = part.shape[1]
    tr = _pick(math.gcd(rc, row0) if row0 else rc, max(16, (1 << 20) // (2 * c)), 16)
    b0 = row0 // tr

    def body(ch_ref, own_ref, *rest):
        slots, o_ref = rest[:n], rest[-1]
        acc = None
        for s in range(n):
            term = jnp.where(ch_ref[0] == s, own_ref[...], slots[s][...]).astype(F32)
            acc = term if acc is None else acc + term
        o_ref[...] = acc

    def slot_spec(s):
        return pl.BlockSpec((None, tr, c), lambda i, ch: (jnp.where(ch[0] == s, (s + 1) % n, s), b0 + i, 0))

    extra = [] if prev is None else [prev]
    return pl.pallas_call(
        body, name=name,
        grid_spec=pltpu.PrefetchScalarGridSpec(
            num_scalar_prefetch=1, grid=(rc // tr,),
            in_specs=[pl.BlockSpec((None, tr, c), lambda i, ch: (ch[0], i, 0))]
            + [slot_spec(s) for s in range(n)] + [ANY] * len(extra),
            out_specs=pl.BlockSpec((tr, c), lambda i, ch: (b0 + i, 0))),
        out_shape=jax.ShapeDtypeStruct((r, c), F32),
        input_output_aliases={2 + n: 0} if extra else {},
        compiler_params=_params(("parallel",), tr * c * (2 * n + 6), 3 * tr * c * 4),
    )(chip_idx, part, *([r2] * n), *extra)


def _sum_devices(packs):
    n, r, c = packs.shape

    def body(p_ref, o_ref):
        acc = p_ref[0]
        for k in range(1, n):
            acc = acc + p_ref[k]
        o_ref[...] = acc

    return pl.pallas_call(
        body, name="sum_devices", out_shape=jax.ShapeDtypeStruct((r, c), F32),
        in_specs=[pl.BlockSpec(memory_space=pltpu.VMEM)],
        out_specs=pl.BlockSpec(memory_space=pltpu.VMEM),
    )(packs)


def _place():
    x, y, c = lax.axis_index("x"), lax.axis_index("y"), lax.axis_index("c")
    return x, y, c


def _chip_peers(x, y):
    out = []
    for k, (fx, fy) in enumerate(((0, 1), (1, 0), (1, 1))):
        px = 1 - x if fx else x
        py = 1 - y if fy else y
        out.append((k, px, py, 2 * px + py))
    return out


def _gather_weights(big, small):
    nb, ns = len(big), len(small)

    class Copies:
        def __init__(self, c_in, c_out, sems):
            b_in, s_in = c_in[:nb], c_in[nb:]
            b_out, s_out = c_out[:nb], c_out[nb:]
            ici_s, ici_r, d2d_s, d2d_r, own_s, own_r = sems[:6]
            x, y, c = _place()
            chip = 2 * x + y
            sibling = (x, y, 1 - c)
            peers = _chip_peers(x, y)
            self.own = [pltpu.make_async_remote_copy(
                src_ref=b_in[t], dst_ref=b_out[t].at[chip], send_sem=own_s.at[t], recv_sem=own_r.at[t],
                device_id=sibling, device_id_type=MESH) for t in range(nb)]
            self.ici = [pltpu.make_async_remote_copy(
                src_ref=b_in[t].at[c], dst_ref=b_out[t].at[chip, c],
                send_sem=ici_s.at[t, k], recv_sem=ici_r.at[t, k],
                device_id=(px, py, c), device_id_type=MESH)
                for t in range(nb) for (k, px, py, pchip) in peers]
            self.landed = [pltpu.make_async_remote_copy(
                src_ref=b_out[t].at[pchip, c], dst_ref=b_out[t].at[pchip, c],
                send_sem=ici_s.at[t, k], recv_sem=ici_r.at[t, k],
                device_id=sibling, device_id_type=MESH)
                for t in range(nb) for (k, px, py, pchip) in peers]
            self.passed = [pltpu.make_async_remote_copy(
                src_ref=b_out[t].at[pchip, c], dst_ref=b_out[t].at[pchip, c],
                send_sem=d2d_s.at[t, k], recv_sem=d2d_r.at[t, k],
                device_id=sibling, device_id_type=MESH)
                for t in range(nb) for (k, px, py, pchip) in peers]
            self.from_sibling = [pltpu.make_async_remote_copy(
                src_ref=b_out[t].at[pchip, 1 - c], dst_ref=b_out[t].at[pchip, 1 - c],
                send_sem=d2d_s.at[t, k], recv_sem=d2d_r.at[t, k],
                device_id=sibling, device_id_type=MESH)
                for t in range(nb) for (k, px, py, pchip) in peers]
            self.small, self.small_landed, self.local = [], [], []
            if ns:
                sm_s, sm_r, loc = sems[6:]
                self.local = [pltpu.make_async_copy(s_in[t], s_out[t].at[chip], loc.at[t]) for t in range(ns)]
                self.small = [pltpu.make_async_remote_copy(
                    src_ref=s_in[t], dst_ref=s_out[t].at[chip],
                    send_sem=sm_s.at[t, k], recv_sem=sm_r.at[t, k],
                    device_id=(px, py, c), device_id_type=MESH)
                    for t in range(ns) for (k, px, py, pchip) in peers]
                self.small_landed = [pltpu.make_async_remote_copy(
                    src_ref=s_in[t], dst_ref=s_out[t].at[pchip],
                    send_sem=sm_s.at[t, k], recv_sem=sm_r.at[t, k],
                    device_id=sibling, device_id_type=MESH)
                    for t in range(ns) for (k, px, py, pchip) in peers]

    def start(c_in, c_out, sems):
        cps = Copies(c_in, c_out, sems)
        for cp in cps.local + cps.own + cps.ici + cps.small:
            cp.start()

    def finish(c_in, c_out, sems):
        cps = Copies(c_in, c_out, sems)
        for arrived, onward in zip(cps.landed, cps.passed):
            arrived.wait_recv()
            onward.start()
        for cp in cps.small_landed + cps.from_sibling:
            cp.wait_recv()
        for cp in cps.ici + cps.small + cps.passed:
            cp.wait_send()
        for cp in cps.own + cps.local:
            cp.wait()

    out_shape = [jax.ShapeDtypeStruct((N_CHIPS,) + b.shape, b.dtype) for b in big]
    out_shape += [jax.ShapeDtypeStruct((N_CHIPS,) + s.shape, s.dtype) for s in small]
    dma = pltpu.SemaphoreType.DMA
    scratch = [dma((nb, 3)), dma((nb, 3)), dma((nb, 3)), dma((nb, 3)), dma((nb,)), dma((nb,))]
    if ns:
        scratch += [dma((ns, 3)), dma((ns, 3)), dma((ns,))]
    return _Exchange(list(big) + list(small), out_shape, scratch, start, finish)


def _pair_exchange(name, grads):
    n = len(grads)

    def body(*refs):
        g_in = refs[:n]
        r_out = refs[n:2 * n]
        send, recv = refs[2 * n:]
        x, y, c = _place()
        cps = [pltpu.make_async_remote_copy(
            src_ref=g_in[t].at[1 - c], dst_ref=r_out[t], send_sem=send.at[t], recv_sem=recv.at[t],
            device_id=(x, y, 1 - c), device_id_type=MESH) for t in range(n)]
        for cp in cps:
            cp.start()
        for cp in cps:
            cp.wait()

    dma = pltpu.SemaphoreType.DMA
    return pl.pallas_call(
        body, name=name,
        out_shape=[jax.ShapeDtypeStruct(g.shape[1:], g.dtype) for g in grads],
        in_specs=[ANY] * n, out_specs=[ANY] * n, scratch_shapes=[dma((n,)), dma((n,))],
    )(*grads)


class _Exchange:
    def __init__(self, operands, out_shape, scratch, start, finish, aliases=None):
        self.operands, self.out_shape, self.scratch = operands, out_shape, scratch
        self.start, self.finish, self.aliases = start, finish, dict(aliases or {})


def _scatter_partials(parts, rows=None, row0=0, prev=None):
    n = len(parts)
    land = [(p.shape[0], p.shape[1] if rows is None else rows, p.shape[2]) for p in parts]

    def window(ref, slot, t):
        return ref.at[slot, pl.ds(row0, parts[t].shape[1])]

    def sends(p_in, r_out, sems):
        send, recv = sems
        x, y, c = _place()
        chip = 2 * x + y
        return [pltpu.make_async_remote_copy(
            src_ref=p_in[t].at[pchip], dst_ref=window(r_out[t], chip, t),
            send_sem=send.at[t, k], recv_sem=recv.at[t, k],
            device_id=(px, py, c), device_id_type=MESH)
            for t in range(n) for (k, px, py, pchip) in _chip_peers(x, y)]

    def start(p_in, r_out, sems):
        for cp in sends(p_in, r_out, sems):
            cp.start()

    def finish(p_in, r_out, sems):
        send, recv = sems
        x, y, c = _place()
        for t in range(n):
            for (k, px, py, pchip) in _chip_peers(x, y):
                pltpu.make_async_remote_copy(
                    src_ref=p_in[t].at[pchip], dst_ref=window(r_out[t], pchip, t),
                    send_sem=send.at[t, k], recv_sem=recv.at[t, k],
                    device_id=(px, py, c), device_id_type=MESH).wait_recv()
        for cp in sends(p_in, r_out, sems):
            cp.wait_send()

    dma = pltpu.SemaphoreType.DMA
    operands = list(parts) + (list(prev) if prev else [])
    return _Exchange(operands, [jax.ShapeDtypeStruct(sh, p.dtype) for sh, p in zip(land, parts)],
                     [dma((n, 3)), dma((n, 3))], start, finish,
                     aliases={n + t: t for t in range(n)} if prev else None)


def _w_in_grad_sibling(h, dpa, dgl, c_idx, carried, *, tm, tn, rows, pq):
    s, d = h.shape
    wa, wg = dpa.shape[1], dgl.shape[2]
    p = wa + 2 * wg
    ni, nj = rows // tm, p // tn
    na, qg, qp = wa // tn, wg // tn, pq // tn
    n_steps = ni * nj
    n_ci, n_co = len(carried.operands), len(carried.out_shape)

    def body(c_ref, h_ref, a_ref, g_ref, *rest):
        c_in = rest[:n_ci]
        r1_ref = rest[n_ci]
        c_out = rest[n_ci + 1:n_ci + 1 + n_co]
        slots, send, recv = rest[n_ci + 1 + n_co:n_ci + 4 + n_co]
        sems = rest[n_ci + 4 + n_co:]
        i, j = pl.program_id(0), pl.program_id(1)
        step = i * nj + j
        slot = lax.rem(step, 2)
        x, y, c = _place()
        sibling = (x, y, 1 - c)

        @pl.when(step == 0)
        def _():
            carried.start(c_in, c_out, sems)

        def tile_copy(sl):
            return pltpu.make_async_remote_copy(
                src_ref=slots.at[sl],
                dst_ref=r1_ref.at[j // qp, pl.ds(i * tm, tm), pl.ds((j % qp) * tn, tn)],
                send_sem=send.at[sl], recv_sem=recv, device_id=sibling, device_id_type=MESH)

        @pl.when(step >= 2)
        def _():
            tile_copy(slot).wait_send()

        def emit(b_ref):
            acc = lax.dot_general(h_ref[...], b_ref[...], TN, preferred_element_type=F32)
            slots[slot] = acc.astype(BF16)

        pl.when(j < na)(lambda: emit(a_ref))
        pl.when(j >= na)(lambda: emit(g_ref))
        tile_copy(slot).start()

        @pl.when(step == n_steps - 1)
        def _():
            tile_copy(0).wait_send()
            tile_copy(1).wait_send()
            pltpu.make_async_remote_copy(
                src_ref=r1_ref, dst_ref=r1_ref, send_sem=send.at[0], recv_sem=recv,
                device_id=sibling, device_id_type=MESH).wait_recv()
            carried.finish(c_in, c_out, sems)

    nbh = rows // tm
    dma = pltpu.SemaphoreType.DMA
    blk = s * tm * 2 + 2 * s * tn * 2
    res = pl.pallas_call(
        body, name="w_in_grad_sibling",
        grid_spec=pltpu.PrefetchScalarGridSpec(
            num_scalar_prefetch=1, grid=(ni, nj),
            in_specs=[pl.BlockSpec((s, tm), lambda i, j, cr: (0, (1 - cr[0]) * nbh + i)),
                      pl.BlockSpec((s, tn), lambda i, j, cr: (0, jnp.minimum(j, na - 1))),
                      pl.BlockSpec((None, s, tn),
                                   lambda i, j, cr: (jnp.maximum(j - na, 0) // qg, 0, jnp.maximum(j - na, 0) % qg))]
            + [ANY] * n_ci,
            out_specs=[ANY] * (1 + n_co),
            scratch_shapes=[pltpu.VMEM((2, tm, tn), BF16), dma((2,)), dma] + list(carried.scratch)),
        out_shape=[jax.ShapeDtypeStruct((N_CHIPS, rows, pq), BF16)] + list(carried.out_shape),
        compiler_params=_params(("arbitrary", "arbitrary"), blk, 2 * tm * tn * 2 + 2 * tm * tn * 4),
    )(c_idx, h, dpa, dgl, *carried.operands)
    return res[0], res[1:]


def _pair_share(halves):
    n = len(halves)

    def body(*refs):
        h_in = refs[:n]
        o_out = refs[n:2 * n]
        send, recv = refs[2 * n:]
        x, y, c = _place()
        cps = [pltpu.make_async_remote_copy(
            src_ref=h_in[t], dst_ref=o_out[t], send_sem=send.at[t], recv_sem=recv.at[t],
            device_id=(x, y, 1 - c), device_id_type=MESH) for t in range(n)]
        for cp in cps:
            cp.start()
        for cp in cps:
            cp.wait()

    dma = pltpu.SemaphoreType.DMA
    return pl.pallas_call(
        body, name="pair_share",
        out_shape=[jax.ShapeDtypeStruct(h.shape, h.dtype) for h in halves],
        in_specs=[ANY] * n, out_specs=[ANY] * n, scratch_shapes=[dma((n,)), dma((n,))],
    )(*halves)


def _gather_packs(pack):
    def body(p_ref, o_ref, send, recv, loc):
        x, y, c = _place()
        me = 4 * x + 2 * y + c
        mine = pltpu.make_async_copy(p_ref, o_ref.at[me], loc)
        mine.start()
        flips = [(fx, fy, fc) for fx in (0, 1) for fy in (0, 1) for fc in (0, 1)][1:]
        cps = []
        for k, (fx, fy, fc) in enumerate(flips):
            peer = (1 - x if fx else x, 1 - y if fy else y, 1 - c if fc else c)
            cps.append(pltpu.make_async_remote_copy(
                src_ref=p_ref, dst_ref=o_ref.at[me], send_sem=send.at[k], recv_sem=recv.at[k],
                device_id=peer, device_id_type=MESH))
        for cp in cps:
            cp.start()
        for k, (fx, fy, fc) in enumerate(flips):
            px, py, pc = (1 - x if fx else x, 1 - y if fy else y, 1 - c if fc else c)
            pltpu.make_async_remote_copy(
                src_ref=p_ref, dst_ref=o_ref.at[4 * px + 2 * py + pc],
                send_sem=send.at[k], recv_sem=recv.at[k],
                device_id=(px, py, pc), device_id_type=MESH).wait_recv()
        for cp in cps:
            cp.wait_send()
        mine.wait()

    dma = pltpu.SemaphoreType.DMA
    return pl.pallas_call(
        body, name="gather_packs", out_shape=jax.ShapeDtypeStruct((N_DEV,) + pack.shape, pack.dtype),
        in_specs=[ANY], out_specs=ANY, scratch_shapes=[dma((7,)), dma((7,)), dma],
    )(pack)


def _flat_pack(pieces):
    flat = jnp.concatenate([p.reshape(-1) for p in pieces])
    pad = (-flat.shape[0]) % (8 * LANES)
    flat = jnp.concatenate([flat, jnp.zeros((pad,), F32)])
    return flat.reshape(-1, LANES)


def _unpack(pack, shapes):
    flat = pack.reshape(-1)
    out, off = [], 0
    for sh in shapes:
        n = 1
        for s in sh:
            n *= s
        out.append(flat[off:off + n].reshape(sh))
        off += n
    return out


def kernel(x, norm_w, w_in, pool_w, pool_scale, conv_w, conv_b, gate_b, w_branch, w_out, final_norm_w, loss_target, m_norm_w, m_w_in, m_pool_w, m_pool_scale, m_conv_w, m_conv_b, m_gate_b, m_w_branch, m_w_out, m_final_norm_w, v_norm_w, v_w_in, v_pool_w, v_pool_scale, v_conv_w, v_conv_b, v_gate_b, v_w_branch, v_w_out, v_final_norm_w):
    _, s, d = x.shape
    w = d // 2
    cg = w // N_POOL_GROUPS
    p = 6 * w + 2 * d
    pq = p // N_CHIPS
    dq = d // N_CHIPS
    assert w_in.shape == (1, d, pq) and w_branch.shape == (1, 2, w, dq) and w_out.shape == (1, dq, d)
    assert pool_w.shape == (1, N_POOL_GROUPS, cg // N_CHIPS, cg) and conv_w.shape == (1, CONV_K, cg)

    x2d = x.reshape(s, d)
    tgt = loss_target.reshape(s, d)
    c_idx = lax.axis_index("c").astype(jnp.int32).reshape(1)
    chip = 2 * lax.axis_index("x") + lax.axis_index("y")

    big_w = [w_in.reshape(d, pq), w_out.reshape(dq, d), w_branch.reshape(2 * w, dq),
             pool_w.reshape(cg, cg)]
    names = ["in", "out", "branch", "pool"]
    big_b = [_cast_bf16("cast_" + nm, a) for nm, a in zip(names, big_w)]
    big_b = [b.reshape(2, b.shape[0] // 2, b.shape[1]) for b in big_b]
    chip_idx = chip.astype(jnp.int32).reshape(1)

    h = _rms_fwd(x2d, norm_w)

    tm = _pick(s, 1024, 16)
    tn_p = _pick(math.gcd(pq, 6 * w, d), 1024, LANES)
    qp = pq // tn_p
    tk_d = _pick(d, 4096, LANES)
    cast_out = lambda accs, ins, outs, ids: outs[0].__setitem__(Ellipsis, accs[0].astype(BF16))
    (proj,), (wg_in,) = _mm(
        "proj_own", grid=(s // tm, qp, d // tk_d), operands=[h, big_b[0].reshape(d, pq)],
        in_specs=[pl.BlockSpec((tm, tk_d), lambda i, j, k, ch: (i, k)),
                  pl.BlockSpec((tk_d, tn_p), lambda i, j, k, ch: (k, j))],
        out_shape=[jax.ShapeDtypeStruct((s, p), BF16)],
        out_specs=[pl.BlockSpec((tm, tn_p), lambda i, j, k, ch: (i, ch[0] * qp + j))],
        pairs=[(0, 1, 0, None)], dims=NN, acc_shapes=[(tm, tn_p)], epilogue=cast_out,
        comm=_gather_weights(big_b[:1], []), prefetch=chip_idx,
    )
    wg_in = wg_in.reshape(N_CHIPS, d, pq)
    other = lambda j, ch: jnp.bitwise_xor(ch[0], 1 + j // qp)
    (proj,), (wg_out, wg_br, wg_pool, wg_cw, wg_gb) = _mm(
        "proj_others", grid=(s // tm, 3 * qp, d // tk_d), operands=[h, wg_in, proj],
        in_specs=[pl.BlockSpec((tm, tk_d), lambda i, j, k, ch: (i, k)),
                  pl.BlockSpec((None, tk_d, tn_p), lambda i, j, k, ch: (other(j, ch), k, j % qp)),
                  None],
        out_shape=[jax.ShapeDtypeStruct((s, p), BF16)],
        out_specs=[pl.BlockSpec((tm, tn_p), lambda i, j, k, ch: (i, other(j, ch) * qp + j % qp))],
        pairs=[(0, 1, 0, None)], dims=NN, acc_shapes=[(tm, tn_p)], epilogue=cast_out,
        comm=_gather_weights(big_b[1:], [conv_w.reshape(CONV_K, cg), gate_b.reshape(2, dq)]),
        prefetch=chip_idx, aliases={2: 0},
    )
    wg_out = wg_out.reshape(d, d)
    wg_pool = wg_pool.reshape(N_CHIPS, N_POOL_GROUPS, cg // N_CHIPS, cg)

    ys = _mix_fwd(proj, wg_pool, pool_scale, wg_cw, conv_b, w)

    tn_d = _pick(dq, 1024, LANES)
    qd = dq // tn_d
    tk_w = _pick(w, 2048, LANES)
    tm_g = _pick(s, 512, 16)
    gl0 = (6 * w) // tn_d
    gl1 = (6 * w + d) // tn_d

    def gate_specs(im, rows):
        return [pl.BlockSpec((rows, tn_d), lambda *a: (im(*a)[0], gl0 + im(*a)[1])),
                pl.BlockSpec((rows, tn_d), lambda *a: (im(*a)[0], gl1 + im(*a)[1])),
                pl.BlockSpec((None, 2, tn_d), lambda *a: (im(*a)[1] // qd, 0, im(*a)[1] % qd))]

    def merge_epilogue(accs, ins, outs, ids):
        gb = ins[6][...]
        g0 = _sigmoid(ins[4][...].astype(F32) + gb[0:1])
        g1 = _sigmoid(ins[5][...].astype(F32) + gb[1:2])
        outs[0][...] = (g0 * accs[0] + g1 * accs[1]).astype(BF16)
        outs[1][0] = accs[0].astype(BF16)
        outs[1][1] = accs[1].astype(BF16)

    (merged, br), _ = _mm(
        "branch_merge", grid=(s // tm_g, d // tn_d, w // tk_w),
        operands=[ys, wg_br, ys, wg_br, proj, proj, wg_gb],
        in_specs=[pl.BlockSpec((None, tm_g, tk_w), lambda i, j, k: (0, i, k)),
                  pl.BlockSpec((None, None, tk_w, tn_d), lambda i, j, k: (j // qd, 0, k, j % qd)),
                  pl.BlockSpec((None, tm_g, tk_w), lambda i, j, k: (1, i, k)),
                  pl.BlockSpec((None, None, tk_w, tn_d), lambda i, j, k: (j // qd, 1, k, j % qd)),
                  *gate_specs(lambda i, j, k: (i, j), tm_g)],
        out_shape=[jax.ShapeDtypeStruct((s, d), BF16), jax.ShapeDtypeStruct((2, s, d), BF16)],
        out_specs=[pl.BlockSpec((tm_g, tn_d), lambda i, j, k: (i, j)),
                   pl.BlockSpec((2, tm_g, tn_d), lambda i, j, k: (0, i, j))],
        pairs=[(0, 1, 0, None), (2, 3, 1, None)], dims=NN, acc_shapes=[(tm_g, tn_d)] * 2,
        epilogue=merge_epilogue, temp_bytes=6 * tm_g * tn_d * 4,
    )

    tn_f = _pick(d, 1024, LANES)
    o = _mm(
        "out_proj", grid=(s // tm, d // tn_f, d // tk_d), operands=[merged, wg_out],
        in_specs=[pl.BlockSpec((tm, tk_d), lambda i, j, k: (i, k)),
                  pl.BlockSpec((tk_d, tn_f), lambda i, j, k: (k, j))],
        out_shape=[jax.ShapeDtypeStruct((s, d), F32)],
        out_specs=[pl.BlockSpec((tm, tn_f), lambda i, j, k: (i, j))],
        pairs=[(0, 1, 0, None)], dims=NN, acc_shapes=[(tm, tn_f)],
        epilogue=lambda accs, ins, outs, ids: outs[0].__setitem__(Ellipsis, accs[0]),
    )[0][0]

    dx2, dx2b, loss_part, g_fnw =_head(x2d, o, tgt, final_norm_w.reshape(1, d))

    def gate_bwd_epilogue(accs, ins, outs, ids):
        dm = accs[0]
        gb = ins[5][...]
        i = ids[1]

        @pl.when(i == 0)
        def _():
            outs[2][...] = jnp.zeros(outs[2].shape, F32)

        for n in range(2):
            gate = _sigmoid(ins[3 + n][...].astype(F32) + gb[n:n + 1])
            outs[0][n] = (dm * gate).astype(BF16)
            dgl = dm * ins[2][n].astype(F32) * gate * (1.0 - gate)
            outs[1][n] = dgl.astype(BF16)
            outs[2][n:n + 1, :] += jnp.sum(dgl, axis=0, keepdims=True)

    tm_b = _pick(s, 256, 16)
    (d_br, dgl, g_gb), _ = _mm(
        "out_proj_bwd_gate", grid=(d // tn_d, s // tm_b, d // tk_d),
        operands=[dx2b, wg_out, br, proj, proj, wg_gb],
        in_specs=[pl.BlockSpec((tm_b, tk_d), lambda j, i, k: (i, k)),
                  pl.BlockSpec((tn_d, tk_d), lambda j, i, k: (j, k)),
                  pl.BlockSpec((2, tm_b, tn_d), lambda j, i, k: (0, i, j)),
                  *gate_specs(lambda j, i, k: (i, j), tm_b)],
        out_shape=[jax.ShapeDtypeStruct((2, s, d), BF16), jax.ShapeDtypeStruct((2, s, d), BF16),
                   jax.ShapeDtypeStruct((2, d), F32)],
        out_specs=[pl.BlockSpec((2, tm_b, tn_d), lambda j, i, k: (0, i, j)),
                   pl.BlockSpec((2, tm_b, tn_d), lambda j, i, k: (0, i, j)),
                   pl.BlockSpec((2, tn_d), lambda j, i, k: (0, j))],
        pairs=[(0, 1, 0, None)], dims=NT, acc_shapes=[(tm_b, tn_d)],
        epilogue=gate_bwd_epilogue, semantics=("parallel", "arbitrary", "arbitrary"),
        temp_bytes=8 * tm_b * tn_d * 4,
    )

    hh_out = d // 8
    tm_o = _pick(hh_out, 512, LANES)
    nb_o = hh_out // tm_o
    tk_s = _pick(s, 4096, LANES)
    g_out = _mm(
        "w_out_grad", grid=(d // tm_o, d // tn_f, s // tk_s), operands=[merged, dx2b],
        in_specs=[pl.BlockSpec((tk_s, tm_o), lambda i, j, k: (k, i)),
                  pl.BlockSpec((tk_s, tn_f), lambda i, j, k: (k, j))],
        out_shape=[jax.ShapeDtypeStruct((2, N_CHIPS, hh_out, d), BF16)],
        out_specs=[pl.BlockSpec((None, None, tm_o, tn_f),
                                lambda i, j, k: ((i // nb_o) % 2, i // (2 * nb_o), i % nb_o, j))],
        pairs=[(0, 1, 0, None)], dims=TN, acc_shapes=[(tm_o, tn_f)],
        epilogue=lambda accs, ins, outs, ids: outs[0].__setitem__(Ellipsis, accs[0].astype(BF16)),
    )[0][0]

    tn_w = _pick(w, 2048, LANES)
    tk_q = _pick(dq, 1024, LANES)
    qk = dq // tk_q
    dys = _mm(
        "branch_bwd", grid=(2, s // tm, w // tn_w, d // tk_q), operands=[d_br, wg_br],
        in_specs=[pl.BlockSpec((None, tm, tk_q), lambda n, i, j, k: (n, i, k)),
                  pl.BlockSpec((None, None, tn_w, tk_q), lambda n, i, j, k: (k // qk, n, j, k % qk))],
        out_shape=[jax.ShapeDtypeStruct((2, s, w), F32)],
        out_specs=[pl.BlockSpec((None, tm, tn_w), lambda n, i, j, k: (n, i, j))],
        pairs=[(0, 1, 0, None)], dims=NT, acc_shapes=[(tm, tn_w)],
        epilogue=lambda accs, ins, outs, ids: outs[0].__setitem__(Ellipsis, accs[0]),
    )[0][0]

    tm_w = _pick(w, 1024, LANES)
    g_br = _mm(
        "w_branch_grad", grid=(2, w // tm_w, d // tn_d, s // tk_s), operands=[ys, d_br],
        in_specs=[pl.BlockSpec((None, tk_s, tm_w), lambda n, i, j, k: (n, k, i)),
                  pl.BlockSpec((None, tk_s, tn_d), lambda n, i, j, k: (n, k, j))],
        out_shape=[jax.ShapeDtypeStruct((2, N_CHIPS, w, dq), BF16)],
        out_specs=[pl.BlockSpec((None, None, tm_w, tn_d), lambda n, i, j, k: (n, j // qd, i, j % qd))],
        pairs=[(0, 1, 0, None)], dims=TN, acc_shapes=[(tm_w, tn_d)],
        epilogue=lambda accs, ins, outs, ids: outs[0].__setitem__(Ellipsis, accs[0].astype(BF16)),
    )[0][0]

    dpa, g_pool, g_ps, g_cw, g_cb = _mix_bwd(proj, dys, wg_pool, pool_scale, wg_cw, conv_b, w)
    g_pool = g_pool.astype(BF16).reshape(2, N_CHIPS, 2 * (cg // N_CHIPS), cg)

    def chip_partials(nms, grads):
        recv = _pair_exchange("pair_exchange_" + nms[0], grads)
        parts = []
        for nm, g, r1 in zip(nms, grads, recv):
            _, nc, hh, wd = g.shape
            pt = _pair_add("pair_add_" + nm, g.reshape(2, nc * hh, wd), r1.reshape(nc * hh, wd), c_idx)
            parts.append(pt.reshape(nc, hh, wd))
        return parts

    parts_early = chip_partials(names[1:], [g_out, g_br, g_pool])

    na = (6 * w) // tn_p
    qg = d // tn_p
    hh_in = d // 2
    assert s == tk_s
    tm_i = _pick(hh_in, 1024, LANES)
    tn_i = _pick(tn_p, 512, LANES)
    from_sibling, recv_early = _w_in_grad_sibling(
        h, dpa, dgl, c_idx, _scatter_partials(parts_early), tm=tm_i, tn=tn_i, rows=hh_in, pq=pq)

    tm_o2 = _pick(hh_in, 512, LANES)
    nb_o2 = hh_in // tm_o2

    def own_rows(name, first, count, comm):
        def add_sibling(accs, ins, outs, ids):
            outs[0][...] = (accs[0] + ins[3][...].astype(F32)).astype(BF16)

        return _mm(
            name, grid=(p // tn_p, count, 1), operands=[h, dpa, dgl, from_sibling], comm=comm, prefetch=c_idx,
            in_specs=[pl.BlockSpec((tk_s, tm_o2), lambda j, i, k, cr: (0, cr[0] * nb_o2 + first + i)),
                      pl.BlockSpec((tk_s, tn_p), lambda j, i, k, cr: (0, jnp.minimum(j, na - 1))),
                      pl.BlockSpec((None, tk_s, tn_p),
                                   lambda j, i, k, cr: (jnp.maximum(j - na, 0) // qg, 0, jnp.maximum(j - na, 0) % qg)),
                      pl.BlockSpec((None, tm_o2, tn_p), lambda j, i, k, cr: (j // qp, first + i, j % qp))],
            out_shape=[jax.ShapeDtypeStruct((N_CHIPS, count * tm_o2, pq), BF16)],
            out_specs=[pl.BlockSpec((None, tm_o2, tn_p), lambda j, i, k, cr: (j // qp, i, j % qp))],
            pairs=[(0, 1, 0, lambda ids: ids[0] < na), (0, 2, 0, lambda ids: ids[0] >= na)],
            dims=TN, acc_shapes=[(tm_o2, tn_p)], epilogue=add_sibling)

    n_first = 1 if nb_o2 > 1 else 0
    if n_first:
        (part_a,), _ = own_rows("w_in_grad_own_a", 0, n_first, None)
        scatter_a = _scatter_partials([part_a], rows=hh_in)
        (part_b,), land_a = own_rows("w_in_grad_own_b", n_first, nb_o2 - n_first, scatter_a)
        scatter_b = _scatter_partials([part_b], rows=hh_in, row0=n_first * tm_o2, prev=land_a)
    else:
        (part_b,), _ = own_rows("w_in_grad_own", 0, nb_o2, None)
        scatter_b = _scatter_partials([part_b])

    tn_h = _pick(d, 2048, LANES)
    (dh,), recv_in = _mm(
        "proj_bwd", grid=(s // tm, d // tn_h, p // tn_p), operands=[dpa, dgl, wg_in],
        comm=scatter_b,
        in_specs=[pl.BlockSpec((tm, tn_p), lambda i, j, k: (i, jnp.minimum(k, na - 1))),
                  pl.BlockSpec((None, tm, tn_p),
                               lambda i, j, k: (jnp.maximum(k - na, 0) // qg, i, jnp.maximum(k - na, 0) % qg)),
                  pl.BlockSpec((None, tn_h, tn_p), lambda i, j, k: (k // qp, j, k % qp))],
        out_shape=[jax.ShapeDtypeStruct((s, d), F32)],
        out_specs=[pl.BlockSpec((tm, tn_h), lambda i, j, k: (i, j))],
        pairs=[(0, 2, 0, lambda ids: ids[2] < na), (1, 2, 0, lambda ids: ids[2] >= na)],
        dims=NT, acc_shapes=[(tm, tn_h)],
        epilogue=lambda accs, ins, outs, ids: outs[0].__setitem__(Ellipsis, accs[0]),
    )

    grad_x, g_nw = _rms_bwd(x2d, dh, dx2, norm_w)

    if n_first:
        half_in = _sum_chips("sum_chips_in_a", part_a, recv_in[0], chip_idx)
        half_in = _sum_chips("sum_chips_in_b", part_b, recv_in[0], chip_idx, row0=n_first * tm_o2, prev=half_in)
    else:
        half_in = _sum_chips("sum_chips_in", part_b, recv_in[0], chip_idx)
    halves = [half_in] + [_sum_chips("sum_chips_" + nm, pt, r2, chip_idx)
                          for nm, pt, r2 in zip(names[1:], parts_early, recv_early)]
    others = _pair_share(halves)

    g_cw_full = jnp.transpose(g_cw, (1, 0, 2)).reshape(CONV_K, w)
    small_shapes = [(LANES,), (1, d), (1, d), (1, w), (1, w), (CONV_K, w), (2, d)]
    pack = _flat_pack([loss_part[0], g_nw, g_fnw, g_ps, g_cb, g_cw_full, g_gb])
    total = _sum_devices(_gather_packs(pack))
    t_loss, t_nw, t_fnw, t_ps, t_cb, t_cw, t_gb = _unpack(total, small_shapes)
    loss = t_loss[0]
    t_cw = lax.dynamic_slice_in_dim(t_cw, chip * cg, cg, axis=1)
    t_gb = lax.dynamic_slice_in_dim(t_gb, chip * dq, dq, axis=1)

    out = {}
    big_names = ["w_in", "w_out", "w_branch", "pool_w"]
    big_m = [m_w_in, m_w_out, m_w_branch, m_pool_w]
    big_v = [v_w_in, v_w_out, v_w_branch, v_pool_w]
    big_orig = [w_in, w_out, w_branch, pool_w]
    for nm, g_own, g_other, w2, mm_, vv_, orig in zip(big_names, halves, others, big_w, big_m, big_v, big_orig):
        sh = (2,) + g_own.shape
        res = _adamw_halves("adamw_" + nm, g_own, g_other, c_idx,
                            w2.reshape(sh), mm_.reshape(sh), vv_.reshape(sh))
        out[nm] = [r.reshape(orig.shape) for r in res]

    sm_names = ["norm_w", "final_norm_w", "pool_scale", "conv_b", "conv_w", "gate_b"]
    sm_g = [t_nw, t_fnw, t_ps, t_cb, t_cw, t_gb]
    sm_w = [norm_w, final_norm_w, pool_scale, conv_b, conv_w, gate_b]
    sm_m = [m_norm_w, m_final_norm_w, m_pool_scale, m_conv_b, m_conv_w, m_gate_b]
    sm_v = [v_norm_w, v_final_norm_w, v_pool_scale, v_conv_b, v_conv_w, v_gate_b]
    sm_shapes = [a.shape for a in sm_w]
    res = _adamw("adamw_small", _flat_pack(sm_g), _flat_pack(sm_w), _flat_pack(sm_m), _flat_pack(sm_v))
    res = [_unpack(r, sm_shapes) for r in res]
    for idx, nm in enumerate(sm_names):
        out[nm] = [r[idx] for r in res]

    order = ["norm_w", "w_in", "pool_w", "pool_scale", "conv_w", "conv_b", "gate_b", "w_branch", "w_out",
             "final_norm_w"]
    outs = [loss, grad_x.reshape(x.shape)]
    for kind in range(4):
        outs += [out[nm][kind] for nm in order]
    return tuple(outs)
```

```python
import math

import jax
import jax.numpy as jnp
from jax import lax
from jax.experimental import pallas as pl
from jax.experimental.pallas import tpu as pltpu

F32 = jnp.float32
BF16 = jnp.bfloat16

NORM_EPS = 1e-6
POOL_WINDOWS = (2, 4, 8, 16)
N_POOL_GROUPS = len(POOL_WINDOWS)
CONV_K = 3
ADAM_LR = 0.001
ADAM_B1 = 0.9
ADAM_B2 = 0.999
ADAM_EPS = 1e-08
ADAM_WD = 0.01
ADAM_STEP = 10

N_CHIPS = 4
N_DEV = 8
HALO = 16
LANES = 128
V7X_VMEM_BYTES = 64 * 1024 * 1024
VMEM_CAP = V7X_VMEM_BYTES - 8 * 1024 * 1024

MESH = pl.DeviceIdType.MESH
ANY = pl.BlockSpec(memory_space=pl.ANY)

NN = (((1,), (0,)), ((), ()))
NT = (((1,), (1,)), ((), ()))
TN = (((0,), (0,)), ((), ()))


def _pick(dim, pref, align):
    if dim <= pref:
        return dim
    t = (pref // align) * align
    while t >= align:
        if dim % t == 0:
            return t
        t -= align
    raise ValueError(f"no tile for {dim} (pref {pref}, align {align})")


def _nbytes(shape, dtype):
    n = 1
    for s in shape:
        if s is not None:
            n *= s
    return n * jnp.dtype(dtype).itemsize


def _params(semantics, block_bytes, extra_bytes=0):
    need = 2 * block_bytes + extra_bytes + (2 << 20)
    return pltpu.CompilerParams(dimension_semantics=semantics,
                                vmem_limit_bytes=int(min(max(need, 16 << 20), VMEM_CAP)))


def _sigmoid(z):
    return jax.nn.sigmoid(z)


def _mm(name, *, grid, operands, in_specs, out_shape, out_specs, pairs, dims, acc_shapes, epilogue,
        semantics=None, temp_bytes=0, comm=None, prefetch=None, aliases=None):
    aliases = dict(aliases or {})
    n_in, n_out = len(operands), len(out_shape)
    kax = len(grid) - 1
    nk = grid[kax]
    single = nk == 1
    conditional = any(p[3] is not None for p in pairs)
    if single and conditional:
        assert len(acc_shapes) == 1 and all(p[3] is not None for p in pairs)
    n_acc = 0 if single else len(acc_shapes)
    n_pf = 0 if prefetch is None else 1
    c_ops = list(comm.operands) if comm else []
    c_out = list(comm.out_shape) if comm else []
    c_sems = list(comm.scratch) if comm else []
    c_alias = dict(comm.aliases) if comm else {}
    n_ci, n_co = len(c_ops), len(c_out)

    def product(ins, ai, bi):
        return lax.dot_general(ins[ai][...], ins[bi][...], dims, preferred_element_type=F32)

    def body(*refs):
        refs = refs[n_pf:]
        ins = refs[:n_in]
        c_in_refs = refs[n_in:n_in + n_ci]
        outs = refs[n_in + n_ci:n_in + n_ci + n_out]
        c_out_refs = refs[n_in + n_ci + n_out:n_in + n_ci + n_out + n_co]
        scratch = refs[n_in + n_ci + n_out + n_co:]
        accs = scratch[:n_acc]
        sems = scratch[n_acc:]
        ids = [pl.program_id(a) for a in range(len(grid))]
        k = ids[kax]

        if comm:
            is_first = ids[0] == 0
            is_last = ids[0] == grid[0] - 1
            for a in range(1, len(grid)):
                is_first = jnp.logical_and(is_first, ids[a] == 0)
                is_last = jnp.logical_and(is_last, ids[a] == grid[a] - 1)

            @pl.when(is_first)
            def _():
                comm.start(c_in_refs, c_out_refs, sems)

        if single and conditional:
            for (ai, bi, ci, cond) in pairs:
                def only(ai=ai, bi=bi):
                    epilogue([product(ins, ai, bi)], ins, outs, ids)
                pl.when(cond(ids))(only)
        elif single:
            vals = [None] * len(acc_shapes)
            for (ai, bi, ci, cond) in pairs:
                r = product(ins, ai, bi)
                vals[ci] = r if vals[ci] is None else vals[ci] + r
            epilogue(vals, ins, outs, ids)
        else:
            @pl.when(k == 0)
            def _():
                for a in accs:
                    a[...] = jnp.zeros(a.shape, a.dtype)

            for (ai, bi, ci, cond) in pairs:
                def step(ai=ai, bi=bi, ci=ci):
                    accs[ci][...] += product(ins, ai, bi)
                if cond is None:
                    step()
                else:
                    pl.when(cond(ids))(step)

            @pl.when(k == nk - 1)
            def _():
                epilogue([a[...] for a in accs], ins, outs, ids)

        if comm:
            @pl.when(is_last)
            def _():
                comm.finish(c_in_refs, c_out_refs, sems)

    if semantics is None:
        semantics = ("parallel",) * kax + ("arbitrary",)
    if comm:
        semantics = ("arbitrary",) * len(grid)
    in_specs = [ANY if idx in aliases else spec for idx, spec in enumerate(in_specs)]
    blk = 0
    for idx, (spec, op) in enumerate(zip(in_specs, operands)):
        if idx not in aliases:
            blk += _nbytes(spec.block_shape, op.dtype)
    for spec, o in zip(out_specs, out_shape):
        blk += _nbytes(spec.block_shape, o.dtype)
    acc_bytes = sum(_nbytes(s, F32) for s in acc_shapes)
    io_alias = {n_pf + i: o for i, o in aliases.items()}
    io_alias.update({n_pf + n_in + i: n_out + o for i, o in c_alias.items()})
    all_in = list(in_specs) + [ANY] * n_ci
    all_out = list(out_specs) + [ANY] * n_co
    scratch_shapes = [pltpu.VMEM(s, F32) for s in acc_shapes[:n_acc]] + c_sems
    params = _params(semantics, blk, 3 * acc_bytes + temp_bytes)
    shapes = list(out_shape) + c_out
    if prefetch is None:
        call = pl.pallas_call(
            body, name=name, grid=grid, in_specs=all_in, out_specs=all_out, out_shape=shapes,
            scratch_shapes=scratch_shapes, input_output_aliases=io_alias, compiler_params=params)
        res = call(*operands, *c_ops)
    else:
        call = pl.pallas_call(
            body, name=name, out_shape=shapes, input_output_aliases=io_alias, compiler_params=params,
            grid_spec=pltpu.PrefetchScalarGridSpec(
                num_scalar_prefetch=1, grid=grid, in_specs=all_in, out_specs=all_out,
                scratch_shapes=scratch_shapes))
        res = call(prefetch, *operands, *c_ops)
    return res[:n_out], res[n_out:]


def _cast_bf16(name, w2d):
    r, c = w2d.shape
    tr = _pick(r, max(16, (4 << 20) // (4 * c)), 16)

    def body(w_ref, o_ref):
        o_ref[...] = w_ref[...].astype(BF16)

    return pl.pallas_call(
        body, name=name, grid=(r // tr,),
        in_specs=[pl.BlockSpec((tr, c), lambda i: (i, 0))],
        out_specs=pl.BlockSpec((tr, c), lambda i: (i, 0)),
        out_shape=jax.ShapeDtypeStruct((r, c), BF16),
        compiler_params=_params(("parallel",), tr * c * 6),
    )(w2d)


def _rms_fwd(x, norm_w):
    s, d = x.shape
    ts = _pick(s, 256, 16)

    def body(x_ref, w_ref, h_ref):
        xv = x_ref[...]
        rstd = lax.rsqrt(jnp.mean(xv * xv, axis=-1, keepdims=True) + NORM_EPS)
        h_ref[...] = (xv * rstd * w_ref[...]).astype(BF16)

    return pl.pallas_call(
        body, name="rms_fwd", grid=(s // ts,),
        in_specs=[pl.BlockSpec((ts, d), lambda i: (i, 0)), pl.BlockSpec((1, d), lambda i: (0, 0))],
        out_specs=pl.BlockSpec((ts, d), lambda i: (i, 0)),
        out_shape=jax.ShapeDtypeStruct((s, d), BF16),
        compiler_params=_params(("parallel",), ts * d * 6, 3 * ts * d * 4),
    )(x, norm_w)


def _head(x, o, target, fnw):
    s, d = x.shape
    ts = _pick(s, 128, 16)

    def body(x_ref, o_ref, t_ref, w_ref, dx_ref, dxb_ref, loss_ref, gw_ref):
        i = pl.program_id(0)

        @pl.when(i == 0)
        def _():
            loss_ref[...] = jnp.zeros(loss_ref.shape, F32)
            gw_ref[...] = jnp.zeros(gw_ref.shape, F32)

        w = w_ref[...]
        x2 = x_ref[...] + o_ref[...]
        rstd = lax.rsqrt(jnp.mean(x2 * x2, axis=-1, keepdims=True) + NORM_EPS)
        n = x2 * rstd
        e = n * w - t_ref[...]
        loss_ref[...] += 0.5 * jnp.sum(e * e) / d
        dy = e / d
        gw_ref[...] += jnp.sum(dy * n, axis=0, keepdims=True)
        gy = dy * w
        dx = rstd * (gy - n * jnp.mean(gy * n, axis=-1, keepdims=True))
        dx_ref[...] = dx
        dxb_ref[...] = dx.astype(BF16)

    row = pl.BlockSpec((ts, d), lambda i: (i, 0))
    return pl.pallas_call(
        body, name="head", grid=(s // ts,),
        in_specs=[row, row, row, pl.BlockSpec((1, d), lambda i: (0, 0))],
        out_specs=[row, row, pl.BlockSpec((8, LANES), lambda i: (0, 0)),
                   pl.BlockSpec((1, d), lambda i: (0, 0))],
        out_shape=[jax.ShapeDtypeStruct((s, d), F32), jax.ShapeDtypeStruct((s, d), BF16),
                   jax.ShapeDtypeStruct((8, LANES), F32), jax.ShapeDtypeStruct((1, d), F32)],
        compiler_params=_params(("arbitrary",), ts * d * 22, 6 * ts * d * 4),
    )(x, o, target, fnw)


def _rms_bwd(x, dh, dx2, norm_w):
    s, d = x.shape
    ts = _pick(s, 128, 16)

    def body(x_ref, dh_ref, dx2_ref, w_ref, gx_ref, gw_ref):
        i = pl.program_id(0)

        @pl.when(i == 0)
        def _():
            gw_ref[...] = jnp.zeros(gw_ref.shape, F32)

        xv = x_ref[...]
        rstd = lax.rsqrt(jnp.mean(xv * xv, axis=-1, keepdims=True) + NORM_EPS)
        n = xv * rstd
        dhv = dh_ref[...]
        gw_ref[...] += jnp.sum(dhv * n, axis=0, keepdims=True)
        gh = dhv * w_ref[...]
        gx_ref[...] = dx2_ref[...] + rstd * (gh - n * jnp.mean(gh * n, axis=-1, keepdims=True))

    row = pl.BlockSpec((ts, d), lambda i: (i, 0))
    vec = pl.BlockSpec((1, d), lambda i: (0, 0))
    return pl.pallas_call(
        body, name="rms_bwd", grid=(s // ts,),
        in_specs=[row, row, row, vec], out_specs=[row, vec],
        out_shape=[jax.ShapeDtypeStruct((s, d), F32), jax.ShapeDtypeStruct((1, d), F32)],
        compiler_params=_params(("arbitrary",), ts * d * 16, 5 * ts * d * 4),
    )(x, dh, dx2, norm_w)


def _silu(z):
    return z * _sigmoid(z)


def _dsilu(z):
    sg = _sigmoid(z)
    return sg * (1.0 + z * (1.0 - sg))


def _window_sum(ext, window, back):
    n = ext.shape[0]
    acc = ext
    step = 1
    while step < window:
        acc = acc + pltpu.roll(acc, step if back else n - step, 0)
        step *= 2
    return acc


def _shift_rows(ext, k, back):
    n = ext.shape[0]
    return pltpu.roll(ext, k if back else n - k, 0)


def _mix_fwd(proj, pool_wg, pool_scale, conv_wg, conv_b, width):
    s = proj.shape[0]
    w = width
    cg = w // N_POOL_GROUPS
    ts = _pick(s, 128, HALO)
    hb = ts // HALO
    cols = 6 * w

    def body(p_ref, ph_ref, pw_ref, ps_ref, cw_ref, cb_ref, ys_ref):
        i = pl.program_id(0)
        first = i == 0
        t1 = (i * ts + lax.broadcasted_iota(jnp.int32, (ts, 1), 0) + 1).astype(F32)

        def tile(part, g):
            lo = part * w + g * cg
            return p_ref[:, lo:lo + cg].astype(F32)

        def prev(part, g):
            lo = part * w + g * cg
            return jnp.where(first, 0.0, ph_ref[:, lo:lo + cg].astype(F32))

        for g, win in enumerate(POOL_WINDOWS):
            gs = slice(g * cg, (g + 1) * cg)
            u = tile(0, g)
            ext = jnp.concatenate([prev(0, g), u], axis=0)
            wsum = _window_sum(ext, win, True)[HALO:]
            pooled = wsum / jnp.minimum(t1, float(win)) - u
            pw = pw_ref[:, g].reshape(cg, cg)
            mixed = jnp.dot(pooled.astype(BF16), pw, preferred_element_type=F32)
            ys_ref[0, :, gs] = (mixed * ps_ref[:, gs] * _silu(tile(1, g))).astype(BF16)
            v = tile(4, g) * tile(2, g)
            vext = jnp.concatenate([prev(4, g) * prev(2, g), v], axis=0)
            v1 = _shift_rows(vext, 1, True)[HALO:]
            v2 = _shift_rows(vext, 2, True)[HALO:]
            cw = [cw_ref[g, tap:tap + 1, :] for tap in range(CONV_K)]
            y = cb_ref[:, gs] + cw[0] * v2 + cw[1] * v1 + cw[2] * v
            ys_ref[1, :, gs] = (tile(3, g) * y * _silu(tile(5, g))).astype(BF16)

    return pl.pallas_call(
        body, name="mix_fwd", grid=(s // ts,),
        in_specs=[pl.BlockSpec((ts, cols), lambda i: (i, 0)),
                  pl.BlockSpec((HALO, cols), lambda i: (jnp.maximum(i * hb - 1, 0), 0)),
                  pl.BlockSpec(pool_wg.shape, lambda i: (0, 0, 0, 0)),
                  pl.BlockSpec((1, w), lambda i: (0, 0)),
                  pl.BlockSpec(conv_wg.shape, lambda i: (0, 0, 0)),
                  pl.BlockSpec((1, w), lambda i: (0, 0))],
        out_specs=pl.BlockSpec((2, ts, w), lambda i: (0, i, 0)),
        out_shape=jax.ShapeDtypeStruct((2, s, w), BF16),
        compiler_params=_params(("parallel",), (ts + HALO) * cols * 2 + 2 * ts * w * 2
                                + _nbytes(pool_wg.shape, BF16), 24 * (ts + HALO) * cg * 4),
    )(proj, proj, pool_wg, pool_scale, conv_wg, conv_b)


def _mix_bwd(proj, dys, pool_wg, pool_scale, conv_wg, conv_b, width):
    s = proj.shape[0]
    w = width
    cg = w // N_POOL_GROUPS
    ts = _pick(s, 128, HALO)
    hb = ts // HALO
    n_tiles = s // ts
    last_hb = s // HALO - 1
    cols = 6 * w

    def body(p_ref, ph_ref, pn_ref, dy_ref, dyn_ref, pw_ref, ps_ref, cw_ref, cb_ref,
             dp_ref, dpw_ref, dps_ref, dcw_ref, dcb_ref):
        i = pl.program_id(0)
        first = i == 0
        last = i == n_tiles - 1

        @pl.when(first)
        def _():
            dpw_ref[...] = jnp.zeros(dpw_ref.shape, F32)
            dps_ref[...] = jnp.zeros(dps_ref.shape, F32)
            dcw_ref[...] = jnp.zeros(dcw_ref.shape, F32)
            dcb_ref[...] = jnp.zeros(dcb_ref.shape, F32)

        row = i * ts + lax.broadcasted_iota(jnp.int32, (ts + HALO, 1), 0)
        t1_ext = (row + 1).astype(F32)
        t1 = t1_ext[:ts]

        def tile(part, g):
            lo = part * w + g * cg
            return p_ref[:, lo:lo + cg].astype(F32)

        def prev(part, g):
            lo = part * w + g * cg
            return jnp.where(first, 0.0, ph_ref[:, lo:lo + cg].astype(F32))

        def ahead(part, g):
            lo = part * w + g * cg
            return jnp.concatenate([tile(part, g), pn_ref[:, lo:lo + cg].astype(F32)], axis=0)

        def dy_ahead(n, g):
            gs = slice(g * cg, (g + 1) * cg)
            nxt = jnp.where(last, 0.0, dyn_ref[n, :, gs])
            return jnp.concatenate([dy_ref[n, :, gs], nxt], axis=0)

        for g, win in enumerate(POOL_WINDOWS):
            gs = slice(g * cg, (g + 1) * cg)
            u = tile(0, g)
            ext = jnp.concatenate([prev(0, g), u], axis=0)
            pooled = _window_sum(ext, win, True)[HALO:] / jnp.minimum(t1, float(win)) - u
            pooled_b = pooled.astype(BF16)
            pw = pw_ref[:, g].reshape(cg, cg)
            mixed = jnp.dot(pooled_b, pw, preferred_element_type=F32)
            zp_ext = ahead(1, g)
            dy0_ext = dy_ahead(0, g)
            scale = ps_ref[:, gs]
            dms_ext = dy0_ext * _silu(zp_ext)
            dmix_b = (dms_ext * scale).astype(BF16)
            dpooled_ext = lax.dot_general(dmix_b, pw, NT, preferred_element_type=F32)
            dy0 = dy0_ext[:ts]
            zp = zp_ext[:ts]
            dp_ref[:, w + g * cg:w + (g + 1) * cg] = (dy0 * mixed * scale * _dsilu(zp)).astype(BF16)
            dps_ref[:, gs] += jnp.sum(dms_ext[:ts] * mixed, axis=0, keepdims=True)
            dpw = lax.dot_general(pooled_b, dmix_b[:ts], TN, preferred_element_type=F32)
            dpw_ref[g // 2, :, g % 2] += dpw.reshape(N_CHIPS, cg // N_CHIPS, cg)
            q_ext = dpooled_ext / jnp.minimum(t1_ext, float(win))
            du = _window_sum(q_ext, win, False)[:ts] - dpooled_ext[:ts]
            dp_ref[:, gs] = du.astype(BF16)
            uc = tile(2, g)
            cc = tile(4, g)
            v = cc * uc
            vext = jnp.concatenate([prev(4, g) * prev(2, g), v], axis=0)
            v1 = _shift_rows(vext, 1, True)[HALO:]
            v2 = _shift_rows(vext, 2, True)[HALO:]
            cw = [cw_ref[g, tap:tap + 1, :] for tap in range(CONV_K)]
            y = cb_ref[:, gs] + cw[0] * v2 + cw[1] * v1 + cw[2] * v
            bc_ext = ahead(3, g)
            zc_ext = ahead(5, g)
            dy1_ext = dy_ahead(1, g)
            dyy_ext = dy1_ext * bc_ext * _silu(zc_ext)
            dy1 = dy1_ext[:ts]
            bc = bc_ext[:ts]
            zc = zc_ext[:ts]
            dp_ref[:, 3 * w + g * cg:3 * w + (g + 1) * cg] = (dy1 * y * _silu(zc)).astype(BF16)
            dp_ref[:, 5 * w + g * cg:5 * w + (g + 1) * cg] = (dy1 * bc * y * _dsilu(zc)).astype(BF16)
            dyy = dyy_ext[:ts]
            dcb_ref[:, gs] += jnp.sum(dyy, axis=0, keepdims=True)
            for tap, vt in enumerate((v2, v1, v)):
                dcw_ref[g, tap:tap + 1, :] += jnp.sum(dyy * vt, axis=0, keepdims=True)
            dv = (cw[2] * dyy + cw[1] * _shift_rows(dyy_ext, 1, False)[:ts]
                  + cw[0] * _shift_rows(dyy_ext, 2, False)[:ts])
            dp_ref[:, 4 * w + g * cg:4 * w + (g + 1) * cg] = (dv * uc).astype(BF16)
            dp_ref[:, 2 * w + g * cg:2 * w + (g + 1) * cg] = (dv * cc).astype(BF16)

    dpw_shape = (2, N_CHIPS, 2, cg // N_CHIPS, cg)
    return pl.pallas_call(
        body, name="mix_bwd", grid=(n_tiles,),
        in_specs=[pl.BlockSpec((ts, cols), lambda i: (i, 0)),
                  pl.BlockSpec((HALO, cols), lambda i: (jnp.maximum(i * hb - 1, 0), 0)),
                  pl.BlockSpec((HALO, cols), lambda i: (jnp.minimum((i + 1) * hb, last_hb), 0)),
                  pl.BlockSpec((2, ts, w), lambda i: (0, i, 0)),
                  pl.BlockSpec((2, HALO, w), lambda i: (0, jnp.minimum((i + 1) * hb, last_hb), 0)),
                  pl.BlockSpec(pool_wg.shape, lambda i: (0, 0, 0, 0)),
                  pl.BlockSpec((1, w), lambda i: (0, 0)),
                  pl.BlockSpec(conv_wg.shape, lambda i: (0, 0, 0)),
                  pl.BlockSpec((1, w), lambda i: (0, 0))],
        out_specs=[pl.BlockSpec((ts, cols), lambda i: (i, 0)),
                   pl.BlockSpec(dpw_shape, lambda i: (0, 0, 0, 0, 0)),
                   pl.BlockSpec((1, w), lambda i: (0, 0)),
                   pl.BlockSpec(conv_wg.shape, lambda i: (0, 0, 0)),
                   pl.BlockSpec((1, w), lambda i: (0, 0))],
        out_shape=[jax.ShapeDtypeStruct((s, cols), BF16), jax.ShapeDtypeStruct(dpw_shape, F32),
                   jax.ShapeDtypeStruct((1, w), F32), jax.ShapeDtypeStruct(conv_wg.shape, F32),
                   jax.ShapeDtypeStruct((1, w), F32)],
        compiler_params=_params(("arbitrary",), (2 * ts + 2 * HALO) * cols * 2 + (ts + HALO) * w * 8
                                + _nbytes(pool_wg.shape, BF16) + _nbytes(dpw_shape, F32),
                                40 * (ts + HALO) * cg * 4),
    )(proj, proj, proj, dys, dys, pool_wg, pool_scale, conv_wg, conv_b)


def _adamw(name, g, w, m, v):
    r, c = w.shape
    tr = _pick(r, max(8, (1 << 20) // (4 * c)), 8)

    def body(g_ref, w_ref, m_ref, v_ref, go_ref, d_ref, mo_ref, vo_ref):
        gv = g_ref[...]
        mn = ADAM_B1 * m_ref[...] + (1.0 - ADAM_B1) * gv
        vn = ADAM_B2 * v_ref[...] + (1.0 - ADAM_B2) * (gv * gv)
        m_hat = mn / (1.0 - ADAM_B1 ** ADAM_STEP)
        v_hat = vn / (1.0 - ADAM_B2 ** ADAM_STEP)
        go_ref[...] = gv
        d_ref[...] = -ADAM_LR * (m_hat / (jnp.sqrt(v_hat) + ADAM_EPS) + ADAM_WD * w_ref[...])
        mo_ref[...] = mn
        vo_ref[...] = vn

    blk = pl.BlockSpec((tr, c), lambda i: (i, 0))
    sh = jax.ShapeDtypeStruct((r, c), F32)
    return pl.pallas_call(
        body, name=name, grid=(r // tr,), in_specs=[blk] * 4, out_specs=[blk] * 4,
        out_shape=[sh] * 4, compiler_params=_params(("parallel",), tr * c * 32, 4 * tr * c * 4),
    )(g, w, m, v)


def _pair_add(name, g, r1, c_idx):
    _, r, c = g.shape
    tr = _pick(r, max(16, (2 << 20) // (2 * c)), 16)

    def body(c_ref, g_ref, r_ref, o_ref):
        o_ref[...] = (g_ref[...].astype(F32) + r_ref[...].astype(F32)).astype(BF16)

    return pl.pallas_call(
        body, name=name,
        grid_spec=pltpu.PrefetchScalarGridSpec(
            num_scalar_prefetch=1, grid=(r // tr,),
            in_specs=[pl.BlockSpec((None, tr, c), lambda i, cr: (cr[0], i, 0)),
                      pl.BlockSpec((tr, c), lambda i, cr: (i, 0))],
            out_specs=pl.BlockSpec((tr, c), lambda i, cr: (i, 0))),
        out_shape=jax.ShapeDtypeStruct((r, c), BF16),
        compiler_params=_params(("parallel",), tr * c * 6, 3 * tr * c * 4),
    )(c_idx, g, r1)


def _adamw_halves(name, g_own, g_other, c_idx, w, m, v):
    _, r, c = w.shape
    tr = _pick(r, max(8, (1 << 20) // (4 * c)), 8)

    def body(c_ref, go_ref, gt_ref, w_ref, m_ref, v_ref, g_out, d_ref, mo_ref, vo_ref):
        gv = jnp.where(pl.program_id(0) == c_ref[0], go_ref[...], gt_ref[...])
        mn = ADAM_B1 * m_ref[...] + (1.0 - ADAM_B1) * gv
        vn = ADAM_B2 * v_ref[...] + (1.0 - ADAM_B2) * (gv * gv)
        m_hat = mn / (1.0 - ADAM_B1 ** ADAM_STEP)
        v_hat = vn / (1.0 - ADAM_B2 ** ADAM_STEP)
        g_out[...] = gv
        d_ref[...] = -ADAM_LR * (m_hat / (jnp.sqrt(v_hat) + ADAM_EPS) + ADAM_WD * w_ref[...])
        mo_ref[...] = mn
        vo_ref[...] = vn

    blk = pl.BlockSpec((None, tr, c), lambda h, i, cr: (h, i, 0))
    sh = jax.ShapeDtypeStruct(w.shape, F32)
    return pl.pallas_call(
        body, name=name,
        grid_spec=pltpu.PrefetchScalarGridSpec(
            num_scalar_prefetch=1, grid=(2, r // tr),
            in_specs=[pl.BlockSpec((tr, c), lambda h, i, cr: (jnp.where(h == cr[0], i, 0), 0)),
                      pl.BlockSpec((tr, c), lambda h, i, cr: (jnp.where(h == cr[0], 0, i), 0)),
                      blk, blk, blk],
            out_specs=[blk] * 4),
        out_shape=[sh] * 4,
        compiler_params=_params(("arbitrary", "arbitrary"), tr * c * 36, 4 * tr * c * 4),
    )(c_idx, g_own, g_other, w, m, v)


def _sum_chips(name, part, r2, chip_idx, row0=0, prev=None):
    n, r, c = r2.shape
    rc = part.shape[1]
    tr = _pick(math.gcd(rc, row0) if row0 else rc, max(16, (1 << 20) // (2 * c)), 16)
    b0 = row0 // tr

    def body(ch_ref, own_ref, *rest):
        slots, o_ref = rest[:n], rest[-1]
        acc = None
        for s in range(n):
            term = jnp.where(ch_ref[0] == s, own_ref[...], slots[s][...]).astype(F32)
            acc = term if acc is None else acc + term
        o_ref[...] = acc

    def slot_spec(s):
        return pl.BlockSpec((None, tr, c), lambda i, ch: (jnp.where(ch[0] == s, (s + 1) % n, s), b0 + i, 0))

    extra = [] if prev is None else [prev]
    return pl.pallas_call(
        body, name=name,
        grid_spec=pltpu.PrefetchScalarGridSpec(
            num_scalar_prefetch=1, grid=(rc // tr,),
            in_specs=[pl.BlockSpec((None, tr, c), lambda i, ch: (ch[0], i, 0))]
            + [slot_spec(s) for s in range(n)] + [ANY] * len(extra),
            out_specs=pl.BlockSpec((tr, c), lambda i, ch: (b0 + i, 0))),
        out_shape=jax.ShapeDtypeStruct((r, c), F32),
        input_output_aliases={2 + n: 0} if extra else {},
        compiler_params=_params(("parallel",), tr * c * (2 * n + 6), 3 * tr * c * 4),
    )(chip_idx, part, *([r2] * n), *extra)


def _sum_devices(packs):
    n, r, c = packs.shape

    def body(p_ref, o_ref):
        acc = p_ref[0]
        for k in range(1, n):
            acc = acc + p_ref[k]
        o_ref[...] = acc

    return pl.pallas_call(
        body, name="sum_devices", out_shape=jax.ShapeDtypeStruct((r, c), F32),
        in_specs=[pl.BlockSpec(memory_space=pltpu.VMEM)],
        out_specs=pl.BlockSpec(memory_space=pltpu.VMEM),
    )(packs)


def _place():
    x, y, c = lax.axis_index("x"), lax.axis_index("y"), lax.axis_index("c")
    return x, y, c


def _chip_peers(x, y):
    out = []
    for k, (fx, fy) in enumerate(((0, 1), (1, 0), (1, 1))):
        px = 1 - x if fx else x
        py = 1 - y if fy else y
        out.append((k, px, py, 2 * px + py))
    return out


def _gather_weights(big, small, relay=False):
    nb, ns = len(big), len(small)

    class Copies:
        def __init__(self, c_in, c_out, sems):
            b_in, s_in = c_in[:nb], c_in[nb:]
            b_out, s_out = c_out[:nb], c_out[nb:]
            ici_s, ici_r, d2d_s, d2d_r, own_s, own_r = sems[:6]
            x, y, c = _place()
            chip = 2 * x + y
            sibling = (x, y, 1 - c)
            peers = _chip_peers(x, y)
            self.own = [pltpu.make_async_remote_copy(
                src_ref=b_in[t], dst_ref=b_out[t].at[chip], send_sem=own_s.at[t], recv_sem=own_r.at[t],
                device_id=sibling, device_id_type=MESH) for t in range(nb)]
            self.ici = [pltpu.make_async_remote_copy(
                src_ref=b_in[t].at[c], dst_ref=b_out[t].at[chip, c],
                send_sem=ici_s.at[t, k], recv_sem=ici_r.at[t, k],
                device_id=(px, py, c), device_id_type=MESH)
                for t in range(nb) for (k, px, py, pchip) in peers if not (relay and k == 2)]
            self.relay = []
            if relay:
                south = c == 0
                from_chip = jnp.where(south, 2 * x + (1 - y), 2 * (1 - x) + y)
                to = (jnp.where(south, 1 - x, x), jnp.where(south, y, 1 - y), c)
                self.relay = [pltpu.make_async_remote_copy(
                    src_ref=b_out[t].at[from_chip, c], dst_ref=b_out[t].at[from_chip, c],
                    send_sem=ici_s.at[t, 2], recv_sem=ici_r.at[t, 2],
                    device_id=to, device_id_type=MESH) for t in range(nb)]
            self.landed = [pltpu.make_async_remote_copy(
                src_ref=b_out[t].at[pchip, c], dst_ref=b_out[t].at[pchip, c],
                send_sem=ici_s.at[t, k], recv_sem=ici_r.at[t, k],
                device_id=sibling, device_id_type=MESH)
                for t in range(nb) for (k, px, py, pchip) in peers]
            self.passed = [pltpu.make_async_remote_copy(
                src_ref=b_out[t].at[pchip, c], dst_ref=b_out[t].at[pchip, c],
                send_sem=d2d_s.at[t, k], recv_sem=d2d_r.at[t, k],
                device_id=sibling, device_id_type=MESH)
                for t in range(nb) for (k, px, py, pchip) in peers]
            self.from_sibling = [pltpu.make_async_remote_copy(
                src_ref=b_out[t].at[pchip, 1 - c], dst_ref=b_out[t].at[pchip, 1 - c],
                send_sem=d2d_s.at[t, k], recv_sem=d2d_r.at[t, k],
                device_id=sibling, device_id_type=MESH)
                for t in range(nb) for (k, px, py, pchip) in peers]
            self.small, self.small_landed, self.local = [], [], []
            if ns:
                sm_s, sm_r, loc = sems[6:]
                self.local = [pltpu.make_async_copy(s_in[t], s_out[t].at[chip], loc.at[t]) for t in range(ns)]
                self.small = [pltpu.make_async_remote_copy(
                    src_ref=s_in[t], dst_ref=s_out[t].at[chip],
                    send_sem=sm_s.at[t, k], recv_sem=sm_r.at[t, k],
                    device_id=(px, py, c), device_id_type=MESH)
                    for t in range(ns) for (k, px, py, pchip) in peers]
                self.small_landed = [pltpu.make_async_remote_copy(
                    src_ref=s_in[t], dst_ref=s_out[t].at[pchip],
                    send_sem=sm_s.at[t, k], recv_sem=sm_r.at[t, k],
                    device_id=sibling, device_id_type=MESH)
                    for t in range(ns) for (k, px, py, pchip) in peers]

    def start(c_in, c_out, sems):
        cps = Copies(c_in, c_out, sems)
        for cp in cps.local + cps.own + cps.ici + cps.small:
            cp.start()

    def finish(c_in, c_out, sems):
        cps = Copies(c_in, c_out, sems)
        for t in range(nb):
            for k in range(3):
                if relay and k == 2:
                    cps.relay[t].start()
                cps.landed[3 * t + k].wait_recv()
                cps.passed[3 * t + k].start()
        for cp in cps.small_landed + cps.from_sibling:
            cp.wait_recv()
        for cp in cps.ici + cps.relay + cps.small + cps.passed:
            cp.wait_send()
        for cp in cps.own + cps.local:
            cp.wait()

    out_shape = [jax.ShapeDtypeStruct((N_CHIPS,) + b.shape, b.dtype) for b in big]
    out_shape += [jax.ShapeDtypeStruct((N_CHIPS,) + s.shape, s.dtype) for s in small]
    dma = pltpu.SemaphoreType.DMA
    scratch = [dma((nb, 3)), dma((nb, 3)), dma((nb, 3)), dma((nb, 3)), dma((nb,)), dma((nb,))]
    if ns:
        scratch += [dma((ns, 3)), dma((ns, 3)), dma((ns,))]
    return _Exchange(list(big) + list(small), out_shape, scratch, start, finish)


def _pair_exchange(name, grads):
    n = len(grads)

    def body(*refs):
        g_in = refs[:n]
        r_out = refs[n:2 * n]
        send, recv = refs[2 * n:]
        x, y, c = _place()
        cps = [pltpu.make_async_remote_copy(
            src_ref=g_in[t].at[1 - c], dst_ref=r_out[t], send_sem=send.at[t], recv_sem=recv.at[t],
            device_id=(x, y, 1 - c), device_id_type=MESH) for t in range(n)]
        for cp in cps:
            cp.start()
        for cp in cps:
            cp.wait()

    dma = pltpu.SemaphoreType.DMA
    return pl.pallas_call(
        body, name=name,
        out_shape=[jax.ShapeDtypeStruct(g.shape[1:], g.dtype) for g in grads],
        in_specs=[ANY] * n, out_specs=[ANY] * n, scratch_shapes=[dma((n,)), dma((n,))],
    )(*grads)


class _Exchange:
    def __init__(self, operands, out_shape, scratch, start, finish, aliases=None):
        self.operands, self.out_shape, self.scratch = operands, out_shape, scratch
        self.start, self.finish, self.aliases = start, finish, dict(aliases or {})


def _scatter_partials(parts, rows=None, row0=0, prev=None):
    n = len(parts)
    land = [(p.shape[0], p.shape[1] if rows is None else rows, p.shape[2]) for p in parts]

    def window(ref, slot, t):
        return ref.at[slot, pl.ds(row0, parts[t].shape[1])]

    def sends(p_in, r_out, sems):
        send, recv = sems
        x, y, c = _place()
        chip = 2 * x + y
        return [pltpu.make_async_remote_copy(
            src_ref=p_in[t].at[pchip], dst_ref=window(r_out[t], chip, t),
            send_sem=send.at[t, k], recv_sem=recv.at[t, k],
            device_id=(px, py, c), device_id_type=MESH)
            for t in range(n) for (k, px, py, pchip) in _chip_peers(x, y)]

    def start(p_in, r_out, sems):
        for cp in sends(p_in, r_out, sems):
            cp.start()

    def finish(p_in, r_out, sems):
        send, recv = sems
        x, y, c = _place()
        for t in range(n):
            for (k, px, py, pchip) in _chip_peers(x, y):
                pltpu.make_async_remote_copy(
                    src_ref=p_in[t].at[pchip], dst_ref=window(r_out[t], pchip, t),
                    send_sem=send.at[t, k], recv_sem=recv.at[t, k],
                    device_id=(px, py, c), device_id_type=MESH).wait_recv()
        for cp in sends(p_in, r_out, sems):
            cp.wait_send()

    dma = pltpu.SemaphoreType.DMA
    operands = list(parts) + (list(prev) if prev else [])
    return _Exchange(operands, [jax.ShapeDtypeStruct(sh, p.dtype) for sh, p in zip(land, parts)],
                     [dma((n, 3)), dma((n, 3))], start, finish,
                     aliases={n + t: t for t in range(n)} if prev else None)


def _w_in_grad_sibling(h, dpa, dgl, c_idx, carried, *, tm, tn, rows, pq):
    s, d = h.shape
    wa, wg = dpa.shape[1], dgl.shape[2]
    p = wa + 2 * wg
    ni, nj = rows // tm, p // tn
    na, qg, qp = wa // tn, wg // tn, pq // tn
    n_steps = ni * nj
    n_ci, n_co = len(carried.operands), len(carried.out_shape)

    def body(c_ref, h_ref, a_ref, g_ref, *rest):
        c_in = rest[:n_ci]
        r1_ref = rest[n_ci]
        c_out = rest[n_ci + 1:n_ci + 1 + n_co]
        slots, send, recv = rest[n_ci + 1 + n_co:n_ci + 4 + n_co]
        sems = rest[n_ci + 4 + n_co:]
        i, j = pl.program_id(0), pl.program_id(1)
        step = i * nj + j
        slot = lax.rem(step, 2)
        x, y, c = _place()
        sibling = (x, y, 1 - c)

        @pl.when(step == 0)
        def _():
            carried.start(c_in, c_out, sems)

        def tile_copy(sl):
            return pltpu.make_async_remote_copy(
                src_ref=slots.at[sl],
                dst_ref=r1_ref.at[j // qp, pl.ds(i * tm, tm), pl.ds((j % qp) * tn, tn)],
                send_sem=send.at[sl], recv_sem=recv, device_id=sibling, device_id_type=MESH)

        @pl.when(step >= 2)
        def _():
            tile_copy(slot).wait_send()

        def emit(b_ref):
            acc = lax.dot_general(h_ref[...], b_ref[...], TN, preferred_element_type=F32)
            slots[slot] = acc.astype(BF16)

        pl.when(j < na)(lambda: emit(a_ref))
        pl.when(j >= na)(lambda: emit(g_ref))
        tile_copy(slot).start()

        @pl.when(step == n_steps - 1)
        def _():
            tile_copy(0).wait_send()
            tile_copy(1).wait_send()
            pltpu.make_async_remote_copy(
                src_ref=r1_ref, dst_ref=r1_ref, send_sem=send.at[0], recv_sem=recv,
                device_id=sibling, device_id_type=MESH).wait_recv()
            carried.finish(c_in, c_out, sems)

    nbh = rows // tm
    dma = pltpu.SemaphoreType.DMA
    blk = s * tm * 2 + 2 * s * tn * 2
    res = pl.pallas_call(
        body, name="w_in_grad_sibling",
        grid_spec=pltpu.PrefetchScalarGridSpec(
            num_scalar_prefetch=1, grid=(ni, nj),
            in_specs=[pl.BlockSpec((s, tm), lambda i, j, cr: (0, (1 - cr[0]) * nbh + i)),
                      pl.BlockSpec((s, tn), lambda i, j, cr: (0, jnp.minimum(j, na - 1))),
                      pl.BlockSpec((None, s, tn),
                                   lambda i, j, cr: (jnp.maximum(j - na, 0) // qg, 0, jnp.maximum(j - na, 0) % qg))]
            + [ANY] * n_ci,
            out_specs=[ANY] * (1 + n_co),
            scratch_shapes=[pltpu.VMEM((2, tm, tn), BF16), dma((2,)), dma] + list(carried.scratch)),
        out_shape=[jax.ShapeDtypeStruct((N_CHIPS, rows, pq), BF16)] + list(carried.out_shape),
        compiler_params=_params(("arbitrary", "arbitrary"), blk, 2 * tm * tn * 2 + 2 * tm * tn * 4),
    )(c_idx, h, dpa, dgl, *carried.operands)
    return res[0], res[1:]


def _pair_share(halves):
    n = len(halves)

    def body(*refs):
        h_in = refs[:n]
        o_out = refs[n:2 * n]
        send, recv = refs[2 * n:]
        x, y, c = _place()
        cps = [pltpu.make_async_remote_copy(
            src_ref=h_in[t], dst_ref=o_out[t], send_sem=send.at[t], recv_sem=recv.at[t],
            device_id=(x, y, 1 - c), device_id_type=MESH) for t in range(n)]
        for cp in cps:
            cp.start()
        for cp in cps:
            cp.wait()

    dma = pltpu.SemaphoreType.DMA
    return pl.pallas_call(
        body, name="pair_share",
        out_shape=[jax.ShapeDtypeStruct(h.shape, h.dtype) for h in halves],
        in_specs=[ANY] * n, out_specs=[ANY] * n, scratch_shapes=[dma((n,)), dma((n,))],
    )(*halves)


def _gather_packs(pack):
    def body(p_ref, o_ref, send, recv, loc):
        x, y, c = _place()
        me = 4 * x + 2 * y + c
        mine = pltpu.make_async_copy(p_ref, o_ref.at[me], loc)
        mine.start()
        flips = [(fx, fy, fc) for fx in (0, 1) for fy in (0, 1) for fc in (0, 1)][1:]
        cps = []
        for k, (fx, fy, fc) in enumerate(flips):
            peer = (1 - x if fx else x, 1 - y if fy else y, 1 - c if fc else c)
            cps.append(pltpu.make_async_remote_copy(
                src_ref=p_ref, dst_ref=o_ref.at[me], send_sem=send.at[k], recv_sem=recv.at[k],
                device_id=peer, device_id_type=MESH))
        for cp in cps:
            cp.start()
        for k, (fx, fy, fc) in enumerate(flips):
            px, py, pc = (1 - x if fx else x, 1 - y if fy else y, 1 - c if fc else c)
            pltpu.make_async_remote_copy(
                src_ref=p_ref, dst_ref=o_ref.at[4 * px + 2 * py + pc],
                send_sem=send.at[k], recv_sem=recv.at[k],
                device_id=(px, py, pc), device_id_type=MESH).wait_recv()
        for cp in cps:
            cp.wait_send()
        mine.wait()

    dma = pltpu.SemaphoreType.DMA
    return pl.pallas_call(
        body, name="gather_packs", out_shape=jax.ShapeDtypeStruct((N_DEV,) + pack.shape, pack.dtype),
        in_specs=[ANY], out_specs=ANY, scratch_shapes=[dma((7,)), dma((7,)), dma],
    )(pack)


def _flat_pack(pieces):
    flat = jnp.concatenate([p.reshape(-1) for p in pieces])
    pad = (-flat.shape[0]) % (8 * LANES)
    flat = jnp.concatenate([flat, jnp.zeros((pad,), F32)])
    return flat.reshape(-1, LANES)


def _unpack(pack, shapes):
    flat = pack.reshape(-1)
    out, off = [], 0
    for sh in shapes:
        n = 1
        for s in sh:
            n *= s
        out.append(flat[off:off + n].reshape(sh))
        off += n
    return out


def kernel(x, norm_w, w_in, pool_w, pool_scale, conv_w, conv_b, gate_b, w_branch, w_out, final_norm_w, loss_target, m_norm_w, m_w_in, m_pool_w, m_pool_scale, m_conv_w, m_conv_b, m_gate_b, m_w_branch, m_w_out, m_final_norm_w, v_norm_w, v_w_in, v_pool_w, v_pool_scale, v_conv_w, v_conv_b, v_gate_b, v_w_branch, v_w_out, v_final_norm_w):
    _, s, d = x.shape
    w = d // 2
    cg = w // N_POOL_GROUPS
    p = 6 * w + 2 * d
    pq = p // N_CHIPS
    dq = d // N_CHIPS
    assert w_in.shape == (1, d, pq) and w_branch.shape == (1, 2, w, dq) and w_out.shape == (1, dq, d)
    assert pool_w.shape == (1, N_POOL_GROUPS, cg // N_CHIPS, cg) and conv_w.shape == (1, CONV_K, cg)

    x2d = x.reshape(s, d)
    tgt = loss_target.reshape(s, d)
    c_idx = lax.axis_index("c").astype(jnp.int32).reshape(1)
    chip = 2 * lax.axis_index("x") + lax.axis_index("y")

    big_w = [w_in.reshape(d, pq), w_out.reshape(dq, d), w_branch.reshape(2 * w, dq),
             pool_w.reshape(cg, cg)]
    names = ["in", "out", "branch", "pool"]
    big_b = [_cast_bf16("cast_" + nm, a) for nm, a in zip(names, big_w)]
    big_b = [b.reshape(2, b.shape[0] // 2, b.shape[1]) for b in big_b]
    chip_idx = chip.astype(jnp.int32).reshape(1)

    h = _rms_fwd(x2d, norm_w)

    tm = _pick(s, 1024, 16)
    tn_p = _pick(math.gcd(pq, 6 * w, d), 1024, LANES)
    qp = pq // tn_p
    tk_d = _pick(d, 4096, LANES)
    cast_out = lambda accs, ins, outs, ids: outs[0].__setitem__(Ellipsis, accs[0].astype(BF16))
    (proj,), (wg_in,) = _mm(
        "proj_own", grid=(s // tm, qp, d // tk_d), operands=[h, big_b[0].reshape(d, pq)],
        in_specs=[pl.BlockSpec((tm, tk_d), lambda i, j, k, ch: (i, k)),
                  pl.BlockSpec((tk_d, tn_p), lambda i, j, k, ch: (k, j))],
        out_shape=[jax.ShapeDtypeStruct((s, p), BF16)],
        out_specs=[pl.BlockSpec((tm, tn_p), lambda i, j, k, ch: (i, ch[0] * qp + j))],
        pairs=[(0, 1, 0, None)], dims=NN, acc_shapes=[(tm, tn_p)], epilogue=cast_out,
        comm=_gather_weights(big_b[:1], [], relay=True), prefetch=chip_idx,
    )
    wg_in = wg_in.reshape(N_CHIPS, d, pq)
    other = lambda j, ch: jnp.bitwise_xor(ch[0], 1 + j // qp)
    (proj,), (wg_out, wg_br, wg_pool, wg_cw, wg_gb) = _mm(
        "proj_others", grid=(s // tm, 3 * qp, d // tk_d), operands=[h, wg_in, proj],
        in_specs=[pl.BlockSpec((tm, tk_d), lambda i, j, k, ch: (i, k)),
                  pl.BlockSpec((None, tk_d, tn_p), lambda i, j, k, ch: (other(j, ch), k, j % qp)),
                  None],
        out_shape=[jax.ShapeDtypeStruct((s, p), BF16)],
        out_specs=[pl.BlockSpec((tm, tn_p), lambda i, j, k, ch: (i, other(j, ch) * qp + j % qp))],
        pairs=[(0, 1, 0, None)], dims=NN, acc_shapes=[(tm, tn_p)], epilogue=cast_out,
        comm=_gather_weights(big_b[1:], [conv_w.reshape(CONV_K, cg), gate_b.reshape(2, dq)]),
        prefetch=chip_idx, aliases={2: 0},
    )
    wg_out = wg_out.reshape(d, d)
    wg_pool = wg_pool.reshape(N_CHIPS, N_POOL_GROUPS, cg // N_CHIPS, cg)

    ys = _mix_fwd(proj, wg_pool, pool_scale, wg_cw, conv_b, w)

    tn_d = _pick(dq, 1024, LANES)
    qd = dq // tn_d
    tk_w = _pick(w, 2048, LANES)
    tm_g = _pick(s, 512, 16)
    gl0 = (6 * w) // tn_d
    gl1 = (6 * w + d) // tn_d

    def gate_specs(im, rows):
        return [pl.BlockSpec((rows, tn_d), lambda *a: (im(*a)[0], gl0 + im(*a)[1])),
                pl.BlockSpec((rows, tn_d), lambda *a: (im(*a)[0], gl1 + im(*a)[1])),
                pl.BlockSpec((None, 2, tn_d), lambda *a: (im(*a)[1] // qd, 0, im(*a)[1] % qd))]

    def merge_epilogue(accs, ins, outs, ids):
        gb = ins[6][...]
        g0 = _sigmoid(ins[4][...].astype(F32) + gb[0:1])
        g1 = _sigmoid(ins[5][...].astype(F32) + gb[1:2])
        outs[0][...] = (g0 * accs[0] + g1 * accs[1]).astype(BF16)
        outs[1][0] = accs[0].astype(BF16)
        outs[1][1] = accs[1].astype(BF16)

    (merged, br), _ = _mm(
        "branch_merge", grid=(s // tm_g, d // tn_d, w // tk_w),
        operands=[ys, wg_br, ys, wg_br, proj, proj, wg_gb],
        in_specs=[pl.BlockSpec((None, tm_g, tk_w), lambda i, j, k: (0, i, k)),
                  pl.BlockSpec((None, None, tk_w, tn_d), lambda i, j, k: (j // qd, 0, k, j % qd)),
                  pl.BlockSpec((None, tm_g, tk_w), lambda i, j, k: (1, i, k)),
                  pl.BlockSpec((None, None, tk_w, tn_d), lambda i, j, k: (j // qd, 1, k, j % qd)),
                  *gate_specs(lambda i, j, k: (i, j), tm_g)],
        out_shape=[jax.ShapeDtypeStruct((s, d), BF16), jax.ShapeDtypeStruct((2, s, d), BF16)],
        out_specs=[pl.BlockSpec((tm_g, tn_d), lambda i, j, k: (i, j)),
                   pl.BlockSpec((2, tm_g, tn_d), lambda i, j, k: (0, i, j))],
        pairs=[(0, 1, 0, None), (2, 3, 1, None)], dims=NN, acc_shapes=[(tm_g, tn_d)] * 2,
        epilogue=merge_epilogue, temp_bytes=6 * tm_g * tn_d * 4,
    )

    tn_f = _pick(d, 1024, LANES)
    o = _mm(
        "out_proj", grid=(s // tm, d // tn_f, d // tk_d), operands=[merged, wg_out],
        in_specs=[pl.BlockSpec((tm, tk_d), lambda i, j, k: (i, k)),
                  pl.BlockSpec((tk_d, tn_f), lambda i, j, k: (k, j))],
        out_shape=[jax.ShapeDtypeStruct((s, d), F32)],
        out_specs=[pl.BlockSpec((tm, tn_f), lambda i, j, k: (i, j))],
        pairs=[(0, 1, 0, None)], dims=NN, acc_shapes=[(tm, tn_f)],
        epilogue=lambda accs, ins, outs, ids: outs[0].__setitem__(Ellipsis, accs[0]),
    )[0][0]

    dx2, dx2b, loss_part, g_fnw =_head(x2d, o, tgt, final_norm_w.reshape(1, d))

    def gate_bwd_epilogue(accs, ins, outs, ids):
        dm = accs[0]
        gb = ins[5][...]
        i = ids[1]

        @pl.when(i == 0)
        def _():
            outs[2][...] = jnp.zeros(outs[2].shape, F32)

        for n in range(2):
            gate = _sigmoid(ins[3 + n][...].astype(F32) + gb[n:n + 1])
            outs[0][n] = (dm * gate).astype(BF16)
            dgl = dm * ins[2][n].astype(F32) * gate * (1.0 - gate)
            outs[1][n] = dgl.astype(BF16)
            outs[2][n:n + 1, :] += jnp.sum(dgl, axis=0, keepdims=True)

    tm_b = _pick(s, 256, 16)
    (d_br, dgl, g_gb), _ = _mm(
        "out_proj_bwd_gate", grid=(d // tn_d, s // tm_b, d // tk_d),
        operands=[dx2b, wg_out, br, proj, proj, wg_gb],
        in_specs=[pl.BlockSpec((tm_b, tk_d), lambda j, i, k: (i, k)),
                  pl.BlockSpec((tn_d, tk_d), lambda j, i, k: (j, k)),
                  pl.BlockSpec((2, tm_b, tn_d), lambda j, i, k: (0, i, j)),
                  *gate_specs(lambda j, i, k: (i, j), tm_b)],
        out_shape=[jax.ShapeDtypeStruct((2, s, d), BF16), jax.ShapeDtypeStruct((2, s, d), BF16),
                   jax.ShapeDtypeStruct((2, d), F32)],
        out_specs=[pl.BlockSpec((2, tm_b, tn_d), lambda j, i, k: (0, i, j)),
                   pl.BlockSpec((2, tm_b, tn_d), lambda j, i, k: (0, i, j)),
                   pl.BlockSpec((2, tn_d), lambda j, i, k: (0, j))],
        pairs=[(0, 1, 0, None)], dims=NT, acc_shapes=[(tm_b, tn_d)],
        epilogue=gate_bwd_epilogue, semantics=("parallel", "arbitrary", "arbitrary"),
        temp_bytes=8 * tm_b * tn_d * 4,
    )

    hh_out = d // 8
    tm_o = _pick(hh_out, 512, LANES)
    nb_o = hh_out // tm_o
    tk_s = _pick(s, 4096, LANES)
    g_out = _mm(
        "w_out_grad", grid=(d // tm_o, d // tn_f, s // tk_s), operands=[merged, dx2b],
        in_specs=[pl.BlockSpec((tk_s, tm_o), lambda i, j, k: (k, i)),
                  pl.BlockSpec((tk_s, tn_f), lambda i, j, k: (k, j))],
        out_shape=[jax.ShapeDtypeStruct((2, N_CHIPS, hh_out, d), BF16)],
        out_specs=[pl.BlockSpec((None, None, tm_o, tn_f),
                                lambda i, j, k: ((i // nb_o) % 2, i // (2 * nb_o), i % nb_o, j))],
        pairs=[(0, 1, 0, None)], dims=TN, acc_shapes=[(tm_o, tn_f)],
        epilogue=lambda accs, ins, outs, ids: outs[0].__setitem__(Ellipsis, accs[0].astype(BF16)),
    )[0][0]

    tn_w = _pick(w, 2048, LANES)
    tk_q = _pick(dq, 1024, LANES)
    qk = dq // tk_q
    dys = _mm(
        "branch_bwd", grid=(2, s // tm, w // tn_w, d // tk_q), operands=[d_br, wg_br],
        in_specs=[pl.BlockSpec((None, tm, tk_q), lambda n, i, j, k: (n, i, k)),
                  pl.BlockSpec((None, None, tn_w, tk_q), lambda n, i, j, k: (k // qk, n, j, k % qk))],
        out_shape=[jax.ShapeDtypeStruct((2, s, w), F32)],
        out_specs=[pl.BlockSpec((None, tm, tn_w), lambda n, i, j, k: (n, i, j))],
        pairs=[(0, 1, 0, None)], dims=NT, acc_shapes=[(tm, tn_w)],
        epilogue=lambda accs, ins, outs, ids: outs[0].__setitem__(Ellipsis, accs[0]),
    )[0][0]

    tm_w = _pick(w, 1024, LANES)
    g_br = _mm(
        "w_branch_grad", grid=(2, w // tm_w, d // tn_d, s // tk_s), operands=[ys, d_br],
        in_specs=[pl.BlockSpec((None, tk_s, tm_w), lambda n, i, j, k: (n, k, i)),
                  pl.BlockSpec((None, tk_s, tn_d), lambda n, i, j, k: (n, k, j))],
        out_shape=[jax.ShapeDtypeStruct((2, N_CHIPS, w, dq), BF16)],
        out_specs=[pl.BlockSpec((None, None, tm_w, tn_d), lambda n, i, j, k: (n, j // qd, i, j % qd))],
        pairs=[(0, 1, 0, None)], dims=TN, acc_shapes=[(tm_w, tn_d)],
        epilogue=lambda accs, ins, outs, ids: outs[0].__setitem__(Ellipsis, accs[0].astype(BF16)),
    )[0][0]

    dpa, g_pool, g_ps, g_cw, g_cb = _mix_bwd(proj, dys, wg_pool, pool_scale, wg_cw, conv_b, w)
    g_pool = g_pool.astype(BF16).reshape(2, N_CHIPS, 2 * (cg // N_CHIPS), cg)

    def chip_partials(nms, grads):
        recv = _pair_exchange("pair_exchange_" + nms[0], grads)
        parts = []
        for nm, g, r1 in zip(nms, grads, recv):
            _, nc, hh, wd = g.shape
            pt = _pair_add("pair_add_" + nm, g.reshape(2, nc * hh, wd), r1.reshape(nc * hh, wd), c_idx)
            parts.append(pt.reshape(nc, hh, wd))
        return parts

    parts_early = chip_partials(names[1:], [g_out, g_br, g_pool])

    na = (6 * w) // tn_p
    qg = d // tn_p
    hh_in = d // 2
    assert s == tk_s
    tm_i = _pick(hh_in, 1024, LANES)
    tn_i = _pick(tn_p, 512, LANES)
    from_sibling, recv_early = _w_in_grad_sibling(
        h, dpa, dgl, c_idx, _scatter_partials(parts_early), tm=tm_i, tn=tn_i, rows=hh_in, pq=pq)

    tm_o2 = _pick(hh_in, 512, LANES)
    nb_o2 = hh_in // tm_o2

    def own_rows(name, first, count, comm):
        def add_sibling(accs, ins, outs, ids):
            outs[0][...] = (accs[0] + ins[3][...].astype(F32)).astype(BF16)

        return _mm(
            name, grid=(p // tn_p, count, 1), operands=[h, dpa, dgl, from_sibling], comm=comm, prefetch=c_idx,
            in_specs=[pl.BlockSpec((tk_s, tm_o2), lambda j, i, k, cr: (0, cr[0] * nb_o2 + first + i)),
                      pl.BlockSpec((tk_s, tn_p), lambda j, i, k, cr: (0, jnp.minimum(j, na - 1))),
                      pl.BlockSpec((None, tk_s, tn_p),
                                   lambda j, i, k, cr: (jnp.maximum(j - na, 0) // qg, 0, jnp.maximum(j - na, 0) % qg)),
                      pl.BlockSpec((None, tm_o2, tn_p), lambda j, i, k, cr: (j // qp, first + i, j % qp))],
            out_shape=[jax.ShapeDtypeStruct((N_CHIPS, count * tm_o2, pq), BF16)],
            out_specs=[pl.BlockSpec((None, tm_o2, tn_p), lambda j, i, k, cr: (j // qp, i, j % qp))],
            pairs=[(0, 1, 0, lambda ids: ids[0] < na), (0, 2, 0, lambda ids: ids[0] >= na)],
            dims=TN, acc_shapes=[(tm_o2, tn_p)], epilogue=add_sibling)

    n_first = 1 if nb_o2 > 1 else 0
    if n_first:
        (part_a,), _ = own_rows("w_in_grad_own_a", 0, n_first, None)
        scatter_a = _scatter_partials([part_a], rows=hh_in)
        (part_b,), land_a = own_rows("w_in_grad_own_b", n_first, nb_o2 - n_first, scatter_a)
        scatter_b = _scatter_partials([part_b], rows=hh_in, row0=n_first * tm_o2, prev=land_a)
    else:
        (part_b,), _ = own_rows("w_in_grad_own", 0, nb_o2, None)
        scatter_b = _scatter_partials([part_b])

    tn_h = _pick(d, 2048, LANES)
    (dh,), recv_in = _mm(
        "proj_bwd", grid=(s // tm, d // tn_h, p // tn_p), operands=[dpa, dgl, wg_in],
        comm=scatter_b,
        in_specs=[pl.BlockSpec((tm, tn_p), lambda i, j, k: (i, jnp.minimum(k, na - 1))),
                  pl.BlockSpec((None, tm, tn_p),
                               lambda i, j, k: (jnp.maximum(k - na, 0) // qg, i, jnp.maximum(k - na, 0) % qg)),
                  pl.BlockSpec((None, tn_h, tn_p), lambda i, j, k: (k // qp, j, k % qp))],
        out_shape=[jax.ShapeDtypeStruct((s, d), F32)],
        out_specs=[pl.BlockSpec((tm, tn_h), lambda i, j, k: (i, j))],
        pairs=[(0, 2, 0, lambda ids: ids[2] < na), (1, 2, 0, lambda ids: ids[2] >= na)],
        dims=NT, acc_shapes=[(tm, tn_h)],
        epilogue=lambda accs, ins, outs, ids: outs[0].__setitem__(Ellipsis, accs[0]),
    )

    grad_x, g_nw = _rms_bwd(x2d, dh, dx2, norm_w)

    if n_first:
        half_in = _sum_chips("sum_chips_in_a", part_a, recv_in[0], chip_idx)
        half_in = _sum_chips("sum_chips_in_b", part_b, recv_in[0], chip_idx, row0=n_first * tm_o2, prev=half_in)
    else:
        half_in = _sum_chips("sum_chips_in", part_b, recv_in[0], chip_idx)
    halves = [half_in] + [_sum_chips("sum_chips_" + nm, pt, r2, chip_idx)
                          for nm, pt, r2 in zip(names[1:], parts_early, recv_early)]
    others = _pair_share(halves)

    g_cw_full = jnp.transpose(g_cw, (1, 0, 2)).reshape(CONV_K, w)
    small_shapes = [(LANES,), (1, d), (1, d), (1, w), (1, w), (CONV_K, w), (2, d)]
    pack = _flat_pack([loss_part[0], g_nw, g_fnw, g_ps, g_cb, g_cw_full, g_gb])
    total = _sum_devices(_gather_packs(pack))
    t_loss, t_nw, t_fnw, t_ps, t_cb, t_cw, t_gb = _unpack(total, small_shapes)
    loss = t_loss[0]
    t_cw = lax.dynamic_slice_in_dim(t_cw, chip * cg, cg, axis=1)
    t_gb = lax.dynamic_slice_in_dim(t_gb, chip * dq, dq, axis=1)

    out = {}
    big_names = ["w_in", "w_out", "w_branch", "pool_w"]
    big_m = [m_w_in, m_w_out, m_w_branch, m_pool_w]
    big_v = [v_w_in, v_w_out, v_w_branch, v_pool_w]
    big_orig = [w_in, w_out, w_branch, pool_w]
    for nm, g_own, g_other, w2, mm_, vv_, orig in zip(big_names, halves, others, big_w, big_m, big_v, big_orig):
        sh = (2,) + g_own.shape
        res = _adamw_halves("adamw_" + nm, g_own, g_other, c_idx,
                            w2.reshape(sh), mm_.reshape(sh), vv_.reshape(sh))
        out[nm] = [r.reshape(orig.shape) for r in res]

    sm_names = ["norm_w", "final_norm_w", "pool_scale", "conv_b", "conv_w", "gate_b"]
    sm_g = [t_nw, t_fnw, t_ps, t_cb, t_cw, t_gb]
    sm_w = [norm_w, final_norm_w, pool_scale, conv_b, conv_w, gate_b]
    sm_m = [m_norm_w, m_final_norm_w, m_pool_scale, m_conv_b, m_conv_w, m_gate_b]
    sm_v = [v_norm_w, v_final_norm_w, v_pool_scale, v_conv_b, v_conv_w, v_gate_b]
    sm_shapes = [a.shape for a in sm_w]
    res = _adamw("adamw_small", _flat_pack(sm_g), _flat_pack(sm_w), _flat_pack(sm_m), _flat_pack(sm_v))
    res = [_unpack(r, sm_shapes) for r in res]
    for idx, nm in enumerate(sm_names):
        out[nm] = [r[idx] for r in res]

    order = ["norm_w", "w_in", "pool_w", "pool_scale", "conv_w", "conv_b", "gate_b", "w_branch", "w_out",
             "final_norm_w"]
    outs = [loss, grad_x.reshape(x.shape)]
    for kind in range(4):
        outs += [out[nm][kind] for nm in order]
    return tuple(outs)
```

```python
import math

import jax
import jax.numpy as jnp
from jax import lax
from jax.experimental import pallas as pl
from jax.experimental.pallas import tpu as pltpu

F32 = jnp.float32
BF16 = jnp.bfloat16

NORM_EPS = 1e-6
POOL_WINDOWS = (2, 4, 8, 16)
N_POOL_GROUPS = len(POOL_WINDOWS)
CONV_K = 3
ADAM_LR = 0.001
ADAM_B1 = 0.9
ADAM_B2 = 0.999
ADAM_EPS = 1e-08
ADAM_WD = 0.01
ADAM_STEP = 10

N_CHIPS = 4
N_DEV = 8
HALO = 16
LANES = 128
V7X_VMEM_BYTES = 64 * 1024 * 1024
VMEM_CAP = V7X_VMEM_BYTES - 8 * 1024 * 1024

MESH = pl.DeviceIdType.MESH
ANY = pl.BlockSpec(memory_space=pl.ANY)

NN = (((1,), (0,)), ((), ()))
NT = (((1,), (1,)), ((), ()))
TN = (((0,), (0,)), ((), ()))


def _pick(dim, pref, align):
    if dim <= pref:
        return dim
    t = (pref // align) * align
    while t >= align:
        if dim % t == 0:
            return t
        t -= align
    raise ValueError(f"no tile for {dim} (pref {pref}, align {align})")


def _nbytes(shape, dtype):
    n = 1
    for s in shape:
        if s is not None:
            n *= s
    return n * jnp.dtype(dtype).itemsize


def _params(semantics, block_bytes, extra_bytes=0):
    need = 2 * block_bytes + extra_bytes + (2 << 20)
    return pltpu.CompilerParams(dimension_semantics=semantics,
                                vmem_limit_bytes=int(min(max(need, 16 << 20), VMEM_CAP)))


def _sigmoid(z):
    return jax.nn.sigmoid(z)


def _mm(name, *, grid, operands, in_specs, out_shape, out_specs, pairs, dims, acc_shapes, epilogue,
        semantics=None, temp_bytes=0, comm=None, prefetch=None, aliases=None):
    aliases = dict(aliases or {})
    n_in, n_out = len(operands), len(out_shape)
    kax = len(grid) - 1
    nk = grid[kax]
    single = nk == 1
    conditional = any(p[3] is not None for p in pairs)
    if single and conditional:
        assert len(acc_shapes) == 1 and all(p[3] is not None for p in pairs)
    n_acc = 0 if single else len(acc_shapes)
    n_pf = 0 if prefetch is None else 1
    c_ops = list(comm.operands) if comm else []
    c_out = list(comm.out_shape) if comm else []
    c_sems = list(comm.scratch) if comm else []
    c_alias = dict(comm.aliases) if comm else {}
    n_ci, n_co = len(c_ops), len(c_out)

    def product(ins, ai, bi):
        return lax.dot_general(ins[ai][...], ins[bi][...], dims, preferred_element_type=F32)

    def body(*refs):
        refs = refs[n_pf:]
        ins = refs[:n_in]
        c_in_refs = refs[n_in:n_in + n_ci]
        outs = refs[n_in + n_ci:n_in + n_ci + n_out]
        c_out_refs = refs[n_in + n_ci + n_out:n_in + n_ci + n_out + n_co]
        scratch = refs[n_in + n_ci + n_out + n_co:]
        accs = scratch[:n_acc]
        sems = scratch[n_acc:]
        ids = [pl.program_id(a) for a in range(len(grid))]
        k = ids[kax]

        if comm:
            is_first = ids[0] == 0
            is_last = ids[0] == grid[0] - 1
            for a in range(1, len(grid)):
                is_first = jnp.logical_and(is_first, ids[a] == 0)
                is_last = jnp.logical_and(is_last, ids[a] == grid[a] - 1)

            @pl.when(is_first)
            def _():
                comm.start(c_in_refs, c_out_refs, sems)

        if single and conditional:
            for (ai, bi, ci, cond) in pairs:
                def only(ai=ai, bi=bi):
                    epilogue([product(ins, ai, bi)], ins, outs, ids)
                pl.when(cond(ids))(only)
        elif single:
            vals = [None] * len(acc_shapes)
            for (ai, bi, ci, cond) in pairs:
                r = product(ins, ai, bi)
                vals[ci] = r if vals[ci] is None else vals[ci] + r
            epilogue(vals, ins, outs, ids)
        else:
            @pl.when(k == 0)
            def _():
                for a in accs:
                    a[...] = jnp.zeros(a.shape, a.dtype)

            for (ai, bi, ci, cond) in pairs:
                def step(ai=ai, bi=bi, ci=ci):
                    accs[ci][...] += product(ins, ai, bi)
                if cond is None:
                    step()
                else:
                    pl.when(cond(ids))(step)

            @pl.when(k == nk - 1)
            def _():
                epilogue([a[...] for a in accs], ins, outs, ids)

        if comm:
            @pl.when(is_last)
            def _():
                comm.finish(c_in_refs, c_out_refs, sems)

    if semantics is None:
        semantics = ("parallel",) * kax + ("arbitrary",)
    if comm:
        semantics = ("arbitrary",) * len(grid)
    in_specs = [ANY if idx in aliases else spec for idx, spec in enumerate(in_specs)]
    blk = 0
    for idx, (spec, op) in enumerate(zip(in_specs, operands)):
        if idx not in aliases:
            blk += _nbytes(spec.block_shape, op.dtype)
    for spec, o in zip(out_specs, out_shape):
        blk += _nbytes(spec.block_shape, o.dtype)
    acc_bytes = sum(_nbytes(s, F32) for s in acc_shapes)
    io_alias = {n_pf + i: o for i, o in aliases.items()}
    io_alias.update({n_pf + n_in + i: n_out + o for i, o in c_alias.items()})
    all_in = list(in_specs) + [ANY] * n_ci
    all_out = list(out_specs) + [ANY] * n_co
    scratch_shapes = [pltpu.VMEM(s, F32) for s in acc_shapes[:n_acc]] + c_sems
    params = _params(semantics, blk, 3 * acc_bytes + temp_bytes)
    shapes = list(out_shape) + c_out
    if prefetch is None:
        call = pl.pallas_call(
            body, name=name, grid=grid, in_specs=all_in, out_specs=all_out, out_shape=shapes,
            scratch_shapes=scratch_shapes, input_output_aliases=io_alias, compiler_params=params)
        res = call(*operands, *c_ops)
    else:
        call = pl.pallas_call(
            body, name=name, out_shape=shapes, input_output_aliases=io_alias, compiler_params=params,
            grid_spec=pltpu.PrefetchScalarGridSpec(
                num_scalar_prefetch=1, grid=grid, in_specs=all_in, out_specs=all_out,
                scratch_shapes=scratch_shapes))
        res = call(prefetch, *operands, *c_ops)
    return res[:n_out], res[n_out:]


def _cast_bf16(name, w2d):
    r, c = w2d.shape
    tr = _pick(r, max(16, (4 << 20) // (4 * c)), 16)

    def body(w_ref, o_ref):
        o_ref[...] = w_ref[...].astype(BF16)

    return pl.pallas_call(
        body, name=name, grid=(r // tr,),
        in_specs=[pl.BlockSpec((tr, c), lambda i: (i, 0))],
        out_specs=pl.BlockSpec((tr, c), lambda i: (i, 0)),
        out_shape=jax.ShapeDtypeStruct((r, c), BF16),
        compiler_params=_params(("parallel",), tr * c * 6),
    )(w2d)


def _rms_fwd(x, norm_w):
    s, d = x.shape
    ts = _pick(s, 256, 16)

    def body(x_ref, w_ref, h_ref):
        xv = x_ref[...]
        rstd = lax.rsqrt(jnp.mean(xv * xv, axis=-1, keepdims=True) + NORM_EPS)
        h_ref[...] = (xv * rstd * w_ref[...]).astype(BF16)

    return pl.pallas_call(
        body, name="rms_fwd", grid=(s // ts,),
        in_specs=[pl.BlockSpec((ts, d), lambda i: (i, 0)), pl.BlockSpec((1, d), lambda i: (0, 0))],
        out_specs=pl.BlockSpec((ts, d), lambda i: (i, 0)),
        out_shape=jax.ShapeDtypeStruct((s, d), BF16),
        compiler_params=_params(("parallel",), ts * d * 6, 3 * ts * d * 4),
    )(x, norm_w)


def _head(x, o, target, fnw):
    s, d = x.shape
    ts = _pick(s, 128, 16)

    def body(x_ref, o_ref, t_ref, w_ref, dx_ref, dxb_ref, loss_ref, gw_ref):
        i = pl.program_id(0)

        @pl.when(i == 0)
        def _():
            loss_ref[...] = jnp.zeros(loss_ref.shape, F32)
            gw_ref[...] = jnp.zeros(gw_ref.shape, F32)

        w = w_ref[...]
        x2 = x_ref[...] + o_ref[...]
        rstd = lax.rsqrt(jnp.mean(x2 * x2, axis=-1, keepdims=True) + NORM_EPS)
        n = x2 * rstd
        e = n * w - t_ref[...]
        loss_ref[...] += 0.5 * jnp.sum(e * e) / d
        dy = e / d
        gw_ref[...] += jnp.sum(dy * n, axis=0, keepdims=True)
        gy = dy * w
        dx = rstd * (gy - n * jnp.mean(gy * n, axis=-1, keepdims=True))
        dx_ref[...] = dx
        dxb_ref[...] = dx.astype(BF16)

    row = pl.BlockSpec((ts, d), lambda i: (i, 0))
    return pl.pallas_call(
        body, name="head", grid=(s // ts,),
        in_specs=[row, row, row, pl.BlockSpec((1, d), lambda i: (0, 0))],
        out_specs=[row, row, pl.BlockSpec((8, LANES), lambda i: (0, 0)),
                   pl.BlockSpec((1, d), lambda i: (0, 0))],
        out_shape=[jax.ShapeDtypeStruct((s, d), F32), jax.ShapeDtypeStruct((s, d), BF16),
                   jax.ShapeDtypeStruct((8, LANES), F32), jax.ShapeDtypeStruct((1, d), F32)],
        compiler_params=_params(("arbitrary",), ts * d * 22, 6 * ts * d * 4),
    )(x, o, target, fnw)


def _rms_bwd(x, dh, dx2, norm_w):
    s, d = x.shape
    ts = _pick(s, 128, 16)

    def body(x_ref, dh_ref, dx2_ref, w_ref, gx_ref, gw_ref):
        i = pl.program_id(0)

        @pl.when(i == 0)
        def _():
            gw_ref[...] = jnp.zeros(gw_ref.shape, F32)

        xv = x_ref[...]
        rstd = lax.rsqrt(jnp.mean(xv * xv, axis=-1, keepdims=True) + NORM_EPS)
        n = xv * rstd
        dhv = dh_ref[...]
        gw_ref[...] += jnp.sum(dhv * n, axis=0, keepdims=True)
        gh = dhv * w_ref[...]
        gx_ref[...] = dx2_ref[...] + rstd * (gh - n * jnp.mean(gh * n, axis=-1, keepdims=True))

    row = pl.BlockSpec((ts, d), lambda i: (i, 0))
    vec = pl.BlockSpec((1, d), lambda i: (0, 0))
    return pl.pallas_call(
        body, name="rms_bwd", grid=(s // ts,),
        in_specs=[row, row, row, vec], out_specs=[row, vec],
        out_shape=[jax.ShapeDtypeStruct((s, d), F32), jax.ShapeDtypeStruct((1, d), F32)],
        compiler_params=_params(("arbitrary",), ts * d * 16, 5 * ts * d * 4),
    )(x, dh, dx2, norm_w)


def _silu(z):
    return z * _sigmoid(z)


def _window_sum(ext, window, back):
    n = ext.shape[0]
    acc = ext
    step = 1
    while step < window:
        acc = acc + pltpu.roll(acc, step if back else n - step, 0)
        step *= 2
    return acc


def _shift_rows(ext, k, back):
    n = ext.shape[0]
    return pltpu.roll(ext, k if back else n - k, 0)


def _mix_fwd(proj, pool_wg, pool_scale, conv_wg, conv_b, width):
    s = proj.shape[0]
    w = width
    cg = w // N_POOL_GROUPS
    ts = _pick(s, 128, HALO)
    hb = ts // HALO
    cols = 6 * w

    def body(p_ref, ph_ref, pw_ref, ps_ref, cw_ref, cb_ref, ys_ref):
        i = pl.program_id(0)
        first = i == 0
        t1 = (i * ts + lax.broadcasted_iota(jnp.int32, (ts, 1), 0) + 1).astype(F32)

        def tile(part, g):
            lo = part * w + g * cg
            return p_ref[:, lo:lo + cg].astype(F32)

        def prev(part, g):
            lo = part * w + g * cg
            return jnp.where(first, 0.0, ph_ref[:, lo:lo + cg].astype(F32))

        for g, win in enumerate(POOL_WINDOWS):
            gs = slice(g * cg, (g + 1) * cg)
            u = tile(0, g)
            ext = jnp.concatenate([prev(0, g), u], axis=0)
            wsum = _window_sum(ext, win, True)[HALO:]
            pooled = wsum / jnp.minimum(t1, float(win)) - u
            pw = pw_ref[:, g].reshape(cg, cg)
            mixed = jnp.dot(pooled.astype(BF16), pw, preferred_element_type=F32)
            ys_ref[0, :, gs] = (mixed * ps_ref[:, gs] * _silu(tile(1, g))).astype(BF16)
            v = tile(4, g) * tile(2, g)
            vext = jnp.concatenate([prev(4, g) * prev(2, g), v], axis=0)
            v1 = _shift_rows(vext, 1, True)[HALO:]
            v2 = _shift_rows(vext, 2, True)[HALO:]
            cw = [cw_ref[g, tap:tap + 1, :] for tap in range(CONV_K)]
            y = cb_ref[:, gs] + cw[0] * v2 + cw[1] * v1 + cw[2] * v
            ys_ref[1, :, gs] = (tile(3, g) * y * _silu(tile(5, g))).astype(BF16)

    return pl.pallas_call(
        body, name="mix_fwd", grid=(s // ts,),
        in_specs=[pl.BlockSpec((ts, cols), lambda i: (i, 0)),
                  pl.BlockSpec((HALO, cols), lambda i: (jnp.maximum(i * hb - 1, 0), 0)),
                  pl.BlockSpec(pool_wg.shape, lambda i: (0, 0, 0, 0)),
                  pl.BlockSpec((1, w), lambda i: (0, 0)),
                  pl.BlockSpec(conv_wg.shape, lambda i: (0, 0, 0)),
                  pl.BlockSpec((1, w), lambda i: (0, 0))],
        out_specs=pl.BlockSpec((2, ts, w), lambda i: (0, i, 0)),
        out_shape=jax.ShapeDtypeStruct((2, s, w), BF16),
        compiler_params=_params(("parallel",), (ts + HALO) * cols * 2 + 2 * ts * w * 2
                                + _nbytes(pool_wg.shape, BF16), 24 * (ts + HALO) * cg * 4),
    )(proj, proj, pool_wg, pool_scale, conv_wg, conv_b)


def _mix_bwd(proj, dys, pool_wg, pool_scale, conv_wg, conv_b, width):
    s = proj.shape[0]
    w = width
    cg = w // N_POOL_GROUPS
    ts = _pick(s, 128, HALO)
    hb = ts // HALO
    n_tiles = s // ts
    last_hb = s // HALO - 1
    cols = 6 * w

    def body(p_ref, ph_ref, pn_ref, dy_ref, dyn_ref, pw_ref, ps_ref, cw_ref, cb_ref,
             dp_ref, dpw_ref, dps_ref, dcw_ref, dcb_ref):
        i = pl.program_id(0)
        first = i == 0
        last = i == n_tiles - 1

        @pl.when(first)
        def _():
            dpw_ref[...] = jnp.zeros(dpw_ref.shape, F32)
            dps_ref[...] = jnp.zeros(dps_ref.shape, F32)
            dcw_ref[...] = jnp.zeros(dcw_ref.shape, F32)
            dcb_ref[...] = jnp.zeros(dcb_ref.shape, F32)

        row = i * ts + lax.broadcasted_iota(jnp.int32, (ts + HALO, 1), 0)
        t1_ext = (row + 1).astype(F32)
        t1 = t1_ext[:ts]

        def tile(part, g):
            lo = part * w + g * cg
            return p_ref[:, lo:lo + cg].astype(F32)

        def prev(part, g):
            lo = part * w + g * cg
            return jnp.where(first, 0.0, ph_ref[:, lo:lo + cg].astype(F32))

        def ahead(part, g):
            lo = part * w + g * cg
            return jnp.concatenate([tile(part, g), pn_ref[:, lo:lo + cg].astype(F32)], axis=0)

        def dy_ahead(n, g):
            gs = slice(g * cg, (g + 1) * cg)
            nxt = jnp.where(last, 0.0, dyn_ref[n, :, gs])
            return jnp.concatenate([dy_ref[n, :, gs], nxt], axis=0)

        for g, win in enumerate(POOL_WINDOWS):
            gs = slice(g * cg, (g + 1) * cg)
            u = tile(0, g)
            ext = jnp.concatenate([prev(0, g), u], axis=0)
            pooled = _window_sum(ext, win, True)[HALO:] / jnp.minimum(t1, float(win)) - u
            pooled_b = pooled.astype(BF16)
            pw = pw_ref[:, g].reshape(cg, cg)
            mixed = jnp.dot(pooled_b, pw, preferred_element_type=F32)
            zp_ext = ahead(1, g)
            dy0_ext = dy_ahead(0, g)
            scale = ps_ref[:, gs]
            sp_ext = _sigmoid(zp_ext)
            dms_ext = dy0_ext * (zp_ext * sp_ext)
            dmix_b = (dms_ext * scale).astype(BF16)
            dpooled_ext = lax.dot_general(dmix_b, pw, NT, preferred_element_type=F32)
            dy0 = dy0_ext[:ts]
            zp, sp = zp_ext[:ts], sp_ext[:ts]
            dsilu_p = sp * (1.0 + zp * (1.0 - sp))
            dp_ref[:, w + g * cg:w + (g + 1) * cg] = (dy0 * mixed * scale * dsilu_p).astype(BF16)
            dps_ref[:, gs] += jnp.sum(dms_ext[:ts] * mixed, axis=0, keepdims=True)
            dpw = lax.dot_general(pooled_b, dmix_b[:ts], TN, preferred_element_type=F32)
            dpw_ref[g // 2, :, g % 2] += dpw.reshape(N_CHIPS, cg // N_CHIPS, cg)
            q_ext = dpooled_ext / jnp.minimum(t1_ext, float(win))
            du = _window_sum(q_ext, win, False)[:ts] - dpooled_ext[:ts]
            dp_ref[:, gs] = du.astype(BF16)
            uc = tile(2, g)
            cc = tile(4, g)
            v = cc * uc
            vext = jnp.concatenate([prev(4, g) * prev(2, g), v], axis=0)
            v1 = _shift_rows(vext, 1, True)[HALO:]
            v2 = _shift_rows(vext, 2, True)[HALO:]
            cw = [cw_ref[g, tap:tap + 1, :] for tap in range(CONV_K)]
            y = cb_ref[:, gs] + cw[0] * v2 + cw[1] * v1 + cw[2] * v
            bc_ext = ahead(3, g)
            zc_ext = ahead(5, g)
            dy1_ext = dy_ahead(1, g)
            sc_ext = _sigmoid(zc_ext)
            silu_c_ext = zc_ext * sc_ext
            dyy_ext = dy1_ext * bc_ext * silu_c_ext
            dy1 = dy1_ext[:ts]
            bc = bc_ext[:ts]
            zc, sc = zc_ext[:ts], sc_ext[:ts]
            dsilu_c = sc * (1.0 + zc * (1.0 - sc))
            dp_ref[:, 3 * w + g * cg:3 * w + (g + 1) * cg] = (dy1 * y * silu_c_ext[:ts]).astype(BF16)
            dp_ref[:, 5 * w + g * cg:5 * w + (g + 1) * cg] = (dy1 * bc * y * dsilu_c).astype(BF16)
            dyy = dyy_ext[:ts]
            dcb_ref[:, gs] += jnp.sum(dyy, axis=0, keepdims=True)
            for tap, vt in enumerate((v2, v1, v)):
                dcw_ref[g, tap:tap + 1, :] += jnp.sum(dyy * vt, axis=0, keepdims=True)
            dv = (cw[2] * dyy + cw[1] * _shift_rows(dyy_ext, 1, False)[:ts]
                  + cw[0] * _shift_rows(dyy_ext, 2, False)[:ts])
            dp_ref[:, 4 * w + g * cg:4 * w + (g + 1) * cg] = (dv * uc).astype(BF16)
            dp_ref[:, 2 * w + g * cg:2 * w + (g + 1) * cg] = (dv * cc).astype(BF16)

    dpw_shape = (2, N_CHIPS, 2, cg // N_CHIPS, cg)
    return pl.pallas_call(
        body, name="mix_bwd", grid=(n_tiles,),
        in_specs=[pl.BlockSpec((ts, cols), lambda i: (i, 0)),
                  pl.BlockSpec((HALO, cols), lambda i: (jnp.maximum(i * hb - 1, 0), 0)),
                  pl.BlockSpec((HALO, cols), lambda i: (jnp.minimum((i + 1) * hb, last_hb), 0)),
                  pl.BlockSpec((2, ts, w), lambda i: (0, i, 0)),
                  pl.BlockSpec((2, HALO, w), lambda i: (0, jnp.minimum((i + 1) * hb, last_hb), 0)),
                  pl.BlockSpec(pool_wg.shape, lambda i: (0, 0, 0, 0)),
                  pl.BlockSpec((1, w), lambda i: (0, 0)),
                  pl.BlockSpec(conv_wg.shape, lambda i: (0, 0, 0)),
                  pl.BlockSpec((1, w), lambda i: (0, 0))],
        out_specs=[pl.BlockSpec((ts, cols), lambda i: (i, 0)),
                   pl.BlockSpec(dpw_shape, lambda i: (0, 0, 0, 0, 0)),
                   pl.BlockSpec((1, w), lambda i: (0, 0)),
                   pl.BlockSpec(conv_wg.shape, lambda i: (0, 0, 0)),
                   pl.BlockSpec((1, w), lambda i: (0, 0))],
        out_shape=[jax.ShapeDtypeStruct((s, cols), BF16), jax.ShapeDtypeStruct(dpw_shape, F32),
                   jax.ShapeDtypeStruct((1, w), F32), jax.ShapeDtypeStruct(conv_wg.shape, F32),
                   jax.ShapeDtypeStruct((1, w), F32)],
        compiler_params=_params(("arbitrary",), (2 * ts + 2 * HALO) * cols * 2 + (ts + HALO) * w * 8
                                + _nbytes(pool_wg.shape, BF16) + _nbytes(dpw_shape, F32),
                                40 * (ts + HALO) * cg * 4),
    )(proj, proj, proj, dys, dys, pool_wg, pool_scale, conv_wg, conv_b)


def _adamw(name, g, w, m, v):
    r, c = w.shape
    tr = _pick(r, max(8, (1 << 20) // (4 * c)), 8)

    def body(g_ref, w_ref, m_ref, v_ref, go_ref, d_ref, mo_ref, vo_ref):
        gv = g_ref[...]
        mn = ADAM_B1 * m_ref[...] + (1.0 - ADAM_B1) * gv
        vn = ADAM_B2 * v_ref[...] + (1.0 - ADAM_B2) * (gv * gv)
        m_hat = mn / (1.0 - ADAM_B1 ** ADAM_STEP)
        v_hat = vn / (1.0 - ADAM_B2 ** ADAM_STEP)
        go_ref[...] = gv
        d_ref[...] = -ADAM_LR * (m_hat / (jnp.sqrt(v_hat) + ADAM_EPS) + ADAM_WD * w_ref[...])
        mo_ref[...] = mn
        vo_ref[...] = vn

    blk = pl.BlockSpec((tr, c), lambda i: (i, 0))
    sh = jax.ShapeDtypeStruct((r, c), F32)
    return pl.pallas_call(
        body, name=name, grid=(r // tr,), in_specs=[blk] * 4, out_specs=[blk] * 4,
        out_shape=[sh] * 4, compiler_params=_params(("parallel",), tr * c * 32, 4 * tr * c * 4),
    )(g, w, m, v)


def _pair_add(name, g, r1, c_idx):
    _, r, c = g.shape
    tr = _pick(r, max(16, (2 << 20) // (2 * c)), 16)

    def body(c_ref, g_ref, r_ref, o_ref):
        o_ref[...] = (g_ref[...].astype(F32) + r_ref[...].astype(F32)).astype(BF16)

    return pl.pallas_call(
        body, name=name,
        grid_spec=pltpu.PrefetchScalarGridSpec(
            num_scalar_prefetch=1, grid=(r // tr,),
            in_specs=[pl.BlockSpec((None, tr, c), lambda i, cr: (cr[0], i, 0)),
                      pl.BlockSpec((tr, c), lambda i, cr: (i, 0))],
            out_specs=pl.BlockSpec((tr, c), lambda i, cr: (i, 0))),
        out_shape=jax.ShapeDtypeStruct((r, c), BF16),
        compiler_params=_params(("parallel",), tr * c * 6, 3 * tr * c * 4),
    )(c_idx, g, r1)


def _adamw_halves(name, g_own, g_other, c_idx, w, m, v):
    _, r, c = w.shape
    tr = _pick(r, max(8, (5 << 18) // (4 * c)), 8)

    def body(c_ref, go_ref, gt_ref, w_ref, m_ref, v_ref, g_out, d_ref, mo_ref, vo_ref):
        gv = jnp.where(pl.program_id(0) == c_ref[0], go_ref[...], gt_ref[...])
        mn = ADAM_B1 * m_ref[...] + (1.0 - ADAM_B1) * gv
        vn = ADAM_B2 * v_ref[...] + (1.0 - ADAM_B2) * (gv * gv)
        m_hat = mn / (1.0 - ADAM_B1 ** ADAM_STEP)
        v_hat = vn / (1.0 - ADAM_B2 ** ADAM_STEP)
        g_out[...] = gv
        d_ref[...] = -ADAM_LR * (m_hat / (jnp.sqrt(v_hat) + ADAM_EPS) + ADAM_WD * w_ref[...])
        mo_ref[...] = mn
        vo_ref[...] = vn

    blk = pl.BlockSpec((None, tr, c), lambda h, i, cr: (h, i, 0))
    sh = jax.ShapeDtypeStruct(w.shape, F32)
    return pl.pallas_call(
        body, name=name,
        grid_spec=pltpu.PrefetchScalarGridSpec(
            num_scalar_prefetch=1, grid=(2, r // tr),
            in_specs=[pl.BlockSpec((tr, c), lambda h, i, cr: (jnp.where(h == cr[0], i, 0), 0)),
                      pl.BlockSpec((tr, c), lambda h, i, cr: (jnp.where(h == cr[0], 0, i), 0)),
                      blk, blk, blk],
            out_specs=[blk] * 4),
        out_shape=[sh] * 4,
        compiler_params=_params(("arbitrary", "arbitrary"), tr * c * 36, 4 * tr * c * 4),
    )(c_idx, g_own, g_other, w, m, v)


def _sum_chips(name, part, r2, chip_idx, row0=0, prev=None):
    n, r, c = r2.shape
    rc = part.shape[1]
    tr = _pick(math.gcd(rc, row0) if row0 else rc, max(16, (1 << 20) // (2 * c)), 16)
    b0 = row0 // tr

    def body(ch_ref, own_ref, *rest):
        slots, o_ref = rest[:n], rest[-1]
        acc = None
        for s in range(n):
            term = jnp.where(ch_ref[0] == s, own_ref[...], slots[s][...]).astype(F32)
            acc = term if acc is None else acc + term
        o_ref[...] = acc

    def slot_spec(s):
        return pl.BlockSpec((None, tr, c), lambda i, ch: (jnp.where(ch[0] == s, (s + 1) % n, s), b0 + i, 0))

    extra = [] if prev is None else [prev]
    return pl.pallas_call(
        body, name=name,
        grid_spec=pltpu.PrefetchScalarGridSpec(
            num_scalar_prefetch=1, grid=(rc // tr,),
            in_specs=[pl.BlockSpec((None, tr, c), lambda i, ch: (ch[0], i, 0))]
            + [slot_spec(s) for s in range(n)] + [ANY] * len(extra),
            out_specs=pl.BlockSpec((tr, c), lambda i, ch: (b0 + i, 0))),
        out_shape=jax.ShapeDtypeStruct((r, c), F32),
        input_output_aliases={2 + n: 0} if extra else {},
        compiler_params=_params(("parallel",), tr * c * (2 * n + 6), 3 * tr * c * 4),
    )(chip_idx, part, *([r2] * n), *extra)


def _sum_devices(packs):
    n, r, c = packs.shape

    def body(p_ref, o_ref):
        acc = p_ref[0]
        for k in range(1, n):
            acc = acc + p_ref[k]
        o_ref[...] = acc

    return pl.pallas_call(
        body, name="sum_devices", out_shape=jax.ShapeDtypeStruct((r, c), F32),
        in_specs=[pl.BlockSpec(memory_space=pltpu.VMEM)],
        out_specs=pl.BlockSpec(memory_space=pltpu.VMEM),
    )(packs)


def _place():
    x, y, c = lax.axis_index("x"), lax.axis_index("y"), lax.axis_index("c")
    return x, y, c


def _chip_peers(x, y):
    out = []
    for k, (fx, fy) in enumerate(((0, 1), (1, 0), (1, 1))):
        px = 1 - x if fx else x
        py = 1 - y if fy else y
        out.append((k, px, py, 2 * px + py))
    return out


def _gather_weights(big, small, relay=False):
    nb, ns = len(big), len(small)

    class Copies:
        def __init__(self, c_in, c_out, sems):
            b_in, s_in = c_in[:nb], c_in[nb:]
            b_out, s_out = c_out[:nb], c_out[nb:]
            ici_s, ici_r, d2d_s, d2d_r, own_s, own_r = sems[:6]
            x, y, c = _place()
            chip = 2 * x + y
            sibling = (x, y, 1 - c)
            peers = _chip_peers(x, y)
            self.own = [pltpu.make_async_remote_copy(
                src_ref=b_in[t], dst_ref=b_out[t].at[chip], send_sem=own_s.at[t], recv_sem=own_r.at[t],
                device_id=sibling, device_id_type=MESH) for t in range(nb)]
            self.ici = [pltpu.make_async_remote_copy(
                src_ref=b_in[t].at[c], dst_ref=b_out[t].at[chip, c],
                send_sem=ici_s.at[t, k], recv_sem=ici_r.at[t, k],
                device_id=(px, py, c), device_id_type=MESH)
                for t in range(nb) for (k, px, py, pchip) in peers if not (relay and k == 2)]
            self.relay = []
            if relay:
                south = c == 0
                from_chip = jnp.where(south, 2 * x + (1 - y), 2 * (1 - x) + y)
                to = (jnp.where(south, 1 - x, x), jnp.where(south, y, 1 - y), c)
                self.relay = [pltpu.make_async_remote_copy(
                    src_ref=b_out[t].at[from_chip, c], dst_ref=b_out[t].at[from_chip, c],
                    send_sem=ici_s.at[t, 2], recv_sem=ici_r.at[t, 2],
                    device_id=to, device_id_type=MESH) for t in range(nb)]
            self.landed = [pltpu.make_async_remote_copy(
                src_ref=b_out[t].at[pchip, c], dst_ref=b_out[t].at[pchip, c],
                send_sem=ici_s.at[t, k], recv_sem=ici_r.at[t, k],
                device_id=sibling, device_id_type=MESH)
                for t in range(nb) for (k, px, py, pchip) in peers]
            self.passed = [pltpu.make_async_remote_copy(
                src_ref=b_out[t].at[pchip, c], dst_ref=b_out[t].at[pchip, c],
                send_sem=d2d_s.at[t, k], recv_sem=d2d_r.at[t, k],
                device_id=sibling, device_id_type=MESH)
                for t in range(nb) for (k, px, py, pchip) in peers]
            self.from_sibling = [pltpu.make_async_remote_copy(
                src_ref=b_out[t].at[pchip, 1 - c], dst_ref=b_out[t].at[pchip, 1 - c],
                send_sem=d2d_s.at[t, k], recv_sem=d2d_r.at[t, k],
                device_id=sibling, device_id_type=MESH)
                for t in range(nb) for (k, px, py, pchip) in peers]
            self.small, self.small_landed, self.local = [], [], []
            if ns:
                sm_s, sm_r, loc = sems[6:]
                self.local = [pltpu.make_async_copy(s_in[t], s_out[t].at[chip], loc.at[t]) for t in range(ns)]
                self.small = [pltpu.make_async_remote_copy(
                    src_ref=s_in[t], dst_ref=s_out[t].at[chip],
                    send_sem=sm_s.at[t, k], recv_sem=sm_r.at[t, k],
                    device_id=(px, py, c), device_id_type=MESH)
                    for t in range(ns) for (k, px, py, pchip) in peers]
                self.small_landed = [pltpu.make_async_remote_copy(
                    src_ref=s_in[t], dst_ref=s_out[t].at[pchip],
                    send_sem=sm_s.at[t, k], recv_sem=sm_r.at[t, k],
                    device_id=sibling, device_id_type=MESH)
                    for t in range(ns) for (k, px, py, pchip) in peers]

    def start(c_in, c_out, sems):
        cps = Copies(c_in, c_out, sems)
        for cp in cps.local + cps.own + cps.ici + cps.small:
            cp.start()

    def finish(c_in, c_out, sems):
        cps = Copies(c_in, c_out, sems)
        for t in range(nb):
            for k in range(3):
                if relay and k == 2:
                    cps.relay[t].start()
                cps.landed[3 * t + k].wait_recv()
                cps.passed[3 * t + k].start()
        for cp in cps.small_landed + cps.from_sibling:
            cp.wait_recv()
        for cp in cps.ici + cps.relay + cps.small + cps.passed:
            cp.wait_send()
        for cp in cps.own + cps.local:
            cp.wait()

    out_shape = [jax.ShapeDtypeStruct((N_CHIPS,) + b.shape, b.dtype) for b in big]
    out_shape += [jax.ShapeDtypeStruct((N_CHIPS,) + s.shape, s.dtype) for s in small]
    dma = pltpu.SemaphoreType.DMA
    scratch = [dma((nb, 3)), dma((nb, 3)), dma((nb, 3)), dma((nb, 3)), dma((nb,)), dma((nb,))]
    if ns:
        scratch += [dma((ns, 3)), dma((ns, 3)), dma((ns,))]
    return _Exchange(list(big) + list(small), out_shape, scratch, start, finish)


class _Exchange:
    def __init__(self, operands, out_shape, scratch, start, finish, aliases=None):
        self.operands, self.out_shape, self.scratch = operands, out_shape, scratch
        self.start, self.finish, self.aliases = start, finish, dict(aliases or {})


def _run_exchange(name, ex):
    n_i, n_o = len(ex.operands), len(ex.out_shape)

    def body(*refs):
        ins, outs, sems = refs[:n_i], refs[n_i:n_i + n_o], refs[n_i + n_o:]
        ex.start(ins, outs, sems)
        ex.finish(ins, outs, sems)

    return list(pl.pallas_call(
        body, name=name, out_shape=list(ex.out_shape), in_specs=[ANY] * n_i, out_specs=[ANY] * n_o,
        scratch_shapes=list(ex.scratch), input_output_aliases=ex.aliases,
    )(*ex.operands))


def _to_sibling(arrays, other_half):
    n = len(arrays)

    def copies(a_in, a_out, sems):
        send, recv = sems
        x, y, c = _place()
        return [pltpu.make_async_remote_copy(
            src_ref=a_in[t].at[1 - c] if other_half else a_in[t], dst_ref=a_out[t],
            send_sem=send.at[t], recv_sem=recv.at[t],
            device_id=(x, y, 1 - c), device_id_type=MESH) for t in range(n)]

    def start(a_in, a_out, sems):
        for cp in copies(a_in, a_out, sems):
            cp.start()

    def finish(a_in, a_out, sems):
        for cp in copies(a_in, a_out, sems):
            cp.wait()

    dma = pltpu.SemaphoreType.DMA
    shapes = [jax.ShapeDtypeStruct(a.shape[1:] if other_half else a.shape, a.dtype) for a in arrays]
    return _Exchange(list(arrays), shapes, [dma((n,)), dma((n,))], start, finish)


def _scatter_partials(parts, rows=None, row0=0, prev=None):
    n = len(parts)
    land = [(p.shape[0], p.shape[1] if rows is None else rows, p.shape[2]) for p in parts]

    def window(ref, slot, t):
        return ref.at[slot, pl.ds(row0, parts[t].shape[1])]

    def sends(p_in, r_out, sems):
        send, recv = sems
        x, y, c = _place()
        chip = 2 * x + y
        return [pltpu.make_async_remote_copy(
            src_ref=p_in[t].at[pchip], dst_ref=window(r_out[t], chip, t),
            send_sem=send.at[t, k], recv_sem=recv.at[t, k],
            device_id=(px, py, c), device_id_type=MESH)
            for t in range(n) for (k, px, py, pchip) in _chip_peers(x, y)]

    def start(p_in, r_out, sems):
        for cp in sends(p_in, r_out, sems):
            cp.start()

    def finish(p_in, r_out, sems):
        send, recv = sems
        x, y, c = _place()
        for t in range(n):
            for (k, px, py, pchip) in _chip_peers(x, y):
                pltpu.make_async_remote_copy(
                    src_ref=p_in[t].at[pchip], dst_ref=window(r_out[t], pchip, t),
                    send_sem=send.at[t, k], recv_sem=recv.at[t, k],
                    device_id=(px, py, c), device_id_type=MESH).wait_recv()
        for cp in sends(p_in, r_out, sems):
            cp.wait_send()

    dma = pltpu.SemaphoreType.DMA
    operands = list(parts) + (list(prev) if prev else [])
    return _Exchange(operands, [jax.ShapeDtypeStruct(sh, p.dtype) for sh, p in zip(land, parts)],
                     [dma((n, 3)), dma((n, 3))], start, finish,
                     aliases={n + t: t for t in range(n)} if prev else None)


def _w_in_grad_sibling(h, dpa, dgl, c_idx, carried, *, tm, tn, rows, pq):
    s, d = h.shape
    wa, wg = dpa.shape[1], dgl.shape[2]
    p = wa + 2 * wg
    ni, nj = rows // tm, p // tn
    na, qg, qp = wa // tn, wg // tn, pq // tn
    n_steps = ni * nj
    n_ci, n_co = len(carried.operands), len(carried.out_shape)

    def body(c_ref, h_ref, a_ref, g_ref, *rest):
        c_in = rest[:n_ci]
        r1_ref = rest[n_ci]
        c_out = rest[n_ci + 1:n_ci + 1 + n_co]
        slots, send, recv = rest[n_ci + 1 + n_co:n_ci + 4 + n_co]
        sems = rest[n_ci + 4 + n_co:]
        i, j = pl.program_id(0), pl.program_id(1)
        step = i * nj + j
        slot = lax.rem(step, 2)
        x, y, c = _place()
        sibling = (x, y, 1 - c)

        @pl.when(step == 0)
        def _():
            carried.start(c_in, c_out, sems)

        def tile_copy(sl):
            return pltpu.make_async_remote_copy(
                src_ref=slots.at[sl],
                dst_ref=r1_ref.at[j // qp, pl.ds(i * tm, tm), pl.ds((j % qp) * tn, tn)],
                send_sem=send.at[sl], recv_sem=recv, device_id=sibling, device_id_type=MESH)

        @pl.when(step >= 2)
        def _():
            tile_copy(slot).wait_send()

        def emit(b_ref):
            acc = lax.dot_general(h_ref[...], b_ref[...], TN, preferred_element_type=F32)
            slots[slot] = acc.astype(BF16)

        pl.when(j < na)(lambda: emit(a_ref))
        pl.when(j >= na)(lambda: emit(g_ref))
        tile_copy(slot).start()

        @pl.when(step == n_steps - 1)
        def _():
            tile_copy(0).wait_send()
            tile_copy(1).wait_send()
            pltpu.make_async_remote_copy(
                src_ref=r1_ref, dst_ref=r1_ref, send_sem=send.at[0], recv_sem=recv,
                device_id=sibling, device_id_type=MESH).wait_recv()
            carried.finish(c_in, c_out, sems)

    nbh = rows // tm
    dma = pltpu.SemaphoreType.DMA
    blk = s * tm * 2 + 2 * s * tn * 2
    res = pl.pallas_call(
        body, name="w_in_grad_sibling",
        grid_spec=pltpu.PrefetchScalarGridSpec(
            num_scalar_prefetch=1, grid=(ni, nj),
            in_specs=[pl.BlockSpec((s, tm), lambda i, j, cr: (0, (1 - cr[0]) * nbh + i)),
                      pl.BlockSpec((s, tn), lambda i, j, cr: (0, jnp.minimum(j, na - 1))),
                      pl.BlockSpec((None, s, tn),
                                   lambda i, j, cr: (jnp.maximum(j - na, 0) // qg, 0, jnp.maximum(j - na, 0) % qg))]
            + [ANY] * n_ci,
            out_specs=[ANY] * (1 + n_co),
            scratch_shapes=[pltpu.VMEM((2, tm, tn), BF16), dma((2,)), dma] + list(carried.scratch)),
        out_shape=[jax.ShapeDtypeStruct((N_CHIPS, rows, pq), BF16)] + list(carried.out_shape),
        compiler_params=_params(("arbitrary", "arbitrary"), blk, 2 * tm * tn * 2 + 2 * tm * tn * 4),
    )(c_idx, h, dpa, dgl, *carried.operands)
    return res[0], res[1:]


def _gather_packs(pack):
    def body(p_ref, o_ref, send, recv, loc):
        x, y, c = _place()
        me = 4 * x + 2 * y + c
        mine = pltpu.make_async_copy(p_ref, o_ref.at[me], loc)
        mine.start()
        flips = [(fx, fy, fc) for fx in (0, 1) for fy in (0, 1) for fc in (0, 1)][1:]
        cps = []
        for k, (fx, fy, fc) in enumerate(flips):
            peer = (1 - x if fx else x, 1 - y if fy else y, 1 - c if fc else c)
            cps.append(pltpu.make_async_remote_copy(
                src_ref=p_ref, dst_ref=o_ref.at[me], send_sem=send.at[k], recv_sem=recv.at[k],
                device_id=peer, device_id_type=MESH))
        for cp in cps:
            cp.start()
        for k, (fx, fy, fc) in enumerate(flips):
            px, py, pc = (1 - x if fx else x, 1 - y if fy else y, 1 - c if fc else c)
            pltpu.make_async_remote_copy(
                src_ref=p_ref, dst_ref=o_ref.at[4 * px + 2 * py + pc],
                send_sem=send.at[k], recv_sem=recv.at[k],
                device_id=(px, py, pc), device_id_type=MESH).wait_recv()
        for cp in cps:
            cp.wait_send()
        mine.wait()

    dma = pltpu.SemaphoreType.DMA
    return pl.pallas_call(
        body, name="gather_packs", out_shape=jax.ShapeDtypeStruct((N_DEV,) + pack.shape, pack.dtype),
        in_specs=[ANY], out_specs=ANY, scratch_shapes=[dma((7,)), dma((7,)), dma],
    )(pack)


def _flat_pack(pieces):
    flat = jnp.concatenate([p.reshape(-1) for p in pieces])
    pad = (-flat.shape[0]) % (8 * LANES)
    flat = jnp.concatenate([flat, jnp.zeros((pad,), F32)])
    return flat.reshape(-1, LANES)


def _unpack(pack, shapes):
    flat = pack.reshape(-1)
    out, off = [], 0
    for sh in shapes:
        n = 1
        for s in sh:
            n *= s
        out.append(flat[off:off + n].reshape(sh))
        off += n
    return out


def kernel(x, norm_w, w_in, pool_w, pool_scale, conv_w, conv_b, gate_b, w_branch, w_out, final_norm_w, loss_target, m_norm_w, m_w_in, m_pool_w, m_pool_scale, m_conv_w, m_conv_b, m_gate_b, m_w_branch, m_w_out, m_final_norm_w, v_norm_w, v_w_in, v_pool_w, v_pool_scale, v_conv_w, v_conv_b, v_gate_b, v_w_branch, v_w_out, v_final_norm_w):
    _, s, d = x.shape
    w = d // 2
    cg = w // N_POOL_GROUPS
    p = 6 * w + 2 * d
    pq = p // N_CHIPS
    dq = d // N_CHIPS
    assert w_in.shape == (1, d, pq) and w_branch.shape == (1, 2, w, dq) and w_out.shape == (1, dq, d)
    assert pool_w.shape == (1, N_POOL_GROUPS, cg // N_CHIPS, cg) and conv_w.shape == (1, CONV_K, cg)

    x2d = x.reshape(s, d)
    tgt = loss_target.reshape(s, d)
    c_idx = lax.axis_index("c").astype(jnp.int32).reshape(1)
    chip = 2 * lax.axis_index("x") + lax.axis_index("y")

    big_w = [w_in.reshape(d, pq), w_out.reshape(dq, d), w_branch.reshape(2 * w, dq),
             pool_w.reshape(cg, cg)]
    names = ["in", "out", "branch", "pool"]
    big_b = [_cast_bf16("cast_" + nm, a) for nm, a in zip(names, big_w)]
    big_b = [b.reshape(2, b.shape[0] // 2, b.shape[1]) for b in big_b]
    chip_idx = chip.astype(jnp.int32).reshape(1)

    h = _rms_fwd(x2d, norm_w)

    tm = _pick(s, 1024, 16)
    tn_p = _pick(math.gcd(pq, 6 * w, d), 1024, LANES)
    qp = pq // tn_p
    tk_d = _pick(d, 4096, LANES)
    cast_out = lambda accs, ins, outs, ids: outs[0].__setitem__(Ellipsis, accs[0].astype(BF16))
    (proj,), (wg_in,) = _mm(
        "proj_own", grid=(s // tm, qp, d // tk_d), operands=[h, big_b[0].reshape(d, pq)],
        in_specs=[pl.BlockSpec((tm, tk_d), lambda i, j, k, ch: (i, k)),
                  pl.BlockSpec((tk_d, tn_p), lambda i, j, k, ch: (k, j))],
        out_shape=[jax.ShapeDtypeStruct((s, p), BF16)],
        out_specs=[pl.BlockSpec((tm, tn_p), lambda i, j, k, ch: (i, ch[0] * qp + j))],
        pairs=[(0, 1, 0, None)], dims=NN, acc_shapes=[(tm, tn_p)], epilogue=cast_out,
        comm=_gather_weights(big_b[:1], [], relay=True), prefetch=chip_idx,
    )
    wg_in = wg_in.reshape(N_CHIPS, d, pq)
    other = lambda j, ch: jnp.bitwise_xor(ch[0], 1 + j // qp)
    (proj,), (wg_out, wg_br, wg_pool, wg_cw, wg_gb) = _mm(
        "proj_others", grid=(s // tm, 3 * qp, d // tk_d), operands=[h, wg_in, proj],
        in_specs=[pl.BlockSpec((tm, tk_d), lambda i, j, k, ch: (i, k)),
                  pl.BlockSpec((None, tk_d, tn_p), lambda i, j, k, ch: (other(j, ch), k, j % qp)),
                  None],
        out_shape=[jax.ShapeDtypeStruct((s, p), BF16)],
        out_specs=[pl.BlockSpec((tm, tn_p), lambda i, j, k, ch: (i, other(j, ch) * qp + j % qp))],
        pairs=[(0, 1, 0, None)], dims=NN, acc_shapes=[(tm, tn_p)], epilogue=cast_out,
        comm=_gather_weights(big_b[1:], [conv_w.reshape(CONV_K, cg), gate_b.reshape(2, dq)], relay=True),
        prefetch=chip_idx, aliases={2: 0},
    )
    wg_out = wg_out.reshape(d, d)
    wg_pool = wg_pool.reshape(N_CHIPS, N_POOL_GROUPS, cg // N_CHIPS, cg)

    ys = _mix_fwd(proj, wg_pool, pool_scale, wg_cw, conv_b, w)

    tn_d = _pick(dq, 1024, LANES)
    qd = dq // tn_d
    tk_w = _pick(w, 2048, LANES)
    tm_g = _pick(s, 512, 16)
    gl0 = (6 * w) // tn_d
    gl1 = (6 * w + d) // tn_d

    def gate_specs(im, rows):
        return [pl.BlockSpec((rows, tn_d), lambda *a: (im(*a)[0], gl0 + im(*a)[1])),
                pl.BlockSpec((rows, tn_d), lambda *a: (im(*a)[0], gl1 + im(*a)[1])),
                pl.BlockSpec((None, 2, tn_d), lambda *a: (im(*a)[1] // qd, 0, im(*a)[1] % qd))]

    def merge_epilogue(accs, ins, outs, ids):
        gb = ins[6][...]
        g0 = _sigmoid(ins[4][...].astype(F32) + gb[0:1])
        g1 = _sigmoid(ins[5][...].astype(F32) + gb[1:2])
        outs[0][...] = (g0 * accs[0] + g1 * accs[1]).astype(BF16)
        outs[1][0] = accs[0].astype(BF16)
        outs[1][1] = accs[1].astype(BF16)

    (merged, br), _ = _mm(
        "branch_merge", grid=(s // tm_g, d // tn_d, w // tk_w),
        operands=[ys, wg_br, ys, wg_br, proj, proj, wg_gb],
        in_specs=[pl.BlockSpec((None, tm_g, tk_w), lambda i, j, k: (0, i, k)),
                  pl.BlockSpec((None, None, tk_w, tn_d), lambda i, j, k: (j // qd, 0, k, j % qd)),
                  pl.BlockSpec((None, tm_g, tk_w), lambda i, j, k: (1, i, k)),
                  pl.BlockSpec((None, None, tk_w, tn_d), lambda i, j, k: (j // qd, 1, k, j % qd)),
                  *gate_specs(lambda i, j, k: (i, j), tm_g)],
        out_shape=[jax.ShapeDtypeStruct((s, d), BF16), jax.ShapeDtypeStruct((2, s, d), BF16)],
        out_specs=[pl.BlockSpec((tm_g, tn_d), lambda i, j, k: (i, j)),
                   pl.BlockSpec((2, tm_g, tn_d), lambda i, j, k: (0, i, j))],
        pairs=[(0, 1, 0, None), (2, 3, 1, None)], dims=NN, acc_shapes=[(tm_g, tn_d)] * 2,
        epilogue=merge_epilogue, temp_bytes=6 * tm_g * tn_d * 4,
    )

    tn_f = _pick(d, 1024, LANES)
    o = _mm(
        "out_proj", grid=(s // tm, d // tn_f, d // tk_d), operands=[merged, wg_out],
        in_specs=[pl.BlockSpec((tm, tk_d), lambda i, j, k: (i, k)),
                  pl.BlockSpec((tk_d, tn_f), lambda i, j, k: (k, j))],
        out_shape=[jax.ShapeDtypeStruct((s, d), F32)],
        out_specs=[pl.BlockSpec((tm, tn_f), lambda i, j, k: (i, j))],
        pairs=[(0, 1, 0, None)], dims=NN, acc_shapes=[(tm, tn_f)],
        epilogue=lambda accs, ins, outs, ids: outs[0].__setitem__(Ellipsis, accs[0]),
    )[0][0]

    dx2, dx2b, loss_part, g_fnw =_head(x2d, o, tgt, final_norm_w.reshape(1, d))

    def gate_bwd_epilogue(accs, ins, outs, ids):
        dm = accs[0]
        gb = ins[5][...]
        i = ids[1]

        @pl.when(i == 0)
        def _():
            outs[2][...] = jnp.zeros(outs[2].shape, F32)

        for n in range(2):
            gate = _sigmoid(ins[3 + n][...].astype(F32) + gb[n:n + 1])
            outs[0][n] = (dm * gate).astype(BF16)
            dgl = dm * ins[2][n].astype(F32) * gate * (1.0 - gate)
            outs[1][n] = dgl.astype(BF16)
            outs[2][n:n + 1, :] += jnp.sum(dgl, axis=0, keepdims=True)

    tm_b = _pick(s, 256, 16)
    (d_br, dgl, g_gb), _ = _mm(
        "out_proj_bwd_gate", grid=(d // tn_d, s // tm_b, d // tk_d),
        operands=[dx2b, wg_out, br, proj, proj, wg_gb],
        in_specs=[pl.BlockSpec((tm_b, tk_d), lambda j, i, k: (i, k)),
                  pl.BlockSpec((tn_d, tk_d), lambda j, i, k: (j, k)),
                  pl.BlockSpec((2, tm_b, tn_d), lambda j, i, k: (0, i, j)),
                  *gate_specs(lambda j, i, k: (i, j), tm_b)],
        out_shape=[jax.ShapeDtypeStruct((2, s, d), BF16), jax.ShapeDtypeStruct((2, s, d), BF16),
                   jax.ShapeDtypeStruct((2, d), F32)],
        out_specs=[pl.BlockSpec((2, tm_b, tn_d), lambda j, i, k: (0, i, j)),
                   pl.BlockSpec((2, tm_b, tn_d), lambda j, i, k: (0, i, j)),
                   pl.BlockSpec((2, tn_d), lambda j, i, k: (0, j))],
        pairs=[(0, 1, 0, None)], dims=NT, acc_shapes=[(tm_b, tn_d)],
        epilogue=gate_bwd_epilogue, semantics=("parallel", "arbitrary", "arbitrary"),
        temp_bytes=8 * tm_b * tn_d * 4,
    )

    hh_out = d // 8
    tm_o = _pick(hh_out, 512, LANES)
    nb_o = hh_out // tm_o
    tk_s = _pick(s, 4096, LANES)
    g_out = _mm(
        "w_out_grad", grid=(d // tm_o, d // tn_f, s // tk_s), operands=[merged, dx2b],
        in_specs=[pl.BlockSpec((tk_s, tm_o), lambda i, j, k: (k, i)),
                  pl.BlockSpec((tk_s, tn_f), lambda i, j, k: (k, j))],
        out_shape=[jax.ShapeDtypeStruct((2, N_CHIPS, hh_out, d), BF16)],
        out_specs=[pl.BlockSpec((None, None, tm_o, tn_f),
                                lambda i, j, k: ((i // nb_o) % 2, i // (2 * nb_o), i % nb_o, j))],
        pairs=[(0, 1, 0, None)], dims=TN, acc_shapes=[(tm_o, tn_f)],
        epilogue=lambda accs, ins, outs, ids: outs[0].__setitem__(Ellipsis, accs[0].astype(BF16)),
    )[0][0]

    tn_w = _pick(w, 2048, LANES)
    tk_q = _pick(dq, 1024, LANES)
    qk = dq // tk_q
    (dys,), from_sibling_out = _mm(
        "branch_bwd", grid=(2, s // tm, w // tn_w, d // tk_q), operands=[d_br, wg_br],
        comm=_to_sibling([g_out], other_half=True),
        in_specs=[pl.BlockSpec((None, tm, tk_q), lambda n, i, j, k: (n, i, k)),
                  pl.BlockSpec((None, None, tn_w, tk_q), lambda n, i, j, k: (k // qk, n, j, k % qk))],
        out_shape=[jax.ShapeDtypeStruct((2, s, w), F32)],
        out_specs=[pl.BlockSpec((None, tm, tn_w), lambda n, i, j, k: (n, i, j))],
        pairs=[(0, 1, 0, None)], dims=NT, acc_shapes=[(tm, tn_w)],
        epilogue=lambda accs, ins, outs, ids: outs[0].__setitem__(Ellipsis, accs[0]),
    )

    tm_w = _pick(w, 1024, LANES)
    g_br = _mm(
        "w_branch_grad", grid=(2, w // tm_w, d // tn_d, s // tk_s), operands=[ys, d_br],
        in_specs=[pl.BlockSpec((None, tk_s, tm_w), lambda n, i, j, k: (n, k, i)),
                  pl.BlockSpec((None, tk_s, tn_d), lambda n, i, j, k: (n, k, j))],
        out_shape=[jax.ShapeDtypeStruct((2, N_CHIPS, w, dq), BF16)],
        out_specs=[pl.BlockSpec((None, None, tm_w, tn_d), lambda n, i, j, k: (n, j // qd, i, j % qd))],
        pairs=[(0, 1, 0, None)], dims=TN, acc_shapes=[(tm_w, tn_d)],
        epilogue=lambda accs, ins, outs, ids: outs[0].__setitem__(Ellipsis, accs[0].astype(BF16)),
    )[0][0]

    dpa, g_pool, g_ps, g_cw, g_cb = _mix_bwd(proj, dys, wg_pool, pool_scale, wg_cw, conv_b, w)
    g_pool = g_pool.astype(BF16).reshape(2, N_CHIPS, 2 * (cg // N_CHIPS), cg)

    early = [g_out, g_br, g_pool]
    from_sibling_early = list(from_sibling_out) + _run_exchange(
        "pair_exchange", _to_sibling(early[1:], other_half=True))
    parts_early = []
    for nm, g, r1 in zip(names[1:], early, from_sibling_early):
        _, nc, hh, wd = g.shape
        pt = _pair_add("pair_add_" + nm, g.reshape(2, nc * hh, wd), r1.reshape(nc * hh, wd), c_idx)
        parts_early.append(pt.reshape(nc, hh, wd))

    na = (6 * w) // tn_p
    qg = d // tn_p
    hh_in = d // 2
    assert s == tk_s
    tm_i = _pick(hh_in, 1024, LANES)
    tn_i = _pick(tn_p, 512, LANES)
    from_sibling, recv_early = _w_in_grad_sibling(
        h, dpa, dgl, c_idx, _scatter_partials(parts_early), tm=tm_i, tn=tn_i, rows=hh_in, pq=pq)

    tm_o2 = _pick(hh_in, 512, LANES)
    nb_o2 = hh_in // tm_o2

    def own_rows(name, first, count, comm):
        def add_sibling(accs, ins, outs, ids):
            outs[0][...] = (accs[0] + ins[3][...].astype(F32)).astype(BF16)

        return _mm(
            name, grid=(p // tn_p, count, 1), operands=[h, dpa, dgl, from_sibling], comm=comm, prefetch=c_idx,
            in_specs=[pl.BlockSpec((tk_s, tm_o2), lambda j, i, k, cr: (0, cr[0] * nb_o2 + first + i)),
                      pl.BlockSpec((tk_s, tn_p), lambda j, i, k, cr: (0, jnp.minimum(j, na - 1))),
                      pl.BlockSpec((None, tk_s, tn_p),
                                   lambda j, i, k, cr: (jnp.maximum(j - na, 0) // qg, 0, jnp.maximum(j - na, 0) % qg)),
                      pl.BlockSpec((None, tm_o2, tn_p), lambda j, i, k, cr: (j // qp, first + i, j % qp))],
            out_shape=[jax.ShapeDtypeStruct((N_CHIPS, count * tm_o2, pq), BF16)],
            out_specs=[pl.BlockSpec((None, tm_o2, tn_p), lambda j, i, k, cr: (j // qp, i, j % qp))],
            pairs=[(0, 1, 0, lambda ids: ids[0] < na), (0, 2, 0, lambda ids: ids[0] >= na)],
            dims=TN, acc_shapes=[(tm_o2, tn_p)], epilogue=add_sibling)

    halves_early = [_sum_chips("sum_chips_" + nm, pt, r2, chip_idx)
                    for nm, pt, r2 in zip(names[1:], parts_early, recv_early)]
    share_early = _to_sibling(halves_early, other_half=False)

    n_first = 1 if nb_o2 > 1 else 0
    if n_first:
        (part_a,), others_early = own_rows("w_in_grad_own_a", 0, n_first, share_early)
        scatter_a = _scatter_partials([part_a], rows=hh_in)
        (part_b,), land_a = own_rows("w_in_grad_own_b", n_first, nb_o2 - n_first, scatter_a)
        scatter_b = _scatter_partials([part_b], rows=hh_in, row0=n_first * tm_o2, prev=land_a)
    else:
        (part_b,), others_early = own_rows("w_in_grad_own", 0, nb_o2, share_early)
        scatter_b = _scatter_partials([part_b])

    tn_h = _pick(d, 2048, LANES)
    (dh,), recv_in = _mm(
        "proj_bwd", grid=(s // tm, d // tn_h, p // tn_p), operands=[dpa, dgl, wg_in],
        comm=scatter_b,
        in_specs=[pl.BlockSpec((tm, tn_p), lambda i, j, k: (i, jnp.minimum(k, na - 1))),
                  pl.BlockSpec((None, tm, tn_p),
                               lambda i, j, k: (jnp.maximum(k - na, 0) // qg, i, jnp.maximum(k - na, 0) % qg)),
                  pl.BlockSpec((None, tn_h, tn_p), lambda i, j, k: (k // qp, j, k % qp))],
        out_shape=[jax.ShapeDtypeStruct((s, d), F32)],
        out_specs=[pl.BlockSpec((tm, tn_h), lambda i, j, k: (i, j))],
        pairs=[(0, 2, 0, lambda ids: ids[2] < na), (1, 2, 0, lambda ids: ids[2] >= na)],
        dims=NT, acc_shapes=[(tm, tn_h)],
        epilogue=lambda accs, ins, outs, ids: outs[0].__setitem__(Ellipsis, accs[0]),
    )

    grad_x, g_nw = _rms_bwd(x2d, dh, dx2, norm_w)

    if n_first:
        half_in = _sum_chips("sum_chips_in_a", part_a, recv_in[0], chip_idx)
        half_in = _sum_chips("sum_chips_in_b", part_b, recv_in[0], chip_idx, row0=n_first * tm_o2, prev=half_in)
    else:
        half_in = _sum_chips("sum_chips_in", part_b, recv_in[0], chip_idx)
    halves = [half_in] + halves_early
    others = _run_exchange("pair_share", _to_sibling([half_in], other_half=False)) + list(others_early)

    g_cw_full = jnp.transpose(g_cw, (1, 0, 2)).reshape(CONV_K, w)
    small_shapes = [(LANES,), (1, d), (1, d), (1, w), (1, w), (CONV_K, w), (2, d)]
    pack = _flat_pack([loss_part[0], g_nw, g_fnw, g_ps, g_cb, g_cw_full, g_gb])
    total = _sum_devices(_gather_packs(pack))
    t_loss, t_nw, t_fnw, t_ps, t_cb, t_cw, t_gb = _unpack(total, small_shapes)
    loss = t_loss[0]
    t_cw = lax.dynamic_slice_in_dim(t_cw, chip * cg, cg, axis=1)
    t_gb = lax.dynamic_slice_in_dim(t_gb, chip * dq, dq, axis=1)

    out = {}
    big_names = ["w_in", "w_out", "w_branch", "pool_w"]
    big_m = [m_w_in, m_w_out, m_w_branch, m_pool_w]
    big_v = [v_w_in, v_w_out, v_w_branch, v_pool_w]
    big_orig = [w_in, w_out, w_branch, pool_w]
    for nm, g_own, g_other, w2, mm_, vv_, orig in zip(big_names, halves, others, big_w, big_m, big_v, big_orig):
        sh = (2,) + g_own.shape
        res = _adamw_halves("adamw_" + nm, g_own, g_other, c_idx,
                            w2.reshape(sh), mm_.reshape(sh), vv_.reshape(sh))
        out[nm] = [r.reshape(orig.shape) for r in res]

    sm_names = ["norm_w", "final_norm_w", "pool_scale", "conv_b", "conv_w", "gate_b"]
    sm_g = [t_nw, t_fnw, t_ps, t_cb, t_cw, t_gb]
    sm_w = [norm_w, final_norm_w, pool_scale, conv_b, conv_w, gate_b]
    sm_m = [m_norm_w, m_final_norm_w, m_pool_scale, m_conv_b, m_conv_w, m_gate_b]
    sm_v = [v_norm_w, v_final_norm_w, v_pool_scale, v_conv_b, v_conv_w, v_gate_b]
    sm_shapes = [a.shape for a in sm_w]
    res = _adamw("adamw_small", _flat_pack(sm_g), _flat_pack(sm_w), _flat_pack(sm_m), _flat_pack(sm_v))
    res = [_unpack(r, sm_shapes) for r in res]
    for idx, nm in enumerate(sm_names):
        out[nm] = [r[idx] for r in res]

    order = ["norm_w", "w_in", "pool_w", "pool_scale", "conv_w", "conv_b", "gate_b", "w_branch", "w_out",
             "final_norm_w"]
    outs = [loss, grad_x.reshape(x.shape)]
    for kind in range(4):
        outs += [out[nm][kind] for nm in order]
    return tuple(outs)
```

```python
import math

import jax
import jax.numpy as jnp
from jax import lax
from jax.experimental import pallas as pl
from jax.experimental.pallas import tpu as pltpu

F32 = jnp.float32
BF16 = jnp.bfloat16

NORM_EPS = 1e-6
POOL_WINDOWS = (2, 4, 8, 16)
N_POOL_GROUPS = len(POOL_WINDOWS)
CONV_K = 3
ADAM_LR = 0.001
ADAM_B1 = 0.9
ADAM_B2 = 0.999
ADAM_EPS = 1e-08
ADAM_WD = 0.01
ADAM_STEP = 10

N_CHIPS = 4
N_DEV = 8
HALO = 16
LANES = 128
V7X_VMEM_BYTES = 64 * 1024 * 1024
VMEM_CAP = V7X_VMEM_BYTES - 8 * 1024 * 1024

MESH = pl.DeviceIdType.MESH
ANY = pl.BlockSpec(memory_space=pl.ANY)

NN = (((1,), (0,)), ((), ()))
NT = (((1,), (1,)), ((), ()))
TN = (((0,), (0,)), ((), ()))


def _pick(dim, pref, align):
    if dim <= pref:
        return dim
    t = (pref // align) * align
    while t >= align:
        if dim % t == 0:
            return t
        t -= align
    raise ValueError(f"no tile for {dim} (pref {pref}, align {align})")


def _nbytes(shape, dtype):
    n = 1
    for s in shape:
        if s is not None:
            n *= s
    return n * jnp.dtype(dtype).itemsize


def _params(semantics, block_bytes, extra_bytes=0):
    need = 2 * block_bytes + extra_bytes + (2 << 20)
    return pltpu.CompilerParams(dimension_semantics=semantics,
                                vmem_limit_bytes=int(min(max(need, 16 << 20), VMEM_CAP)))


def _sigmoid(z):
    return jax.nn.sigmoid(z)


def _mm(name, *, grid, operands, in_specs, out_shape, out_specs, pairs, dims, acc_shapes, epilogue,
        semantics=None, temp_bytes=0, comm=None, prefetch=None, aliases=None):
    aliases = dict(aliases or {})
    n_in, n_out = len(operands), len(out_shape)
    kax = len(grid) - 1
    nk = grid[kax]
    single = nk == 1
    conditional = any(p[3] is not None for p in pairs)
    if single and conditional:
        assert len(acc_shapes) == 1 and all(p[3] is not None for p in pairs)
    n_acc = 0 if single else len(acc_shapes)
    n_pf = 0 if prefetch is None else 1
    c_ops = list(comm.operands) if comm else []
    c_out = list(comm.out_shape) if comm else []
    c_sems = list(comm.scratch) if comm else []
    c_alias = dict(comm.aliases) if comm else {}
    n_ci, n_co = len(c_ops), len(c_out)

    def product(ins, ai, bi):
        return lax.dot_general(ins[ai][...], ins[bi][...], dims, preferred_element_type=F32)

    def body(*refs):
        refs = refs[n_pf:]
        ins = refs[:n_in]
        c_in_refs = refs[n_in:n_in + n_ci]
        outs = refs[n_in + n_ci:n_in + n_ci + n_out]
        c_out_refs = refs[n_in + n_ci + n_out:n_in + n_ci + n_out + n_co]
        scratch = refs[n_in + n_ci + n_out + n_co:]
        accs = scratch[:n_acc]
        sems = scratch[n_acc:]
        ids = [pl.program_id(a) for a in range(len(grid))]
        k = ids[kax]

        if comm:
            is_first = ids[0] == 0
            is_last = ids[0] == grid[0] - 1
            for a in range(1, len(grid)):
                is_first = jnp.logical_and(is_first, ids[a] == 0)
                is_last = jnp.logical_and(is_last, ids[a] == grid[a] - 1)

            @pl.when(is_first)
            def _():
                comm.start(c_in_refs, c_out_refs, sems)

        if single and conditional:
            for (ai, bi, ci, cond) in pairs:
                def only(ai=ai, bi=bi):
                    epilogue([product(ins, ai, bi)], ins, outs, ids)
                pl.when(cond(ids))(only)
        elif single:
            vals = [None] * len(acc_shapes)
            for (ai, bi, ci, cond) in pairs:
                r = product(ins, ai, bi)
                vals[ci] = r if vals[ci] is None else vals[ci] + r
            epilogue(vals, ins, outs, ids)
        else:
            @pl.when(k == 0)
            def _():
                for a in accs:
                    a[...] = jnp.zeros(a.shape, a.dtype)

            for (ai, bi, ci, cond) in pairs:
                def step(ai=ai, bi=bi, ci=ci):
                    accs[ci][...] += product(ins, ai, bi)
                if cond is None:
                    step()
                else:
                    pl.when(cond(ids))(step)

            @pl.when(k == nk - 1)
            def _():
                epilogue([a[...] for a in accs], ins, outs, ids)

        if comm:
            @pl.when(is_last)
            def _():
                comm.finish(c_in_refs, c_out_refs, sems)

    if semantics is None:
        semantics = ("parallel",) * kax + ("arbitrary",)
    if comm:
        semantics = ("arbitrary",) * len(grid)
    in_specs = [ANY if idx in aliases else spec for idx, spec in enumerate(in_specs)]
    blk = 0
    for idx, (spec, op) in enumerate(zip(in_specs, operands)):
        if idx not in aliases:
            blk += _nbytes(spec.block_shape, op.dtype)
    for spec, o in zip(out_specs, out_shape):
        blk += _nbytes(spec.block_shape, o.dtype)
    acc_bytes = sum(_nbytes(s, F32) for s in acc_shapes)
    io_alias = {n_pf + i: o for i, o in aliases.items()}
    io_alias.update({n_pf + n_in + i: n_out + o for i, o in c_alias.items()})
    all_in = list(in_specs) + [ANY] * n_ci
    all_out = list(out_specs) + [ANY] * n_co
    scratch_shapes = [pltpu.VMEM(s, F32) for s in acc_shapes[:n_acc]] + c_sems
    params = _params(semantics, blk, 3 * acc_bytes + temp_bytes)
    shapes = list(out_shape) + c_out
    if prefetch is None:
        call = pl.pallas_call(
            body, name=name, grid=grid, in_specs=all_in, out_specs=all_out, out_shape=shapes,
            scratch_shapes=scratch_shapes, input_output_aliases=io_alias, compiler_params=params)
        res = call(*operands, *c_ops)
    else:
        call = pl.pallas_call(
            body, name=name, out_shape=shapes, input_output_aliases=io_alias, compiler_params=params,
            grid_spec=pltpu.PrefetchScalarGridSpec(
                num_scalar_prefetch=1, grid=grid, in_specs=all_in, out_specs=all_out,
                scratch_shapes=scratch_shapes))
        res = call(prefetch, *operands, *c_ops)
    return res[:n_out], res[n_out:]


def _cast_bf16(name, w2d):
    r, c = w2d.shape
    tr = _pick(r, max(16, (4 << 20) // (4 * c)), 16)

    def body(w_ref, o_ref):
        o_ref[...] = w_ref[...].astype(BF16)

    return pl.pallas_call(
        body, name=name, grid=(r // tr,),
        in_specs=[pl.BlockSpec((tr, c), lambda i: (i, 0))],
        out_specs=pl.BlockSpec((tr, c), lambda i: (i, 0)),
        out_shape=jax.ShapeDtypeStruct((r, c), BF16),
        compiler_params=_params(("parallel",), tr * c * 6),
    )(w2d)


def _rms_fwd(x, norm_w):
    s, d = x.shape
    ts = _pick(s, 256, 16)

    def body(x_ref, w_ref, h_ref):
        xv = x_ref[...]
        rstd = lax.rsqrt(jnp.mean(xv * xv, axis=-1, keepdims=True) + NORM_EPS)
        h_ref[...] = (xv * rstd * w_ref[...]).astype(BF16)

    return pl.pallas_call(
        body, name="rms_fwd", grid=(s // ts,),
        in_specs=[pl.BlockSpec((ts, d), lambda i: (i, 0)), pl.BlockSpec((1, d), lambda i: (0, 0))],
        out_specs=pl.BlockSpec((ts, d), lambda i: (i, 0)),
        out_shape=jax.ShapeDtypeStruct((s, d), BF16),
        compiler_params=_params(("parallel",), ts * d * 6, 3 * ts * d * 4),
    )(x, norm_w)


def _head(x, o, target, fnw):
    s, d = x.shape
    ts = _pick(s, 128, 16)

    def body(x_ref, o_ref, t_ref, w_ref, dx_ref, dxb_ref, loss_ref, gw_ref):
        i = pl.program_id(0)

        @pl.when(i == 0)
        def _():
            loss_ref[...] = jnp.zeros(loss_ref.shape, F32)
            gw_ref[...] = jnp.zeros(gw_ref.shape, F32)

        w = w_ref[...]
        x2 = x_ref[...] + o_ref[...]
        rstd = lax.rsqrt(jnp.mean(x2 * x2, axis=-1, keepdims=True) + NORM_EPS)
        n = x2 * rstd
        e = n * w - t_ref[...]
        loss_ref[...] += 0.5 * jnp.sum(e * e) / d
        dy = e / d
        gw_ref[...] += jnp.sum(dy * n, axis=0, keepdims=True)
        gy = dy * w
        dx = rstd * (gy - n * jnp.mean(gy * n, axis=-1, keepdims=True))
        dx_ref[...] = dx
        dxb_ref[...] = dx.astype(BF16)

    row = pl.BlockSpec((ts, d), lambda i: (i, 0))
    return pl.pallas_call(
        body, name="head", grid=(s // ts,),
        in_specs=[row, row, row, pl.BlockSpec((1, d), lambda i: (0, 0))],
        out_specs=[row, row, pl.BlockSpec((8, LANES), lambda i: (0, 0)),
                   pl.BlockSpec((1, d), lambda i: (0, 0))],
        out_shape=[jax.ShapeDtypeStruct((s, d), F32), jax.ShapeDtypeStruct((s, d), BF16),
                   jax.ShapeDtypeStruct((8, LANES), F32), jax.ShapeDtypeStruct((1, d), F32)],
        compiler_params=_params(("arbitrary",), ts * d * 22, 6 * ts * d * 4),
    )(x, o, target, fnw)


def _rms_bwd(x, dh, dx2, norm_w):
    s, d = x.shape
    ts = _pick(s, 128, 16)

    def body(x_ref, dh_ref, dx2_ref, w_ref, gx_ref, gw_ref):
        i = pl.program_id(0)

        @pl.when(i == 0)
        def _():
            gw_ref[...] = jnp.zeros(gw_ref.shape, F32)

        xv = x_ref[...]
        rstd = lax.rsqrt(jnp.mean(xv * xv, axis=-1, keepdims=True) + NORM_EPS)
        n = xv * rstd
        dhv = dh_ref[...]
        gw_ref[...] += jnp.sum(dhv * n, axis=0, keepdims=True)
        gh = dhv * w_ref[...]
        gx_ref[...] = dx2_ref[...] + rstd * (gh - n * jnp.mean(gh * n, axis=-1, keepdims=True))

    row = pl.BlockSpec((ts, d), lambda i: (i, 0))
    vec = pl.BlockSpec((1, d), lambda i: (0, 0))
    return pl.pallas_call(
        body, name="rms_bwd", grid=(s // ts,),
        in_specs=[row, row, row, vec], out_specs=[row, vec],
        out_shape=[jax.ShapeDtypeStruct((s, d), F32), jax.ShapeDtypeStruct((1, d), F32)],
        compiler_params=_params(("arbitrary",), ts * d * 16, 5 * ts * d * 4),
    )(x, dh, dx2, norm_w)


def _silu(z):
    return z * _sigmoid(z)


def _window_sum(ext, window, back):
    n = ext.shape[0]
    acc = ext
    step = 1
    while step < window:
        acc = acc + pltpu.roll(acc, step if back else n - step, 0)
        step *= 2
    return acc


def _shift_rows(ext, k, back):
    n = ext.shape[0]
    return pltpu.roll(ext, k if back else n - k, 0)


def _mix_fwd(proj, pool_wg, pool_scale, conv_wg, conv_b, width):
    s = proj.shape[0]
    w = width
    cg = w // N_POOL_GROUPS
    ts = _pick(s, 128, HALO)
    hb = ts // HALO
    cols = 6 * w

    def body(p_ref, ph_ref, pw_ref, ps_ref, cw_ref, cb_ref, ys_ref):
        i = pl.program_id(0)
        first = i == 0
        t1 = (i * ts + lax.broadcasted_iota(jnp.int32, (ts, 1), 0) + 1).astype(F32)

        def tile(part, g):
            lo = part * w + g * cg
            return p_ref[:, lo:lo + cg].astype(F32)

        def prev(part, g):
            lo = part * w + g * cg
            return jnp.where(first, 0.0, ph_ref[:, lo:lo + cg].astype(F32))

        for g, win in enumerate(POOL_WINDOWS):
            gs = slice(g * cg, (g + 1) * cg)
            u = tile(0, g)
            ext = jnp.concatenate([prev(0, g), u], axis=0)
            wsum = _window_sum(ext, win, True)[HALO:]
            pooled = wsum / jnp.minimum(t1, float(win)) - u
            pw = pw_ref[:, g].reshape(cg, cg)
            mixed = jnp.dot(pooled.astype(BF16), pw, preferred_element_type=F32)
            ys_ref[0, :, gs] = (mixed * ps_ref[:, gs] * _silu(tile(1, g))).astype(BF16)
            v = tile(4, g) * tile(2, g)
            vext = jnp.concatenate([prev(4, g) * prev(2, g), v], axis=0)
            v1 = _shift_rows(vext, 1, True)[HALO:]
            v2 = _shift_rows(vext, 2, True)[HALO:]
            cw = [cw_ref[g, tap:tap + 1, :] for tap in range(CONV_K)]
            y = cb_ref[:, gs] + cw[0] * v2 + cw[1] * v1 + cw[2] * v
            ys_ref[1, :, gs] = (tile(3, g) * y * _silu(tile(5, g))).astype(BF16)

    return pl.pallas_call(
        body, name="mix_fwd", grid=(s // ts,),
        in_specs=[pl.BlockSpec((ts, cols), lambda i: (i, 0)),
                  pl.BlockSpec((HALO, cols), lambda i: (jnp.maximum(i * hb - 1, 0), 0)),
                  pl.BlockSpec(pool_wg.shape, lambda i: (0, 0, 0, 0)),
                  pl.BlockSpec((1, w), lambda i: (0, 0)),
                  pl.BlockSpec(conv_wg.shape, lambda i: (0, 0, 0)),
                  pl.BlockSpec((1, w), lambda i: (0, 0))],
        out_specs=pl.BlockSpec((2, ts, w), lambda i: (0, i, 0)),
        out_shape=jax.ShapeDtypeStruct((2, s, w), BF16),
        compiler_params=_params(("parallel",), (ts + HALO) * cols * 2 + 2 * ts * w * 2
                                + _nbytes(pool_wg.shape, BF16), 24 * (ts + HALO) * cg * 4),
    )(proj, proj, pool_wg, pool_scale, conv_wg, conv_b)


def _mix_bwd(proj, dys, pool_wg, pool_scale, conv_wg, conv_b, width):
    s = proj.shape[0]
    w = width
    cg = w // N_POOL_GROUPS
    ts = _pick(s, 128, HALO)
    hb = ts // HALO
    n_tiles = s // ts
    last_hb = s // HALO - 1
    cols = 6 * w

    def body(p_ref, ph_ref, pn_ref, dy_ref, dyn_ref, pw_ref, ps_ref, cw_ref, cb_ref,
             dp_ref, dpw_ref, dps_ref, dcw_ref, dcb_ref):
        i = pl.program_id(0)
        first = i == 0
        last = i == n_tiles - 1

        @pl.when(first)
        def _():
            dpw_ref[...] = jnp.zeros(dpw_ref.shape, F32)
            dps_ref[...] = jnp.zeros(dps_ref.shape, F32)
            dcw_ref[...] = jnp.zeros(dcw_ref.shape, F32)
            dcb_ref[...] = jnp.zeros(dcb_ref.shape, F32)

        row = i * ts + lax.broadcasted_iota(jnp.int32, (ts + HALO, 1), 0)
        t1_ext = (row + 1).astype(F32)
        t1 = t1_ext[:ts]

        def tile(part, g):
            lo = part * w + g * cg
            return p_ref[:, lo:lo + cg].astype(F32)

        def prev(part, g):
            lo = part * w + g * cg
            return jnp.where(first, 0.0, ph_ref[:, lo:lo + cg].astype(F32))

        def ahead(part, g):
            lo = part * w + g * cg
            return jnp.concatenate([tile(part, g), pn_ref[:, lo:lo + cg].astype(F32)], axis=0)

        def dy_ahead(n, g):
            gs = slice(g * cg, (g + 1) * cg)
            nxt = jnp.where(last, 0.0, dyn_ref[n, :, gs])
            return jnp.concatenate([dy_ref[n, :, gs], nxt], axis=0)

        for g, win in enumerate(POOL_WINDOWS):
            gs = slice(g * cg, (g + 1) * cg)
            u = tile(0, g)
            ext = jnp.concatenate([prev(0, g), u], axis=0)
            pooled = _window_sum(ext, win, True)[HALO:] / jnp.minimum(t1, float(win)) - u
            pooled_b = pooled.astype(BF16)
            pw = pw_ref[:, g].reshape(cg, cg)
            mixed = jnp.dot(pooled_b, pw, preferred_element_type=F32)
            zp_ext = ahead(1, g)
            dy0_ext = dy_ahead(0, g)
            scale = ps_ref[:, gs]
            sp_ext = _sigmoid(zp_ext)
            dms_ext = dy0_ext * (zp_ext * sp_ext)
            dmix_b = (dms_ext * scale).astype(BF16)
            dpooled_ext = lax.dot_general(dmix_b, pw, NT, preferred_element_type=F32)
            dy0 = dy0_ext[:ts]
            zp, sp = zp_ext[:ts], sp_ext[:ts]
            dsilu_p = sp * (1.0 + zp * (1.0 - sp))
            dp_ref[:, w + g * cg:w + (g + 1) * cg] = (dy0 * mixed * scale * dsilu_p).astype(BF16)
            dps_ref[:, gs] += jnp.sum(dms_ext[:ts] * mixed, axis=0, keepdims=True)
            dpw = lax.dot_general(pooled_b, dmix_b[:ts], TN, preferred_element_type=F32)
            dpw_ref[g // 2, :, g % 2] += dpw.reshape(N_CHIPS, cg // N_CHIPS, cg)
            q_ext = dpooled_ext / jnp.minimum(t1_ext, float(win))
            du = _window_sum(q_ext, win, False)[:ts] - dpooled_ext[:ts]
            dp_ref[:, gs] = du.astype(BF16)
            uc = tile(2, g)
            cc = tile(4, g)
            v = cc * uc
            vext = jnp.concatenate([prev(4, g) * prev(2, g), v], axis=0)
            v1 = _shift_rows(vext, 1, True)[HALO:]
            v2 = _shift_rows(vext, 2, True)[HALO:]
            cw = [cw_ref[g, tap:tap + 1, :] for tap in range(CONV_K)]
            y = cb_ref[:, gs] + cw[0] * v2 + cw[1] * v1 + cw[2] * v
            bc_ext = ahead(3, g)
            zc_ext = ahead(5, g)
            dy1_ext = dy_ahead(1, g)
            sc_ext = _sigmoid(zc_ext)
            silu_c_ext = zc_ext * sc_ext
            dyy_ext = dy1_ext * bc_ext * silu_c_ext
            dy1 = dy1_ext[:ts]
            bc = bc_ext[:ts]
            zc, sc = zc_ext[:ts], sc_ext[:ts]
            dsilu_c = sc * (1.0 + zc * (1.0 - sc))
            dp_ref[:, 3 * w + g * cg:3 * w + (g + 1) * cg] = (dy1 * y * silu_c_ext[:ts]).astype(BF16)
            dp_ref[:, 5 * w + g * cg:5 * w + (g + 1) * cg] = (dy1 * bc * y * dsilu_c).astype(BF16)
            dyy = dyy_ext[:ts]
            dcb_ref[:, gs] += jnp.sum(dyy, axis=0, keepdims=True)
            for tap, vt in enumerate((v2, v1, v)):
                dcw_ref[g, tap:tap + 1, :] += jnp.sum(dyy * vt, axis=0, keepdims=True)
            dv = (cw[2] * dyy + cw[1] * _shift_rows(dyy_ext, 1, False)[:ts]
                  + cw[0] * _shift_rows(dyy_ext, 2, False)[:ts])
            dp_ref[:, 4 * w + g * cg:4 * w + (g + 1) * cg] = (dv * uc).astype(BF16)
            dp_ref[:, 2 * w + g * cg:2 * w + (g + 1) * cg] = (dv * cc).astype(BF16)

    dpw_shape = (2, N_CHIPS, 2, cg // N_CHIPS, cg)
    return pl.pallas_call(
        body, name="mix_bwd", grid=(n_tiles,),
        in_specs=[pl.BlockSpec((ts, cols), lambda i: (i, 0)),
                  pl.BlockSpec((HALO, cols), lambda i: (jnp.maximum(i * hb - 1, 0), 0)),
                  pl.BlockSpec((HALO, cols), lambda i: (jnp.minimum((i + 1) * hb, last_hb), 0)),
                  pl.BlockSpec((2, ts, w), lambda i: (0, i, 0)),
                  pl.BlockSpec((2, HALO, w), lambda i: (0, jnp.minimum((i + 1) * hb, last_hb), 0)),
                  pl.BlockSpec(pool_wg.shape, lambda i: (0, 0, 0, 0)),
                  pl.BlockSpec((1, w), lambda i: (0, 0)),
                  pl.BlockSpec(conv_wg.shape, lambda i: (0, 0, 0)),
                  pl.BlockSpec((1, w), lambda i: (0, 0))],
        out_specs=[pl.BlockSpec((ts, cols), lambda i: (i, 0)),
                   pl.BlockSpec(dpw_shape, lambda i: (0, 0, 0, 0, 0)),
                   pl.BlockSpec((1, w), lambda i: (0, 0)),
                   pl.BlockSpec(conv_wg.shape, lambda i: (0, 0, 0)),
                   pl.BlockSpec((1, w), lambda i: (0, 0))],
        out_shape=[jax.ShapeDtypeStruct((s, cols), BF16), jax.ShapeDtypeStruct(dpw_shape, F32),
                   jax.ShapeDtypeStruct((1, w), F32), jax.ShapeDtypeStruct(conv_wg.shape, F32),
                   jax.ShapeDtypeStruct((1, w), F32)],
        compiler_params=_params(("arbitrary",), (2 * ts + 2 * HALO) * cols * 2 + (ts + HALO) * w * 8
                                + _nbytes(pool_wg.shape, BF16) + _nbytes(dpw_shape, F32),
                                40 * (ts + HALO) * cg * 4),
    )(proj, proj, proj, dys, dys, pool_wg, pool_scale, conv_wg, conv_b)


def _adamw(name, g, w, m, v):
    r, c = w.shape
    tr = _pick(r, max(8, (1 << 20) // (4 * c)), 8)

    def body(g_ref, w_ref, m_ref, v_ref, go_ref, d_ref, mo_ref, vo_ref):
        gv = g_ref[...]
        mn = ADAM_B1 * m_ref[...] + (1.0 - ADAM_B1) * gv
        vn = ADAM_B2 * v_ref[...] + (1.0 - ADAM_B2) * (gv * gv)
        m_hat = mn / (1.0 - ADAM_B1 ** ADAM_STEP)
        v_hat = vn / (1.0 - ADAM_B2 ** ADAM_STEP)
        go_ref[...] = gv
        d_ref[...] = -ADAM_LR * (m_hat / (jnp.sqrt(v_hat) + ADAM_EPS) + ADAM_WD * w_ref[...])
        mo_ref[...] = mn
        vo_ref[...] = vn

    blk = pl.BlockSpec((tr, c), lambda i: (i, 0))
    sh = jax.ShapeDtypeStruct((r, c), F32)
    return pl.pallas_call(
        body, name=name, grid=(r // tr,), in_specs=[blk] * 4, out_specs=[blk] * 4,
        out_shape=[sh] * 4, compiler_params=_params(("parallel",), tr * c * 32, 4 * tr * c * 4),
    )(g, w, m, v)


def _pair_add(name, g, r1, c_idx):
    _, r, c = g.shape
    tr = _pick(r, max(16, (2 << 20) // (2 * c)), 16)

    def body(c_ref, g_ref, r_ref, o_ref):
        o_ref[...] = (g_ref[...].astype(F32) + r_ref[...].astype(F32)).astype(BF16)

    return pl.pallas_call(
        body, name=name,
        grid_spec=pltpu.PrefetchScalarGridSpec(
            num_scalar_prefetch=1, grid=(r // tr,),
            in_specs=[pl.BlockSpec((None, tr, c), lambda i, cr: (cr[0], i, 0)),
                      pl.BlockSpec((tr, c), lambda i, cr: (i, 0))],
            out_specs=pl.BlockSpec((tr, c), lambda i, cr: (i, 0))),
        out_shape=jax.ShapeDtypeStruct((r, c), BF16),
        compiler_params=_params(("parallel",), tr * c * 6, 3 * tr * c * 4),
    )(c_idx, g, r1)


def _adamw_halves(name, g_own, g_other, c_idx, w, m, v):
    _, r, c = w.shape
    tr = _pick(r, max(8, (5 << 18) // (4 * c)), 8)

    def body(c_ref, go_ref, gt_ref, w_ref, m_ref, v_ref, g_out, d_ref, mo_ref, vo_ref):
        gv = jnp.where(pl.program_id(0) == c_ref[0], go_ref[...], gt_ref[...])
        mn = ADAM_B1 * m_ref[...] + (1.0 - ADAM_B1) * gv
        vn = ADAM_B2 * v_ref[...] + (1.0 - ADAM_B2) * (gv * gv)
        m_hat = mn / (1.0 - ADAM_B1 ** ADAM_STEP)
        v_hat = vn / (1.0 - ADAM_B2 ** ADAM_STEP)
        g_out[...] = gv
        d_ref[...] = -ADAM_LR * (m_hat / (jnp.sqrt(v_hat) + ADAM_EPS) + ADAM_WD * w_ref[...])
        mo_ref[...] = mn
        vo_ref[...] = vn

    blk = pl.BlockSpec((None, tr, c), lambda h, i, cr: (h, i, 0))
    sh = jax.ShapeDtypeStruct(w.shape, F32)
    return pl.pallas_call(
        body, name=name,
        grid_spec=pltpu.PrefetchScalarGridSpec(
            num_scalar_prefetch=1, grid=(2, r // tr),
            in_specs=[pl.BlockSpec((tr, c), lambda h, i, cr: (jnp.where(h == cr[0], i, 0), 0)),
                      pl.BlockSpec((tr, c), lambda h, i, cr: (jnp.where(h == cr[0], 0, i), 0)),
                      blk, blk, blk],
            out_specs=[blk] * 4),
        out_shape=[sh] * 4,
        compiler_params=_params(("arbitrary", "arbitrary"), tr * c * 36, 4 * tr * c * 4),
    )(c_idx, g_own, g_other, w, m, v)


def _sum_chips(name, part, r2, chip_idx, row0=0, prev=None):
    n, r, c = r2.shape
    rc = part.shape[1]
    tr = _pick(math.gcd(rc, row0) if row0 else rc, max(16, (1 << 20) // (2 * c)), 16)
    b0 = row0 // tr

    def body(ch_ref, own_ref, *rest):
        slots, o_ref = rest[:n], rest[-1]
        acc = None
        for s in range(n):
            term = jnp.where(ch_ref[0] == s, own_ref[...], slots[s][...]).astype(F32)
            acc = term if acc is None else acc + term
        o_ref[...] = acc

    def slot_spec(s):
        return pl.BlockSpec((None, tr, c), lambda i, ch: (jnp.where(ch[0] == s, (s + 1) % n, s), b0 + i, 0))

    extra = [] if prev is None else [prev]
    return pl.pallas_call(
        body, name=name,
        grid_spec=pltpu.PrefetchScalarGridSpec(
            num_scalar_prefetch=1, grid=(rc // tr,),
            in_specs=[pl.BlockSpec((None, tr, c), lambda i, ch: (ch[0], i, 0))]
            + [slot_spec(s) for s in range(n)] + [ANY] * len(extra),
            out_specs=pl.BlockSpec((tr, c), lambda i, ch: (b0 + i, 0))),
        out_shape=jax.ShapeDtypeStruct((r, c), F32),
        input_output_aliases={2 + n: 0} if extra else {},
        compiler_params=_params(("parallel",), tr * c * (2 * n + 6), 3 * tr * c * 4),
    )(chip_idx, part, *([r2] * n), *extra)


def _sum_devices(packs):
    n, r, c = packs.shape

    def body(p_ref, o_ref):
        acc = p_ref[0]
        for k in range(1, n):
            acc = acc + p_ref[k]
        o_ref[...] = acc

    return pl.pallas_call(
        body, name="sum_devices", out_shape=jax.ShapeDtypeStruct((r, c), F32),
        in_specs=[pl.BlockSpec(memory_space=pltpu.VMEM)],
        out_specs=pl.BlockSpec(memory_space=pltpu.VMEM),
    )(packs)


def _place():
    x, y, c = lax.axis_index("x"), lax.axis_index("y"), lax.axis_index("c")
    return x, y, c


def _chip_peers(x, y):
    out = []
    for k, (fx, fy) in enumerate(((0, 1), (1, 0), (1, 1))):
        px = 1 - x if fx else x
        py = 1 - y if fy else y
        out.append((k, px, py, 2 * px + py))
    return out


def _gather_weights(big, small, relay=False):
    nb, ns = len(big), len(small)

    class Copies:
        def __init__(self, c_in, c_out, sems):
            b_in, s_in = c_in[:nb], c_in[nb:]
            b_out, s_out = c_out[:nb], c_out[nb:]
            ici_s, ici_r, d2d_s, d2d_r, own_s, own_r = sems[:6]
            x, y, c = _place()
            chip = 2 * x + y
            sibling = (x, y, 1 - c)
            peers = _chip_peers(x, y)
            self.own = [pltpu.make_async_remote_copy(
                src_ref=b_in[t], dst_ref=b_out[t].at[chip], send_sem=own_s.at[t], recv_sem=own_r.at[t],
                device_id=sibling, device_id_type=MESH) for t in range(nb)]
            self.ici = [pltpu.make_async_remote_copy(
                src_ref=b_in[t].at[c], dst_ref=b_out[t].at[chip, c],
                send_sem=ici_s.at[t, k], recv_sem=ici_r.at[t, k],
                device_id=(px, py, c), device_id_type=MESH)
                for t in range(nb) for (k, px, py, pchip) in peers if not (relay and k == 2)]
            self.relay = []
            if relay:
                south = c == 0
                from_chip = jnp.where(south, 2 * x + (1 - y), 2 * (1 - x) + y)
                to = (jnp.where(south, 1 - x, x), jnp.where(south, y, 1 - y), c)
                self.relay = [pltpu.make_async_remote_copy(
                    src_ref=b_out[t].at[from_chip, c], dst_ref=b_out[t].at[from_chip, c],
                    send_sem=ici_s.at[t, 2], recv_sem=ici_r.at[t, 2],
                    device_id=to, device_id_type=MESH) for t in range(nb)]
            self.landed = [pltpu.make_async_remote_copy(
                src_ref=b_out[t].at[pchip, c], dst_ref=b_out[t].at[pchip, c],
                send_sem=ici_s.at[t, k], recv_sem=ici_r.at[t, k],
                device_id=sibling, device_id_type=MESH)
                for t in range(nb) for (k, px, py, pchip) in peers]
            self.passed = [pltpu.make_async_remote_copy(
                src_ref=b_out[t].at[pchip, c], dst_ref=b_out[t].at[pchip, c],
                send_sem=d2d_s.at[t, k], recv_sem=d2d_r.at[t, k],
                device_id=sibling, device_id_type=MESH)
                for t in range(nb) for (k, px, py, pchip) in peers]
            self.from_sibling = [pltpu.make_async_remote_copy(
                src_ref=b_out[t].at[pchip, 1 - c], dst_ref=b_out[t].at[pchip, 1 - c],
                send_sem=d2d_s.at[t, k], recv_sem=d2d_r.at[t, k],
                device_id=sibling, device_id_type=MESH)
                for t in range(nb) for (k, px, py, pchip) in peers]
            self.small, self.small_landed, self.local = [], [], []
            if ns:
                sm_s, sm_r, loc = sems[6:]
                self.local = [pltpu.make_async_copy(s_in[t], s_out[t].at[chip], loc.at[t]) for t in range(ns)]
                self.small = [pltpu.make_async_remote_copy(
                    src_ref=s_in[t], dst_ref=s_out[t].at[chip],
                    send_sem=sm_s.at[t, k], recv_sem=sm_r.at[t, k],
                    device_id=(px, py, c), device_id_type=MESH)
                    for t in range(ns) for (k, px, py, pchip) in peers]
                self.small_landed = [pltpu.make_async_remote_copy(
                    src_ref=s_in[t], dst_ref=s_out[t].at[pchip],
                    send_sem=sm_s.at[t, k], recv_sem=sm_r.at[t, k],
                    device_id=sibling, device_id_type=MESH)
                    for t in range(ns) for (k, px, py, pchip) in peers]

    def start(c_in, c_out, sems):
        cps = Copies(c_in, c_out, sems)
        for cp in cps.local + cps.own + cps.ici + cps.small:
            cp.start()

    def finish(c_in, c_out, sems):
        cps = Copies(c_in, c_out, sems)
        for t in range(nb):
            for k in range(3):
                if relay and k == 2:
                    cps.relay[t].start()
                cps.landed[3 * t + k].wait_recv()
                cps.passed[3 * t + k].start()
        for cp in cps.small_landed + cps.from_sibling:
            cp.wait_recv()
        for cp in cps.ici + cps.relay + cps.small + cps.passed:
            cp.wait_send()
        for cp in cps.own + cps.local:
            cp.wait()

    out_shape = [jax.ShapeDtypeStruct((N_CHIPS,) + b.shape, b.dtype) for b in big]
    out_shape += [jax.ShapeDtypeStruct((N_CHIPS,) + s.shape, s.dtype) for s in small]
    dma = pltpu.SemaphoreType.DMA
    scratch = [dma((nb, 3)), dma((nb, 3)), dma((nb, 3)), dma((nb, 3)), dma((nb,)), dma((nb,))]
    if ns:
        scratch += [dma((ns, 3)), dma((ns, 3)), dma((ns,))]
    ex = _Exchange(list(big) + list(small), out_shape, scratch, start, finish)
    ex.copies = Copies
    return ex


class _Exchange:
    def __init__(self, operands, out_shape, scratch, start, finish, aliases=None):
        self.operands, self.out_shape, self.scratch = operands, out_shape, scratch
        self.start, self.finish, self.aliases = start, finish, dict(aliases or {})


def _run_exchange(name, ex):
    n_i, n_o = len(ex.operands), len(ex.out_shape)

    def body(*refs):
        ins, outs, sems = refs[:n_i], refs[n_i:n_i + n_o], refs[n_i + n_o:]
        ex.start(ins, outs, sems)
        ex.finish(ins, outs, sems)

    return list(pl.pallas_call(
        body, name=name, out_shape=list(ex.out_shape), in_specs=[ANY] * n_i, out_specs=[ANY] * n_o,
        scratch_shapes=list(ex.scratch), input_output_aliases=ex.aliases,
    )(*ex.operands))


def _to_sibling(arrays, other_half):
    n = len(arrays)

    def copies(a_in, a_out, sems):
        send, recv = sems
        x, y, c = _place()
        return [pltpu.make_async_remote_copy(
            src_ref=a_in[t].at[1 - c] if other_half else a_in[t], dst_ref=a_out[t],
            send_sem=send.at[t], recv_sem=recv.at[t],
            device_id=(x, y, 1 - c), device_id_type=MESH) for t in range(n)]

    def start(a_in, a_out, sems):
        for cp in copies(a_in, a_out, sems):
            cp.start()

    def finish(a_in, a_out, sems):
        for cp in copies(a_in, a_out, sems):
            cp.wait()

    dma = pltpu.SemaphoreType.DMA
    shapes = [jax.ShapeDtypeStruct(a.shape[1:] if other_half else a.shape, a.dtype) for a in arrays]
    return _Exchange(list(arrays), shapes, [dma((n,)), dma((n,))], start, finish)


def _scatter_partials(parts, rows=None, row0=0, prev=None):
    n = len(parts)
    land = [(p.shape[0], p.shape[1] if rows is None else rows, p.shape[2]) for p in parts]

    def window(ref, slot, t):
        return ref.at[slot, pl.ds(row0, parts[t].shape[1])]

    def sends(p_in, r_out, sems):
        send, recv = sems
        x, y, c = _place()
        chip = 2 * x + y
        return [pltpu.make_async_remote_copy(
            src_ref=p_in[t].at[pchip], dst_ref=window(r_out[t], chip, t),
            send_sem=send.at[t, k], recv_sem=recv.at[t, k],
            device_id=(px, py, c), device_id_type=MESH)
            for t in range(n) for (k, px, py, pchip) in _chip_peers(x, y)]

    def start(p_in, r_out, sems):
        for cp in sends(p_in, r_out, sems):
            cp.start()

    def finish(p_in, r_out, sems):
        send, recv = sems
        x, y, c = _place()
        for t in range(n):
            for (k, px, py, pchip) in _chip_peers(x, y):
                pltpu.make_async_remote_copy(
                    src_ref=p_in[t].at[pchip], dst_ref=window(r_out[t], pchip, t),
                    send_sem=send.at[t, k], recv_sem=recv.at[t, k],
                    device_id=(px, py, c), device_id_type=MESH).wait_recv()
        for cp in sends(p_in, r_out, sems):
            cp.wait_send()

    dma = pltpu.SemaphoreType.DMA
    operands = list(parts) + (list(prev) if prev else [])
    return _Exchange(operands, [jax.ShapeDtypeStruct(sh, p.dtype) for sh, p in zip(land, parts)],
                     [dma((n, 3)), dma((n, 3))], start, finish,
                     aliases={n + t: t for t in range(n)} if prev else None)


def _w_in_grad_sibling(h, dpa, dgl, c_idx, carried, *, tm, tn, rows, pq):
    s, d = h.shape
    wa, wg = dpa.shape[1], dgl.shape[2]
    p = wa + 2 * wg
    ni, nj = rows // tm, p // tn
    na, qg, qp = wa // tn, wg // tn, pq // tn
    n_steps = ni * nj
    n_ci, n_co = len(carried.operands), len(carried.out_shape)

    def body(c_ref, h_ref, a_ref, g_ref, *rest):
        c_in = rest[:n_ci]
        r1_ref = rest[n_ci]
        c_out = rest[n_ci + 1:n_ci + 1 + n_co]
        slots, send, recv = rest[n_ci + 1 + n_co:n_ci + 4 + n_co]
        sems = rest[n_ci + 4 + n_co:]
        i, j = pl.program_id(0), pl.program_id(1)
        step = i * nj + j
        slot = lax.rem(step, 2)
        x, y, c = _place()
        sibling = (x, y, 1 - c)

        @pl.when(step == 0)
        def _():
            carried.start(c_in, c_out, sems)

        def tile_copy(sl):
            return pltpu.make_async_remote_copy(
                src_ref=slots.at[sl],
                dst_ref=r1_ref.at[j // qp, pl.ds(i * tm, tm), pl.ds((j % qp) * tn, tn)],
                send_sem=send.at[sl], recv_sem=recv, device_id=sibling, device_id_type=MESH)

        @pl.when(step >= 2)
        def _():
            tile_copy(slot).wait_send()

        def emit(b_ref):
            acc = lax.dot_general(h_ref[...], b_ref[...], TN, preferred_element_type=F32)
            slots[slot] = acc.astype(BF16)

        pl.when(j < na)(lambda: emit(a_ref))
        pl.when(j >= na)(lambda: emit(g_ref))
        tile_copy(slot).start()

        @pl.when(step == n_steps - 1)
        def _():
            tile_copy(0).wait_send()
            tile_copy(1).wait_send()
            pltpu.make_async_remote_copy(
                src_ref=r1_ref, dst_ref=r1_ref, send_sem=send.at[0], recv_sem=recv,
                device_id=sibling, device_id_type=MESH).wait_recv()
            carried.finish(c_in, c_out, sems)

    nbh = rows // tm
    dma = pltpu.SemaphoreType.DMA
    blk = s * tm * 2 + 2 * s * tn * 2
    res = pl.pallas_call(
        body, name="w_in_grad_sibling",
        grid_spec=pltpu.PrefetchScalarGridSpec(
            num_scalar_prefetch=1, grid=(ni, nj),
            in_specs=[pl.BlockSpec((s, tm), lambda i, j, cr: (0, (1 - cr[0]) * nbh + i)),
                      pl.BlockSpec((s, tn), lambda i, j, cr: (0, jnp.minimum(j, na - 1))),
                      pl.BlockSpec((None, s, tn),
                                   lambda i, j, cr: (jnp.maximum(j - na, 0) // qg, 0, jnp.maximum(j - na, 0) % qg))]
            + [ANY] * n_ci,
            out_specs=[ANY] * (1 + n_co),
            scratch_shapes=[pltpu.VMEM((2, tm, tn), BF16), dma((2,)), dma] + list(carried.scratch)),
        out_shape=[jax.ShapeDtypeStruct((N_CHIPS, rows, pq), BF16)] + list(carried.out_shape),
        compiler_params=_params(("arbitrary", "arbitrary"), blk, 2 * tm * tn * 2 + 2 * tm * tn * 4),
    )(c_idx, h, dpa, dgl, *carried.operands)
    return res[0], res[1:]


def _proj_gathering(h, chip_idx, gather_in, gather_rest, *, tm, tn):
    s, d = h.shape
    wb = gather_in.operands[0]
    _, hh, pq = wb.shape
    qp, ni = pq // tn, s // tm
    n_tiles = N_CHIPS * qp
    exs = (gather_in, gather_rest)
    n_ci = [len(e.operands) for e in exs]
    n_co = [len(e.out_shape) for e in exs]
    n_cs = [len(e.scratch) for e in exs]

    def body(ch_ref, h_ref, *rest):
        pos = 0
        c_in, c_out, c_sem = [], [], []
        for n in n_ci:
            c_in.append(rest[pos:pos + n])
            pos += n
        o_ref = rest[pos]
        pos += 1
        for n in n_co:
            c_out.append(rest[pos:pos + n])
            pos += n
        slots, fetch_sem = rest[pos], rest[pos + 1]
        pos += 2
        for n in n_cs:
            c_sem.append(rest[pos:pos + n])
            pos += n
        g, jj, i = pl.program_id(0), pl.program_id(1), pl.program_id(2)
        tile = g * qp + jj
        own_ref, gathered = c_in[0][0], c_out[0][0]

        def w_in_copies():
            return gather_in.copies(c_in[0], c_out[0], c_sem[0])

        def fetch(src, slot):
            return pltpu.make_async_copy(src, slots.at[slot], fetch_sem.at[slot])

        def fetch_own(col_tile, slot):
            return fetch(own_ref.at[:, :, pl.ds(col_tile * tn, tn)], slot)

        def fetch_gathered(shard, col_tile, slot):
            return fetch(gathered.at[shard, :, :, pl.ds(col_tile * tn, tn)], slot)

        first = jnp.logical_and(tile == 0, i == 0)

        @pl.when(first)
        def _():
            cps = w_in_copies()
            for cp in cps.own + cps.ici:
                cp.start()
            fetch_own(0, 0).start()

        @pl.when(i == 0)
        def _():
            fetch_own(0, lax.rem(tile, 2)).wait()

        last_row = i == ni - 1
        nxt = tile + 1
        nslot = lax.rem(nxt, 2)

        @pl.when(jnp.logical_and(last_row, nxt < qp))
        def _():
            fetch_own(nxt, nslot).start()

        @pl.when(jnp.logical_and(last_row, nxt == qp))
        def _():
            cps = w_in_copies()
            for k in range(2):
                cps.landed[k].wait_recv()
                cps.passed[k].start()
            cps.relay[0].start()
            gather_rest.start(c_in[1], c_out[1], c_sem[1])
            cps.from_sibling[0].wait_recv()

        @pl.when(jnp.logical_and(last_row, nxt == 2 * qp))
        def _():
            w_in_copies().from_sibling[1].wait_recv()

        @pl.when(jnp.logical_and(last_row, nxt == 2 * qp + qp // 2))
        def _():
            cps = w_in_copies()
            cps.landed[2].wait_recv()
            cps.passed[2].start()

        @pl.when(jnp.logical_and(last_row, nxt == 3 * qp))
        def _():
            w_in_copies().from_sibling[2].wait_recv()

        @pl.when(jnp.logical_and(last_row, jnp.logical_and(nxt >= qp, nxt < n_tiles)))
        def _():
            fetch_gathered(jnp.bitwise_xor(ch_ref[0], nxt // qp), lax.rem(nxt, qp), nslot).start()

        w_tile = slots[lax.rem(tile, 2)].reshape(d, tn)
        o_ref[...] = jnp.dot(h_ref[...], w_tile, preferred_element_type=F32).astype(BF16)

        @pl.when(jnp.logical_and(tile == n_tiles - 1, last_row))
        def _():
            cps = w_in_copies()
            for cp in cps.ici + cps.relay + cps.passed:
                cp.wait_send()
            for cp in cps.own:
                cp.wait()
            gather_rest.finish(c_in[1], c_out[1], c_sem[1])

    assert qp >= 2 and not gather_in.aliases and not gather_rest.aliases
    dma = pltpu.SemaphoreType.DMA
    p = N_CHIPS * pq
    res = pl.pallas_call(
        body, name="proj_gathering",
        grid_spec=pltpu.PrefetchScalarGridSpec(
            num_scalar_prefetch=1, grid=(N_CHIPS, qp, ni),
            in_specs=[pl.BlockSpec((tm, d), lambda g, jj, i, ch: (i, 0))] + [ANY] * sum(n_ci),
            out_specs=[pl.BlockSpec((tm, tn), lambda g, jj, i, ch: (i, jnp.bitwise_xor(ch[0], g) * qp + jj))]
            + [ANY] * sum(n_co),
            scratch_shapes=[pltpu.VMEM((2, 2, hh, tn), BF16), dma((2,))]
            + list(gather_in.scratch) + list(gather_rest.scratch)),
        out_shape=[jax.ShapeDtypeStruct((s, p), BF16)] + list(gather_in.out_shape) + list(gather_rest.out_shape),
        compiler_params=_params(("arbitrary",) * 3, tm * d * 2 + tm * tn * 2, 3 * d * tn * 2 + tm * tn * 4),
    )(chip_idx, h, *gather_in.operands, *gather_rest.operands)
    return res[0], res[1:1 + n_co[0]], res[1 + n_co[0]:]


def _gather_packs(pack):
    def body(p_ref, o_ref, send, recv, loc):
        x, y, c = _place()
        me = 4 * x + 2 * y + c
        mine = pltpu.make_async_copy(p_ref, o_ref.at[me], loc)
        mine.start()
        flips = [(fx, fy, fc) for fx in (0, 1) for fy in (0, 1) for fc in (0, 1)][1:]
        cps = []
        for k, (fx, fy, fc) in enumerate(flips):
            peer = (1 - x if fx else x, 1 - y if fy else y, 1 - c if fc else c)
            cps.append(pltpu.make_async_remote_copy(
                src_ref=p_ref, dst_ref=o_ref.at[me], send_sem=send.at[k], recv_sem=recv.at[k],
                device_id=peer, device_id_type=MESH))
        for cp in cps:
            cp.start()
        for k, (fx, fy, fc) in enumerate(flips):
            px, py, pc = (1 - x if fx else x, 1 - y if fy else y, 1 - c if fc else c)
            pltpu.make_async_remote_copy(
                src_ref=p_ref, dst_ref=o_ref.at[4 * px + 2 * py + pc],
                send_sem=send.at[k], recv_sem=recv.at[k],
                device_id=(px, py, pc), device_id_type=MESH).wait_recv()
        for cp in cps:
            cp.wait_send()
        mine.wait()

    dma = pltpu.SemaphoreType.DMA
    return pl.pallas_call(
        body, name="gather_packs", out_shape=jax.ShapeDtypeStruct((N_DEV,) + pack.shape, pack.dtype),
        in_specs=[ANY], out_specs=ANY, scratch_shapes=[dma((7,)), dma((7,)), dma],
    )(pack)


def _flat_pack(pieces):
    flat = jnp.concatenate([p.reshape(-1) for p in pieces])
    pad = (-flat.shape[0]) % (8 * LANES)
    flat = jnp.concatenate([flat, jnp.zeros((pad,), F32)])
    return flat.reshape(-1, LANES)


def _unpack(pack, shapes):
    flat = pack.reshape(-1)
    out, off = [], 0
    for sh in shapes:
        n = 1
        for s in sh:
            n *= s
        out.append(flat[off:off + n].reshape(sh))
        off += n
    return out


def kernel(x, norm_w, w_in, pool_w, pool_scale, conv_w, conv_b, gate_b, w_branch, w_out, final_norm_w, loss_target, m_norm_w, m_w_in, m_pool_w, m_pool_scale, m_conv_w, m_conv_b, m_gate_b, m_w_branch, m_w_out, m_final_norm_w, v_norm_w, v_w_in, v_pool_w, v_pool_scale, v_conv_w, v_conv_b, v_gate_b, v_w_branch, v_w_out, v_final_norm_w):
    _, s, d = x.shape
    w = d // 2
    cg = w // N_POOL_GROUPS
    p = 6 * w + 2 * d
    pq = p // N_CHIPS
    dq = d // N_CHIPS
    assert w_in.shape == (1, d, pq) and w_branch.shape == (1, 2, w, dq) and w_out.shape == (1, dq, d)
    assert pool_w.shape == (1, N_POOL_GROUPS, cg // N_CHIPS, cg) and conv_w.shape == (1, CONV_K, cg)

    x2d = x.reshape(s, d)
    tgt = loss_target.reshape(s, d)
    c_idx = lax.axis_index("c").astype(jnp.int32).reshape(1)
    chip = 2 * lax.axis_index("x") + lax.axis_index("y")

    big_w = [w_in.reshape(d, pq), w_out.reshape(dq, d), w_branch.reshape(2 * w, dq),
             pool_w.reshape(cg, cg)]
    names = ["in", "out", "branch", "pool"]
    big_b = [_cast_bf16("cast_" + nm, a) for nm, a in zip(names, big_w)]
    big_b = [b.reshape(2, b.shape[0] // 2, b.shape[1]) for b in big_b]
    chip_idx = chip.astype(jnp.int32).reshape(1)

    h = _rms_fwd(x2d, norm_w)

    tm = _pick(s, 1024, 16)
    tn_p = _pick(math.gcd(pq, 6 * w, d), 1024, LANES)
    qp = pq // tn_p
    tk_d = _pick(d, 4096, LANES)
    proj, (wg_in,), (wg_out, wg_br, wg_pool, wg_cw, wg_gb) = _proj_gathering(
        h, chip_idx, _gather_weights(big_b[:1], [], relay=True),
        _gather_weights(big_b[1:], [conv_w.reshape(CONV_K, cg), gate_b.reshape(2, dq)]),
        tm=tm, tn=tn_p)
    wg_in = wg_in.reshape(N_CHIPS, d, pq)
    wg_out = wg_out.reshape(d, d)
    wg_pool = wg_pool.reshape(N_CHIPS, N_POOL_GROUPS, cg // N_CHIPS, cg)

    ys = _mix_fwd(proj, wg_pool, pool_scale, wg_cw, conv_b, w)

    tn_d = _pick(dq, 1024, LANES)
    qd = dq // tn_d
    tk_w = _pick(w, 2048, LANES)
    tm_g = _pick(s, 512, 16)
    gl0 = (6 * w) // tn_d
    gl1 = (6 * w + d) // tn_d

    def gate_specs(im, rows):
        return [pl.BlockSpec((rows, tn_d), lambda *a: (im(*a)[0], gl0 + im(*a)[1])),
                pl.BlockSpec((rows, tn_d), lambda *a: (im(*a)[0], gl1 + im(*a)[1])),
                pl.BlockSpec((None, 2, tn_d), lambda *a: (im(*a)[1] // qd, 0, im(*a)[1] % qd))]

    def merge_epilogue(accs, ins, outs, ids):
        gb = ins[6][...]
        g0 = _sigmoid(ins[4][...].astype(F32) + gb[0:1])
        g1 = _sigmoid(ins[5][...].astype(F32) + gb[1:2])
        outs[0][...] = (g0 * accs[0] + g1 * accs[1]).astype(BF16)
        outs[1][0] = accs[0].astype(BF16)
        outs[1][1] = accs[1].astype(BF16)

    (merged, br), _ = _mm(
        "branch_merge", grid=(s // tm_g, d // tn_d, w // tk_w),
        operands=[ys, wg_br, ys, wg_br, proj, proj, wg_gb],
        in_specs=[pl.BlockSpec((None, tm_g, tk_w), lambda i, j, k: (0, i, k)),
                  pl.BlockSpec((None, None, tk_w, tn_d), lambda i, j, k: (j // qd, 0, k, j % qd)),
                  pl.BlockSpec((None, tm_g, tk_w), lambda i, j, k: (1, i, k)),
                  pl.BlockSpec((None, None, tk_w, tn_d), lambda i, j, k: (j // qd, 1, k, j % qd)),
                  *gate_specs(lambda i, j, k: (i, j), tm_g)],
        out_shape=[jax.ShapeDtypeStruct((s, d), BF16), jax.ShapeDtypeStruct((2, s, d), BF16)],
        out_specs=[pl.BlockSpec((tm_g, tn_d), lambda i, j, k: (i, j)),
                   pl.BlockSpec((2, tm_g, tn_d), lambda i, j, k: (0, i, j))],
        pairs=[(0, 1, 0, None), (2, 3, 1, None)], dims=NN, acc_shapes=[(tm_g, tn_d)] * 2,
        epilogue=merge_epilogue, temp_bytes=6 * tm_g * tn_d * 4,
    )

    tn_f = _pick(d, 1024, LANES)
    o = _mm(
        "out_proj", grid=(s // tm, d // tn_f, d // tk_d), operands=[merged, wg_out],
        in_specs=[pl.BlockSpec((tm, tk_d), lambda i, j, k: (i, k)),
                  pl.BlockSpec((tk_d, tn_f), lambda i, j, k: (k, j))],
        out_shape=[jax.ShapeDtypeStruct((s, d), F32)],
        out_specs=[pl.BlockSpec((tm, tn_f), lambda i, j, k: (i, j))],
        pairs=[(0, 1, 0, None)], dims=NN, acc_shapes=[(tm, tn_f)],
        epilogue=lambda accs, ins, outs, ids: outs[0].__setitem__(Ellipsis, accs[0]),
    )[0][0]

    dx2, dx2b, loss_part, g_fnw =_head(x2d, o, tgt, final_norm_w.reshape(1, d))

    def gate_bwd_epilogue(accs, ins, outs, ids):
        dm = accs[0]
        gb = ins[5][...]
        i = ids[1]

        @pl.when(i == 0)
        def _():
            outs[2][...] = jnp.zeros(outs[2].shape, F32)

        for n in range(2):
            gate = _sigmoid(ins[3 + n][...].astype(F32) + gb[n:n + 1])
            outs[0][n] = (dm * gate).astype(BF16)
            dgl = dm * ins[2][n].astype(F32) * gate * (1.0 - gate)
            outs[1][n] = dgl.astype(BF16)
            outs[2][n:n + 1, :] += jnp.sum(dgl, axis=0, keepdims=True)

    tm_b = _pick(s, 256, 16)
    (d_br, dgl, g_gb), _ = _mm(
        "out_proj_bwd_gate", grid=(d // tn_d, s // tm_b, d // tk_d),
        operands=[dx2b, wg_out, br, proj, proj, wg_gb],
        in_specs=[pl.BlockSpec((tm_b, tk_d), lambda j, i, k: (i, k)),
                  pl.BlockSpec((tn_d, tk_d), lambda j, i, k: (j, k)),
                  pl.BlockSpec((2, tm_b, tn_d), lambda j, i, k: (0, i, j)),
                  *gate_specs(lambda j, i, k: (i, j), tm_b)],
        out_shape=[jax.ShapeDtypeStruct((2, s, d), BF16), jax.ShapeDtypeStruct((2, s, d), BF16),
                   jax.ShapeDtypeStruct((2, d), F32)],
        out_specs=[pl.BlockSpec((2, tm_b, tn_d), lambda j, i, k: (0, i, j)),
                   pl.BlockSpec((2, tm_b, tn_d), lambda j, i, k: (0, i, j)),
                   pl.BlockSpec((2, tn_d), lambda j, i, k: (0, j))],
        pairs=[(0, 1, 0, None)], dims=NT, acc_shapes=[(tm_b, tn_d)],
        epilogue=gate_bwd_epilogue, semantics=("parallel", "arbitrary", "arbitrary"),
        temp_bytes=8 * tm_b * tn_d * 4,
    )

    hh_out = d // 8
    tm_o = _pick(hh_out, 512, LANES)
    nb_o = hh_out // tm_o
    tk_s = _pick(s, 4096, LANES)
    g_out = _mm(
        "w_out_grad", grid=(d // tm_o, d // tn_f, s // tk_s), operands=[merged, dx2b],
        in_specs=[pl.BlockSpec((tk_s, tm_o), lambda i, j, k: (k, i)),
                  pl.BlockSpec((tk_s, tn_f), lambda i, j, k: (k, j))],
        out_shape=[jax.ShapeDtypeStruct((2, N_CHIPS, hh_out, d), BF16)],
        out_specs=[pl.BlockSpec((None, None, tm_o, tn_f),
                                lambda i, j, k: ((i // nb_o) % 2, i // (2 * nb_o), i % nb_o, j))],
        pairs=[(0, 1, 0, None)], dims=TN, acc_shapes=[(tm_o, tn_f)],
        epilogue=lambda accs, ins, outs, ids: outs[0].__setitem__(Ellipsis, accs[0].astype(BF16)),
    )[0][0]

    tn_w = _pick(w, 2048, LANES)
    tk_q = _pick(dq, 1024, LANES)
    qk = dq // tk_q
    (dys,), from_sibling_out = _mm(
        "branch_bwd", grid=(2, s // tm, w // tn_w, d // tk_q), operands=[d_br, wg_br],
        comm=_to_sibling([g_out], other_half=True),
        in_specs=[pl.BlockSpec((None, tm, tk_q), lambda n, i, j, k: (n, i, k)),
                  pl.BlockSpec((None, None, tn_w, tk_q), lambda n, i, j, k: (k // qk, n, j, k % qk))],
        out_shape=[jax.ShapeDtypeStruct((2, s, w), F32)],
        out_specs=[pl.BlockSpec((None, tm, tn_w), lambda n, i, j, k: (n, i, j))],
        pairs=[(0, 1, 0, None)], dims=NT, acc_shapes=[(tm, tn_w)],
        epilogue=lambda accs, ins, outs, ids: outs[0].__setitem__(Ellipsis, accs[0]),
    )

    tm_w = _pick(w, 1024, LANES)
    g_br = _mm(
        "w_branch_grad", grid=(2, w // tm_w, d // tn_d, s // tk_s), operands=[ys, d_br],
        in_specs=[pl.BlockSpec((None, tk_s, tm_w), lambda n, i, j, k: (n, k, i)),
                  pl.BlockSpec((None, tk_s, tn_d), lambda n, i, j, k: (n, k, j))],
        out_shape=[jax.ShapeDtypeStruct((2, N_CHIPS, w, dq), BF16)],
        out_specs=[pl.BlockSpec((None, None, tm_w, tn_d), lambda n, i, j, k: (n, j // qd, i, j % qd))],
        pairs=[(0, 1, 0, None)], dims=TN, acc_shapes=[(tm_w, tn_d)],
        epilogue=lambda accs, ins, outs, ids: outs[0].__setitem__(Ellipsis, accs[0].astype(BF16)),
    )[0][0]

    dpa, g_pool, g_ps, g_cw, g_cb = _mix_bwd(proj, dys, wg_pool, pool_scale, wg_cw, conv_b, w)
    g_pool = g_pool.astype(BF16).reshape(2, N_CHIPS, 2 * (cg // N_CHIPS), cg)

    early = [g_out, g_br, g_pool]
    from_sibling_early = list(from_sibling_out) + _run_exchange(
        "pair_exchange", _to_sibling(early[1:], other_half=True))
    parts_early = []
    for nm, g, r1 in zip(names[1:], early, from_sibling_early):
        _, nc, hh, wd = g.shape
        pt = _pair_add("pair_add_" + nm, g.reshape(2, nc * hh, wd), r1.reshape(nc * hh, wd), c_idx)
        parts_early.append(pt.reshape(nc, hh, wd))

    na = (6 * w) // tn_p
    qg = d // tn_p
    hh_in = d // 2
    assert s == tk_s
    tm_i = _pick(hh_in, 1024, LANES)
    tn_i = _pick(tn_p, 512, LANES)
    from_sibling, recv_early = _w_in_grad_sibling(
        h, dpa, dgl, c_idx, _scatter_partials(parts_early), tm=tm_i, tn=tn_i, rows=hh_in, pq=pq)

    tm_o2 = _pick(hh_in, 512, LANES)
    nb_o2 = hh_in // tm_o2

    def own_rows(name, first, count, comm):
        def add_sibling(accs, ins, outs, ids):
            outs[0][...] = (accs[0] + ins[3][...].astype(F32)).astype(BF16)

        return _mm(
            name, grid=(p // tn_p, count, 1), operands=[h, dpa, dgl, from_sibling], comm=comm, prefetch=c_idx,
            in_specs=[pl.BlockSpec((tk_s, tm_o2), lambda j, i, k, cr: (0, cr[0] * nb_o2 + first + i)),
                      pl.BlockSpec((tk_s, tn_p), lambda j, i, k, cr: (0, jnp.minimum(j, na - 1))),
                      pl.BlockSpec((None, tk_s, tn_p),
                                   lambda j, i, k, cr: (jnp.maximum(j - na, 0) // qg, 0, jnp.maximum(j - na, 0) % qg)),
                      pl.BlockSpec((None, tm_o2, tn_p), lambda j, i, k, cr: (j // qp, first + i, j % qp))],
            out_shape=[jax.ShapeDtypeStruct((N_CHIPS, count * tm_o2, pq), BF16)],
            out_specs=[pl.BlockSpec((None, tm_o2, tn_p), lambda j, i, k, cr: (j // qp, i, j % qp))],
            pairs=[(0, 1, 0, lambda ids: ids[0] < na), (0, 2, 0, lambda ids: ids[0] >= na)],
            dims=TN, acc_shapes=[(tm_o2, tn_p)], epilogue=add_sibling)

    halves_early = [_sum_chips("sum_chips_" + nm, pt, r2, chip_idx)
                    for nm, pt, r2 in zip(names[1:], parts_early, recv_early)]
    share_early = _to_sibling(halves_early, other_half=False)

    n_first = 1 if nb_o2 > 1 else 0
    if n_first:
        (part_a,), others_early = own_rows("w_in_grad_own_a", 0, n_first, share_early)
        scatter_a = _scatter_partials([part_a], rows=hh_in)
        (part_b,), land_a = own_rows("w_in_grad_own_b", n_first, nb_o2 - n_first, scatter_a)
        scatter_b = _scatter_partials([part_b], rows=hh_in, row0=n_first * tm_o2, prev=land_a)
    else:
        (part_b,), others_early = own_rows("w_in_grad_own", 0, nb_o2, share_early)
        scatter_b = _scatter_partials([part_b])

    tn_h = _pick(d, 2048, LANES)
    (dh,), recv_in = _mm(
        "proj_bwd", grid=(s // tm, d // tn_h, p // tn_p), operands=[dpa, dgl, wg_in],
        comm=scatter_b,
        in_specs=[pl.BlockSpec((tm, tn_p), lambda i, j, k: (i, jnp.minimum(k, na - 1))),
                  pl.BlockSpec((None, tm, tn_p),
                               lambda i, j, k: (jnp.maximum(k - na, 0) // qg, i, jnp.maximum(k - na, 0) % qg)),
                  pl.BlockSpec((None, tn_h, tn_p), lambda i, j, k: (k // qp, j, k % qp))],
        out_shape=[jax.ShapeDtypeStruct((s, d), F32)],
        out_specs=[pl.BlockSpec((tm, tn_h), lambda i, j, k: (i, j))],
        pairs=[(0, 2, 0, lambda ids: ids[2] < na), (1, 2, 0, lambda ids: ids[2] >= na)],
        dims=NT, acc_shapes=[(tm, tn_h)],
        epilogue=lambda accs, ins, outs, ids: outs[0].__setitem__(Ellipsis, accs[0]),
    )

    grad_x, g_nw = _rms_bwd(x2d, dh, dx2, norm_w)

    if n_first:
        half_in = _sum_chips("sum_chips_in_a", part_a, recv_in[0], chip_idx)
        half_in = _sum_chips("sum_chips_in_b", part_b, recv_in[0], chip_idx, row0=n_first * tm_o2, prev=half_in)
    else:
        half_in = _sum_chips("sum_chips_in", part_b, recv_in[0], chip_idx)
    halves = [half_in] + halves_early
    others = _run_exchange("pair_share", _to_sibling([half_in], other_half=False)) + list(others_early)

    g_cw_full = jnp.transpose(g_cw, (1, 0, 2)).reshape(CONV_K, w)
    small_shapes = [(LANES,), (1, d), (1, d), (1, w), (1, w), (CONV_K, w), (2, d)]
    pack = _flat_pack([loss_part[0], g_nw, g_fnw, g_ps, g_cb, g_cw_full, g_gb])
    total = _sum_devices(_gather_packs(pack))
    t_loss, t_nw, t_fnw, t_ps, t_cb, t_cw, t_gb = _unpack(total, small_shapes)
    loss = t_loss[0]
    t_cw = lax.dynamic_slice_in_dim(t_cw, chip * cg, cg, axis=1)
    t_gb = lax.dynamic_slice_in_dim(t_gb, chip * dq, dq, axis=1)

    out = {}
    big_names = ["w_in", "w_out", "w_branch", "pool_w"]
    big_m = [m_w_in, m_w_out, m_w_branch, m_pool_w]
    big_v = [v_w_in, v_w_out, v_w_branch, v_pool_w]
    big_orig = [w_in, w_out, w_branch, pool_w]
    for nm, g_own, g_other, w2, mm_, vv_, orig in zip(big_names, halves, others, big_w, big_m, big_v, big_orig):
        sh = (2,) + g_own.shape
        res = _adamw_halves("adamw_" + nm, g_own, g_other, c_idx,
                            w2.reshape(sh), mm_.reshape(sh), vv_.reshape(sh))
        out[nm] = [r.reshape(orig.shape) for r in res]

    sm_names = ["norm_w", "final_norm_w", "pool_scale", "conv_b", "conv_w", "gate_b"]
    sm_g = [t_nw, t_fnw, t_ps, t_cb, t_cw, t_gb]
    sm_w = [norm_w, final_norm_w, pool_scale, conv_b, conv_w, gate_b]
    sm_m = [m_norm_w, m_final_norm_w, m_pool_scale, m_conv_b, m_conv_w, m_gate_b]
    sm_v = [v_norm_w, v_final_norm_w, v_pool_scale, v_conv_b, v_conv_w, v_gate_b]
    sm_shapes = [a.shape for a in sm_w]
    res = _adamw("adamw_small", _flat_pack(sm_g), _flat_pack(sm_w), _flat_pack(sm_m), _flat_pack(sm_v))
    res = [_unpack(r, sm_shapes) for r in res]
    for idx, nm in enumerate(sm_names):
        out[nm] = [r[idx] for r in res]

    order = ["norm_w", "w_in", "pool_w", "pool_scale", "conv_w", "conv_b", "gate_b", "w_branch", "w_out",
             "final_norm_w"]
    outs = [loss, grad_x.reshape(x.shape)]
    for kind in range(4):
        outs += [out[nm][kind] for nm in order]
    return tuple(outs)
```

```python
import math

import jax
import jax.numpy as jnp
from jax import lax
from jax.experimental import pallas as pl
from jax.experimental.pallas import tpu as pltpu

F32 = jnp.float32
BF16 = jnp.bfloat16

NORM_EPS = 1e-6
POOL_WINDOWS = (2, 4, 8, 16)
N_POOL_GROUPS = len(POOL_WINDOWS)
CONV_K = 3
ADAM_LR = 0.001
ADAM_B1 = 0.9
ADAM_B2 = 0.999
ADAM_EPS = 1e-08
ADAM_WD = 0.01
ADAM_STEP = 10

N_CHIPS = 4
N_DEV = 8
HALO = 16
LANES = 128
V7X_VMEM_BYTES = 64 * 1024 * 1024
VMEM_CAP = V7X_VMEM_BYTES - 8 * 1024 * 1024

MESH = pl.DeviceIdType.MESH
ANY = pl.BlockSpec(memory_space=pl.ANY)

NN = (((1,), (0,)), ((), ()))
NT = (((1,), (1,)), ((), ()))
TN = (((0,), (0,)), ((), ()))


def _pick(dim, pref, align):
    if dim <= pref:
        return dim
    t = (pref // align) * align
    while t >= align:
        if dim % t == 0:
            return t
        t -= align
    raise ValueError(f"no tile for {dim} (pref {pref}, align {align})")


def _nbytes(shape, dtype):
    n = 1
    for s in shape:
        if s is not None:
            n *= s
    return n * jnp.dtype(dtype).itemsize


def _params(semantics, block_bytes, extra_bytes=0):
    need = 2 * block_bytes + extra_bytes + (2 << 20)
    return pltpu.CompilerParams(dimension_semantics=semantics,
                                vmem_limit_bytes=int(min(max(need, 16 << 20), VMEM_CAP)))


def _sigmoid(z):
    return jax.nn.sigmoid(z)


def _mm(name, *, grid, operands, in_specs, out_shape, out_specs, pairs, dims, acc_shapes, epilogue,
        semantics=None, temp_bytes=0, comm=None, prefetch=None, aliases=None):
    aliases = dict(aliases or {})
    n_in, n_out = len(operands), len(out_shape)
    kax = len(grid) - 1
    nk = grid[kax]
    single = nk == 1
    conditional = any(p[3] is not None for p in pairs)
    if single and conditional:
        assert len(acc_shapes) == 1 and all(p[3] is not None for p in pairs)
    n_acc = 0 if single else len(acc_shapes)
    n_pf = 0 if prefetch is None else 1
    c_ops = list(comm.operands) if comm else []
    c_out = list(comm.out_shape) if comm else []
    c_sems = list(comm.scratch) if comm else []
    c_alias = dict(comm.aliases) if comm else {}
    n_ci, n_co = len(c_ops), len(c_out)

    def product(ins, ai, bi):
        return lax.dot_general(ins[ai][...], ins[bi][...], dims, preferred_element_type=F32)

    def body(*refs):
        refs = refs[n_pf:]
        ins = refs[:n_in]
        c_in_refs = refs[n_in:n_in + n_ci]
        outs = refs[n_in + n_ci:n_in + n_ci + n_out]
        c_out_refs = refs[n_in + n_ci + n_out:n_in + n_ci + n_out + n_co]
        scratch = refs[n_in + n_ci + n_out + n_co:]
        accs = scratch[:n_acc]
        sems = scratch[n_acc:]
        ids = [pl.program_id(a) for a in range(len(grid))]
        k = ids[kax]

        if comm:
            is_first = ids[0] == 0
            is_last = ids[0] == grid[0] - 1
            for a in range(1, len(grid)):
                is_first = jnp.logical_and(is_first, ids[a] == 0)
                is_last = jnp.logical_and(is_last, ids[a] == grid[a] - 1)

            @pl.when(is_first)
            def _():
                comm.start(c_in_refs, c_out_refs, sems)

        if single and conditional:
            for (ai, bi, ci, cond) in pairs:
                def only(ai=ai, bi=bi):
                    epilogue([product(ins, ai, bi)], ins, outs, ids)
                pl.when(cond(ids))(only)
        elif single:
            vals = [None] * len(acc_shapes)
            for (ai, bi, ci, cond) in pairs:
                r = product(ins, ai, bi)
                vals[ci] = r if vals[ci] is None else vals[ci] + r
            epilogue(vals, ins, outs, ids)
        else:
            @pl.when(k == 0)
            def _():
                for a in accs:
                    a[...] = jnp.zeros(a.shape, a.dtype)

            for (ai, bi, ci, cond) in pairs:
                def step(ai=ai, bi=bi, ci=ci):
                    accs[ci][...] += product(ins, ai, bi)
                if cond is None:
                    step()
                else:
                    pl.when(cond(ids))(step)

            @pl.when(k == nk - 1)
            def _():
                epilogue([a[...] for a in accs], ins, outs, ids)

        if comm:
            @pl.when(is_last)
            def _():
                comm.finish(c_in_refs, c_out_refs, sems)

    if semantics is None:
        semantics = ("parallel",) * kax + ("arbitrary",)
    if comm:
        semantics = ("arbitrary",) * len(grid)
    in_specs = [ANY if idx in aliases else spec for idx, spec in enumerate(in_specs)]
    blk = 0
    for idx, (spec, op) in enumerate(zip(in_specs, operands)):
        if idx not in aliases:
            blk += _nbytes(spec.block_shape, op.dtype)
    for spec, o in zip(out_specs, out_shape):
        blk += _nbytes(spec.block_shape, o.dtype)
    acc_bytes = sum(_nbytes(s, F32) for s in acc_shapes)
    io_alias = {n_pf + i: o for i, o in aliases.items()}
    io_alias.update({n_pf + n_in + i: n_out + o for i, o in c_alias.items()})
    all_in = list(in_specs) + [ANY] * n_ci
    all_out = list(out_specs) + [ANY] * n_co
    scratch_shapes = [pltpu.VMEM(s, F32) for s in acc_shapes[:n_acc]] + c_sems
    params = _params(semantics, blk, 3 * acc_bytes + temp_bytes)
    shapes = list(out_shape) + c_out
    if prefetch is None:
        call = pl.pallas_call(
            body, name=name, grid=grid, in_specs=all_in, out_specs=all_out, out_shape=shapes,
            scratch_shapes=scratch_shapes, input_output_aliases=io_alias, compiler_params=params)
        res = call(*operands, *c_ops)
    else:
        call = pl.pallas_call(
            body, name=name, out_shape=shapes, input_output_aliases=io_alias, compiler_params=params,
            grid_spec=pltpu.PrefetchScalarGridSpec(
                num_scalar_prefetch=1, grid=grid, in_specs=all_in, out_specs=all_out,
                scratch_shapes=scratch_shapes))
        res = call(prefetch, *operands, *c_ops)
    return res[:n_out], res[n_out:]


def _cast_bf16(name, w2d):
    r, c = w2d.shape
    tr = _pick(r, max(16, (4 << 20) // (4 * c)), 16)

    def body(w_ref, o_ref):
        o_ref[...] = w_ref[...].astype(BF16)

    return pl.pallas_call(
        body, name=name, grid=(r // tr,),
        in_specs=[pl.BlockSpec((tr, c), lambda i: (i, 0))],
        out_specs=pl.BlockSpec((tr, c), lambda i: (i, 0)),
        out_shape=jax.ShapeDtypeStruct((r, c), BF16),
        compiler_params=_params(("parallel",), tr * c * 6),
    )(w2d)


def _rms_fwd(x, norm_w):
    s, d = x.shape
    ts = _pick(s, 256, 16)

    def body(x_ref, w_ref, h_ref):
        xv = x_ref[...]
        rstd = lax.rsqrt(jnp.mean(xv * xv, axis=-1, keepdims=True) + NORM_EPS)
        h_ref[...] = (xv * rstd * w_ref[...]).astype(BF16)

    return pl.pallas_call(
        body, name="rms_fwd", grid=(s // ts,),
        in_specs=[pl.BlockSpec((ts, d), lambda i: (i, 0)), pl.BlockSpec((1, d), lambda i: (0, 0))],
        out_specs=pl.BlockSpec((ts, d), lambda i: (i, 0)),
        out_shape=jax.ShapeDtypeStruct((s, d), BF16),
        compiler_params=_params(("parallel",), ts * d * 6, 3 * ts * d * 4),
    )(x, norm_w)


def _head(x, o, target, fnw):
    s, d = x.shape
    ts = _pick(s, 128, 16)

    def body(x_ref, o_ref, t_ref, w_ref, dx_ref, dxb_ref, loss_ref, gw_ref):
        i = pl.program_id(0)

        @pl.when(i == 0)
        def _():
            loss_ref[...] = jnp.zeros(loss_ref.shape, F32)
            gw_ref[...] = jnp.zeros(gw_ref.shape, F32)

        w = w_ref[...]
        x2 = x_ref[...] + o_ref[...]
        rstd = lax.rsqrt(jnp.mean(x2 * x2, axis=-1, keepdims=True) + NORM_EPS)
        n = x2 * rstd
        e = n * w - t_ref[...]
        loss_ref[...] += 0.5 * jnp.sum(e * e) / d
        dy = e / d
        gw_ref[...] += jnp.sum(dy * n, axis=0, keepdims=True)
        gy = dy * w
        dx = rstd * (gy - n * jnp.mean(gy * n, axis=-1, keepdims=True))
        dx_ref[...] = dx
        dxb_ref[...] = dx.astype(BF16)

    row = pl.BlockSpec((ts, d), lambda i: (i, 0))
    return pl.pallas_call(
        body, name="head", grid=(s // ts,),
        in_specs=[row, row, row, pl.BlockSpec((1, d), lambda i: (0, 0))],
        out_specs=[row, row, pl.BlockSpec((8, LANES), lambda i: (0, 0)),
                   pl.BlockSpec((1, d), lambda i: (0, 0))],
        out_shape=[jax.ShapeDtypeStruct((s, d), F32), jax.ShapeDtypeStruct((s, d), BF16),
                   jax.ShapeDtypeStruct((8, LANES), F32), jax.ShapeDtypeStruct((1, d), F32)],
        compiler_params=_params(("arbitrary",), ts * d * 22, 6 * ts * d * 4),
    )(x, o, target, fnw)


def _rms_bwd(x, dh, dx2, norm_w):
    s, d = x.shape
    ts = _pick(s, 128, 16)

    def body(x_ref, dh_ref, dx2_ref, w_ref, gx_ref, gw_ref):
        i = pl.program_id(0)

        @pl.when(i == 0)
        def _():
            gw_ref[...] = jnp.zeros(gw_ref.shape, F32)

        xv = x_ref[...]
        rstd = lax.rsqrt(jnp.mean(xv * xv, axis=-1, keepdims=True) + NORM_EPS)
        n = xv * rstd
        dhv = dh_ref[...]
        gw_ref[...] += jnp.sum(dhv * n, axis=0, keepdims=True)
        gh = dhv * w_ref[...]
        gx_ref[...] = dx2_ref[...] + rstd * (gh - n * jnp.mean(gh * n, axis=-1, keepdims=True))

    row = pl.BlockSpec((ts, d), lambda i: (i, 0))
    vec = pl.BlockSpec((1, d), lambda i: (0, 0))
    return pl.pallas_call(
        body, name="rms_bwd", grid=(s // ts,),
        in_specs=[row, row, row, vec], out_specs=[row, vec],
        out_shape=[jax.ShapeDtypeStruct((s, d), F32), jax.ShapeDtypeStruct((1, d), F32)],
        compiler_params=_params(("arbitrary",), ts * d * 16, 5 * ts * d * 4),
    )(x, dh, dx2, norm_w)


def _silu(z):
    return z * _sigmoid(z)


def _window_sum(ext, window, back):
    n = ext.shape[0]
    acc = ext
    step = 1
    while step < window:
        acc = acc + pltpu.roll(acc, step if back else n - step, 0)
        step *= 2
    return acc


def _shift_rows(ext, k, back):
    n = ext.shape[0]
    return pltpu.roll(ext, k if back else n - k, 0)


def _mix_fwd(proj, pool_wg, pool_scale, conv_wg, conv_b, width):
    s = proj.shape[0]
    w = width
    cg = w // N_POOL_GROUPS
    ts = _pick(s, 128, HALO)
    hb = ts // HALO
    cols = 6 * w

    def body(p_ref, ph_ref, pw_ref, ps_ref, cw_ref, cb_ref, ys_ref):
        i = pl.program_id(0)
        first = i == 0
        t1 = (i * ts + lax.broadcasted_iota(jnp.int32, (ts, 1), 0) + 1).astype(F32)

        def tile(part, g):
            lo = part * w + g * cg
            return p_ref[:, lo:lo + cg].astype(F32)

        def prev(part, g):
            lo = part * w + g * cg
            return jnp.where(first, 0.0, ph_ref[:, lo:lo + cg].astype(F32))

        for g, win in enumerate(POOL_WINDOWS):
            gs = slice(g * cg, (g + 1) * cg)
            u = tile(0, g)
            ext = jnp.concatenate([prev(0, g), u], axis=0)
            wsum = _window_sum(ext, win, True)[HALO:]
            pooled = wsum / jnp.minimum(t1, float(win)) - u
            pw = pw_ref[:, g].reshape(cg, cg)
            mixed = jnp.dot(pooled.astype(BF16), pw, preferred_element_type=F32)
            ys_ref[0, :, gs] = (mixed * ps_ref[:, gs] * _silu(tile(1, g))).astype(BF16)
            v = tile(4, g) * tile(2, g)
            vext = jnp.concatenate([prev(4, g) * prev(2, g), v], axis=0)
            v1 = _shift_rows(vext, 1, True)[HALO:]
            v2 = _shift_rows(vext, 2, True)[HALO:]
            cw = [cw_ref[g, tap:tap + 1, :] for tap in range(CONV_K)]
            y = cb_ref[:, gs] + cw[0] * v2 + cw[1] * v1 + cw[2] * v
            ys_ref[1, :, gs] = (tile(3, g) * y * _silu(tile(5, g))).astype(BF16)

    return pl.pallas_call(
        body, name="mix_fwd", grid=(s // ts,),
        in_specs=[pl.BlockSpec((ts, cols), lambda i: (i, 0)),
                  pl.BlockSpec((HALO, cols), lambda i: (jnp.maximum(i * hb - 1, 0), 0)),
                  pl.BlockSpec(pool_wg.shape, lambda i: (0, 0, 0, 0)),
                  pl.BlockSpec((1, w), lambda i: (0, 0)),
                  pl.BlockSpec(conv_wg.shape, lambda i: (0, 0, 0)),
                  pl.BlockSpec((1, w), lambda i: (0, 0))],
        out_specs=pl.BlockSpec((2, ts, w), lambda i: (0, i, 0)),
        out_shape=jax.ShapeDtypeStruct((2, s, w), BF16),
        compiler_params=_params(("parallel",), (ts + HALO) * cols * 2 + 2 * ts * w * 2
                                + _nbytes(pool_wg.shape, BF16), 24 * (ts + HALO) * cg * 4),
    )(proj, proj, pool_wg, pool_scale, conv_wg, conv_b)


def _mix_bwd(proj, dys, pool_wg, pool_scale, conv_wg, conv_b, width):
    s = proj.shape[0]
    w = width
    cg = w // N_POOL_GROUPS
    ts = _pick(s, 128, HALO)
    hb = ts // HALO
    n_tiles = s // ts
    last_hb = s // HALO - 1
    cols = 6 * w

    def body(p_ref, ph_ref, pn_ref, dy_ref, dyn_ref, pw_ref, ps_ref, cw_ref, cb_ref,
             dp_ref, dpw_ref, dps_ref, dcw_ref, dcb_ref):
        i = pl.program_id(0)
        first = i == 0
        last = i == n_tiles - 1

        @pl.when(first)
        def _():
            dpw_ref[...] = jnp.zeros(dpw_ref.shape, F32)
            dps_ref[...] = jnp.zeros(dps_ref.shape, F32)
            dcw_ref[...] = jnp.zeros(dcw_ref.shape, F32)
            dcb_ref[...] = jnp.zeros(dcb_ref.shape, F32)

        row = i * ts + lax.broadcasted_iota(jnp.int32, (ts + HALO, 1), 0)
        t1_ext = (row + 1).astype(F32)
        t1 = t1_ext[:ts]

        def tile(part, g):
            lo = part * w + g * cg
            return p_ref[:, lo:lo + cg].astype(F32)

        def prev(part, g):
            lo = part * w + g * cg
            return jnp.where(first, 0.0, ph_ref[:, lo:lo + cg].astype(F32))

        def ahead(part, g):
            lo = part * w + g * cg
            return jnp.concatenate([tile(part, g), pn_ref[:, lo:lo + cg].astype(F32)], axis=0)

        def dy_ahead(n, g):
            gs = slice(g * cg, (g + 1) * cg)
            nxt = jnp.where(last, 0.0, dyn_ref[n, :, gs])
            return jnp.concatenate([dy_ref[n, :, gs], nxt], axis=0)

        for g, win in enumerate(POOL_WINDOWS):
            gs = slice(g * cg, (g + 1) * cg)
            u = tile(0, g)
            ext = jnp.concatenate([prev(0, g), u], axis=0)
            pooled = _window_sum(ext, win, True)[HALO:] / jnp.minimum(t1, float(win)) - u
            pooled_b = pooled.astype(BF16)
            pw = pw_ref[:, g].reshape(cg, cg)
            mixed = jnp.dot(pooled_b, pw, preferred_element_type=F32)
            zp_ext = ahead(1, g)
            dy0_ext = dy_ahead(0, g)
            scale = ps_ref[:, gs]
            sp_ext = _sigmoid(zp_ext)
            dms_ext = dy0_ext * (zp_ext * sp_ext)
            dmix_b = (dms_ext * scale).astype(BF16)
            dpooled_ext = lax.dot_general(dmix_b, pw, NT, preferred_element_type=F32)
            dy0 = dy0_ext[:ts]
            zp, sp = zp_ext[:ts], sp_ext[:ts]
            dsilu_p = sp * (1.0 + zp * (1.0 - sp))
            dp_ref[:, w + g * cg:w + (g + 1) * cg] = (dy0 * mixed * scale * dsilu_p).astype(BF16)
            dps_ref[:, gs] += jnp.sum(dms_ext[:ts] * mixed, axis=0, keepdims=True)
            dpw = lax.dot_general(pooled_b, dmix_b[:ts], TN, preferred_element_type=F32)
            dpw_ref[g // 2, :, g % 2] += dpw.reshape(N_CHIPS, cg // N_CHIPS, cg)
            q_ext = dpooled_ext / jnp.minimum(t1_ext, float(win))
            du = _window_sum(q_ext, win, False)[:ts] - dpooled_ext[:ts]
            dp_ref[:, gs] = du.astype(BF16)
            uc = tile(2, g)
            cc = tile(4, g)
            v = cc * uc
            vext = jnp.concatenate([prev(4, g) * prev(2, g), v], axis=0)
            v1 = _shift_rows(vext, 1, True)[HALO:]
            v2 = _shift_rows(vext, 2, True)[HALO:]
            cw = [cw_ref[g, tap:tap + 1, :] for tap in range(CONV_K)]
            y = cb_ref[:, gs] + cw[0] * v2 + cw[1] * v1 + cw[2] * v
            bc_ext = ahead(3, g)
            zc_ext = ahead(5, g)
            dy1_ext = dy_ahead(1, g)
            sc_ext = _sigmoid(zc_ext)
            silu_c_ext = zc_ext * sc_ext
            dyy_ext = dy1_ext * bc_ext * silu_c_ext
            dy1 = dy1_ext[:ts]
            bc = bc_ext[:ts]
            zc, sc = zc_ext[:ts], sc_ext[:ts]
            dsilu_c = sc * (1.0 + zc * (1.0 - sc))
            dp_ref[:, 3 * w + g * cg:3 * w + (g + 1) * cg] = (dy1 * y * silu_c_ext[:ts]).astype(BF16)
            dp_ref[:, 5 * w + g * cg:5 * w + (g + 1) * cg] = (dy1 * bc * y * dsilu_c).astype(BF16)
            dyy = dyy_ext[:ts]
            dcb_ref[:, gs] += jnp.sum(dyy, axis=0, keepdims=True)
            for tap, vt in enumerate((v2, v1, v)):
                dcw_ref[g, tap:tap + 1, :] += jnp.sum(dyy * vt, axis=0, keepdims=True)
            dv = (cw[2] * dyy + cw[1] * _shift_rows(dyy_ext, 1, False)[:ts]
                  + cw[0] * _shift_rows(dyy_ext, 2, False)[:ts])
            dp_ref[:, 4 * w + g * cg:4 * w + (g + 1) * cg] = (dv * uc).astype(BF16)
            dp_ref[:, 2 * w + g * cg:2 * w + (g + 1) * cg] = (dv * cc).astype(BF16)

    dpw_shape = (2, N_CHIPS, 2, cg // N_CHIPS, cg)
    return pl.pallas_call(
        body, name="mix_bwd", grid=(n_tiles,),
        in_specs=[pl.BlockSpec((ts, cols), lambda i: (i, 0)),
                  pl.BlockSpec((HALO, cols), lambda i: (jnp.maximum(i * hb - 1, 0), 0)),
                  pl.BlockSpec((HALO, cols), lambda i: (jnp.minimum((i + 1) * hb, last_hb), 0)),
                  pl.BlockSpec((2, ts, w), lambda i: (0, i, 0)),
                  pl.BlockSpec((2, HALO, w), lambda i: (0, jnp.minimum((i + 1) * hb, last_hb), 0)),
                  pl.BlockSpec(pool_wg.shape, lambda i: (0, 0, 0, 0)),
                  pl.BlockSpec((1, w), lambda i: (0, 0)),
                  pl.BlockSpec(conv_wg.shape, lambda i: (0, 0, 0)),
                  pl.BlockSpec((1, w), lambda i: (0, 0))],
        out_specs=[pl.BlockSpec((ts, cols), lambda i: (i, 0)),
                   pl.BlockSpec(dpw_shape, lambda i: (0, 0, 0, 0, 0)),
                   pl.BlockSpec((1, w), lambda i: (0, 0)),
                   pl.BlockSpec(conv_wg.shape, lambda i: (0, 0, 0)),
                   pl.BlockSpec((1, w), lambda i: (0, 0))],
        out_shape=[jax.ShapeDtypeStruct((s, cols), BF16), jax.ShapeDtypeStruct(dpw_shape, F32),
                   jax.ShapeDtypeStruct((1, w), F32), jax.ShapeDtypeStruct(conv_wg.shape, F32),
                   jax.ShapeDtypeStruct((1, w), F32)],
        compiler_params=_params(("arbitrary",), (2 * ts + 2 * HALO) * cols * 2 + (ts + HALO) * w * 8
                                + _nbytes(pool_wg.shape, BF16) + _nbytes(dpw_shape, F32),
                                40 * (ts + HALO) * cg * 4),
    )(proj, proj, proj, dys, dys, pool_wg, pool_scale, conv_wg, conv_b)


def _adamw(name, g, w, m, v):
    r, c = w.shape
    tr = _pick(r, max(8, (1 << 20) // (4 * c)), 8)

    def body(g_ref, w_ref, m_ref, v_ref, go_ref, d_ref, mo_ref, vo_ref):
        gv = g_ref[...]
        mn = ADAM_B1 * m_ref[...] + (1.0 - ADAM_B1) * gv
        vn = ADAM_B2 * v_ref[...] + (1.0 - ADAM_B2) * (gv * gv)
        m_hat = mn / (1.0 - ADAM_B1 ** ADAM_STEP)
        v_hat = vn / (1.0 - ADAM_B2 ** ADAM_STEP)
        go_ref[...] = gv
        d_ref[...] = -ADAM_LR * (m_hat / (jnp.sqrt(v_hat) + ADAM_EPS) + ADAM_WD * w_ref[...])
        mo_ref[...] = mn
        vo_ref[...] = vn

    blk = pl.BlockSpec((tr, c), lambda i: (i, 0))
    sh = jax.ShapeDtypeStruct((r, c), F32)
    return pl.pallas_call(
        body, name=name, grid=(r // tr,), in_specs=[blk] * 4, out_specs=[blk] * 4,
        out_shape=[sh] * 4, compiler_params=_params(("parallel",), tr * c * 32, 4 * tr * c * 4),
    )(g, w, m, v)


def _pair_add(name, g, r1, c_idx):
    _, r, c = g.shape
    tr = _pick(r, max(16, (2 << 20) // (2 * c)), 16)

    def body(c_ref, g_ref, r_ref, o_ref):
        o_ref[...] = (g_ref[...].astype(F32) + r_ref[...].astype(F32)).astype(BF16)

    return pl.pallas_call(
        body, name=name,
        grid_spec=pltpu.PrefetchScalarGridSpec(
            num_scalar_prefetch=1, grid=(r // tr,),
            in_specs=[pl.BlockSpec((None, tr, c), lambda i, cr: (cr[0], i, 0)),
                      pl.BlockSpec((tr, c), lambda i, cr: (i, 0))],
            out_specs=pl.BlockSpec((tr, c), lambda i, cr: (i, 0))),
        out_shape=jax.ShapeDtypeStruct((r, c), BF16),
        compiler_params=_params(("parallel",), tr * c * 6, 3 * tr * c * 4),
    )(c_idx, g, r1)


def _adamw_halves(name, g_own, g_other, c_idx, w, m, v):
    _, r, c = w.shape
    tr = _pick(r, max(8, (5 << 18) // (4 * c)), 8)

    def body(c_ref, go_ref, gt_ref, w_ref, m_ref, v_ref, g_out, d_ref, mo_ref, vo_ref):
        gv = jnp.where(pl.program_id(0) == c_ref[0], go_ref[...], gt_ref[...])
        mn = ADAM_B1 * m_ref[...] + (1.0 - ADAM_B1) * gv
        vn = ADAM_B2 * v_ref[...] + (1.0 - ADAM_B2) * (gv * gv)
        m_hat = mn / (1.0 - ADAM_B1 ** ADAM_STEP)
        v_hat = vn / (1.0 - ADAM_B2 ** ADAM_STEP)
        g_out[...] = gv
        d_ref[...] = -ADAM_LR * (m_hat / (jnp.sqrt(v_hat) + ADAM_EPS) + ADAM_WD * w_ref[...])
        mo_ref[...] = mn
        vo_ref[...] = vn

    blk = pl.BlockSpec((None, tr, c), lambda h, i, cr: (h, i, 0))
    sh = jax.ShapeDtypeStruct(w.shape, F32)
    return pl.pallas_call(
        body, name=name,
        grid_spec=pltpu.PrefetchScalarGridSpec(
            num_scalar_prefetch=1, grid=(2, r // tr),
            in_specs=[pl.BlockSpec((tr, c), lambda h, i, cr: (jnp.where(h == cr[0], i, 0), 0)),
                      pl.BlockSpec((tr, c), lambda h, i, cr: (jnp.where(h == cr[0], 0, i), 0)),
                      blk, blk, blk],
            out_specs=[blk] * 4),
        out_shape=[sh] * 4,
        compiler_params=_params(("arbitrary", "arbitrary"), tr * c * 36, 4 * tr * c * 4),
    )(c_idx, g_own, g_other, w, m, v)


def _sum_chips(name, part, r2, chip_idx, row0=0, prev=None):
    n, r, c = r2.shape
    rc = part.shape[1]
    tr = _pick(math.gcd(rc, row0) if row0 else rc, max(16, (1 << 20) // (2 * c)), 16)
    b0 = row0 // tr

    def body(ch_ref, own_ref, *rest):
        slots, o_ref = rest[:n], rest[-1]
        acc = None
        for s in range(n):
            term = jnp.where(ch_ref[0] == s, own_ref[...], slots[s][...]).astype(F32)
            acc = term if acc is None else acc + term
        o_ref[...] = acc

    def slot_spec(s):
        return pl.BlockSpec((None, tr, c), lambda i, ch: (jnp.where(ch[0] == s, (s + 1) % n, s), b0 + i, 0))

    extra = [] if prev is None else [prev]
    return pl.pallas_call(
        body, name=name,
        grid_spec=pltpu.PrefetchScalarGridSpec(
            num_scalar_prefetch=1, grid=(rc // tr,),
            in_specs=[pl.BlockSpec((None, tr, c), lambda i, ch: (ch[0], i, 0))]
            + [slot_spec(s) for s in range(n)] + [ANY] * len(extra),
            out_specs=pl.BlockSpec((tr, c), lambda i, ch: (b0 + i, 0))),
        out_shape=jax.ShapeDtypeStruct((r, c), F32),
        input_output_aliases={2 + n: 0} if extra else {},
        compiler_params=_params(("parallel",), tr * c * (2 * n + 6), 3 * tr * c * 4),
    )(chip_idx, part, *([r2] * n), *extra)


def _sum_devices(packs):
    n, r, c = packs.shape

    def body(p_ref, o_ref):
        acc = p_ref[0]
        for k in range(1, n):
            acc = acc + p_ref[k]
        o_ref[...] = acc

    return pl.pallas_call(
        body, name="sum_devices", out_shape=jax.ShapeDtypeStruct((r, c), F32),
        in_specs=[pl.BlockSpec(memory_space=pltpu.VMEM)],
        out_specs=pl.BlockSpec(memory_space=pltpu.VMEM),
    )(packs)


def _place():
    x, y, c = lax.axis_index("x"), lax.axis_index("y"), lax.axis_index("c")
    return x, y, c


def _chip_peers(x, y):
    out = []
    for k, (fx, fy) in enumerate(((0, 1), (1, 0), (1, 1))):
        px = 1 - x if fx else x
        py = 1 - y if fy else y
        out.append((k, px, py, 2 * px + py))
    return out


def _gather_weights(big, small, relay=False):
    nb, ns = len(big), len(small)

    class Copies:
        def __init__(self, c_in, c_out, sems):
            b_in, s_in = c_in[:nb], c_in[nb:]
            b_out, s_out = c_out[:nb], c_out[nb:]
            ici_s, ici_r, d2d_s, d2d_r, own_s, own_r = sems[:6]
            x, y, c = _place()
            chip = 2 * x + y
            sibling = (x, y, 1 - c)
            peers = _chip_peers(x, y)
            self.own = [pltpu.make_async_remote_copy(
                src_ref=b_in[t], dst_ref=b_out[t].at[chip], send_sem=own_s.at[t], recv_sem=own_r.at[t],
                device_id=sibling, device_id_type=MESH) for t in range(nb)]
            self.ici = [pltpu.make_async_remote_copy(
                src_ref=b_in[t].at[c], dst_ref=b_out[t].at[chip, c],
                send_sem=ici_s.at[t, k], recv_sem=ici_r.at[t, k],
                device_id=(px, py, c), device_id_type=MESH)
                for t in range(nb) for (k, px, py, pchip) in peers if not (relay and k == 2)]
            self.relay = []
            if relay:
                south = c == 0
                from_chip = jnp.where(south, 2 * x + (1 - y), 2 * (1 - x) + y)
                to = (jnp.where(south, 1 - x, x), jnp.where(south, y, 1 - y), c)
                self.relay = [pltpu.make_async_remote_copy(
                    src_ref=b_out[t].at[from_chip, c], dst_ref=b_out[t].at[from_chip, c],
                    send_sem=ici_s.at[t, 2], recv_sem=ici_r.at[t, 2],
                    device_id=to, device_id_type=MESH) for t in range(nb)]
            self.landed = [pltpu.make_async_remote_copy(
                src_ref=b_out[t].at[pchip, c], dst_ref=b_out[t].at[pchip, c],
                send_sem=ici_s.at[t, k], recv_sem=ici_r.at[t, k],
                device_id=sibling, device_id_type=MESH)
                for t in range(nb) for (k, px, py, pchip) in peers]
            self.passed = [pltpu.make_async_remote_copy(
                src_ref=b_out[t].at[pchip, c], dst_ref=b_out[t].at[pchip, c],
                send_sem=d2d_s.at[t, k], recv_sem=d2d_r.at[t, k],
                device_id=sibling, device_id_type=MESH)
                for t in range(nb) for (k, px, py, pchip) in peers]
            self.from_sibling = [pltpu.make_async_remote_copy(
                src_ref=b_out[t].at[pchip, 1 - c], dst_ref=b_out[t].at[pchip, 1 - c],
                send_sem=d2d_s.at[t, k], recv_sem=d2d_r.at[t, k],
                device_id=sibling, device_id_type=MESH)
                for t in range(nb) for (k, px, py, pchip) in peers]
            self.small, self.small_landed, self.local = [], [], []
            if ns:
                sm_s, sm_r, loc = sems[6:]
                self.local = [pltpu.make_async_copy(s_in[t], s_out[t].at[chip], loc.at[t]) for t in range(ns)]
                self.small = [pltpu.make_async_remote_copy(
                    src_ref=s_in[t], dst_ref=s_out[t].at[chip],
                    send_sem=sm_s.at[t, k], recv_sem=sm_r.at[t, k],
                    device_id=(px, py, c), device_id_type=MESH)
                    for t in range(ns) for (k, px, py, pchip) in peers]
                self.small_landed = [pltpu.make_async_remote_copy(
                    src_ref=s_in[t], dst_ref=s_out[t].at[pchip],
                    send_sem=sm_s.at[t, k], recv_sem=sm_r.at[t, k],
                    device_id=sibling, device_id_type=MESH)
                    for t in range(ns) for (k, px, py, pchip) in peers]

    def start(c_in, c_out, sems):
        cps = Copies(c_in, c_out, sems)
        for cp in cps.local + cps.own + cps.ici + cps.small:
            cp.start()

    def finish(c_in, c_out, sems):
        cps = Copies(c_in, c_out, sems)
        for t in range(nb):
            for k in range(3):
                if relay and k == 2:
                    cps.relay[t].start()
                cps.landed[3 * t + k].wait_recv()
                cps.passed[3 * t + k].start()
        for cp in cps.small_landed + cps.from_sibling:
            cp.wait_recv()
        for cp in cps.ici + cps.relay + cps.small + cps.passed:
            cp.wait_send()
        for cp in cps.own + cps.local:
            cp.wait()

    out_shape = [jax.ShapeDtypeStruct((N_CHIPS,) + b.shape, b.dtype) for b in big]
    out_shape += [jax.ShapeDtypeStruct((N_CHIPS,) + s.shape, s.dtype) for s in small]
    dma = pltpu.SemaphoreType.DMA
    scratch = [dma((nb, 3)), dma((nb, 3)), dma((nb, 3)), dma((nb, 3)), dma((nb,)), dma((nb,))]
    if ns:
        scratch += [dma((ns, 3)), dma((ns, 3)), dma((ns,))]
    ex = _Exchange(list(big) + list(small), out_shape, scratch, start, finish)
    ex.copies = Copies
    return ex


class _Exchange:
    def __init__(self, operands, out_shape, scratch, start, finish, aliases=None):
        self.operands, self.out_shape, self.scratch = operands, out_shape, scratch
        self.start, self.finish, self.aliases = start, finish, dict(aliases or {})


def _run_exchange(name, ex):
    n_i, n_o = len(ex.operands), len(ex.out_shape)

    def body(*refs):
        ins, outs, sems = refs[:n_i], refs[n_i:n_i + n_o], refs[n_i + n_o:]
        ex.start(ins, outs, sems)
        ex.finish(ins, outs, sems)

    return list(pl.pallas_call(
        body, name=name, out_shape=list(ex.out_shape), in_specs=[ANY] * n_i, out_specs=[ANY] * n_o,
        scratch_shapes=list(ex.scratch), input_output_aliases=ex.aliases,
    )(*ex.operands))


def _to_sibling(arrays, other_half):
    n = len(arrays)

    def copies(a_in, a_out, sems):
        send, recv = sems
        x, y, c = _place()
        return [pltpu.make_async_remote_copy(
            src_ref=a_in[t].at[1 - c] if other_half else a_in[t], dst_ref=a_out[t],
            send_sem=send.at[t], recv_sem=recv.at[t],
            device_id=(x, y, 1 - c), device_id_type=MESH) for t in range(n)]

    def start(a_in, a_out, sems):
        for cp in copies(a_in, a_out, sems):
            cp.start()

    def finish(a_in, a_out, sems):
        for cp in copies(a_in, a_out, sems):
            cp.wait()

    dma = pltpu.SemaphoreType.DMA
    shapes = [jax.ShapeDtypeStruct(a.shape[1:] if other_half else a.shape, a.dtype) for a in arrays]
    return _Exchange(list(arrays), shapes, [dma((n,)), dma((n,))], start, finish)


def _scatter_partials(parts, rows=None, row0=0, prev=None):
    n = len(parts)
    land = [(p.shape[0], p.shape[1] if rows is None else rows, p.shape[2]) for p in parts]

    def window(ref, slot, t):
        return ref.at[slot, pl.ds(row0, parts[t].shape[1])]

    def sends(p_in, r_out, sems):
        send, recv = sems
        x, y, c = _place()
        chip = 2 * x + y
        return [pltpu.make_async_remote_copy(
            src_ref=p_in[t].at[pchip], dst_ref=window(r_out[t], chip, t),
            send_sem=send.at[t, k], recv_sem=recv.at[t, k],
            device_id=(px, py, c), device_id_type=MESH)
            for t in range(n) for (k, px, py, pchip) in _chip_peers(x, y)]

    def start(p_in, r_out, sems):
        for cp in sends(p_in, r_out, sems):
            cp.start()

    def finish(p_in, r_out, sems):
        send, recv = sems
        x, y, c = _place()
        for t in range(n):
            for (k, px, py, pchip) in _chip_peers(x, y):
                pltpu.make_async_remote_copy(
                    src_ref=p_in[t].at[pchip], dst_ref=window(r_out[t], pchip, t),
                    send_sem=send.at[t, k], recv_sem=recv.at[t, k],
                    device_id=(px, py, c), device_id_type=MESH).wait_recv()
        for cp in sends(p_in, r_out, sems):
            cp.wait_send()

    dma = pltpu.SemaphoreType.DMA
    operands = list(parts) + (list(prev) if prev else [])
    return _Exchange(operands, [jax.ShapeDtypeStruct(sh, p.dtype) for sh, p in zip(land, parts)],
                     [dma((n, 3)), dma((n, 3))], start, finish,
                     aliases={n + t: t for t in range(n)} if prev else None)


def _w_in_grad_sibling(h, dpa, dgl, c_idx, carried, *, tm, tn, rows, pq):
    s, d = h.shape
    wa, wg = dpa.shape[1], dgl.shape[2]
    p = wa + 2 * wg
    ni, nj = rows // tm, p // tn
    na, qg, qp = wa // tn, wg // tn, pq // tn
    n_steps = ni * nj
    n_ci, n_co = len(carried.operands), len(carried.out_shape)

    def body(c_ref, h_ref, a_ref, g_ref, *rest):
        c_in = rest[:n_ci]
        r1_ref = rest[n_ci]
        c_out = rest[n_ci + 1:n_ci + 1 + n_co]
        slots, send, recv = rest[n_ci + 1 + n_co:n_ci + 4 + n_co]
        sems = rest[n_ci + 4 + n_co:]
        i, j = pl.program_id(0), pl.program_id(1)
        step = i * nj + j
        slot = lax.rem(step, 2)
        x, y, c = _place()
        sibling = (x, y, 1 - c)

        @pl.when(step == 0)
        def _():
            carried.start(c_in, c_out, sems)

        def tile_copy(sl):
            return pltpu.make_async_remote_copy(
                src_ref=slots.at[sl],
                dst_ref=r1_ref.at[j // qp, pl.ds(i * tm, tm), pl.ds((j % qp) * tn, tn)],
                send_sem=send.at[sl], recv_sem=recv, device_id=sibling, device_id_type=MESH)

        @pl.when(step >= 2)
        def _():
            tile_copy(slot).wait_send()

        def emit(b_ref):
            acc = lax.dot_general(h_ref[...], b_ref[...], TN, preferred_element_type=F32)
            slots[slot] = acc.astype(BF16)

        pl.when(j < na)(lambda: emit(a_ref))
        pl.when(j >= na)(lambda: emit(g_ref))
        tile_copy(slot).start()

        @pl.when(step == n_steps - 1)
        def _():
            tile_copy(0).wait_send()
            tile_copy(1).wait_send()
            pltpu.make_async_remote_copy(
                src_ref=r1_ref, dst_ref=r1_ref, send_sem=send.at[0], recv_sem=recv,
                device_id=sibling, device_id_type=MESH).wait_recv()
            carried.finish(c_in, c_out, sems)

    nbh = rows // tm
    dma = pltpu.SemaphoreType.DMA
    blk = s * tm * 2 + 2 * s * tn * 2
    res = pl.pallas_call(
        body, name="w_in_grad_sibling",
        grid_spec=pltpu.PrefetchScalarGridSpec(
            num_scalar_prefetch=1, grid=(ni, nj),
            in_specs=[pl.BlockSpec((s, tm), lambda i, j, cr: (0, (1 - cr[0]) * nbh + i)),
                      pl.BlockSpec((s, tn), lambda i, j, cr: (0, jnp.minimum(j, na - 1))),
                      pl.BlockSpec((None, s, tn),
                                   lambda i, j, cr: (jnp.maximum(j - na, 0) // qg, 0, jnp.maximum(j - na, 0) % qg))]
            + [ANY] * n_ci,
            out_specs=[ANY] * (1 + n_co),
            scratch_shapes=[pltpu.VMEM((2, tm, tn), BF16), dma((2,)), dma] + list(carried.scratch)),
        out_shape=[jax.ShapeDtypeStruct((N_CHIPS, rows, pq), BF16)] + list(carried.out_shape),
        compiler_params=_params(("arbitrary", "arbitrary"), blk, 2 * tm * tn * 2 + 2 * tm * tn * 4),
    )(c_idx, h, dpa, dgl, *carried.operands)
    return res[0], res[1:]


def _proj_gathering(h, wb, chip_idx, gather_rest, *, tm, tn):
    s, d = h.shape
    _, hh, pq = wb.shape
    qp, ni = pq // tn, s // tm
    n_tiles = N_CHIPS * qp
    n_ci, n_co, n_cs = len(gather_rest.operands), len(gather_rest.out_shape), len(gather_rest.scratch)

    def where_of(t):
        near = t - qp
        g = jnp.where(t < qp, 0, jnp.where(t < 3 * qp, 1 + lax.rem(near, 2), 3))
        jj = jnp.where(t < qp, t, jnp.where(t < 3 * qp, near // 2, t - 3 * qp))
        return g, jj

    def body(ch_ref, h_ref, own_ref, *rest):
        c_in = rest[:n_ci]
        o_ref, gathered = rest[n_ci], rest[n_ci + 1]
        c_out = rest[n_ci + 2:n_ci + 2 + n_co]
        slots, fetch_sem, ici_s, ici_r, d2d_s, d2d_r, own_s, own_r = rest[n_ci + 2 + n_co:n_ci + 10 + n_co]
        c_sem = rest[n_ci + 10 + n_co:]
        tile, i = pl.program_id(0), pl.program_id(1)
        x, y, c = _place()
        chip = 2 * x + y
        sibling = (x, y, 1 - c)

        def cols(jj):
            return pl.ds(jj * tn, tn)

        def remote(src, dst, send, recv, to):
            return pltpu.make_async_remote_copy(src_ref=src, dst_ref=dst, send_sem=send, recv_sem=recv,
                                                device_id=to, device_id_type=MESH)

        def own_shard():
            return remote(own_ref, gathered.at[chip], own_s, own_r, sibling)

        def to_neighbour(k, jj):
            to = (x, 1 - y, c) if k == 0 else (1 - x, y, c)
            return remote(own_ref.at[c, :, cols(jj)], gathered.at[chip, c, :, cols(jj)],
                          ici_s.at[k, jj], ici_r.at[k, jj], to)

        def relay(jj):
            south = c == 0
            from_chip = jnp.where(south, 2 * x + (1 - y), 2 * (1 - x) + y)
            to = (jnp.where(south, 1 - x, x), jnp.where(south, y, 1 - y), c)
            chunk = gathered.at[from_chip, c, :, cols(jj)]
            return remote(chunk, chunk, ici_s.at[2, jj], ici_r.at[2, jj], to)

        def chunk_of(k, jj, half):
            return gathered.at[jnp.bitwise_xor(chip, k + 1), half, :, cols(jj)]

        def landed(k, jj):
            return remote(chunk_of(k, jj, c), chunk_of(k, jj, c), ici_s.at[k, jj], ici_r.at[k, jj], sibling)

        def passed(k, jj):
            return remote(chunk_of(k, jj, c), chunk_of(k, jj, c), d2d_s.at[k, jj], d2d_r.at[k, jj], sibling)

        def from_sibling(k, jj):
            return remote(chunk_of(k, jj, 1 - c), chunk_of(k, jj, 1 - c), d2d_s.at[k, jj], d2d_r.at[k, jj], sibling)

        def fetch(src, slot):
            return pltpu.make_async_copy(src, slots.at[slot], fetch_sem.at[slot])

        @pl.when(jnp.logical_and(tile == 0, i == 0))
        def _():
            own_shard().start()
            for jj in range(qp):
                for k in range(2):
                    to_neighbour(k, jj).start()
            gather_rest.start(c_in, c_out, c_sem)
            fetch(own_ref.at[:, :, cols(0)], 0).start()

        @pl.when(i == 0)
        def _():
            fetch(own_ref.at[:, :, cols(0)], lax.rem(tile, 2)).wait()

        last_row = i == ni - 1
        nxt, ahead = tile + 1, tile + 2
        g1, j1 = where_of(nxt)
        g2, j2 = where_of(ahead)

        def pass_on(t, g, jj):
            @pl.when(jnp.logical_and(last_row, jnp.logical_and(t >= qp, t < n_tiles)))
            def _():
                landed(g - 1, jj).wait_recv()
                passed(g - 1, jj).start()

            @pl.when(jnp.logical_and(last_row, jnp.logical_and(g == 2, t < n_tiles)))
            def _():
                relay(jj).start()

        pass_on(ahead, g2, j2)

        @pl.when(jnp.logical_and(last_row, nxt < qp))
        def _():
            fetch(own_ref.at[:, :, cols(nxt)], lax.rem(nxt, 2)).start()

        @pl.when(jnp.logical_and(last_row, jnp.logical_and(nxt >= qp, nxt < n_tiles)))
        def _():
            from_sibling(g1 - 1, j1).wait_recv()
            fetch(gathered.at[jnp.bitwise_xor(chip, g1), :, :, cols(j1)], lax.rem(nxt, 2)).start()

        w_tile = slots[lax.rem(tile, 2)].reshape(d, tn)
        o_ref[...] = jnp.dot(h_ref[...], w_tile, preferred_element_type=F32).astype(BF16)

        @pl.when(jnp.logical_and(tile == n_tiles - 1, last_row))
        def _():
            for jj in range(qp):
                for k in range(2):
                    to_neighbour(k, jj).wait_send()
                relay(jj).wait_send()
                for k in range(3):
                    passed(k, jj).wait_send()
            own_shard().wait()
            gather_rest.finish(c_in, c_out, c_sem)

    assert qp >= 2 and not gather_rest.aliases
    dma = pltpu.SemaphoreType.DMA
    p = N_CHIPS * pq

    def out_cols(t, i, ch):
        g, jj = where_of(t)
        return (i, jnp.bitwise_xor(ch[0], g) * qp + jj)

    res = pl.pallas_call(
        body, name="proj_gathering",
        grid_spec=pltpu.PrefetchScalarGridSpec(
            num_scalar_prefetch=1, grid=(n_tiles, ni),
            in_specs=[pl.BlockSpec((tm, d), lambda t, i, ch: (i, 0)), ANY] + [ANY] * n_ci,
            out_specs=[pl.BlockSpec((tm, tn), out_cols), ANY] + [ANY] * n_co,
            scratch_shapes=[pltpu.VMEM((2, 2, hh, tn), BF16), dma((2,)),
                            dma((3, qp)), dma((3, qp)), dma((3, qp)), dma((3, qp)), dma, dma]
            + list(gather_rest.scratch)),
        out_shape=[jax.ShapeDtypeStruct((s, p), BF16), jax.ShapeDtypeStruct((N_CHIPS,) + wb.shape, wb.dtype)]
        + list(gather_rest.out_shape),
        compiler_params=_params(("arbitrary",) * 2, tm * d * 2 + tm * tn * 2, 3 * d * tn * 2 + tm * tn * 4),
    )(chip_idx, h, wb, *gather_rest.operands)
    return res[0], res[1], res[2:]


def _gather_packs(pack):
    def body(p_ref, o_ref, send, recv, loc):
        x, y, c = _place()
        me = 4 * x + 2 * y + c
        mine = pltpu.make_async_copy(p_ref, o_ref.at[me], loc)
        mine.start()
        flips = [(fx, fy, fc) for fx in (0, 1) for fy in (0, 1) for fc in (0, 1)][1:]
        cps = []
        for k, (fx, fy, fc) in enumerate(flips):
            peer = (1 - x if fx else x, 1 - y if fy else y, 1 - c if fc else c)
            cps.append(pltpu.make_async_remote_copy(
                src_ref=p_ref, dst_ref=o_ref.at[me], send_sem=send.at[k], recv_sem=recv.at[k],
                device_id=peer, device_id_type=MESH))
        for cp in cps:
            cp.start()
        for k, (fx, fy, fc) in enumerate(flips):
            px, py, pc = (1 - x if fx else x, 1 - y if fy else y, 1 - c if fc else c)
            pltpu.make_async_remote_copy(
                src_ref=p_ref, dst_ref=o_ref.at[4 * px + 2 * py + pc],
                send_sem=send.at[k], recv_sem=recv.at[k],
                device_id=(px, py, pc), device_id_type=MESH).wait_recv()
        for cp in cps:
            cp.wait_send()
        mine.wait()

    dma = pltpu.SemaphoreType.DMA
    return pl.pallas_call(
        body, name="gather_packs", out_shape=jax.ShapeDtypeStruct((N_DEV,) + pack.shape, pack.dtype),
        in_specs=[ANY], out_specs=ANY, scratch_shapes=[dma((7,)), dma((7,)), dma],
    )(pack)


def _flat_pack(pieces):
    flat = jnp.concatenate([p.reshape(-1) for p in pieces])
    pad = (-flat.shape[0]) % (8 * LANES)
    flat = jnp.concatenate([flat, jnp.zeros((pad,), F32)])
    return flat.reshape(-1, LANES)


def _unpack(pack, shapes):
    flat = pack.reshape(-1)
    out, off = [], 0
    for sh in shapes:
        n = 1
        for s in sh:
            n *= s
        out.append(flat[off:off + n].reshape(sh))
        off += n
    return out


def kernel(x, norm_w, w_in, pool_w, pool_scale, conv_w, conv_b, gate_b, w_branch, w_out, final_norm_w, loss_target, m_norm_w, m_w_in, m_pool_w, m_pool_scale, m_conv_w, m_conv_b, m_gate_b, m_w_branch, m_w_out, m_final_norm_w, v_norm_w, v_w_in, v_pool_w, v_pool_scale, v_conv_w, v_conv_b, v_gate_b, v_w_branch, v_w_out, v_final_norm_w):
    _, s, d = x.shape
    w = d // 2
    cg = w // N_POOL_GROUPS
    p = 6 * w + 2 * d
    pq = p // N_CHIPS
    dq = d // N_CHIPS
    assert w_in.shape == (1, d, pq) and w_branch.shape == (1, 2, w, dq) and w_out.shape == (1, dq, d)
    assert pool_w.shape == (1, N_POOL_GROUPS, cg // N_CHIPS, cg) and conv_w.shape == (1, CONV_K, cg)

    x2d = x.reshape(s, d)
    tgt = loss_target.reshape(s, d)
    c_idx = lax.axis_index("c").astype(jnp.int32).reshape(1)
    chip = 2 * lax.axis_index("x") + lax.axis_index("y")

    big_w = [w_in.reshape(d, pq), w_out.reshape(dq, d), w_branch.reshape(2 * w, dq),
             pool_w.reshape(cg, cg)]
    names = ["in", "out", "branch", "pool"]
    big_b = [_cast_bf16("cast_" + nm, a) for nm, a in zip(names, big_w)]
    big_b = [b.reshape(2, b.shape[0] // 2, b.shape[1]) for b in big_b]
    chip_idx = chip.astype(jnp.int32).reshape(1)

    h = _rms_fwd(x2d, norm_w)

    tm = _pick(s, 1024, 16)
    tn_p = _pick(math.gcd(pq, 6 * w, d), 1024, LANES)
    qp = pq // tn_p
    tk_d = _pick(d, 4096, LANES)
    proj, wg_in, (wg_br, wg_pool, wg_cw, wg_gb) = _proj_gathering(
        h, big_b[0], chip_idx,
        _gather_weights(big_b[2:], [conv_w.reshape(CONV_K, cg), gate_b.reshape(2, dq)]),
        tm=tm, tn=tn_p)
    wg_in = wg_in.reshape(N_CHIPS, d, pq)
    wg_pool = wg_pool.reshape(N_CHIPS, N_POOL_GROUPS, cg // N_CHIPS, cg)

    ys = _mix_fwd(proj, wg_pool, pool_scale, wg_cw, conv_b, w)

    tn_d = _pick(dq, 1024, LANES)
    qd = dq // tn_d
    tk_w = _pick(w, 2048, LANES)
    tm_g = _pick(s, 512, 16)
    gl0 = (6 * w) // tn_d
    gl1 = (6 * w + d) // tn_d

    def gate_specs(im, rows):
        return [pl.BlockSpec((rows, tn_d), lambda *a: (im(*a)[0], gl0 + im(*a)[1])),
                pl.BlockSpec((rows, tn_d), lambda *a: (im(*a)[0], gl1 + im(*a)[1])),
                pl.BlockSpec((None, 2, tn_d), lambda *a: (im(*a)[1] // qd, 0, im(*a)[1] % qd))]

    def merge_epilogue(accs, ins, outs, ids):
        gb = ins[6][...]
        g0 = _sigmoid(ins[4][...].astype(F32) + gb[0:1])
        g1 = _sigmoid(ins[5][...].astype(F32) + gb[1:2])
        outs[0][...] = (g0 * accs[0] + g1 * accs[1]).astype(BF16)
        outs[1][0] = accs[0].astype(BF16)
        outs[1][1] = accs[1].astype(BF16)

    (merged, br), (wg_out,) = _mm(
        "branch_merge", grid=(s // tm_g, d // tn_d, w // tk_w), comm=_gather_weights(big_b[1:2], []),
        operands=[ys, wg_br, ys, wg_br, proj, proj, wg_gb],
        in_specs=[pl.BlockSpec((None, tm_g, tk_w), lambda i, j, k: (0, i, k)),
                  pl.BlockSpec((None, None, tk_w, tn_d), lambda i, j, k: (j // qd, 0, k, j % qd)),
                  pl.BlockSpec((None, tm_g, tk_w), lambda i, j, k: (1, i, k)),
                  pl.BlockSpec((None, None, tk_w, tn_d), lambda i, j, k: (j // qd, 1, k, j % qd)),
                  *gate_specs(lambda i, j, k: (i, j), tm_g)],
        out_shape=[jax.ShapeDtypeStruct((s, d), BF16), jax.ShapeDtypeStruct((2, s, d), BF16)],
        out_specs=[pl.BlockSpec((tm_g, tn_d), lambda i, j, k: (i, j)),
                   pl.BlockSpec((2, tm_g, tn_d), lambda i, j, k: (0, i, j))],
        pairs=[(0, 1, 0, None), (2, 3, 1, None)], dims=NN, acc_shapes=[(tm_g, tn_d)] * 2,
        epilogue=merge_epilogue, temp_bytes=6 * tm_g * tn_d * 4,
    )
    wg_out = wg_out.reshape(d, d)

    tn_f = _pick(d, 1024, LANES)
    o = _mm(
        "out_proj", grid=(s // tm, d // tn_f, d // tk_d), operands=[merged, wg_out],
        in_specs=[pl.BlockSpec((tm, tk_d), lambda i, j, k: (i, k)),
                  pl.BlockSpec((tk_d, tn_f), lambda i, j, k: (k, j))],
        out_shape=[jax.ShapeDtypeStruct((s, d), F32)],
        out_specs=[pl.BlockSpec((tm, tn_f), lambda i, j, k: (i, j))],
        pairs=[(0, 1, 0, None)], dims=NN, acc_shapes=[(tm, tn_f)],
        epilogue=lambda accs, ins, outs, ids: outs[0].__setitem__(Ellipsis, accs[0]),
    )[0][0]

    dx2, dx2b, loss_part, g_fnw =_head(x2d, o, tgt, final_norm_w.reshape(1, d))

    def gate_bwd_epilogue(accs, ins, outs, ids):
        dm = accs[0]
        gb = ins[5][...]
        i = ids[1]

        @pl.when(i == 0)
        def _():
            outs[2][...] = jnp.zeros(outs[2].shape, F32)

        for n in range(2):
            gate = _sigmoid(ins[3 + n][...].astype(F32) + gb[n:n + 1])
            outs[0][n] = (dm * gate).astype(BF16)
            dgl = dm * ins[2][n].astype(F32) * gate * (1.0 - gate)
            outs[1][n] = dgl.astype(BF16)
            outs[2][n:n + 1, :] += jnp.sum(dgl, axis=0, keepdims=True)

    tm_b = _pick(s, 256, 16)
    (d_br, dgl, g_gb), _ = _mm(
        "out_proj_bwd_gate", grid=(d // tn_d, s // tm_b, d // tk_d),
        operands=[dx2b, wg_out, br, proj, proj, wg_gb],
        in_specs=[pl.BlockSpec((tm_b, tk_d), lambda j, i, k: (i, k)),
                  pl.BlockSpec((tn_d, tk_d), lambda j, i, k: (j, k)),
                  pl.BlockSpec((2, tm_b, tn_d), lambda j, i, k: (0, i, j)),
                  *gate_specs(lambda j, i, k: (i, j), tm_b)],
        out_shape=[jax.ShapeDtypeStruct((2, s, d), BF16), jax.ShapeDtypeStruct((2, s, d), BF16),
                   jax.ShapeDtypeStruct((2, d), F32)],
        out_specs=[pl.BlockSpec((2, tm_b, tn_d), lambda j, i, k: (0, i, j)),
                   pl.BlockSpec((2, tm_b, tn_d), lambda j, i, k: (0, i, j)),
                   pl.BlockSpec((2, tn_d), lambda j, i, k: (0, j))],
        pairs=[(0, 1, 0, None)], dims=NT, acc_shapes=[(tm_b, tn_d)],
        epilogue=gate_bwd_epilogue, semantics=("parallel", "arbitrary", "arbitrary"),
        temp_bytes=8 * tm_b * tn_d * 4,
    )

    hh_out = d // 8
    tm_o = _pick(hh_out, 512, LANES)
    nb_o = hh_out // tm_o
    tk_s = _pick(s, 4096, LANES)
    g_out = _mm(
        "w_out_grad", grid=(d // tm_o, d // tn_f, s // tk_s), operands=[merged, dx2b],
        in_specs=[pl.BlockSpec((tk_s, tm_o), lambda i, j, k: (k, i)),
                  pl.BlockSpec((tk_s, tn_f), lambda i, j, k: (k, j))],
        out_shape=[jax.ShapeDtypeStruct((2, N_CHIPS, hh_out, d), BF16)],
        out_specs=[pl.BlockSpec((None, None, tm_o, tn_f),
                                lambda i, j, k: ((i // nb_o) % 2, i // (2 * nb_o), i % nb_o, j))],
        pairs=[(0, 1, 0, None)], dims=TN, acc_shapes=[(tm_o, tn_f)],
        epilogue=lambda accs, ins, outs, ids: outs[0].__setitem__(Ellipsis, accs[0].astype(BF16)),
    )[0][0]

    tn_w = _pick(w, 2048, LANES)
    tk_q = _pick(dq, 1024, LANES)
    qk = dq // tk_q
    (dys,), from_sibling_out = _mm(
        "branch_bwd", grid=(2, s // tm, w // tn_w, d // tk_q), operands=[d_br, wg_br],
        comm=_to_sibling([g_out], other_half=True),
        in_specs=[pl.BlockSpec((None, tm, tk_q), lambda n, i, j, k: (n, i, k)),
                  pl.BlockSpec((None, None, tn_w, tk_q), lambda n, i, j, k: (k // qk, n, j, k % qk))],
        out_shape=[jax.ShapeDtypeStruct((2, s, w), F32)],
        out_specs=[pl.BlockSpec((None, tm, tn_w), lambda n, i, j, k: (n, i, j))],
        pairs=[(0, 1, 0, None)], dims=NT, acc_shapes=[(tm, tn_w)],
        epilogue=lambda accs, ins, outs, ids: outs[0].__setitem__(Ellipsis, accs[0]),
    )

    tm_w = _pick(w, 1024, LANES)
    g_br = _mm(
        "w_branch_grad", grid=(2, w // tm_w, d // tn_d, s // tk_s), operands=[ys, d_br],
        in_specs=[pl.BlockSpec((None, tk_s, tm_w), lambda n, i, j, k: (n, k, i)),
                  pl.BlockSpec((None, tk_s, tn_d), lambda n, i, j, k: (n, k, j))],
        out_shape=[jax.ShapeDtypeStruct((2, N_CHIPS, w, dq), BF16)],
        out_specs=[pl.BlockSpec((None, None, tm_w, tn_d), lambda n, i, j, k: (n, j // qd, i, j % qd))],
        pairs=[(0, 1, 0, None)], dims=TN, acc_shapes=[(tm_w, tn_d)],
        epilogue=lambda accs, ins, outs, ids: outs[0].__setitem__(Ellipsis, accs[0].astype(BF16)),
    )[0][0]

    dpa, g_pool, g_ps, g_cw, g_cb = _mix_bwd(proj, dys, wg_pool, pool_scale, wg_cw, conv_b, w)
    g_pool = g_pool.astype(BF16).reshape(2, N_CHIPS, 2 * (cg // N_CHIPS), cg)

    early = [g_out, g_br, g_pool]
    from_sibling_early = list(from_sibling_out) + _run_exchange(
        "pair_exchange", _to_sibling(early[1:], other_half=True))
    parts_early = []
    for nm, g, r1 in zip(names[1:], early, from_sibling_early):
        _, nc, hh, wd = g.shape
        pt = _pair_add("pair_add_" + nm, g.reshape(2, nc * hh, wd), r1.reshape(nc * hh, wd), c_idx)
        parts_early.append(pt.reshape(nc, hh, wd))

    na = (6 * w) // tn_p
    qg = d // tn_p
    hh_in = d // 2
    assert s == tk_s
    tm_i = _pick(hh_in, 1024, LANES)
    tn_i = _pick(tn_p, 512, LANES)
    from_sibling, recv_early = _w_in_grad_sibling(
        h, dpa, dgl, c_idx, _scatter_partials(parts_early), tm=tm_i, tn=tn_i, rows=hh_in, pq=pq)

    tm_o2 = _pick(hh_in, 512, LANES)
    nb_o2 = hh_in // tm_o2

    def own_rows(name, first, count, comm):
        def add_sibling(accs, ins, outs, ids):
            outs[0][...] = (accs[0] + ins[3][...].astype(F32)).astype(BF16)

        return _mm(
            name, grid=(p // tn_p, count, 1), operands=[h, dpa, dgl, from_sibling], comm=comm, prefetch=c_idx,
            in_specs=[pl.BlockSpec((tk_s, tm_o2), lambda j, i, k, cr: (0, cr[0] * nb_o2 + first + i)),
                      pl.BlockSpec((tk_s, tn_p), lambda j, i, k, cr: (0, jnp.minimum(j, na - 1))),
                      pl.BlockSpec((None, tk_s, tn_p),
                                   lambda j, i, k, cr: (jnp.maximum(j - na, 0) // qg, 0, jnp.maximum(j - na, 0) % qg)),
                      pl.BlockSpec((None, tm_o2, tn_p), lambda j, i, k, cr: (j // qp, first + i, j % qp))],
            out_shape=[jax.ShapeDtypeStruct((N_CHIPS, count * tm_o2, pq), BF16)],
            out_specs=[pl.BlockSpec((None, tm_o2, tn_p), lambda j, i, k, cr: (j // qp, i, j % qp))],
            pairs=[(0, 1, 0, lambda ids: ids[0] < na), (0, 2, 0, lambda ids: ids[0] >= na)],
            dims=TN, acc_shapes=[(tm_o2, tn_p)], epilogue=add_sibling)

    halves_early = [_sum_chips("sum_chips_" + nm, pt, r2, chip_idx)
                    for nm, pt, r2 in zip(names[1:], parts_early, recv_early)]
    share_early = _to_sibling(halves_early, other_half=False)

    n_first = 1 if nb_o2 > 1 else 0
    if n_first:
        (part_a,), others_early = own_rows("w_in_grad_own_a", 0, n_first, share_early)
        scatter_a = _scatter_partials([part_a], rows=hh_in)
        (part_b,), land_a = own_rows("w_in_grad_own_b", n_first, nb_o2 - n_first, scatter_a)
        scatter_b = _scatter_partials([part_b], rows=hh_in, row0=n_first * tm_o2, prev=land_a)
    else:
        (part_b,), others_early = own_rows("w_in_grad_own", 0, nb_o2, share_early)
        scatter_b = _scatter_partials([part_b])

    tn_h = _pick(d, 2048, LANES)
    (dh,), recv_in = _mm(
        "proj_bwd", grid=(s // tm, d // tn_h, p // tn_p), operands=[dpa, dgl, wg_in],
        comm=scatter_b,
        in_specs=[pl.BlockSpec((tm, tn_p), lambda i, j, k: (i, jnp.minimum(k, na - 1))),
                  pl.BlockSpec((None, tm, tn_p),
                               lambda i, j, k: (jnp.maximum(k - na, 0) // qg, i, jnp.maximum(k - na, 0) % qg)),
                  pl.BlockSpec((None, tn_h, tn_p), lambda i, j, k: (k // qp, j, k % qp))],
        out_shape=[jax.ShapeDtypeStruct((s, d), F32)],
        out_specs=[pl.BlockSpec((tm, tn_h), lambda i, j, k: (i, j))],
        pairs=[(0, 2, 0, lambda ids: ids[2] < na), (1, 2, 0, lambda ids: ids[2] >= na)],
        dims=NT, acc_shapes=[(tm, tn_h)],
        epilogue=lambda accs, ins, outs, ids: outs[0].__setitem__(Ellipsis, accs[0]),
    )

    grad_x, g_nw = _rms_bwd(x2d, dh, dx2, norm_w)

    if n_first:
        half_in = _sum_chips("sum_chips_in_a", part_a, recv_in[0], chip_idx)
        half_in = _sum_chips("sum_chips_in_b", part_b, recv_in[0], chip_idx, row0=n_first * tm_o2, prev=half_in)
    else:
        half_in = _sum_chips("sum_chips_in", part_b, recv_in[0], chip_idx)
    halves = [half_in] + halves_early
    others = _run_exchange("pair_share", _to_sibling([half_in], other_half=False)) + list(others_early)

    g_cw_full = jnp.transpose(g_cw, (1, 0, 2)).reshape(CONV_K, w)
    small_shapes = [(LANES,), (1, d), (1, d), (1, w), (1, w), (CONV_K, w), (2, d)]
    pack = _flat_pack([loss_part[0], g_nw, g_fnw, g_ps, g_cb, g_cw_full, g_gb])
    total = _sum_devices(_gather_packs(pack))
    t_loss, t_nw, t_fnw, t_ps, t_cb, t_cw, t_gb = _unpack(total, small_shapes)
    loss = t_loss[0]
    t_cw = lax.dynamic_slice_in_dim(t_cw, chip * cg, cg, axis=1)
    t_gb = lax.dynamic_slice_in_dim(t_gb, chip * dq, dq, axis=1)

    out = {}
    big_names = ["w_in", "w_out", "w_branch", "pool_w"]
    big_m = [m_w_in, m_w_out, m_w_branch, m_pool_w]
    big_v = [v_w_in, v_w_out, v_w_branch, v_pool_w]
    big_orig = [w_in, w_out, w_branch, pool_w]
    for nm, g_own, g_other, w2, mm_, vv_, orig in zip(big_names, halves, others, big_w, big_m, big_v, big_orig):
        sh = (2,) + g_own.shape
        res = _adamw_halves("adamw_" + nm, g_own, g_other, c_idx,
                            w2.reshape(sh), mm_.reshape(sh), vv_.reshape(sh))
        out[nm] = [r.reshape(orig.shape) for r in res]

    sm_names = ["norm_w", "final_norm_w", "pool_scale", "conv_b", "conv_w", "gate_b"]
    sm_g = [t_nw, t_fnw, t_ps, t_cb, t_cw, t_gb]
    sm_w = [norm_w, final_norm_w, pool_scale, conv_b, conv_w, gate_b]
    sm_m = [m_norm_w, m_final_norm_w, m_pool_scale, m_conv_b, m_conv_w, m_gate_b]
    sm_v = [v_norm_w, v_final_norm_w, v_pool_scale, v_conv_b, v_conv_w, v_gate_b]
    sm_shapes = [a.shape for a in sm_w]
    res = _adamw("adamw_small", _flat_pack(sm_g), _flat_pack(sm_w), _flat_pack(sm_m), _flat_pack(sm_v))
    res = [_unpack(r, sm_shapes) for r in res]
    for idx, nm in enumerate(sm_names):
        out[nm] = [r[idx] for r in res]

    order = ["norm_w", "w_in", "pool_w", "pool_scale", "conv_w", "conv_b", "gate_b", "w_branch", "w_out",
             "final_norm_w"]
    outs = [loss, grad_x.reshape(x.shape)]
    for kind in range(4):
        outs += [out[nm][kind] for nm in order]
    return tuple(outs)
```

```python
import math

import jax
import jax.numpy as jnp
from jax import lax
from jax.experimental import pallas as pl
from jax.experimental.pallas import tpu as pltpu

F32 = jnp.float32
BF16 = jnp.bfloat16

NORM_EPS = 1e-6
POOL_WINDOWS = (2, 4, 8, 16)
N_POOL_GROUPS = len(POOL_WINDOWS)
CONV_K = 3
ADAM_LR = 0.001
ADAM_B1 = 0.9
ADAM_B2 = 0.999
ADAM_EPS = 1e-08
ADAM_WD = 0.01
ADAM_STEP = 10

N_CHIPS = 4
N_DEV = 8
HALO = 16
LANES = 128
V7X_VMEM_BYTES = 64 * 1024 * 1024
VMEM_CAP = V7X_VMEM_BYTES - 8 * 1024 * 1024

MESH = pl.DeviceIdType.MESH
ANY = pl.BlockSpec(memory_space=pl.ANY)

NN = (((1,), (0,)), ((), ()))
NT = (((1,), (1,)), ((), ()))
TN = (((0,), (0,)), ((), ()))


def _pick(dim, pref, align):
    if dim <= pref:
        return dim
    t = (pref // align) * align
    while t >= align:
        if dim % t == 0:
            return t
        t -= align
    raise ValueError(f"no tile for {dim} (pref {pref}, align {align})")


def _nbytes(shape, dtype):
    n = 1
    for s in shape:
        if s is not None:
            n *= s
    return n * jnp.dtype(dtype).itemsize


def _params(semantics, block_bytes, extra_bytes=0):
    need = 2 * block_bytes + extra_bytes + (2 << 20)
    return pltpu.CompilerParams(dimension_semantics=semantics,
                                vmem_limit_bytes=int(min(max(need, 16 << 20), VMEM_CAP)))


def _sigmoid(z):
    return jax.nn.sigmoid(z)


def _mm(name, *, grid, operands, in_specs, out_shape, out_specs, pairs, dims, acc_shapes, epilogue,
        semantics=None, temp_bytes=0, comm=None, prefetch=None, aliases=None):
    aliases = dict(aliases or {})
    n_in, n_out = len(operands), len(out_shape)
    kax = len(grid) - 1
    nk = grid[kax]
    single = nk == 1
    conditional = any(p[3] is not None for p in pairs)
    if single and conditional:
        assert len(acc_shapes) == 1 and all(p[3] is not None for p in pairs)
    n_acc = 0 if single else len(acc_shapes)
    n_pf = 0 if prefetch is None else 1
    c_ops = list(comm.operands) if comm else []
    c_out = list(comm.out_shape) if comm else []
    c_sems = list(comm.scratch) if comm else []
    c_alias = dict(comm.aliases) if comm else {}
    n_ci, n_co = len(c_ops), len(c_out)

    def product(ins, ai, bi):
        return lax.dot_general(ins[ai][...], ins[bi][...], dims, preferred_element_type=F32)

    def body(*refs):
        refs = refs[n_pf:]
        ins = refs[:n_in]
        c_in_refs = refs[n_in:n_in + n_ci]
        outs = refs[n_in + n_ci:n_in + n_ci + n_out]
        c_out_refs = refs[n_in + n_ci + n_out:n_in + n_ci + n_out + n_co]
        scratch = refs[n_in + n_ci + n_out + n_co:]
        accs = scratch[:n_acc]
        sems = scratch[n_acc:]
        ids = [pl.program_id(a) for a in range(len(grid))]
        k = ids[kax]

        if comm:
            is_first = ids[0] == 0
            is_last = ids[0] == grid[0] - 1
            for a in range(1, len(grid)):
                is_first = jnp.logical_and(is_first, ids[a] == 0)
                is_last = jnp.logical_and(is_last, ids[a] == grid[a] - 1)

            @pl.when(is_first)
            def _():
                comm.start(c_in_refs, c_out_refs, sems)

        if single and conditional:
            for (ai, bi, ci, cond) in pairs:
                def only(ai=ai, bi=bi):
                    epilogue([product(ins, ai, bi)], ins, outs, ids)
                pl.when(cond(ids))(only)
        elif single:
            vals = [None] * len(acc_shapes)
            for (ai, bi, ci, cond) in pairs:
                r = product(ins, ai, bi)
                vals[ci] = r if vals[ci] is None else vals[ci] + r
            epilogue(vals, ins, outs, ids)
        else:
            @pl.when(k == 0)
            def _():
                for a in accs:
                    a[...] = jnp.zeros(a.shape, a.dtype)

            for (ai, bi, ci, cond) in pairs:
                def step(ai=ai, bi=bi, ci=ci):
                    accs[ci][...] += product(ins, ai, bi)
                if cond is None:
                    step()
                else:
                    pl.when(cond(ids))(step)

            @pl.when(k == nk - 1)
            def _():
                epilogue([a[...] for a in accs], ins, outs, ids)

        if comm:
            @pl.when(is_last)
            def _():
                comm.finish(c_in_refs, c_out_refs, sems)

    if semantics is None:
        semantics = ("parallel",) * kax + ("arbitrary",)
    if comm:
        semantics = ("arbitrary",) * len(grid)
    in_specs = [ANY if idx in aliases else spec for idx, spec in enumerate(in_specs)]
    blk = 0
    for idx, (spec, op) in enumerate(zip(in_specs, operands)):
        if idx not in aliases:
            blk += _nbytes(spec.block_shape, op.dtype)
    for spec, o in zip(out_specs, out_shape):
        blk += _nbytes(spec.block_shape, o.dtype)
    acc_bytes = sum(_nbytes(s, F32) for s in acc_shapes)
    io_alias = {n_pf + i: o for i, o in aliases.items()}
    io_alias.update({n_pf + n_in + i: n_out + o for i, o in c_alias.items()})
    all_in = list(in_specs) + [ANY] * n_ci
    all_out = list(out_specs) + [ANY] * n_co
    scratch_shapes = [pltpu.VMEM(s, F32) for s in acc_shapes[:n_acc]] + c_sems
    params = _params(semantics, blk, 3 * acc_bytes + temp_bytes)
    shapes = list(out_shape) + c_out
    if prefetch is None:
        call = pl.pallas_call(
            body, name=name, grid=grid, in_specs=all_in, out_specs=all_out, out_shape=shapes,
            scratch_shapes=scratch_shapes, input_output_aliases=io_alias, compiler_params=params)
        res = call(*operands, *c_ops)
    else:
        call = pl.pallas_call(
            body, name=name, out_shape=shapes, input_output_aliases=io_alias, compiler_params=params,
            grid_spec=pltpu.PrefetchScalarGridSpec(
                num_scalar_prefetch=1, grid=grid, in_specs=all_in, out_specs=all_out,
                scratch_shapes=scratch_shapes))
        res = call(prefetch, *operands, *c_ops)
    return res[:n_out], res[n_out:]


def _cast_bf16(name, w2d):
    r, c = w2d.shape
    tr = _pick(r, max(16, (4 << 20) // (4 * c)), 16)

    def body(w_ref, o_ref):
        o_ref[...] = w_ref[...].astype(BF16)

    return pl.pallas_call(
        body, name=name, grid=(r // tr,),
        in_specs=[pl.BlockSpec((tr, c), lambda i: (i, 0))],
        out_specs=pl.BlockSpec((tr, c), lambda i: (i, 0)),
        out_shape=jax.ShapeDtypeStruct((r, c), BF16),
        compiler_params=_params(("parallel",), tr * c * 6),
    )(w2d)


def _rms_fwd(x, norm_w):
    s, d = x.shape
    ts = _pick(s, 256, 16)

    def body(x_ref, w_ref, h_ref):
        xv = x_ref[...]
        rstd = lax.rsqrt(jnp.mean(xv * xv, axis=-1, keepdims=True) + NORM_EPS)
        h_ref[...] = (xv * rstd * w_ref[...]).astype(BF16)

    return pl.pallas_call(
        body, name="rms_fwd", grid=(s // ts,),
        in_specs=[pl.BlockSpec((ts, d), lambda i: (i, 0)), pl.BlockSpec((1, d), lambda i: (0, 0))],
        out_specs=pl.BlockSpec((ts, d), lambda i: (i, 0)),
        out_shape=jax.ShapeDtypeStruct((s, d), BF16),
        compiler_params=_params(("parallel",), ts * d * 6, 3 * ts * d * 4),
    )(x, norm_w)


def _head(x, o, target, fnw):
    s, d = x.shape
    ts = _pick(s, 128, 16)

    def body(x_ref, o_ref, t_ref, w_ref, dx_ref, dxb_ref, loss_ref, gw_ref):
        i = pl.program_id(0)

        @pl.when(i == 0)
        def _():
            loss_ref[...] = jnp.zeros(loss_ref.shape, F32)
            gw_ref[...] = jnp.zeros(gw_ref.shape, F32)

        w = w_ref[...]
        x2 = x_ref[...] + o_ref[...]
        rstd = lax.rsqrt(jnp.mean(x2 * x2, axis=-1, keepdims=True) + NORM_EPS)
        n = x2 * rstd
        e = n * w - t_ref[...]
        loss_ref[...] += 0.5 * jnp.sum(e * e) / d
        dy = e / d
        gw_ref[...] += jnp.sum(dy * n, axis=0, keepdims=True)
        gy = dy * w
        dx = rstd * (gy - n * jnp.mean(gy * n, axis=-1, keepdims=True))
        dx_ref[...] = dx
        dxb_ref[...] = dx.astype(BF16)

    row = pl.BlockSpec((ts, d), lambda i: (i, 0))
    return pl.pallas_call(
        body, name="head", grid=(s // ts,),
        in_specs=[row, row, row, pl.BlockSpec((1, d), lambda i: (0, 0))],
        out_specs=[row, row, pl.BlockSpec((8, LANES), lambda i: (0, 0)),
                   pl.BlockSpec((1, d), lambda i: (0, 0))],
        out_shape=[jax.ShapeDtypeStruct((s, d), F32), jax.ShapeDtypeStruct((s, d), BF16),
                   jax.ShapeDtypeStruct((8, LANES), F32), jax.ShapeDtypeStruct((1, d), F32)],
        compiler_params=_params(("arbitrary",), ts * d * 22, 6 * ts * d * 4),
    )(x, o, target, fnw)


def _rms_bwd(x, dh, dx2, norm_w):
    s, d = x.shape
    ts = _pick(s, 128, 16)

    def body(x_ref, dh_ref, dx2_ref, w_ref, gx_ref, gw_ref):
        i = pl.program_id(0)

        @pl.when(i == 0)
        def _():
            gw_ref[...] = jnp.zeros(gw_ref.shape, F32)

        xv = x_ref[...]
        rstd = lax.rsqrt(jnp.mean(xv * xv, axis=-1, keepdims=True) + NORM_EPS)
        n = xv * rstd
        dhv = dh_ref[...]
        gw_ref[...] += jnp.sum(dhv * n, axis=0, keepdims=True)
        gh = dhv * w_ref[...]
        gx_ref[...] = dx2_ref[...] + rstd * (gh - n * jnp.mean(gh * n, axis=-1, keepdims=True))

    row = pl.BlockSpec((ts, d), lambda i: (i, 0))
    vec = pl.BlockSpec((1, d), lambda i: (0, 0))
    return pl.pallas_call(
        body, name="rms_bwd", grid=(s // ts,),
        in_specs=[row, row, row, vec], out_specs=[row, vec],
        out_shape=[jax.ShapeDtypeStruct((s, d), F32), jax.ShapeDtypeStruct((1, d), F32)],
        compiler_params=_params(("arbitrary",), ts * d * 16, 5 * ts * d * 4),
    )(x, dh, dx2, norm_w)


def _silu(z):
    return z * _sigmoid(z)


def _window_sum(ext, window, back):
    n = ext.shape[0]
    acc = ext
    step = 1
    while step < window:
        acc = acc + pltpu.roll(acc, step if back else n - step, 0)
        step *= 2
    return acc


def _shift_rows(ext, k, back):
    n = ext.shape[0]
    return pltpu.roll(ext, k if back else n - k, 0)


def _mix_fwd(proj, pool_wg, pool_scale, conv_wg, conv_b, width):
    s = proj.shape[0]
    w = width
    cg = w // N_POOL_GROUPS
    ts = _pick(s, 128, HALO)
    hb = ts // HALO
    cols = 6 * w

    def body(p_ref, ph_ref, pw_ref, ps_ref, cw_ref, cb_ref, ys_ref):
        i = pl.program_id(0)
        first = i == 0
        t1 = (i * ts + lax.broadcasted_iota(jnp.int32, (ts, 1), 0) + 1).astype(F32)

        def tile(part, g):
            lo = part * w + g * cg
            return p_ref[:, lo:lo + cg].astype(F32)

        def prev(part, g):
            lo = part * w + g * cg
            return jnp.where(first, 0.0, ph_ref[:, lo:lo + cg].astype(F32))

        for g, win in enumerate(POOL_WINDOWS):
            gs = slice(g * cg, (g + 1) * cg)
            u = tile(0, g)
            ext = jnp.concatenate([prev(0, g), u], axis=0)
            wsum = _window_sum(ext, win, True)[HALO:]
            pooled = wsum / jnp.minimum(t1, float(win)) - u
            pw = pw_ref[:, g].reshape(cg, cg)
            mixed = jnp.dot(pooled.astype(BF16), pw, preferred_element_type=F32)
            ys_ref[0, :, gs] = (mixed * ps_ref[:, gs] * _silu(tile(1, g))).astype(BF16)
            v = tile(4, g) * tile(2, g)
            vext = jnp.concatenate([prev(4, g) * prev(2, g), v], axis=0)
            v1 = _shift_rows(vext, 1, True)[HALO:]
            v2 = _shift_rows(vext, 2, True)[HALO:]
            cw = [cw_ref[g, tap:tap + 1, :] for tap in range(CONV_K)]
            y = cb_ref[:, gs] + cw[0] * v2 + cw[1] * v1 + cw[2] * v
            ys_ref[1, :, gs] = (tile(3, g) * y * _silu(tile(5, g))).astype(BF16)

    return pl.pallas_call(
        body, name="mix_fwd", grid=(s // ts,),
        in_specs=[pl.BlockSpec((ts, cols), lambda i: (i, 0)),
                  pl.BlockSpec((HALO, cols), lambda i: (jnp.maximum(i * hb - 1, 0), 0)),
                  pl.BlockSpec(pool_wg.shape, lambda i: (0, 0, 0, 0)),
                  pl.BlockSpec((1, w), lambda i: (0, 0)),
                  pl.BlockSpec(conv_wg.shape, lambda i: (0, 0, 0)),
                  pl.BlockSpec((1, w), lambda i: (0, 0))],
        out_specs=pl.BlockSpec((2, ts, w), lambda i: (0, i, 0)),
        out_shape=jax.ShapeDtypeStruct((2, s, w), BF16),
        compiler_params=_params(("parallel",), (ts + HALO) * cols * 2 + 2 * ts * w * 2
                                + _nbytes(pool_wg.shape, BF16), 24 * (ts + HALO) * cg * 4),
    )(proj, proj, pool_wg, pool_scale, conv_wg, conv_b)


def _mix_bwd(proj, dys, pool_wg, pool_scale, conv_wg, conv_b, width):
    s = proj.shape[0]
    w = width
    cg = w // N_POOL_GROUPS
    ts = _pick(s, 128, HALO)
    hb = ts // HALO
    n_tiles = s // ts
    last_hb = s // HALO - 1
    cols = 6 * w

    def body(p_ref, ph_ref, pn_ref, dy_ref, dyn_ref, pw_ref, ps_ref, cw_ref, cb_ref,
             dp_ref, dpw_ref, dps_ref, dcw_ref, dcb_ref):
        i = pl.program_id(0)
        first = i == 0
        last = i == n_tiles - 1

        @pl.when(first)
        def _():
            dpw_ref[...] = jnp.zeros(dpw_ref.shape, F32)
            dps_ref[...] = jnp.zeros(dps_ref.shape, F32)
            dcw_ref[...] = jnp.zeros(dcw_ref.shape, F32)
            dcb_ref[...] = jnp.zeros(dcb_ref.shape, F32)

        row = i * ts + lax.broadcasted_iota(jnp.int32, (ts + HALO, 1), 0)
        t1_ext = (row + 1).astype(F32)
        t1 = t1_ext[:ts]

        def tile(part, g):
            lo = part * w + g * cg
            return p_ref[:, lo:lo + cg].astype(F32)

        def prev(part, g):
            lo = part * w + g * cg
            return jnp.where(first, 0.0, ph_ref[:, lo:lo + cg].astype(F32))

        def ahead(part, g):
            lo = part * w + g * cg
            return jnp.concatenate([tile(part, g), pn_ref[:, lo:lo + cg].astype(F32)], axis=0)

        def dy_ahead(n, g):
            gs = slice(g * cg, (g + 1) * cg)
            nxt = jnp.where(last, 0.0, dyn_ref[n, :, gs])
            return jnp.concatenate([dy_ref[n, :, gs], nxt], axis=0)

        for g, win in enumerate(POOL_WINDOWS):
            gs = slice(g * cg, (g + 1) * cg)
            u = tile(0, g)
            ext = jnp.concatenate([prev(0, g), u], axis=0)
            pooled = _window_sum(ext, win, True)[HALO:] / jnp.minimum(t1, float(win)) - u
            pooled_b = pooled.astype(BF16)
            pw = pw_ref[:, g].reshape(cg, cg)
            mixed = jnp.dot(pooled_b, pw, preferred_element_type=F32)
            zp_ext = ahead(1, g)
            dy0_ext = dy_ahead(0, g)
            scale = ps_ref[:, gs]
            sp_ext = _sigmoid(zp_ext)
            dms_ext = dy0_ext * (zp_ext * sp_ext)
            dmix_b = (dms_ext * scale).astype(BF16)
            dpooled_ext = lax.dot_general(dmix_b, pw, NT, preferred_element_type=F32)
            dy0 = dy0_ext[:ts]
            zp, sp = zp_ext[:ts], sp_ext[:ts]
            dsilu_p = sp * (1.0 + zp * (1.0 - sp))
            dp_ref[:, w + g * cg:w + (g + 1) * cg] = (dy0 * mixed * scale * dsilu_p).astype(BF16)
            dps_ref[:, gs] += jnp.sum(dms_ext[:ts] * mixed, axis=0, keepdims=True)
            dpw = lax.dot_general(pooled_b, dmix_b[:ts], TN, preferred_element_type=F32)
            dpw_ref[g // 2, :, g % 2] += dpw.reshape(N_CHIPS, cg // N_CHIPS, cg)
            q_ext = dpooled_ext / jnp.minimum(t1_ext, float(win))
            du = _window_sum(q_ext, win, False)[:ts] - dpooled_ext[:ts]
            dp_ref[:, gs] = du.astype(BF16)
            uc = tile(2, g)
            cc = tile(4, g)
            v = cc * uc
            vext = jnp.concatenate([prev(4, g) * prev(2, g), v], axis=0)
            v1 = _shift_rows(vext, 1, True)[HALO:]
            v2 = _shift_rows(vext, 2, True)[HALO:]
            cw = [cw_ref[g, tap:tap + 1, :] for tap in range(CONV_K)]
            y = cb_ref[:, gs] + cw[0] * v2 + cw[1] * v1 + cw[2] * v
            bc_ext = ahead(3, g)
            zc_ext = ahead(5, g)
            dy1_ext = dy_ahead(1, g)
            sc_ext = _sigmoid(zc_ext)
            silu_c_ext = zc_ext * sc_ext
            dyy_ext = dy1_ext * bc_ext * silu_c_ext
            dy1 = dy1_ext[:ts]
            bc = bc_ext[:ts]
            zc, sc = zc_ext[:ts], sc_ext[:ts]
            dsilu_c = sc * (1.0 + zc * (1.0 - sc))
            dp_ref[:, 3 * w + g * cg:3 * w + (g + 1) * cg] = (dy1 * y * silu_c_ext[:ts]).astype(BF16)
            dp_ref[:, 5 * w + g * cg:5 * w + (g + 1) * cg] = (dy1 * bc * y * dsilu_c).astype(BF16)
            dyy = dyy_ext[:ts]
            dcb_ref[:, gs] += jnp.sum(dyy, axis=0, keepdims=True)
            for tap, vt in enumerate((v2, v1, v)):
                dcw_ref[g, tap:tap + 1, :] += jnp.sum(dyy * vt, axis=0, keepdims=True)
            dv = (cw[2] * dyy + cw[1] * _shift_rows(dyy_ext, 1, False)[:ts]
                  + cw[0] * _shift_rows(dyy_ext, 2, False)[:ts])
            dp_ref[:, 4 * w + g * cg:4 * w + (g + 1) * cg] = (dv * uc).astype(BF16)
            dp_ref[:, 2 * w + g * cg:2 * w + (g + 1) * cg] = (dv * cc).astype(BF16)

    dpw_shape = (2, N_CHIPS, 2, cg // N_CHIPS, cg)
    return pl.pallas_call(
        body, name="mix_bwd", grid=(n_tiles,),
        in_specs=[pl.BlockSpec((ts, cols), lambda i: (i, 0)),
                  pl.BlockSpec((HALO, cols), lambda i: (jnp.maximum(i * hb - 1, 0), 0)),
                  pl.BlockSpec((HALO, cols), lambda i: (jnp.minimum((i + 1) * hb, last_hb), 0)),
                  pl.BlockSpec((2, ts, w), lambda i: (0, i, 0)),
                  pl.BlockSpec((2, HALO, w), lambda i: (0, jnp.minimum((i + 1) * hb, last_hb), 0)),
                  pl.BlockSpec(pool_wg.shape, lambda i: (0, 0, 0, 0)),
                  pl.BlockSpec((1, w), lambda i: (0, 0)),
                  pl.BlockSpec(conv_wg.shape, lambda i: (0, 0, 0)),
                  pl.BlockSpec((1, w), lambda i: (0, 0))],
        out_specs=[pl.BlockSpec((ts, cols), lambda i: (i, 0)),
                   pl.BlockSpec(dpw_shape, lambda i: (0, 0, 0, 0, 0)),
                   pl.BlockSpec((1, w), lambda i: (0, 0)),
                   pl.BlockSpec(conv_wg.shape, lambda i: (0, 0, 0)),
                   pl.BlockSpec((1, w), lambda i: (0, 0))],
        out_shape=[jax.ShapeDtypeStruct((s, cols), BF16), jax.ShapeDtypeStruct(dpw_shape, F32),
                   jax.ShapeDtypeStruct((1, w), F32), jax.ShapeDtypeStruct(conv_wg.shape, F32),
                   jax.ShapeDtypeStruct((1, w), F32)],
        compiler_params=_params(("arbitrary",), (2 * ts + 2 * HALO) * cols * 2 + (ts + HALO) * w * 8
                                + _nbytes(pool_wg.shape, BF16) + _nbytes(dpw_shape, F32),
                                40 * (ts + HALO) * cg * 4),
    )(proj, proj, proj, dys, dys, pool_wg, pool_scale, conv_wg, conv_b)


def _adamw(name, g, w, m, v):
    r, c = w.shape
    tr = _pick(r, max(8, (1 << 20) // (4 * c)), 8)

    def body(g_ref, w_ref, m_ref, v_ref, go_ref, d_ref, mo_ref, vo_ref):
        gv = g_ref[...]
        mn = ADAM_B1 * m_ref[...] + (1.0 - ADAM_B1) * gv
        vn = ADAM_B2 * v_ref[...] + (1.0 - ADAM_B2) * (gv * gv)
        m_hat = mn / (1.0 - ADAM_B1 ** ADAM_STEP)
        v_hat = vn / (1.0 - ADAM_B2 ** ADAM_STEP)
        go_ref[...] = gv
        d_ref[...] = -ADAM_LR * (m_hat / (jnp.sqrt(v_hat) + ADAM_EPS) + ADAM_WD * w_ref[...])
        mo_ref[...] = mn
        vo_ref[...] = vn

    blk = pl.BlockSpec((tr, c), lambda i: (i, 0))
    sh = jax.ShapeDtypeStruct((r, c), F32)
    return pl.pallas_call(
        body, name=name, grid=(r // tr,), in_specs=[blk] * 4, out_specs=[blk] * 4,
        out_shape=[sh] * 4, compiler_params=_params(("parallel",), tr * c * 32, 4 * tr * c * 4),
    )(g, w, m, v)


def _pair_add(name, g, r1, c_idx):
    _, r, c = g.shape
    tr = _pick(r, max(16, (2 << 20) // (2 * c)), 16)

    def body(c_ref, g_ref, r_ref, o_ref):
        o_ref[...] = (g_ref[...].astype(F32) + r_ref[...].astype(F32)).astype(BF16)

    return pl.pallas_call(
        body, name=name,
        grid_spec=pltpu.PrefetchScalarGridSpec(
            num_scalar_prefetch=1, grid=(r // tr,),
            in_specs=[pl.BlockSpec((None, tr, c), lambda i, cr: (cr[0], i, 0)),
                      pl.BlockSpec((tr, c), lambda i, cr: (i, 0))],
            out_specs=pl.BlockSpec((tr, c), lambda i, cr: (i, 0))),
        out_shape=jax.ShapeDtypeStruct((r, c), BF16),
        compiler_params=_params(("parallel",), tr * c * 6, 3 * tr * c * 4),
    )(c_idx, g, r1)


def _adamw_halves(name, g_own, g_other, c_idx, w, m, v):
    _, r, c = w.shape
    tr = _pick(r, max(8, (5 << 18) // (4 * c)), 8)

    def body(c_ref, go_ref, gt_ref, w_ref, m_ref, v_ref, g_out, d_ref, mo_ref, vo_ref):
        gv = jnp.where(pl.program_id(0) == c_ref[0], go_ref[...], gt_ref[...])
        mn = ADAM_B1 * m_ref[...] + (1.0 - ADAM_B1) * gv
        vn = ADAM_B2 * v_ref[...] + (1.0 - ADAM_B2) * (gv * gv)
        m_hat = mn / (1.0 - ADAM_B1 ** ADAM_STEP)
        v_hat = vn / (1.0 - ADAM_B2 ** ADAM_STEP)
        g_out[...] = gv
        d_ref[...] = -ADAM_LR * (m_hat / (jnp.sqrt(v_hat) + ADAM_EPS) + ADAM_WD * w_ref[...])
        mo_ref[...] = mn
        vo_ref[...] = vn

    blk = pl.BlockSpec((None, tr, c), lambda h, i, cr: (h, i, 0))
    sh = jax.ShapeDtypeStruct(w.shape, F32)
    return pl.pallas_call(
        body, name=name,
        grid_spec=pltpu.PrefetchScalarGridSpec(
            num_scalar_prefetch=1, grid=(2, r // tr),
            in_specs=[pl.BlockSpec((tr, c), lambda h, i, cr: (jnp.where(h == cr[0], i, 0), 0)),
                      pl.BlockSpec((tr, c), lambda h, i, cr: (jnp.where(h == cr[0], 0, i), 0)),
                      blk, blk, blk],
            out_specs=[blk] * 4),
        out_shape=[sh] * 4,
        compiler_params=_params(("arbitrary", "arbitrary"), tr * c * 36, 4 * tr * c * 4),
    )(c_idx, g_own, g_other, w, m, v)


def _sum_chips(name, part, r2, chip_idx, row0=0, prev=None):
    n, r, c = r2.shape
    rc = part.shape[1]
    tr = _pick(math.gcd(rc, row0) if row0 else rc, max(16, (1 << 20) // (2 * c)), 16)
    b0 = row0 // tr

    def body(ch_ref, own_ref, *rest):
        slots, o_ref = rest[:n], rest[-1]
        acc = None
        for s in range(n):
            term = jnp.where(ch_ref[0] == s, own_ref[...], slots[s][...]).astype(F32)
            acc = term if acc is None else acc + term
        o_ref[...] = acc

    def slot_spec(s):
        return pl.BlockSpec((None, tr, c), lambda i, ch: (jnp.where(ch[0] == s, (s + 1) % n, s), b0 + i, 0))

    extra = [] if prev is None else [prev]
    return pl.pallas_call(
        body, name=name,
        grid_spec=pltpu.PrefetchScalarGridSpec(
            num_scalar_prefetch=1, grid=(rc // tr,),
            in_specs=[pl.BlockSpec((None, tr, c), lambda i, ch: (ch[0], i, 0))]
            + [slot_spec(s) for s in range(n)] + [ANY] * len(extra),
            out_specs=pl.BlockSpec((tr, c), lambda i, ch: (b0 + i, 0))),
        out_shape=jax.ShapeDtypeStruct((r, c), F32),
        input_output_aliases={2 + n: 0} if extra else {},
        compiler_params=_params(("parallel",), tr * c * (2 * n + 6), 3 * tr * c * 4),
    )(chip_idx, part, *([r2] * n), *extra)


def _sum_devices(packs):
    n, r, c = packs.shape

    def body(p_ref, o_ref):
        acc = p_ref[0]
        for k in range(1, n):
            acc = acc + p_ref[k]
        o_ref[...] = acc

    return pl.pallas_call(
        body, name="sum_devices", out_shape=jax.ShapeDtypeStruct((r, c), F32),
        in_specs=[pl.BlockSpec(memory_space=pltpu.VMEM)],
        out_specs=pl.BlockSpec(memory_space=pltpu.VMEM),
    )(packs)


def _place():
    x, y, c = lax.axis_index("x"), lax.axis_index("y"), lax.axis_index("c")
    return x, y, c


def _chip_peers(x, y):
    out = []
    for k, (fx, fy) in enumerate(((0, 1), (1, 0), (1, 1))):
        px = 1 - x if fx else x
        py = 1 - y if fy else y
        out.append((k, px, py, 2 * px + py))
    return out


def _gather_weights(big, small, relay=False):
    nb, ns = len(big), len(small)

    class Copies:
        def __init__(self, c_in, c_out, sems):
            b_in, s_in = c_in[:nb], c_in[nb:]
            b_out, s_out = c_out[:nb], c_out[nb:]
            ici_s, ici_r, d2d_s, d2d_r, own_s, own_r = sems[:6]
            x, y, c = _place()
            chip = 2 * x + y
            sibling = (x, y, 1 - c)
            peers = _chip_peers(x, y)
            self.own = [pltpu.make_async_remote_copy(
                src_ref=b_in[t], dst_ref=b_out[t].at[chip], send_sem=own_s.at[t], recv_sem=own_r.at[t],
                device_id=sibling, device_id_type=MESH) for t in range(nb)]
            self.ici = [pltpu.make_async_remote_copy(
                src_ref=b_in[t].at[c], dst_ref=b_out[t].at[chip, c],
                send_sem=ici_s.at[t, k], recv_sem=ici_r.at[t, k],
                device_id=(px, py, c), device_id_type=MESH)
                for t in range(nb) for (k, px, py, pchip) in peers if not (relay and k == 2)]
            self.relay = []
            if relay:
                south = c == 0
                from_chip = jnp.where(south, 2 * x + (1 - y), 2 * (1 - x) + y)
                to = (jnp.where(south, 1 - x, x), jnp.where(south, y, 1 - y), c)
                self.relay = [pltpu.make_async_remote_copy(
                    src_ref=b_out[t].at[from_chip, c], dst_ref=b_out[t].at[from_chip, c],
                    send_sem=ici_s.at[t, 2], recv_sem=ici_r.at[t, 2],
                    device_id=to, device_id_type=MESH) for t in range(nb)]
            self.landed = [pltpu.make_async_remote_copy(
                src_ref=b_out[t].at[pchip, c], dst_ref=b_out[t].at[pchip, c],
                send_sem=ici_s.at[t, k], recv_sem=ici_r.at[t, k],
                device_id=sibling, device_id_type=MESH)
                for t in range(nb) for (k, px, py, pchip) in peers]
            self.passed = [pltpu.make_async_remote_copy(
                src_ref=b_out[t].at[pchip, c], dst_ref=b_out[t].at[pchip, c],
                send_sem=d2d_s.at[t, k], recv_sem=d2d_r.at[t, k],
                device_id=sibling, device_id_type=MESH)
                for t in range(nb) for (k, px, py, pchip) in peers]
            self.from_sibling = [pltpu.make_async_remote_copy(
                src_ref=b_out[t].at[pchip, 1 - c], dst_ref=b_out[t].at[pchip, 1 - c],
                send_sem=d2d_s.at[t, k], recv_sem=d2d_r.at[t, k],
                device_id=sibling, device_id_type=MESH)
                for t in range(nb) for (k, px, py, pchip) in peers]
            self.small, self.small_landed, self.local = [], [], []
            if ns:
                sm_s, sm_r, loc = sems[6:]
                self.local = [pltpu.make_async_copy(s_in[t], s_out[t].at[chip], loc.at[t]) for t in range(ns)]
                self.small = [pltpu.make_async_remote_copy(
                    src_ref=s_in[t], dst_ref=s_out[t].at[chip],
                    send_sem=sm_s.at[t, k], recv_sem=sm_r.at[t, k],
                    device_id=(px, py, c), device_id_type=MESH)
                    for t in range(ns) for (k, px, py, pchip) in peers]
                self.small_landed = [pltpu.make_async_remote_copy(
                    src_ref=s_in[t], dst_ref=s_out[t].at[pchip],
                    send_sem=sm_s.at[t, k], recv_sem=sm_r.at[t, k],
                    device_id=sibling, device_id_type=MESH)
                    for t in range(ns) for (k, px, py, pchip) in peers]

    def start(c_in, c_out, sems):
        cps = Copies(c_in, c_out, sems)
        for cp in cps.local + cps.own + cps.ici + cps.small:
            cp.start()

    def finish(c_in, c_out, sems):
        cps = Copies(c_in, c_out, sems)
        for t in range(nb):
            for k in range(3):
                if relay and k == 2:
                    cps.relay[t].start()
                cps.landed[3 * t + k].wait_recv()
                cps.passed[3 * t + k].start()
        for cp in cps.small_landed + cps.from_sibling:
            cp.wait_recv()
        for cp in cps.ici + cps.relay + cps.small + cps.passed:
            cp.wait_send()
        for cp in cps.own + cps.local:
            cp.wait()

    out_shape = [jax.ShapeDtypeStruct((N_CHIPS,) + b.shape, b.dtype) for b in big]
    out_shape += [jax.ShapeDtypeStruct((N_CHIPS,) + s.shape, s.dtype) for s in small]
    dma = pltpu.SemaphoreType.DMA
    scratch = [dma((nb, 3)), dma((nb, 3)), dma((nb, 3)), dma((nb, 3)), dma((nb,)), dma((nb,))]
    if ns:
        scratch += [dma((ns, 3)), dma((ns, 3)), dma((ns,))]
    ex = _Exchange(list(big) + list(small), out_shape, scratch, start, finish)
    ex.copies = Copies
    return ex


class _Exchange:
    def __init__(self, operands, out_shape, scratch, start, finish, aliases=None):
        self.operands, self.out_shape, self.scratch = operands, out_shape, scratch
        self.start, self.finish, self.aliases = start, finish, dict(aliases or {})


def _run_exchange(name, ex):
    n_i, n_o = len(ex.operands), len(ex.out_shape)

    def body(*refs):
        ins, outs, sems = refs[:n_i], refs[n_i:n_i + n_o], refs[n_i + n_o:]
        ex.start(ins, outs, sems)
        ex.finish(ins, outs, sems)

    return list(pl.pallas_call(
        body, name=name, out_shape=list(ex.out_shape), in_specs=[ANY] * n_i, out_specs=[ANY] * n_o,
        scratch_shapes=list(ex.scratch), input_output_aliases=ex.aliases,
    )(*ex.operands))


def _to_sibling(arrays, other_half):
    n = len(arrays)

    def copies(a_in, a_out, sems):
        send, recv = sems
        x, y, c = _place()
        return [pltpu.make_async_remote_copy(
            src_ref=a_in[t].at[1 - c] if other_half else a_in[t], dst_ref=a_out[t],
            send_sem=send.at[t], recv_sem=recv.at[t],
            device_id=(x, y, 1 - c), device_id_type=MESH) for t in range(n)]

    def start(a_in, a_out, sems):
        for cp in copies(a_in, a_out, sems):
            cp.start()

    def finish(a_in, a_out, sems):
        for cp in copies(a_in, a_out, sems):
            cp.wait()

    dma = pltpu.SemaphoreType.DMA
    shapes = [jax.ShapeDtypeStruct(a.shape[1:] if other_half else a.shape, a.dtype) for a in arrays]
    return _Exchange(list(arrays), shapes, [dma((n,)), dma((n,))], start, finish)


def _scatter_partials(parts, rows=None, row0=0, prev=None):
    n = len(parts)
    land = [(p.shape[0], p.shape[1] if rows is None else rows, p.shape[2]) for p in parts]

    def window(ref, slot, t):
        return ref.at[slot, pl.ds(row0, parts[t].shape[1])]

    def sends(p_in, r_out, sems):
        send, recv = sems
        x, y, c = _place()
        chip = 2 * x + y
        return [pltpu.make_async_remote_copy(
            src_ref=p_in[t].at[pchip], dst_ref=window(r_out[t], chip, t),
            send_sem=send.at[t, k], recv_sem=recv.at[t, k],
            device_id=(px, py, c), device_id_type=MESH)
            for t in range(n) for (k, px, py, pchip) in _chip_peers(x, y)]

    def start(p_in, r_out, sems):
        for cp in sends(p_in, r_out, sems):
            cp.start()

    def finish(p_in, r_out, sems):
        send, recv = sems
        x, y, c = _place()
        for t in range(n):
            for (k, px, py, pchip) in _chip_peers(x, y):
                pltpu.make_async_remote_copy(
                    src_ref=p_in[t].at[pchip], dst_ref=window(r_out[t], pchip, t),
                    send_sem=send.at[t, k], recv_sem=recv.at[t, k],
                    device_id=(px, py, c), device_id_type=MESH).wait_recv()
        for cp in sends(p_in, r_out, sems):
            cp.wait_send()

    dma = pltpu.SemaphoreType.DMA
    operands = list(parts) + (list(prev) if prev else [])
    return _Exchange(operands, [jax.ShapeDtypeStruct(sh, p.dtype) for sh, p in zip(land, parts)],
                     [dma((n, 3)), dma((n, 3))], start, finish,
                     aliases={n + t: t for t in range(n)} if prev else None)


def _w_in_grad_sibling(h, dpa, dgl, c_idx, carried, *, tm, tn, rows, pq):
    s, d = h.shape
    wa, wg = dpa.shape[1], dgl.shape[2]
    p = wa + 2 * wg
    ni, nj = rows // tm, p // tn
    na, qg, qp = wa // tn, wg // tn, pq // tn
    n_steps = ni * nj
    n_ci, n_co = len(carried.operands), len(carried.out_shape)

    def body(c_ref, h_ref, a_ref, g_ref, *rest):
        c_in = rest[:n_ci]
        r1_ref = rest[n_ci]
        c_out = rest[n_ci + 1:n_ci + 1 + n_co]
        slots, send, recv = rest[n_ci + 1 + n_co:n_ci + 4 + n_co]
        sems = rest[n_ci + 4 + n_co:]
        i, j = pl.program_id(0), pl.program_id(1)
        step = i * nj + j
        slot = lax.rem(step, 2)
        x, y, c = _place()
        sibling = (x, y, 1 - c)

        @pl.when(step == 0)
        def _():
            carried.start(c_in, c_out, sems)

        def tile_copy(sl):
            return pltpu.make_async_remote_copy(
                src_ref=slots.at[sl],
                dst_ref=r1_ref.at[j // qp, pl.ds(i * tm, tm), pl.ds((j % qp) * tn, tn)],
                send_sem=send.at[sl], recv_sem=recv, device_id=sibling, device_id_type=MESH)

        @pl.when(step >= 2)
        def _():
            tile_copy(slot).wait_send()

        def emit(b_ref):
            acc = lax.dot_general(h_ref[...], b_ref[...], TN, preferred_element_type=F32)
            slots[slot] = acc.astype(BF16)

        pl.when(j < na)(lambda: emit(a_ref))
        pl.when(j >= na)(lambda: emit(g_ref))
        tile_copy(slot).start()

        @pl.when(step == n_steps - 1)
        def _():
            tile_copy(0).wait_send()
            tile_copy(1).wait_send()
            pltpu.make_async_remote_copy(
                src_ref=r1_ref, dst_ref=r1_ref, send_sem=send.at[0], recv_sem=recv,
                device_id=sibling, device_id_type=MESH).wait_recv()
            carried.finish(c_in, c_out, sems)

    nbh = rows // tm
    dma = pltpu.SemaphoreType.DMA
    blk = s * tm * 2 + 2 * s * tn * 2
    res = pl.pallas_call(
        body, name="w_in_grad_sibling",
        grid_spec=pltpu.PrefetchScalarGridSpec(
            num_scalar_prefetch=1, grid=(ni, nj),
            in_specs=[pl.BlockSpec((s, tm), lambda i, j, cr: (0, (1 - cr[0]) * nbh + i)),
                      pl.BlockSpec((s, tn), lambda i, j, cr: (0, jnp.minimum(j, na - 1))),
                      pl.BlockSpec((None, s, tn),
                                   lambda i, j, cr: (jnp.maximum(j - na, 0) // qg, 0, jnp.maximum(j - na, 0) % qg))]
            + [ANY] * n_ci,
            out_specs=[ANY] * (1 + n_co),
            scratch_shapes=[pltpu.VMEM((2, tm, tn), BF16), dma((2,)), dma] + list(carried.scratch)),
        out_shape=[jax.ShapeDtypeStruct((N_CHIPS, rows, pq), BF16)] + list(carried.out_shape),
        compiler_params=_params(("arbitrary", "arbitrary"), blk, 2 * tm * tn * 2 + 2 * tm * tn * 4),
    )(c_idx, h, dpa, dgl, *carried.operands)
    return res[0], res[1:]


def _proj_gathering(h, wb, chip_idx, gather_rest, *, tm, tn):
    s, d = h.shape
    _, hh, pq = wb.shape
    qp, ni = pq // tn, s // tm
    n_tiles = N_CHIPS * qp
    n_ci, n_co, n_cs = len(gather_rest.operands), len(gather_rest.out_shape), len(gather_rest.scratch)

    def where_of(t):
        near = t - qp
        g = jnp.where(t < qp, 0, jnp.where(t < 3 * qp, 1 + lax.rem(near, 2), 3))
        jj = jnp.where(t < qp, t, jnp.where(t < 3 * qp, near // 2, t - 3 * qp))
        return g, jj

    def body(ch_ref, h_ref, own_ref, *rest):
        c_in = rest[:n_ci]
        o_ref, gathered = rest[n_ci], rest[n_ci + 1]
        c_out = rest[n_ci + 2:n_ci + 2 + n_co]
        slots, fetch_sem, ici_s, ici_r, d2d_s, d2d_r, own_s, own_r = rest[n_ci + 2 + n_co:n_ci + 10 + n_co]
        c_sem = rest[n_ci + 10 + n_co:]
        tile, i = pl.program_id(0), pl.program_id(1)
        x, y, c = _place()
        chip = 2 * x + y
        sibling = (x, y, 1 - c)

        def cols(jj):
            return pl.ds(jj * tn, tn)

        def remote(src, dst, send, recv, to):
            return pltpu.make_async_remote_copy(src_ref=src, dst_ref=dst, send_sem=send, recv_sem=recv,
                                                device_id=to, device_id_type=MESH)

        def own_shard():
            return remote(own_ref, gathered.at[chip], own_s, own_r, sibling)

        def to_neighbour(k, jj):
            to = (x, 1 - y, c) if k == 0 else (1 - x, y, c)
            return remote(own_ref.at[c, :, cols(jj)], gathered.at[chip, c, :, cols(jj)],
                          ici_s.at[k, jj], ici_r.at[k, jj], to)

        def relay(jj):
            south = c == 0
            from_chip = jnp.where(south, 2 * x + (1 - y), 2 * (1 - x) + y)
            to = (jnp.where(south, 1 - x, x), jnp.where(south, y, 1 - y), c)
            chunk = gathered.at[from_chip, c, :, cols(jj)]
            return remote(chunk, chunk, ici_s.at[2, jj], ici_r.at[2, jj], to)

        def chunk_of(k, jj, half):
            return gathered.at[jnp.bitwise_xor(chip, k + 1), half, :, cols(jj)]

        def landed(k, jj):
            return remote(chunk_of(k, jj, c), chunk_of(k, jj, c), ici_s.at[k, jj], ici_r.at[k, jj], sibling)

        def passed(k, jj):
            return remote(chunk_of(k, jj, c), chunk_of(k, jj, c), d2d_s.at[k, jj], d2d_r.at[k, jj], sibling)

        def from_sibling(k, jj):
            return remote(chunk_of(k, jj, 1 - c), chunk_of(k, jj, 1 - c), d2d_s.at[k, jj], d2d_r.at[k, jj], sibling)

        def fetch(src, slot):
            return pltpu.make_async_copy(src, slots.at[slot], fetch_sem.at[slot])

        @pl.when(jnp.logical_and(tile == 0, i == 0))
        def _():
            own_shard().start()
            for jj in range(qp):
                for k in range(2):
                    to_neighbour(k, jj).start()
            fetch(own_ref.at[:, :, cols(0)], 0).start()

        @pl.when(i == 0)
        def _():
            fetch(own_ref.at[:, :, cols(0)], lax.rem(tile, 2)).wait()

        last_row = i == ni - 1
        nxt, ahead = tile + 1, tile + 2
        g1, j1 = where_of(nxt)
        g2, j2 = where_of(ahead)

        def pass_on(t, g, jj):
            @pl.when(jnp.logical_and(last_row, jnp.logical_and(t >= qp, t < n_tiles)))
            def _():
                landed(g - 1, jj).wait_recv()
                passed(g - 1, jj).start()

            @pl.when(jnp.logical_and(last_row, t == 3 * qp - 1))
            def _():
                for ch in range(qp):
                    relay(ch).start()
                gather_rest.start(c_in, c_out, c_sem)

        pass_on(ahead, g2, j2)

        @pl.when(jnp.logical_and(last_row, nxt < qp))
        def _():
            fetch(own_ref.at[:, :, cols(nxt)], lax.rem(nxt, 2)).start()

        @pl.when(jnp.logical_and(last_row, jnp.logical_and(nxt >= qp, nxt < n_tiles)))
        def _():
            from_sibling(g1 - 1, j1).wait_recv()
            fetch(gathered.at[jnp.bitwise_xor(chip, g1), :, :, cols(j1)], lax.rem(nxt, 2)).start()

        w_tile = slots[lax.rem(tile, 2)].reshape(d, tn)
        o_ref[...] = jnp.dot(h_ref[...], w_tile, preferred_element_type=F32).astype(BF16)

        @pl.when(jnp.logical_and(tile == n_tiles - 1, last_row))
        def _():
            for jj in range(qp):
                for k in range(2):
                    to_neighbour(k, jj).wait_send()
                relay(jj).wait_send()
                for k in range(3):
                    passed(k, jj).wait_send()
            own_shard().wait()
            gather_rest.finish(c_in, c_out, c_sem)

    assert qp >= 2 and not gather_rest.aliases
    dma = pltpu.SemaphoreType.DMA
    p = N_CHIPS * pq

    def out_cols(t, i, ch):
        g, jj = where_of(t)
        return (i, jnp.bitwise_xor(ch[0], g) * qp + jj)

    res = pl.pallas_call(
        body, name="proj_gathering",
        grid_spec=pltpu.PrefetchScalarGridSpec(
            num_scalar_prefetch=1, grid=(n_tiles, ni),
            in_specs=[pl.BlockSpec((tm, d), lambda t, i, ch: (i, 0)), ANY] + [ANY] * n_ci,
            out_specs=[pl.BlockSpec((tm, tn), out_cols), ANY] + [ANY] * n_co,
            scratch_shapes=[pltpu.VMEM((2, 2, hh, tn), BF16), dma((2,)),
                            dma((3, qp)), dma((3, qp)), dma((3, qp)), dma((3, qp)), dma, dma]
            + list(gather_rest.scratch)),
        out_shape=[jax.ShapeDtypeStruct((s, p), BF16), jax.ShapeDtypeStruct((N_CHIPS,) + wb.shape, wb.dtype)]
        + list(gather_rest.out_shape),
        compiler_params=_params(("arbitrary",) * 2, tm * d * 2 + tm * tn * 2, 3 * d * tn * 2 + tm * tn * 4),
    )(chip_idx, h, wb, *gather_rest.operands)
    return res[0], res[1], res[2:]


def _gather_packs(pack):
    def body(p_ref, o_ref, send, recv, loc):
        x, y, c = _place()
        me = 4 * x + 2 * y + c
        mine = pltpu.make_async_copy(p_ref, o_ref.at[me], loc)
        mine.start()
        flips = [(fx, fy, fc) for fx in (0, 1) for fy in (0, 1) for fc in (0, 1)][1:]
        cps = []
        for k, (fx, fy, fc) in enumerate(flips):
            peer = (1 - x if fx else x, 1 - y if fy else y, 1 - c if fc else c)
            cps.append(pltpu.make_async_remote_copy(
                src_ref=p_ref, dst_ref=o_ref.at[me], send_sem=send.at[k], recv_sem=recv.at[k],
                device_id=peer, device_id_type=MESH))
        for cp in cps:
            cp.start()
        for k, (fx, fy, fc) in enumerate(flips):
            px, py, pc = (1 - x if fx else x, 1 - y if fy else y, 1 - c if fc else c)
            pltpu.make_async_remote_copy(
                src_ref=p_ref, dst_ref=o_ref.at[4 * px + 2 * py + pc],
                send_sem=send.at[k], recv_sem=recv.at[k],
                device_id=(px, py, pc), device_id_type=MESH).wait_recv()
        for cp in cps:
            cp.wait_send()
        mine.wait()

    dma = pltpu.SemaphoreType.DMA
    return pl.pallas_call(
        body, name="gather_packs", out_shape=jax.ShapeDtypeStruct((N_DEV,) + pack.shape, pack.dtype),
        in_specs=[ANY], out_specs=ANY, scratch_shapes=[dma((7,)), dma((7,)), dma],
    )(pack)


def _flat_pack(pieces):
    flat = jnp.concatenate([p.reshape(-1) for p in pieces])
    pad = (-flat.shape[0]) % (8 * LANES)
    flat = jnp.concatenate([flat, jnp.zeros((pad,), F32)])
    return flat.reshape(-1, LANES)


def _unpack(pack, shapes):
    flat = pack.reshape(-1)
    out, off = [], 0
    for sh in shapes:
        n = 1
        for s in sh:
            n *= s
        out.append(flat[off:off + n].reshape(sh))
        off += n
    return out


def kernel(x, norm_w, w_in, pool_w, pool_scale, conv_w, conv_b, gate_b, w_branch, w_out, final_norm_w, loss_target, m_norm_w, m_w_in, m_pool_w, m_pool_scale, m_conv_w, m_conv_b, m_gate_b, m_w_branch, m_w_out, m_final_norm_w, v_norm_w, v_w_in, v_pool_w, v_pool_scale, v_conv_w, v_conv_b, v_gate_b, v_w_branch, v_w_out, v_final_norm_w):
    _, s, d = x.shape
    w = d // 2
    cg = w // N_POOL_GROUPS
    p = 6 * w + 2 * d
    pq = p // N_CHIPS
    dq = d // N_CHIPS
    assert w_in.shape == (1, d, pq) and w_branch.shape == (1, 2, w, dq) and w_out.shape == (1, dq, d)
    assert pool_w.shape == (1, N_POOL_GROUPS, cg // N_CHIPS, cg) and conv_w.shape == (1, CONV_K, cg)

    x2d = x.reshape(s, d)
    tgt = loss_target.reshape(s, d)
    c_idx = lax.axis_index("c").astype(jnp.int32).reshape(1)
    chip = 2 * lax.axis_index("x") + lax.axis_index("y")

    big_w = [w_in.reshape(d, pq), w_out.reshape(dq, d), w_branch.reshape(2 * w, dq),
             pool_w.reshape(cg, cg)]
    names = ["in", "out", "branch", "pool"]
    big_b = [_cast_bf16("cast_" + nm, a) for nm, a in zip(names, big_w)]
    big_b = [b.reshape(2, b.shape[0] // 2, b.shape[1]) for b in big_b]
    chip_idx = chip.astype(jnp.int32).reshape(1)

    h = _rms_fwd(x2d, norm_w)

    tm = _pick(s, 1024, 16)
    tn_p = _pick(math.gcd(pq, 6 * w, d), 1024, LANES)
    qp = pq // tn_p
    tk_d = _pick(d, 4096, LANES)
    proj, wg_in, (wg_br, wg_pool, wg_cw, wg_gb) = _proj_gathering(
        h, big_b[0], chip_idx,
        _gather_weights(big_b[2:], [conv_w.reshape(CONV_K, cg), gate_b.reshape(2, dq)]),
        tm=tm, tn=tn_p)
    wg_in = wg_in.reshape(N_CHIPS, d, pq)
    wg_pool = wg_pool.reshape(N_CHIPS, N_POOL_GROUPS, cg // N_CHIPS, cg)

    ys = _mix_fwd(proj, wg_pool, pool_scale, wg_cw, conv_b, w)

    tn_d = _pick(dq, 1024, LANES)
    qd = dq // tn_d
    tk_w = _pick(w, 2048, LANES)
    tm_g = _pick(s, 512, 16)
    gl0 = (6 * w) // tn_d
    gl1 = (6 * w + d) // tn_d

    def gate_specs(im, rows):
        return [pl.BlockSpec((rows, tn_d), lambda *a: (im(*a)[0], gl0 + im(*a)[1])),
                pl.BlockSpec((rows, tn_d), lambda *a: (im(*a)[0], gl1 + im(*a)[1])),
                pl.BlockSpec((None, 2, tn_d), lambda *a: (im(*a)[1] // qd, 0, im(*a)[1] % qd))]

    def merge_epilogue(accs, ins, outs, ids):
        gb = ins[6][...]
        g0 = _sigmoid(ins[4][...].astype(F32) + gb[0:1])
        g1 = _sigmoid(ins[5][...].astype(F32) + gb[1:2])
        outs[0][...] = (g0 * accs[0] + g1 * accs[1]).astype(BF16)
        outs[1][0] = accs[0].astype(BF16)
        outs[1][1] = accs[1].astype(BF16)

    (merged, br), (wg_out,) = _mm(
        "branch_merge", grid=(s // tm_g, d // tn_d, w // tk_w), comm=_gather_weights(big_b[1:2], []),
        operands=[ys, wg_br, ys, wg_br, proj, proj, wg_gb],
        in_specs=[pl.BlockSpec((None, tm_g, tk_w), lambda i, j, k: (0, i, k)),
                  pl.BlockSpec((None, None, tk_w, tn_d), lambda i, j, k: (j // qd, 0, k, j % qd)),
                  pl.BlockSpec((None, tm_g, tk_w), lambda i, j, k: (1, i, k)),
                  pl.BlockSpec((None, None, tk_w, tn_d), lambda i, j, k: (j // qd, 1, k, j % qd)),
                  *gate_specs(lambda i, j, k: (i, j), tm_g)],
        out_shape=[jax.ShapeDtypeStruct((s, d), BF16), jax.ShapeDtypeStruct((2, s, d), BF16)],
        out_specs=[pl.BlockSpec((tm_g, tn_d), lambda i, j, k: (i, j)),
                   pl.BlockSpec((2, tm_g, tn_d), lambda i, j, k: (0, i, j))],
        pairs=[(0, 1, 0, None), (2, 3, 1, None)], dims=NN, acc_shapes=[(tm_g, tn_d)] * 2,
        epilogue=merge_epilogue, temp_bytes=6 * tm_g * tn_d * 4,
    )
    wg_out = wg_out.reshape(d, d)

    tn_f = _pick(d, 1024, LANES)
    o = _mm(
        "out_proj", grid=(s // tm, d // tn_f, d // tk_d), operands=[merged, wg_out],
        in_specs=[pl.BlockSpec((tm, tk_d), lambda i, j, k: (i, k)),
                  pl.BlockSpec((tk_d, tn_f), lambda i, j, k: (k, j))],
        out_shape=[jax.ShapeDtypeStruct((s, d), F32)],
        out_specs=[pl.BlockSpec((tm, tn_f), lambda i, j, k: (i, j))],
        pairs=[(0, 1, 0, None)], dims=NN, acc_shapes=[(tm, tn_f)],
        epilogue=lambda accs, ins, outs, ids: outs[0].__setitem__(Ellipsis, accs[0]),
    )[0][0]

    dx2, dx2b, loss_part, g_fnw =_head(x2d, o, tgt, final_norm_w.reshape(1, d))

    def gate_bwd_epilogue(accs, ins, outs, ids):
        dm = accs[0]
        gb = ins[5][...]
        i = ids[1]

        @pl.when(i == 0)
        def _():
            outs[2][...] = jnp.zeros(outs[2].shape, F32)

        for n in range(2):
            gate = _sigmoid(ins[3 + n][...].astype(F32) + gb[n:n + 1])
            outs[0][n] = (dm * gate).astype(BF16)
            dgl = dm * ins[2][n].astype(F32) * gate * (1.0 - gate)
            outs[1][n] = dgl.astype(BF16)
            outs[2][n:n + 1, :] += jnp.sum(dgl, axis=0, keepdims=True)

    tm_b = _pick(s, 256, 16)
    (d_br, dgl, g_gb), _ = _mm(
        "out_proj_bwd_gate", grid=(d // tn_d, s // tm_b, d // tk_d),
        operands=[dx2b, wg_out, br, proj, proj, wg_gb],
        in_specs=[pl.BlockSpec((tm_b, tk_d), lambda j, i, k: (i, k)),
                  pl.BlockSpec((tn_d, tk_d), lambda j, i, k: (j, k)),
                  pl.BlockSpec((2, tm_b, tn_d), lambda j, i, k: (0, i, j)),
                  *gate_specs(lambda j, i, k: (i, j), tm_b)],
        out_shape=[jax.ShapeDtypeStruct((2, s, d), BF16), jax.ShapeDtypeStruct((2, s, d), BF16),
                   jax.ShapeDtypeStruct((2, d), F32)],
        out_specs=[pl.BlockSpec((2, tm_b, tn_d), lambda j, i, k: (0, i, j)),
                   pl.BlockSpec((2, tm_b, tn_d), lambda j, i, k: (0, i, j)),
                   pl.BlockSpec((2, tn_d), lambda j, i, k: (0, j))],
        pairs=[(0, 1, 0, None)], dims=NT, acc_shapes=[(tm_b, tn_d)],
        epilogue=gate_bwd_epilogue, semantics=("parallel", "arbitrary", "arbitrary"),
        temp_bytes=8 * tm_b * tn_d * 4,
    )

    hh_out = d // 8
    tm_o = _pick(hh_out, 512, LANES)
    nb_o = hh_out // tm_o
    tk_s = _pick(s, 4096, LANES)
    g_out = _mm(
        "w_out_grad", grid=(d // tm_o, d // tn_f, s // tk_s), operands=[merged, dx2b],
        in_specs=[pl.BlockSpec((tk_s, tm_o), lambda i, j, k: (k, i)),
                  pl.BlockSpec((tk_s, tn_f), lambda i, j, k: (k, j))],
        out_shape=[jax.ShapeDtypeStruct((2, N_CHIPS, hh_out, d), BF16)],
        out_specs=[pl.BlockSpec((None, None, tm_o, tn_f),
                                lambda i, j, k: ((i // nb_o) % 2, i // (2 * nb_o), i % nb_o, j))],
        pairs=[(0, 1, 0, None)], dims=TN, acc_shapes=[(tm_o, tn_f)],
        epilogue=lambda accs, ins, outs, ids: outs[0].__setitem__(Ellipsis, accs[0].astype(BF16)),
    )[0][0]

    tn_w = _pick(w, 2048, LANES)
    tk_q = _pick(dq, 1024, LANES)
    qk = dq // tk_q
    (dys,), from_sibling_out = _mm(
        "branch_bwd", grid=(2, s // tm, w // tn_w, d // tk_q), operands=[d_br, wg_br],
        comm=_to_sibling([g_out], other_half=True),
        in_specs=[pl.BlockSpec((None, tm, tk_q), lambda n, i, j, k: (n, i, k)),
                  pl.BlockSpec((None, None, tn_w, tk_q), lambda n, i, j, k: (k // qk, n, j, k % qk))],
        out_shape=[jax.ShapeDtypeStruct((2, s, w), F32)],
        out_specs=[pl.BlockSpec((None, tm, tn_w), lambda n, i, j, k: (n, i, j))],
        pairs=[(0, 1, 0, None)], dims=NT, acc_shapes=[(tm, tn_w)],
        epilogue=lambda accs, ins, outs, ids: outs[0].__setitem__(Ellipsis, accs[0]),
    )

    tm_w = _pick(w, 1024, LANES)
    g_br = _mm(
        "w_branch_grad", grid=(2, w // tm_w, d // tn_d, s // tk_s), operands=[ys, d_br],
        in_specs=[pl.BlockSpec((None, tk_s, tm_w), lambda n, i, j, k: (n, k, i)),
                  pl.BlockSpec((None, tk_s, tn_d), lambda n, i, j, k: (n, k, j))],
        out_shape=[jax.ShapeDtypeStruct((2, N_CHIPS, w, dq), BF16)],
        out_specs=[pl.BlockSpec((None, None, tm_w, tn_d), lambda n, i, j, k: (n, j // qd, i, j % qd))],
        pairs=[(0, 1, 0, None)], dims=TN, acc_shapes=[(tm_w, tn_d)],
        epilogue=lambda accs, ins, outs, ids: outs[0].__setitem__(Ellipsis, accs[0].astype(BF16)),
    )[0][0]

    dpa, g_pool, g_ps, g_cw, g_cb = _mix_bwd(proj, dys, wg_pool, pool_scale, wg_cw, conv_b, w)
    g_pool = g_pool.astype(BF16).reshape(2, N_CHIPS, 2 * (cg // N_CHIPS), cg)

    early = [g_out, g_br, g_pool]
    from_sibling_early = list(from_sibling_out) + _run_exchange(
        "pair_exchange", _to_sibling(early[1:], other_half=True))
    parts_early = []
    for nm, g, r1 in zip(names[1:], early, from_sibling_early):
        _, nc, hh, wd = g.shape
        pt = _pair_add("pair_add_" + nm, g.reshape(2, nc * hh, wd), r1.reshape(nc * hh, wd), c_idx)
        parts_early.append(pt.reshape(nc, hh, wd))

    na = (6 * w) // tn_p
    qg = d // tn_p
    hh_in = d // 2
    assert s == tk_s
    tm_i = _pick(hh_in, 1024, LANES)
    tn_i = _pick(tn_p, 512, LANES)
    from_sibling, recv_early = _w_in_grad_sibling(
        h, dpa, dgl, c_idx, _scatter_partials(parts_early), tm=tm_i, tn=tn_i, rows=hh_in, pq=pq)

    tm_o2 = _pick(hh_in, 512, LANES)
    nb_o2 = hh_in // tm_o2

    def own_rows(name, first, count, comm):
        def add_sibling(accs, ins, outs, ids):
            outs[0][...] = (accs[0] + ins[3][...].astype(F32)).astype(BF16)

        return _mm(
            name, grid=(p // tn_p, count, 1), operands=[h, dpa, dgl, from_sibling], comm=comm, prefetch=c_idx,
            in_specs=[pl.BlockSpec((tk_s, tm_o2), lambda j, i, k, cr: (0, cr[0] * nb_o2 + first + i)),
                      pl.BlockSpec((tk_s, tn_p), lambda j, i, k, cr: (0, jnp.minimum(j, na - 1))),
                      pl.BlockSpec((None, tk_s, tn_p),
                                   lambda j, i, k, cr: (jnp.maximum(j - na, 0) // qg, 0, jnp.maximum(j - na, 0) % qg)),
                      pl.BlockSpec((None, tm_o2, tn_p), lambda j, i, k, cr: (j // qp, first + i, j % qp))],
            out_shape=[jax.ShapeDtypeStruct((N_CHIPS, count * tm_o2, pq), BF16)],
            out_specs=[pl.BlockSpec((None, tm_o2, tn_p), lambda j, i, k, cr: (j // qp, i, j % qp))],
            pairs=[(0, 1, 0, lambda ids: ids[0] < na), (0, 2, 0, lambda ids: ids[0] >= na)],
            dims=TN, acc_shapes=[(tm_o2, tn_p)], epilogue=add_sibling)

    halves_early = [_sum_chips("sum_chips_" + nm, pt, r2, chip_idx)
                    for nm, pt, r2 in zip(names[1:], parts_early, recv_early)]
    share_early = _to_sibling(halves_early, other_half=False)

    n_first = 1 if nb_o2 > 1 else 0
    if n_first:
        (part_a,), others_early = own_rows("w_in_grad_own_a", 0, n_first, share_early)
        scatter_a = _scatter_partials([part_a], rows=hh_in)
        (part_b,), land_a = own_rows("w_in_grad_own_b", n_first, nb_o2 - n_first, scatter_a)
        scatter_b = _scatter_partials([part_b], rows=hh_in, row0=n_first * tm_o2, prev=land_a)
    else:
        (part_b,), others_early = own_rows("w_in_grad_own", 0, nb_o2, share_early)
        scatter_b = _scatter_partials([part_b])

    tn_h = _pick(d, 2048, LANES)
    (dh,), recv_in = _mm(
        "proj_bwd", grid=(s // tm, d // tn_h, p // tn_p), operands=[dpa, dgl, wg_in],
        comm=scatter_b,
        in_specs=[pl.BlockSpec((tm, tn_p), lambda i, j, k: (i, jnp.minimum(k, na - 1))),
                  pl.BlockSpec((None, tm, tn_p),
                               lambda i, j, k: (jnp.maximum(k - na, 0) // qg, i, jnp.maximum(k - na, 0) % qg)),
                  pl.BlockSpec((None, tn_h, tn_p), lambda i, j, k: (k // qp, j, k % qp))],
        out_shape=[jax.ShapeDtypeStruct((s, d), F32)],
        out_specs=[pl.BlockSpec((tm, tn_h), lambda i, j, k: (i, j))],
        pairs=[(0, 2, 0, lambda ids: ids[2] < na), (1, 2, 0, lambda ids: ids[2] >= na)],
        dims=NT, acc_shapes=[(tm, tn_h)],
        epilogue=lambda accs, ins, outs, ids: outs[0].__setitem__(Ellipsis, accs[0]),
    )

    grad_x, g_nw = _rms_bwd(x2d, dh, dx2, norm_w)

    if n_first:
        half_in = _sum_chips("sum_chips_in_a", part_a, recv_in[0], chip_idx)
        half_in = _sum_chips("sum_chips_in_b", part_b, recv_in[0], chip_idx, row0=n_first * tm_o2, prev=half_in)
    else:
        half_in = _sum_chips("sum_chips_in", part_b, recv_in[0], chip_idx)
    halves = [half_in] + halves_early
    others = _run_exchange("pair_share", _to_sibling([half_in], other_half=False)) + list(others_early)

    g_cw_full = jnp.transpose(g_cw, (1, 0, 2)).reshape(CONV_K, w)
    small_shapes = [(LANES,), (1, d), (1, d), (1, w), (1, w), (CONV_K, w), (2, d)]
    pack = _flat_pack([loss_part[0], g_nw, g_fnw, g_ps, g_cb, g_cw_full, g_gb])
    total = _sum_devices(_gather_packs(pack))
    t_loss, t_nw, t_fnw, t_ps, t_cb, t_cw, t_gb = _unpack(total, small_shapes)
    loss = t_loss[0]
    t_cw = lax.dynamic_slice_in_dim(t_cw, chip * cg, cg, axis=1)
    t_gb = lax.dynamic_slice_in_dim(t_gb, chip * dq, dq, axis=1)

    out = {}
    big_names = ["w_in", "w_out", "w_branch", "pool_w"]
    big_m = [m_w_in, m_w_out, m_w_branch, m_pool_w]
    big_v = [v_w_in, v_w_out, v_w_branch, v_pool_w]
    big_orig = [w_in, w_out, w_branch, pool_w]
    for nm, g_own, g_other, w2, mm_, vv_, orig in zip(big_names, halves, others, big_w, big_m, big_v, big_orig):
        sh = (2,) + g_own.shape
        res = _adamw_halves("adamw_" + nm, g_own, g_other, c_idx,
                            w2.reshape(sh), mm_.reshape(sh), vv_.reshape(sh))
        out[nm] = [r.reshape(orig.shape) for r in res]

    sm_names = ["norm_w", "final_norm_w", "pool_scale", "conv_b", "conv_w", "gate_b"]
    sm_g = [t_nw, t_fnw, t_ps, t_cb, t_cw, t_gb]
    sm_w = [norm_w, final_norm_w, pool_scale, conv_b, conv_w, gate_b]
    sm_m = [m_norm_w, m_final_norm_w, m_pool_scale, m_conv_b, m_conv_w, m_gate_b]
    sm_v = [v_norm_w, v_final_norm_w, v_pool_scale, v_conv_b, v_conv_w, v_gate_b]
    sm_shapes = [a.shape for a in sm_w]
    res = _adamw("adamw_small", _flat_pack(sm_g), _flat_pack(sm_w), _flat_pack(sm_m), _flat_pack(sm_v))
    res = [_unpack(r, sm_shapes) for r in res]
    for idx, nm in enumerate(sm_names):
        out[nm] = [r[idx] for r in res]

    order = ["norm_w", "w_in", "pool_w", "pool_scale", "conv_w", "conv_b", "gate_b", "w_branch", "w_out",
             "final_norm_w"]
    outs = [loss, grad_x.reshape(x.shape)]
    for kind in range(4):
        outs += [out[nm][kind] for nm in order]
    return tuple(outs)
```

```python
import math

import jax
import jax.numpy as jnp
from jax import lax
from jax.experimental import pallas as pl
from jax.experimental.pallas import tpu as pltpu

F32 = jnp.float32
BF16 = jnp.bfloat16

NORM_EPS = 1e-6
POOL_WINDOWS = (2, 4, 8, 16)
N_POOL_GROUPS = len(POOL_WINDOWS)
CONV_K = 3
ADAM_LR = 0.001
ADAM_B1 = 0.9
ADAM_B2 = 0.999
ADAM_EPS = 1e-08
ADAM_WD = 0.01
ADAM_STEP = 10

N_CHIPS = 4
N_DEV = 8
HALO = 16
LANES = 128
V7X_VMEM_BYTES = 64 * 1024 * 1024
VMEM_CAP = V7X_VMEM_BYTES - 8 * 1024 * 1024

MESH = pl.DeviceIdType.MESH
ANY = pl.BlockSpec(memory_space=pl.ANY)

NN = (((1,), (0,)), ((), ()))
NT = (((1,), (1,)), ((), ()))
TN = (((0,), (0,)), ((), ()))


def _pick(dim, pref, align):
    if dim <= pref:
        return dim
    t = (pref // align) * align
    while t >= align:
        if dim % t == 0:
            return t
        t -= align
    raise ValueError(f"no tile for {dim} (pref {pref}, align {align})")


def _nbytes(shape, dtype):
    n = 1
    for s in shape:
        if s is not None:
            n *= s
    return n * jnp.dtype(dtype).itemsize


def _params(semantics, block_bytes, extra_bytes=0):
    need = 2 * block_bytes + extra_bytes + (2 << 20)
    return pltpu.CompilerParams(dimension_semantics=semantics,
                                vmem_limit_bytes=int(min(max(need, 16 << 20), VMEM_CAP)))


def _sigmoid(z):
    return jax.nn.sigmoid(z)


def _mm(name, *, grid, operands, in_specs, out_shape, out_specs, pairs, dims, acc_shapes, epilogue,
        semantics=None, temp_bytes=0, comm=None, prefetch=None, aliases=None):
    aliases = dict(aliases or {})
    n_in, n_out = len(operands), len(out_shape)
    kax = len(grid) - 1
    nk = grid[kax]
    single = nk == 1
    conditional = any(p[3] is not None for p in pairs)
    if single and conditional:
        assert len(acc_shapes) == 1 and all(p[3] is not None for p in pairs)
    n_acc = 0 if single else len(acc_shapes)
    n_pf = 0 if prefetch is None else 1
    c_ops = list(comm.operands) if comm else []
    c_out = list(comm.out_shape) if comm else []
    c_sems = list(comm.scratch) if comm else []
    c_alias = dict(comm.aliases) if comm else {}
    n_ci, n_co = len(c_ops), len(c_out)

    def product(ins, ai, bi):
        return lax.dot_general(ins[ai][...], ins[bi][...], dims, preferred_element_type=F32)

    def body(*refs):
        refs = refs[n_pf:]
        ins = refs[:n_in]
        c_in_refs = refs[n_in:n_in + n_ci]
        outs = refs[n_in + n_ci:n_in + n_ci + n_out]
        c_out_refs = refs[n_in + n_ci + n_out:n_in + n_ci + n_out + n_co]
        scratch = refs[n_in + n_ci + n_out + n_co:]
        accs = scratch[:n_acc]
        sems = scratch[n_acc:]
        ids = [pl.program_id(a) for a in range(len(grid))]
        k = ids[kax]

        if comm:
            is_first = ids[0] == 0
            is_last = ids[0] == grid[0] - 1
            for a in range(1, len(grid)):
                is_first = jnp.logical_and(is_first, ids[a] == 0)
                is_last = jnp.logical_and(is_last, ids[a] == grid[a] - 1)

            @pl.when(is_first)
            def _():
                comm.start(c_in_refs, c_out_refs, sems)

        if single and conditional:
            for (ai, bi, ci, cond) in pairs:
                def only(ai=ai, bi=bi):
                    epilogue([product(ins, ai, bi)], ins, outs, ids)
                pl.when(cond(ids))(only)
        elif single:
            vals = [None] * len(acc_shapes)
            for (ai, bi, ci, cond) in pairs:
                r = product(ins, ai, bi)
                vals[ci] = r if vals[ci] is None else vals[ci] + r
            epilogue(vals, ins, outs, ids)
        else:
            @pl.when(k == 0)
            def _():
                for a in accs:
                    a[...] = jnp.zeros(a.shape, a.dtype)

            for (ai, bi, ci, cond) in pairs:
                def step(ai=ai, bi=bi, ci=ci):
                    accs[ci][...] += product(ins, ai, bi)
                if cond is None:
                    step()
                else:
                    pl.when(cond(ids))(step)

            @pl.when(k == nk - 1)
            def _():
                epilogue([a[...] for a in accs], ins, outs, ids)

        if comm:
            @pl.when(is_last)
            def _():
                comm.finish(c_in_refs, c_out_refs, sems)

    if semantics is None:
        semantics = ("parallel",) * kax + ("arbitrary",)
    if comm:
        semantics = ("arbitrary",) * len(grid)
    in_specs = [ANY if idx in aliases else spec for idx, spec in enumerate(in_specs)]
    blk = 0
    for idx, (spec, op) in enumerate(zip(in_specs, operands)):
        if idx not in aliases:
            blk += _nbytes(spec.block_shape, op.dtype)
    for spec, o in zip(out_specs, out_shape):
        blk += _nbytes(spec.block_shape, o.dtype)
    acc_bytes = sum(_nbytes(s, F32) for s in acc_shapes)
    io_alias = {n_pf + i: o for i, o in aliases.items()}
    io_alias.update({n_pf + n_in + i: n_out + o for i, o in c_alias.items()})
    all_in = list(in_specs) + [ANY] * n_ci
    all_out = list(out_specs) + [ANY] * n_co
    scratch_shapes = [pltpu.VMEM(s, F32) for s in acc_shapes[:n_acc]] + c_sems
    params = _params(semantics, blk, 3 * acc_bytes + temp_bytes)
    shapes = list(out_shape) + c_out
    if prefetch is None:
        call = pl.pallas_call(
            body, name=name, grid=grid, in_specs=all_in, out_specs=all_out, out_shape=shapes,
            scratch_shapes=scratch_shapes, input_output_aliases=io_alias, compiler_params=params)
        res = call(*operands, *c_ops)
    else:
        call = pl.pallas_call(
            body, name=name, out_shape=shapes, input_output_aliases=io_alias, compiler_params=params,
            grid_spec=pltpu.PrefetchScalarGridSpec(
                num_scalar_prefetch=1, grid=grid, in_specs=all_in, out_specs=all_out,
                scratch_shapes=scratch_shapes))
        res = call(prefetch, *operands, *c_ops)
    return res[:n_out], res[n_out:]


def _cast_bf16(name, w2d):
    r, c = w2d.shape
    tr = _pick(r, max(16, (4 << 20) // (4 * c)), 16)

    def body(w_ref, o_ref):
        o_ref[...] = w_ref[...].astype(BF16)

    return pl.pallas_call(
        body, name=name, grid=(r // tr,),
        in_specs=[pl.BlockSpec((tr, c), lambda i: (i, 0))],
        out_specs=pl.BlockSpec((tr, c), lambda i: (i, 0)),
        out_shape=jax.ShapeDtypeStruct((r, c), BF16),
        compiler_params=_params(("parallel",), tr * c * 6),
    )(w2d)


def _rms_fwd(x, norm_w):
    s, d = x.shape
    ts = _pick(s, 256, 16)

    def body(x_ref, w_ref, h_ref):
        xv = x_ref[...]
        rstd = lax.rsqrt(jnp.mean(xv * xv, axis=-1, keepdims=True) + NORM_EPS)
        h_ref[...] = (xv * rstd * w_ref[...]).astype(BF16)

    return pl.pallas_call(
        body, name="rms_fwd", grid=(s // ts,),
        in_specs=[pl.BlockSpec((ts, d), lambda i: (i, 0)), pl.BlockSpec((1, d), lambda i: (0, 0))],
        out_specs=pl.BlockSpec((ts, d), lambda i: (i, 0)),
        out_shape=jax.ShapeDtypeStruct((s, d), BF16),
        compiler_params=_params(("parallel",), ts * d * 6, 3 * ts * d * 4),
    )(x, norm_w)


def _head(x, o, target, fnw):
    s, d = x.shape
    ts = _pick(s, 128, 16)

    def body(x_ref, o_ref, t_ref, w_ref, dx_ref, dxb_ref, loss_ref, gw_ref):
        i = pl.program_id(0)

        @pl.when(i == 0)
        def _():
            loss_ref[...] = jnp.zeros(loss_ref.shape, F32)
            gw_ref[...] = jnp.zeros(gw_ref.shape, F32)

        w = w_ref[...]
        x2 = x_ref[...] + o_ref[...]
        rstd = lax.rsqrt(jnp.mean(x2 * x2, axis=-1, keepdims=True) + NORM_EPS)
        n = x2 * rstd
        e = n * w - t_ref[...]
        loss_ref[...] += 0.5 * jnp.sum(e * e) / d
        dy = e / d
        gw_ref[...] += jnp.sum(dy * n, axis=0, keepdims=True)
        gy = dy * w
        dx = rstd * (gy - n * jnp.mean(gy * n, axis=-1, keepdims=True))
        dx_ref[...] = dx
        dxb_ref[...] = dx.astype(BF16)

    row = pl.BlockSpec((ts, d), lambda i: (i, 0))
    return pl.pallas_call(
        body, name="head", grid=(s // ts,),
        in_specs=[row, row, row, pl.BlockSpec((1, d), lambda i: (0, 0))],
        out_specs=[row, row, pl.BlockSpec((8, LANES), lambda i: (0, 0)),
                   pl.BlockSpec((1, d), lambda i: (0, 0))],
        out_shape=[jax.ShapeDtypeStruct((s, d), F32), jax.ShapeDtypeStruct((s, d), BF16),
                   jax.ShapeDtypeStruct((8, LANES), F32), jax.ShapeDtypeStruct((1, d), F32)],
        compiler_params=_params(("arbitrary",), ts * d * 22, 6 * ts * d * 4),
    )(x, o, target, fnw)


def _rms_bwd(x, dh, dx2, norm_w):
    s, d = x.shape
    ts = _pick(s, 128, 16)

    def body(x_ref, dh_ref, dx2_ref, w_ref, gx_ref, gw_ref):
        i = pl.program_id(0)

        @pl.when(i == 0)
        def _():
            gw_ref[...] = jnp.zeros(gw_ref.shape, F32)

        xv = x_ref[...]
        rstd = lax.rsqrt(jnp.mean(xv * xv, axis=-1, keepdims=True) + NORM_EPS)
        n = xv * rstd
        dhv = dh_ref[...]
        gw_ref[...] += jnp.sum(dhv * n, axis=0, keepdims=True)
        gh = dhv * w_ref[...]
        gx_ref[...] = dx2_ref[...] + rstd * (gh - n * jnp.mean(gh * n, axis=-1, keepdims=True))

    row = pl.BlockSpec((ts, d), lambda i: (i, 0))
    vec = pl.BlockSpec((1, d), lambda i: (0, 0))
    return pl.pallas_call(
        body, name="rms_bwd", grid=(s // ts,),
        in_specs=[row, row, row, vec], out_specs=[row, vec],
        out_shape=[jax.ShapeDtypeStruct((s, d), F32), jax.ShapeDtypeStruct((1, d), F32)],
        compiler_params=_params(("arbitrary",), ts * d * 16, 5 * ts * d * 4),
    )(x, dh, dx2, norm_w)


def _silu(z):
    return z * _sigmoid(z)


def _window_sum(ext, window, back):
    n = ext.shape[0]
    acc = ext
    step = 1
    while step < window:
        acc = acc + pltpu.roll(acc, step if back else n - step, 0)
        step *= 2
    return acc


def _shift_rows(ext, k, back):
    n = ext.shape[0]
    return pltpu.roll(ext, k if back else n - k, 0)


def _mix_fwd(proj, pool_wg, pool_scale, conv_wg, conv_b, width):
    s = proj.shape[0]
    w = width
    cg = w // N_POOL_GROUPS
    ts = _pick(s, 128, HALO)
    hb = ts // HALO
    cols = 6 * w

    def body(p_ref, ph_ref, pw_ref, ps_ref, cw_ref, cb_ref, ys_ref):
        i = pl.program_id(0)
        first = i == 0
        t1 = (i * ts + lax.broadcasted_iota(jnp.int32, (ts, 1), 0) + 1).astype(F32)

        def tile(part, g):
            lo = part * w + g * cg
            return p_ref[:, lo:lo + cg].astype(F32)

        def prev(part, g):
            lo = part * w + g * cg
            return jnp.where(first, 0.0, ph_ref[:, lo:lo + cg].astype(F32))

        for g, win in enumerate(POOL_WINDOWS):
            gs = slice(g * cg, (g + 1) * cg)
            u = tile(0, g)
            ext = jnp.concatenate([prev(0, g), u], axis=0)
            wsum = _window_sum(ext, win, True)[HALO:]
            pooled = wsum / jnp.minimum(t1, float(win)) - u
            pw = pw_ref[:, g].reshape(cg, cg)
            mixed = jnp.dot(pooled.astype(BF16), pw, preferred_element_type=F32)
            ys_ref[0, :, gs] = (mixed * ps_ref[:, gs] * _silu(tile(1, g))).astype(BF16)
            v = tile(4, g) * tile(2, g)
            vext = jnp.concatenate([prev(4, g) * prev(2, g), v], axis=0)
            v1 = _shift_rows(vext, 1, True)[HALO:]
            v2 = _shift_rows(vext, 2, True)[HALO:]
            cw = [cw_ref[g, tap:tap + 1, :] for tap in range(CONV_K)]
            y = cb_ref[:, gs] + cw[0] * v2 + cw[1] * v1 + cw[2] * v
            ys_ref[1, :, gs] = (tile(3, g) * y * _silu(tile(5, g))).astype(BF16)

    return pl.pallas_call(
        body, name="mix_fwd", grid=(s // ts,),
        in_specs=[pl.BlockSpec((ts, cols), lambda i: (i, 0)),
                  pl.BlockSpec((HALO, cols), lambda i: (jnp.maximum(i * hb - 1, 0), 0)),
                  pl.BlockSpec(pool_wg.shape, lambda i: (0, 0, 0, 0)),
                  pl.BlockSpec((1, w), lambda i: (0, 0)),
                  pl.BlockSpec(conv_wg.shape, lambda i: (0, 0, 0)),
                  pl.BlockSpec((1, w), lambda i: (0, 0))],
        out_specs=pl.BlockSpec((2, ts, w), lambda i: (0, i, 0)),
        out_shape=jax.ShapeDtypeStruct((2, s, w), BF16),
        compiler_params=_params(("parallel",), (ts + HALO) * cols * 2 + 2 * ts * w * 2
                                + _nbytes(pool_wg.shape, BF16), 24 * (ts + HALO) * cg * 4),
    )(proj, proj, pool_wg, pool_scale, conv_wg, conv_b)


def _mix_bwd(proj, dys, pool_wg, pool_scale, conv_wg, conv_b, width):
    s = proj.shape[0]
    w = width
    cg = w // N_POOL_GROUPS
    ts = _pick(s, 128, HALO)
    hb = ts // HALO
    n_tiles = s // ts
    last_hb = s // HALO - 1
    cols = 6 * w

    def body(p_ref, ph_ref, pn_ref, dy_ref, dyn_ref, pw_ref, ps_ref, cw_ref, cb_ref,
             dp_ref, dpw_ref, dps_ref, dcw_ref, dcb_ref):
        i = pl.program_id(0)
        first = i == 0
        last = i == n_tiles - 1

        @pl.when(first)
        def _():
            dpw_ref[...] = jnp.zeros(dpw_ref.shape, F32)
            dps_ref[...] = jnp.zeros(dps_ref.shape, F32)
            dcw_ref[...] = jnp.zeros(dcw_ref.shape, F32)
            dcb_ref[...] = jnp.zeros(dcb_ref.shape, F32)

        row = i * ts + lax.broadcasted_iota(jnp.int32, (ts + HALO, 1), 0)
        t1_ext = (row + 1).astype(F32)
        t1 = t1_ext[:ts]

        def tile(part, g):
            lo = part * w + g * cg
            return p_ref[:, lo:lo + cg].astype(F32)

        def prev(part, g):
            lo = part * w + g * cg
            return jnp.where(first, 0.0, ph_ref[:, lo:lo + cg].astype(F32))

        def ahead(part, g):
            lo = part * w + g * cg
            return jnp.concatenate([tile(part, g), pn_ref[:, lo:lo + cg].astype(F32)], axis=0)

        def dy_ahead(n, g):
            gs = slice(g * cg, (g + 1) * cg)
            nxt = jnp.where(last, 0.0, dyn_ref[n, :, gs])
            return jnp.concatenate([dy_ref[n, :, gs], nxt], axis=0)

        for g, win in enumerate(POOL_WINDOWS):
            gs = slice(g * cg, (g + 1) * cg)
            u = tile(0, g)
            ext = jnp.concatenate([prev(0, g), u], axis=0)
            pooled = _window_sum(ext, win, True)[HALO:] / jnp.minimum(t1, float(win)) - u
            pooled_b = pooled.astype(BF16)
            pw = pw_ref[:, g].reshape(cg, cg)
            mixed = jnp.dot(pooled_b, pw, preferred_element_type=F32)
            zp_ext = ahead(1, g)
            dy0_ext = dy_ahead(0, g)
            scale = ps_ref[:, gs]
            sp_ext = _sigmoid(zp_ext)
            dms_ext = dy0_ext * (zp_ext * sp_ext)
            dmix_b = (dms_ext * scale).astype(BF16)
            dpooled_ext = lax.dot_general(dmix_b, pw, NT, preferred_element_type=F32)
            dy0 = dy0_ext[:ts]
            zp, sp = zp_ext[:ts], sp_ext[:ts]
            dsilu_p = sp * (1.0 + zp * (1.0 - sp))
            dp_ref[:, w + g * cg:w + (g + 1) * cg] = (dy0 * mixed * scale * dsilu_p).astype(BF16)
            dps_ref[:, gs] += jnp.sum(dms_ext[:ts] * mixed, axis=0, keepdims=True)
            dpw = lax.dot_general(pooled_b, dmix_b[:ts], TN, preferred_element_type=F32)
            dpw_ref[g // 2, :, g % 2] += dpw.reshape(N_CHIPS, cg // N_CHIPS, cg)
            q_ext = dpooled_ext / jnp.minimum(t1_ext, float(win))
            du = _window_sum(q_ext, win, False)[:ts] - dpooled_ext[:ts]
            dp_ref[:, gs] = du.astype(BF16)
            uc = tile(2, g)
            cc = tile(4, g)
            v = cc * uc
            vext = jnp.concatenate([prev(4, g) * prev(2, g), v], axis=0)
            v1 = _shift_rows(vext, 1, True)[HALO:]
            v2 = _shift_rows(vext, 2, True)[HALO:]
            cw = [cw_ref[g, tap:tap + 1, :] for tap in range(CONV_K)]
            y = cb_ref[:, gs] + cw[0] * v2 + cw[1] * v1 + cw[2] * v
            bc_ext = ahead(3, g)
            zc_ext = ahead(5, g)
            dy1_ext = dy_ahead(1, g)
            sc_ext = _sigmoid(zc_ext)
            silu_c_ext = zc_ext * sc_ext
            dyy_ext = dy1_ext * bc_ext * silu_c_ext
            dy1 = dy1_ext[:ts]
            bc = bc_ext[:ts]
            zc, sc = zc_ext[:ts], sc_ext[:ts]
            dsilu_c = sc * (1.0 + zc * (1.0 - sc))
            dp_ref[:, 3 * w + g * cg:3 * w + (g + 1) * cg] = (dy1 * y * silu_c_ext[:ts]).astype(BF16)
            dp_ref[:, 5 * w + g * cg:5 * w + (g + 1) * cg] = (dy1 * bc * y * dsilu_c).astype(BF16)
            dyy = dyy_ext[:ts]
            dcb_ref[:, gs] += jnp.sum(dyy, axis=0, keepdims=True)
            for tap, vt in enumerate((v2, v1, v)):
                dcw_ref[g, tap:tap + 1, :] += jnp.sum(dyy * vt, axis=0, keepdims=True)
            dv = (cw[2] * dyy + cw[1] * _shift_rows(dyy_ext, 1, False)[:ts]
                  + cw[0] * _shift_rows(dyy_ext, 2, False)[:ts])
            dp_ref[:, 4 * w + g * cg:4 * w + (g + 1) * cg] = (dv * uc).astype(BF16)
            dp_ref[:, 2 * w + g * cg:2 * w + (g + 1) * cg] = (dv * cc).astype(BF16)

    dpw_shape = (2, N_CHIPS, 2, cg // N_CHIPS, cg)
    return pl.pallas_call(
        body, name="mix_bwd", grid=(n_tiles,),
        in_specs=[pl.BlockSpec((ts, cols), lambda i: (i, 0)),
                  pl.BlockSpec((HALO, cols), lambda i: (jnp.maximum(i * hb - 1, 0), 0)),
                  pl.BlockSpec((HALO, cols), lambda i: (jnp.minimum((i + 1) * hb, last_hb), 0)),
                  pl.BlockSpec((2, ts, w), lambda i: (0, i, 0)),
                  pl.BlockSpec((2, HALO, w), lambda i: (0, jnp.minimum((i + 1) * hb, last_hb), 0)),
                  pl.BlockSpec(pool_wg.shape, lambda i: (0, 0, 0, 0)),
                  pl.BlockSpec((1, w), lambda i: (0, 0)),
                  pl.BlockSpec(conv_wg.shape, lambda i: (0, 0, 0)),
                  pl.BlockSpec((1, w), lambda i: (0, 0))],
        out_specs=[pl.BlockSpec((ts, cols), lambda i: (i, 0)),
                   pl.BlockSpec(dpw_shape, lambda i: (0, 0, 0, 0, 0)),
                   pl.BlockSpec((1, w), lambda i: (0, 0)),
                   pl.BlockSpec(conv_wg.shape, lambda i: (0, 0, 0)),
                   pl.BlockSpec((1, w), lambda i: (0, 0))],
        out_shape=[jax.ShapeDtypeStruct((s, cols), BF16), jax.ShapeDtypeStruct(dpw_shape, F32),
                   jax.ShapeDtypeStruct((1, w), F32), jax.ShapeDtypeStruct(conv_wg.shape, F32),
                   jax.ShapeDtypeStruct((1, w), F32)],
        compiler_params=_params(("arbitrary",), (2 * ts + 2 * HALO) * cols * 2 + (ts + HALO) * w * 8
                                + _nbytes(pool_wg.shape, BF16) + _nbytes(dpw_shape, F32),
                                40 * (ts + HALO) * cg * 4),
    )(proj, proj, proj, dys, dys, pool_wg, pool_scale, conv_wg, conv_b)


def _adamw(name, g, w, m, v):
    r, c = w.shape
    tr = _pick(r, max(8, (1 << 20) // (4 * c)), 8)

    def body(g_ref, w_ref, m_ref, v_ref, go_ref, d_ref, mo_ref, vo_ref):
        gv = g_ref[...]
        mn = ADAM_B1 * m_ref[...] + (1.0 - ADAM_B1) * gv
        vn = ADAM_B2 * v_ref[...] + (1.0 - ADAM_B2) * (gv * gv)
        m_hat = mn / (1.0 - ADAM_B1 ** ADAM_STEP)
        v_hat = vn / (1.0 - ADAM_B2 ** ADAM_STEP)
        go_ref[...] = gv
        d_ref[...] = -ADAM_LR * (m_hat / (jnp.sqrt(v_hat) + ADAM_EPS) + ADAM_WD * w_ref[...])
        mo_ref[...] = mn
        vo_ref[...] = vn

    blk = pl.BlockSpec((tr, c), lambda i: (i, 0))
    sh = jax.ShapeDtypeStruct((r, c), F32)
    return pl.pallas_call(
        body, name=name, grid=(r // tr,), in_specs=[blk] * 4, out_specs=[blk] * 4,
        out_shape=[sh] * 4, compiler_params=_params(("parallel",), tr * c * 32, 4 * tr * c * 4),
    )(g, w, m, v)


def _pair_add(name, g, r1, c_idx):
    _, r, c = g.shape
    tr = _pick(r, max(16, (2 << 20) // (2 * c)), 16)

    def body(c_ref, g_ref, r_ref, o_ref):
        o_ref[...] = (g_ref[...].astype(F32) + r_ref[...].astype(F32)).astype(BF16)

    return pl.pallas_call(
        body, name=name,
        grid_spec=pltpu.PrefetchScalarGridSpec(
            num_scalar_prefetch=1, grid=(r // tr,),
            in_specs=[pl.BlockSpec((None, tr, c), lambda i, cr: (cr[0], i, 0)),
                      pl.BlockSpec((tr, c), lambda i, cr: (i, 0))],
            out_specs=pl.BlockSpec((tr, c), lambda i, cr: (i, 0))),
        out_shape=jax.ShapeDtypeStruct((r, c), BF16),
        compiler_params=_params(("parallel",), tr * c * 6, 3 * tr * c * 4),
    )(c_idx, g, r1)


def _adamw_halves(name, g_own, g_other, c_idx, w, m, v):
    _, r, c = w.shape
    tr = _pick(r, max(8, (5 << 18) // (4 * c)), 8)

    def body(c_ref, go_ref, gt_ref, w_ref, m_ref, v_ref, g_out, d_ref, mo_ref, vo_ref):
        gv = jnp.where(pl.program_id(0) == c_ref[0], go_ref[...], gt_ref[...])
        mn = ADAM_B1 * m_ref[...] + (1.0 - ADAM_B1) * gv
        vn = ADAM_B2 * v_ref[...] + (1.0 - ADAM_B2) * (gv * gv)
        m_hat = mn / (1.0 - ADAM_B1 ** ADAM_STEP)
        v_hat = vn / (1.0 - ADAM_B2 ** ADAM_STEP)
        g_out[...] = gv
        d_ref[...] = -ADAM_LR * (m_hat / (jnp.sqrt(v_hat) + ADAM_EPS) + ADAM_WD * w_ref[...])
        mo_ref[...] = mn
        vo_ref[...] = vn

    blk = pl.BlockSpec((None, tr, c), lambda h, i, cr: (h, i, 0))
    sh = jax.ShapeDtypeStruct(w.shape, F32)
    return pl.pallas_call(
        body, name=name,
        grid_spec=pltpu.PrefetchScalarGridSpec(
            num_scalar_prefetch=1, grid=(2, r // tr),
            in_specs=[pl.BlockSpec((tr, c), lambda h, i, cr: (jnp.where(h == cr[0], i, 0), 0)),
                      pl.BlockSpec((tr, c), lambda h, i, cr: (jnp.where(h == cr[0], 0, i), 0)),
                      blk, blk, blk],
            out_specs=[blk] * 4),
        out_shape=[sh] * 4,
        compiler_params=_params(("arbitrary", "arbitrary"), tr * c * 36, 4 * tr * c * 4),
    )(c_idx, g_own, g_other, w, m, v)


def _sum_chips(name, part, r2, chip_idx, row0=0, prev=None):
    n, r, c = r2.shape
    rc = part.shape[1]
    tr = _pick(math.gcd(rc, row0) if row0 else rc, max(16, (1 << 20) // (2 * c)), 16)
    b0 = row0 // tr

    def body(ch_ref, own_ref, *rest):
        slots, o_ref = rest[:n], rest[-1]
        acc = None
        for s in range(n):
            term = jnp.where(ch_ref[0] == s, own_ref[...], slots[s][...]).astype(F32)
            acc = term if acc is None else acc + term
        o_ref[...] = acc

    def slot_spec(s):
        return pl.BlockSpec((None, tr, c), lambda i, ch: (jnp.where(ch[0] == s, (s + 1) % n, s), b0 + i, 0))

    extra = [] if prev is None else [prev]
    return pl.pallas_call(
        body, name=name,
        grid_spec=pltpu.PrefetchScalarGridSpec(
            num_scalar_prefetch=1, grid=(rc // tr,),
            in_specs=[pl.BlockSpec((None, tr, c), lambda i, ch: (ch[0], i, 0))]
            + [slot_spec(s) for s in range(n)] + [ANY] * len(extra),
            out_specs=pl.BlockSpec((tr, c), lambda i, ch: (b0 + i, 0))),
        out_shape=jax.ShapeDtypeStruct((r, c), F32),
        input_output_aliases={2 + n: 0} if extra else {},
        compiler_params=_params(("parallel",), tr * c * (2 * n + 6), 3 * tr * c * 4),
    )(chip_idx, part, *([r2] * n), *extra)


def _sum_devices(packs):
    n, r, c = packs.shape

    def body(p_ref, o_ref):
        acc = p_ref[0]
        for k in range(1, n):
            acc = acc + p_ref[k]
        o_ref[...] = acc

    return pl.pallas_call(
        body, name="sum_devices", out_shape=jax.ShapeDtypeStruct((r, c), F32),
        in_specs=[pl.BlockSpec(memory_space=pltpu.VMEM)],
        out_specs=pl.BlockSpec(memory_space=pltpu.VMEM),
    )(packs)


def _place():
    x, y, c = lax.axis_index("x"), lax.axis_index("y"), lax.axis_index("c")
    return x, y, c


def _chip_peers(x, y):
    out = []
    for k, (fx, fy) in enumerate(((0, 1), (1, 0), (1, 1))):
        px = 1 - x if fx else x
        py = 1 - y if fy else y
        out.append((k, px, py, 2 * px + py))
    return out


def _gather_weights(big, small, relay=False):
    nb, ns = len(big), len(small)

    class Copies:
        def __init__(self, c_in, c_out, sems):
            b_in, s_in = c_in[:nb], c_in[nb:]
            b_out, s_out = c_out[:nb], c_out[nb:]
            ici_s, ici_r, d2d_s, d2d_r, own_s, own_r = sems[:6]
            x, y, c = _place()
            chip = 2 * x + y
            sibling = (x, y, 1 - c)
            peers = _chip_peers(x, y)
            self.own = [pltpu.make_async_remote_copy(
                src_ref=b_in[t], dst_ref=b_out[t].at[chip], send_sem=own_s.at[t], recv_sem=own_r.at[t],
                device_id=sibling, device_id_type=MESH) for t in range(nb)]
            self.ici = [pltpu.make_async_remote_copy(
                src_ref=b_in[t].at[c], dst_ref=b_out[t].at[chip, c],
                send_sem=ici_s.at[t, k], recv_sem=ici_r.at[t, k],
                device_id=(px, py, c), device_id_type=MESH)
                for t in range(nb) for (k, px, py, pchip) in peers if not (relay and k == 2)]
            self.relay = []
            if relay:
                south = c == 0
                from_chip = jnp.where(south, 2 * x + (1 - y), 2 * (1 - x) + y)
                to = (jnp.where(south, 1 - x, x), jnp.where(south, y, 1 - y), c)
                self.relay = [pltpu.make_async_remote_copy(
                    src_ref=b_out[t].at[from_chip, c], dst_ref=b_out[t].at[from_chip, c],
                    send_sem=ici_s.at[t, 2], recv_sem=ici_r.at[t, 2],
                    device_id=to, device_id_type=MESH) for t in range(nb)]
            self.landed = [pltpu.make_async_remote_copy(
                src_ref=b_out[t].at[pchip, c], dst_ref=b_out[t].at[pchip, c],
                send_sem=ici_s.at[t, k], recv_sem=ici_r.at[t, k],
                device_id=sibling, device_id_type=MESH)
                for t in range(nb) for (k, px, py, pchip) in peers]
            self.passed = [pltpu.make_async_remote_copy(
                src_ref=b_out[t].at[pchip, c], dst_ref=b_out[t].at[pchip, c],
                send_sem=d2d_s.at[t, k], recv_sem=d2d_r.at[t, k],
                device_id=sibling, device_id_type=MESH)
                for t in range(nb) for (k, px, py, pchip) in peers]
            self.from_sibling = [pltpu.make_async_remote_copy(
                src_ref=b_out[t].at[pchip, 1 - c], dst_ref=b_out[t].at[pchip, 1 - c],
                send_sem=d2d_s.at[t, k], recv_sem=d2d_r.at[t, k],
                device_id=sibling, device_id_type=MESH)
                for t in range(nb) for (k, px, py, pchip) in peers]
            self.small, self.small_landed, self.local = [], [], []
            if ns:
                sm_s, sm_r, loc = sems[6:]
                self.local = [pltpu.make_async_copy(s_in[t], s_out[t].at[chip], loc.at[t]) for t in range(ns)]
                self.small = [pltpu.make_async_remote_copy(
                    src_ref=s_in[t], dst_ref=s_out[t].at[chip],
                    send_sem=sm_s.at[t, k], recv_sem=sm_r.at[t, k],
                    device_id=(px, py, c), device_id_type=MESH)
                    for t in range(ns) for (k, px, py, pchip) in peers]
                self.small_landed = [pltpu.make_async_remote_copy(
                    src_ref=s_in[t], dst_ref=s_out[t].at[pchip],
                    send_sem=sm_s.at[t, k], recv_sem=sm_r.at[t, k],
                    device_id=sibling, device_id_type=MESH)
                    for t in range(ns) for (k, px, py, pchip) in peers]

    def start(c_in, c_out, sems):
        cps = Copies(c_in, c_out, sems)
        for cp in cps.local + cps.own + cps.ici + cps.small:
            cp.start()

    def finish(c_in, c_out, sems):
        cps = Copies(c_in, c_out, sems)
        for t in range(nb):
            for k in range(3):
                if relay and k == 2:
                    cps.relay[t].start()
                cps.landed[3 * t + k].wait_recv()
                cps.passed[3 * t + k].start()
        for cp in cps.small_landed + cps.from_sibling:
            cp.wait_recv()
        for cp in cps.ici + cps.relay + cps.small + cps.passed:
            cp.wait_send()
        for cp in cps.own + cps.local:
            cp.wait()

    out_shape = [jax.ShapeDtypeStruct((N_CHIPS,) + b.shape, b.dtype) for b in big]
    out_shape += [jax.ShapeDtypeStruct((N_CHIPS,) + s.shape, s.dtype) for s in small]
    dma = pltpu.SemaphoreType.DMA
    scratch = [dma((nb, 3)), dma((nb, 3)), dma((nb, 3)), dma((nb, 3)), dma((nb,)), dma((nb,))]
    if ns:
        scratch += [dma((ns, 3)), dma((ns, 3)), dma((ns,))]
    ex = _Exchange(list(big) + list(small), out_shape, scratch, start, finish)
    ex.copies = Copies
    return ex


class _Exchange:
    def __init__(self, operands, out_shape, scratch, start, finish, aliases=None):
        self.operands, self.out_shape, self.scratch = operands, out_shape, scratch
        self.start, self.finish, self.aliases = start, finish, dict(aliases or {})


def _run_exchange(name, ex):
    n_i, n_o = len(ex.operands), len(ex.out_shape)

    def body(*refs):
        ins, outs, sems = refs[:n_i], refs[n_i:n_i + n_o], refs[n_i + n_o:]
        ex.start(ins, outs, sems)
        ex.finish(ins, outs, sems)

    return list(pl.pallas_call(
        body, name=name, out_shape=list(ex.out_shape), in_specs=[ANY] * n_i, out_specs=[ANY] * n_o,
        scratch_shapes=list(ex.scratch), input_output_aliases=ex.aliases,
    )(*ex.operands))


def _to_sibling(arrays, other_half):
    n = len(arrays)

    def copies(a_in, a_out, sems):
        send, recv = sems
        x, y, c = _place()
        return [pltpu.make_async_remote_copy(
            src_ref=a_in[t].at[1 - c] if other_half else a_in[t], dst_ref=a_out[t],
            send_sem=send.at[t], recv_sem=recv.at[t],
            device_id=(x, y, 1 - c), device_id_type=MESH) for t in range(n)]

    def start(a_in, a_out, sems):
        for cp in copies(a_in, a_out, sems):
            cp.start()

    def finish(a_in, a_out, sems):
        for cp in copies(a_in, a_out, sems):
            cp.wait()

    dma = pltpu.SemaphoreType.DMA
    shapes = [jax.ShapeDtypeStruct(a.shape[1:] if other_half else a.shape, a.dtype) for a in arrays]
    return _Exchange(list(arrays), shapes, [dma((n,)), dma((n,))], start, finish)


def _scatter_partials(parts, rows=None, row0=0, prev=None):
    n = len(parts)
    land = [(p.shape[0], p.shape[1] if rows is None else rows, p.shape[2]) for p in parts]

    def window(ref, slot, t):
        return ref.at[slot, pl.ds(row0, parts[t].shape[1])]

    def sends(p_in, r_out, sems):
        send, recv = sems
        x, y, c = _place()
        chip = 2 * x + y
        return [pltpu.make_async_remote_copy(
            src_ref=p_in[t].at[pchip], dst_ref=window(r_out[t], chip, t),
            send_sem=send.at[t, k], recv_sem=recv.at[t, k],
            device_id=(px, py, c), device_id_type=MESH)
            for t in range(n) for (k, px, py, pchip) in _chip_peers(x, y)]

    def start(p_in, r_out, sems):
        for cp in sends(p_in, r_out, sems):
            cp.start()

    def finish(p_in, r_out, sems):
        send, recv = sems
        x, y, c = _place()
        for t in range(n):
            for (k, px, py, pchip) in _chip_peers(x, y):
                pltpu.make_async_remote_copy(
                    src_ref=p_in[t].at[pchip], dst_ref=window(r_out[t], pchip, t),
                    send_sem=send.at[t, k], recv_sem=recv.at[t, k],
                    device_id=(px, py, c), device_id_type=MESH).wait_recv()
        for cp in sends(p_in, r_out, sems):
            cp.wait_send()

    dma = pltpu.SemaphoreType.DMA
    operands = list(parts) + (list(prev) if prev else [])
    return _Exchange(operands, [jax.ShapeDtypeStruct(sh, p.dtype) for sh, p in zip(land, parts)],
                     [dma((n, 3)), dma((n, 3))], start, finish,
                     aliases={n + t: t for t in range(n)} if prev else None)


def _w_in_grad_sibling(h, dpa, dgl, c_idx, carried, *, tm, tn, rows, pq):
    s, d = h.shape
    wa, wg = dpa.shape[1], dgl.shape[2]
    p = wa + 2 * wg
    ni, nj = rows // tm, p // tn
    na, qg, qp = wa // tn, wg // tn, pq // tn
    n_steps = ni * nj
    n_ci, n_co = len(carried.operands), len(carried.out_shape)

    def body(c_ref, h_ref, a_ref, g_ref, *rest):
        c_in = rest[:n_ci]
        r1_ref = rest[n_ci]
        c_out = rest[n_ci + 1:n_ci + 1 + n_co]
        slots, send, recv = rest[n_ci + 1 + n_co:n_ci + 4 + n_co]
        sems = rest[n_ci + 4 + n_co:]
        i, j = pl.program_id(0), pl.program_id(1)
        step = i * nj + j
        slot = lax.rem(step, 2)
        x, y, c = _place()
        sibling = (x, y, 1 - c)

        @pl.when(step == 0)
        def _():
            carried.start(c_in, c_out, sems)

        def tile_copy(sl):
            return pltpu.make_async_remote_copy(
                src_ref=slots.at[sl],
                dst_ref=r1_ref.at[j // qp, pl.ds(i * tm, tm), pl.ds((j % qp) * tn, tn)],
                send_sem=send.at[sl], recv_sem=recv, device_id=sibling, device_id_type=MESH)

        @pl.when(step >= 2)
        def _():
            tile_copy(slot).wait_send()

        def emit(b_ref):
            acc = lax.dot_general(h_ref[...], b_ref[...], TN, preferred_element_type=F32)
            slots[slot] = acc.astype(BF16)

        pl.when(j < na)(lambda: emit(a_ref))
        pl.when(j >= na)(lambda: emit(g_ref))
        tile_copy(slot).start()

        @pl.when(step == n_steps - 1)
        def _():
            tile_copy(0).wait_send()
            tile_copy(1).wait_send()
            pltpu.make_async_remote_copy(
                src_ref=r1_ref, dst_ref=r1_ref, send_sem=send.at[0], recv_sem=recv,
                device_id=sibling, device_id_type=MESH).wait_recv()
            carried.finish(c_in, c_out, sems)

    nbh = rows // tm
    dma = pltpu.SemaphoreType.DMA
    blk = s * tm * 2 + 2 * s * tn * 2
    res = pl.pallas_call(
        body, name="w_in_grad_sibling",
        grid_spec=pltpu.PrefetchScalarGridSpec(
            num_scalar_prefetch=1, grid=(ni, nj),
            in_specs=[pl.BlockSpec((s, tm), lambda i, j, cr: (0, (1 - cr[0]) * nbh + i)),
                      pl.BlockSpec((s, tn), lambda i, j, cr: (0, jnp.minimum(j, na - 1))),
                      pl.BlockSpec((None, s, tn),
                                   lambda i, j, cr: (jnp.maximum(j - na, 0) // qg, 0, jnp.maximum(j - na, 0) % qg))]
            + [ANY] * n_ci,
            out_specs=[ANY] * (1 + n_co),
            scratch_shapes=[pltpu.VMEM((2, tm, tn), BF16), dma((2,)), dma] + list(carried.scratch)),
        out_shape=[jax.ShapeDtypeStruct((N_CHIPS, rows, pq), BF16)] + list(carried.out_shape),
        compiler_params=_params(("arbitrary", "arbitrary"), blk, 2 * tm * tn * 2 + 2 * tm * tn * 4),
    )(c_idx, h, dpa, dgl, *carried.operands)
    return res[0], res[1:]


def _proj_gathering(h, wb, chip_idx, gather_rest, *, tm, tn):
    s, d = h.shape
    _, hh, pq = wb.shape
    qp, ni = pq // tn, s // tm
    n_tiles = N_CHIPS * qp
    n_ci, n_co, n_cs = len(gather_rest.operands), len(gather_rest.out_shape), len(gather_rest.scratch)

    def where_of(t):
        near = t - qp
        g = jnp.where(t < qp, 0, jnp.where(t < 3 * qp, 1 + lax.rem(near, 2), 3))
        jj = jnp.where(t < qp, t, jnp.where(t < 3 * qp, near // 2, t - 3 * qp))
        return g, jj

    def body(ch_ref, h_ref, own_ref, *rest):
        c_in = rest[:n_ci]
        o_ref, gathered = rest[n_ci], rest[n_ci + 1]
        c_out = rest[n_ci + 2:n_ci + 2 + n_co]
        slots, fetch_sem, ici_s, ici_r, d2d_s, d2d_r, own_s, own_r = rest[n_ci + 2 + n_co:n_ci + 10 + n_co]
        c_sem = rest[n_ci + 10 + n_co:]
        tile, i = pl.program_id(0), pl.program_id(1)
        x, y, c = _place()
        chip = 2 * x + y
        sibling = (x, y, 1 - c)

        def cols(jj):
            return pl.ds(jj * tn, tn)

        def remote(src, dst, send, recv, to):
            return pltpu.make_async_remote_copy(src_ref=src, dst_ref=dst, send_sem=send, recv_sem=recv,
                                                device_id=to, device_id_type=MESH)

        def own_shard():
            return remote(own_ref, gathered.at[chip], own_s, own_r, sibling)

        def to_neighbour(k, jj):
            to = (x, 1 - y, c) if k == 0 else (1 - x, y, c)
            return remote(own_ref.at[c, :, cols(jj)], gathered.at[chip, c, :, cols(jj)],
                          ici_s.at[k, jj], ici_r.at[k, jj], to)

        def relay(jj):
            south = c == 0
            from_chip = jnp.where(south, 2 * x + (1 - y), 2 * (1 - x) + y)
            to = (jnp.where(south, 1 - x, x), jnp.where(south, y, 1 - y), c)
            chunk = gathered.at[from_chip, c, :, cols(jj)]
            return remote(chunk, chunk, ici_s.at[2, jj], ici_r.at[2, jj], to)

        def chunk_of(k, jj, half):
            return gathered.at[jnp.bitwise_xor(chip, k + 1), half, :, cols(jj)]

        def landed(k, jj):
            return remote(chunk_of(k, jj, c), chunk_of(k, jj, c), ici_s.at[k, jj], ici_r.at[k, jj], sibling)

        def passed(k, jj):
            return remote(chunk_of(k, jj, c), chunk_of(k, jj, c), d2d_s.at[k, jj], d2d_r.at[k, jj], sibling)

        def from_sibling(k, jj):
            return remote(chunk_of(k, jj, 1 - c), chunk_of(k, jj, 1 - c), d2d_s.at[k, jj], d2d_r.at[k, jj], sibling)

        def fetch(src, slot):
            return pltpu.make_async_copy(src, slots.at[slot], fetch_sem.at[slot])

        @pl.when(jnp.logical_and(tile == 0, i == 0))
        def _():
            own_shard().start()
            for jj in range(qp):
                for k in range(2):
                    to_neighbour(k, jj).start()
            fetch(own_ref.at[:, :, cols(0)], 0).start()

        @pl.when(i == 0)
        def _():
            fetch(own_ref.at[:, :, cols(0)], lax.rem(tile, 2)).wait()

        last_row = i == ni - 1
        nxt = tile + 1
        g1, j1 = where_of(nxt)
        pass_row = i == max(ni - 2, 0)

        @pl.when(jnp.logical_and(pass_row, jnp.logical_and(nxt >= qp, nxt < n_tiles)))
        def _():
            landed(g1 - 1, j1).wait_recv()
            passed(g1 - 1, j1).start()

        @pl.when(jnp.logical_and(pass_row, nxt == 3 * qp - 1))
        def _():
            for ch in range(qp):
                relay(ch).start()
            gather_rest.start(c_in, c_out, c_sem)

        @pl.when(jnp.logical_and(last_row, nxt < qp))
        def _():
            fetch(own_ref.at[:, :, cols(nxt)], lax.rem(nxt, 2)).start()

        @pl.when(jnp.logical_and(last_row, jnp.logical_and(nxt >= qp, nxt < n_tiles)))
        def _():
            from_sibling(g1 - 1, j1).wait_recv()
            fetch(gathered.at[jnp.bitwise_xor(chip, g1), :, :, cols(j1)], lax.rem(nxt, 2)).start()

        w_tile = slots[lax.rem(tile, 2)].reshape(d, tn)
        o_ref[...] = jnp.dot(h_ref[...], w_tile, preferred_element_type=F32).astype(BF16)

        @pl.when(jnp.logical_and(tile == n_tiles - 1, last_row))
        def _():
            for jj in range(qp):
                for k in range(2):
                    to_neighbour(k, jj).wait_send()
                relay(jj).wait_send()
                for k in range(3):
                    passed(k, jj).wait_send()
            own_shard().wait()
            gather_rest.finish(c_in, c_out, c_sem)

    assert qp >= 2 and not gather_rest.aliases
    dma = pltpu.SemaphoreType.DMA
    p = N_CHIPS * pq

    def out_cols(t, i, ch):
        g, jj = where_of(t)
        return (i, jnp.bitwise_xor(ch[0], g) * qp + jj)

    res = pl.pallas_call(
        body, name="proj_gathering",
        grid_spec=pltpu.PrefetchScalarGridSpec(
            num_scalar_prefetch=1, grid=(n_tiles, ni),
            in_specs=[pl.BlockSpec((tm, d), lambda t, i, ch: (i, 0)), ANY] + [ANY] * n_ci,
            out_specs=[pl.BlockSpec((tm, tn), out_cols), ANY] + [ANY] * n_co,
            scratch_shapes=[pltpu.VMEM((2, 2, hh, tn), BF16), dma((2,)),
                            dma((3, qp)), dma((3, qp)), dma((3, qp)), dma((3, qp)), dma, dma]
            + list(gather_rest.scratch)),
        out_shape=[jax.ShapeDtypeStruct((s, p), BF16), jax.ShapeDtypeStruct((N_CHIPS,) + wb.shape, wb.dtype)]
        + list(gather_rest.out_shape),
        compiler_params=_params(("arbitrary",) * 2, tm * d * 2 + tm * tn * 2, 3 * d * tn * 2 + tm * tn * 4),
    )(chip_idx, h, wb, *gather_rest.operands)
    return res[0], res[1], res[2:]


def _gather_packs(pack):
    def body(p_ref, o_ref, send, recv, loc):
        x, y, c = _place()
        me = 4 * x + 2 * y + c
        mine = pltpu.make_async_copy(p_ref, o_ref.at[me], loc)
        mine.start()
        flips = [(fx, fy, fc) for fx in (0, 1) for fy in (0, 1) for fc in (0, 1)][1:]
        cps = []
        for k, (fx, fy, fc) in enumerate(flips):
            peer = (1 - x if fx else x, 1 - y if fy else y, 1 - c if fc else c)
            cps.append(pltpu.make_async_remote_copy(
                src_ref=p_ref, dst_ref=o_ref.at[me], send_sem=send.at[k], recv_sem=recv.at[k],
                device_id=peer, device_id_type=MESH))
        for cp in cps:
            cp.start()
        for k, (fx, fy, fc) in enumerate(flips):
            px, py, pc = (1 - x if fx else x, 1 - y if fy else y, 1 - c if fc else c)
            pltpu.make_async_remote_copy(
                src_ref=p_ref, dst_ref=o_ref.at[4 * px + 2 * py + pc],
                send_sem=send.at[k], recv_sem=recv.at[k],
                device_id=(px, py, pc), device_id_type=MESH).wait_recv()
        for cp in cps:
            cp.wait_send()
        mine.wait()

    dma = pltpu.SemaphoreType.DMA
    return pl.pallas_call(
        body, name="gather_packs", out_shape=jax.ShapeDtypeStruct((N_DEV,) + pack.shape, pack.dtype),
        in_specs=[ANY], out_specs=ANY, scratch_shapes=[dma((7,)), dma((7,)), dma],
    )(pack)


def _flat_pack(pieces):
    flat = jnp.concatenate([p.reshape(-1) for p in pieces])
    pad = (-flat.shape[0]) % (8 * LANES)
    flat = jnp.concatenate([flat, jnp.zeros((pad,), F32)])
    return flat.reshape(-1, LANES)


def _unpack(pack, shapes):
    flat = pack.reshape(-1)
    out, off = [], 0
    for sh in shapes:
        n = 1
        for s in sh:
            n *= s
        out.append(flat[off:off + n].reshape(sh))
        off += n
    return out


def kernel(x, norm_w, w_in, pool_w, pool_scale, conv_w, conv_b, gate_b, w_branch, w_out, final_norm_w, loss_target, m_norm_w, m_w_in, m_pool_w, m_pool_scale, m_conv_w, m_conv_b, m_gate_b, m_w_branch, m_w_out, m_final_norm_w, v_norm_w, v_w_in, v_pool_w, v_pool_scale, v_conv_w, v_conv_b, v_gate_b, v_w_branch, v_w_out, v_final_norm_w):
    _, s, d = x.shape
    w = d // 2
    cg = w // N_POOL_GROUPS
    p = 6 * w + 2 * d
    pq = p // N_CHIPS
    dq = d // N_CHIPS
    assert w_in.shape == (1, d, pq) and w_branch.shape == (1, 2, w, dq) and w_out.shape == (1, dq, d)
    assert pool_w.shape == (1, N_POOL_GROUPS, cg // N_CHIPS, cg) and conv_w.shape == (1, CONV_K, cg)

    x2d = x.reshape(s, d)
    tgt = loss_target.reshape(s, d)
    c_idx = lax.axis_index("c").astype(jnp.int32).reshape(1)
    chip = 2 * lax.axis_index("x") + lax.axis_index("y")

    big_w = [w_in.reshape(d, pq), w_out.reshape(dq, d), w_branch.reshape(2 * w, dq),
             pool_w.reshape(cg, cg)]
    names = ["in", "out", "branch", "pool"]
    big_b = [_cast_bf16("cast_" + nm, a) for nm, a in zip(names, big_w)]
    big_b = [b.reshape(2, b.shape[0] // 2, b.shape[1]) for b in big_b]
    chip_idx = chip.astype(jnp.int32).reshape(1)

    h = _rms_fwd(x2d, norm_w)

    tm = _pick(s, 1024, 16)
    tn_p = _pick(math.gcd(pq, 6 * w, d), 1024, LANES)
    qp = pq // tn_p
    tk_d = _pick(d, 4096, LANES)
    proj, wg_in, (wg_br, wg_pool, wg_cw, wg_gb) = _proj_gathering(
        h, big_b[0], chip_idx,
        _gather_weights(big_b[2:], [conv_w.reshape(CONV_K, cg), gate_b.reshape(2, dq)], relay=True),
        tm=tm, tn=tn_p)
    wg_in = wg_in.reshape(N_CHIPS, d, pq)
    wg_pool = wg_pool.reshape(N_CHIPS, N_POOL_GROUPS, cg // N_CHIPS, cg)

    ys = _mix_fwd(proj, wg_pool, pool_scale, wg_cw, conv_b, w)

    tn_d = _pick(dq, 1024, LANES)
    qd = dq // tn_d
    tk_w = _pick(w, 2048, LANES)
    tm_g = _pick(s, 512, 16)
    gl0 = (6 * w) // tn_d
    gl1 = (6 * w + d) // tn_d

    def gate_specs(im, rows):
        return [pl.BlockSpec((rows, tn_d), lambda *a: (im(*a)[0], gl0 + im(*a)[1])),
                pl.BlockSpec((rows, tn_d), lambda *a: (im(*a)[0], gl1 + im(*a)[1])),
                pl.BlockSpec((None, 2, tn_d), lambda *a: (im(*a)[1] // qd, 0, im(*a)[1] % qd))]

    def merge_epilogue(accs, ins, outs, ids):
        gb = ins[6][...]
        g0 = _sigmoid(ins[4][...].astype(F32) + gb[0:1])
        g1 = _sigmoid(ins[5][...].astype(F32) + gb[1:2])
        outs[0][...] = (g0 * accs[0] + g1 * accs[1]).astype(BF16)
        outs[1][0] = accs[0].astype(BF16)
        outs[1][1] = accs[1].astype(BF16)

    (merged, br), (wg_out,) = _mm(
        "branch_merge", grid=(s // tm_g, d // tn_d, w // tk_w), comm=_gather_weights(big_b[1:2], []),
        operands=[ys, wg_br, ys, wg_br, proj, proj, wg_gb],
        in_specs=[pl.BlockSpec((None, tm_g, tk_w), lambda i, j, k: (0, i, k)),
                  pl.BlockSpec((None, None, tk_w, tn_d), lambda i, j, k: (j // qd, 0, k, j % qd)),
                  pl.BlockSpec((None, tm_g, tk_w), lambda i, j, k: (1, i, k)),
                  pl.BlockSpec((None, None, tk_w, tn_d), lambda i, j, k: (j // qd, 1, k, j % qd)),
                  *gate_specs(lambda i, j, k: (i, j), tm_g)],
        out_shape=[jax.ShapeDtypeStruct((s, d), BF16), jax.ShapeDtypeStruct((2, s, d), BF16)],
        out_specs=[pl.BlockSpec((tm_g, tn_d), lambda i, j, k: (i, j)),
                   pl.BlockSpec((2, tm_g, tn_d), lambda i, j, k: (0, i, j))],
        pairs=[(0, 1, 0, None), (2, 3, 1, None)], dims=NN, acc_shapes=[(tm_g, tn_d)] * 2,
        epilogue=merge_epilogue, temp_bytes=6 * tm_g * tn_d * 4,
    )
    wg_out = wg_out.reshape(d, d)

    tn_f = _pick(d, 1024, LANES)
    o = _mm(
        "out_proj", grid=(s // tm, d // tn_f, d // tk_d), operands=[merged, wg_out],
        in_specs=[pl.BlockSpec((tm, tk_d), lambda i, j, k: (i, k)),
                  pl.BlockSpec((tk_d, tn_f), lambda i, j, k: (k, j))],
        out_shape=[jax.ShapeDtypeStruct((s, d), F32)],
        out_specs=[pl.BlockSpec((tm, tn_f), lambda i, j, k: (i, j))],
        pairs=[(0, 1, 0, None)], dims=NN, acc_shapes=[(tm, tn_f)],
        epilogue=lambda accs, ins, outs, ids: outs[0].__setitem__(Ellipsis, accs[0]),
    )[0][0]

    dx2, dx2b, loss_part, g_fnw =_head(x2d, o, tgt, final_norm_w.reshape(1, d))

    def gate_bwd_epilogue(accs, ins, outs, ids):
        dm = accs[0]
        gb = ins[5][...]
        i = ids[1]

        @pl.when(i == 0)
        def _():
            outs[2][...] = jnp.zeros(outs[2].shape, F32)

        for n in range(2):
            gate = _sigmoid(ins[3 + n][...].astype(F32) + gb[n:n + 1])
            outs[0][n] = (dm * gate).astype(BF16)
            dgl = dm * ins[2][n].astype(F32) * gate * (1.0 - gate)
            outs[1][n] = dgl.astype(BF16)
            outs[2][n:n + 1, :] += jnp.sum(dgl, axis=0, keepdims=True)

    tm_b = _pick(s, 256, 16)
    (d_br, dgl, g_gb), _ = _mm(
        "out_proj_bwd_gate", grid=(d // tn_d, s // tm_b, d // tk_d),
        operands=[dx2b, wg_out, br, proj, proj, wg_gb],
        in_specs=[pl.BlockSpec((tm_b, tk_d), lambda j, i, k: (i, k)),
                  pl.BlockSpec((tn_d, tk_d), lambda j, i, k: (j, k)),
                  pl.BlockSpec((2, tm_b, tn_d), lambda j, i, k: (0, i, j)),
                  *gate_specs(lambda j, i, k: (i, j), tm_b)],
        out_shape=[jax.ShapeDtypeStruct((2, s, d), BF16), jax.ShapeDtypeStruct((2, s, d), BF16),
                   jax.ShapeDtypeStruct((2, d), F32)],
        out_specs=[pl.BlockSpec((2, tm_b, tn_d), lambda j, i, k: (0, i, j)),
                   pl.BlockSpec((2, tm_b, tn_d), lambda j, i, k: (0, i, j)),
                   pl.BlockSpec((2, tn_d), lambda j, i, k: (0, j))],
        pairs=[(0, 1, 0, None)], dims=NT, acc_shapes=[(tm_b, tn_d)],
        epilogue=gate_bwd_epilogue, semantics=("parallel", "arbitrary", "arbitrary"),
        temp_bytes=8 * tm_b * tn_d * 4,
    )

    hh_out = d // 8
    tm_o = _pick(hh_out, 512, LANES)
    nb_o = hh_out // tm_o
    tk_s = _pick(s, 4096, LANES)
    g_out = _mm(
        "w_out_grad", grid=(d // tm_o, d // tn_f, s // tk_s), operands=[merged, dx2b],
        in_specs=[pl.BlockSpec((tk_s, tm_o), lambda i, j, k: (k, i)),
                  pl.BlockSpec((tk_s, tn_f), lambda i, j, k: (k, j))],
        out_shape=[jax.ShapeDtypeStruct((2, N_CHIPS, hh_out, d), BF16)],
        out_specs=[pl.BlockSpec((None, None, tm_o, tn_f),
                                lambda i, j, k: ((i // nb_o) % 2, i // (2 * nb_o), i % nb_o, j))],
        pairs=[(0, 1, 0, None)], dims=TN, acc_shapes=[(tm_o, tn_f)],
        epilogue=lambda accs, ins, outs, ids: outs[0].__setitem__(Ellipsis, accs[0].astype(BF16)),
    )[0][0]

    tn_w = _pick(w, 2048, LANES)
    tk_q = _pick(dq, 1024, LANES)
    qk = dq // tk_q
    (dys,), from_sibling_out = _mm(
        "branch_bwd", grid=(2, s // tm, w // tn_w, d // tk_q), operands=[d_br, wg_br],
        comm=_to_sibling([g_out], other_half=True),
        in_specs=[pl.BlockSpec((None, tm, tk_q), lambda n, i, j, k: (n, i, k)),
                  pl.BlockSpec((None, None, tn_w, tk_q), lambda n, i, j, k: (k // qk, n, j, k % qk))],
        out_shape=[jax.ShapeDtypeStruct((2, s, w), F32)],
        out_specs=[pl.BlockSpec((None, tm, tn_w), lambda n, i, j, k: (n, i, j))],
        pairs=[(0, 1, 0, None)], dims=NT, acc_shapes=[(tm, tn_w)],
        epilogue=lambda accs, ins, outs, ids: outs[0].__setitem__(Ellipsis, accs[0]),
    )

    tm_w = _pick(w, 1024, LANES)
    g_br = _mm(
        "w_branch_grad", grid=(2, w // tm_w, d // tn_d, s // tk_s), operands=[ys, d_br],
        in_specs=[pl.BlockSpec((None, tk_s, tm_w), lambda n, i, j, k: (n, k, i)),
                  pl.BlockSpec((None, tk_s, tn_d), lambda n, i, j, k: (n, k, j))],
        out_shape=[jax.ShapeDtypeStruct((2, N_CHIPS, w, dq), BF16)],
        out_specs=[pl.BlockSpec((None, None, tm_w, tn_d), lambda n, i, j, k: (n, j // qd, i, j % qd))],
        pairs=[(0, 1, 0, None)], dims=TN, acc_shapes=[(tm_w, tn_d)],
        epilogue=lambda accs, ins, outs, ids: outs[0].__setitem__(Ellipsis, accs[0].astype(BF16)),
    )[0][0]

    dpa, g_pool, g_ps, g_cw, g_cb = _mix_bwd(proj, dys, wg_pool, pool_scale, wg_cw, conv_b, w)
    g_pool = g_pool.astype(BF16).reshape(2, N_CHIPS, 2 * (cg // N_CHIPS), cg)

    early = [g_out, g_br, g_pool]
    from_sibling_early = list(from_sibling_out) + _run_exchange(
        "pair_exchange", _to_sibling(early[1:], other_half=True))
    parts_early = []
    for nm, g, r1 in zip(names[1:], early, from_sibling_early):
        _, nc, hh, wd = g.shape
        pt = _pair_add("pair_add_" + nm, g.reshape(2, nc * hh, wd), r1.reshape(nc * hh, wd), c_idx)
        parts_early.append(pt.reshape(nc, hh, wd))

    na = (6 * w) // tn_p
    qg = d // tn_p
    hh_in = d // 2
    assert s == tk_s
    tm_i = _pick(hh_in, 1024, LANES)
    tn_i = _pick(tn_p, 512, LANES)
    from_sibling, recv_early = _w_in_grad_sibling(
        h, dpa, dgl, c_idx, _scatter_partials(parts_early), tm=tm_i, tn=tn_i, rows=hh_in, pq=pq)

    tm_o2 = _pick(hh_in, 512, LANES)
    nb_o2 = hh_in // tm_o2

    def own_rows(name, first, count, comm):
        def add_sibling(accs, ins, outs, ids):
            outs[0][...] = (accs[0] + ins[3][...].astype(F32)).astype(BF16)

        return _mm(
            name, grid=(p // tn_p, count, 1), operands=[h, dpa, dgl, from_sibling], comm=comm, prefetch=c_idx,
            in_specs=[pl.BlockSpec((tk_s, tm_o2), lambda j, i, k, cr: (0, cr[0] * nb_o2 + first + i)),
                      pl.BlockSpec((tk_s, tn_p), lambda j, i, k, cr: (0, jnp.minimum(j, na - 1))),
                      pl.BlockSpec((None, tk_s, tn_p),
                                   lambda j, i, k, cr: (jnp.maximum(j - na, 0) // qg, 0, jnp.maximum(j - na, 0) % qg)),
                      pl.BlockSpec((None, tm_o2, tn_p), lambda j, i, k, cr: (j // qp, first + i, j % qp))],
            out_shape=[jax.ShapeDtypeStruct((N_CHIPS, count * tm_o2, pq), BF16)],
            out_specs=[pl.BlockSpec((None, tm_o2, tn_p), lambda j, i, k, cr: (j // qp, i, j % qp))],
            pairs=[(0, 1, 0, lambda ids: ids[0] < na), (0, 2, 0, lambda ids: ids[0] >= na)],
            dims=TN, acc_shapes=[(tm_o2, tn_p)], epilogue=add_sibling)

    halves_early = [_sum_chips("sum_chips_" + nm, pt, r2, chip_idx)
                    for nm, pt, r2 in zip(names[1:], parts_early, recv_early)]
    share_early = _to_sibling(halves_early, other_half=False)

    n_first = 1 if nb_o2 > 1 else 0
    if n_first:
        (part_a,), others_early = own_rows("w_in_grad_own_a", 0, n_first, share_early)
        scatter_a = _scatter_partials([part_a], rows=hh_in)
        (part_b,), land_a = own_rows("w_in_grad_own_b", n_first, nb_o2 - n_first, scatter_a)
        scatter_b = _scatter_partials([part_b], rows=hh_in, row0=n_first * tm_o2, prev=land_a)
    else:
        (part_b,), others_early = own_rows("w_in_grad_own", 0, nb_o2, share_early)
        scatter_b = _scatter_partials([part_b])

    tn_h = _pick(d, 2048, LANES)
    (dh,), recv_in = _mm(
        "proj_bwd", grid=(s // tm, d // tn_h, p // tn_p), operands=[dpa, dgl, wg_in],
        comm=scatter_b,
        in_specs=[pl.BlockSpec((tm, tn_p), lambda i, j, k: (i, jnp.minimum(k, na - 1))),
                  pl.BlockSpec((None, tm, tn_p),
                               lambda i, j, k: (jnp.maximum(k - na, 0) // qg, i, jnp.maximum(k - na, 0) % qg)),
                  pl.BlockSpec((None, tn_h, tn_p), lambda i, j, k: (k // qp, j, k % qp))],
        out_shape=[jax.ShapeDtypeStruct((s, d), F32)],
        out_specs=[pl.BlockSpec((tm, tn_h), lambda i, j, k: (i, j))],
        pairs=[(0, 2, 0, lambda ids: ids[2] < na), (1, 2, 0, lambda ids: ids[2] >= na)],
        dims=NT, acc_shapes=[(tm, tn_h)],
        epilogue=lambda accs, ins, outs, ids: outs[0].__setitem__(Ellipsis, accs[0]),
    )

    grad_x, g_nw = _rms_bwd(x2d, dh, dx2, norm_w)

    if n_first:
        half_in = _sum_chips("sum_chips_in_a", part_a, recv_in[0], chip_idx)
        half_in = _sum_chips("sum_chips_in_b", part_b, recv_in[0], chip_idx, row0=n_first * tm_o2, prev=half_in)
    else:
        half_in = _sum_chips("sum_chips_in", part_b, recv_in[0], chip_idx)
    halves = [half_in] + halves_early
    others = _run_exchange("pair_share", _to_sibling([half_in], other_half=False)) + list(others_early)

    g_cw_full = jnp.transpose(g_cw, (1, 0, 2)).reshape(CONV_K, w)
    small_shapes = [(LANES,), (1, d), (1, d), (1, w), (1, w), (CONV_K, w), (2, d)]
    pack = _flat_pack([loss_part[0], g_nw, g_fnw, g_ps, g_cb, g_cw_full, g_gb])
    total = _sum_devices(_gather_packs(pack))
    t_loss, t_nw, t_fnw, t_ps, t_cb, t_cw, t_gb = _unpack(total, small_shapes)
    loss = t_loss[0]
    t_cw = lax.dynamic_slice_in_dim(t_cw, chip * cg, cg, axis=1)
    t_gb = lax.dynamic_slice_in_dim(t_gb, chip * dq, dq, axis=1)

    out = {}
    big_names = ["w_in", "w_out", "w_branch", "pool_w"]
    big_m = [m_w_in, m_w_out, m_w_branch, m_pool_w]
    big_v = [v_w_in, v_w_out, v_w_branch, v_pool_w]
    big_orig = [w_in, w_out, w_branch, pool_w]
    for nm, g_own, g_other, w2, mm_, vv_, orig in zip(big_names, halves, others, big_w, big_m, big_v, big_orig):
        sh = (2,) + g_own.shape
        res = _adamw_halves("adamw_" + nm, g_own, g_other, c_idx,
                            w2.reshape(sh), mm_.reshape(sh), vv_.reshape(sh))
        out[nm] = [r.reshape(orig.shape) for r in res]

    sm_names = ["norm_w", "final_norm_w", "pool_scale", "conv_b", "conv_w", "gate_b"]
    sm_g = [t_nw, t_fnw, t_ps, t_cb, t_cw, t_gb]
    sm_w = [norm_w, final_norm_w, pool_scale, conv_b, conv_w, gate_b]
    sm_m = [m_norm_w, m_final_norm_w, m_pool_scale, m_conv_b, m_conv_w, m_gate_b]
    sm_v = [v_norm_w, v_final_norm_w, v_pool_scale, v_conv_b, v_conv_w, v_gate_b]
    sm_shapes = [a.shape for a in sm_w]
    res = _adamw("adamw_small", _flat_pack(sm_g), _flat_pack(sm_w), _flat_pack(sm_m), _flat_pack(sm_v))
    res = [_unpack(r, sm_shapes) for r in res]
    for idx, nm in enumerate(sm_names):
        out[nm] = [r[idx] for r in res]

    order = ["norm_w", "w_in", "pool_w", "pool_scale", "conv_w", "conv_b", "gate_b", "w_branch", "w_out",
             "final_norm_w"]
    outs = [loss, grad_x.reshape(x.shape)]
    for kind in range(4):
        outs += [out[nm][kind] for nm in order]
    return tuple(outs)
```

```python
import math

import jax
import jax.numpy as jnp
from jax import lax
from jax.experimental import pallas as pl
from jax.experimental.pallas import tpu as pltpu

F32 = jnp.float32
BF16 = jnp.bfloat16

NORM_EPS = 1e-6
POOL_WINDOWS = (2, 4, 8, 16)
N_POOL_GROUPS = len(POOL_WINDOWS)
CONV_K = 3
ADAM_LR = 0.001
ADAM_B1 = 0.9
ADAM_B2 = 0.999
ADAM_EPS = 1e-08
ADAM_WD = 0.01
ADAM_STEP = 10

N_CHIPS = 4
N_DEV = 8
HALO = 16
LANES = 128
V7X_VMEM_BYTES = 64 * 1024 * 1024
VMEM_CAP = V7X_VMEM_BYTES - 8 * 1024 * 1024

MESH = pl.DeviceIdType.MESH
ANY = pl.BlockSpec(memory_space=pl.ANY)

NN = (((1,), (0,)), ((), ()))
NT = (((1,), (1,)), ((), ()))
TN = (((0,), (0,)), ((), ()))


def _pick(dim, pref, align):
    if dim <= pref:
        return dim
    t = (pref // align) * align
    while t >= align:
        if dim % t == 0:
            return t
        t -= align
    raise ValueError(f"no tile for {dim} (pref {pref}, align {align})")


def _nbytes(shape, dtype):
    n = 1
    for s in shape:
        if s is not None:
            n *= s
    return n * jnp.dtype(dtype).itemsize


def _params(semantics, block_bytes, extra_bytes=0):
    need = 2 * block_bytes + extra_bytes + (2 << 20)
    return pltpu.CompilerParams(dimension_semantics=semantics,
                                vmem_limit_bytes=int(min(max(need, 16 << 20), VMEM_CAP)))


def _sigmoid(z):
    return jax.nn.sigmoid(z)


def _mm(name, *, grid, operands, in_specs, out_shape, out_specs, pairs, dims, acc_shapes, epilogue,
        semantics=None, temp_bytes=0, comm=None, prefetch=None, aliases=None):
    aliases = dict(aliases or {})
    n_in, n_out = len(operands), len(out_shape)
    kax = len(grid) - 1
    nk = grid[kax]
    single = nk == 1
    conditional = any(p[3] is not None for p in pairs)
    if single and conditional:
        assert len(acc_shapes) == 1 and all(p[3] is not None for p in pairs)
    n_acc = 0 if single else len(acc_shapes)
    n_pf = 0 if prefetch is None else 1
    c_ops = list(comm.operands) if comm else []
    c_out = list(comm.out_shape) if comm else []
    c_sems = list(comm.scratch) if comm else []
    c_alias = dict(comm.aliases) if comm else {}
    n_ci, n_co = len(c_ops), len(c_out)

    def product(ins, ai, bi):
        return lax.dot_general(ins[ai][...], ins[bi][...], dims, preferred_element_type=F32)

    def body(*refs):
        refs = refs[n_pf:]
        ins = refs[:n_in]
        c_in_refs = refs[n_in:n_in + n_ci]
        outs = refs[n_in + n_ci:n_in + n_ci + n_out]
        c_out_refs = refs[n_in + n_ci + n_out:n_in + n_ci + n_out + n_co]
        scratch = refs[n_in + n_ci + n_out + n_co:]
        accs = scratch[:n_acc]
        sems = scratch[n_acc:]
        ids = [pl.program_id(a) for a in range(len(grid))]
        k = ids[kax]

        if comm:
            is_first = ids[0] == 0
            is_last = ids[0] == grid[0] - 1
            for a in range(1, len(grid)):
                is_first = jnp.logical_and(is_first, ids[a] == 0)
                is_last = jnp.logical_and(is_last, ids[a] == grid[a] - 1)

            @pl.when(is_first)
            def _():
                comm.start(c_in_refs, c_out_refs, sems)

        if single and conditional:
            for (ai, bi, ci, cond) in pairs:
                def only(ai=ai, bi=bi):
                    epilogue([product(ins, ai, bi)], ins, outs, ids)
                pl.when(cond(ids))(only)
        elif single:
            vals = [None] * len(acc_shapes)
            for (ai, bi, ci, cond) in pairs:
                r = product(ins, ai, bi)
                vals[ci] = r if vals[ci] is None else vals[ci] + r
            epilogue(vals, ins, outs, ids)
        else:
            @pl.when(k == 0)
            def _():
                for a in accs:
                    a[...] = jnp.zeros(a.shape, a.dtype)

            for (ai, bi, ci, cond) in pairs:
                def step(ai=ai, bi=bi, ci=ci):
                    accs[ci][...] += product(ins, ai, bi)
                if cond is None:
                    step()
                else:
                    pl.when(cond(ids))(step)

            @pl.when(k == nk - 1)
            def _():
                epilogue([a[...] for a in accs], ins, outs, ids)

        if comm:
            @pl.when(is_last)
            def _():
                comm.finish(c_in_refs, c_out_refs, sems)

    if semantics is None:
        semantics = ("parallel",) * kax + ("arbitrary",)
    if comm:
        semantics = ("arbitrary",) * len(grid)
    in_specs = [ANY if idx in aliases else spec for idx, spec in enumerate(in_specs)]
    blk = 0
    for idx, (spec, op) in enumerate(zip(in_specs, operands)):
        if idx not in aliases:
            blk += _nbytes(spec.block_shape, op.dtype)
    for spec, o in zip(out_specs, out_shape):
        blk += _nbytes(spec.block_shape, o.dtype)
    acc_bytes = sum(_nbytes(s, F32) for s in acc_shapes)
    io_alias = {n_pf + i: o for i, o in aliases.items()}
    io_alias.update({n_pf + n_in + i: n_out + o for i, o in c_alias.items()})
    all_in = list(in_specs) + [ANY] * n_ci
    all_out = list(out_specs) + [ANY] * n_co
    scratch_shapes = [pltpu.VMEM(s, F32) for s in acc_shapes[:n_acc]] + c_sems
    params = _params(semantics, blk, 3 * acc_bytes + temp_bytes)
    shapes = list(out_shape) + c_out
    if prefetch is None:
        call = pl.pallas_call(
            body, name=name, grid=grid, in_specs=all_in, out_specs=all_out, out_shape=shapes,
            scratch_shapes=scratch_shapes, input_output_aliases=io_alias, compiler_params=params)
        res = call(*operands, *c_ops)
    else:
        call = pl.pallas_call(
            body, name=name, out_shape=shapes, input_output_aliases=io_alias, compiler_params=params,
            grid_spec=pltpu.PrefetchScalarGridSpec(
                num_scalar_prefetch=1, grid=grid, in_specs=all_in, out_specs=all_out,
                scratch_shapes=scratch_shapes))
        res = call(prefetch, *operands, *c_ops)
    return res[:n_out], res[n_out:]


def _cast_bf16(name, w2d):
    r, c = w2d.shape
    tr = _pick(r, max(16, (4 << 20) // (4 * c)), 16)

    def body(w_ref, o_ref):
        o_ref[...] = w_ref[...].astype(BF16)

    return pl.pallas_call(
        body, name=name, grid=(r // tr,),
        in_specs=[pl.BlockSpec((tr, c), lambda i: (i, 0))],
        out_specs=pl.BlockSpec((tr, c), lambda i: (i, 0)),
        out_shape=jax.ShapeDtypeStruct((r, c), BF16),
        compiler_params=_params(("parallel",), tr * c * 6),
    )(w2d)


def _rms_fwd(x, norm_w):
    s, d = x.shape
    ts = _pick(s, 256, 16)

    def body(x_ref, w_ref, h_ref):
        xv = x_ref[...]
        rstd = lax.rsqrt(jnp.mean(xv * xv, axis=-1, keepdims=True) + NORM_EPS)
        h_ref[...] = (xv * rstd * w_ref[...]).astype(BF16)

    return pl.pallas_call(
        body, name="rms_fwd", grid=(s // ts,),
        in_specs=[pl.BlockSpec((ts, d), lambda i: (i, 0)), pl.BlockSpec((1, d), lambda i: (0, 0))],
        out_specs=pl.BlockSpec((ts, d), lambda i: (i, 0)),
        out_shape=jax.ShapeDtypeStruct((s, d), BF16),
        compiler_params=_params(("parallel",), ts * d * 6, 3 * ts * d * 4),
    )(x, norm_w)


def _head(x, o, target, fnw):
    s, d = x.shape
    ts = _pick(s, 128, 16)

    def body(x_ref, o_ref, t_ref, w_ref, dx_ref, dxb_ref, loss_ref, gw_ref):
        i = pl.program_id(0)

        @pl.when(i == 0)
        def _():
            loss_ref[...] = jnp.zeros(loss_ref.shape, F32)
            gw_ref[...] = jnp.zeros(gw_ref.shape, F32)

        w = w_ref[...]
        x2 = x_ref[...] + o_ref[...]
        rstd = lax.rsqrt(jnp.mean(x2 * x2, axis=-1, keepdims=True) + NORM_EPS)
        n = x2 * rstd
        e = n * w - t_ref[...]
        loss_ref[...] += 0.5 * jnp.sum(e * e) / d
        dy = e / d
        gw_ref[...] += jnp.sum(dy * n, axis=0, keepdims=True)
        gy = dy * w
        dx = rstd * (gy - n * jnp.mean(gy * n, axis=-1, keepdims=True))
        dx_ref[...] = dx
        dxb_ref[...] = dx.astype(BF16)

    row = pl.BlockSpec((ts, d), lambda i: (i, 0))
    return pl.pallas_call(
        body, name="head", grid=(s // ts,),
        in_specs=[row, row, row, pl.BlockSpec((1, d), lambda i: (0, 0))],
        out_specs=[row, row, pl.BlockSpec((8, LANES), lambda i: (0, 0)),
                   pl.BlockSpec((1, d), lambda i: (0, 0))],
        out_shape=[jax.ShapeDtypeStruct((s, d), F32), jax.ShapeDtypeStruct((s, d), BF16),
                   jax.ShapeDtypeStruct((8, LANES), F32), jax.ShapeDtypeStruct((1, d), F32)],
        compiler_params=_params(("arbitrary",), ts * d * 22, 6 * ts * d * 4),
    )(x, o, target, fnw)


def _rms_bwd(x, dh, dx2, norm_w):
    s, d = x.shape
    ts = _pick(s, 128, 16)

    def body(x_ref, dh_ref, dx2_ref, w_ref, gx_ref, gw_ref):
        i = pl.program_id(0)

        @pl.when(i == 0)
        def _():
            gw_ref[...] = jnp.zeros(gw_ref.shape, F32)

        xv = x_ref[...]
        rstd = lax.rsqrt(jnp.mean(xv * xv, axis=-1, keepdims=True) + NORM_EPS)
        n = xv * rstd
        dhv = dh_ref[...]
        gw_ref[...] += jnp.sum(dhv * n, axis=0, keepdims=True)
        gh = dhv * w_ref[...]
        gx_ref[...] = dx2_ref[...] + rstd * (gh - n * jnp.mean(gh * n, axis=-1, keepdims=True))

    row = pl.BlockSpec((ts, d), lambda i: (i, 0))
    vec = pl.BlockSpec((1, d), lambda i: (0, 0))
    return pl.pallas_call(
        body, name="rms_bwd", grid=(s // ts,),
        in_specs=[row, row, row, vec], out_specs=[row, vec],
        out_shape=[jax.ShapeDtypeStruct((s, d), F32), jax.ShapeDtypeStruct((1, d), F32)],
        compiler_params=_params(("arbitrary",), ts * d * 16, 5 * ts * d * 4),
    )(x, dh, dx2, norm_w)


def _silu(z):
    return z * _sigmoid(z)


def _window_sum(ext, window, back):
    n = ext.shape[0]
    acc = ext
    step = 1
    while step < window:
        acc = acc + pltpu.roll(acc, step if back else n - step, 0)
        step *= 2
    return acc


def _shift_rows(ext, k, back):
    n = ext.shape[0]
    return pltpu.roll(ext, k if back else n - k, 0)


def _mix_fwd(proj, pool_wg, pool_scale, conv_wg, conv_b, width, comm):
    s = proj.shape[0]
    w = width
    cg = w // N_POOL_GROUPS
    ts = _pick(s, 128, HALO)
    hb = ts // HALO
    cols = 6 * w

    n_ci, n_co = len(comm.operands), len(comm.out_shape)
    n_steps = s // ts

    def body(p_ref, ph_ref, pw_ref, ps_ref, cw_ref, cb_ref, *rest):
        c_in, ys_ref = rest[:n_ci], rest[n_ci]
        c_out, sems = rest[n_ci + 1:n_ci + 1 + n_co], rest[n_ci + 1 + n_co:]
        i = pl.program_id(0)
        first = i == 0
        t1 = (i * ts + lax.broadcasted_iota(jnp.int32, (ts, 1), 0) + 1).astype(F32)

        @pl.when(first)
        def _():
            comm.start(c_in, c_out, sems)

        def tile(part, g):
            lo = part * w + g * cg
            return p_ref[:, lo:lo + cg].astype(F32)

        def prev(part, g):
            lo = part * w + g * cg
            return jnp.where(first, 0.0, ph_ref[:, lo:lo + cg].astype(F32))

        for g, win in enumerate(POOL_WINDOWS):
            gs = slice(g * cg, (g + 1) * cg)
            u = tile(0, g)
            ext = jnp.concatenate([prev(0, g), u], axis=0)
            wsum = _window_sum(ext, win, True)[HALO:]
            pooled = wsum / jnp.minimum(t1, float(win)) - u
            pw = pw_ref[:, g].reshape(cg, cg)
            mixed = jnp.dot(pooled.astype(BF16), pw, preferred_element_type=F32)
            ys_ref[0, :, gs] = (mixed * ps_ref[:, gs] * _silu(tile(1, g))).astype(BF16)
            v = tile(4, g) * tile(2, g)
            vext = jnp.concatenate([prev(4, g) * prev(2, g), v], axis=0)
            v1 = _shift_rows(vext, 1, True)[HALO:]
            v2 = _shift_rows(vext, 2, True)[HALO:]
            cw = [cw_ref[g, tap:tap + 1, :] for tap in range(CONV_K)]
            y = cb_ref[:, gs] + cw[0] * v2 + cw[1] * v1 + cw[2] * v
            ys_ref[1, :, gs] = (tile(3, g) * y * _silu(tile(5, g))).astype(BF16)

        @pl.when(i == n_steps - 1)
        def _():
            comm.finish(c_in, c_out, sems)

    res = pl.pallas_call(
        body, name="mix_fwd", grid=(n_steps,),
        in_specs=[pl.BlockSpec((ts, cols), lambda i: (i, 0)),
                  pl.BlockSpec((HALO, cols), lambda i: (jnp.maximum(i * hb - 1, 0), 0)),
                  pl.BlockSpec(pool_wg.shape, lambda i: (0, 0, 0, 0)),
                  pl.BlockSpec((1, w), lambda i: (0, 0)),
                  pl.BlockSpec(conv_wg.shape, lambda i: (0, 0, 0)),
                  pl.BlockSpec((1, w), lambda i: (0, 0))] + [ANY] * n_ci,
        out_specs=[pl.BlockSpec((2, ts, w), lambda i: (0, i, 0))] + [ANY] * n_co,
        out_shape=[jax.ShapeDtypeStruct((2, s, w), BF16)] + list(comm.out_shape),
        scratch_shapes=list(comm.scratch),
        compiler_params=_params(("arbitrary",), (ts + HALO) * cols * 2 + 2 * ts * w * 2
                                + _nbytes(pool_wg.shape, BF16), 24 * (ts + HALO) * cg * 4),
    )(proj, proj, pool_wg, pool_scale, conv_wg, conv_b, *comm.operands)
    return res[0], res[1:]


def _mix_bwd(proj, dys, pool_wg, pool_scale, conv_wg, conv_b, width):
    s = proj.shape[0]
    w = width
    cg = w // N_POOL_GROUPS
    ts = _pick(s, 128, HALO)
    hb = ts // HALO
    n_tiles = s // ts
    last_hb = s // HALO - 1
    cols = 6 * w

    def body(p_ref, ph_ref, pn_ref, dy_ref, dyn_ref, pw_ref, ps_ref, cw_ref, cb_ref,
             dp_ref, dpw_ref, dps_ref, dcw_ref, dcb_ref):
        i = pl.program_id(0)
        first = i == 0
        last = i == n_tiles - 1

        @pl.when(first)
        def _():
            dpw_ref[...] = jnp.zeros(dpw_ref.shape, F32)
            dps_ref[...] = jnp.zeros(dps_ref.shape, F32)
            dcw_ref[...] = jnp.zeros(dcw_ref.shape, F32)
            dcb_ref[...] = jnp.zeros(dcb_ref.shape, F32)

        row = i * ts + lax.broadcasted_iota(jnp.int32, (ts + HALO, 1), 0)
        t1_ext = (row + 1).astype(F32)
        t1 = t1_ext[:ts]

        def tile(part, g):
            lo = part * w + g * cg
            return p_ref[:, lo:lo + cg].astype(F32)

        def prev(part, g):
            lo = part * w + g * cg
            return jnp.where(first, 0.0, ph_ref[:, lo:lo + cg].astype(F32))

        def ahead(part, g):
            lo = part * w + g * cg
            return jnp.concatenate([tile(part, g), pn_ref[:, lo:lo + cg].astype(F32)], axis=0)

        def dy_ahead(n, g):
            gs = slice(g * cg, (g + 1) * cg)
            nxt = jnp.where(last, 0.0, dyn_ref[n, :, gs])
            return jnp.concatenate([dy_ref[n, :, gs], nxt], axis=0)

        for g, win in enumerate(POOL_WINDOWS):
            gs = slice(g * cg, (g + 1) * cg)
            u = tile(0, g)
            ext = jnp.concatenate([prev(0, g), u], axis=0)
            pooled = _window_sum(ext, win, True)[HALO:] / jnp.minimum(t1, float(win)) - u
            pooled_b = pooled.astype(BF16)
            pw = pw_ref[:, g].reshape(cg, cg)
            mixed = jnp.dot(pooled_b, pw, preferred_element_type=F32)
            zp_ext = ahead(1, g)
            dy0_ext = dy_ahead(0, g)
            scale = ps_ref[:, gs]
            sp_ext = _sigmoid(zp_ext)
            dms_ext = dy0_ext * (zp_ext * sp_ext)
            dmix_b = (dms_ext * scale).astype(BF16)
            dpooled_ext = lax.dot_general(dmix_b, pw, NT, preferred_element_type=F32)
            dy0 = dy0_ext[:ts]
            zp, sp = zp_ext[:ts], sp_ext[:ts]
            dsilu_p = sp * (1.0 + zp * (1.0 - sp))
            dp_ref[:, w + g * cg:w + (g + 1) * cg] = (dy0 * mixed * scale * dsilu_p).astype(BF16)
            dps_ref[:, gs] += jnp.sum(dms_ext[:ts] * mixed, axis=0, keepdims=True)
            dpw = lax.dot_general(pooled_b, dmix_b[:ts], TN, preferred_element_type=F32)
            dpw_ref[g // 2, :, g % 2] += dpw.reshape(N_CHIPS, cg // N_CHIPS, cg)
            q_ext = dpooled_ext / jnp.minimum(t1_ext, float(win))
            du = _window_sum(q_ext, win, False)[:ts] - dpooled_ext[:ts]
            dp_ref[:, gs] = du.astype(BF16)
            uc = tile(2, g)
            cc = tile(4, g)
            v = cc * uc
            vext = jnp.concatenate([prev(4, g) * prev(2, g), v], axis=0)
            v1 = _shift_rows(vext, 1, True)[HALO:]
            v2 = _shift_rows(vext, 2, True)[HALO:]
            cw = [cw_ref[g, tap:tap + 1, :] for tap in range(CONV_K)]
            y = cb_ref[:, gs] + cw[0] * v2 + cw[1] * v1 + cw[2] * v
            bc_ext = ahead(3, g)
            zc_ext = ahead(5, g)
            dy1_ext = dy_ahead(1, g)
            sc_ext = _sigmoid(zc_ext)
            silu_c_ext = zc_ext * sc_ext
            dyy_ext = dy1_ext * bc_ext * silu_c_ext
            dy1 = dy1_ext[:ts]
            bc = bc_ext[:ts]
            zc, sc = zc_ext[:ts], sc_ext[:ts]
            dsilu_c = sc * (1.0 + zc * (1.0 - sc))
            dp_ref[:, 3 * w + g * cg:3 * w + (g + 1) * cg] = (dy1 * y * silu_c_ext[:ts]).astype(BF16)
            dp_ref[:, 5 * w + g * cg:5 * w + (g + 1) * cg] = (dy1 * bc * y * dsilu_c).astype(BF16)
            dyy = dyy_ext[:ts]
            dcb_ref[:, gs] += jnp.sum(dyy, axis=0, keepdims=True)
            for tap, vt in enumerate((v2, v1, v)):
                dcw_ref[g, tap:tap + 1, :] += jnp.sum(dyy * vt, axis=0, keepdims=True)
            dv = (cw[2] * dyy + cw[1] * _shift_rows(dyy_ext, 1, False)[:ts]
                  + cw[0] * _shift_rows(dyy_ext, 2, False)[:ts])
            dp_ref[:, 4 * w + g * cg:4 * w + (g + 1) * cg] = (dv * uc).astype(BF16)
            dp_ref[:, 2 * w + g * cg:2 * w + (g + 1) * cg] = (dv * cc).astype(BF16)

    dpw_shape = (2, N_CHIPS, 2, cg // N_CHIPS, cg)
    return pl.pallas_call(
        body, name="mix_bwd", grid=(n_tiles,),
        in_specs=[pl.BlockSpec((ts, cols), lambda i: (i, 0)),
                  pl.BlockSpec((HALO, cols), lambda i: (jnp.maximum(i * hb - 1, 0), 0)),
                  pl.BlockSpec((HALO, cols), lambda i: (jnp.minimum((i + 1) * hb, last_hb), 0)),
                  pl.BlockSpec((2, ts, w), lambda i: (0, i, 0)),
                  pl.BlockSpec((2, HALO, w), lambda i: (0, jnp.minimum((i + 1) * hb, last_hb), 0)),
                  pl.BlockSpec(pool_wg.shape, lambda i: (0, 0, 0, 0)),
                  pl.BlockSpec((1, w), lambda i: (0, 0)),
                  pl.BlockSpec(conv_wg.shape, lambda i: (0, 0, 0)),
                  pl.BlockSpec((1, w), lambda i: (0, 0))],
        out_specs=[pl.BlockSpec((ts, cols), lambda i: (i, 0)),
                   pl.BlockSpec(dpw_shape, lambda i: (0, 0, 0, 0, 0)),
                   pl.BlockSpec((1, w), lambda i: (0, 0)),
                   pl.BlockSpec(conv_wg.shape, lambda i: (0, 0, 0)),
                   pl.BlockSpec((1, w), lambda i: (0, 0))],
        out_shape=[jax.ShapeDtypeStruct((s, cols), BF16), jax.ShapeDtypeStruct(dpw_shape, F32),
                   jax.ShapeDtypeStruct((1, w), F32), jax.ShapeDtypeStruct(conv_wg.shape, F32),
                   jax.ShapeDtypeStruct((1, w), F32)],
        compiler_params=_params(("arbitrary",), (2 * ts + 2 * HALO) * cols * 2 + (ts + HALO) * w * 8
                                + _nbytes(pool_wg.shape, BF16) + _nbytes(dpw_shape, F32),
                                40 * (ts + HALO) * cg * 4),
    )(proj, proj, proj, dys, dys, pool_wg, pool_scale, conv_wg, conv_b)


def _adamw(name, g, w, m, v):
    r, c = w.shape
    tr = _pick(r, max(8, (1 << 20) // (4 * c)), 8)

    def body(g_ref, w_ref, m_ref, v_ref, go_ref, d_ref, mo_ref, vo_ref):
        gv = g_ref[...]
        mn = ADAM_B1 * m_ref[...] + (1.0 - ADAM_B1) * gv
        vn = ADAM_B2 * v_ref[...] + (1.0 - ADAM_B2) * (gv * gv)
        m_hat = mn / (1.0 - ADAM_B1 ** ADAM_STEP)
        v_hat = vn / (1.0 - ADAM_B2 ** ADAM_STEP)
        go_ref[...] = gv
        d_ref[...] = -ADAM_LR * (m_hat / (jnp.sqrt(v_hat) + ADAM_EPS) + ADAM_WD * w_ref[...])
        mo_ref[...] = mn
        vo_ref[...] = vn

    blk = pl.BlockSpec((tr, c), lambda i: (i, 0))
    sh = jax.ShapeDtypeStruct((r, c), F32)
    return pl.pallas_call(
        body, name=name, grid=(r // tr,), in_specs=[blk] * 4, out_specs=[blk] * 4,
        out_shape=[sh] * 4, compiler_params=_params(("parallel",), tr * c * 32, 4 * tr * c * 4),
    )(g, w, m, v)


def _pair_add(name, g, r1, c_idx):
    _, r, c = g.shape
    tr = _pick(r, max(16, (2 << 20) // (2 * c)), 16)

    def body(c_ref, g_ref, r_ref, o_ref):
        o_ref[...] = (g_ref[...].astype(F32) + r_ref[...].astype(F32)).astype(BF16)

    return pl.pallas_call(
        body, name=name,
        grid_spec=pltpu.PrefetchScalarGridSpec(
            num_scalar_prefetch=1, grid=(r // tr,),
            in_specs=[pl.BlockSpec((None, tr, c), lambda i, cr: (cr[0], i, 0)),
                      pl.BlockSpec((tr, c), lambda i, cr: (i, 0))],
            out_specs=pl.BlockSpec((tr, c), lambda i, cr: (i, 0))),
        out_shape=jax.ShapeDtypeStruct((r, c), BF16),
        compiler_params=_params(("parallel",), tr * c * 6, 3 * tr * c * 4),
    )(c_idx, g, r1)


def _adamw_halves(name, g_own, g_other, c_idx, w, m, v):
    _, r, c = w.shape
    tr = _pick(r, max(8, (5 << 18) // (4 * c)), 8)

    def body(c_ref, go_ref, gt_ref, w_ref, m_ref, v_ref, g_out, d_ref, mo_ref, vo_ref):
        gv = jnp.where(pl.program_id(0) == c_ref[0], go_ref[...], gt_ref[...])
        mn = ADAM_B1 * m_ref[...] + (1.0 - ADAM_B1) * gv
        vn = ADAM_B2 * v_ref[...] + (1.0 - ADAM_B2) * (gv * gv)
        m_hat = mn / (1.0 - ADAM_B1 ** ADAM_STEP)
        v_hat = vn / (1.0 - ADAM_B2 ** ADAM_STEP)
        g_out[...] = gv
        d_ref[...] = -ADAM_LR * (m_hat / (jnp.sqrt(v_hat) + ADAM_EPS) + ADAM_WD * w_ref[...])
        mo_ref[...] = mn
        vo_ref[...] = vn

    blk = pl.BlockSpec((None, tr, c), lambda h, i, cr: (h, i, 0))
    sh = jax.ShapeDtypeStruct(w.shape, F32)
    return pl.pallas_call(
        body, name=name,
        grid_spec=pltpu.PrefetchScalarGridSpec(
            num_scalar_prefetch=1, grid=(2, r // tr),
            in_specs=[pl.BlockSpec((tr, c), lambda h, i, cr: (jnp.where(h == cr[0], i, 0), 0)),
                      pl.BlockSpec((tr, c), lambda h, i, cr: (jnp.where(h == cr[0], 0, i), 0)),
                      blk, blk, blk],
            out_specs=[blk] * 4),
        out_shape=[sh] * 4,
        compiler_params=_params(("arbitrary", "arbitrary"), tr * c * 36, 4 * tr * c * 4),
    )(c_idx, g_own, g_other, w, m, v)


def _sum_chips(name, part, r2, chip_idx, row0=0, prev=None):
    n, r, c = r2.shape
    rc = part.shape[1]
    tr = _pick(math.gcd(rc, row0) if row0 else rc, max(16, (1 << 20) // (2 * c)), 16)
    b0 = row0 // tr

    def body(ch_ref, own_ref, *rest):
        slots, o_ref = rest[:n], rest[-1]
        acc = None
        for s in range(n):
            term = jnp.where(ch_ref[0] == s, own_ref[...], slots[s][...]).astype(F32)
            acc = term if acc is None else acc + term
        o_ref[...] = acc

    def slot_spec(s):
        return pl.BlockSpec((None, tr, c), lambda i, ch: (jnp.where(ch[0] == s, (s + 1) % n, s), b0 + i, 0))

    extra = [] if prev is None else [prev]
    return pl.pallas_call(
        body, name=name,
        grid_spec=pltpu.PrefetchScalarGridSpec(
            num_scalar_prefetch=1, grid=(rc // tr,),
            in_specs=[pl.BlockSpec((None, tr, c), lambda i, ch: (ch[0], i, 0))]
            + [slot_spec(s) for s in range(n)] + [ANY] * len(extra),
            out_specs=pl.BlockSpec((tr, c), lambda i, ch: (b0 + i, 0))),
        out_shape=jax.ShapeDtypeStruct((r, c), F32),
        input_output_aliases={2 + n: 0} if extra else {},
        compiler_params=_params(("parallel",), tr * c * (2 * n + 6), 3 * tr * c * 4),
    )(chip_idx, part, *([r2] * n), *extra)


def _sum_devices(packs):
    n, r, c = packs.shape

    def body(p_ref, o_ref):
        acc = p_ref[0]
        for k in range(1, n):
            acc = acc + p_ref[k]
        o_ref[...] = acc

    return pl.pallas_call(
        body, name="sum_devices", out_shape=jax.ShapeDtypeStruct((r, c), F32),
        in_specs=[pl.BlockSpec(memory_space=pltpu.VMEM)],
        out_specs=pl.BlockSpec(memory_space=pltpu.VMEM),
    )(packs)


def _place():
    x, y, c = lax.axis_index("x"), lax.axis_index("y"), lax.axis_index("c")
    return x, y, c


def _chip_peers(x, y):
    out = []
    for k, (fx, fy) in enumerate(((0, 1), (1, 0), (1, 1))):
        px = 1 - x if fx else x
        py = 1 - y if fy else y
        out.append((k, px, py, 2 * px + py))
    return out


def _gather_weights(big, small, relay=False):
    nb, ns = len(big), len(small)

    class Copies:
        def __init__(self, c_in, c_out, sems):
            b_in, s_in = c_in[:nb], c_in[nb:]
            b_out, s_out = c_out[:nb], c_out[nb:]
            ici_s, ici_r, d2d_s, d2d_r, own_s, own_r = sems[:6]
            x, y, c = _place()
            chip = 2 * x + y
            sibling = (x, y, 1 - c)
            peers = _chip_peers(x, y)
            self.own = [pltpu.make_async_remote_copy(
                src_ref=b_in[t], dst_ref=b_out[t].at[chip], send_sem=own_s.at[t], recv_sem=own_r.at[t],
                device_id=sibling, device_id_type=MESH) for t in range(nb)]
            self.ici = [pltpu.make_async_remote_copy(
                src_ref=b_in[t].at[c], dst_ref=b_out[t].at[chip, c],
                send_sem=ici_s.at[t, k], recv_sem=ici_r.at[t, k],
                device_id=(px, py, c), device_id_type=MESH)
                for t in range(nb) for (k, px, py, pchip) in peers if not (relay and k == 2)]
            self.relay = []
            if relay:
                south = c == 0
                from_chip = jnp.where(south, 2 * x + (1 - y), 2 * (1 - x) + y)
                to = (jnp.where(south, 1 - x, x), jnp.where(south, y, 1 - y), c)
                self.relay = [pltpu.make_async_remote_copy(
                    src_ref=b_out[t].at[from_chip, c], dst_ref=b_out[t].at[from_chip, c],
                    send_sem=ici_s.at[t, 2], recv_sem=ici_r.at[t, 2],
                    device_id=to, device_id_type=MESH) for t in range(nb)]
            self.landed = [pltpu.make_async_remote_copy(
                src_ref=b_out[t].at[pchip, c], dst_ref=b_out[t].at[pchip, c],
                send_sem=ici_s.at[t, k], recv_sem=ici_r.at[t, k],
                device_id=sibling, device_id_type=MESH)
                for t in range(nb) for (k, px, py, pchip) in peers]
            self.passed = [pltpu.make_async_remote_copy(
                src_ref=b_out[t].at[pchip, c], dst_ref=b_out[t].at[pchip, c],
                send_sem=d2d_s.at[t, k], recv_sem=d2d_r.at[t, k],
                device_id=sibling, device_id_type=MESH)
                for t in range(nb) for (k, px, py, pchip) in peers]
            self.from_sibling = [pltpu.make_async_remote_copy(
                src_ref=b_out[t].at[pchip, 1 - c], dst_ref=b_out[t].at[pchip, 1 - c],
                send_sem=d2d_s.at[t, k], recv_sem=d2d_r.at[t, k],
                device_id=sibling, device_id_type=MESH)
                for t in range(nb) for (k, px, py, pchip) in peers]
            self.small, self.small_landed, self.local = [], [], []
            if ns:
                sm_s, sm_r, loc = sems[6:]
                self.local = [pltpu.make_async_copy(s_in[t], s_out[t].at[chip], loc.at[t]) for t in range(ns)]
                self.small = [pltpu.make_async_remote_copy(
                    src_ref=s_in[t], dst_ref=s_out[t].at[chip],
                    send_sem=sm_s.at[t, k], recv_sem=sm_r.at[t, k],
                    device_id=(px, py, c), device_id_type=MESH)
                    for t in range(ns) for (k, px, py, pchip) in peers]
                self.small_landed = [pltpu.make_async_remote_copy(
                    src_ref=s_in[t], dst_ref=s_out[t].at[pchip],
                    send_sem=sm_s.at[t, k], recv_sem=sm_r.at[t, k],
                    device_id=sibling, device_id_type=MESH)
                    for t in range(ns) for (k, px, py, pchip) in peers]

    def start(c_in, c_out, sems):
        cps = Copies(c_in, c_out, sems)
        for cp in cps.local + cps.own + cps.ici + cps.small:
            cp.start()

    def finish(c_in, c_out, sems):
        cps = Copies(c_in, c_out, sems)
        for t in range(nb):
            for k in range(3):
                if relay and k == 2:
                    cps.relay[t].start()
                cps.landed[3 * t + k].wait_recv()
                cps.passed[3 * t + k].start()
        for cp in cps.small_landed + cps.from_sibling:
            cp.wait_recv()
        for cp in cps.ici + cps.relay + cps.small + cps.passed:
            cp.wait_send()
        for cp in cps.own + cps.local:
            cp.wait()

    out_shape = [jax.ShapeDtypeStruct((N_CHIPS,) + b.shape, b.dtype) for b in big]
    out_shape += [jax.ShapeDtypeStruct((N_CHIPS,) + s.shape, s.dtype) for s in small]
    dma = pltpu.SemaphoreType.DMA
    scratch = [dma((nb, 3)), dma((nb, 3)), dma((nb, 3)), dma((nb, 3)), dma((nb,)), dma((nb,))]
    if ns:
        scratch += [dma((ns, 3)), dma((ns, 3)), dma((ns,))]
    ex = _Exchange(list(big) + list(small), out_shape, scratch, start, finish)
    ex.copies = Copies
    return ex


class _Exchange:
    def __init__(self, operands, out_shape, scratch, start, finish, aliases=None):
        self.operands, self.out_shape, self.scratch = operands, out_shape, scratch
        self.start, self.finish, self.aliases = start, finish, dict(aliases or {})


def _run_exchange(name, ex):
    n_i, n_o = len(ex.operands), len(ex.out_shape)

    def body(*refs):
        ins, outs, sems = refs[:n_i], refs[n_i:n_i + n_o], refs[n_i + n_o:]
        ex.start(ins, outs, sems)
        ex.finish(ins, outs, sems)

    return list(pl.pallas_call(
        body, name=name, out_shape=list(ex.out_shape), in_specs=[ANY] * n_i, out_specs=[ANY] * n_o,
        scratch_shapes=list(ex.scratch), input_output_aliases=ex.aliases,
    )(*ex.operands))


def _to_sibling(arrays, other_half):
    n = len(arrays)

    def copies(a_in, a_out, sems):
        send, recv = sems
        x, y, c = _place()
        return [pltpu.make_async_remote_copy(
            src_ref=a_in[t].at[1 - c] if other_half else a_in[t], dst_ref=a_out[t],
            send_sem=send.at[t], recv_sem=recv.at[t],
            device_id=(x, y, 1 - c), device_id_type=MESH) for t in range(n)]

    def start(a_in, a_out, sems):
        for cp in copies(a_in, a_out, sems):
            cp.start()

    def finish(a_in, a_out, sems):
        for cp in copies(a_in, a_out, sems):
            cp.wait()

    dma = pltpu.SemaphoreType.DMA
    shapes = [jax.ShapeDtypeStruct(a.shape[1:] if other_half else a.shape, a.dtype) for a in arrays]
    return _Exchange(list(arrays), shapes, [dma((n,)), dma((n,))], start, finish)


def _scatter_partials(parts, rows=None, row0=0, prev=None):
    n = len(parts)
    land = [(p.shape[0], p.shape[1] if rows is None else rows, p.shape[2]) for p in parts]

    def window(ref, slot, t):
        return ref.at[slot, pl.ds(row0, parts[t].shape[1])]

    def sends(p_in, r_out, sems):
        send, recv = sems
        x, y, c = _place()
        chip = 2 * x + y
        return [pltpu.make_async_remote_copy(
            src_ref=p_in[t].at[pchip], dst_ref=window(r_out[t], chip, t),
            send_sem=send.at[t, k], recv_sem=recv.at[t, k],
            device_id=(px, py, c), device_id_type=MESH)
            for t in range(n) for (k, px, py, pchip) in _chip_peers(x, y)]

    def start(p_in, r_out, sems):
        for cp in sends(p_in, r_out, sems):
            cp.start()

    def finish(p_in, r_out, sems):
        send, recv = sems
        x, y, c = _place()
        for t in range(n):
            for (k, px, py, pchip) in _chip_peers(x, y):
                pltpu.make_async_remote_copy(
                    src_ref=p_in[t].at[pchip], dst_ref=window(r_out[t], pchip, t),
                    send_sem=send.at[t, k], recv_sem=recv.at[t, k],
                    device_id=(px, py, c), device_id_type=MESH).wait_recv()
        for cp in sends(p_in, r_out, sems):
            cp.wait_send()

    dma = pltpu.SemaphoreType.DMA
    operands = list(parts) + (list(prev) if prev else [])
    return _Exchange(operands, [jax.ShapeDtypeStruct(sh, p.dtype) for sh, p in zip(land, parts)],
                     [dma((n, 3)), dma((n, 3))], start, finish,
                     aliases={n + t: t for t in range(n)} if prev else None)


def _w_in_grad_sibling(h, dpa, dgl, c_idx, carried, *, tm, tn, rows, pq):
    s, d = h.shape
    wa, wg = dpa.shape[1], dgl.shape[2]
    p = wa + 2 * wg
    ni, nj = rows // tm, p // tn
    na, qg, qp = wa // tn, wg // tn, pq // tn
    n_steps = ni * nj
    n_ci, n_co = len(carried.operands), len(carried.out_shape)

    def body(c_ref, h_ref, a_ref, g_ref, *rest):
        c_in = rest[:n_ci]
        r1_ref = rest[n_ci]
        c_out = rest[n_ci + 1:n_ci + 1 + n_co]
        slots, send, recv = rest[n_ci + 1 + n_co:n_ci + 4 + n_co]
        sems = rest[n_ci + 4 + n_co:]
        i, j = pl.program_id(0), pl.program_id(1)
        step = i * nj + j
        slot = lax.rem(step, 2)
        x, y, c = _place()
        sibling = (x, y, 1 - c)

        @pl.when(step == 0)
        def _():
            carried.start(c_in, c_out, sems)

        def tile_copy(sl):
            return pltpu.make_async_remote_copy(
                src_ref=slots.at[sl],
                dst_ref=r1_ref.at[j // qp, pl.ds(i * tm, tm), pl.ds((j % qp) * tn, tn)],
                send_sem=send.at[sl], recv_sem=recv, device_id=sibling, device_id_type=MESH)

        @pl.when(step >= 2)
        def _():
            tile_copy(slot).wait_send()

        def emit(b_ref):
            acc = lax.dot_general(h_ref[...], b_ref[...], TN, preferred_element_type=F32)
            slots[slot] = acc.astype(BF16)

        pl.when(j < na)(lambda: emit(a_ref))
        pl.when(j >= na)(lambda: emit(g_ref))
        tile_copy(slot).start()

        @pl.when(step == n_steps - 1)
        def _():
            tile_copy(0).wait_send()
            tile_copy(1).wait_send()
            pltpu.make_async_remote_copy(
                src_ref=r1_ref, dst_ref=r1_ref, send_sem=send.at[0], recv_sem=recv,
                device_id=sibling, device_id_type=MESH).wait_recv()
            carried.finish(c_in, c_out, sems)

    nbh = rows // tm
    dma = pltpu.SemaphoreType.DMA
    blk = s * tm * 2 + 2 * s * tn * 2
    res = pl.pallas_call(
        body, name="w_in_grad_sibling",
        grid_spec=pltpu.PrefetchScalarGridSpec(
            num_scalar_prefetch=1, grid=(ni, nj),
            in_specs=[pl.BlockSpec((s, tm), lambda i, j, cr: (0, (1 - cr[0]) * nbh + i)),
                      pl.BlockSpec((s, tn), lambda i, j, cr: (0, jnp.minimum(j, na - 1))),
                      pl.BlockSpec((None, s, tn),
                                   lambda i, j, cr: (jnp.maximum(j - na, 0) // qg, 0, jnp.maximum(j - na, 0) % qg))]
            + [ANY] * n_ci,
            out_specs=[ANY] * (1 + n_co),
            scratch_shapes=[pltpu.VMEM((2, tm, tn), BF16), dma((2,)), dma] + list(carried.scratch)),
        out_shape=[jax.ShapeDtypeStruct((N_CHIPS, rows, pq), BF16)] + list(carried.out_shape),
        compiler_params=_params(("arbitrary", "arbitrary"), blk, 2 * tm * tn * 2 + 2 * tm * tn * 4),
    )(c_idx, h, dpa, dgl, *carried.operands)
    return res[0], res[1:]


def _proj_gathering(h, wb, chip_idx, gather_rest, *, tm, tn):
    s, d = h.shape
    _, hh, pq = wb.shape
    qp, ni = pq // tn, s // tm
    n_tiles = N_CHIPS * qp
    n_ci, n_co, n_cs = len(gather_rest.operands), len(gather_rest.out_shape), len(gather_rest.scratch)

    def where_of(t):
        near = t - qp
        g = jnp.where(t < qp, 0, jnp.where(t < 3 * qp, 1 + lax.rem(near, 2), 3))
        jj = jnp.where(t < qp, t, jnp.where(t < 3 * qp, near // 2, t - 3 * qp))
        return g, jj

    def body(ch_ref, h_ref, own_ref, *rest):
        c_in = rest[:n_ci]
        o_ref, gathered = rest[n_ci], rest[n_ci + 1]
        c_out = rest[n_ci + 2:n_ci + 2 + n_co]
        slots, fetch_sem, ici_s, ici_r, d2d_s, d2d_r, own_s, own_r = rest[n_ci + 2 + n_co:n_ci + 10 + n_co]
        c_sem = rest[n_ci + 10 + n_co:]
        tile, i = pl.program_id(0), pl.program_id(1)
        x, y, c = _place()
        chip = 2 * x + y
        sibling = (x, y, 1 - c)

        def cols(jj):
            return pl.ds(jj * tn, tn)

        def remote(src, dst, send, recv, to):
            return pltpu.make_async_remote_copy(src_ref=src, dst_ref=dst, send_sem=send, recv_sem=recv,
                                                device_id=to, device_id_type=MESH)

        def own_shard():
            return remote(own_ref, gathered.at[chip], own_s, own_r, sibling)

        def to_neighbour(k, jj):
            to = (x, 1 - y, c) if k == 0 else (1 - x, y, c)
            return remote(own_ref.at[c, :, cols(jj)], gathered.at[chip, c, :, cols(jj)],
                          ici_s.at[k, jj], ici_r.at[k, jj], to)

        def relay(jj):
            south = c == 0
            from_chip = jnp.where(south, 2 * x + (1 - y), 2 * (1 - x) + y)
            to = (jnp.where(south, 1 - x, x), jnp.where(south, y, 1 - y), c)
            chunk = gathered.at[from_chip, c, :, cols(jj)]
            return remote(chunk, chunk, ici_s.at[2, jj], ici_r.at[2, jj], to)

        def chunk_of(k, jj, half):
            return gathered.at[jnp.bitwise_xor(chip, k + 1), half, :, cols(jj)]

        def landed(k, jj):
            return remote(chunk_of(k, jj, c), chunk_of(k, jj, c), ici_s.at[k, jj], ici_r.at[k, jj], sibling)

        def passed(k, jj):
            return remote(chunk_of(k, jj, c), chunk_of(k, jj, c), d2d_s.at[k, jj], d2d_r.at[k, jj], sibling)

        def from_sibling(k, jj):
            return remote(chunk_of(k, jj, 1 - c), chunk_of(k, jj, 1 - c), d2d_s.at[k, jj], d2d_r.at[k, jj], sibling)

        def fetch(src, slot):
            return pltpu.make_async_copy(src, slots.at[slot], fetch_sem.at[slot])

        @pl.when(jnp.logical_and(tile == 0, i == 0))
        def _():
            own_shard().start()
            for jj in range(qp):
                for k in range(2):
                    to_neighbour(k, jj).start()
            fetch(own_ref.at[:, :, cols(0)], 0).start()

        @pl.when(i == 0)
        def _():
            fetch(own_ref.at[:, :, cols(0)], lax.rem(tile, 2)).wait()

        last_row = i == ni - 1
        nxt = tile + 1
        g1, j1 = where_of(nxt)
        pass_row = i == max(ni - 2, 0)

        @pl.when(jnp.logical_and(pass_row, jnp.logical_and(nxt >= qp, nxt < n_tiles)))
        def _():
            landed(g1 - 1, j1).wait_recv()
            passed(g1 - 1, j1).start()

        @pl.when(jnp.logical_and(pass_row, nxt == 3 * qp - 1))
        def _():
            for ch in range(qp):
                relay(ch).start()
            gather_rest.start(c_in, c_out, c_sem)

        @pl.when(jnp.logical_and(last_row, nxt < qp))
        def _():
            fetch(own_ref.at[:, :, cols(nxt)], lax.rem(nxt, 2)).start()

        @pl.when(jnp.logical_and(last_row, jnp.logical_and(nxt >= qp, nxt < n_tiles)))
        def _():
            from_sibling(g1 - 1, j1).wait_recv()
            fetch(gathered.at[jnp.bitwise_xor(chip, g1), :, :, cols(j1)], lax.rem(nxt, 2)).start()

        w_tile = slots[lax.rem(tile, 2)].reshape(d, tn)
        o_ref[...] = jnp.dot(h_ref[...], w_tile, preferred_element_type=F32).astype(BF16)

        @pl.when(jnp.logical_and(tile == n_tiles - 1, last_row))
        def _():
            for jj in range(qp):
                for k in range(2):
                    to_neighbour(k, jj).wait_send()
                relay(jj).wait_send()
                for k in range(3):
                    passed(k, jj).wait_send()
            own_shard().wait()
            gather_rest.finish(c_in, c_out, c_sem)

    assert qp >= 2 and not gather_rest.aliases
    dma = pltpu.SemaphoreType.DMA
    p = N_CHIPS * pq

    def out_cols(t, i, ch):
        g, jj = where_of(t)
        return (i, jnp.bitwise_xor(ch[0], g) * qp + jj)

    res = pl.pallas_call(
        body, name="proj_gathering",
        grid_spec=pltpu.PrefetchScalarGridSpec(
            num_scalar_prefetch=1, grid=(n_tiles, ni),
            in_specs=[pl.BlockSpec((tm, d), lambda t, i, ch: (i, 0)), ANY] + [ANY] * n_ci,
            out_specs=[pl.BlockSpec((tm, tn), out_cols), ANY] + [ANY] * n_co,
            scratch_shapes=[pltpu.VMEM((2, 2, hh, tn), BF16), dma((2,)),
                            dma((3, qp)), dma((3, qp)), dma((3, qp)), dma((3, qp)), dma, dma]
            + list(gather_rest.scratch)),
        out_shape=[jax.ShapeDtypeStruct((s, p), BF16), jax.ShapeDtypeStruct((N_CHIPS,) + wb.shape, wb.dtype)]
        + list(gather_rest.out_shape),
        compiler_params=_params(("arbitrary",) * 2, tm * d * 2 + tm * tn * 2, 3 * d * tn * 2 + tm * tn * 4),
    )(chip_idx, h, wb, *gather_rest.operands)
    return res[0], res[1], res[2:]


def _gather_packs(pack):
    def body(p_ref, o_ref, send, recv, loc):
        x, y, c = _place()
        me = 4 * x + 2 * y + c
        mine = pltpu.make_async_copy(p_ref, o_ref.at[me], loc)
        mine.start()
        flips = [(fx, fy, fc) for fx in (0, 1) for fy in (0, 1) for fc in (0, 1)][1:]
        cps = []
        for k, (fx, fy, fc) in enumerate(flips):
            peer = (1 - x if fx else x, 1 - y if fy else y, 1 - c if fc else c)
            cps.append(pltpu.make_async_remote_copy(
                src_ref=p_ref, dst_ref=o_ref.at[me], send_sem=send.at[k], recv_sem=recv.at[k],
                device_id=peer, device_id_type=MESH))
        for cp in cps:
            cp.start()
        for k, (fx, fy, fc) in enumerate(flips):
            px, py, pc = (1 - x if fx else x, 1 - y if fy else y, 1 - c if fc else c)
            pltpu.make_async_remote_copy(
                src_ref=p_ref, dst_ref=o_ref.at[4 * px + 2 * py + pc],
                send_sem=send.at[k], recv_sem=recv.at[k],
                device_id=(px, py, pc), device_id_type=MESH).wait_recv()
        for cp in cps:
            cp.wait_send()
        mine.wait()

    dma = pltpu.SemaphoreType.DMA
    return pl.pallas_call(
        body, name="gather_packs", out_shape=jax.ShapeDtypeStruct((N_DEV,) + pack.shape, pack.dtype),
        in_specs=[ANY], out_specs=ANY, scratch_shapes=[dma((7,)), dma((7,)), dma],
    )(pack)


def _flat_pack(pieces):
    flat = jnp.concatenate([p.reshape(-1) for p in pieces])
    pad = (-flat.shape[0]) % (8 * LANES)
    flat = jnp.concatenate([flat, jnp.zeros((pad,), F32)])
    return flat.reshape(-1, LANES)


def _unpack(pack, shapes):
    flat = pack.reshape(-1)
    out, off = [], 0
    for sh in shapes:
        n = 1
        for s in sh:
            n *= s
        out.append(flat[off:off + n].reshape(sh))
        off += n
    return out


def kernel(x, norm_w, w_in, pool_w, pool_scale, conv_w, conv_b, gate_b, w_branch, w_out, final_norm_w, loss_target, m_norm_w, m_w_in, m_pool_w, m_pool_scale, m_conv_w, m_conv_b, m_gate_b, m_w_branch, m_w_out, m_final_norm_w, v_norm_w, v_w_in, v_pool_w, v_pool_scale, v_conv_w, v_conv_b, v_gate_b, v_w_branch, v_w_out, v_final_norm_w):
    _, s, d = x.shape
    w = d // 2
    cg = w // N_POOL_GROUPS
    p = 6 * w + 2 * d
    pq = p // N_CHIPS
    dq = d // N_CHIPS
    assert w_in.shape == (1, d, pq) and w_branch.shape == (1, 2, w, dq) and w_out.shape == (1, dq, d)
    assert pool_w.shape == (1, N_POOL_GROUPS, cg // N_CHIPS, cg) and conv_w.shape == (1, CONV_K, cg)

    x2d = x.reshape(s, d)
    tgt = loss_target.reshape(s, d)
    c_idx = lax.axis_index("c").astype(jnp.int32).reshape(1)
    chip = 2 * lax.axis_index("x") + lax.axis_index("y")

    big_w = [w_in.reshape(d, pq), w_out.reshape(dq, d), w_branch.reshape(2 * w, dq),
             pool_w.reshape(cg, cg)]
    names = ["in", "out", "branch", "pool"]
    big_b = [_cast_bf16("cast_" + nm, a) for nm, a in zip(names, big_w)]
    big_b = [b.reshape(2, b.shape[0] // 2, b.shape[1]) for b in big_b]
    chip_idx = chip.astype(jnp.int32).reshape(1)

    h = _rms_fwd(x2d, norm_w)

    tm = _pick(s, 1024, 16)
    tn_p = _pick(math.gcd(pq, 6 * w, d), 1024, LANES)
    qp = pq // tn_p
    tk_d = _pick(d, 4096, LANES)
    proj, wg_in, (wg_pool, wg_cw, wg_gb) = _proj_gathering(
        h, big_b[0], chip_idx,
        _gather_weights(big_b[3:], [conv_w.reshape(CONV_K, cg), gate_b.reshape(2, dq)]),
        tm=tm, tn=tn_p)
    wg_in = wg_in.reshape(N_CHIPS, d, pq)
    wg_pool = wg_pool.reshape(N_CHIPS, N_POOL_GROUPS, cg // N_CHIPS, cg)

    ys, (wg_br,) = _mix_fwd(proj, wg_pool, pool_scale, wg_cw, conv_b, w,
                            _gather_weights(big_b[2:3], [], relay=True))

    tn_d = _pick(dq, 1024, LANES)
    qd = dq // tn_d
    tk_w = _pick(w, 2048, LANES)
    tm_g = _pick(s, 512, 16)
    gl0 = (6 * w) // tn_d
    gl1 = (6 * w + d) // tn_d

    def gate_specs(im, rows):
        return [pl.BlockSpec((rows, tn_d), lambda *a: (im(*a)[0], gl0 + im(*a)[1])),
                pl.BlockSpec((rows, tn_d), lambda *a: (im(*a)[0], gl1 + im(*a)[1])),
                pl.BlockSpec((None, 2, tn_d), lambda *a: (im(*a)[1] // qd, 0, im(*a)[1] % qd))]

    def merge_epilogue(accs, ins, outs, ids):
        gb = ins[6][...]
        g0 = _sigmoid(ins[4][...].astype(F32) + gb[0:1])
        g1 = _sigmoid(ins[5][...].astype(F32) + gb[1:2])
        outs[0][...] = (g0 * accs[0] + g1 * accs[1]).astype(BF16)
        outs[1][0] = accs[0].astype(BF16)
        outs[1][1] = accs[1].astype(BF16)

    (merged, br), (wg_out,) = _mm(
        "branch_merge", grid=(s // tm_g, d // tn_d, w // tk_w), comm=_gather_weights(big_b[1:2], []),
        operands=[ys, wg_br, ys, wg_br, proj, proj, wg_gb],
        in_specs=[pl.BlockSpec((None, tm_g, tk_w), lambda i, j, k: (0, i, k)),
                  pl.BlockSpec((None, None, tk_w, tn_d), lambda i, j, k: (j // qd, 0, k, j % qd)),
                  pl.BlockSpec((None, tm_g, tk_w), lambda i, j, k: (1, i, k)),
                  pl.BlockSpec((None, None, tk_w, tn_d), lambda i, j, k: (j // qd, 1, k, j % qd)),
                  *gate_specs(lambda i, j, k: (i, j), tm_g)],
        out_shape=[jax.ShapeDtypeStruct((s, d), BF16), jax.ShapeDtypeStruct((2, s, d), BF16)],
        out_specs=[pl.BlockSpec((tm_g, tn_d), lambda i, j, k: (i, j)),
                   pl.BlockSpec((2, tm_g, tn_d), lambda i, j, k: (0, i, j))],
        pairs=[(0, 1, 0, None), (2, 3, 1, None)], dims=NN, acc_shapes=[(tm_g, tn_d)] * 2,
        epilogue=merge_epilogue, temp_bytes=6 * tm_g * tn_d * 4,
    )
    wg_out = wg_out.reshape(d, d)

    tn_f = _pick(d, 1024, LANES)
    o = _mm(
        "out_proj", grid=(s // tm, d // tn_f, d // tk_d), operands=[merged, wg_out],
        in_specs=[pl.BlockSpec((tm, tk_d), lambda i, j, k: (i, k)),
                  pl.BlockSpec((tk_d, tn_f), lambda i, j, k: (k, j))],
        out_shape=[jax.ShapeDtypeStruct((s, d), F32)],
        out_specs=[pl.BlockSpec((tm, tn_f), lambda i, j, k: (i, j))],
        pairs=[(0, 1, 0, None)], dims=NN, acc_shapes=[(tm, tn_f)],
        epilogue=lambda accs, ins, outs, ids: outs[0].__setitem__(Ellipsis, accs[0]),
    )[0][0]

    dx2, dx2b, loss_part, g_fnw =_head(x2d, o, tgt, final_norm_w.reshape(1, d))

    def gate_bwd_epilogue(accs, ins, outs, ids):
        dm = accs[0]
        gb = ins[5][...]
        i = ids[1]

        @pl.when(i == 0)
        def _():
            outs[2][...] = jnp.zeros(outs[2].shape, F32)

        for n in range(2):
            gate = _sigmoid(ins[3 + n][...].astype(F32) + gb[n:n + 1])
            outs[0][n] = (dm * gate).astype(BF16)
            dgl = dm * ins[2][n].astype(F32) * gate * (1.0 - gate)
            outs[1][n] = dgl.astype(BF16)
            outs[2][n:n + 1, :] += jnp.sum(dgl, axis=0, keepdims=True)

    tm_b = _pick(s, 512, 16)
    (d_br, dgl, g_gb), _ = _mm(
        "out_proj_bwd_gate", grid=(d // tn_d, s // tm_b, d // tk_d),
        operands=[dx2b, wg_out, br, proj, proj, wg_gb],
        in_specs=[pl.BlockSpec((tm_b, tk_d), lambda j, i, k: (i, k)),
                  pl.BlockSpec((tn_d, tk_d), lambda j, i, k: (j, k)),
                  pl.BlockSpec((2, tm_b, tn_d), lambda j, i, k: (0, i, j)),
                  *gate_specs(lambda j, i, k: (i, j), tm_b)],
        out_shape=[jax.ShapeDtypeStruct((2, s, d), BF16), jax.ShapeDtypeStruct((2, s, d), BF16),
                   jax.ShapeDtypeStruct((2, d), F32)],
        out_specs=[pl.BlockSpec((2, tm_b, tn_d), lambda j, i, k: (0, i, j)),
                   pl.BlockSpec((2, tm_b, tn_d), lambda j, i, k: (0, i, j)),
                   pl.BlockSpec((2, tn_d), lambda j, i, k: (0, j))],
        pairs=[(0, 1, 0, None)], dims=NT, acc_shapes=[(tm_b, tn_d)],
        epilogue=gate_bwd_epilogue, semantics=("parallel", "arbitrary", "arbitrary"),
        temp_bytes=8 * tm_b * tn_d * 4,
    )

    hh_out = d // 8
    tm_o = _pick(hh_out, 512, LANES)
    nb_o = hh_out // tm_o
    tk_s = _pick(s, 4096, LANES)
    g_out = _mm(
        "w_out_grad", grid=(d // tm_o, d // tn_f, s // tk_s), operands=[merged, dx2b],
        in_specs=[pl.BlockSpec((tk_s, tm_o), lambda i, j, k: (k, i)),
                  pl.BlockSpec((tk_s, tn_f), lambda i, j, k: (k, j))],
        out_shape=[jax.ShapeDtypeStruct((2, N_CHIPS, hh_out, d), BF16)],
        out_specs=[pl.BlockSpec((None, None, tm_o, tn_f),
                                lambda i, j, k: ((i // nb_o) % 2, i // (2 * nb_o), i % nb_o, j))],
        pairs=[(0, 1, 0, None)], dims=TN, acc_shapes=[(tm_o, tn_f)],
        epilogue=lambda accs, ins, outs, ids: outs[0].__setitem__(Ellipsis, accs[0].astype(BF16)),
    )[0][0]

    tn_w = _pick(w, 2048, LANES)
    tk_q = _pick(dq, 1024, LANES)
    qk = dq // tk_q
    (dys,), from_sibling_out = _mm(
        "branch_bwd", grid=(2, s // tm, w // tn_w, d // tk_q), operands=[d_br, wg_br],
        comm=_to_sibling([g_out], other_half=True),
        in_specs=[pl.BlockSpec((None, tm, tk_q), lambda n, i, j, k: (n, i, k)),
                  pl.BlockSpec((None, None, tn_w, tk_q), lambda n, i, j, k: (k // qk, n, j, k % qk))],
        out_shape=[jax.ShapeDtypeStruct((2, s, w), F32)],
        out_specs=[pl.BlockSpec((None, tm, tn_w), lambda n, i, j, k: (n, i, j))],
        pairs=[(0, 1, 0, None)], dims=NT, acc_shapes=[(tm, tn_w)],
        epilogue=lambda accs, ins, outs, ids: outs[0].__setitem__(Ellipsis, accs[0]),
    )

    tm_w = _pick(w, 1024, LANES)
    g_br = _mm(
        "w_branch_grad", grid=(2, w // tm_w, d // tn_d, s // tk_s), operands=[ys, d_br],
        in_specs=[pl.BlockSpec((None, tk_s, tm_w), lambda n, i, j, k: (n, k, i)),
                  pl.BlockSpec((None, tk_s, tn_d), lambda n, i, j, k: (n, k, j))],
        out_shape=[jax.ShapeDtypeStruct((2, N_CHIPS, w, dq), BF16)],
        out_specs=[pl.BlockSpec((None, None, tm_w, tn_d), lambda n, i, j, k: (n, j // qd, i, j % qd))],
        pairs=[(0, 1, 0, None)], dims=TN, acc_shapes=[(tm_w, tn_d)],
        epilogue=lambda accs, ins, outs, ids: outs[0].__setitem__(Ellipsis, accs[0].astype(BF16)),
    )[0][0]

    dpa, g_pool, g_ps, g_cw, g_cb = _mix_bwd(proj, dys, wg_pool, pool_scale, wg_cw, conv_b, w)
    g_pool = g_pool.astype(BF16).reshape(2, N_CHIPS, 2 * (cg // N_CHIPS), cg)

    early = [g_out, g_br, g_pool]
    from_sibling_early = list(from_sibling_out) + _run_exchange(
        "pair_exchange", _to_sibling(early[1:], other_half=True))
    parts_early = []
    for nm, g, r1 in zip(names[1:], early, from_sibling_early):
        _, nc, hh, wd = g.shape
        pt = _pair_add("pair_add_" + nm, g.reshape(2, nc * hh, wd), r1.reshape(nc * hh, wd), c_idx)
        parts_early.append(pt.reshape(nc, hh, wd))

    na = (6 * w) // tn_p
    qg = d // tn_p
    hh_in = d // 2
    assert s == tk_s
    tm_i = _pick(hh_in, 1024, LANES)
    tn_i = _pick(tn_p, 512, LANES)
    from_sibling, recv_early = _w_in_grad_sibling(
        h, dpa, dgl, c_idx, _scatter_partials(parts_early), tm=tm_i, tn=tn_i, rows=hh_in, pq=pq)

    tm_o2 = _pick(hh_in, 512, LANES)
    nb_o2 = hh_in // tm_o2

    def own_rows(name, first, count, comm):
        def add_sibling(accs, ins, outs, ids):
            outs[0][...] = (accs[0] + ins[3][...].astype(F32)).astype(BF16)

        return _mm(
            name, grid=(p // tn_p, count, 1), operands=[h, dpa, dgl, from_sibling], comm=comm, prefetch=c_idx,
            in_specs=[pl.BlockSpec((tk_s, tm_o2), lambda j, i, k, cr: (0, cr[0] * nb_o2 + first + i)),
                      pl.BlockSpec((tk_s, tn_p), lambda j, i, k, cr: (0, jnp.minimum(j, na - 1))),
                      pl.BlockSpec((None, tk_s, tn_p),
                                   lambda j, i, k, cr: (jnp.maximum(j - na, 0) // qg, 0, jnp.maximum(j - na, 0) % qg)),
                      pl.BlockSpec((None, tm_o2, tn_p), lambda j, i, k, cr: (j // qp, first + i, j % qp))],
            out_shape=[jax.ShapeDtypeStruct((N_CHIPS, count * tm_o2, pq), BF16)],
            out_specs=[pl.BlockSpec((None, tm_o2, tn_p), lambda j, i, k, cr: (j // qp, i, j % qp))],
            pairs=[(0, 1, 0, lambda ids: ids[0] < na), (0, 2, 0, lambda ids: ids[0] >= na)],
            dims=TN, acc_shapes=[(tm_o2, tn_p)], epilogue=add_sibling)

    halves_early = [_sum_chips("sum_chips_" + nm, pt, r2, chip_idx)
                    for nm, pt, r2 in zip(names[1:], parts_early, recv_early)]
    share_early = _to_sibling(halves_early, other_half=False)

    n_first = 1 if nb_o2 > 1 else 0
    if n_first:
        (part_a,), others_early = own_rows("w_in_grad_own_a", 0, n_first, share_early)
        scatter_a = _scatter_partials([part_a], rows=hh_in)
        (part_b,), land_a = own_rows("w_in_grad_own_b", n_first, nb_o2 - n_first, scatter_a)
        scatter_b = _scatter_partials([part_b], rows=hh_in, row0=n_first * tm_o2, prev=land_a)
    else:
        (part_b,), others_early = own_rows("w_in_grad_own", 0, nb_o2, share_early)
        scatter_b = _scatter_partials([part_b])

    tn_h = _pick(d, 2048, LANES)
    (dh,), recv_in = _mm(
        "proj_bwd", grid=(s // tm, d // tn_h, p // tn_p), operands=[dpa, dgl, wg_in],
        comm=scatter_b,
        in_specs=[pl.BlockSpec((tm, tn_p), lambda i, j, k: (i, jnp.minimum(k, na - 1))),
                  pl.BlockSpec((None, tm, tn_p),
                               lambda i, j, k: (jnp.maximum(k - na, 0) // qg, i, jnp.maximum(k - na, 0) % qg)),
                  pl.BlockSpec((None, tn_h, tn_p), lambda i, j, k: (k // qp, j, k % qp))],
        out_shape=[jax.ShapeDtypeStruct((s, d), F32)],
        out_specs=[pl.BlockSpec((tm, tn_h), lambda i, j, k: (i, j))],
        pairs=[(0, 2, 0, lambda ids: ids[2] < na), (1, 2, 0, lambda ids: ids[2] >= na)],
        dims=NT, acc_shapes=[(tm, tn_h)],
        epilogue=lambda accs, ins, outs, ids: outs[0].__setitem__(Ellipsis, accs[0]),
    )

    grad_x, g_nw = _rms_bwd(x2d, dh, dx2, norm_w)

    if n_first:
        half_in = _sum_chips("sum_chips_in_a", part_a, recv_in[0], chip_idx)
        half_in = _sum_chips("sum_chips_in_b", part_b, recv_in[0], chip_idx, row0=n_first * tm_o2, prev=half_in)
    else:
        half_in = _sum_chips("sum_chips_in", part_b, recv_in[0], chip_idx)
    halves = [half_in] + halves_early
    others = _run_exchange("pair_share", _to_sibling([half_in], other_half=False)) + list(others_early)

    g_cw_full = jnp.transpose(g_cw, (1, 0, 2)).reshape(CONV_K, w)
    small_shapes = [(LANES,), (1, d), (1, d), (1, w), (1, w), (CONV_K, w), (2, d)]
    pack = _flat_pack([loss_part[0], g_nw, g_fnw, g_ps, g_cb, g_cw_full, g_gb])
    total = _sum_devices(_gather_packs(pack))
    t_loss, t_nw, t_fnw, t_ps, t_cb, t_cw, t_gb = _unpack(total, small_shapes)
    loss = t_loss[0]
    t_cw = lax.dynamic_slice_in_dim(t_cw, chip * cg, cg, axis=1)
    t_gb = lax.dynamic_slice_in_dim(t_gb, chip * dq, dq, axis=1)

    out = {}
    big_names = ["w_in", "w_out", "w_branch", "pool_w"]
    big_m = [m_w_in, m_w_out, m_w_branch, m_pool_w]
    big_v = [v_w_in, v_w_out, v_w_branch, v_pool_w]
    big_orig = [w_in, w_out, w_branch, pool_w]
    for nm, g_own, g_other, w2, mm_, vv_, orig in zip(big_names, halves, others, big_w, big_m, big_v, big_orig):
        sh = (2,) + g_own.shape
        res = _adamw_halves("adamw_" + nm, g_own, g_other, c_idx,
                            w2.reshape(sh), mm_.reshape(sh), vv_.reshape(sh))
        out[nm] = [r.reshape(orig.shape) for r in res]

    sm_names = ["norm_w", "final_norm_w", "pool_scale", "conv_b", "conv_w", "gate_b"]
    sm_g = [t_nw, t_fnw, t_ps, t_cb, t_cw, t_gb]
    sm_w = [norm_w, final_norm_w, pool_scale, conv_b, conv_w, gate_b]
    sm_m = [m_norm_w, m_final_norm_w, m_pool_scale, m_conv_b, m_conv_w, m_gate_b]
    sm_v = [v_norm_w, v_final_norm_w, v_pool_scale, v_conv_b, v_conv_w, v_gate_b]
    sm_shapes = [a.shape for a in sm_w]
    res = _adamw("adamw_small", _flat_pack(sm_g), _flat_pack(sm_w), _flat_pack(sm_m), _flat_pack(sm_v))
    res = [_unpack(r, sm_shapes) for r in res]
    for idx, nm in enumerate(sm_names):
        out[nm] = [r[idx] for r in res]

    order = ["norm_w", "w_in", "pool_w", "pool_scale", "conv_w", "conv_b", "gate_b", "w_branch", "w_out",
             "final_norm_w"]
    outs = [loss, grad_x.reshape(x.shape)]
    for kind in range(4):
        outs += [out[nm][kind] for nm in order]
    return tuple(outs)
```

```python
import math

import jax
import jax.numpy as jnp
from jax import lax
from jax.experimental import pallas as pl
from jax.experimental.pallas import tpu as pltpu

F32 = jnp.float32
BF16 = jnp.bfloat16

NORM_EPS = 1e-6
POOL_WINDOWS = (2, 4, 8, 16)
N_POOL_GROUPS = len(POOL_WINDOWS)
CONV_K = 3
ADAM_LR = 0.001
ADAM_B1 = 0.9
ADAM_B2 = 0.999
ADAM_EPS = 1e-08
ADAM_WD = 0.01
ADAM_STEP = 10

N_CHIPS = 4
N_DEV = 8
HALO = 16
LANES = 128
V7X_VMEM_BYTES = 64 * 1024 * 1024
VMEM_CAP = V7X_VMEM_BYTES - 8 * 1024 * 1024

MESH = pl.DeviceIdType.MESH
ANY = pl.BlockSpec(memory_space=pl.ANY)

NN = (((1,), (0,)), ((), ()))
NT = (((1,), (1,)), ((), ()))
TN = (((0,), (0,)), ((), ()))


def _pick(dim, pref, align):
    if dim <= pref:
        return dim
    t = (pref // align) * align
    while t >= align:
        if dim % t == 0:
            return t
        t -= align
    raise ValueError(f"no tile for {dim} (pref {pref}, align {align})")


def _nbytes(shape, dtype):
    n = 1
    for s in shape:
        if s is not None:
            n *= s
    return n * jnp.dtype(dtype).itemsize


def _params(semantics, block_bytes, extra_bytes=0):
    need = 2 * block_bytes + extra_bytes + (2 << 20)
    return pltpu.CompilerParams(dimension_semantics=semantics,
                                vmem_limit_bytes=int(min(max(need, 16 << 20), VMEM_CAP)))


def _sigmoid(z):
    return jax.nn.sigmoid(z)


def _mm(name, *, grid, operands, in_specs, out_shape, out_specs, pairs, dims, acc_shapes, epilogue,
        semantics=None, temp_bytes=0, comm=None, prefetch=None, aliases=None):
    aliases = dict(aliases or {})
    n_in, n_out = len(operands), len(out_shape)
    kax = len(grid) - 1
    nk = grid[kax]
    single = nk == 1
    conditional = any(p[3] is not None for p in pairs)
    if single and conditional:
        assert len(acc_shapes) == 1 and all(p[3] is not None for p in pairs)
    n_acc = 0 if single else len(acc_shapes)
    n_pf = 0 if prefetch is None else 1
    c_ops = list(comm.operands) if comm else []
    c_out = list(comm.out_shape) if comm else []
    c_sems = list(comm.scratch) if comm else []
    c_alias = dict(comm.aliases) if comm else {}
    n_ci, n_co = len(c_ops), len(c_out)

    def product(ins, ai, bi):
        return lax.dot_general(ins[ai][...], ins[bi][...], dims, preferred_element_type=F32)

    def body(*refs):
        refs = refs[n_pf:]
        ins = refs[:n_in]
        c_in_refs = refs[n_in:n_in + n_ci]
        outs = refs[n_in + n_ci:n_in + n_ci + n_out]
        c_out_refs = refs[n_in + n_ci + n_out:n_in + n_ci + n_out + n_co]
        scratch = refs[n_in + n_ci + n_out + n_co:]
        accs = scratch[:n_acc]
        sems = scratch[n_acc:]
        ids = [pl.program_id(a) for a in range(len(grid))]
        k = ids[kax]

        if comm:
            is_first = ids[0] == 0
            is_last = ids[0] == grid[0] - 1
            for a in range(1, len(grid)):
                is_first = jnp.logical_and(is_first, ids[a] == 0)
                is_last = jnp.logical_and(is_last, ids[a] == grid[a] - 1)

            @pl.when(is_first)
            def _():
                comm.start(c_in_refs, c_out_refs, sems)

        if single and conditional:
            for (ai, bi, ci, cond) in pairs:
                def only(ai=ai, bi=bi):
                    epilogue([product(ins, ai, bi)], ins, outs, ids)
                pl.when(cond(ids))(only)
        elif single:
            vals = [None] * len(acc_shapes)
            for (ai, bi, ci, cond) in pairs:
                r = product(ins, ai, bi)
                vals[ci] = r if vals[ci] is None else vals[ci] + r
            epilogue(vals, ins, outs, ids)
        else:
            @pl.when(k == 0)
            def _():
                for a in accs:
                    a[...] = jnp.zeros(a.shape, a.dtype)

            for (ai, bi, ci, cond) in pairs:
                def step(ai=ai, bi=bi, ci=ci):
                    accs[ci][...] += product(ins, ai, bi)
                if cond is None:
                    step()
                else:
                    pl.when(cond(ids))(step)

            @pl.when(k == nk - 1)
            def _():
                epilogue([a[...] for a in accs], ins, outs, ids)

        if comm:
            @pl.when(is_last)
            def _():
                comm.finish(c_in_refs, c_out_refs, sems)

    if semantics is None:
        semantics = ("parallel",) * kax + ("arbitrary",)
    if comm:
        semantics = ("arbitrary",) * len(grid)
    in_specs = [ANY if idx in aliases else spec for idx, spec in enumerate(in_specs)]
    blk = 0
    for idx, (spec, op) in enumerate(zip(in_specs, operands)):
        if idx not in aliases:
            blk += _nbytes(spec.block_shape, op.dtype)
    for spec, o in zip(out_specs, out_shape):
        blk += _nbytes(spec.block_shape, o.dtype)
    acc_bytes = sum(_nbytes(s, F32) for s in acc_shapes)
    io_alias = {n_pf + i: o for i, o in aliases.items()}
    io_alias.update({n_pf + n_in + i: n_out + o for i, o in c_alias.items()})
    all_in = list(in_specs) + [ANY] * n_ci
    all_out = list(out_specs) + [ANY] * n_co
    scratch_shapes = [pltpu.VMEM(s, F32) for s in acc_shapes[:n_acc]] + c_sems
    params = _params(semantics, blk, 3 * acc_bytes + temp_bytes)
    shapes = list(out_shape) + c_out
    if prefetch is None:
        call = pl.pallas_call(
            body, name=name, grid=grid, in_specs=all_in, out_specs=all_out, out_shape=shapes,
            scratch_shapes=scratch_shapes, input_output_aliases=io_alias, compiler_params=params)
        res = call(*operands, *c_ops)
    else:
        call = pl.pallas_call(
            body, name=name, out_shape=shapes, input_output_aliases=io_alias, compiler_params=params,
            grid_spec=pltpu.PrefetchScalarGridSpec(
                num_scalar_prefetch=1, grid=grid, in_specs=all_in, out_specs=all_out,
                scratch_shapes=scratch_shapes))
        res = call(prefetch, *operands, *c_ops)
    return res[:n_out], res[n_out:]


def _cast_bf16(name, w2d):
    r, c = w2d.shape
    tr = _pick(r, max(16, (4 << 20) // (4 * c)), 16)

    def body(w_ref, o_ref):
        o_ref[...] = w_ref[...].astype(BF16)

    return pl.pallas_call(
        body, name=name, grid=(r // tr,),
        in_specs=[pl.BlockSpec((tr, c), lambda i: (i, 0))],
        out_specs=pl.BlockSpec((tr, c), lambda i: (i, 0)),
        out_shape=jax.ShapeDtypeStruct((r, c), BF16),
        compiler_params=_params(("parallel",), tr * c * 6),
    )(w2d)


def _rms_fwd(x, norm_w):
    s, d = x.shape
    ts = _pick(s, 256, 16)

    def body(x_ref, w_ref, h_ref):
        xv = x_ref[...]
        rstd = lax.rsqrt(jnp.mean(xv * xv, axis=-1, keepdims=True) + NORM_EPS)
        h_ref[...] = (xv * rstd * w_ref[...]).astype(BF16)

    return pl.pallas_call(
        body, name="rms_fwd", grid=(s // ts,),
        in_specs=[pl.BlockSpec((ts, d), lambda i: (i, 0)), pl.BlockSpec((1, d), lambda i: (0, 0))],
        out_specs=pl.BlockSpec((ts, d), lambda i: (i, 0)),
        out_shape=jax.ShapeDtypeStruct((s, d), BF16),
        compiler_params=_params(("parallel",), ts * d * 6, 3 * ts * d * 4),
    )(x, norm_w)


def _head(x, o, target, fnw):
    s, d = x.shape
    ts = _pick(s, 128, 16)

    def body(x_ref, o_ref, t_ref, w_ref, dx_ref, dxb_ref, loss_ref, gw_ref):
        i = pl.program_id(0)

        @pl.when(i == 0)
        def _():
            loss_ref[...] = jnp.zeros(loss_ref.shape, F32)
            gw_ref[...] = jnp.zeros(gw_ref.shape, F32)

        w = w_ref[...]
        x2 = x_ref[...] + o_ref[...]
        rstd = lax.rsqrt(jnp.mean(x2 * x2, axis=-1, keepdims=True) + NORM_EPS)
        n = x2 * rstd
        e = n * w - t_ref[...]
        loss_ref[...] += 0.5 * jnp.sum(e * e) / d
        dy = e / d
        gw_ref[...] += jnp.sum(dy * n, axis=0, keepdims=True)
        gy = dy * w
        dx = rstd * (gy - n * jnp.mean(gy * n, axis=-1, keepdims=True))
        dx_ref[...] = dx
        dxb_ref[...] = dx.astype(BF16)

    row = pl.BlockSpec((ts, d), lambda i: (i, 0))
    return pl.pallas_call(
        body, name="head", grid=(s // ts,),
        in_specs=[row, row, row, pl.BlockSpec((1, d), lambda i: (0, 0))],
        out_specs=[row, row, pl.BlockSpec((8, LANES), lambda i: (0, 0)),
                   pl.BlockSpec((1, d), lambda i: (0, 0))],
        out_shape=[jax.ShapeDtypeStruct((s, d), F32), jax.ShapeDtypeStruct((s, d), BF16),
                   jax.ShapeDtypeStruct((8, LANES), F32), jax.ShapeDtypeStruct((1, d), F32)],
        compiler_params=_params(("arbitrary",), ts * d * 22, 6 * ts * d * 4),
    )(x, o, target, fnw)


def _rms_bwd(x, dh, dx2, norm_w):
    s, d = x.shape
    ts = _pick(s, 128, 16)

    def body(x_ref, dh_ref, dx2_ref, w_ref, gx_ref, gw_ref):
        i = pl.program_id(0)

        @pl.when(i == 0)
        def _():
            gw_ref[...] = jnp.zeros(gw_ref.shape, F32)

        xv = x_ref[...]
        rstd = lax.rsqrt(jnp.mean(xv * xv, axis=-1, keepdims=True) + NORM_EPS)
        n = xv * rstd
        dhv = dh_ref[...]
        gw_ref[...] += jnp.sum(dhv * n, axis=0, keepdims=True)
        gh = dhv * w_ref[...]
        gx_ref[...] = dx2_ref[...] + rstd * (gh - n * jnp.mean(gh * n, axis=-1, keepdims=True))

    row = pl.BlockSpec((ts, d), lambda i: (i, 0))
    vec = pl.BlockSpec((1, d), lambda i: (0, 0))
    return pl.pallas_call(
        body, name="rms_bwd", grid=(s // ts,),
        in_specs=[row, row, row, vec], out_specs=[row, vec],
        out_shape=[jax.ShapeDtypeStruct((s, d), F32), jax.ShapeDtypeStruct((1, d), F32)],
        compiler_params=_params(("arbitrary",), ts * d * 16, 5 * ts * d * 4),
    )(x, dh, dx2, norm_w)


def _silu(z):
    return z * _sigmoid(z)


def _window_sum(ext, window, back):
    n = ext.shape[0]
    acc = ext
    step = 1
    while step < window:
        acc = acc + pltpu.roll(acc, step if back else n - step, 0)
        step *= 2
    return acc


def _shift_rows(ext, k, back):
    n = ext.shape[0]
    return pltpu.roll(ext, k if back else n - k, 0)


def _mix_fwd(proj, pool_wg, pool_scale, conv_wg, conv_b, width, comm):
    s = proj.shape[0]
    w = width
    cg = w // N_POOL_GROUPS
    ts = _pick(s, 128, HALO)
    hb = ts // HALO
    cols = 6 * w

    n_ci, n_co = len(comm.operands), len(comm.out_shape)
    n_steps = s // ts

    def body(p_ref, ph_ref, pw_ref, ps_ref, cw_ref, cb_ref, *rest):
        c_in, ys_ref = rest[:n_ci], rest[n_ci]
        c_out, sems = rest[n_ci + 1:n_ci + 1 + n_co], rest[n_ci + 1 + n_co:]
        i = pl.program_id(0)
        first = i == 0
        t1 = (i * ts + lax.broadcasted_iota(jnp.int32, (ts, 1), 0) + 1).astype(F32)

        @pl.when(first)
        def _():
            comm.start(c_in, c_out, sems)

        def tile(part, g):
            lo = part * w + g * cg
            return p_ref[:, lo:lo + cg].astype(F32)

        def prev(part, g):
            lo = part * w + g * cg
            return jnp.where(first, 0.0, ph_ref[:, lo:lo + cg].astype(F32))

        for g, win in enumerate(POOL_WINDOWS):
            gs = slice(g * cg, (g + 1) * cg)
            u = tile(0, g)
            ext = jnp.concatenate([prev(0, g), u], axis=0)
            wsum = _window_sum(ext, win, True)[HALO:]
            pooled = wsum / jnp.minimum(t1, float(win)) - u
            pw = pw_ref[:, g].reshape(cg, cg)
            mixed = jnp.dot(pooled.astype(BF16), pw, preferred_element_type=F32)
            ys_ref[0, :, gs] = (mixed * ps_ref[:, gs] * _silu(tile(1, g))).astype(BF16)
            v = tile(4, g) * tile(2, g)
            vext = jnp.concatenate([prev(4, g) * prev(2, g), v], axis=0)
            v1 = _shift_rows(vext, 1, True)[HALO:]
            v2 = _shift_rows(vext, 2, True)[HALO:]
            cw = [cw_ref[g, tap:tap + 1, :] for tap in range(CONV_K)]
            y = cb_ref[:, gs] + cw[0] * v2 + cw[1] * v1 + cw[2] * v
            ys_ref[1, :, gs] = (tile(3, g) * y * _silu(tile(5, g))).astype(BF16)

        @pl.when(i == n_steps - 1)
        def _():
            comm.finish(c_in, c_out, sems)

    res = pl.pallas_call(
        body, name="mix_fwd", grid=(n_steps,),
        in_specs=[pl.BlockSpec((ts, cols), lambda i: (i, 0)),
                  pl.BlockSpec((HALO, cols), lambda i: (jnp.maximum(i * hb - 1, 0), 0)),
                  pl.BlockSpec(pool_wg.shape, lambda i: (0, 0, 0, 0)),
                  pl.BlockSpec((1, w), lambda i: (0, 0)),
                  pl.BlockSpec(conv_wg.shape, lambda i: (0, 0, 0)),
                  pl.BlockSpec((1, w), lambda i: (0, 0))] + [ANY] * n_ci,
        out_specs=[pl.BlockSpec((2, ts, w), lambda i: (0, i, 0))] + [ANY] * n_co,
        out_shape=[jax.ShapeDtypeStruct((2, s, w), BF16)] + list(comm.out_shape),
        scratch_shapes=list(comm.scratch),
        input_output_aliases={6 + i: 1 + o for i, o in comm.aliases.items()},
        compiler_params=_params(("arbitrary",), (ts + HALO) * cols * 2 + 2 * ts * w * 2
                                + _nbytes(pool_wg.shape, BF16), 24 * (ts + HALO) * cg * 4),
    )(proj, proj, pool_wg, pool_scale, conv_wg, conv_b, *comm.operands)
    return res[0], res[1:]


def _mix_bwd(proj, dys, pool_wg, pool_scale, conv_wg, conv_b, width):
    s = proj.shape[0]
    w = width
    cg = w // N_POOL_GROUPS
    ts = _pick(s, 128, HALO)
    hb = ts // HALO
    n_tiles = s // ts
    last_hb = s // HALO - 1
    cols = 6 * w

    def body(p_ref, ph_ref, pn_ref, dy_ref, dyn_ref, pw_ref, ps_ref, cw_ref, cb_ref,
             dp_ref, dpw_ref, dps_ref, dcw_ref, dcb_ref):
        i = pl.program_id(0)
        first = i == 0
        last = i == n_tiles - 1

        @pl.when(first)
        def _():
            dpw_ref[...] = jnp.zeros(dpw_ref.shape, F32)
            dps_ref[...] = jnp.zeros(dps_ref.shape, F32)
            dcw_ref[...] = jnp.zeros(dcw_ref.shape, F32)
            dcb_ref[...] = jnp.zeros(dcb_ref.shape, F32)

        row = i * ts + lax.broadcasted_iota(jnp.int32, (ts + HALO, 1), 0)
        t1_ext = (row + 1).astype(F32)
        t1 = t1_ext[:ts]

        def tile(part, g):
            lo = part * w + g * cg
            return p_ref[:, lo:lo + cg].astype(F32)

        def prev(part, g):
            lo = part * w + g * cg
            return jnp.where(first, 0.0, ph_ref[:, lo:lo + cg].astype(F32))

        def ahead(part, g):
            lo = part * w + g * cg
            return jnp.concatenate([tile(part, g), pn_ref[:, lo:lo + cg].astype(F32)], axis=0)

        def dy_ahead(n, g):
            gs = slice(g * cg, (g + 1) * cg)
            nxt = jnp.where(last, 0.0, dyn_ref[n, :, gs])
            return jnp.concatenate([dy_ref[n, :, gs], nxt], axis=0)

        for g, win in enumerate(POOL_WINDOWS):
            gs = slice(g * cg, (g + 1) * cg)
            u = tile(0, g)
            ext = jnp.concatenate([prev(0, g), u], axis=0)
            pooled = _window_sum(ext, win, True)[HALO:] / jnp.minimum(t1, float(win)) - u
            pooled_b = pooled.astype(BF16)
            pw = pw_ref[:, g].reshape(cg, cg)
            mixed = jnp.dot(pooled_b, pw, preferred_element_type=F32)
            zp_ext = ahead(1, g)
            dy0_ext = dy_ahead(0, g)
            scale = ps_ref[:, gs]
            sp_ext = _sigmoid(zp_ext)
            dms_ext = dy0_ext * (zp_ext * sp_ext)
            dmix_b = (dms_ext * scale).astype(BF16)
            dpooled_ext = lax.dot_general(dmix_b, pw, NT, preferred_element_type=F32)
            dy0 = dy0_ext[:ts]
            zp, sp = zp_ext[:ts], sp_ext[:ts]
            dsilu_p = sp * (1.0 + zp * (1.0 - sp))
            dp_ref[:, w + g * cg:w + (g + 1) * cg] = (dy0 * mixed * scale * dsilu_p).astype(BF16)
            dps_ref[:, gs] += jnp.sum(dms_ext[:ts] * mixed, axis=0, keepdims=True)
            dpw = lax.dot_general(pooled_b, dmix_b[:ts], TN, preferred_element_type=F32)
            dpw_ref[g // 2, :, g % 2] += dpw.reshape(N_CHIPS, cg // N_CHIPS, cg)
            q_ext = dpooled_ext / jnp.minimum(t1_ext, float(win))
            du = _window_sum(q_ext, win, False)[:ts] - dpooled_ext[:ts]
            dp_ref[:, gs] = du.astype(BF16)
            uc = tile(2, g)
            cc = tile(4, g)
            v = cc * uc
            vext = jnp.concatenate([prev(4, g) * prev(2, g), v], axis=0)
            v1 = _shift_rows(vext, 1, True)[HALO:]
            v2 = _shift_rows(vext, 2, True)[HALO:]
            cw = [cw_ref[g, tap:tap + 1, :] for tap in range(CONV_K)]
            y = cb_ref[:, gs] + cw[0] * v2 + cw[1] * v1 + cw[2] * v
            bc_ext = ahead(3, g)
            zc_ext = ahead(5, g)
            dy1_ext = dy_ahead(1, g)
            sc_ext = _sigmoid(zc_ext)
            silu_c_ext = zc_ext * sc_ext
            dyy_ext = dy1_ext * bc_ext * silu_c_ext
            dy1 = dy1_ext[:ts]
            bc = bc_ext[:ts]
            zc, sc = zc_ext[:ts], sc_ext[:ts]
            dsilu_c = sc * (1.0 + zc * (1.0 - sc))
            dp_ref[:, 3 * w + g * cg:3 * w + (g + 1) * cg] = (dy1 * y * silu_c_ext[:ts]).astype(BF16)
            dp_ref[:, 5 * w + g * cg:5 * w + (g + 1) * cg] = (dy1 * bc * y * dsilu_c).astype(BF16)
            dyy = dyy_ext[:ts]
            dcb_ref[:, gs] += jnp.sum(dyy, axis=0, keepdims=True)
            for tap, vt in enumerate((v2, v1, v)):
                dcw_ref[g, tap:tap + 1, :] += jnp.sum(dyy * vt, axis=0, keepdims=True)
            dv = (cw[2] * dyy + cw[1] * _shift_rows(dyy_ext, 1, False)[:ts]
                  + cw[0] * _shift_rows(dyy_ext, 2, False)[:ts])
            dp_ref[:, 4 * w + g * cg:4 * w + (g + 1) * cg] = (dv * uc).astype(BF16)
            dp_ref[:, 2 * w + g * cg:2 * w + (g + 1) * cg] = (dv * cc).astype(BF16)

    dpw_shape = (2, N_CHIPS, 2, cg // N_CHIPS, cg)
    return pl.pallas_call(
        body, name="mix_bwd", grid=(n_tiles,),
        in_specs=[pl.BlockSpec((ts, cols), lambda i: (i, 0)),
                  pl.BlockSpec((HALO, cols), lambda i: (jnp.maximum(i * hb - 1, 0), 0)),
                  pl.BlockSpec((HALO, cols), lambda i: (jnp.minimum((i + 1) * hb, last_hb), 0)),
                  pl.BlockSpec((2, ts, w), lambda i: (0, i, 0)),
                  pl.BlockSpec((2, HALO, w), lambda i: (0, jnp.minimum((i + 1) * hb, last_hb), 0)),
                  pl.BlockSpec(pool_wg.shape, lambda i: (0, 0, 0, 0)),
                  pl.BlockSpec((1, w), lambda i: (0, 0)),
                  pl.BlockSpec(conv_wg.shape, lambda i: (0, 0, 0)),
                  pl.BlockSpec((1, w), lambda i: (0, 0))],
        out_specs=[pl.BlockSpec((ts, cols), lambda i: (i, 0)),
                   pl.BlockSpec(dpw_shape, lambda i: (0, 0, 0, 0, 0)),
                   pl.BlockSpec((1, w), lambda i: (0, 0)),
                   pl.BlockSpec(conv_wg.shape, lambda i: (0, 0, 0)),
                   pl.BlockSpec((1, w), lambda i: (0, 0))],
        out_shape=[jax.ShapeDtypeStruct((s, cols), BF16), jax.ShapeDtypeStruct(dpw_shape, F32),
                   jax.ShapeDtypeStruct((1, w), F32), jax.ShapeDtypeStruct(conv_wg.shape, F32),
                   jax.ShapeDtypeStruct((1, w), F32)],
        compiler_params=_params(("arbitrary",), (2 * ts + 2 * HALO) * cols * 2 + (ts + HALO) * w * 8
                                + _nbytes(pool_wg.shape, BF16) + _nbytes(dpw_shape, F32),
                                40 * (ts + HALO) * cg * 4),
    )(proj, proj, proj, dys, dys, pool_wg, pool_scale, conv_wg, conv_b)


def _adamw(name, g, w, m, v):
    r, c = w.shape
    tr = _pick(r, max(8, (1 << 20) // (4 * c)), 8)

    def body(g_ref, w_ref, m_ref, v_ref, go_ref, d_ref, mo_ref, vo_ref):
        gv = g_ref[...]
        mn = ADAM_B1 * m_ref[...] + (1.0 - ADAM_B1) * gv
        vn = ADAM_B2 * v_ref[...] + (1.0 - ADAM_B2) * (gv * gv)
        m_hat = mn / (1.0 - ADAM_B1 ** ADAM_STEP)
        v_hat = vn / (1.0 - ADAM_B2 ** ADAM_STEP)
        go_ref[...] = gv
        d_ref[...] = -ADAM_LR * (m_hat / (jnp.sqrt(v_hat) + ADAM_EPS) + ADAM_WD * w_ref[...])
        mo_ref[...] = mn
        vo_ref[...] = vn

    blk = pl.BlockSpec((tr, c), lambda i: (i, 0))
    sh = jax.ShapeDtypeStruct((r, c), F32)
    return pl.pallas_call(
        body, name=name, grid=(r // tr,), in_specs=[blk] * 4, out_specs=[blk] * 4,
        out_shape=[sh] * 4, compiler_params=_params(("parallel",), tr * c * 32, 4 * tr * c * 4),
    )(g, w, m, v)


def _pair_add(name, g, r1, c_idx):
    _, r, c = g.shape
    tr = _pick(r, max(16, (2 << 20) // (2 * c)), 16)

    def body(c_ref, g_ref, r_ref, o_ref):
        o_ref[...] = (g_ref[...].astype(F32) + r_ref[...].astype(F32)).astype(BF16)

    return pl.pallas_call(
        body, name=name,
        grid_spec=pltpu.PrefetchScalarGridSpec(
            num_scalar_prefetch=1, grid=(r // tr,),
            in_specs=[pl.BlockSpec((None, tr, c), lambda i, cr: (cr[0], i, 0)),
                      pl.BlockSpec((tr, c), lambda i, cr: (i, 0))],
            out_specs=pl.BlockSpec((tr, c), lambda i, cr: (i, 0))),
        out_shape=jax.ShapeDtypeStruct((r, c), BF16),
        compiler_params=_params(("parallel",), tr * c * 6, 3 * tr * c * 4),
    )(c_idx, g, r1)


def _adamw_halves(name, g_own, g_other, c_idx, w, m, v):
    _, r, c = w.shape
    tr = _pick(r, max(8, (5 << 18) // (4 * c)), 8)

    def body(c_ref, go_ref, gt_ref, w_ref, m_ref, v_ref, g_out, d_ref, mo_ref, vo_ref):
        gv = jnp.where(pl.program_id(0) == c_ref[0], go_ref[...], gt_ref[...])
        mn = ADAM_B1 * m_ref[...] + (1.0 - ADAM_B1) * gv
        vn = ADAM_B2 * v_ref[...] + (1.0 - ADAM_B2) * (gv * gv)
        m_hat = mn / (1.0 - ADAM_B1 ** ADAM_STEP)
        v_hat = vn / (1.0 - ADAM_B2 ** ADAM_STEP)
        g_out[...] = gv
        d_ref[...] = -ADAM_LR * (m_hat / (jnp.sqrt(v_hat) + ADAM_EPS) + ADAM_WD * w_ref[...])
        mo_ref[...] = mn
        vo_ref[...] = vn

    blk = pl.BlockSpec((None, tr, c), lambda h, i, cr: (h, i, 0))
    sh = jax.ShapeDtypeStruct(w.shape, F32)
    return pl.pallas_call(
        body, name=name,
        grid_spec=pltpu.PrefetchScalarGridSpec(
            num_scalar_prefetch=1, grid=(2, r // tr),
            in_specs=[pl.BlockSpec((tr, c), lambda h, i, cr: (jnp.where(h == cr[0], i, 0), 0)),
                      pl.BlockSpec((tr, c), lambda h, i, cr: (jnp.where(h == cr[0], 0, i), 0)),
                      blk, blk, blk],
            out_specs=[blk] * 4),
        out_shape=[sh] * 4,
        compiler_params=_params(("arbitrary", "arbitrary"), tr * c * 36, 4 * tr * c * 4),
    )(c_idx, g_own, g_other, w, m, v)


def _sum_chips(name, part, r2, chip_idx, row0=0, prev=None):
    n, r, c = r2.shape
    rc = part.shape[1]
    tr = _pick(math.gcd(rc, row0) if row0 else rc, max(16, (1 << 20) // (2 * c)), 16)
    b0 = row0 // tr

    def body(ch_ref, own_ref, *rest):
        slots, o_ref = rest[:n], rest[-1]
        acc = None
        for s in range(n):
            term = jnp.where(ch_ref[0] == s, own_ref[...], slots[s][...]).astype(F32)
            acc = term if acc is None else acc + term
        o_ref[...] = acc

    def slot_spec(s):
        return pl.BlockSpec((None, tr, c), lambda i, ch: (jnp.where(ch[0] == s, (s + 1) % n, s), b0 + i, 0))

    extra = [] if prev is None else [prev]
    return pl.pallas_call(
        body, name=name,
        grid_spec=pltpu.PrefetchScalarGridSpec(
            num_scalar_prefetch=1, grid=(rc // tr,),
            in_specs=[pl.BlockSpec((None, tr, c), lambda i, ch: (ch[0], i, 0))]
            + [slot_spec(s) for s in range(n)] + [ANY] * len(extra),
            out_specs=pl.BlockSpec((tr, c), lambda i, ch: (b0 + i, 0))),
        out_shape=jax.ShapeDtypeStruct((r, c), F32),
        input_output_aliases={2 + n: 0} if extra else {},
        compiler_params=_params(("parallel",), tr * c * (2 * n + 6), 3 * tr * c * 4),
    )(chip_idx, part, *([r2] * n), *extra)


def _sum_devices(packs):
    n, r, c = packs.shape

    def body(p_ref, o_ref):
        acc = p_ref[0]
        for k in range(1, n):
            acc = acc + p_ref[k]
        o_ref[...] = acc

    return pl.pallas_call(
        body, name="sum_devices", out_shape=jax.ShapeDtypeStruct((r, c), F32),
        in_specs=[pl.BlockSpec(memory_space=pltpu.VMEM)],
        out_specs=pl.BlockSpec(memory_space=pltpu.VMEM),
    )(packs)


def _place():
    x, y, c = lax.axis_index("x"), lax.axis_index("y"), lax.axis_index("c")
    return x, y, c


def _chip_peers(x, y):
    out = []
    for k, (fx, fy) in enumerate(((0, 1), (1, 0), (1, 1))):
        px = 1 - x if fx else x
        py = 1 - y if fy else y
        out.append((k, px, py, 2 * px + py))
    return out


def _gather_weights(big, small, relay=False, phase=None, prev=None):
    nb, ns = len(big), len(small)

    class Copies:
        def __init__(self, c_in, c_out, sems):
            b_in, s_in = c_in[:nb], c_in[nb:]
            b_out, s_out = c_out[:nb], c_out[nb:]
            ici_s, ici_r, d2d_s, d2d_r, own_s, own_r = sems[:6]
            x, y, c = _place()
            chip = 2 * x + y
            sibling = (x, y, 1 - c)
            peers = _chip_peers(x, y)
            self.own = [pltpu.make_async_remote_copy(
                src_ref=b_in[t], dst_ref=b_out[t].at[chip], send_sem=own_s.at[t], recv_sem=own_r.at[t],
                device_id=sibling, device_id_type=MESH) for t in range(nb)]
            self.ici = [pltpu.make_async_remote_copy(
                src_ref=b_in[t].at[c], dst_ref=b_out[t].at[chip, c],
                send_sem=ici_s.at[t, k], recv_sem=ici_r.at[t, k],
                device_id=(px, py, c), device_id_type=MESH)
                for t in range(nb) for (k, px, py, pchip) in peers if not (relay and k == 2)]
            self.relay = []
            if relay:
                south = c == 0
                from_chip = jnp.where(south, 2 * x + (1 - y), 2 * (1 - x) + y)
                to = (jnp.where(south, 1 - x, x), jnp.where(south, y, 1 - y), c)
                self.relay = [pltpu.make_async_remote_copy(
                    src_ref=b_out[t].at[from_chip, c], dst_ref=b_out[t].at[from_chip, c],
                    send_sem=ici_s.at[t, 2], recv_sem=ici_r.at[t, 2],
                    device_id=to, device_id_type=MESH) for t in range(nb)]
            self.landed = [pltpu.make_async_remote_copy(
                src_ref=b_out[t].at[pchip, c], dst_ref=b_out[t].at[pchip, c],
                send_sem=ici_s.at[t, k], recv_sem=ici_r.at[t, k],
                device_id=sibling, device_id_type=MESH)
                for t in range(nb) for (k, px, py, pchip) in peers]
            self.passed = [pltpu.make_async_remote_copy(
                src_ref=b_out[t].at[pchip, c], dst_ref=b_out[t].at[pchip, c],
                send_sem=d2d_s.at[t, k], recv_sem=d2d_r.at[t, k],
                device_id=sibling, device_id_type=MESH)
                for t in range(nb) for (k, px, py, pchip) in peers]
            self.from_sibling = [pltpu.make_async_remote_copy(
                src_ref=b_out[t].at[pchip, 1 - c], dst_ref=b_out[t].at[pchip, 1 - c],
                send_sem=d2d_s.at[t, k], recv_sem=d2d_r.at[t, k],
                device_id=sibling, device_id_type=MESH)
                for t in range(nb) for (k, px, py, pchip) in peers]
            self.small, self.small_landed, self.local = [], [], []
            if ns:
                sm_s, sm_r, loc = sems[6:]
                self.local = [pltpu.make_async_copy(s_in[t], s_out[t].at[chip], loc.at[t]) for t in range(ns)]
                self.small = [pltpu.make_async_remote_copy(
                    src_ref=s_in[t], dst_ref=s_out[t].at[chip],
                    send_sem=sm_s.at[t, k], recv_sem=sm_r.at[t, k],
                    device_id=(px, py, c), device_id_type=MESH)
                    for t in range(ns) for (k, px, py, pchip) in peers]
                self.small_landed = [pltpu.make_async_remote_copy(
                    src_ref=s_in[t], dst_ref=s_out[t].at[pchip],
                    send_sem=sm_s.at[t, k], recv_sem=sm_r.at[t, k],
                    device_id=sibling, device_id_type=MESH)
                    for t in range(ns) for (k, px, py, pchip) in peers]

    assert phase is None or (relay and not small)
    near = [3 * t + k for t in range(nb) for k in range(2)]
    diag = [3 * t + 2 for t in range(nb)]

    def start(c_in, c_out, sems):
        cps = Copies(c_in, c_out, sems)
        if phase == 2:
            for cp in cps.relay:
                cp.start()
        else:
            for cp in cps.local + cps.own + cps.ici + cps.small:
                cp.start()

    def finish(c_in, c_out, sems):
        cps = Copies(c_in, c_out, sems)
        legs = near + diag if phase is None else (near if phase == 1 else diag)
        for n in sorted(legs):
            if relay and phase is None and n % 3 == 2:
                cps.relay[n // 3].start()
            cps.landed[n].wait_recv()
            cps.passed[n].start()
        for cp in (cps.small_landed if phase != 2 else []) + [cps.from_sibling[n] for n in legs]:
            cp.wait_recv()
        sends = [cps.passed[n] for n in legs]
        if phase != 2:
            sends += cps.ici + cps.small
        if phase != 1:
            sends += cps.relay
        for cp in sends:
            cp.wait_send()
        if phase != 2:
            for cp in cps.own + cps.local:
                cp.wait()

    out_shape = [jax.ShapeDtypeStruct((N_CHIPS,) + b.shape, b.dtype) for b in big]
    out_shape += [jax.ShapeDtypeStruct((N_CHIPS,) + s.shape, s.dtype) for s in small]
    dma = pltpu.SemaphoreType.DMA
    scratch = [dma((nb, 3)), dma((nb, 3)), dma((nb, 3)), dma((nb, 3)), dma((nb,)), dma((nb,))]
    if ns:
        scratch += [dma((ns, 3)), dma((ns, 3)), dma((ns,))]
    operands = list(big) + list(small) + (list(prev) if prev else [])
    ex = _Exchange(operands, out_shape, scratch, start, finish,
                   aliases={nb + t: t for t in range(nb)} if prev else None)
    ex.copies = Copies
    return ex


def _join(exs):
    assert not any(e.aliases for e in exs)

    def parts(refs, counts):
        out, pos = [], 0
        for n in counts:
            out.append(refs[pos:pos + n])
            pos += n
        return out

    def run(which):
        def go(c_in, c_out, sems):
            ins = parts(c_in, [len(e.operands) for e in exs])
            outs = parts(c_out, [len(e.out_shape) for e in exs])
            ss = parts(sems, [len(e.scratch) for e in exs])
            for e, a, b, c in zip(exs, ins, outs, ss):
                getattr(e, which)(a, b, c)
        return go

    return _Exchange(sum((list(e.operands) for e in exs), []), sum((list(e.out_shape) for e in exs), []),
                     sum((list(e.scratch) for e in exs), []), run("start"), run("finish"))


class _Exchange:
    def __init__(self, operands, out_shape, scratch, start, finish, aliases=None):
        self.operands, self.out_shape, self.scratch = operands, out_shape, scratch
        self.start, self.finish, self.aliases = start, finish, dict(aliases or {})


def _run_exchange(name, ex):
    n_i, n_o = len(ex.operands), len(ex.out_shape)

    def body(*refs):
        ins, outs, sems = refs[:n_i], refs[n_i:n_i + n_o], refs[n_i + n_o:]
        ex.start(ins, outs, sems)
        ex.finish(ins, outs, sems)

    return list(pl.pallas_call(
        body, name=name, out_shape=list(ex.out_shape), in_specs=[ANY] * n_i, out_specs=[ANY] * n_o,
        scratch_shapes=list(ex.scratch), input_output_aliases=ex.aliases,
    )(*ex.operands))


def _to_sibling(arrays, other_half):
    n = len(arrays)

    def copies(a_in, a_out, sems):
        send, recv = sems
        x, y, c = _place()
        return [pltpu.make_async_remote_copy(
            src_ref=a_in[t].at[1 - c] if other_half else a_in[t], dst_ref=a_out[t],
            send_sem=send.at[t], recv_sem=recv.at[t],
            device_id=(x, y, 1 - c), device_id_type=MESH) for t in range(n)]

    def start(a_in, a_out, sems):
        for cp in copies(a_in, a_out, sems):
            cp.start()

    def finish(a_in, a_out, sems):
        for cp in copies(a_in, a_out, sems):
            cp.wait()

    dma = pltpu.SemaphoreType.DMA
    shapes = [jax.ShapeDtypeStruct(a.shape[1:] if other_half else a.shape, a.dtype) for a in arrays]
    return _Exchange(list(arrays), shapes, [dma((n,)), dma((n,))], start, finish)


def _scatter_partials(parts, rows=None, row0=0, prev=None):
    n = len(parts)
    land = [(p.shape[0], p.shape[1] if rows is None else rows, p.shape[2]) for p in parts]

    def window(ref, slot, t):
        return ref.at[slot, pl.ds(row0, parts[t].shape[1])]

    def sends(p_in, r_out, sems):
        send, recv = sems
        x, y, c = _place()
        chip = 2 * x + y
        return [pltpu.make_async_remote_copy(
            src_ref=p_in[t].at[pchip], dst_ref=window(r_out[t], chip, t),
            send_sem=send.at[t, k], recv_sem=recv.at[t, k],
            device_id=(px, py, c), device_id_type=MESH)
            for t in range(n) for (k, px, py, pchip) in _chip_peers(x, y)]

    def start(p_in, r_out, sems):
        for cp in sends(p_in, r_out, sems):
            cp.start()

    def finish(p_in, r_out, sems):
        send, recv = sems
        x, y, c = _place()
        for t in range(n):
            for (k, px, py, pchip) in _chip_peers(x, y):
                pltpu.make_async_remote_copy(
                    src_ref=p_in[t].at[pchip], dst_ref=window(r_out[t], pchip, t),
                    send_sem=send.at[t, k], recv_sem=recv.at[t, k],
                    device_id=(px, py, c), device_id_type=MESH).wait_recv()
        for cp in sends(p_in, r_out, sems):
            cp.wait_send()

    dma = pltpu.SemaphoreType.DMA
    operands = list(parts) + (list(prev) if prev else [])
    return _Exchange(operands, [jax.ShapeDtypeStruct(sh, p.dtype) for sh, p in zip(land, parts)],
                     [dma((n, 3)), dma((n, 3))], start, finish,
                     aliases={n + t: t for t in range(n)} if prev else None)


def _w_in_grad_sibling(h, dpa, dgl, c_idx, carried, *, tm, tn, rows, pq):
    s, d = h.shape
    wa, wg = dpa.shape[1], dgl.shape[2]
    p = wa + 2 * wg
    ni, nj = rows // tm, p // tn
    na, qg, qp = wa // tn, wg // tn, pq // tn
    n_steps = ni * nj
    n_ci, n_co = len(carried.operands), len(carried.out_shape)

    def body(c_ref, h_ref, a_ref, g_ref, *rest):
        c_in = rest[:n_ci]
        r1_ref = rest[n_ci]
        c_out = rest[n_ci + 1:n_ci + 1 + n_co]
        slots, send, recv = rest[n_ci + 1 + n_co:n_ci + 4 + n_co]
        sems = rest[n_ci + 4 + n_co:]
        i, j = pl.program_id(0), pl.program_id(1)
        step = i * nj + j
        slot = lax.rem(step, 2)
        x, y, c = _place()
        sibling = (x, y, 1 - c)

        @pl.when(step == 0)
        def _():
            carried.start(c_in, c_out, sems)

        def tile_copy(sl):
            return pltpu.make_async_remote_copy(
                src_ref=slots.at[sl],
                dst_ref=r1_ref.at[j // qp, pl.ds(i * tm, tm), pl.ds((j % qp) * tn, tn)],
                send_sem=send.at[sl], recv_sem=recv, device_id=sibling, device_id_type=MESH)

        @pl.when(step >= 2)
        def _():
            tile_copy(slot).wait_send()

        def emit(b_ref):
            acc = lax.dot_general(h_ref[...], b_ref[...], TN, preferred_element_type=F32)
            slots[slot] = acc.astype(BF16)

        pl.when(j < na)(lambda: emit(a_ref))
        pl.when(j >= na)(lambda: emit(g_ref))
        tile_copy(slot).start()

        @pl.when(step == n_steps - 1)
        def _():
            tile_copy(0).wait_send()
            tile_copy(1).wait_send()
            pltpu.make_async_remote_copy(
                src_ref=r1_ref, dst_ref=r1_ref, send_sem=send.at[0], recv_sem=recv,
                device_id=sibling, device_id_type=MESH).wait_recv()
            carried.finish(c_in, c_out, sems)

    nbh = rows // tm
    dma = pltpu.SemaphoreType.DMA
    blk = s * tm * 2 + 2 * s * tn * 2
    res = pl.pallas_call(
        body, name="w_in_grad_sibling",
        grid_spec=pltpu.PrefetchScalarGridSpec(
            num_scalar_prefetch=1, grid=(ni, nj),
            in_specs=[pl.BlockSpec((s, tm), lambda i, j, cr: (0, (1 - cr[0]) * nbh + i)),
                      pl.BlockSpec((s, tn), lambda i, j, cr: (0, jnp.minimum(j, na - 1))),
                      pl.BlockSpec((None, s, tn),
                                   lambda i, j, cr: (jnp.maximum(j - na, 0) // qg, 0, jnp.maximum(j - na, 0) % qg))]
            + [ANY] * n_ci,
            out_specs=[ANY] * (1 + n_co),
            scratch_shapes=[pltpu.VMEM((2, tm, tn), BF16), dma((2,)), dma] + list(carried.scratch)),
        out_shape=[jax.ShapeDtypeStruct((N_CHIPS, rows, pq), BF16)] + list(carried.out_shape),
        compiler_params=_params(("arbitrary", "arbitrary"), blk, 2 * tm * tn * 2 + 2 * tm * tn * 4),
    )(c_idx, h, dpa, dgl, *carried.operands)
    return res[0], res[1:]


def _proj_gathering(h, wb, chip_idx, gather_rest, *, tm, tn):
    s, d = h.shape
    _, hh, pq = wb.shape
    qp, ni = pq // tn, s // tm
    n_tiles = N_CHIPS * qp
    n_ci, n_co, n_cs = len(gather_rest.operands), len(gather_rest.out_shape), len(gather_rest.scratch)

    def where_of(t):
        near = t - qp
        g = jnp.where(t < qp, 0, jnp.where(t < 3 * qp, 1 + lax.rem(near, 2), 3))
        jj = jnp.where(t < qp, t, jnp.where(t < 3 * qp, near // 2, t - 3 * qp))
        return g, jj

    def body(ch_ref, h_ref, own_ref, *rest):
        c_in = rest[:n_ci]
        o_ref, gathered = rest[n_ci], rest[n_ci + 1]
        c_out = rest[n_ci + 2:n_ci + 2 + n_co]
        slots, fetch_sem, ici_s, ici_r, d2d_s, d2d_r, own_s, own_r = rest[n_ci + 2 + n_co:n_ci + 10 + n_co]
        c_sem = rest[n_ci + 10 + n_co:]
        tile, i = pl.program_id(0), pl.program_id(1)
        x, y, c = _place()
        chip = 2 * x + y
        sibling = (x, y, 1 - c)

        def cols(jj):
            return pl.ds(jj * tn, tn)

        def remote(src, dst, send, recv, to):
            return pltpu.make_async_remote_copy(src_ref=src, dst_ref=dst, send_sem=send, recv_sem=recv,
                                                device_id=to, device_id_type=MESH)

        def own_shard():
            return remote(own_ref, gathered.at[chip], own_s, own_r, sibling)

        def to_neighbour(k, jj):
            to = (x, 1 - y, c) if k == 0 else (1 - x, y, c)
            return remote(own_ref.at[c, :, cols(jj)], gathered.at[chip, c, :, cols(jj)],
                          ici_s.at[k, jj], ici_r.at[k, jj], to)

        def relay(jj):
            south = c == 0
            from_chip = jnp.where(south, 2 * x + (1 - y), 2 * (1 - x) + y)
            to = (jnp.where(south, 1 - x, x), jnp.where(south, y, 1 - y), c)
            chunk = gathered.at[from_chip, c, :, cols(jj)]
            return remote(chunk, chunk, ici_s.at[2, jj], ici_r.at[2, jj], to)

        def chunk_of(k, jj, half):
            return gathered.at[jnp.bitwise_xor(chip, k + 1), half, :, cols(jj)]

        def landed(k, jj):
            return remote(chunk_of(k, jj, c), chunk_of(k, jj, c), ici_s.at[k, jj], ici_r.at[k, jj], sibling)

        def passed(k, jj):
            return remote(chunk_of(k, jj, c), chunk_of(k, jj, c), d2d_s.at[k, jj], d2d_r.at[k, jj], sibling)

        def from_sibling(k, jj):
            return remote(chunk_of(k, jj, 1 - c), chunk_of(k, jj, 1 - c), d2d_s.at[k, jj], d2d_r.at[k, jj], sibling)

        def fetch(src, slot):
            return pltpu.make_async_copy(src, slots.at[slot], fetch_sem.at[slot])

        @pl.when(jnp.logical_and(tile == 0, i == 0))
        def _():
            own_shard().start()
            for jj in range(qp):
                for k in range(2):
                    to_neighbour(k, jj).start()
            fetch(own_ref.at[:, :, cols(0)], 0).start()

        @pl.when(i == 0)
        def _():
            fetch(own_ref.at[:, :, cols(0)], lax.rem(tile, 2)).wait()

        last_row = i == ni - 1
        nxt = tile + 1
        g1, j1 = where_of(nxt)
        pass_row = i == max(ni - 2, 0)

        @pl.when(jnp.logical_and(pass_row, jnp.logical_and(nxt >= qp, nxt < n_tiles)))
        def _():
            landed(g1 - 1, j1).wait_recv()
            passed(g1 - 1, j1).start()

        @pl.when(jnp.logical_and(pass_row, nxt == 3 * qp - 1))
        def _():
            for ch in range(qp):
                relay(ch).start()
            gather_rest.start(c_in, c_out, c_sem)

        @pl.when(jnp.logical_and(last_row, nxt < qp))
        def _():
            fetch(own_ref.at[:, :, cols(nxt)], lax.rem(nxt, 2)).start()

        @pl.when(jnp.logical_and(last_row, jnp.logical_and(nxt >= qp, nxt < n_tiles)))
        def _():
            from_sibling(g1 - 1, j1).wait_recv()
            fetch(gathered.at[jnp.bitwise_xor(chip, g1), :, :, cols(j1)], lax.rem(nxt, 2)).start()

        w_tile = slots[lax.rem(tile, 2)].reshape(d, tn)
        o_ref[...] = jnp.dot(h_ref[...], w_tile, preferred_element_type=F32).astype(BF16)

        @pl.when(jnp.logical_and(tile == n_tiles - 1, last_row))
        def _():
            for jj in range(qp):
                for k in range(2):
                    to_neighbour(k, jj).wait_send()
                relay(jj).wait_send()
                for k in range(3):
                    passed(k, jj).wait_send()
            own_shard().wait()
            gather_rest.finish(c_in, c_out, c_sem)

    assert qp >= 2 and not gather_rest.aliases
    dma = pltpu.SemaphoreType.DMA
    p = N_CHIPS * pq

    def out_cols(t, i, ch):
        g, jj = where_of(t)
        return (i, jnp.bitwise_xor(ch[0], g) * qp + jj)

    res = pl.pallas_call(
        body, name="proj_gathering",
        grid_spec=pltpu.PrefetchScalarGridSpec(
            num_scalar_prefetch=1, grid=(n_tiles, ni),
            in_specs=[pl.BlockSpec((tm, d), lambda t, i, ch: (i, 0)), ANY] + [ANY] * n_ci,
            out_specs=[pl.BlockSpec((tm, tn), out_cols), ANY] + [ANY] * n_co,
            scratch_shapes=[pltpu.VMEM((2, 2, hh, tn), BF16), dma((2,)),
                            dma((3, qp)), dma((3, qp)), dma((3, qp)), dma((3, qp)), dma, dma]
            + list(gather_rest.scratch)),
        out_shape=[jax.ShapeDtypeStruct((s, p), BF16), jax.ShapeDtypeStruct((N_CHIPS,) + wb.shape, wb.dtype)]
        + list(gather_rest.out_shape),
        compiler_params=_params(("arbitrary",) * 2, tm * d * 2 + tm * tn * 2, 3 * d * tn * 2 + tm * tn * 4),
    )(chip_idx, h, wb, *gather_rest.operands)
    return res[0], res[1], res[2:]


def _gather_packs(pack):
    def body(p_ref, o_ref, send, recv, loc):
        x, y, c = _place()
        me = 4 * x + 2 * y + c
        mine = pltpu.make_async_copy(p_ref, o_ref.at[me], loc)
        mine.start()
        flips = [(fx, fy, fc) for fx in (0, 1) for fy in (0, 1) for fc in (0, 1)][1:]
        cps = []
        for k, (fx, fy, fc) in enumerate(flips):
            peer = (1 - x if fx else x, 1 - y if fy else y, 1 - c if fc else c)
            cps.append(pltpu.make_async_remote_copy(
                src_ref=p_ref, dst_ref=o_ref.at[me], send_sem=send.at[k], recv_sem=recv.at[k],
                device_id=peer, device_id_type=MESH))
        for cp in cps:
            cp.start()
        for k, (fx, fy, fc) in enumerate(flips):
            px, py, pc = (1 - x if fx else x, 1 - y if fy else y, 1 - c if fc else c)
            pltpu.make_async_remote_copy(
                src_ref=p_ref, dst_ref=o_ref.at[4 * px + 2 * py + pc],
                send_sem=send.at[k], recv_sem=recv.at[k],
                device_id=(px, py, pc), device_id_type=MESH).wait_recv()
        for cp in cps:
            cp.wait_send()
        mine.wait()

    dma = pltpu.SemaphoreType.DMA
    return pl.pallas_call(
        body, name="gather_packs", out_shape=jax.ShapeDtypeStruct((N_DEV,) + pack.shape, pack.dtype),
        in_specs=[ANY], out_specs=ANY, scratch_shapes=[dma((7,)), dma((7,)), dma],
    )(pack)


def _flat_pack(pieces):
    flat = jnp.concatenate([p.reshape(-1) for p in pieces])
    pad = (-flat.shape[0]) % (8 * LANES)
    flat = jnp.concatenate([flat, jnp.zeros((pad,), F32)])
    return flat.reshape(-1, LANES)


def _unpack(pack, shapes):
    flat = pack.reshape(-1)
    out, off = [], 0
    for sh in shapes:
        n = 1
        for s in sh:
            n *= s
        out.append(flat[off:off + n].reshape(sh))
        off += n
    return out


def kernel(x, norm_w, w_in, pool_w, pool_scale, conv_w, conv_b, gate_b, w_branch, w_out, final_norm_w, loss_target, m_norm_w, m_w_in, m_pool_w, m_pool_scale, m_conv_w, m_conv_b, m_gate_b, m_w_branch, m_w_out, m_final_norm_w, v_norm_w, v_w_in, v_pool_w, v_pool_scale, v_conv_w, v_conv_b, v_gate_b, v_w_branch, v_w_out, v_final_norm_w):
    _, s, d = x.shape
    w = d // 2
    cg = w // N_POOL_GROUPS
    p = 6 * w + 2 * d
    pq = p // N_CHIPS
    dq = d // N_CHIPS
    assert w_in.shape == (1, d, pq) and w_branch.shape == (1, 2, w, dq) and w_out.shape == (1, dq, d)
    assert pool_w.shape == (1, N_POOL_GROUPS, cg // N_CHIPS, cg) and conv_w.shape == (1, CONV_K, cg)

    x2d = x.reshape(s, d)
    tgt = loss_target.reshape(s, d)
    c_idx = lax.axis_index("c").astype(jnp.int32).reshape(1)
    chip = 2 * lax.axis_index("x") + lax.axis_index("y")

    big_w = [w_in.reshape(d, pq), w_out.reshape(dq, d), w_branch.reshape(2 * w, dq),
             pool_w.reshape(cg, cg)]
    names = ["in", "out", "branch", "pool"]
    big_b = [_cast_bf16("cast_" + nm, a) for nm, a in zip(names, big_w)]
    big_b = [b.reshape(2, b.shape[0] // 2, b.shape[1]) for b in big_b]
    chip_idx = chip.astype(jnp.int32).reshape(1)

    h = _rms_fwd(x2d, norm_w)

    tm = _pick(s, 1024, 16)
    tn_p = _pick(math.gcd(pq, 6 * w, d), 1024, LANES)
    qp = pq // tn_p
    tk_d = _pick(d, 4096, LANES)
    proj, wg_in, (wg_pool, wg_cw, wg_gb, wg_br_near) = _proj_gathering(
        h, big_b[0], chip_idx,
        _join([_gather_weights(big_b[3:], [conv_w.reshape(CONV_K, cg), gate_b.reshape(2, dq)]),
               _gather_weights(big_b[2:3], [], relay=True, phase=1)]),
        tm=tm, tn=tn_p)
    wg_in = wg_in.reshape(N_CHIPS, d, pq)
    wg_pool = wg_pool.reshape(N_CHIPS, N_POOL_GROUPS, cg // N_CHIPS, cg)

    ys, (wg_br,) = _mix_fwd(proj, wg_pool, pool_scale, wg_cw, conv_b, w,
                            _gather_weights(big_b[2:3], [], relay=True, phase=2, prev=[wg_br_near]))

    tn_d = _pick(dq, 1024, LANES)
    qd = dq // tn_d
    tk_w = _pick(w, 2048, LANES)
    tm_g = _pick(s, 512, 16)
    gl0 = (6 * w) // tn_d
    gl1 = (6 * w + d) // tn_d

    def gate_specs(im, rows):
        return [pl.BlockSpec((rows, tn_d), lambda *a: (im(*a)[0], gl0 + im(*a)[1])),
                pl.BlockSpec((rows, tn_d), lambda *a: (im(*a)[0], gl1 + im(*a)[1])),
                pl.BlockSpec((None, 2, tn_d), lambda *a: (im(*a)[1] // qd, 0, im(*a)[1] % qd))]

    def merge_epilogue(accs, ins, outs, ids):
        gb = ins[6][...]
        g0 = _sigmoid(ins[4][...].astype(F32) + gb[0:1])
        g1 = _sigmoid(ins[5][...].astype(F32) + gb[1:2])
        outs[0][...] = (g0 * accs[0] + g1 * accs[1]).astype(BF16)
        outs[1][0] = accs[0].astype(BF16)
        outs[1][1] = accs[1].astype(BF16)

    (merged, br), (wg_out,) = _mm(
        "branch_merge", grid=(s // tm_g, d // tn_d, w // tk_w), comm=_gather_weights(big_b[1:2], []),
        operands=[ys, wg_br, ys, wg_br, proj, proj, wg_gb],
        in_specs=[pl.BlockSpec((None, tm_g, tk_w), lambda i, j, k: (0, i, k)),
                  pl.BlockSpec((None, None, tk_w, tn_d), lambda i, j, k: (j // qd, 0, k, j % qd)),
                  pl.BlockSpec((None, tm_g, tk_w), lambda i, j, k: (1, i, k)),
                  pl.BlockSpec((None, None, tk_w, tn_d), lambda i, j, k: (j // qd, 1, k, j % qd)),
                  *gate_specs(lambda i, j, k: (i, j), tm_g)],
        out_shape=[jax.ShapeDtypeStruct((s, d), BF16), jax.ShapeDtypeStruct((2, s, d), BF16)],
        out_specs=[pl.BlockSpec((tm_g, tn_d), lambda i, j, k: (i, j)),
                   pl.BlockSpec((2, tm_g, tn_d), lambda i, j, k: (0, i, j))],
        pairs=[(0, 1, 0, None), (2, 3, 1, None)], dims=NN, acc_shapes=[(tm_g, tn_d)] * 2,
        epilogue=merge_epilogue, temp_bytes=6 * tm_g * tn_d * 4,
    )
    wg_out = wg_out.reshape(d, d)

    tn_f = _pick(d, 1024, LANES)
    o = _mm(
        "out_proj", grid=(s // tm, d // tn_f, d // tk_d), operands=[merged, wg_out],
        in_specs=[pl.BlockSpec((tm, tk_d), lambda i, j, k: (i, k)),
                  pl.BlockSpec((tk_d, tn_f), lambda i, j, k: (k, j))],
        out_shape=[jax.ShapeDtypeStruct((s, d), F32)],
        out_specs=[pl.BlockSpec((tm, tn_f), lambda i, j, k: (i, j))],
        pairs=[(0, 1, 0, None)], dims=NN, acc_shapes=[(tm, tn_f)],
        epilogue=lambda accs, ins, outs, ids: outs[0].__setitem__(Ellipsis, accs[0]),
    )[0][0]

    dx2, dx2b, loss_part, g_fnw =_head(x2d, o, tgt, final_norm_w.reshape(1, d))

    def gate_bwd_epilogue(accs, ins, outs, ids):
        dm = accs[0]
        gb = ins[5][...]
        i = ids[1]

        @pl.when(i == 0)
        def _():
            outs[2][...] = jnp.zeros(outs[2].shape, F32)

        for n in range(2):
            gate = _sigmoid(ins[3 + n][...].astype(F32) + gb[n:n + 1])
            outs[0][n] = (dm * gate).astype(BF16)
            dgl = dm * ins[2][n].astype(F32) * gate * (1.0 - gate)
            outs[1][n] = dgl.astype(BF16)
            outs[2][n:n + 1, :] += jnp.sum(dgl, axis=0, keepdims=True)

    tm_b = _pick(s, 512, 16)
    (d_br, dgl, g_gb), _ = _mm(
        "out_proj_bwd_gate", grid=(d // tn_d, s // tm_b, d // tk_d),
        operands=[dx2b, wg_out, br, proj, proj, wg_gb],
        in_specs=[pl.BlockSpec((tm_b, tk_d), lambda j, i, k: (i, k)),
                  pl.BlockSpec((tn_d, tk_d), lambda j, i, k: (j, k)),
                  pl.BlockSpec((2, tm_b, tn_d), lambda j, i, k: (0, i, j)),
                  *gate_specs(lambda j, i, k: (i, j), tm_b)],
        out_shape=[jax.ShapeDtypeStruct((2, s, d), BF16), jax.ShapeDtypeStruct((2, s, d), BF16),
                   jax.ShapeDtypeStruct((2, d), F32)],
        out_specs=[pl.BlockSpec((2, tm_b, tn_d), lambda j, i, k: (0, i, j)),
                   pl.BlockSpec((2, tm_b, tn_d), lambda j, i, k: (0, i, j)),
                   pl.BlockSpec((2, tn_d), lambda j, i, k: (0, j))],
        pairs=[(0, 1, 0, None)], dims=NT, acc_shapes=[(tm_b, tn_d)],
        epilogue=gate_bwd_epilogue, semantics=("parallel", "arbitrary", "arbitrary"),
        temp_bytes=8 * tm_b * tn_d * 4,
    )

    hh_out = d // 8
    tm_o = _pick(hh_out, 512, LANES)
    nb_o = hh_out // tm_o
    tk_s = _pick(s, 4096, LANES)
    g_out = _mm(
        "w_out_grad", grid=(d // tm_o, d // tn_f, s // tk_s), operands=[merged, dx2b],
        in_specs=[pl.BlockSpec((tk_s, tm_o), lambda i, j, k: (k, i)),
                  pl.BlockSpec((tk_s, tn_f), lambda i, j, k: (k, j))],
        out_shape=[jax.ShapeDtypeStruct((2, N_CHIPS, hh_out, d), BF16)],
        out_specs=[pl.BlockSpec((None, None, tm_o, tn_f),
                                lambda i, j, k: ((i // nb_o) % 2, i // (2 * nb_o), i % nb_o, j))],
        pairs=[(0, 1, 0, None)], dims=TN, acc_shapes=[(tm_o, tn_f)],
        epilogue=lambda accs, ins, outs, ids: outs[0].__setitem__(Ellipsis, accs[0].astype(BF16)),
    )[0][0]

    tn_w = _pick(w, 2048, LANES)
    tk_q = _pick(dq, 1024, LANES)
    qk = dq // tk_q
    (dys,), from_sibling_out = _mm(
        "branch_bwd", grid=(2, s // tm, w // tn_w, d // tk_q), operands=[d_br, wg_br],
        comm=_to_sibling([g_out], other_half=True),
        in_specs=[pl.BlockSpec((None, tm, tk_q), lambda n, i, j, k: (n, i, k)),
                  pl.BlockSpec((None, None, tn_w, tk_q), lambda n, i, j, k: (k // qk, n, j, k % qk))],
        out_shape=[jax.ShapeDtypeStruct((2, s, w), F32)],
        out_specs=[pl.BlockSpec((None, tm, tn_w), lambda n, i, j, k: (n, i, j))],
        pairs=[(0, 1, 0, None)], dims=NT, acc_shapes=[(tm, tn_w)],
        epilogue=lambda accs, ins, outs, ids: outs[0].__setitem__(Ellipsis, accs[0]),
    )

    tm_w = _pick(w, 1024, LANES)
    g_br = _mm(
        "w_branch_grad", grid=(2, w // tm_w, d // tn_d, s // tk_s), operands=[ys, d_br],
        in_specs=[pl.BlockSpec((None, tk_s, tm_w), lambda n, i, j, k: (n, k, i)),
                  pl.BlockSpec((None, tk_s, tn_d), lambda n, i, j, k: (n, k, j))],
        out_shape=[jax.ShapeDtypeStruct((2, N_CHIPS, w, dq), BF16)],
        out_specs=[pl.BlockSpec((None, None, tm_w, tn_d), lambda n, i, j, k: (n, j // qd, i, j % qd))],
        pairs=[(0, 1, 0, None)], dims=TN, acc_shapes=[(tm_w, tn_d)],
        epilogue=lambda accs, ins, outs, ids: outs[0].__setitem__(Ellipsis, accs[0].astype(BF16)),
    )[0][0]

    dpa, g_pool, g_ps, g_cw, g_cb = _mix_bwd(proj, dys, wg_pool, pool_scale, wg_cw, conv_b, w)
    g_pool = g_pool.astype(BF16).reshape(2, N_CHIPS, 2 * (cg // N_CHIPS), cg)

    early = [g_out, g_br, g_pool]
    from_sibling_early = list(from_sibling_out) + _run_exchange(
        "pair_exchange", _to_sibling(early[1:], other_half=True))
    parts_early = []
    for nm, g, r1 in zip(names[1:], early, from_sibling_early):
        _, nc, hh, wd = g.shape
        pt = _pair_add("pair_add_" + nm, g.reshape(2, nc * hh, wd), r1.reshape(nc * hh, wd), c_idx)
        parts_early.append(pt.reshape(nc, hh, wd))

    na = (6 * w) // tn_p
    qg = d // tn_p
    hh_in = d // 2
    assert s == tk_s
    tm_i = _pick(hh_in, 1024, LANES)
    tn_i = _pick(tn_p, 512, LANES)
    from_sibling, recv_early = _w_in_grad_sibling(
        h, dpa, dgl, c_idx, _scatter_partials(parts_early), tm=tm_i, tn=tn_i, rows=hh_in, pq=pq)

    tm_o2 = _pick(hh_in, 512, LANES)
    nb_o2 = hh_in // tm_o2

    def own_rows(name, first, count, comm):
        def add_sibling(accs, ins, outs, ids):
            outs[0][...] = (accs[0] + ins[3][...].astype(F32)).astype(BF16)

        return _mm(
            name, grid=(p // tn_p, count, 1), operands=[h, dpa, dgl, from_sibling], comm=comm, prefetch=c_idx,
            in_specs=[pl.BlockSpec((tk_s, tm_o2), lambda j, i, k, cr: (0, cr[0] * nb_o2 + first + i)),
                      pl.BlockSpec((tk_s, tn_p), lambda j, i, k, cr: (0, jnp.minimum(j, na - 1))),
                      pl.BlockSpec((None, tk_s, tn_p),
                                   lambda j, i, k, cr: (jnp.maximum(j - na, 0) // qg, 0, jnp.maximum(j - na, 0) % qg)),
                      pl.BlockSpec((None, tm_o2, tn_p), lambda j, i, k, cr: (j // qp, first + i, j % qp))],
            out_shape=[jax.ShapeDtypeStruct((N_CHIPS, count * tm_o2, pq), BF16)],
            out_specs=[pl.BlockSpec((None, tm_o2, tn_p), lambda j, i, k, cr: (j // qp, i, j % qp))],
            pairs=[(0, 1, 0, lambda ids: ids[0] < na), (0, 2, 0, lambda ids: ids[0] >= na)],
            dims=TN, acc_shapes=[(tm_o2, tn_p)], epilogue=add_sibling)

    halves_early = [_sum_chips("sum_chips_" + nm, pt, r2, chip_idx)
                    for nm, pt, r2 in zip(names[1:], parts_early, recv_early)]
    share_early = _to_sibling(halves_early, other_half=False)

    n_first = 1 if nb_o2 > 1 else 0
    if n_first:
        (part_a,), others_early = own_rows("w_in_grad_own_a", 0, n_first, share_early)
        scatter_a = _scatter_partials([part_a], rows=hh_in)
        (part_b,), land_a = own_rows("w_in_grad_own_b", n_first, nb_o2 - n_first, scatter_a)
        scatter_b = _scatter_partials([part_b], rows=hh_in, row0=n_first * tm_o2, prev=land_a)
    else:
        (part_b,), others_early = own_rows("w_in_grad_own", 0, nb_o2, share_early)
        scatter_b = _scatter_partials([part_b])

    tn_h = _pick(d, 2048, LANES)
    (dh,), recv_in = _mm(
        "proj_bwd", grid=(s // tm, d // tn_h, p // tn_p), operands=[dpa, dgl, wg_in],
        comm=scatter_b,
        in_specs=[pl.BlockSpec((tm, tn_p), lambda i, j, k: (i, jnp.minimum(k, na - 1))),
                  pl.BlockSpec((None, tm, tn_p),
                               lambda i, j, k: (jnp.maximum(k - na, 0) // qg, i, jnp.maximum(k - na, 0) % qg)),
                  pl.BlockSpec((None, tn_h, tn_p), lambda i, j, k: (k // qp, j, k % qp))],
        out_shape=[jax.ShapeDtypeStruct((s, d), F32)],
        out_specs=[pl.BlockSpec((tm, tn_h), lambda i, j, k: (i, j))],
        pairs=[(0, 2, 0, lambda ids: ids[2] < na), (1, 2, 0, lambda ids: ids[2] >= na)],
        dims=NT, acc_shapes=[(tm, tn_h)],
        epilogue=lambda accs, ins, outs, ids: outs[0].__setitem__(Ellipsis, accs[0]),
    )

    grad_x, g_nw = _rms_bwd(x2d, dh, dx2, norm_w)

    if n_first:
        half_in = _sum_chips("sum_chips_in_a", part_a, recv_in[0], chip_idx)
        half_in = _sum_chips("sum_chips_in_b", part_b, recv_in[0], chip_idx, row0=n_first * tm_o2, prev=half_in)
    else:
        half_in = _sum_chips("sum_chips_in", part_b, recv_in[0], chip_idx)
    halves = [half_in] + halves_early
    others = _run_exchange("pair_share", _to_sibling([half_in], other_half=False)) + list(others_early)

    g_cw_full = jnp.transpose(g_cw, (1, 0, 2)).reshape(CONV_K, w)
    small_shapes = [(LANES,), (1, d), (1, d), (1, w), (1, w), (CONV_K, w), (2, d)]
    pack = _flat_pack([loss_part[0], g_nw, g_fnw, g_ps, g_cb, g_cw_full, g_gb])
    total = _sum_devices(_gather_packs(pack))
    t_loss, t_nw, t_fnw, t_ps, t_cb, t_cw, t_gb = _unpack(total, small_shapes)
    loss = t_loss[0]
    t_cw = lax.dynamic_slice_in_dim(t_cw, chip * cg, cg, axis=1)
    t_gb = lax.dynamic_slice_in_dim(t_gb, chip * dq, dq, axis=1)

    out = {}
    big_names = ["w_in", "w_out", "w_branch", "pool_w"]
    big_m = [m_w_in, m_w_out, m_w_branch, m_pool_w]
    big_v = [v_w_in, v_w_out, v_w_branch, v_pool_w]
    big_orig = [w_in, w_out, w_branch, pool_w]
    for nm, g_own, g_other, w2, mm_, vv_, orig in zip(big_names, halves, others, big_w, big_m, big_v, big_orig):
        sh = (2,) + g_own.shape
        res = _adamw_halves("adamw_" + nm, g_own, g_other, c_idx,
                            w2.reshape(sh), mm_.reshape(sh), vv_.reshape(sh))
        out[nm] = [r.reshape(orig.shape) for r in res]

    sm_names = ["norm_w", "final_norm_w", "pool_scale", "conv_b", "conv_w", "gate_b"]
    sm_g = [t_nw, t_fnw, t_ps, t_cb, t_cw, t_gb]
    sm_w = [norm_w, final_norm_w, pool_scale, conv_b, conv_w, gate_b]
    sm_m = [m_norm_w, m_final_norm_w, m_pool_scale, m_conv_b, m_conv_w, m_gate_b]
    sm_v = [v_norm_w, v_final_norm_w, v_pool_scale, v_conv_b, v_conv_w, v_gate_b]
    sm_shapes = [a.shape for a in sm_w]
    res = _adamw("adamw_small", _flat_pack(sm_g), _flat_pack(sm_w), _flat_pack(sm_m), _flat_pack(sm_v))
    res = [_unpack(r, sm_shapes) for r in res]
    for idx, nm in enumerate(sm_names):
        out[nm] = [r[idx] for r in res]

    order = ["norm_w", "w_in", "pool_w", "pool_scale", "conv_w", "conv_b", "gate_b", "w_branch", "w_out",
             "final_norm_w"]
    outs = [loss, grad_x.reshape(x.shape)]
    for kind in range(4):
        outs += [out[nm][kind] for nm in order]
    return tuple(outs)
```

```python
import math

import jax
import jax.numpy as jnp
from jax import lax
from jax.experimental import pallas as pl
from jax.experimental.pallas import tpu as pltpu

F32 = jnp.float32
BF16 = jnp.bfloat16

NORM_EPS = 1e-6
POOL_WINDOWS = (2, 4, 8, 16)
N_POOL_GROUPS = len(POOL_WINDOWS)
CONV_K = 3
ADAM_LR = 0.001
ADAM_B1 = 0.9
ADAM_B2 = 0.999
ADAM_EPS = 1e-08
ADAM_WD = 0.01
ADAM_STEP = 10

N_CHIPS = 4
N_DEV = 8
HALO = 16
LANES = 128
V7X_VMEM_BYTES = 64 * 1024 * 1024
VMEM_CAP = V7X_VMEM_BYTES - 8 * 1024 * 1024

MESH = pl.DeviceIdType.MESH
ANY = pl.BlockSpec(memory_space=pl.ANY)

NN = (((1,), (0,)), ((), ()))
NT = (((1,), (1,)), ((), ()))
TN = (((0,), (0,)), ((), ()))


def _pick(dim, pref, align):
    if dim <= pref:
        return dim
    t = (pref // align) * align
    while t >= align:
        if dim % t == 0:
            return t
        t -= align
    raise ValueError(f"no tile for {dim} (pref {pref}, align {align})")


def _nbytes(shape, dtype):
    n = 1
    for s in shape:
        if s is not None:
            n *= s
    return n * jnp.dtype(dtype).itemsize


def _params(semantics, block_bytes, extra_bytes=0):
    need = 2 * block_bytes + extra_bytes + (2 << 20)
    return pltpu.CompilerParams(dimension_semantics=semantics,
                                vmem_limit_bytes=int(min(max(need, 16 << 20), VMEM_CAP)))


def _sigmoid(z):
    return jax.nn.sigmoid(z)


def _mm(name, *, grid, operands, in_specs, out_shape, out_specs, pairs, dims, acc_shapes, epilogue,
        semantics=None, temp_bytes=0, comm=None, prefetch=None, aliases=None):
    aliases = dict(aliases or {})
    n_in, n_out = len(operands), len(out_shape)
    kax = len(grid) - 1
    nk = grid[kax]
    single = nk == 1
    conditional = any(p[3] is not None for p in pairs)
    if single and conditional:
        assert len(acc_shapes) == 1 and all(p[3] is not None for p in pairs)
    n_acc = 0 if single else len(acc_shapes)
    n_pf = 0 if prefetch is None else 1
    c_ops = list(comm.operands) if comm else []
    c_out = list(comm.out_shape) if comm else []
    c_sems = list(comm.scratch) if comm else []
    c_alias = dict(comm.aliases) if comm else {}
    n_ci, n_co = len(c_ops), len(c_out)

    def product(ins, ai, bi):
        return lax.dot_general(ins[ai][...], ins[bi][...], dims, preferred_element_type=F32)

    def body(*refs):
        refs = refs[n_pf:]
        ins = refs[:n_in]
        c_in_refs = refs[n_in:n_in + n_ci]
        outs = refs[n_in + n_ci:n_in + n_ci + n_out]
        c_out_refs = refs[n_in + n_ci + n_out:n_in + n_ci + n_out + n_co]
        scratch = refs[n_in + n_ci + n_out + n_co:]
        accs = scratch[:n_acc]
        sems = scratch[n_acc:]
        ids = [pl.program_id(a) for a in range(len(grid))]
        k = ids[kax]

        if comm:
            is_first = ids[0] == 0
            is_last = ids[0] == grid[0] - 1
            for a in range(1, len(grid)):
                is_first = jnp.logical_and(is_first, ids[a] == 0)
                is_last = jnp.logical_and(is_last, ids[a] == grid[a] - 1)

            @pl.when(is_first)
            def _():
                comm.start(c_in_refs, c_out_refs, sems)

        if single and conditional:
            for (ai, bi, ci, cond) in pairs:
                def only(ai=ai, bi=bi):
                    epilogue([product(ins, ai, bi)], ins, outs, ids)
                pl.when(cond(ids))(only)
        elif single:
            vals = [None] * len(acc_shapes)
            for (ai, bi, ci, cond) in pairs:
                r = product(ins, ai, bi)
                vals[ci] = r if vals[ci] is None else vals[ci] + r
            epilogue(vals, ins, outs, ids)
        else:
            @pl.when(k == 0)
            def _():
                for a in accs:
                    a[...] = jnp.zeros(a.shape, a.dtype)

            for (ai, bi, ci, cond) in pairs:
                def step(ai=ai, bi=bi, ci=ci):
                    accs[ci][...] += product(ins, ai, bi)
                if cond is None:
                    step()
                else:
                    pl.when(cond(ids))(step)

            @pl.when(k == nk - 1)
            def _():
                epilogue([a[...] for a in accs], ins, outs, ids)

        if comm:
            @pl.when(is_last)
            def _():
                comm.finish(c_in_refs, c_out_refs, sems)

    if semantics is None:
        semantics = ("parallel",) * kax + ("arbitrary",)
    if comm:
        semantics = ("arbitrary",) * len(grid)
    in_specs = [ANY if idx in aliases else spec for idx, spec in enumerate(in_specs)]
    blk = 0
    for idx, (spec, op) in enumerate(zip(in_specs, operands)):
        if idx not in aliases:
            blk += _nbytes(spec.block_shape, op.dtype)
    for spec, o in zip(out_specs, out_shape):
        blk += _nbytes(spec.block_shape, o.dtype)
    acc_bytes = sum(_nbytes(s, F32) for s in acc_shapes)
    io_alias = {n_pf + i: o for i, o in aliases.items()}
    io_alias.update({n_pf + n_in + i: n_out + o for i, o in c_alias.items()})
    all_in = list(in_specs) + [ANY] * n_ci
    all_out = list(out_specs) + [ANY] * n_co
    scratch_shapes = [pltpu.VMEM(s, F32) for s in acc_shapes[:n_acc]] + c_sems
    params = _params(semantics, blk, 3 * acc_bytes + temp_bytes)
    shapes = list(out_shape) + c_out
    if prefetch is None:
        call = pl.pallas_call(
            body, name=name, grid=grid, in_specs=all_in, out_specs=all_out, out_shape=shapes,
            scratch_shapes=scratch_shapes, input_output_aliases=io_alias, compiler_params=params)
        res = call(*operands, *c_ops)
    else:
        call = pl.pallas_call(
            body, name=name, out_shape=shapes, input_output_aliases=io_alias, compiler_params=params,
            grid_spec=pltpu.PrefetchScalarGridSpec(
                num_scalar_prefetch=1, grid=grid, in_specs=all_in, out_specs=all_out,
                scratch_shapes=scratch_shapes))
        res = call(prefetch, *operands, *c_ops)
    return res[:n_out], res[n_out:]


def _cast_bf16(name, w2d):
    r, c = w2d.shape
    tr = _pick(r, max(16, (4 << 20) // (4 * c)), 16)

    def body(w_ref, o_ref):
        o_ref[...] = w_ref[...].astype(BF16)

    return pl.pallas_call(
        body, name=name, grid=(r // tr,),
        in_specs=[pl.BlockSpec((tr, c), lambda i: (i, 0))],
        out_specs=pl.BlockSpec((tr, c), lambda i: (i, 0)),
        out_shape=jax.ShapeDtypeStruct((r, c), BF16),
        compiler_params=_params(("parallel",), tr * c * 6),
    )(w2d)


def _rms_fwd(x, norm_w):
    s, d = x.shape
    ts = _pick(s, 256, 16)

    def body(x_ref, w_ref, h_ref):
        xv = x_ref[...]
        rstd = lax.rsqrt(jnp.mean(xv * xv, axis=-1, keepdims=True) + NORM_EPS)
        h_ref[...] = (xv * rstd * w_ref[...]).astype(BF16)

    return pl.pallas_call(
        body, name="rms_fwd", grid=(s // ts,),
        in_specs=[pl.BlockSpec((ts, d), lambda i: (i, 0)), pl.BlockSpec((1, d), lambda i: (0, 0))],
        out_specs=pl.BlockSpec((ts, d), lambda i: (i, 0)),
        out_shape=jax.ShapeDtypeStruct((s, d), BF16),
        compiler_params=_params(("parallel",), ts * d * 6, 3 * ts * d * 4),
    )(x, norm_w)


def _head(x, o, target, fnw):
    s, d = x.shape
    ts = _pick(s, 128, 16)

    def body(x_ref, o_ref, t_ref, w_ref, dx_ref, dxb_ref, loss_ref, gw_ref):
        i = pl.program_id(0)

        @pl.when(i == 0)
        def _():
            loss_ref[...] = jnp.zeros(loss_ref.shape, F32)
            gw_ref[...] = jnp.zeros(gw_ref.shape, F32)

        w = w_ref[...]
        x2 = x_ref[...] + o_ref[...]
        rstd = lax.rsqrt(jnp.mean(x2 * x2, axis=-1, keepdims=True) + NORM_EPS)
        n = x2 * rstd
        e = n * w - t_ref[...]
        loss_ref[...] += 0.5 * jnp.sum(e * e) / d
        dy = e / d
        gw_ref[...] += jnp.sum(dy * n, axis=0, keepdims=True)
        gy = dy * w
        dx = rstd * (gy - n * jnp.mean(gy * n, axis=-1, keepdims=True))
        dx_ref[...] = dx
        dxb_ref[...] = dx.astype(BF16)

    row = pl.BlockSpec((ts, d), lambda i: (i, 0))
    return pl.pallas_call(
        body, name="head", grid=(s // ts,),
        in_specs=[row, row, row, pl.BlockSpec((1, d), lambda i: (0, 0))],
        out_specs=[row, row, pl.BlockSpec((8, LANES), lambda i: (0, 0)),
                   pl.BlockSpec((1, d), lambda i: (0, 0))],
        out_shape=[jax.ShapeDtypeStruct((s, d), F32), jax.ShapeDtypeStruct((s, d), BF16),
                   jax.ShapeDtypeStruct((8, LANES), F32), jax.ShapeDtypeStruct((1, d), F32)],
        compiler_params=_params(("arbitrary",), ts * d * 22, 6 * ts * d * 4),
    )(x, o, target, fnw)


def _rms_bwd(x, dh, dx2, norm_w):
    s, d = x.shape
    ts = _pick(s, 128, 16)

    def body(x_ref, dh_ref, dx2_ref, w_ref, gx_ref, gw_ref):
        i = pl.program_id(0)

        @pl.when(i == 0)
        def _():
            gw_ref[...] = jnp.zeros(gw_ref.shape, F32)

        xv = x_ref[...]
        rstd = lax.rsqrt(jnp.mean(xv * xv, axis=-1, keepdims=True) + NORM_EPS)
        n = xv * rstd
        dhv = dh_ref[...]
        gw_ref[...] += jnp.sum(dhv * n, axis=0, keepdims=True)
        gh = dhv * w_ref[...]
        gx_ref[...] = dx2_ref[...] + rstd * (gh - n * jnp.mean(gh * n, axis=-1, keepdims=True))

    row = pl.BlockSpec((ts, d), lambda i: (i, 0))
    vec = pl.BlockSpec((1, d), lambda i: (0, 0))
    return pl.pallas_call(
        body, name="rms_bwd", grid=(s // ts,),
        in_specs=[row, row, row, vec], out_specs=[row, vec],
        out_shape=[jax.ShapeDtypeStruct((s, d), F32), jax.ShapeDtypeStruct((1, d), F32)],
        compiler_params=_params(("arbitrary",), ts * d * 16, 5 * ts * d * 4),
    )(x, dh, dx2, norm_w)


def _silu(z):
    return z * _sigmoid(z)


def _window_sum(ext, window, back):
    n = ext.shape[0]
    acc = ext
    step = 1
    while step < window:
        acc = acc + pltpu.roll(acc, step if back else n - step, 0)
        step *= 2
    return acc


def _shift_rows(ext, k, back):
    n = ext.shape[0]
    return pltpu.roll(ext, k if back else n - k, 0)


def _mix_fwd(proj, pool_wg, pool_scale, conv_wg, conv_b, width, comm):
    s = proj.shape[0]
    w = width
    cg = w // N_POOL_GROUPS
    ts = _pick(s, 128, HALO)
    hb = ts // HALO
    cols = 6 * w

    n_ci, n_co = len(comm.operands), len(comm.out_shape)
    n_steps = s // ts

    def body(p_ref, ph_ref, pw_ref, ps_ref, cw_ref, cb_ref, *rest):
        c_in, ys_ref = rest[:n_ci], rest[n_ci]
        c_out, sems = rest[n_ci + 1:n_ci + 1 + n_co], rest[n_ci + 1 + n_co:]
        i = pl.program_id(0)
        first = i == 0
        t1 = (i * ts + lax.broadcasted_iota(jnp.int32, (ts, 1), 0) + 1).astype(F32)

        @pl.when(first)
        def _():
            comm.start(c_in, c_out, sems)

        def tile(part, g):
            lo = part * w + g * cg
            return p_ref[:, lo:lo + cg].astype(F32)

        def prev(part, g):
            lo = part * w + g * cg
            return jnp.where(first, 0.0, ph_ref[:, lo:lo + cg].astype(F32))

        for g, win in enumerate(POOL_WINDOWS):
            gs = slice(g * cg, (g + 1) * cg)
            u = tile(0, g)
            ext = jnp.concatenate([prev(0, g), u], axis=0)
            wsum = _window_sum(ext, win, True)[HALO:]
            pooled = wsum / jnp.minimum(t1, float(win)) - u
            pw = pw_ref[:, g].reshape(cg, cg)
            mixed = jnp.dot(pooled.astype(BF16), pw, preferred_element_type=F32)
            ys_ref[0, :, gs] = (mixed * ps_ref[:, gs] * _silu(tile(1, g))).astype(BF16)
            v = tile(4, g) * tile(2, g)
            vext = jnp.concatenate([prev(4, g) * prev(2, g), v], axis=0)
            v1 = _shift_rows(vext, 1, True)[HALO:]
            v2 = _shift_rows(vext, 2, True)[HALO:]
            cw = [cw_ref[g, tap:tap + 1, :] for tap in range(CONV_K)]
            y = cb_ref[:, gs] + cw[0] * v2 + cw[1] * v1 + cw[2] * v
            ys_ref[1, :, gs] = (tile(3, g) * y * _silu(tile(5, g))).astype(BF16)

        @pl.when(i == n_steps - 1)
        def _():
            comm.finish(c_in, c_out, sems)

    res = pl.pallas_call(
        body, name="mix_fwd", grid=(n_steps,),
        in_specs=[pl.BlockSpec((ts, cols), lambda i: (i, 0)),
                  pl.BlockSpec((HALO, cols), lambda i: (jnp.maximum(i * hb - 1, 0), 0)),
                  pl.BlockSpec(pool_wg.shape, lambda i: (0, 0, 0, 0)),
                  pl.BlockSpec((1, w), lambda i: (0, 0)),
                  pl.BlockSpec(conv_wg.shape, lambda i: (0, 0, 0)),
                  pl.BlockSpec((1, w), lambda i: (0, 0))] + [ANY] * n_ci,
        out_specs=[pl.BlockSpec((2, ts, w), lambda i: (0, i, 0))] + [ANY] * n_co,
        out_shape=[jax.ShapeDtypeStruct((2, s, w), BF16)] + list(comm.out_shape),
        scratch_shapes=list(comm.scratch),
        input_output_aliases={6 + i: 1 + o for i, o in comm.aliases.items()},
        compiler_params=_params(("arbitrary",), (ts + HALO) * cols * 2 + 2 * ts * w * 2
                                + _nbytes(pool_wg.shape, BF16), 24 * (ts + HALO) * cg * 4),
    )(proj, proj, pool_wg, pool_scale, conv_wg, conv_b, *comm.operands)
    return res[0], res[1:]


def _mix_bwd(proj, dys, pool_wg, pool_scale, conv_wg, conv_b, width):
    s = proj.shape[0]
    w = width
    cg = w // N_POOL_GROUPS
    ts = _pick(s, 128, HALO)
    hb = ts // HALO
    n_tiles = s // ts
    last_hb = s // HALO - 1
    cols = 6 * w

    def body(p_ref, ph_ref, pn_ref, dy_ref, dyn_ref, pw_ref, ps_ref, cw_ref, cb_ref,
             dp_ref, dpw_ref, dps_ref, dcw_ref, dcb_ref):
        i = pl.program_id(0)
        first = i == 0
        last = i == n_tiles - 1

        @pl.when(first)
        def _():
            dpw_ref[...] = jnp.zeros(dpw_ref.shape, F32)
            dps_ref[...] = jnp.zeros(dps_ref.shape, F32)
            dcw_ref[...] = jnp.zeros(dcw_ref.shape, F32)
            dcb_ref[...] = jnp.zeros(dcb_ref.shape, F32)

        row = i * ts + lax.broadcasted_iota(jnp.int32, (ts + HALO, 1), 0)
        t1_ext = (row + 1).astype(F32)
        t1 = t1_ext[:ts]

        def tile(part, g):
            lo = part * w + g * cg
            return p_ref[:, lo:lo + cg].astype(F32)

        def prev(part, g):
            lo = part * w + g * cg
            return jnp.where(first, 0.0, ph_ref[:, lo:lo + cg].astype(F32))

        def ahead(part, g):
            lo = part * w + g * cg
            return jnp.concatenate([tile(part, g), pn_ref[:, lo:lo + cg].astype(F32)], axis=0)

        def dy_ahead(n, g):
            gs = slice(g * cg, (g + 1) * cg)
            nxt = jnp.where(last, 0.0, dyn_ref[n, :, gs])
            return jnp.concatenate([dy_ref[n, :, gs], nxt], axis=0)

        for g, win in enumerate(POOL_WINDOWS):
            gs = slice(g * cg, (g + 1) * cg)
            u = tile(0, g)
            ext = jnp.concatenate([prev(0, g), u], axis=0)
            pooled = _window_sum(ext, win, True)[HALO:] / jnp.minimum(t1, float(win)) - u
            pooled_b = pooled.astype(BF16)
            pw = pw_ref[:, g].reshape(cg, cg)
            mixed = jnp.dot(pooled_b, pw, preferred_element_type=F32)
            zp_ext = ahead(1, g)
            dy0_ext = dy_ahead(0, g)
            scale = ps_ref[:, gs]
            sp_ext = _sigmoid(zp_ext)
            dms_ext = dy0_ext * (zp_ext * sp_ext)
            dmix_b = (dms_ext * scale).astype(BF16)
            dpooled_ext = lax.dot_general(dmix_b, pw, NT, preferred_element_type=F32)
            dy0 = dy0_ext[:ts]
            zp, sp = zp_ext[:ts], sp_ext[:ts]
            dsilu_p = sp * (1.0 + zp * (1.0 - sp))
            dp_ref[:, w + g * cg:w + (g + 1) * cg] = (dy0 * mixed * scale * dsilu_p).astype(BF16)
            dps_ref[:, gs] += jnp.sum(dms_ext[:ts] * mixed, axis=0, keepdims=True)
            dpw = lax.dot_general(pooled_b, dmix_b[:ts], TN, preferred_element_type=F32)
            dpw_ref[g // 2, :, g % 2] += dpw.reshape(N_CHIPS, cg // N_CHIPS, cg)
            q_ext = dpooled_ext / jnp.minimum(t1_ext, float(win))
            du = _window_sum(q_ext, win, False)[:ts] - dpooled_ext[:ts]
            dp_ref[:, gs] = du.astype(BF16)
            uc = tile(2, g)
            cc = tile(4, g)
            v = cc * uc
            vext = jnp.concatenate([prev(4, g) * prev(2, g), v], axis=0)
            v1 = _shift_rows(vext, 1, True)[HALO:]
            v2 = _shift_rows(vext, 2, True)[HALO:]
            cw = [cw_ref[g, tap:tap + 1, :] for tap in range(CONV_K)]
            y = cb_ref[:, gs] + cw[0] * v2 + cw[1] * v1 + cw[2] * v
            bc_ext = ahead(3, g)
            zc_ext = ahead(5, g)
            dy1_ext = dy_ahead(1, g)
            sc_ext = _sigmoid(zc_ext)
            silu_c_ext = zc_ext * sc_ext
            dyy_ext = dy1_ext * bc_ext * silu_c_ext
            dy1 = dy1_ext[:ts]
            bc = bc_ext[:ts]
            zc, sc = zc_ext[:ts], sc_ext[:ts]
            dsilu_c = sc * (1.0 + zc * (1.0 - sc))
            dp_ref[:, 3 * w + g * cg:3 * w + (g + 1) * cg] = (dy1 * y * silu_c_ext[:ts]).astype(BF16)
            dp_ref[:, 5 * w + g * cg:5 * w + (g + 1) * cg] = (dy1 * bc * y * dsilu_c).astype(BF16)
            dyy = dyy_ext[:ts]
            dcb_ref[:, gs] += jnp.sum(dyy, axis=0, keepdims=True)
            for tap, vt in enumerate((v2, v1, v)):
                dcw_ref[g, tap:tap + 1, :] += jnp.sum(dyy * vt, axis=0, keepdims=True)
            dv = (cw[2] * dyy + cw[1] * _shift_rows(dyy_ext, 1, False)[:ts]
                  + cw[0] * _shift_rows(dyy_ext, 2, False)[:ts])
            dp_ref[:, 4 * w + g * cg:4 * w + (g + 1) * cg] = (dv * uc).astype(BF16)
            dp_ref[:, 2 * w + g * cg:2 * w + (g + 1) * cg] = (dv * cc).astype(BF16)

    dpw_shape = (2, N_CHIPS, 2, cg // N_CHIPS, cg)
    return pl.pallas_call(
        body, name="mix_bwd", grid=(n_tiles,),
        in_specs=[pl.BlockSpec((ts, cols), lambda i: (i, 0)),
                  pl.BlockSpec((HALO, cols), lambda i: (jnp.maximum(i * hb - 1, 0), 0)),
                  pl.BlockSpec((HALO, cols), lambda i: (jnp.minimum((i + 1) * hb, last_hb), 0)),
                  pl.BlockSpec((2, ts, w), lambda i: (0, i, 0)),
                  pl.BlockSpec((2, HALO, w), lambda i: (0, jnp.minimum((i + 1) * hb, last_hb), 0)),
                  pl.BlockSpec(pool_wg.shape, lambda i: (0, 0, 0, 0)),
                  pl.BlockSpec((1, w), lambda i: (0, 0)),
                  pl.BlockSpec(conv_wg.shape, lambda i: (0, 0, 0)),
                  pl.BlockSpec((1, w), lambda i: (0, 0))],
        out_specs=[pl.BlockSpec((ts, cols), lambda i: (i, 0)),
                   pl.BlockSpec(dpw_shape, lambda i: (0, 0, 0, 0, 0)),
                   pl.BlockSpec((1, w), lambda i: (0, 0)),
                   pl.BlockSpec(conv_wg.shape, lambda i: (0, 0, 0)),
                   pl.BlockSpec((1, w), lambda i: (0, 0))],
        out_shape=[jax.ShapeDtypeStruct((s, cols), BF16), jax.ShapeDtypeStruct(dpw_shape, F32),
                   jax.ShapeDtypeStruct((1, w), F32), jax.ShapeDtypeStruct(conv_wg.shape, F32),
                   jax.ShapeDtypeStruct((1, w), F32)],
        compiler_params=_params(("arbitrary",), (2 * ts + 2 * HALO) * cols * 2 + (ts + HALO) * w * 8
                                + _nbytes(pool_wg.shape, BF16) + _nbytes(dpw_shape, F32),
                                40 * (ts + HALO) * cg * 4),
    )(proj, proj, proj, dys, dys, pool_wg, pool_scale, conv_wg, conv_b)


def _adamw(name, g, w, m, v):
    r, c = w.shape
    tr = _pick(r, max(8, (1 << 20) // (4 * c)), 8)

    def body(g_ref, w_ref, m_ref, v_ref, go_ref, d_ref, mo_ref, vo_ref):
        gv = g_ref[...]
        mn = ADAM_B1 * m_ref[...] + (1.0 - ADAM_B1) * gv
        vn = ADAM_B2 * v_ref[...] + (1.0 - ADAM_B2) * (gv * gv)
        m_hat = mn / (1.0 - ADAM_B1 ** ADAM_STEP)
        v_hat = vn / (1.0 - ADAM_B2 ** ADAM_STEP)
        go_ref[...] = gv
        d_ref[...] = -ADAM_LR * (m_hat / (jnp.sqrt(v_hat) + ADAM_EPS) + ADAM_WD * w_ref[...])
        mo_ref[...] = mn
        vo_ref[...] = vn

    blk = pl.BlockSpec((tr, c), lambda i: (i, 0))
    sh = jax.ShapeDtypeStruct((r, c), F32)
    return pl.pallas_call(
        body, name=name, grid=(r // tr,), in_specs=[blk] * 4, out_specs=[blk] * 4,
        out_shape=[sh] * 4, compiler_params=_params(("parallel",), tr * c * 32, 4 * tr * c * 4),
    )(g, w, m, v)


def _pair_add(name, g, r1, c_idx):
    _, r, c = g.shape
    tr = _pick(r, max(16, (2 << 20) // (2 * c)), 16)

    def body(c_ref, g_ref, r_ref, o_ref):
        o_ref[...] = (g_ref[...].astype(F32) + r_ref[...].astype(F32)).astype(BF16)

    return pl.pallas_call(
        body, name=name,
        grid_spec=pltpu.PrefetchScalarGridSpec(
            num_scalar_prefetch=1, grid=(r // tr,),
            in_specs=[pl.BlockSpec((None, tr, c), lambda i, cr: (cr[0], i, 0)),
                      pl.BlockSpec((tr, c), lambda i, cr: (i, 0))],
            out_specs=pl.BlockSpec((tr, c), lambda i, cr: (i, 0))),
        out_shape=jax.ShapeDtypeStruct((r, c), BF16),
        compiler_params=_params(("parallel",), tr * c * 6, 3 * tr * c * 4),
    )(c_idx, g, r1)


def _adamw_halves(name, g_own, g_other, c_idx, w, m, v):
    _, r, c = w.shape
    tr = _pick(r, max(8, (5 << 18) // (4 * c)), 8)

    def body(c_ref, go_ref, gt_ref, w_ref, m_ref, v_ref, g_out, d_ref, mo_ref, vo_ref):
        gv = jnp.where(pl.program_id(0) == c_ref[0], go_ref[...], gt_ref[...])
        mn = ADAM_B1 * m_ref[...] + (1.0 - ADAM_B1) * gv
        vn = ADAM_B2 * v_ref[...] + (1.0 - ADAM_B2) * (gv * gv)
        m_hat = mn / (1.0 - ADAM_B1 ** ADAM_STEP)
        v_hat = vn / (1.0 - ADAM_B2 ** ADAM_STEP)
        g_out[...] = gv
        d_ref[...] = -ADAM_LR * (m_hat / (jnp.sqrt(v_hat) + ADAM_EPS) + ADAM_WD * w_ref[...])
        mo_ref[...] = mn
        vo_ref[...] = vn

    blk = pl.BlockSpec((None, tr, c), lambda h, i, cr: (h, i, 0))
    sh = jax.ShapeDtypeStruct(w.shape, F32)
    return pl.pallas_call(
        body, name=name,
        grid_spec=pltpu.PrefetchScalarGridSpec(
            num_scalar_prefetch=1, grid=(2, r // tr),
            in_specs=[pl.BlockSpec((tr, c), lambda h, i, cr: (jnp.where(h == cr[0], i, 0), 0)),
                      pl.BlockSpec((tr, c), lambda h, i, cr: (jnp.where(h == cr[0], 0, i), 0)),
                      blk, blk, blk],
            out_specs=[blk] * 4),
        out_shape=[sh] * 4,
        compiler_params=_params(("arbitrary", "arbitrary"), tr * c * 36, 4 * tr * c * 4),
    )(c_idx, g_own, g_other, w, m, v)


def _sum_chips(name, part, r2, chip_idx, row0=0, prev=None):
    n, r, c = r2.shape
    rc = part.shape[1]
    tr = _pick(math.gcd(rc, row0) if row0 else rc, max(16, (1 << 20) // (2 * c)), 16)
    b0 = row0 // tr

    def body(ch_ref, own_ref, *rest):
        slots, o_ref = rest[:n], rest[-1]
        acc = None
        for s in range(n):
            term = jnp.where(ch_ref[0] == s, own_ref[...], slots[s][...]).astype(F32)
            acc = term if acc is None else acc + term
        o_ref[...] = acc

    def slot_spec(s):
        return pl.BlockSpec((None, tr, c), lambda i, ch: (jnp.where(ch[0] == s, (s + 1) % n, s), b0 + i, 0))

    extra = [] if prev is None else [prev]
    return pl.pallas_call(
        body, name=name,
        grid_spec=pltpu.PrefetchScalarGridSpec(
            num_scalar_prefetch=1, grid=(rc // tr,),
            in_specs=[pl.BlockSpec((None, tr, c), lambda i, ch: (ch[0], i, 0))]
            + [slot_spec(s) for s in range(n)] + [ANY] * len(extra),
            out_specs=pl.BlockSpec((tr, c), lambda i, ch: (b0 + i, 0))),
        out_shape=jax.ShapeDtypeStruct((r, c), F32),
        input_output_aliases={2 + n: 0} if extra else {},
        compiler_params=_params(("parallel",), tr * c * (2 * n + 6), 3 * tr * c * 4),
    )(chip_idx, part, *([r2] * n), *extra)


def _sum_devices(packs):
    n, r, c = packs.shape

    def body(p_ref, o_ref):
        acc = p_ref[0]
        for k in range(1, n):
            acc = acc + p_ref[k]
        o_ref[...] = acc

    return pl.pallas_call(
        body, name="sum_devices", out_shape=jax.ShapeDtypeStruct((r, c), F32),
        in_specs=[pl.BlockSpec(memory_space=pltpu.VMEM)],
        out_specs=pl.BlockSpec(memory_space=pltpu.VMEM),
    )(packs)


def _place():
    x, y, c = lax.axis_index("x"), lax.axis_index("y"), lax.axis_index("c")
    return x, y, c


def _chip_peers(x, y):
    out = []
    for k, (fx, fy) in enumerate(((0, 1), (1, 0), (1, 1))):
        px = 1 - x if fx else x
        py = 1 - y if fy else y
        out.append((k, px, py, 2 * px + py))
    return out


def _gather_weights(big, small, relay=False, phase=None, prev=None):
    nb, ns = len(big), len(small)

    class Copies:
        def __init__(self, c_in, c_out, sems):
            b_in, s_in = c_in[:nb], c_in[nb:]
            b_out, s_out = c_out[:nb], c_out[nb:]
            ici_s, ici_r, d2d_s, d2d_r, own_s, own_r = sems[:6]
            x, y, c = _place()
            chip = 2 * x + y
            sibling = (x, y, 1 - c)
            peers = _chip_peers(x, y)
            self.own = [pltpu.make_async_remote_copy(
                src_ref=b_in[t], dst_ref=b_out[t].at[chip], send_sem=own_s.at[t], recv_sem=own_r.at[t],
                device_id=sibling, device_id_type=MESH) for t in range(nb)]
            self.ici = [pltpu.make_async_remote_copy(
                src_ref=b_in[t].at[c], dst_ref=b_out[t].at[chip, c],
                send_sem=ici_s.at[t, k], recv_sem=ici_r.at[t, k],
                device_id=(px, py, c), device_id_type=MESH)
                for t in range(nb) for (k, px, py, pchip) in peers if not (relay and k == 2)]
            self.relay = []
            if relay:
                south = c == 0
                from_chip = jnp.where(south, 2 * x + (1 - y), 2 * (1 - x) + y)
                to = (jnp.where(south, 1 - x, x), jnp.where(south, y, 1 - y), c)
                self.relay = [pltpu.make_async_remote_copy(
                    src_ref=b_out[t].at[from_chip, c], dst_ref=b_out[t].at[from_chip, c],
                    send_sem=ici_s.at[t, 2], recv_sem=ici_r.at[t, 2],
                    device_id=to, device_id_type=MESH) for t in range(nb)]
            self.landed = [pltpu.make_async_remote_copy(
                src_ref=b_out[t].at[pchip, c], dst_ref=b_out[t].at[pchip, c],
                send_sem=ici_s.at[t, k], recv_sem=ici_r.at[t, k],
                device_id=sibling, device_id_type=MESH)
                for t in range(nb) for (k, px, py, pchip) in peers]
            self.passed = [pltpu.make_async_remote_copy(
                src_ref=b_out[t].at[pchip, c], dst_ref=b_out[t].at[pchip, c],
                send_sem=d2d_s.at[t, k], recv_sem=d2d_r.at[t, k],
                device_id=sibling, device_id_type=MESH)
                for t in range(nb) for (k, px, py, pchip) in peers]
            self.from_sibling = [pltpu.make_async_remote_copy(
                src_ref=b_out[t].at[pchip, 1 - c], dst_ref=b_out[t].at[pchip, 1 - c],
                send_sem=d2d_s.at[t, k], recv_sem=d2d_r.at[t, k],
                device_id=sibling, device_id_type=MESH)
                for t in range(nb) for (k, px, py, pchip) in peers]
            self.small, self.small_landed, self.local = [], [], []
            if ns:
                sm_s, sm_r, loc = sems[6:]
                self.local = [pltpu.make_async_copy(s_in[t], s_out[t].at[chip], loc.at[t]) for t in range(ns)]
                self.small = [pltpu.make_async_remote_copy(
                    src_ref=s_in[t], dst_ref=s_out[t].at[chip],
                    send_sem=sm_s.at[t, k], recv_sem=sm_r.at[t, k],
                    device_id=(px, py, c), device_id_type=MESH)
                    for t in range(ns) for (k, px, py, pchip) in peers]
                self.small_landed = [pltpu.make_async_remote_copy(
                    src_ref=s_in[t], dst_ref=s_out[t].at[pchip],
                    send_sem=sm_s.at[t, k], recv_sem=sm_r.at[t, k],
                    device_id=sibling, device_id_type=MESH)
                    for t in range(ns) for (k, px, py, pchip) in peers]

    assert phase is None or (relay and not small)
    near = [3 * t + k for t in range(nb) for k in range(2)]
    diag = [3 * t + 2 for t in range(nb)]

    def start(c_in, c_out, sems):
        cps = Copies(c_in, c_out, sems)
        if phase == 2:
            for cp in cps.relay:
                cp.start()
        else:
            for cp in cps.local + cps.own + cps.ici + cps.small:
                cp.start()

    def finish(c_in, c_out, sems):
        cps = Copies(c_in, c_out, sems)
        legs = near + diag if phase is None else (near if phase == 1 else diag)
        for n in sorted(legs):
            if relay and phase is None and n % 3 == 2:
                cps.relay[n // 3].start()
            cps.landed[n].wait_recv()
            cps.passed[n].start()
        for cp in (cps.small_landed if phase != 2 else []) + [cps.from_sibling[n] for n in legs]:
            cp.wait_recv()
        sends = [cps.passed[n] for n in legs]
        if phase != 2:
            sends += cps.ici + cps.small
        if phase != 1:
            sends += cps.relay
        for cp in sends:
            cp.wait_send()
        if phase != 2:
            for cp in cps.own + cps.local:
                cp.wait()

    out_shape = [jax.ShapeDtypeStruct((N_CHIPS,) + b.shape, b.dtype) for b in big]
    out_shape += [jax.ShapeDtypeStruct((N_CHIPS,) + s.shape, s.dtype) for s in small]
    dma = pltpu.SemaphoreType.DMA
    scratch = [dma((nb, 3)), dma((nb, 3)), dma((nb, 3)), dma((nb, 3)), dma((nb,)), dma((nb,))]
    if ns:
        scratch += [dma((ns, 3)), dma((ns, 3)), dma((ns,))]
    operands = list(big) + list(small) + (list(prev) if prev else [])
    ex = _Exchange(operands, out_shape, scratch, start, finish,
                   aliases={nb + t: t for t in range(nb)} if prev else None)
    ex.copies = Copies
    return ex


def _join(exs):
    assert not any(e.aliases for e in exs)

    def parts(refs, counts):
        out, pos = [], 0
        for n in counts:
            out.append(refs[pos:pos + n])
            pos += n
        return out

    def run(which):
        def go(c_in, c_out, sems):
            ins = parts(c_in, [len(e.operands) for e in exs])
            outs = parts(c_out, [len(e.out_shape) for e in exs])
            ss = parts(sems, [len(e.scratch) for e in exs])
            for e, a, b, c in zip(exs, ins, outs, ss):
                getattr(e, which)(a, b, c)
        return go

    return _Exchange(sum((list(e.operands) for e in exs), []), sum((list(e.out_shape) for e in exs), []),
                     sum((list(e.scratch) for e in exs), []), run("start"), run("finish"))


class _Exchange:
    def __init__(self, operands, out_shape, scratch, start, finish, aliases=None):
        self.operands, self.out_shape, self.scratch = operands, out_shape, scratch
        self.start, self.finish, self.aliases = start, finish, dict(aliases or {})


def _run_exchange(name, ex):
    n_i, n_o = len(ex.operands), len(ex.out_shape)

    def body(*refs):
        ins, outs, sems = refs[:n_i], refs[n_i:n_i + n_o], refs[n_i + n_o:]
        ex.start(ins, outs, sems)
        ex.finish(ins, outs, sems)

    return list(pl.pallas_call(
        body, name=name, out_shape=list(ex.out_shape), in_specs=[ANY] * n_i, out_specs=[ANY] * n_o,
        scratch_shapes=list(ex.scratch), input_output_aliases=ex.aliases,
    )(*ex.operands))


def _to_sibling(arrays, other_half):
    n = len(arrays)

    def copies(a_in, a_out, sems):
        send, recv = sems
        x, y, c = _place()
        return [pltpu.make_async_remote_copy(
            src_ref=a_in[t].at[1 - c] if other_half else a_in[t], dst_ref=a_out[t],
            send_sem=send.at[t], recv_sem=recv.at[t],
            device_id=(x, y, 1 - c), device_id_type=MESH) for t in range(n)]

    def start(a_in, a_out, sems):
        for cp in copies(a_in, a_out, sems):
            cp.start()

    def finish(a_in, a_out, sems):
        for cp in copies(a_in, a_out, sems):
            cp.wait()

    dma = pltpu.SemaphoreType.DMA
    shapes = [jax.ShapeDtypeStruct(a.shape[1:] if other_half else a.shape, a.dtype) for a in arrays]
    return _Exchange(list(arrays), shapes, [dma((n,)), dma((n,))], start, finish)


def _scatter_partials(parts, rows=None, row0=0, prev=None):
    n = len(parts)
    land = [(p.shape[0], p.shape[1] if rows is None else rows, p.shape[2]) for p in parts]

    def window(ref, slot, t):
        return ref.at[slot, pl.ds(row0, parts[t].shape[1])]

    def sends(p_in, r_out, sems):
        send, recv = sems
        x, y, c = _place()
        chip = 2 * x + y
        return [pltpu.make_async_remote_copy(
            src_ref=p_in[t].at[pchip], dst_ref=window(r_out[t], chip, t),
            send_sem=send.at[t, k], recv_sem=recv.at[t, k],
            device_id=(px, py, c), device_id_type=MESH)
            for t in range(n) for (k, px, py, pchip) in _chip_peers(x, y)]

    def start(p_in, r_out, sems):
        for cp in sends(p_in, r_out, sems):
            cp.start()

    def finish(p_in, r_out, sems):
        send, recv = sems
        x, y, c = _place()
        for t in range(n):
            for (k, px, py, pchip) in _chip_peers(x, y):
                pltpu.make_async_remote_copy(
                    src_ref=p_in[t].at[pchip], dst_ref=window(r_out[t], pchip, t),
                    send_sem=send.at[t, k], recv_sem=recv.at[t, k],
                    device_id=(px, py, c), device_id_type=MESH).wait_recv()
        for cp in sends(p_in, r_out, sems):
            cp.wait_send()

    dma = pltpu.SemaphoreType.DMA
    operands = list(parts) + (list(prev) if prev else [])
    return _Exchange(operands, [jax.ShapeDtypeStruct(sh, p.dtype) for sh, p in zip(land, parts)],
                     [dma((n, 3)), dma((n, 3))], start, finish,
                     aliases={n + t: t for t in range(n)} if prev else None)


def _w_in_grad_sibling(h, dpa, dgl, c_idx, carried, *, tm, tn, rows, pq):
    s, d = h.shape
    wa, wg = dpa.shape[1], dgl.shape[2]
    p = wa + 2 * wg
    ni, nj = rows // tm, p // tn
    na, qg, qp = wa // tn, wg // tn, pq // tn
    n_steps = ni * nj
    n_ci, n_co = len(carried.operands), len(carried.out_shape)

    def body(c_ref, h_ref, a_ref, g_ref, *rest):
        c_in = rest[:n_ci]
        r1_ref = rest[n_ci]
        c_out = rest[n_ci + 1:n_ci + 1 + n_co]
        slots, send, recv = rest[n_ci + 1 + n_co:n_ci + 4 + n_co]
        sems = rest[n_ci + 4 + n_co:]
        i, j = pl.program_id(0), pl.program_id(1)
        step = i * nj + j
        slot = lax.rem(step, 2)
        x, y, c = _place()
        sibling = (x, y, 1 - c)

        @pl.when(step == 0)
        def _():
            carried.start(c_in, c_out, sems)

        def tile_copy(sl):
            return pltpu.make_async_remote_copy(
                src_ref=slots.at[sl],
                dst_ref=r1_ref.at[j // qp, pl.ds(i * tm, tm), pl.ds((j % qp) * tn, tn)],
                send_sem=send.at[sl], recv_sem=recv, device_id=sibling, device_id_type=MESH)

        @pl.when(step >= 2)
        def _():
            tile_copy(slot).wait_send()

        def emit(b_ref):
            acc = lax.dot_general(h_ref[...], b_ref[...], TN, preferred_element_type=F32)
            slots[slot] = acc.astype(BF16)

        pl.when(j < na)(lambda: emit(a_ref))
        pl.when(j >= na)(lambda: emit(g_ref))
        tile_copy(slot).start()

        @pl.when(step == n_steps - 1)
        def _():
            tile_copy(0).wait_send()
            tile_copy(1).wait_send()
            pltpu.make_async_remote_copy(
                src_ref=r1_ref, dst_ref=r1_ref, send_sem=send.at[0], recv_sem=recv,
                device_id=sibling, device_id_type=MESH).wait_recv()
            carried.finish(c_in, c_out, sems)

    nbh = rows // tm
    dma = pltpu.SemaphoreType.DMA
    blk = s * tm * 2 + 2 * s * tn * 2
    res = pl.pallas_call(
        body, name="w_in_grad_sibling",
        grid_spec=pltpu.PrefetchScalarGridSpec(
            num_scalar_prefetch=1, grid=(ni, nj),
            in_specs=[pl.BlockSpec((s, tm), lambda i, j, cr: (0, (1 - cr[0]) * nbh + i)),
                      pl.BlockSpec((s, tn), lambda i, j, cr: (0, jnp.minimum(j, na - 1))),
                      pl.BlockSpec((None, s, tn),
                                   lambda i, j, cr: (jnp.maximum(j - na, 0) // qg, 0, jnp.maximum(j - na, 0) % qg))]
            + [ANY] * n_ci,
            out_specs=[ANY] * (1 + n_co),
            scratch_shapes=[pltpu.VMEM((2, tm, tn), BF16), dma((2,)), dma] + list(carried.scratch)),
        out_shape=[jax.ShapeDtypeStruct((N_CHIPS, rows, pq), BF16)] + list(carried.out_shape),
        compiler_params=_params(("arbitrary", "arbitrary"), blk, 2 * tm * tn * 2 + 2 * tm * tn * 4),
    )(c_idx, h, dpa, dgl, *carried.operands)
    return res[0], res[1:]


def _proj_gathering(h, wb, chip_idx, gather_rest, *, tm, tn):
    s, d = h.shape
    _, hh, pq = wb.shape
    qp, ni = pq // tn, s // tm
    n_tiles = N_CHIPS * qp
    n_ci, n_co, n_cs = len(gather_rest.operands), len(gather_rest.out_shape), len(gather_rest.scratch)

    def where_of(t):
        near = t - qp
        g = jnp.where(t < qp, 0, jnp.where(t < 3 * qp, 1 + lax.rem(near, 2), 3))
        jj = jnp.where(t < qp, t, jnp.where(t < 3 * qp, near // 2, t - 3 * qp))
        return g, jj

    def body(ch_ref, h_ref, own_ref, *rest):
        c_in = rest[:n_ci]
        o_ref, gathered = rest[n_ci], rest[n_ci + 1]
        c_out = rest[n_ci + 2:n_ci + 2 + n_co]
        slots, fetch_sem, ici_s, ici_r, d2d_s, d2d_r, own_s, own_r = rest[n_ci + 2 + n_co:n_ci + 10 + n_co]
        c_sem = rest[n_ci + 10 + n_co:]
        tile, i = pl.program_id(0), pl.program_id(1)
        x, y, c = _place()
        chip = 2 * x + y
        sibling = (x, y, 1 - c)

        def cols(jj):
            return pl.ds(jj * tn, tn)

        def remote(src, dst, send, recv, to):
            return pltpu.make_async_remote_copy(src_ref=src, dst_ref=dst, send_sem=send, recv_sem=recv,
                                                device_id=to, device_id_type=MESH)

        def own_shard():
            return remote(own_ref, gathered.at[chip], own_s, own_r, sibling)

        def to_neighbour(k, jj):
            to = (x, 1 - y, c) if k == 0 else (1 - x, y, c)
            return remote(own_ref.at[c, :, cols(jj)], gathered.at[chip, c, :, cols(jj)],
                          ici_s.at[k, jj], ici_r.at[k, jj], to)

        def relay(jj):
            south = c == 0
            from_chip = jnp.where(south, 2 * x + (1 - y), 2 * (1 - x) + y)
            to = (jnp.where(south, 1 - x, x), jnp.where(south, y, 1 - y), c)
            chunk = gathered.at[from_chip, c, :, cols(jj)]
            return remote(chunk, chunk, ici_s.at[2, jj], ici_r.at[2, jj], to)

        def chunk_of(k, jj, half):
            return gathered.at[jnp.bitwise_xor(chip, k + 1), half, :, cols(jj)]

        def landed(k, jj):
            return remote(chunk_of(k, jj, c), chunk_of(k, jj, c), ici_s.at[k, jj], ici_r.at[k, jj], sibling)

        def passed(k, jj):
            return remote(chunk_of(k, jj, c), chunk_of(k, jj, c), d2d_s.at[k, jj], d2d_r.at[k, jj], sibling)

        def from_sibling(k, jj):
            return remote(chunk_of(k, jj, 1 - c), chunk_of(k, jj, 1 - c), d2d_s.at[k, jj], d2d_r.at[k, jj], sibling)

        def fetch(src, slot):
            return pltpu.make_async_copy(src, slots.at[slot], fetch_sem.at[slot])

        @pl.when(jnp.logical_and(tile == 0, i == 0))
        def _():
            own_shard().start()
            for jj in range(qp):
                for k in range(2):
                    to_neighbour(k, jj).start()
            fetch(own_ref.at[:, :, cols(0)], 0).start()

        @pl.when(i == 0)
        def _():
            fetch(own_ref.at[:, :, cols(0)], lax.rem(tile, 2)).wait()

        last_row = i == ni - 1
        nxt = tile + 1
        g1, j1 = where_of(nxt)
        pass_row = i == max(ni - 2, 0)

        @pl.when(jnp.logical_and(pass_row, jnp.logical_and(nxt >= qp, nxt < n_tiles)))
        def _():
            landed(g1 - 1, j1).wait_recv()
            passed(g1 - 1, j1).start()

        @pl.when(jnp.logical_and(pass_row, nxt == 3 * qp - 1))
        def _():
            gather_rest.start(c_in, c_out, c_sem)
            for ch in range(qp):
                relay(ch).start()

        @pl.when(jnp.logical_and(last_row, nxt < qp))
        def _():
            fetch(own_ref.at[:, :, cols(nxt)], lax.rem(nxt, 2)).start()

        @pl.when(jnp.logical_and(last_row, jnp.logical_and(nxt >= qp, nxt < n_tiles)))
        def _():
            from_sibling(g1 - 1, j1).wait_recv()
            fetch(gathered.at[jnp.bitwise_xor(chip, g1), :, :, cols(j1)], lax.rem(nxt, 2)).start()

        w_tile = slots[lax.rem(tile, 2)].reshape(d, tn)
        o_ref[...] = jnp.dot(h_ref[...], w_tile, preferred_element_type=F32).astype(BF16)

        @pl.when(jnp.logical_and(tile == n_tiles - 1, last_row))
        def _():
            for jj in range(qp):
                for k in range(2):
                    to_neighbour(k, jj).wait_send()
                relay(jj).wait_send()
                for k in range(3):
                    passed(k, jj).wait_send()
            own_shard().wait()
            gather_rest.finish(c_in, c_out, c_sem)

    assert qp >= 2 and not gather_rest.aliases
    dma = pltpu.SemaphoreType.DMA
    p = N_CHIPS * pq

    def out_cols(t, i, ch):
        g, jj = where_of(t)
        return (i, jnp.bitwise_xor(ch[0], g) * qp + jj)

    res = pl.pallas_call(
        body, name="proj_gathering",
        grid_spec=pltpu.PrefetchScalarGridSpec(
            num_scalar_prefetch=1, grid=(n_tiles, ni),
            in_specs=[pl.BlockSpec((tm, d), lambda t, i, ch: (i, 0)), ANY] + [ANY] * n_ci,
            out_specs=[pl.BlockSpec((tm, tn), out_cols), ANY] + [ANY] * n_co,
            scratch_shapes=[pltpu.VMEM((2, 2, hh, tn), BF16), dma((2,)),
                            dma((3, qp)), dma((3, qp)), dma((3, qp)), dma((3, qp)), dma, dma]
            + list(gather_rest.scratch)),
        out_shape=[jax.ShapeDtypeStruct((s, p), BF16), jax.ShapeDtypeStruct((N_CHIPS,) + wb.shape, wb.dtype)]
        + list(gather_rest.out_shape),
        compiler_params=_params(("arbitrary",) * 2, tm * d * 2 + tm * tn * 2, 3 * d * tn * 2 + tm * tn * 4),
    )(chip_idx, h, wb, *gather_rest.operands)
    return res[0], res[1], res[2:]


def _gather_packs(pack):
    def body(p_ref, o_ref, send, recv, loc):
        x, y, c = _place()
        me = 4 * x + 2 * y + c
        mine = pltpu.make_async_copy(p_ref, o_ref.at[me], loc)
        mine.start()
        flips = [(fx, fy, fc) for fx in (0, 1) for fy in (0, 1) for fc in (0, 1)][1:]
        cps = []
        for k, (fx, fy, fc) in enumerate(flips):
            peer = (1 - x if fx else x, 1 - y if fy else y, 1 - c if fc else c)
            cps.append(pltpu.make_async_remote_copy(
                src_ref=p_ref, dst_ref=o_ref.at[me], send_sem=send.at[k], recv_sem=recv.at[k],
                device_id=peer, device_id_type=MESH))
        for cp in cps:
            cp.start()
        for k, (fx, fy, fc) in enumerate(flips):
            px, py, pc = (1 - x if fx else x, 1 - y if fy else y, 1 - c if fc else c)
            pltpu.make_async_remote_copy(
                src_ref=p_ref, dst_ref=o_ref.at[4 * px + 2 * py + pc],
                send_sem=send.at[k], recv_sem=recv.at[k],
                device_id=(px, py, pc), device_id_type=MESH).wait_recv()
        for cp in cps:
            cp.wait_send()
        mine.wait()

    dma = pltpu.SemaphoreType.DMA
    return pl.pallas_call(
        body, name="gather_packs", out_shape=jax.ShapeDtypeStruct((N_DEV,) + pack.shape, pack.dtype),
        in_specs=[ANY], out_specs=ANY, scratch_shapes=[dma((7,)), dma((7,)), dma],
    )(pack)


def _flat_pack(pieces):
    flat = jnp.concatenate([p.reshape(-1) for p in pieces])
    pad = (-flat.shape[0]) % (8 * LANES)
    flat = jnp.concatenate([flat, jnp.zeros((pad,), F32)])
    return flat.reshape(-1, LANES)


def _unpack(pack, shapes):
    flat = pack.reshape(-1)
    out, off = [], 0
    for sh in shapes:
        n = 1
        for s in sh:
            n *= s
        out.append(flat[off:off + n].reshape(sh))
        off += n
    return out


def kernel(x, norm_w, w_in, pool_w, pool_scale, conv_w, conv_b, gate_b, w_branch, w_out, final_norm_w, loss_target, m_norm_w, m_w_in, m_pool_w, m_pool_scale, m_conv_w, m_conv_b, m_gate_b, m_w_branch, m_w_out, m_final_norm_w, v_norm_w, v_w_in, v_pool_w, v_pool_scale, v_conv_w, v_conv_b, v_gate_b, v_w_branch, v_w_out, v_final_norm_w):
    _, s, d = x.shape
    w = d // 2
    cg = w // N_POOL_GROUPS
    p = 6 * w + 2 * d
    pq = p // N_CHIPS
    dq = d // N_CHIPS
    assert w_in.shape == (1, d, pq) and w_branch.shape == (1, 2, w, dq) and w_out.shape == (1, dq, d)
    assert pool_w.shape == (1, N_POOL_GROUPS, cg // N_CHIPS, cg) and conv_w.shape == (1, CONV_K, cg)

    x2d = x.reshape(s, d)
    tgt = loss_target.reshape(s, d)
    c_idx = lax.axis_index("c").astype(jnp.int32).reshape(1)
    chip = 2 * lax.axis_index("x") + lax.axis_index("y")

    big_w = [w_in.reshape(d, pq), w_out.reshape(dq, d), w_branch.reshape(2 * w, dq),
             pool_w.reshape(cg, cg)]
    names = ["in", "out", "branch", "pool"]
    big_b = [_cast_bf16("cast_" + nm, a) for nm, a in zip(names, big_w)]
    big_b = [b.reshape(2, b.shape[0] // 2, b.shape[1]) for b in big_b]
    chip_idx = chip.astype(jnp.int32).reshape(1)

    h = _rms_fwd(x2d, norm_w)

    tm = _pick(s, 1024, 16)
    tn_p = _pick(math.gcd(pq, 6 * w, d), 1024, LANES)
    qp = pq // tn_p
    tk_d = _pick(d, 4096, LANES)
    proj, wg_in, (wg_pool, wg_cw, wg_gb, wg_br_near) = _proj_gathering(
        h, big_b[0], chip_idx,
        _join([_gather_weights(big_b[3:], [conv_w.reshape(CONV_K, cg), gate_b.reshape(2, dq)]),
               _gather_weights(big_b[2:3], [], relay=True, phase=1)]),
        tm=tm, tn=tn_p)
    wg_in = wg_in.reshape(N_CHIPS, d, pq)
    wg_pool = wg_pool.reshape(N_CHIPS, N_POOL_GROUPS, cg // N_CHIPS, cg)

    ys, (wg_br,) = _mix_fwd(proj, wg_pool, pool_scale, wg_cw, conv_b, w,
                            _gather_weights(big_b[2:3], [], relay=True, phase=2, prev=[wg_br_near]))

    tn_d = _pick(dq, 1024, LANES)
    qd = dq // tn_d
    tk_w = _pick(w, 2048, LANES)
    tm_g = _pick(s, 512, 16)
    gl0 = (6 * w) // tn_d
    gl1 = (6 * w + d) // tn_d

    def gate_specs(im, rows):
        return [pl.BlockSpec((rows, tn_d), lambda *a: (im(*a)[0], gl0 + im(*a)[1])),
                pl.BlockSpec((rows, tn_d), lambda *a: (im(*a)[0], gl1 + im(*a)[1])),
                pl.BlockSpec((None, 2, tn_d), lambda *a: (im(*a)[1] // qd, 0, im(*a)[1] % qd))]

    def merge_epilogue(accs, ins, outs, ids):
        gb = ins[6][...]
        g0 = _sigmoid(ins[4][...].astype(F32) + gb[0:1])
        g1 = _sigmoid(ins[5][...].astype(F32) + gb[1:2])
        outs[0][...] = (g0 * accs[0] + g1 * accs[1]).astype(BF16)
        outs[1][0] = accs[0].astype(BF16)
        outs[1][1] = accs[1].astype(BF16)

    (merged, br), (wg_out,) = _mm(
        "branch_merge", grid=(s // tm_g, d // tn_d, w // tk_w), comm=_gather_weights(big_b[1:2], []),
        operands=[ys, wg_br, ys, wg_br, proj, proj, wg_gb],
        in_specs=[pl.BlockSpec((None, tm_g, tk_w), lambda i, j, k: (0, i, k)),
                  pl.BlockSpec((None, None, tk_w, tn_d), lambda i, j, k: (j // qd, 0, k, j % qd)),
                  pl.BlockSpec((None, tm_g, tk_w), lambda i, j, k: (1, i, k)),
                  pl.BlockSpec((None, None, tk_w, tn_d), lambda i, j, k: (j // qd, 1, k, j % qd)),
                  *gate_specs(lambda i, j, k: (i, j), tm_g)],
        out_shape=[jax.ShapeDtypeStruct((s, d), BF16), jax.ShapeDtypeStruct((2, s, d), BF16)],
        out_specs=[pl.BlockSpec((tm_g, tn_d), lambda i, j, k: (i, j)),
                   pl.BlockSpec((2, tm_g, tn_d), lambda i, j, k: (0, i, j))],
        pairs=[(0, 1, 0, None), (2, 3, 1, None)], dims=NN, acc_shapes=[(tm_g, tn_d)] * 2,
        epilogue=merge_epilogue, temp_bytes=6 * tm_g * tn_d * 4,
    )
    wg_out = wg_out.reshape(d, d)

    tn_f = _pick(d, 1024, LANES)
    o = _mm(
        "out_proj", grid=(s // tm, d // tn_f, d // tk_d), operands=[merged, wg_out],
        in_specs=[pl.BlockSpec((tm, tk_d), lambda i, j, k: (i, k)),
                  pl.BlockSpec((tk_d, tn_f), lambda i, j, k: (k, j))],
        out_shape=[jax.ShapeDtypeStruct((s, d), F32)],
        out_specs=[pl.BlockSpec((tm, tn_f), lambda i, j, k: (i, j))],
        pairs=[(0, 1, 0, None)], dims=NN, acc_shapes=[(tm, tn_f)],
        epilogue=lambda accs, ins, outs, ids: outs[0].__setitem__(Ellipsis, accs[0]),
    )[0][0]

    dx2, dx2b, loss_part, g_fnw =_head(x2d, o, tgt, final_norm_w.reshape(1, d))

    def gate_bwd_epilogue(accs, ins, outs, ids):
        dm = accs[0]
        gb = ins[5][...]
        i = ids[1]

        @pl.when(i == 0)
        def _():
            outs[2][...] = jnp.zeros(outs[2].shape, F32)

        for n in range(2):
            gate = _sigmoid(ins[3 + n][...].astype(F32) + gb[n:n + 1])
            outs[0][n] = (dm * gate).astype(BF16)
            dgl = dm * ins[2][n].astype(F32) * gate * (1.0 - gate)
            outs[1][n] = dgl.astype(BF16)
            outs[2][n:n + 1, :] += jnp.sum(dgl, axis=0, keepdims=True)

    tm_b = _pick(s, 512, 16)
    (d_br, dgl, g_gb), _ = _mm(
        "out_proj_bwd_gate", grid=(d // tn_d, s // tm_b, d // tk_d),
        operands=[dx2b, wg_out, br, proj, proj, wg_gb],
        in_specs=[pl.BlockSpec((tm_b, tk_d), lambda j, i, k: (i, k)),
                  pl.BlockSpec((tn_d, tk_d), lambda j, i, k: (j, k)),
                  pl.BlockSpec((2, tm_b, tn_d), lambda j, i, k: (0, i, j)),
                  *gate_specs(lambda j, i, k: (i, j), tm_b)],
        out_shape=[jax.ShapeDtypeStruct((2, s, d), BF16), jax.ShapeDtypeStruct((2, s, d), BF16),
                   jax.ShapeDtypeStruct((2, d), F32)],
        out_specs=[pl.BlockSpec((2, tm_b, tn_d), lambda j, i, k: (0, i, j)),
                   pl.BlockSpec((2, tm_b, tn_d), lambda j, i, k: (0, i, j)),
                   pl.BlockSpec((2, tn_d), lambda j, i, k: (0, j))],
        pairs=[(0, 1, 0, None)], dims=NT, acc_shapes=[(tm_b, tn_d)],
        epilogue=gate_bwd_epilogue, semantics=("parallel", "arbitrary", "arbitrary"),
        temp_bytes=8 * tm_b * tn_d * 4,
    )

    hh_out = d // 8
    tm_o = _pick(hh_out, 512, LANES)
    nb_o = hh_out // tm_o
    tk_s = _pick(s, 4096, LANES)
    g_out = _mm(
        "w_out_grad", grid=(d // tm_o, d // tn_f, s // tk_s), operands=[merged, dx2b],
        in_specs=[pl.BlockSpec((tk_s, tm_o), lambda i, j, k: (k, i)),
                  pl.BlockSpec((tk_s, tn_f), lambda i, j, k: (k, j))],
        out_shape=[jax.ShapeDtypeStruct((2, N_CHIPS, hh_out, d), BF16)],
        out_specs=[pl.BlockSpec((None, None, tm_o, tn_f),
                                lambda i, j, k: ((i // nb_o) % 2, i // (2 * nb_o), i % nb_o, j))],
        pairs=[(0, 1, 0, None)], dims=TN, acc_shapes=[(tm_o, tn_f)],
        epilogue=lambda accs, ins, outs, ids: outs[0].__setitem__(Ellipsis, accs[0].astype(BF16)),
    )[0][0]

    tn_w = _pick(w, 2048, LANES)
    tk_q = _pick(dq, 1024, LANES)
    qk = dq // tk_q
    (dys,), from_sibling_out = _mm(
        "branch_bwd", grid=(2, s // tm, w // tn_w, d // tk_q), operands=[d_br, wg_br],
        comm=_to_sibling([g_out], other_half=True),
        in_specs=[pl.BlockSpec((None, tm, tk_q), lambda n, i, j, k: (n, i, k)),
                  pl.BlockSpec((None, None, tn_w, tk_q), lambda n, i, j, k: (k // qk, n, j, k % qk))],
        out_shape=[jax.ShapeDtypeStruct((2, s, w), F32)],
        out_specs=[pl.BlockSpec((None, tm, tn_w), lambda n, i, j, k: (n, i, j))],
        pairs=[(0, 1, 0, None)], dims=NT, acc_shapes=[(tm, tn_w)],
        epilogue=lambda accs, ins, outs, ids: outs[0].__setitem__(Ellipsis, accs[0]),
    )

    tm_w = _pick(w, 1024, LANES)
    g_br = _mm(
        "w_branch_grad", grid=(2, w // tm_w, d // tn_d, s // tk_s), operands=[ys, d_br],
        in_specs=[pl.BlockSpec((None, tk_s, tm_w), lambda n, i, j, k: (n, k, i)),
                  pl.BlockSpec((None, tk_s, tn_d), lambda n, i, j, k: (n, k, j))],
        out_shape=[jax.ShapeDtypeStruct((2, N_CHIPS, w, dq), BF16)],
        out_specs=[pl.BlockSpec((None, None, tm_w, tn_d), lambda n, i, j, k: (n, j // qd, i, j % qd))],
        pairs=[(0, 1, 0, None)], dims=TN, acc_shapes=[(tm_w, tn_d)],
        epilogue=lambda accs, ins, outs, ids: outs[0].__setitem__(Ellipsis, accs[0].astype(BF16)),
    )[0][0]

    dpa, g_pool, g_ps, g_cw, g_cb = _mix_bwd(proj, dys, wg_pool, pool_scale, wg_cw, conv_b, w)
    g_pool = g_pool.astype(BF16).reshape(2, N_CHIPS, 2 * (cg // N_CHIPS), cg)

    early = [g_out, g_br, g_pool]
    from_sibling_early = list(from_sibling_out) + _run_exchange(
        "pair_exchange", _to_sibling(early[1:], other_half=True))
    parts_early = []
    for nm, g, r1 in zip(names[1:], early, from_sibling_early):
        _, nc, hh, wd = g.shape
        pt = _pair_add("pair_add_" + nm, g.reshape(2, nc * hh, wd), r1.reshape(nc * hh, wd), c_idx)
        parts_early.append(pt.reshape(nc, hh, wd))

    na = (6 * w) // tn_p
    qg = d // tn_p
    hh_in = d // 2
    assert s == tk_s
    tm_i = _pick(hh_in, 1024, LANES)
    tn_i = _pick(tn_p, 512, LANES)
    from_sibling, recv_early = _w_in_grad_sibling(
        h, dpa, dgl, c_idx, _scatter_partials(parts_early), tm=tm_i, tn=tn_i, rows=hh_in, pq=pq)

    tm_o2 = _pick(hh_in, 512, LANES)
    nb_o2 = hh_in // tm_o2

    def own_rows(name, first, count, comm):
        def add_sibling(accs, ins, outs, ids):
            outs[0][...] = (accs[0] + ins[3][...].astype(F32)).astype(BF16)

        return _mm(
            name, grid=(p // tn_p, count, 1), operands=[h, dpa, dgl, from_sibling], comm=comm, prefetch=c_idx,
            in_specs=[pl.BlockSpec((tk_s, tm_o2), lambda j, i, k, cr: (0, cr[0] * nb_o2 + first + i)),
                      pl.BlockSpec((tk_s, tn_p), lambda j, i, k, cr: (0, jnp.minimum(j, na - 1))),
                      pl.BlockSpec((None, tk_s, tn_p),
                                   lambda j, i, k, cr: (jnp.maximum(j - na, 0) // qg, 0, jnp.maximum(j - na, 0) % qg)),
                      pl.BlockSpec((None, tm_o2, tn_p), lambda j, i, k, cr: (j // qp, first + i, j % qp))],
            out_shape=[jax.ShapeDtypeStruct((N_CHIPS, count * tm_o2, pq), BF16)],
            out_specs=[pl.BlockSpec((None, tm_o2, tn_p), lambda j, i, k, cr: (j // qp, i, j % qp))],
            pairs=[(0, 1, 0, lambda ids: ids[0] < na), (0, 2, 0, lambda ids: ids[0] >= na)],
            dims=TN, acc_shapes=[(tm_o2, tn_p)], epilogue=add_sibling)

    halves_early = [_sum_chips("sum_chips_" + nm, pt, r2, chip_idx)
                    for nm, pt, r2 in zip(names[1:], parts_early, recv_early)]
    share_early = _to_sibling(halves_early, other_half=False)

    n_first = 1 if nb_o2 > 1 else 0
    if n_first:
        (part_a,), others_early = own_rows("w_in_grad_own_a", 0, n_first, share_early)
        scatter_a = _scatter_partials([part_a], rows=hh_in)
        (part_b,), land_a = own_rows("w_in_grad_own_b", n_first, nb_o2 - n_first, scatter_a)
        scatter_b = _scatter_partials([part_b], rows=hh_in, row0=n_first * tm_o2, prev=land_a)
    else:
        (part_b,), others_early = own_rows("w_in_grad_own", 0, nb_o2, share_early)
        scatter_b = _scatter_partials([part_b])

    tn_h = _pick(d, 2048, LANES)
    (dh,), recv_in = _mm(
        "proj_bwd", grid=(s // tm, d // tn_h, p // tn_p), operands=[dpa, dgl, wg_in],
        comm=scatter_b,
        in_specs=[pl.BlockSpec((tm, tn_p), lambda i, j, k: (i, jnp.minimum(k, na - 1))),
                  pl.BlockSpec((None, tm, tn_p),
                               lambda i, j, k: (jnp.maximum(k - na, 0) // qg, i, jnp.maximum(k - na, 0) % qg)),
                  pl.BlockSpec((None, tn_h, tn_p), lambda i, j, k: (k // qp, j, k % qp))],
        out_shape=[jax.ShapeDtypeStruct((s, d), F32)],
        out_specs=[pl.BlockSpec((tm, tn_h), lambda i, j, k: (i, j))],
        pairs=[(0, 2, 0, lambda ids: ids[2] < na), (1, 2, 0, lambda ids: ids[2] >= na)],
        dims=NT, acc_shapes=[(tm, tn_h)],
        epilogue=lambda accs, ins, outs, ids: outs[0].__setitem__(Ellipsis, accs[0]),
    )

    grad_x, g_nw = _rms_bwd(x2d, dh, dx2, norm_w)

    if n_first:
        half_in = _sum_chips("sum_chips_in_a", part_a, recv_in[0], chip_idx)
        half_in = _sum_chips("sum_chips_in_b", part_b, recv_in[0], chip_idx, row0=n_first * tm_o2, prev=half_in)
    else:
        half_in = _sum_chips("sum_chips_in", part_b, recv_in[0], chip_idx)
    halves = [half_in] + halves_early
    others = _run_exchange("pair_share", _to_sibling([half_in], other_half=False)) + list(others_early)

    g_cw_full = jnp.transpose(g_cw, (1, 0, 2)).reshape(CONV_K, w)
    small_shapes = [(LANES,), (1, d), (1, d), (1, w), (1, w), (CONV_K, w), (2, d)]
    pack = _flat_pack([loss_part[0], g_nw, g_fnw, g_ps, g_cb, g_cw_full, g_gb])
    total = _sum_devices(_gather_packs(pack))
    t_loss, t_nw, t_fnw, t_ps, t_cb, t_cw, t_gb = _unpack(total, small_shapes)
    loss = t_loss[0]
    t_cw = lax.dynamic_slice_in_dim(t_cw, chip * cg, cg, axis=1)
    t_gb = lax.dynamic_slice_in_dim(t_gb, chip * dq, dq, axis=1)

    out = {}
    big_names = ["w_in", "w_out", "w_branch", "pool_w"]
    big_m = [m_w_in, m_w_out, m_w_branch, m_pool_w]
    big_v = [v_w_in, v_w_out, v_w_branch, v_pool_w]
    big_orig = [w_in, w_out, w_branch, pool_w]
    for nm, g_own, g_other, w2, mm_, vv_, orig in zip(big_names, halves, others, big_w, big_m, big_v, big_orig):
        sh = (2,) + g_own.shape
        res = _adamw_halves("adamw_" + nm, g_own, g_other, c_idx,
                            w2.reshape(sh), mm_.reshape(sh), vv_.reshape(sh))
        out[nm] = [r.reshape(orig.shape) for r in res]

    sm_names = ["norm_w", "final_norm_w", "pool_scale", "conv_b", "conv_w", "gate_b"]
    sm_g = [t_nw, t_fnw, t_ps, t_cb, t_cw, t_gb]
    sm_w = [norm_w, final_norm_w, pool_scale, conv_b, conv_w, gate_b]
    sm_m = [m_norm_w, m_final_norm_w, m_pool_scale, m_conv_b, m_conv_w, m_gate_b]
    sm_v = [v_norm_w, v_final_norm_w, v_pool_scale, v_conv_b, v_conv_w, v_gate_b]
    sm_shapes = [a.shape for a in sm_w]
    res = _adamw("adamw_small", _flat_pack(sm_g), _flat_pack(sm_w), _flat_pack(sm_m), _flat_pack(sm_v))
    res = [_unpack(r, sm_shapes) for r in res]
    for idx, nm in enumerate(sm_names):
        out[nm] = [r[idx] for r in res]

    order = ["norm_w", "w_in", "pool_w", "pool_scale", "conv_w", "conv_b", "gate_b", "w_branch", "w_out",
             "final_norm_w"]
    outs = [loss, grad_x.reshape(x.shape)]
    for kind in range(4):
        outs += [out[nm][kind] for nm in order]
    return tuple(outs)
```

```python
import math

import jax
import jax.numpy as jnp
from jax import lax
from jax.experimental import pallas as pl
from jax.experimental.pallas import tpu as pltpu

F32 = jnp.float32
BF16 = jnp.bfloat16

NORM_EPS = 1e-6
POOL_WINDOWS = (2, 4, 8, 16)
N_POOL_GROUPS = len(POOL_WINDOWS)
CONV_K = 3
ADAM_LR = 0.001
ADAM_B1 = 0.9
ADAM_B2 = 0.999
ADAM_EPS = 1e-08
ADAM_WD = 0.01
ADAM_STEP = 10

N_CHIPS = 4
N_DEV = 8
HALO = 16
LANES = 128
V7X_VMEM_BYTES = 64 * 1024 * 1024
VMEM_CAP = V7X_VMEM_BYTES - 8 * 1024 * 1024

MESH = pl.DeviceIdType.MESH
ANY = pl.BlockSpec(memory_space=pl.ANY)

NN = (((1,), (0,)), ((), ()))
NT = (((1,), (1,)), ((), ()))
TN = (((0,), (0,)), ((), ()))


def _pick(dim, pref, align):
    if dim <= pref:
        return dim
    t = (pref // align) * align
    while t >= align:
        if dim % t == 0:
            return t
        t -= align
    raise ValueError(f"no tile for {dim} (pref {pref}, align {align})")


def _nbytes(shape, dtype):
    n = 1
    for s in shape:
        if s is not None:
            n *= s
    return n * jnp.dtype(dtype).itemsize


def _params(semantics, block_bytes, extra_bytes=0):
    need = 2 * block_bytes + extra_bytes + (2 << 20)
    return pltpu.CompilerParams(dimension_semantics=semantics,
                                vmem_limit_bytes=int(min(max(need, 16 << 20), VMEM_CAP)))


def _sigmoid(z):
    return jax.nn.sigmoid(z)


def _mm(name, *, grid, operands, in_specs, out_shape, out_specs, pairs, dims, acc_shapes, epilogue,
        semantics=None, temp_bytes=0, comm=None, prefetch=None, aliases=None):
    aliases = dict(aliases or {})
    n_in, n_out = len(operands), len(out_shape)
    kax = len(grid) - 1
    nk = grid[kax]
    single = nk == 1
    conditional = any(p[3] is not None for p in pairs)
    if single and conditional:
        assert len(acc_shapes) == 1 and all(p[3] is not None for p in pairs)
    n_acc = 0 if single else len(acc_shapes)
    n_pf = 0 if prefetch is None else 1
    c_ops = list(comm.operands) if comm else []
    c_out = list(comm.out_shape) if comm else []
    c_sems = list(comm.scratch) if comm else []
    c_alias = dict(comm.aliases) if comm else {}
    n_ci, n_co = len(c_ops), len(c_out)

    def product(ins, ai, bi):
        return lax.dot_general(ins[ai][...], ins[bi][...], dims, preferred_element_type=F32)

    def body(*refs):
        refs = refs[n_pf:]
        ins = refs[:n_in]
        c_in_refs = refs[n_in:n_in + n_ci]
        outs = refs[n_in + n_ci:n_in + n_ci + n_out]
        c_out_refs = refs[n_in + n_ci + n_out:n_in + n_ci + n_out + n_co]
        scratch = refs[n_in + n_ci + n_out + n_co:]
        accs = scratch[:n_acc]
        sems = scratch[n_acc:]
        ids = [pl.program_id(a) for a in range(len(grid))]
        k = ids[kax]

        if comm:
            is_first = ids[0] == 0
            is_last = ids[0] == grid[0] - 1
            for a in range(1, len(grid)):
                is_first = jnp.logical_and(is_first, ids[a] == 0)
                is_last = jnp.logical_and(is_last, ids[a] == grid[a] - 1)

            @pl.when(is_first)
            def _():
                comm.start(c_in_refs, c_out_refs, sems)

        if single and conditional:
            for (ai, bi, ci, cond) in pairs:
                def only(ai=ai, bi=bi):
                    epilogue([product(ins, ai, bi)], ins, outs, ids)
                pl.when(cond(ids))(only)
        elif single:
            vals = [None] * len(acc_shapes)
            for (ai, bi, ci, cond) in pairs:
                r = product(ins, ai, bi)
                vals[ci] = r if vals[ci] is None else vals[ci] + r
            epilogue(vals, ins, outs, ids)
        else:
            @pl.when(k == 0)
            def _():
                for a in accs:
                    a[...] = jnp.zeros(a.shape, a.dtype)

            for (ai, bi, ci, cond) in pairs:
                def step(ai=ai, bi=bi, ci=ci):
                    accs[ci][...] += product(ins, ai, bi)
                if cond is None:
                    step()
                else:
                    pl.when(cond(ids))(step)

            @pl.when(k == nk - 1)
            def _():
                epilogue([a[...] for a in accs], ins, outs, ids)

        if comm:
            @pl.when(is_last)
            def _():
                comm.finish(c_in_refs, c_out_refs, sems)

    if semantics is None:
        semantics = ("parallel",) * kax + ("arbitrary",)
    if comm:
        semantics = ("arbitrary",) * len(grid)
    in_specs = [ANY if idx in aliases else spec for idx, spec in enumerate(in_specs)]
    blk = 0
    for idx, (spec, op) in enumerate(zip(in_specs, operands)):
        if idx not in aliases:
            blk += _nbytes(spec.block_shape, op.dtype)
    for spec, o in zip(out_specs, out_shape):
        blk += _nbytes(spec.block_shape, o.dtype)
    acc_bytes = sum(_nbytes(s, F32) for s in acc_shapes)
    io_alias = {n_pf + i: o for i, o in aliases.items()}
    io_alias.update({n_pf + n_in + i: n_out + o for i, o in c_alias.items()})
    all_in = list(in_specs) + [ANY] * n_ci
    all_out = list(out_specs) + [ANY] * n_co
    scratch_shapes = [pltpu.VMEM(s, F32) for s in acc_shapes[:n_acc]] + c_sems
    params = _params(semantics, blk, 3 * acc_bytes + temp_bytes)
    shapes = list(out_shape) + c_out
    if prefetch is None:
        call = pl.pallas_call(
            body, name=name, grid=grid, in_specs=all_in, out_specs=all_out, out_shape=shapes,
            scratch_shapes=scratch_shapes, input_output_aliases=io_alias, compiler_params=params)
        res = call(*operands, *c_ops)
    else:
        call = pl.pallas_call(
            body, name=name, out_shape=shapes, input_output_aliases=io_alias, compiler_params=params,
            grid_spec=pltpu.PrefetchScalarGridSpec(
                num_scalar_prefetch=1, grid=grid, in_specs=all_in, out_specs=all_out,
                scratch_shapes=scratch_shapes))
        res = call(prefetch, *operands, *c_ops)
    return res[:n_out], res[n_out:]


def _cast_bf16(name, w2d):
    r, c = w2d.shape
    tr = _pick(r, max(16, (4 << 20) // (4 * c)), 16)

    def body(w_ref, o_ref):
        o_ref[...] = w_ref[...].astype(BF16)

    return pl.pallas_call(
        body, name=name, grid=(r // tr,),
        in_specs=[pl.BlockSpec((tr, c), lambda i: (i, 0))],
        out_specs=pl.BlockSpec((tr, c), lambda i: (i, 0)),
        out_shape=jax.ShapeDtypeStruct((r, c), BF16),
        compiler_params=_params(("parallel",), tr * c * 6),
    )(w2d)


def _rms_fwd(x, norm_w):
    s, d = x.shape
    ts = _pick(s, 256, 16)

    def body(x_ref, w_ref, h_ref):
        xv = x_ref[...]
        rstd = lax.rsqrt(jnp.mean(xv * xv, axis=-1, keepdims=True) + NORM_EPS)
        h_ref[...] = (xv * rstd * w_ref[...]).astype(BF16)

    return pl.pallas_call(
        body, name="rms_fwd", grid=(s // ts,),
        in_specs=[pl.BlockSpec((ts, d), lambda i: (i, 0)), pl.BlockSpec((1, d), lambda i: (0, 0))],
        out_specs=pl.BlockSpec((ts, d), lambda i: (i, 0)),
        out_shape=jax.ShapeDtypeStruct((s, d), BF16),
        compiler_params=_params(("parallel",), ts * d * 6, 3 * ts * d * 4),
    )(x, norm_w)


def _head(x, o, target, fnw):
    s, d = x.shape
    ts = _pick(s, 128, 16)

    def body(x_ref, o_ref, t_ref, w_ref, dx_ref, dxb_ref, loss_ref, gw_ref):
        i = pl.program_id(0)

        @pl.when(i == 0)
        def _():
            loss_ref[...] = jnp.zeros(loss_ref.shape, F32)
            gw_ref[...] = jnp.zeros(gw_ref.shape, F32)

        w = w_ref[...]
        x2 = x_ref[...] + o_ref[...]
        rstd = lax.rsqrt(jnp.mean(x2 * x2, axis=-1, keepdims=True) + NORM_EPS)
        n = x2 * rstd
        e = n * w - t_ref[...]
        loss_ref[...] += 0.5 * jnp.sum(e * e) / d
        dy = e / d
        gw_ref[...] += jnp.sum(dy * n, axis=0, keepdims=True)
        gy = dy * w
        dx = rstd * (gy - n * jnp.mean(gy * n, axis=-1, keepdims=True))
        dx_ref[...] = dx
        dxb_ref[...] = dx.astype(BF16)

    row = pl.BlockSpec((ts, d), lambda i: (i, 0))
    return pl.pallas_call(
        body, name="head", grid=(s // ts,),
        in_specs=[row, row, row, pl.BlockSpec((1, d), lambda i: (0, 0))],
        out_specs=[row, row, pl.BlockSpec((8, LANES), lambda i: (0, 0)),
                   pl.BlockSpec((1, d), lambda i: (0, 0))],
        out_shape=[jax.ShapeDtypeStruct((s, d), F32), jax.ShapeDtypeStruct((s, d), BF16),
                   jax.ShapeDtypeStruct((8, LANES), F32), jax.ShapeDtypeStruct((1, d), F32)],
        compiler_params=_params(("arbitrary",), ts * d * 22, 6 * ts * d * 4),
    )(x, o, target, fnw)


def _rms_bwd(x, dh, dx2, norm_w, comm):
    s, d = x.shape
    ts = _pick(s, 128, 16)
    n_ci, n_co = len(comm.operands), len(comm.out_shape)
    n_steps = s // ts

    def body(x_ref, dh_ref, dx2_ref, w_ref, *rest):
        c_in = rest[:n_ci]
        gx_ref, gw_ref = rest[n_ci], rest[n_ci + 1]
        c_out, sems = rest[n_ci + 2:n_ci + 2 + n_co], rest[n_ci + 2 + n_co:]
        i = pl.program_id(0)

        @pl.when(i == 0)
        def _():
            gw_ref[...] = jnp.zeros(gw_ref.shape, F32)
            comm.start(c_in, c_out, sems)

        @pl.when(i == n_steps - 1)
        def _():
            comm.finish(c_in, c_out, sems)

        xv = x_ref[...]
        rstd = lax.rsqrt(jnp.mean(xv * xv, axis=-1, keepdims=True) + NORM_EPS)
        n = xv * rstd
        dhv = dh_ref[...]
        gw_ref[...] += jnp.sum(dhv * n, axis=0, keepdims=True)
        gh = dhv * w_ref[...]
        gx_ref[...] = dx2_ref[...] + rstd * (gh - n * jnp.mean(gh * n, axis=-1, keepdims=True))

    row = pl.BlockSpec((ts, d), lambda i: (i, 0))
    vec = pl.BlockSpec((1, d), lambda i: (0, 0))
    res = pl.pallas_call(
        body, name="rms_bwd", grid=(n_steps,),
        in_specs=[row, row, row, vec] + [ANY] * n_ci, out_specs=[row, vec] + [ANY] * n_co,
        out_shape=[jax.ShapeDtypeStruct((s, d), F32), jax.ShapeDtypeStruct((1, d), F32)] + list(comm.out_shape),
        scratch_shapes=list(comm.scratch),
        compiler_params=_params(("arbitrary",), ts * d * 16, 5 * ts * d * 4),
    )(x, dh, dx2, norm_w, *comm.operands)
    return res[0], res[1], res[2:]


def _silu(z):
    return z * _sigmoid(z)


def _window_sum(ext, window, back):
    n = ext.shape[0]
    acc = ext
    step = 1
    while step < window:
        acc = acc + pltpu.roll(acc, step if back else n - step, 0)
        step *= 2
    return acc


def _shift_rows(ext, k, back):
    n = ext.shape[0]
    return pltpu.roll(ext, k if back else n - k, 0)


def _mix_fwd(proj, pool_wg, pool_scale, conv_wg, conv_b, width, comm):
    s = proj.shape[0]
    w = width
    cg = w // N_POOL_GROUPS
    ts = _pick(s, 128, HALO)
    hb = ts // HALO
    cols = 6 * w

    n_ci, n_co = len(comm.operands), len(comm.out_shape)
    n_steps = s // ts

    def body(p_ref, ph_ref, pw_ref, ps_ref, cw_ref, cb_ref, *rest):
        c_in, ys_ref = rest[:n_ci], rest[n_ci]
        c_out, sems = rest[n_ci + 1:n_ci + 1 + n_co], rest[n_ci + 1 + n_co:]
        i = pl.program_id(0)
        first = i == 0
        t1 = (i * ts + lax.broadcasted_iota(jnp.int32, (ts, 1), 0) + 1).astype(F32)

        @pl.when(first)
        def _():
            comm.start(c_in, c_out, sems)

        def tile(part, g):
            lo = part * w + g * cg
            return p_ref[:, lo:lo + cg].astype(F32)

        def prev(part, g):
            lo = part * w + g * cg
            return jnp.where(first, 0.0, ph_ref[:, lo:lo + cg].astype(F32))

        for g, win in enumerate(POOL_WINDOWS):
            gs = slice(g * cg, (g + 1) * cg)
            u = tile(0, g)
            ext = jnp.concatenate([prev(0, g), u], axis=0)
            wsum = _window_sum(ext, win, True)[HALO:]
            pooled = wsum / jnp.minimum(t1, float(win)) - u
            pw = pw_ref[:, g].reshape(cg, cg)
            mixed = jnp.dot(pooled.astype(BF16), pw, preferred_element_type=F32)
            ys_ref[0, :, gs] = (mixed * ps_ref[:, gs] * _silu(tile(1, g))).astype(BF16)
            v = tile(4, g) * tile(2, g)
            vext = jnp.concatenate([prev(4, g) * prev(2, g), v], axis=0)
            v1 = _shift_rows(vext, 1, True)[HALO:]
            v2 = _shift_rows(vext, 2, True)[HALO:]
            cw = [cw_ref[g, tap:tap + 1, :] for tap in range(CONV_K)]
            y = cb_ref[:, gs] + cw[0] * v2 + cw[1] * v1 + cw[2] * v
            ys_ref[1, :, gs] = (tile(3, g) * y * _silu(tile(5, g))).astype(BF16)

        @pl.when(i == n_steps - 1)
        def _():
            comm.finish(c_in, c_out, sems)

    res = pl.pallas_call(
        body, name="mix_fwd", grid=(n_steps,),
        in_specs=[pl.BlockSpec((ts, cols), lambda i: (i, 0)),
                  pl.BlockSpec((HALO, cols), lambda i: (jnp.maximum(i * hb - 1, 0), 0)),
                  pl.BlockSpec(pool_wg.shape, lambda i: (0, 0, 0, 0)),
                  pl.BlockSpec((1, w), lambda i: (0, 0)),
                  pl.BlockSpec(conv_wg.shape, lambda i: (0, 0, 0)),
                  pl.BlockSpec((1, w), lambda i: (0, 0))] + [ANY] * n_ci,
        out_specs=[pl.BlockSpec((2, ts, w), lambda i: (0, i, 0))] + [ANY] * n_co,
        out_shape=[jax.ShapeDtypeStruct((2, s, w), BF16)] + list(comm.out_shape),
        scratch_shapes=list(comm.scratch),
        input_output_aliases={6 + i: 1 + o for i, o in comm.aliases.items()},
        compiler_params=_params(("arbitrary",), (ts + HALO) * cols * 2 + 2 * ts * w * 2
                                + _nbytes(pool_wg.shape, BF16), 24 * (ts + HALO) * cg * 4),
    )(proj, proj, pool_wg, pool_scale, conv_wg, conv_b, *comm.operands)
    return res[0], res[1:]


def _mix_bwd(proj, dys, pool_wg, pool_scale, conv_wg, conv_b, width):
    s = proj.shape[0]
    w = width
    cg = w // N_POOL_GROUPS
    ts = _pick(s, 128, HALO)
    hb = ts // HALO
    n_tiles = s // ts
    last_hb = s // HALO - 1
    cols = 6 * w

    def body(p_ref, ph_ref, pn_ref, dy_ref, dyn_ref, pw_ref, ps_ref, cw_ref, cb_ref,
             dp_ref, dpw_ref, dps_ref, dcw_ref, dcb_ref):
        i = pl.program_id(0)
        first = i == 0
        last = i == n_tiles - 1

        @pl.when(first)
        def _():
            dpw_ref[...] = jnp.zeros(dpw_ref.shape, F32)
            dps_ref[...] = jnp.zeros(dps_ref.shape, F32)
            dcw_ref[...] = jnp.zeros(dcw_ref.shape, F32)
            dcb_ref[...] = jnp.zeros(dcb_ref.shape, F32)

        row = i * ts + lax.broadcasted_iota(jnp.int32, (ts + HALO, 1), 0)
        t1_ext = (row + 1).astype(F32)
        t1 = t1_ext[:ts]

        def tile(part, g):
            lo = part * w + g * cg
            return p_ref[:, lo:lo + cg].astype(F32)

        def prev(part, g):
            lo = part * w + g * cg
            return jnp.where(first, 0.0, ph_ref[:, lo:lo + cg].astype(F32))

        def ahead(part, g):
            lo = part * w + g * cg
            return jnp.concatenate([tile(part, g), pn_ref[:, lo:lo + cg].astype(F32)], axis=0)

        def dy_ahead(n, g):
            gs = slice(g * cg, (g + 1) * cg)
            nxt = jnp.where(last, 0.0, dyn_ref[n, :, gs])
            return jnp.concatenate([dy_ref[n, :, gs], nxt], axis=0)

        for g, win in enumerate(POOL_WINDOWS):
            gs = slice(g * cg, (g + 1) * cg)
            u = tile(0, g)
            ext = jnp.concatenate([prev(0, g), u], axis=0)
            pooled = _window_sum(ext, win, True)[HALO:] / jnp.minimum(t1, float(win)) - u
            pooled_b = pooled.astype(BF16)
            pw = pw_ref[:, g].reshape(cg, cg)
            mixed = jnp.dot(pooled_b, pw, preferred_element_type=F32)
            zp_ext = ahead(1, g)
            dy0_ext = dy_ahead(0, g)
            scale = ps_ref[:, gs]
            sp_ext = _sigmoid(zp_ext)
            dms_ext = dy0_ext * (zp_ext * sp_ext)
            dmix_b = (dms_ext * scale).astype(BF16)
            dpooled_ext = lax.dot_general(dmix_b, pw, NT, preferred_element_type=F32)
            dy0 = dy0_ext[:ts]
            zp, sp = zp_ext[:ts], sp_ext[:ts]
            dsilu_p = sp * (1.0 + zp * (1.0 - sp))
            dp_ref[:, w + g * cg:w + (g + 1) * cg] = (dy0 * mixed * scale * dsilu_p).astype(BF16)
            dps_ref[:, gs] += jnp.sum(dms_ext[:ts] * mixed, axis=0, keepdims=True)
            dpw = lax.dot_general(pooled_b, dmix_b[:ts], TN, preferred_element_type=F32)
            dpw_ref[g // 2, :, g % 2] += dpw.reshape(N_CHIPS, cg // N_CHIPS, cg)
            q_ext = dpooled_ext / jnp.minimum(t1_ext, float(win))
            du = _window_sum(q_ext, win, False)[:ts] - dpooled_ext[:ts]
            dp_ref[:, gs] = du.astype(BF16)
            uc = tile(2, g)
            cc = tile(4, g)
            v = cc * uc
            vext = jnp.concatenate([prev(4, g) * prev(2, g), v], axis=0)
            v1 = _shift_rows(vext, 1, True)[HALO:]
            v2 = _shift_rows(vext, 2, True)[HALO:]
            cw = [cw_ref[g, tap:tap + 1, :] for tap in range(CONV_K)]
            y = cb_ref[:, gs] + cw[0] * v2 + cw[1] * v1 + cw[2] * v
            bc_ext = ahead(3, g)
            zc_ext = ahead(5, g)
            dy1_ext = dy_ahead(1, g)
            sc_ext = _sigmoid(zc_ext)
            silu_c_ext = zc_ext * sc_ext
            dyy_ext = dy1_ext * bc_ext * silu_c_ext
            dy1 = dy1_ext[:ts]
            bc = bc_ext[:ts]
            zc, sc = zc_ext[:ts], sc_ext[:ts]
            dsilu_c = sc * (1.0 + zc * (1.0 - sc))
            dp_ref[:, 3 * w + g * cg:3 * w + (g + 1) * cg] = (dy1 * y * silu_c_ext[:ts]).astype(BF16)
            dp_ref[:, 5 * w + g * cg:5 * w + (g + 1) * cg] = (dy1 * bc * y * dsilu_c).astype(BF16)
            dyy = dyy_ext[:ts]
            dcb_ref[:, gs] += jnp.sum(dyy, axis=0, keepdims=True)
            for tap, vt in enumerate((v2, v1, v)):
                dcw_ref[g, tap:tap + 1, :] += jnp.sum(dyy * vt, axis=0, keepdims=True)
            dv = (cw[2] * dyy + cw[1] * _shift_rows(dyy_ext, 1, False)[:ts]
                  + cw[0] * _shift_rows(dyy_ext, 2, False)[:ts])
            dp_ref[:, 4 * w + g * cg:4 * w + (g + 1) * cg] = (dv * uc).astype(BF16)
            dp_ref[:, 2 * w + g * cg:2 * w + (g + 1) * cg] = (dv * cc).astype(BF16)

    dpw_shape = (2, N_CHIPS, 2, cg // N_CHIPS, cg)
    return pl.pallas_call(
        body, name="mix_bwd", grid=(n_tiles,),
        in_specs=[pl.BlockSpec((ts, cols), lambda i: (i, 0)),
                  pl.BlockSpec((HALO, cols), lambda i: (jnp.maximum(i * hb - 1, 0), 0)),
                  pl.BlockSpec((HALO, cols), lambda i: (jnp.minimum((i + 1) * hb, last_hb), 0)),
                  pl.BlockSpec((2, ts, w), lambda i: (0, i, 0)),
                  pl.BlockSpec((2, HALO, w), lambda i: (0, jnp.minimum((i + 1) * hb, last_hb), 0)),
                  pl.BlockSpec(pool_wg.shape, lambda i: (0, 0, 0, 0)),
                  pl.BlockSpec((1, w), lambda i: (0, 0)),
                  pl.BlockSpec(conv_wg.shape, lambda i: (0, 0, 0)),
                  pl.BlockSpec((1, w), lambda i: (0, 0))],
        out_specs=[pl.BlockSpec((ts, cols), lambda i: (i, 0)),
                   pl.BlockSpec(dpw_shape, lambda i: (0, 0, 0, 0, 0)),
                   pl.BlockSpec((1, w), lambda i: (0, 0)),
                   pl.BlockSpec(conv_wg.shape, lambda i: (0, 0, 0)),
                   pl.BlockSpec((1, w), lambda i: (0, 0))],
        out_shape=[jax.ShapeDtypeStruct((s, cols), BF16), jax.ShapeDtypeStruct(dpw_shape, F32),
                   jax.ShapeDtypeStruct((1, w), F32), jax.ShapeDtypeStruct(conv_wg.shape, F32),
                   jax.ShapeDtypeStruct((1, w), F32)],
        compiler_params=_params(("arbitrary",), (2 * ts + 2 * HALO) * cols * 2 + (ts + HALO) * w * 8
                                + _nbytes(pool_wg.shape, BF16) + _nbytes(dpw_shape, F32),
                                40 * (ts + HALO) * cg * 4),
    )(proj, proj, proj, dys, dys, pool_wg, pool_scale, conv_wg, conv_b)


def _adamw(name, g, w, m, v):
    r, c = w.shape
    tr = _pick(r, max(8, (1 << 20) // (4 * c)), 8)

    def body(g_ref, w_ref, m_ref, v_ref, go_ref, d_ref, mo_ref, vo_ref):
        gv = g_ref[...]
        mn = ADAM_B1 * m_ref[...] + (1.0 - ADAM_B1) * gv
        vn = ADAM_B2 * v_ref[...] + (1.0 - ADAM_B2) * (gv * gv)
        m_hat = mn / (1.0 - ADAM_B1 ** ADAM_STEP)
        v_hat = vn / (1.0 - ADAM_B2 ** ADAM_STEP)
        go_ref[...] = gv
        d_ref[...] = -ADAM_LR * (m_hat / (jnp.sqrt(v_hat) + ADAM_EPS) + ADAM_WD * w_ref[...])
        mo_ref[...] = mn
        vo_ref[...] = vn

    blk = pl.BlockSpec((tr, c), lambda i: (i, 0))
    sh = jax.ShapeDtypeStruct((r, c), F32)
    return pl.pallas_call(
        body, name=name, grid=(r // tr,), in_specs=[blk] * 4, out_specs=[blk] * 4,
        out_shape=[sh] * 4, compiler_params=_params(("parallel",), tr * c * 32, 4 * tr * c * 4),
    )(g, w, m, v)


def _pair_add(name, g, r1, c_idx):
    _, r, c = g.shape
    tr = _pick(r, max(16, (2 << 20) // (2 * c)), 16)

    def body(c_ref, g_ref, r_ref, o_ref):
        o_ref[...] = (g_ref[...].astype(F32) + r_ref[...].astype(F32)).astype(BF16)

    return pl.pallas_call(
        body, name=name,
        grid_spec=pltpu.PrefetchScalarGridSpec(
            num_scalar_prefetch=1, grid=(r // tr,),
            in_specs=[pl.BlockSpec((None, tr, c), lambda i, cr: (cr[0], i, 0)),
                      pl.BlockSpec((tr, c), lambda i, cr: (i, 0))],
            out_specs=pl.BlockSpec((tr, c), lambda i, cr: (i, 0))),
        out_shape=jax.ShapeDtypeStruct((r, c), BF16),
        compiler_params=_params(("parallel",), tr * c * 6, 3 * tr * c * 4),
    )(c_idx, g, r1)


def _adamw_halves(name, g_own, g_other, c_idx, w, m, v):
    _, r, c = w.shape
    tr = _pick(r, max(8, (5 << 18) // (4 * c)), 8)

    def body(c_ref, go_ref, gt_ref, w_ref, m_ref, v_ref, g_out, d_ref, mo_ref, vo_ref):
        gv = jnp.where(pl.program_id(0) == c_ref[0], go_ref[...], gt_ref[...])
        mn = ADAM_B1 * m_ref[...] + (1.0 - ADAM_B1) * gv
        vn = ADAM_B2 * v_ref[...] + (1.0 - ADAM_B2) * (gv * gv)
        m_hat = mn / (1.0 - ADAM_B1 ** ADAM_STEP)
        v_hat = vn / (1.0 - ADAM_B2 ** ADAM_STEP)
        g_out[...] = gv
        d_ref[...] = -ADAM_LR * (m_hat / (jnp.sqrt(v_hat) + ADAM_EPS) + ADAM_WD * w_ref[...])
        mo_ref[...] = mn
        vo_ref[...] = vn

    blk = pl.BlockSpec((None, tr, c), lambda h, i, cr: (h, i, 0))
    sh = jax.ShapeDtypeStruct(w.shape, F32)
    return pl.pallas_call(
        body, name=name,
        grid_spec=pltpu.PrefetchScalarGridSpec(
            num_scalar_prefetch=1, grid=(2, r // tr),
            in_specs=[pl.BlockSpec((tr, c), lambda h, i, cr: (jnp.where(h == cr[0], i, 0), 0)),
                      pl.BlockSpec((tr, c), lambda h, i, cr: (jnp.where(h == cr[0], 0, i), 0)),
                      blk, blk, blk],
            out_specs=[blk] * 4),
        out_shape=[sh] * 4,
        compiler_params=_params(("arbitrary", "arbitrary"), tr * c * 36, 4 * tr * c * 4),
    )(c_idx, g_own, g_other, w, m, v)


def _sum_chips(name, part, r2, chip_idx, row0=0, prev=None):
    n, r, c = r2.shape
    rc = part.shape[1]
    tr = _pick(math.gcd(rc, row0) if row0 else rc, max(16, (1 << 20) // (2 * c)), 16)
    b0 = row0 // tr

    def body(ch_ref, own_ref, *rest):
        slots, o_ref = rest[:n], rest[-1]
        acc = None
        for s in range(n):
            term = jnp.where(ch_ref[0] == s, own_ref[...], slots[s][...]).astype(F32)
            acc = term if acc is None else acc + term
        o_ref[...] = acc

    def slot_spec(s):
        return pl.BlockSpec((None, tr, c), lambda i, ch: (jnp.where(ch[0] == s, (s + 1) % n, s), b0 + i, 0))

    extra = [] if prev is None else [prev]
    return pl.pallas_call(
        body, name=name,
        grid_spec=pltpu.PrefetchScalarGridSpec(
            num_scalar_prefetch=1, grid=(rc // tr,),
            in_specs=[pl.BlockSpec((None, tr, c), lambda i, ch: (ch[0], i, 0))]
            + [slot_spec(s) for s in range(n)] + [ANY] * len(extra),
            out_specs=pl.BlockSpec((tr, c), lambda i, ch: (b0 + i, 0))),
        out_shape=jax.ShapeDtypeStruct((r, c), F32),
        input_output_aliases={2 + n: 0} if extra else {},
        compiler_params=_params(("parallel",), tr * c * (2 * n + 6), 3 * tr * c * 4),
    )(chip_idx, part, *([r2] * n), *extra)


def _sum_devices(packs):
    n, r, c = packs.shape

    def body(p_ref, o_ref):
        acc = p_ref[0]
        for k in range(1, n):
            acc = acc + p_ref[k]
        o_ref[...] = acc

    return pl.pallas_call(
        body, name="sum_devices", out_shape=jax.ShapeDtypeStruct((r, c), F32),
        in_specs=[pl.BlockSpec(memory_space=pltpu.VMEM)],
        out_specs=pl.BlockSpec(memory_space=pltpu.VMEM),
    )(packs)


def _place():
    x, y, c = lax.axis_index("x"), lax.axis_index("y"), lax.axis_index("c")
    return x, y, c


def _chip_peers(x, y):
    out = []
    for k, (fx, fy) in enumerate(((0, 1), (1, 0), (1, 1))):
        px = 1 - x if fx else x
        py = 1 - y if fy else y
        out.append((k, px, py, 2 * px + py))
    return out


def _gather_weights(big, small, relay=False, phase=None, prev=None):
    nb, ns = len(big), len(small)

    class Copies:
        def __init__(self, c_in, c_out, sems):
            b_in, s_in = c_in[:nb], c_in[nb:]
            b_out, s_out = c_out[:nb], c_out[nb:]
            ici_s, ici_r, d2d_s, d2d_r, own_s, own_r = sems[:6]
            x, y, c = _place()
            chip = 2 * x + y
            sibling = (x, y, 1 - c)
            peers = _chip_peers(x, y)
            self.own = [pltpu.make_async_remote_copy(
                src_ref=b_in[t], dst_ref=b_out[t].at[chip], send_sem=own_s.at[t], recv_sem=own_r.at[t],
                device_id=sibling, device_id_type=MESH) for t in range(nb)]
            self.ici = [pltpu.make_async_remote_copy(
                src_ref=b_in[t].at[c], dst_ref=b_out[t].at[chip, c],
                send_sem=ici_s.at[t, k], recv_sem=ici_r.at[t, k],
                device_id=(px, py, c), device_id_type=MESH)
                for t in range(nb) for (k, px, py, pchip) in peers if not (relay and k == 2)]
            self.relay = []
            if relay:
                south = c == 0
                from_chip = jnp.where(south, 2 * x + (1 - y), 2 * (1 - x) + y)
                to = (jnp.where(south, 1 - x, x), jnp.where(south, y, 1 - y), c)
                self.relay = [pltpu.make_async_remote_copy(
                    src_ref=b_out[t].at[from_chip, c], dst_ref=b_out[t].at[from_chip, c],
                    send_sem=ici_s.at[t, 2], recv_sem=ici_r.at[t, 2],
                    device_id=to, device_id_type=MESH) for t in range(nb)]
            self.landed = [pltpu.make_async_remote_copy(
                src_ref=b_out[t].at[pchip, c], dst_ref=b_out[t].at[pchip, c],
                send_sem=ici_s.at[t, k], recv_sem=ici_r.at[t, k],
                device_id=sibling, device_id_type=MESH)
                for t in range(nb) for (k, px, py, pchip) in peers]
            self.passed = [pltpu.make_async_remote_copy(
                src_ref=b_out[t].at[pchip, c], dst_ref=b_out[t].at[pchip, c],
                send_sem=d2d_s.at[t, k], recv_sem=d2d_r.at[t, k],
                device_id=sibling, device_id_type=MESH)
                for t in range(nb) for (k, px, py, pchip) in peers]
            self.from_sibling = [pltpu.make_async_remote_copy(
                src_ref=b_out[t].at[pchip, 1 - c], dst_ref=b_out[t].at[pchip, 1 - c],
                send_sem=d2d_s.at[t, k], recv_sem=d2d_r.at[t, k],
                device_id=sibling, device_id_type=MESH)
                for t in range(nb) for (k, px, py, pchip) in peers]
            self.small, self.small_landed, self.local = [], [], []
            if ns:
                sm_s, sm_r, loc = sems[6:]
                self.local = [pltpu.make_async_copy(s_in[t], s_out[t].at[chip], loc.at[t]) for t in range(ns)]
                self.small = [pltpu.make_async_remote_copy(
                    src_ref=s_in[t], dst_ref=s_out[t].at[chip],
                    send_sem=sm_s.at[t, k], recv_sem=sm_r.at[t, k],
                    device_id=(px, py, c), device_id_type=MESH)
                    for t in range(ns) for (k, px, py, pchip) in peers]
                self.small_landed = [pltpu.make_async_remote_copy(
                    src_ref=s_in[t], dst_ref=s_out[t].at[pchip],
                    send_sem=sm_s.at[t, k], recv_sem=sm_r.at[t, k],
                    device_id=sibling, device_id_type=MESH)
                    for t in range(ns) for (k, px, py, pchip) in peers]

    assert phase is None or (relay and not small)
    near = [3 * t + k for t in range(nb) for k in range(2)]
    diag = [3 * t + 2 for t in range(nb)]

    def start(c_in, c_out, sems):
        cps = Copies(c_in, c_out, sems)
        if phase == 2:
            for cp in cps.relay:
                cp.start()
        else:
            for cp in cps.local + cps.own + cps.ici + cps.small:
                cp.start()

    def finish(c_in, c_out, sems):
        cps = Copies(c_in, c_out, sems)
        legs = near + diag if phase is None else (near if phase == 1 else diag)
        for n in sorted(legs):
            if relay and phase is None and n % 3 == 2:
                cps.relay[n // 3].start()
            cps.landed[n].wait_recv()
            cps.passed[n].start()
        for cp in (cps.small_landed if phase != 2 else []) + [cps.from_sibling[n] for n in legs]:
            cp.wait_recv()
        sends = [cps.passed[n] for n in legs]
        if phase != 2:
            sends += cps.ici + cps.small
        if phase != 1:
            sends += cps.relay
        for cp in sends:
            cp.wait_send()
        if phase != 2:
            for cp in cps.own + cps.local:
                cp.wait()

    out_shape = [jax.ShapeDtypeStruct((N_CHIPS,) + b.shape, b.dtype) for b in big]
    out_shape += [jax.ShapeDtypeStruct((N_CHIPS,) + s.shape, s.dtype) for s in small]
    dma = pltpu.SemaphoreType.DMA
    scratch = [dma((nb, 3)), dma((nb, 3)), dma((nb, 3)), dma((nb, 3)), dma((nb,)), dma((nb,))]
    if ns:
        scratch += [dma((ns, 3)), dma((ns, 3)), dma((ns,))]
    operands = list(big) + list(small) + (list(prev) if prev else [])
    ex = _Exchange(operands, out_shape, scratch, start, finish,
                   aliases={nb + t: t for t in range(nb)} if prev else None)
    ex.copies = Copies
    return ex


def _join(exs):
    assert not any(e.aliases for e in exs)

    def parts(refs, counts):
        out, pos = [], 0
        for n in counts:
            out.append(refs[pos:pos + n])
            pos += n
        return out

    def run(which):
        def go(c_in, c_out, sems):
            ins = parts(c_in, [len(e.operands) for e in exs])
            outs = parts(c_out, [len(e.out_shape) for e in exs])
            ss = parts(sems, [len(e.scratch) for e in exs])
            for e, a, b, c in zip(exs, ins, outs, ss):
                getattr(e, which)(a, b, c)
        return go

    return _Exchange(sum((list(e.operands) for e in exs), []), sum((list(e.out_shape) for e in exs), []),
                     sum((list(e.scratch) for e in exs), []), run("start"), run("finish"))


class _Exchange:
    def __init__(self, operands, out_shape, scratch, start, finish, aliases=None):
        self.operands, self.out_shape, self.scratch = operands, out_shape, scratch
        self.start, self.finish, self.aliases = start, finish, dict(aliases or {})


def _run_exchange(name, ex):
    n_i, n_o = len(ex.operands), len(ex.out_shape)

    def body(*refs):
        ins, outs, sems = refs[:n_i], refs[n_i:n_i + n_o], refs[n_i + n_o:]
        ex.start(ins, outs, sems)
        ex.finish(ins, outs, sems)

    return list(pl.pallas_call(
        body, name=name, out_shape=list(ex.out_shape), in_specs=[ANY] * n_i, out_specs=[ANY] * n_o,
        scratch_shapes=list(ex.scratch), input_output_aliases=ex.aliases,
    )(*ex.operands))


def _to_sibling(arrays, other_half):
    n = len(arrays)

    def copies(a_in, a_out, sems):
        send, recv = sems
        x, y, c = _place()
        return [pltpu.make_async_remote_copy(
            src_ref=a_in[t].at[1 - c] if other_half else a_in[t], dst_ref=a_out[t],
            send_sem=send.at[t], recv_sem=recv.at[t],
            device_id=(x, y, 1 - c), device_id_type=MESH) for t in range(n)]

    def start(a_in, a_out, sems):
        for cp in copies(a_in, a_out, sems):
            cp.start()

    def finish(a_in, a_out, sems):
        for cp in copies(a_in, a_out, sems):
            cp.wait()

    dma = pltpu.SemaphoreType.DMA
    shapes = [jax.ShapeDtypeStruct(a.shape[1:] if other_half else a.shape, a.dtype) for a in arrays]
    return _Exchange(list(arrays), shapes, [dma((n,)), dma((n,))], start, finish)


def _scatter_partials(parts, rows=None, row0=0, prev=None):
    n = len(parts)
    land = [(p.shape[0], p.shape[1] if rows is None else rows, p.shape[2]) for p in parts]

    def window(ref, slot, t):
        return ref.at[slot, pl.ds(row0, parts[t].shape[1])]

    def sends(p_in, r_out, sems):
        send, recv = sems
        x, y, c = _place()
        chip = 2 * x + y
        return [pltpu.make_async_remote_copy(
            src_ref=p_in[t].at[pchip], dst_ref=window(r_out[t], chip, t),
            send_sem=send.at[t, k], recv_sem=recv.at[t, k],
            device_id=(px, py, c), device_id_type=MESH)
            for t in range(n) for (k, px, py, pchip) in _chip_peers(x, y)]

    def start(p_in, r_out, sems):
        for cp in sends(p_in, r_out, sems):
            cp.start()

    def finish(p_in, r_out, sems):
        send, recv = sems
        x, y, c = _place()
        for t in range(n):
            for (k, px, py, pchip) in _chip_peers(x, y):
                pltpu.make_async_remote_copy(
                    src_ref=p_in[t].at[pchip], dst_ref=window(r_out[t], pchip, t),
                    send_sem=send.at[t, k], recv_sem=recv.at[t, k],
                    device_id=(px, py, c), device_id_type=MESH).wait_recv()
        for cp in sends(p_in, r_out, sems):
            cp.wait_send()

    dma = pltpu.SemaphoreType.DMA
    operands = list(parts) + (list(prev) if prev else [])
    return _Exchange(operands, [jax.ShapeDtypeStruct(sh, p.dtype) for sh, p in zip(land, parts)],
                     [dma((n, 3)), dma((n, 3))], start, finish,
                     aliases={n + t: t for t in range(n)} if prev else None)


def _w_in_grad_sibling(h, dpa, dgl, c_idx, carried, *, tm, tn, rows, pq):
    s, d = h.shape
    wa, wg = dpa.shape[1], dgl.shape[2]
    p = wa + 2 * wg
    ni, nj = rows // tm, p // tn
    na, qg, qp = wa // tn, wg // tn, pq // tn
    n_steps = ni * nj
    n_ci, n_co = len(carried.operands), len(carried.out_shape)

    def body(c_ref, h_ref, a_ref, g_ref, *rest):
        c_in = rest[:n_ci]
        r1_ref = rest[n_ci]
        c_out = rest[n_ci + 1:n_ci + 1 + n_co]
        slots, send, recv = rest[n_ci + 1 + n_co:n_ci + 4 + n_co]
        sems = rest[n_ci + 4 + n_co:]
        i, j = pl.program_id(0), pl.program_id(1)
        step = i * nj + j
        slot = lax.rem(step, 2)
        x, y, c = _place()
        sibling = (x, y, 1 - c)

        @pl.when(step == 0)
        def _():
            carried.start(c_in, c_out, sems)

        def tile_copy(sl):
            return pltpu.make_async_remote_copy(
                src_ref=slots.at[sl],
                dst_ref=r1_ref.at[j // qp, pl.ds(i * tm, tm), pl.ds((j % qp) * tn, tn)],
                send_sem=send.at[sl], recv_sem=recv, device_id=sibling, device_id_type=MESH)

        @pl.when(step >= 2)
        def _():
            tile_copy(slot).wait_send()

        def emit(b_ref):
            acc = lax.dot_general(h_ref[...], b_ref[...], TN, preferred_element_type=F32)
            slots[slot] = acc.astype(BF16)

        pl.when(j < na)(lambda: emit(a_ref))
        pl.when(j >= na)(lambda: emit(g_ref))
        tile_copy(slot).start()

        @pl.when(step == n_steps - 1)
        def _():
            tile_copy(0).wait_send()
            tile_copy(1).wait_send()
            pltpu.make_async_remote_copy(
                src_ref=r1_ref, dst_ref=r1_ref, send_sem=send.at[0], recv_sem=recv,
                device_id=sibling, device_id_type=MESH).wait_recv()
            carried.finish(c_in, c_out, sems)

    nbh = rows // tm
    dma = pltpu.SemaphoreType.DMA
    blk = s * tm * 2 + 2 * s * tn * 2
    res = pl.pallas_call(
        body, name="w_in_grad_sibling",
        grid_spec=pltpu.PrefetchScalarGridSpec(
            num_scalar_prefetch=1, grid=(ni, nj),
            in_specs=[pl.BlockSpec((s, tm), lambda i, j, cr: (0, (1 - cr[0]) * nbh + i)),
                      pl.BlockSpec((s, tn), lambda i, j, cr: (0, jnp.minimum(j, na - 1))),
                      pl.BlockSpec((None, s, tn),
                                   lambda i, j, cr: (jnp.maximum(j - na, 0) // qg, 0, jnp.maximum(j - na, 0) % qg))]
            + [ANY] * n_ci,
            out_specs=[ANY] * (1 + n_co),
            scratch_shapes=[pltpu.VMEM((2, tm, tn), BF16), dma((2,)), dma] + list(carried.scratch)),
        out_shape=[jax.ShapeDtypeStruct((N_CHIPS, rows, pq), BF16)] + list(carried.out_shape),
        compiler_params=_params(("arbitrary", "arbitrary"), blk, 2 * tm * tn * 2 + 2 * tm * tn * 4),
    )(c_idx, h, dpa, dgl, *carried.operands)
    return res[0], res[1:]


def _proj_gathering(h, wb, chip_idx, gather_rest, *, tm, tn):
    s, d = h.shape
    _, hh, pq = wb.shape
    qp, ni = pq // tn, s // tm
    n_tiles = N_CHIPS * qp
    n_ci, n_co, n_cs = len(gather_rest.operands), len(gather_rest.out_shape), len(gather_rest.scratch)

    def where_of(t):
        near = t - qp
        g = jnp.where(t < qp, 0, jnp.where(t < 3 * qp, 1 + lax.rem(near, 2), 3))
        jj = jnp.where(t < qp, t, jnp.where(t < 3 * qp, near // 2, t - 3 * qp))
        return g, jj

    def body(ch_ref, h_ref, own_ref, *rest):
        c_in = rest[:n_ci]
        o_ref, gathered = rest[n_ci], rest[n_ci + 1]
        c_out = rest[n_ci + 2:n_ci + 2 + n_co]
        slots, fetch_sem, ici_s, ici_r, d2d_s, d2d_r, own_s, own_r = rest[n_ci + 2 + n_co:n_ci + 10 + n_co]
        c_sem = rest[n_ci + 10 + n_co:]
        tile, i = pl.program_id(0), pl.program_id(1)
        x, y, c = _place()
        chip = 2 * x + y
        sibling = (x, y, 1 - c)

        def cols(jj):
            return pl.ds(jj * tn, tn)

        def remote(src, dst, send, recv, to):
            return pltpu.make_async_remote_copy(src_ref=src, dst_ref=dst, send_sem=send, recv_sem=recv,
                                                device_id=to, device_id_type=MESH)

        def own_shard():
            return remote(own_ref, gathered.at[chip], own_s, own_r, sibling)

        def to_neighbour(k, jj):
            to = (x, 1 - y, c) if k == 0 else (1 - x, y, c)
            return remote(own_ref.at[c, :, cols(jj)], gathered.at[chip, c, :, cols(jj)],
                          ici_s.at[k, jj], ici_r.at[k, jj], to)

        def relay(jj):
            south = c == 0
            from_chip = jnp.where(south, 2 * x + (1 - y), 2 * (1 - x) + y)
            to = (jnp.where(south, 1 - x, x), jnp.where(south, y, 1 - y), c)
            chunk = gathered.at[from_chip, c, :, cols(jj)]
            return remote(chunk, chunk, ici_s.at[2, jj], ici_r.at[2, jj], to)

        def chunk_of(k, jj, half):
            return gathered.at[jnp.bitwise_xor(chip, k + 1), half, :, cols(jj)]

        def landed(k, jj):
            return remote(chunk_of(k, jj, c), chunk_of(k, jj, c), ici_s.at[k, jj], ici_r.at[k, jj], sibling)

        def passed(k, jj):
            return remote(chunk_of(k, jj, c), chunk_of(k, jj, c), d2d_s.at[k, jj], d2d_r.at[k, jj], sibling)

        def from_sibling(k, jj):
            return remote(chunk_of(k, jj, 1 - c), chunk_of(k, jj, 1 - c), d2d_s.at[k, jj], d2d_r.at[k, jj], sibling)

        def fetch(src, slot):
            return pltpu.make_async_copy(src, slots.at[slot], fetch_sem.at[slot])

        @pl.when(jnp.logical_and(tile == 0, i == 0))
        def _():
            own_shard().start()
            for jj in range(qp):
                for k in range(2):
                    to_neighbour(k, jj).start()
            fetch(own_ref.at[:, :, cols(0)], 0).start()

        @pl.when(i == 0)
        def _():
            fetch(own_ref.at[:, :, cols(0)], lax.rem(tile, 2)).wait()

        last_row = i == ni - 1
        nxt = tile + 1
        g1, j1 = where_of(nxt)
        pass_row = i == max(ni - 2, 0)

        @pl.when(jnp.logical_and(pass_row, jnp.logical_and(nxt >= qp, nxt < n_tiles)))
        def _():
            landed(g1 - 1, j1).wait_recv()
            passed(g1 - 1, j1).start()

        @pl.when(jnp.logical_and(pass_row, nxt == 3 * qp - 1))
        def _():
            for ch in range(qp):
                relay(ch).start()
            gather_rest.start(c_in, c_out, c_sem)

        @pl.when(jnp.logical_and(last_row, nxt < qp))
        def _():
            fetch(own_ref.at[:, :, cols(nxt)], lax.rem(nxt, 2)).start()

        @pl.when(jnp.logical_and(last_row, jnp.logical_and(nxt >= qp, nxt < n_tiles)))
        def _():
            from_sibling(g1 - 1, j1).wait_recv()
            fetch(gathered.at[jnp.bitwise_xor(chip, g1), :, :, cols(j1)], lax.rem(nxt, 2)).start()

        w_tile = slots[lax.rem(tile, 2)].reshape(d, tn)
        o_ref[...] = jnp.dot(h_ref[...], w_tile, preferred_element_type=F32).astype(BF16)

        @pl.when(jnp.logical_and(tile == n_tiles - 1, last_row))
        def _():
            for jj in range(qp):
                for k in range(2):
                    to_neighbour(k, jj).wait_send()
                relay(jj).wait_send()
                for k in range(3):
                    passed(k, jj).wait_send()
            own_shard().wait()
            gather_rest.finish(c_in, c_out, c_sem)

    assert qp >= 2 and not gather_rest.aliases
    dma = pltpu.SemaphoreType.DMA
    p = N_CHIPS * pq

    def out_cols(t, i, ch):
        g, jj = where_of(t)
        return (i, jnp.bitwise_xor(ch[0], g) * qp + jj)

    res = pl.pallas_call(
        body, name="proj_gathering",
        grid_spec=pltpu.PrefetchScalarGridSpec(
            num_scalar_prefetch=1, grid=(n_tiles, ni),
            in_specs=[pl.BlockSpec((tm, d), lambda t, i, ch: (i, 0)), ANY] + [ANY] * n_ci,
            out_specs=[pl.BlockSpec((tm, tn), out_cols), ANY] + [ANY] * n_co,
            scratch_shapes=[pltpu.VMEM((2, 2, hh, tn), BF16), dma((2,)),
                            dma((3, qp)), dma((3, qp)), dma((3, qp)), dma((3, qp)), dma, dma]
            + list(gather_rest.scratch)),
        out_shape=[jax.ShapeDtypeStruct((s, p), BF16), jax.ShapeDtypeStruct((N_CHIPS,) + wb.shape, wb.dtype)]
        + list(gather_rest.out_shape),
        compiler_params=_params(("arbitrary",) * 2, tm * d * 2 + tm * tn * 2, 3 * d * tn * 2 + tm * tn * 4),
    )(chip_idx, h, wb, *gather_rest.operands)
    return res[0], res[1], res[2:]


def _gather_packs(pack):
    def body(p_ref, o_ref, send, recv, loc):
        x, y, c = _place()
        me = 4 * x + 2 * y + c
        mine = pltpu.make_async_copy(p_ref, o_ref.at[me], loc)
        mine.start()
        flips = [(fx, fy, fc) for fx in (0, 1) for fy in (0, 1) for fc in (0, 1)][1:]
        cps = []
        for k, (fx, fy, fc) in enumerate(flips):
            peer = (1 - x if fx else x, 1 - y if fy else y, 1 - c if fc else c)
            cps.append(pltpu.make_async_remote_copy(
                src_ref=p_ref, dst_ref=o_ref.at[me], send_sem=send.at[k], recv_sem=recv.at[k],
                device_id=peer, device_id_type=MESH))
        for cp in cps:
            cp.start()
        for k, (fx, fy, fc) in enumerate(flips):
            px, py, pc = (1 - x if fx else x, 1 - y if fy else y, 1 - c if fc else c)
            pltpu.make_async_remote_copy(
                src_ref=p_ref, dst_ref=o_ref.at[4 * px + 2 * py + pc],
                send_sem=send.at[k], recv_sem=recv.at[k],
                device_id=(px, py, pc), device_id_type=MESH).wait_recv()
        for cp in cps:
            cp.wait_send()
        mine.wait()

    dma = pltpu.SemaphoreType.DMA
    return pl.pallas_call(
        body, name="gather_packs", out_shape=jax.ShapeDtypeStruct((N_DEV,) + pack.shape, pack.dtype),
        in_specs=[ANY], out_specs=ANY, scratch_shapes=[dma((7,)), dma((7,)), dma],
    )(pack)


def _flat_pack(pieces):
    flat = jnp.concatenate([p.reshape(-1) for p in pieces])
    pad = (-flat.shape[0]) % (8 * LANES)
    flat = jnp.concatenate([flat, jnp.zeros((pad,), F32)])
    return flat.reshape(-1, LANES)


def _unpack(pack, shapes):
    flat = pack.reshape(-1)
    out, off = [], 0
    for sh in shapes:
        n = 1
        for s in sh:
            n *= s
        out.append(flat[off:off + n].reshape(sh))
        off += n
    return out


def kernel(x, norm_w, w_in, pool_w, pool_scale, conv_w, conv_b, gate_b, w_branch, w_out, final_norm_w, loss_target, m_norm_w, m_w_in, m_pool_w, m_pool_scale, m_conv_w, m_conv_b, m_gate_b, m_w_branch, m_w_out, m_final_norm_w, v_norm_w, v_w_in, v_pool_w, v_pool_scale, v_conv_w, v_conv_b, v_gate_b, v_w_branch, v_w_out, v_final_norm_w):
    _, s, d = x.shape
    w = d // 2
    cg = w // N_POOL_GROUPS
    p = 6 * w + 2 * d
    pq = p // N_CHIPS
    dq = d // N_CHIPS
    assert w_in.shape == (1, d, pq) and w_branch.shape == (1, 2, w, dq) and w_out.shape == (1, dq, d)
    assert pool_w.shape == (1, N_POOL_GROUPS, cg // N_CHIPS, cg) and conv_w.shape == (1, CONV_K, cg)

    x2d = x.reshape(s, d)
    tgt = loss_target.reshape(s, d)
    c_idx = lax.axis_index("c").astype(jnp.int32).reshape(1)
    chip = 2 * lax.axis_index("x") + lax.axis_index("y")

    big_w = [w_in.reshape(d, pq), w_out.reshape(dq, d), w_branch.reshape(2 * w, dq),
             pool_w.reshape(cg, cg)]
    names = ["in", "out", "branch", "pool"]
    big_b = [_cast_bf16("cast_" + nm, a) for nm, a in zip(names, big_w)]
    big_b = [b.reshape(2, b.shape[0] // 2, b.shape[1]) for b in big_b]
    chip_idx = chip.astype(jnp.int32).reshape(1)

    h = _rms_fwd(x2d, norm_w)

    tm = _pick(s, 1024, 16)
    tn_p = _pick(math.gcd(pq, 6 * w, d), 1024, LANES)
    qp = pq // tn_p
    tk_d = _pick(d, 4096, LANES)
    proj, wg_in, (wg_pool, wg_cw, wg_gb, wg_br_near) = _proj_gathering(
        h, big_b[0], chip_idx,
        _join([_gather_weights(big_b[3:], [conv_w.reshape(CONV_K, cg), gate_b.reshape(2, dq)]),
               _gather_weights(big_b[2:3], [], relay=True, phase=1)]),
        tm=tm, tn=tn_p)
    wg_in = wg_in.reshape(N_CHIPS, d, pq)
    wg_pool = wg_pool.reshape(N_CHIPS, N_POOL_GROUPS, cg // N_CHIPS, cg)

    ys, (wg_br,) = _mix_fwd(proj, wg_pool, pool_scale, wg_cw, conv_b, w,
                            _gather_weights(big_b[2:3], [], relay=True, phase=2, prev=[wg_br_near]))

    tn_d = _pick(dq, 1024, LANES)
    qd = dq // tn_d
    tk_w = _pick(w, 2048, LANES)
    tm_g = _pick(s, 512, 16)
    gl0 = (6 * w) // tn_d
    gl1 = (6 * w + d) // tn_d

    def gate_specs(im, rows):
        return [pl.BlockSpec((rows, tn_d), lambda *a: (im(*a)[0], gl0 + im(*a)[1])),
                pl.BlockSpec((rows, tn_d), lambda *a: (im(*a)[0], gl1 + im(*a)[1])),
                pl.BlockSpec((None, 2, tn_d), lambda *a: (im(*a)[1] // qd, 0, im(*a)[1] % qd))]

    def merge_epilogue(accs, ins, outs, ids):
        gb = ins[6][...]
        g0 = _sigmoid(ins[4][...].astype(F32) + gb[0:1])
        g1 = _sigmoid(ins[5][...].astype(F32) + gb[1:2])
        outs[0][...] = (g0 * accs[0] + g1 * accs[1]).astype(BF16)
        outs[1][0] = accs[0].astype(BF16)
        outs[1][1] = accs[1].astype(BF16)

    (merged, br), (wg_out,) = _mm(
        "branch_merge", grid=(s // tm_g, d // tn_d, w // tk_w), comm=_gather_weights(big_b[1:2], []),
        operands=[ys, wg_br, ys, wg_br, proj, proj, wg_gb],
        in_specs=[pl.BlockSpec((None, tm_g, tk_w), lambda i, j, k: (0, i, k)),
                  pl.BlockSpec((None, None, tk_w, tn_d), lambda i, j, k: (j // qd, 0, k, j % qd)),
                  pl.BlockSpec((None, tm_g, tk_w), lambda i, j, k: (1, i, k)),
                  pl.BlockSpec((None, None, tk_w, tn_d), lambda i, j, k: (j // qd, 1, k, j % qd)),
                  *gate_specs(lambda i, j, k: (i, j), tm_g)],
        out_shape=[jax.ShapeDtypeStruct((s, d), BF16), jax.ShapeDtypeStruct((2, s, d), BF16)],
        out_specs=[pl.BlockSpec((tm_g, tn_d), lambda i, j, k: (i, j)),
                   pl.BlockSpec((2, tm_g, tn_d), lambda i, j, k: (0, i, j))],
        pairs=[(0, 1, 0, None), (2, 3, 1, None)], dims=NN, acc_shapes=[(tm_g, tn_d)] * 2,
        epilogue=merge_epilogue, temp_bytes=6 * tm_g * tn_d * 4,
    )
    wg_out = wg_out.reshape(d, d)

    tn_f = _pick(d, 1024, LANES)
    o = _mm(
        "out_proj", grid=(s // tm, d // tn_f, d // tk_d), operands=[merged, wg_out],
        in_specs=[pl.BlockSpec((tm, tk_d), lambda i, j, k: (i, k)),
                  pl.BlockSpec((tk_d, tn_f), lambda i, j, k: (k, j))],
        out_shape=[jax.ShapeDtypeStruct((s, d), F32)],
        out_specs=[pl.BlockSpec((tm, tn_f), lambda i, j, k: (i, j))],
        pairs=[(0, 1, 0, None)], dims=NN, acc_shapes=[(tm, tn_f)],
        epilogue=lambda accs, ins, outs, ids: outs[0].__setitem__(Ellipsis, accs[0]),
    )[0][0]

    dx2, dx2b, loss_part, g_fnw =_head(x2d, o, tgt, final_norm_w.reshape(1, d))

    def gate_bwd_epilogue(accs, ins, outs, ids):
        dm = accs[0]
        gb = ins[5][...]
        i = ids[1]

        @pl.when(i == 0)
        def _():
            outs[2][...] = jnp.zeros(outs[2].shape, F32)

        for n in range(2):
            gate = _sigmoid(ins[3 + n][...].astype(F32) + gb[n:n + 1])
            outs[0][n] = (dm * gate).astype(BF16)
            dgl = dm * ins[2][n].astype(F32) * gate * (1.0 - gate)
            outs[1][n] = dgl.astype(BF16)
            outs[2][n:n + 1, :] += jnp.sum(dgl, axis=0, keepdims=True)

    tm_b = _pick(s, 512, 16)
    (d_br, dgl, g_gb), _ = _mm(
        "out_proj_bwd_gate", grid=(d // tn_d, s // tm_b, d // tk_d),
        operands=[dx2b, wg_out, br, proj, proj, wg_gb],
        in_specs=[pl.BlockSpec((tm_b, tk_d), lambda j, i, k: (i, k)),
                  pl.BlockSpec((tn_d, tk_d), lambda j, i, k: (j, k)),
                  pl.BlockSpec((2, tm_b, tn_d), lambda j, i, k: (0, i, j)),
                  *gate_specs(lambda j, i, k: (i, j), tm_b)],
        out_shape=[jax.ShapeDtypeStruct((2, s, d), BF16), jax.ShapeDtypeStruct((2, s, d), BF16),
                   jax.ShapeDtypeStruct((2, d), F32)],
        out_specs=[pl.BlockSpec((2, tm_b, tn_d), lambda j, i, k: (0, i, j)),
                   pl.BlockSpec((2, tm_b, tn_d), lambda j, i, k: (0, i, j)),
                   pl.BlockSpec((2, tn_d), lambda j, i, k: (0, j))],
        pairs=[(0, 1, 0, None)], dims=NT, acc_shapes=[(tm_b, tn_d)],
        epilogue=gate_bwd_epilogue, semantics=("parallel", "arbitrary", "arbitrary"),
        temp_bytes=8 * tm_b * tn_d * 4,
    )

    hh_out = d // 8
    tm_o = _pick(hh_out, 512, LANES)
    nb_o = hh_out // tm_o
    tk_s = _pick(s, 4096, LANES)
    g_out = _mm(
        "w_out_grad", grid=(d // tm_o, d // tn_f, s // tk_s), operands=[merged, dx2b],
        in_specs=[pl.BlockSpec((tk_s, tm_o), lambda i, j, k: (k, i)),
                  pl.BlockSpec((tk_s, tn_f), lambda i, j, k: (k, j))],
        out_shape=[jax.ShapeDtypeStruct((2, N_CHIPS, hh_out, d), BF16)],
        out_specs=[pl.BlockSpec((None, None, tm_o, tn_f),
                                lambda i, j, k: ((i // nb_o) % 2, i // (2 * nb_o), i % nb_o, j))],
        pairs=[(0, 1, 0, None)], dims=TN, acc_shapes=[(tm_o, tn_f)],
        epilogue=lambda accs, ins, outs, ids: outs[0].__setitem__(Ellipsis, accs[0].astype(BF16)),
    )[0][0]

    tn_w = _pick(w, 2048, LANES)
    tk_q = _pick(dq, 1024, LANES)
    qk = dq // tk_q
    (dys,), from_sibling_out = _mm(
        "branch_bwd", grid=(2, s // tm, w // tn_w, d // tk_q), operands=[d_br, wg_br],
        comm=_to_sibling([g_out], other_half=True),
        in_specs=[pl.BlockSpec((None, tm, tk_q), lambda n, i, j, k: (n, i, k)),
                  pl.BlockSpec((None, None, tn_w, tk_q), lambda n, i, j, k: (k // qk, n, j, k % qk))],
        out_shape=[jax.ShapeDtypeStruct((2, s, w), F32)],
        out_specs=[pl.BlockSpec((None, tm, tn_w), lambda n, i, j, k: (n, i, j))],
        pairs=[(0, 1, 0, None)], dims=NT, acc_shapes=[(tm, tn_w)],
        epilogue=lambda accs, ins, outs, ids: outs[0].__setitem__(Ellipsis, accs[0]),
    )

    tm_w = _pick(w, 1024, LANES)
    g_br = _mm(
        "w_branch_grad", grid=(2, w // tm_w, d // tn_d, s // tk_s), operands=[ys, d_br],
        in_specs=[pl.BlockSpec((None, tk_s, tm_w), lambda n, i, j, k: (n, k, i)),
                  pl.BlockSpec((None, tk_s, tn_d), lambda n, i, j, k: (n, k, j))],
        out_shape=[jax.ShapeDtypeStruct((2, N_CHIPS, w, dq), BF16)],
        out_specs=[pl.BlockSpec((None, None, tm_w, tn_d), lambda n, i, j, k: (n, j // qd, i, j % qd))],
        pairs=[(0, 1, 0, None)], dims=TN, acc_shapes=[(tm_w, tn_d)],
        epilogue=lambda accs, ins, outs, ids: outs[0].__setitem__(Ellipsis, accs[0].astype(BF16)),
    )[0][0]

    dpa, g_pool, g_ps, g_cw, g_cb = _mix_bwd(proj, dys, wg_pool, pool_scale, wg_cw, conv_b, w)
    g_pool = g_pool.astype(BF16).reshape(2, N_CHIPS, 2 * (cg // N_CHIPS), cg)

    early = [g_out, g_br, g_pool]
    from_sibling_early = list(from_sibling_out) + _run_exchange(
        "pair_exchange", _to_sibling(early[1:], other_half=True))
    parts_early = []
    for nm, g, r1 in zip(names[1:], early, from_sibling_early):
        _, nc, hh, wd = g.shape
        pt = _pair_add("pair_add_" + nm, g.reshape(2, nc * hh, wd), r1.reshape(nc * hh, wd), c_idx)
        parts_early.append(pt.reshape(nc, hh, wd))

    na = (6 * w) // tn_p
    qg = d // tn_p
    hh_in = d // 2
    assert s == tk_s
    tm_i = _pick(hh_in, 1024, LANES)
    tn_i = _pick(tn_p, 512, LANES)
    from_sibling, recv_early = _w_in_grad_sibling(
        h, dpa, dgl, c_idx, _scatter_partials(parts_early), tm=tm_i, tn=tn_i, rows=hh_in, pq=pq)

    tm_o2 = _pick(hh_in, 512, LANES)
    nb_o2 = hh_in // tm_o2

    def own_rows(name, first, count, comm):
        def add_sibling(accs, ins, outs, ids):
            outs[0][...] = (accs[0] + ins[3][...].astype(F32)).astype(BF16)

        return _mm(
            name, grid=(p // tn_p, count, 1), operands=[h, dpa, dgl, from_sibling], comm=comm, prefetch=c_idx,
            in_specs=[pl.BlockSpec((tk_s, tm_o2), lambda j, i, k, cr: (0, cr[0] * nb_o2 + first + i)),
                      pl.BlockSpec((tk_s, tn_p), lambda j, i, k, cr: (0, jnp.minimum(j, na - 1))),
                      pl.BlockSpec((None, tk_s, tn_p),
                                   lambda j, i, k, cr: (jnp.maximum(j - na, 0) // qg, 0, jnp.maximum(j - na, 0) % qg)),
                      pl.BlockSpec((None, tm_o2, tn_p), lambda j, i, k, cr: (j // qp, first + i, j % qp))],
            out_shape=[jax.ShapeDtypeStruct((N_CHIPS, count * tm_o2, pq), BF16)],
            out_specs=[pl.BlockSpec((None, tm_o2, tn_p), lambda j, i, k, cr: (j // qp, i, j % qp))],
            pairs=[(0, 1, 0, lambda ids: ids[0] < na), (0, 2, 0, lambda ids: ids[0] >= na)],
            dims=TN, acc_shapes=[(tm_o2, tn_p)], epilogue=add_sibling)

    halves_early = [_sum_chips("sum_chips_" + nm, pt, r2, chip_idx)
                    for nm, pt, r2 in zip(names[1:], parts_early, recv_early)]
    share_early = _to_sibling(halves_early, other_half=False)

    n_first = 1 if nb_o2 > 1 else 0
    if n_first:
        (part_a,), others_early = own_rows("w_in_grad_own_a", 0, n_first, share_early)
        scatter_a = _scatter_partials([part_a], rows=hh_in)
        (part_b,), land_a = own_rows("w_in_grad_own_b", n_first, nb_o2 - n_first, scatter_a)
        scatter_b = _scatter_partials([part_b], rows=hh_in, row0=n_first * tm_o2, prev=land_a)
    else:
        (part_b,), others_early = own_rows("w_in_grad_own", 0, nb_o2, share_early)
        scatter_b = _scatter_partials([part_b])

    tn_h = _pick(d, 2048, LANES)
    (dh,), recv_in = _mm(
        "proj_bwd", grid=(s // tm, d // tn_h, p // tn_p), operands=[dpa, dgl, wg_in],
        comm=scatter_b,
        in_specs=[pl.BlockSpec((tm, tn_p), lambda i, j, k: (i, jnp.minimum(k, na - 1))),
                  pl.BlockSpec((None, tm, tn_p),
                               lambda i, j, k: (jnp.maximum(k - na, 0) // qg, i, jnp.maximum(k - na, 0) % qg)),
                  pl.BlockSpec((None, tn_h, tn_p), lambda i, j, k: (k // qp, j, k % qp))],
        out_shape=[jax.ShapeDtypeStruct((s, d), F32)],
        out_specs=[pl.BlockSpec((tm, tn_h), lambda i, j, k: (i, j))],
        pairs=[(0, 2, 0, lambda ids: ids[2] < na), (1, 2, 0, lambda ids: ids[2] >= na)],
        dims=NT, acc_shapes=[(tm, tn_h)],
        epilogue=lambda accs, ins, outs, ids: outs[0].__setitem__(Ellipsis, accs[0]),
    )

    if n_first:
        half_in = _sum_chips("sum_chips_in_a", part_a, recv_in[0], chip_idx)
        half_in = _sum_chips("sum_chips_in_b", part_b, recv_in[0], chip_idx, row0=n_first * tm_o2, prev=half_in)
    else:
        half_in = _sum_chips("sum_chips_in", part_b, recv_in[0], chip_idx)
    halves = [half_in] + halves_early
    grad_x, g_nw, other_in = _rms_bwd(x2d, dh, dx2, norm_w, _to_sibling([half_in], other_half=False))
    others = list(other_in) + list(others_early)

    g_cw_full = jnp.transpose(g_cw, (1, 0, 2)).reshape(CONV_K, w)
    small_shapes = [(LANES,), (1, d), (1, d), (1, w), (1, w), (CONV_K, w), (2, d)]
    pack = _flat_pack([loss_part[0], g_nw, g_fnw, g_ps, g_cb, g_cw_full, g_gb])
    total = _sum_devices(_gather_packs(pack))
    t_loss, t_nw, t_fnw, t_ps, t_cb, t_cw, t_gb = _unpack(total, small_shapes)
    loss = t_loss[0]
    t_cw = lax.dynamic_slice_in_dim(t_cw, chip * cg, cg, axis=1)
    t_gb = lax.dynamic_slice_in_dim(t_gb, chip * dq, dq, axis=1)

    out = {}
    big_names = ["w_in", "w_out", "w_branch", "pool_w"]
    big_m = [m_w_in, m_w_out, m_w_branch, m_pool_w]
    big_v = [v_w_in, v_w_out, v_w_branch, v_pool_w]
    big_orig = [w_in, w_out, w_branch, pool_w]
    for nm, g_own, g_other, w2, mm_, vv_, orig in zip(big_names, halves, others, big_w, big_m, big_v, big_orig):
        sh = (2,) + g_own.shape
        res = _adamw_halves("adamw_" + nm, g_own, g_other, c_idx,
                            w2.reshape(sh), mm_.reshape(sh), vv_.reshape(sh))
        out[nm] = [r.reshape(orig.shape) for r in res]

    sm_names = ["norm_w", "final_norm_w", "pool_scale", "conv_b", "conv_w", "gate_b"]
    sm_g = [t_nw, t_fnw, t_ps, t_cb, t_cw, t_gb]
    sm_w = [norm_w, final_norm_w, pool_scale, conv_b, conv_w, gate_b]
    sm_m = [m_norm_w, m_final_norm_w, m_pool_scale, m_conv_b, m_conv_w, m_gate_b]
    sm_v = [v_norm_w, v_final_norm_w, v_pool_scale, v_conv_b, v_conv_w, v_gate_b]
    sm_shapes = [a.shape for a in sm_w]
    res = _adamw("adamw_small", _flat_pack(sm_g), _flat_pack(sm_w), _flat_pack(sm_m), _flat_pack(sm_v))
    res = [_unpack(r, sm_shapes) for r in res]
    for idx, nm in enumerate(sm_names):
        out[nm] = [r[idx] for r in res]

    order = ["norm_w", "w_in", "pool_w", "pool_scale", "conv_w", "conv_b", "gate_b", "w_branch", "w_out",
             "final_norm_w"]
    outs = [loss, grad_x.reshape(x.shape)]
    for kind in range(4):
        outs += [out[nm][kind] for nm in order]
    return tuple(outs)
```
